```python
import math
import jax, jax.numpy as jnp
from jax import lax
import numpy as np

D_MODEL = 2048
BATCH = 8
SEQ = 8192
DEPTH = 2

CHUNK = 64
Q_BLOCK = 128
HEAD_DIM = 128
N_BRANCH = 4
BRANCH_WIDTH = D_MODEL // N_BRANCH
N_HEADS = BRANCH_WIDTH // HEAD_DIM
LRU_WIDTH = BRANCH_WIDTH
LRU_BLOCKS = N_HEADS
LRU_BLOCK = LRU_WIDTH // LRU_BLOCKS
CONV_WIDTH = 4
LRU_C = 8.0
LOOKBACK_CHUNKS = 8
BAND_CHUNKS = LOOKBACK_CHUNKS + 1
REL_CLIP = 256
REL_TABLE = REL_CLIP + CHUNK
D_FF = 4 * D_MODEL
ALPHA = (2.0 * DEPTH) ** 0.25
BETA = (8.0 * DEPTH) ** -0.25
LN_EPS = 1e-5

IN_SIZES = (
    BRANCH_WIDTH, BRANCH_WIDTH, BRANCH_WIDTH, N_HEADS,
    LRU_WIDTH, LRU_WIDTH,
    BRANCH_WIDTH, BRANCH_WIDTH, BRANCH_WIDTH,
    BRANCH_WIDTH, BRANCH_WIDTH, BRANCH_WIDTH,
)
D_IN = sum(IN_SIZES)

kernel_name = "chunk_causal_hybrid_fox_rglru_stickbreak_chunkattn"

F32 = jnp.float32


def layer_norm(x, g, b):
    xf = x.astype(F32)
    mu = jnp.mean(xf, axis=-1, keepdims=True)
    var = jnp.mean(jnp.square(xf - mu), axis=-1, keepdims=True)
    y = (xf - mu) * lax.rsqrt(var + LN_EPS) * g.astype(F32) + b.astype(F32)
    return y.astype(x.dtype)


def split_cols(u, sizes):
    outs, off = [], 0
    for n in sizes:
        outs.append(u[..., off:off + n])
        off += n
    return outs


def heads(t):
    b, s, _ = t.shape
    return t.reshape(b, s, N_HEADS, HEAD_DIM)


def fox_attention(q, k, v, f_logit):
    B, S, H, Dh = q.shape
    nb = S // Q_BLOCK
    cum_f = jnp.cumsum(jax.nn.log_sigmoid(f_logit.astype(F32)), axis=1)
    cum_f_k = cum_f.transpose(0, 2, 1)
    qb = q.reshape(B, nb, Q_BLOCK, H, Dh).transpose(1, 0, 2, 3, 4)
    fb = cum_f.reshape(B, nb, Q_BLOCK, H).transpose(1, 0, 3, 2)
    kpos = jnp.arange(S)
    scale = Dh ** -0.5

    def block(args):
        i, qi, fi = args
        s = jnp.einsum('bqhd,bkhd->bhqk', qi, k).astype(F32) * scale
        s = s + fi[..., :, None] - cum_f_k[:, :, None, :]
        qpos = i * Q_BLOCK + jnp.arange(Q_BLOCK)
        s = jnp.where(kpos[None, :] <= qpos[:, None], s, -jnp.inf)
        p = jax.nn.softmax(s, axis=-1)
        return jnp.einsum('bhqk,bkhd->bqhd', p.astype(v.dtype), v)

    out = lax.map(block, (jnp.arange(nb), qb, fb))
    return out.transpose(1, 0, 2, 3, 4).reshape(B, S, H * Dh)


def stick_breaking_attention(q, k, v):
    B, S, H, Dh = q.shape
    nb = S // Q_BLOCK
    qb = q.reshape(B, nb, Q_BLOCK, H, Dh).transpose(1, 0, 2, 3, 4)
    kpos = jnp.arange(S)
    scale = Dh ** -0.5

    def block(args):
        i, qi = args
        z = jnp.einsum('bqhd,bkhd->bhqk', qi, k).astype(F32) * scale
        qpos = i * Q_BLOCK + jnp.arange(Q_BLOCK)
        mask = kpos[None, :] < qpos[:, None]
        log_1m_beta = jnp.where(mask, jax.nn.log_sigmoid(-z), 0.0)
        later = lax.cumsum(log_1m_beta, axis=3, reverse=True) - log_1m_beta
        a = jnp.where(mask, jnp.exp(jax.nn.log_sigmoid(z) + later), 0.0)
        return jnp.einsum('bhqk,bkhd->bqhd', a.astype(v.dtype), v)

    out = lax.map(block, (jnp.arange(nb), qb))
    return out.transpose(1, 0, 2, 3, 4).reshape(B, S, H * Dh)


def chunk_band_attention(q, k, v, rel_bias):
    B, S, H, Dh = q.shape
    nc = S // CHUNK
    qc = q.reshape(B, nc, CHUNK, H, Dh)
    pad = ((0, 0), (LOOKBACK_CHUNKS, 0), (0, 0), (0, 0), (0, 0))
    kc = jnp.pad(k.reshape(B, nc, CHUNK, H, Dh), pad)
    vc = jnp.pad(v.reshape(B, nc, CHUNK, H, Dh), pad)
    kband = jnp.concatenate([kc[:, j:j + nc] for j in range(BAND_CHUNKS)], axis=2)
    vband = jnp.concatenate([vc[:, j:j + nc] for j in range(BAND_CHUNKS)], axis=2)
    kidx = jnp.arange(BAND_CHUNKS * CHUNK)
    dist = LOOKBACK_CHUNKS * CHUNK + jnp.arange(CHUNK)[:, None] - kidx[None, :]
    ridx = jnp.clip(dist, -(CHUNK - 1), REL_CLIP) + (CHUNK - 1)
    bias = rel_bias.astype(F32)[:, ridx]
    chunk_of_slot = jnp.arange(nc)[:, None] - LOOKBACK_CHUNKS + jnp.arange(BAND_CHUNKS)[None, :]
    valid = jnp.repeat(chunk_of_slot >= 0, CHUNK, axis=1)
    s = jnp.einsum('bcqhd,bckhd->bchqk', qc, kband).astype(F32) * (Dh ** -0.5)
    s = s + bias[None, None]
    s = jnp.where(valid[None, :, None, None, :], s, -jnp.inf)
    p = jax.nn.softmax(s, axis=-1)
    out = jnp.einsum('bchqk,bckhd->bcqhd', p.astype(v.dtype), vband)
    return out.reshape(B, S, H * Dh)


def recurrent_branch(xr, yr, conv_w, conv_b, w_r, b_r, w_i, b_i, lam):
    B, S, W = xr.shape
    xp = jnp.pad(xr, ((0, 0), (CONV_WIDTH - 1, 0), (0, 0)))
    xc = conv_b
    for j in range(CONV_WIDTH):
        xc = xc + xp[:, j:j + S] * conv_w[j]
    xg = xc.reshape(B, S, LRU_BLOCKS, LRU_BLOCK)
    r = jax.nn.sigmoid(jnp.einsum('bsnc,ncd->bsnd', xg, w_r).reshape(B, S, W) + b_r)
    gi = jax.nn.sigmoid(jnp.einsum('bsnc,ncd->bsnd', xg, w_i).reshape(B, S, W) + b_i)
    log_a = LRU_C * r.astype(F32) * jax.nn.log_sigmoid(lam.astype(F32))
    a = jnp.exp(log_a)
    inp = jnp.sqrt(-jnp.expm1(2.0 * log_a)) * (gi * xc).astype(F32)

    def combine(left, right):
        a1, b1 = left
        a2, b2 = right
        return a1 * a2, a2 * b1 + b2

    _, h = lax.associative_scan(combine, (a, inp), axis=1)
    return h.astype(xr.dtype) * jax.nn.gelu(yr)


def _fwd_setup_inputs(seed: int = 0) -> dict:
    key = jax.random.key(seed)
    ks = jax.random.split(key, 24)
    L, D, Wb = DEPTH, D_MODEL, BRANCH_WIDTH

    def nrm(k, shape, scale):
        return jax.random.normal(k, shape, F32) * scale

    u = jax.random.uniform(ks[9], (L, LRU_WIDTH), F32, 0.9, 0.999)
    a0 = u ** (1.0 / LRU_C)
    lru_lambda = jnp.log(a0) - jnp.log1p(-a0)
    return {
        "x": nrm(ks[0], (BATCH, SEQ, D), 1.0),
        "ln_in_g": 1.0 + nrm(ks[1], (D,), 0.02),
        "ln_in_b": nrm(ks[2], (D,), 0.02),
        "w_in": nrm(ks[3], (L, D, D_IN), D ** -0.5),
        "b_forget": 3.0 + nrm(ks[4], (L, N_HEADS), 0.5),
        "conv_w": nrm(ks[5], (L, CONV_WIDTH, LRU_WIDTH), CONV_WIDTH ** -0.5),
        "conv_b": nrm(ks[6], (L, LRU_WIDTH), 0.02),
        "w_r": nrm(ks[7], (L, LRU_BLOCKS, LRU_BLOCK, LRU_BLOCK), LRU_BLOCK ** -0.5),
        "b_r": nrm(ks[8], (L, LRU_WIDTH), 0.02),
        "w_i": nrm(ks[10], (L, LRU_BLOCKS, LRU_BLOCK, LRU_BLOCK), LRU_BLOCK ** -0.5),
        "b_i": nrm(ks[11], (L, LRU_WIDTH), 0.02),
        "lru_lambda": lru_lambda,
        "rel_bias": nrm(ks[12], (L, N_HEADS, REL_TABLE), 0.1),
        "w_branch": nrm(ks[13], (L, N_BRANCH, Wb, D), Wb ** -0.5),
        "w_gate": nrm(ks[14], (L, N_BRANCH, D, D), D ** -0.5),
        "b_gate": nrm(ks[15], (L, N_BRANCH, D), 0.02),
        "w_out": nrm(ks[16], (L, D, D), BETA * D ** -0.5),
        "ln1_g": 1.0 + nrm(ks[17], (L, D), 0.02),
        "ln1_b": nrm(ks[18], (L, D), 0.02),
        "w_ff1": nrm(ks[19], (L, D, D_FF), D ** -0.5),
        "w_ff2": nrm(ks[20], (L, D_FF, D), BETA * D_FF ** -0.5),
        "ln2_g": 1.0 + nrm(ks[21], (L, D), 0.02),
        "ln2_b": nrm(ks[22], (L, D), 0.02),
    }


def _fwd_reference(x, ln_in_g, ln_in_b, w_in, b_forget, conv_w, conv_b, w_r, b_r, w_i, b_i,
              lru_lambda, rel_bias, w_branch, w_gate, b_gate, w_out, ln1_g, ln1_b,
              w_ff1, w_ff2, ln2_g, ln2_b):
    x = layer_norm(x, ln_in_g, ln_in_b)
    for l in range(DEPTH):
        u = x @ w_in[l]
        (fq, fk, fv, ff, rx, ry, sq, sk, sv, cq, ck, cv) = split_cols(u, IN_SIZES)
        o_fox = fox_attention(heads(fq), heads(fk), heads(fv), ff + b_forget[l])
        o_lru = recurrent_branch(rx, ry, conv_w[l], conv_b[l], w_r[l], b_r[l],
                                 w_i[l], b_i[l], lru_lambda[l])
        o_sb = stick_breaking_attention(heads(sq), heads(sk), heads(sv))
        o_ch = chunk_band_attention(heads(cq), heads(ck), heads(cv), rel_bias[l])
        merged = None
        for g, o in enumerate((o_fox, o_lru, o_sb, o_ch)):
            gate = jax.nn.sigmoid(x @ w_gate[l, g] + b_gate[l, g])
            term = gate * (o @ w_branch[l, g])
            merged = term if merged is None else merged + term
        x = layer_norm(ALPHA * x + merged @ w_out[l], ln1_g[l], ln1_b[l])
        hid = jnp.square(jax.nn.relu(x @ w_ff1[l]))
        x = layer_norm(ALPHA * x + hid @ w_ff2[l], ln2_g[l], ln2_b[l])
    return x


import jax as _jax
import jax.numpy as _jnp

TWIN_FORMAT = 'train_step'
FWD_PARAMS = ['x', 'ln_in_g', 'ln_in_b', 'w_in', 'b_forget', 'conv_w', 'conv_b', 'w_r', 'b_r', 'w_i', 'b_i', 'lru_lambda', 'rel_bias', 'w_branch', 'w_gate', 'b_gate', 'w_out', 'ln1_g', 'ln1_b', 'w_ff1', 'w_ff2', 'ln2_g', 'ln2_b']
TWIN_WEIGHTS = ['ln_in_g', 'ln_in_b', 'w_in', 'b_forget', 'conv_w', 'conv_b', 'w_r', 'b_r', 'w_i', 'b_i', 'lru_lambda', 'rel_bias', 'w_branch', 'w_gate', 'b_gate', 'w_out', 'ln1_g', 'ln1_b', 'w_ff1', 'w_ff2', 'ln2_g', 'ln2_b']
TWIN_DIFF_INPUT = 'x'
TWIN_INPUTS = ['x', 'ln_in_g', 'ln_in_b', 'w_in', 'b_forget', 'conv_w', 'conv_b', 'w_r', 'b_r', 'w_i', 'b_i', 'lru_lambda', 'rel_bias', 'w_branch', 'w_gate', 'b_gate', 'w_out', 'ln1_g', 'ln1_b', 'w_ff1', 'w_ff2', 'ln2_g', 'ln2_b', 'loss_target', 'm_ln_in_g', 'm_ln_in_b', 'm_w_in', 'm_b_forget', 'm_conv_w', 'm_conv_b', 'm_w_r', 'm_b_r', 'm_w_i', 'm_b_i', 'm_lru_lambda', 'm_rel_bias', 'm_w_branch', 'm_w_gate', 'm_b_gate', 'm_w_out', 'm_ln1_g', 'm_ln1_b', 'm_w_ff1', 'm_w_ff2', 'm_ln2_g', 'm_ln2_b', 'v_ln_in_g', 'v_ln_in_b', 'v_w_in', 'v_b_forget', 'v_conv_w', 'v_conv_b', 'v_w_r', 'v_b_r', 'v_w_i', 'v_b_i', 'v_lru_lambda', 'v_rel_bias', 'v_w_branch', 'v_w_gate', 'v_b_gate', 'v_w_out', 'v_ln1_g', 'v_ln1_b', 'v_w_ff1', 'v_w_ff2', 'v_ln2_g', 'v_ln2_b']
TWIN_OUTPUTS = ['loss', 'grad_x', 'grad_ln_in_g', 'grad_ln_in_b', 'grad_w_in', 'grad_b_forget', 'grad_conv_w', 'grad_conv_b', 'grad_w_r', 'grad_b_r', 'grad_w_i', 'grad_b_i', 'grad_lru_lambda', 'grad_rel_bias', 'grad_w_branch', 'grad_w_gate', 'grad_b_gate', 'grad_w_out', 'grad_ln1_g', 'grad_ln1_b', 'grad_w_ff1', 'grad_w_ff2', 'grad_ln2_g', 'grad_ln2_b', 'delta_ln_in_g', 'delta_ln_in_b', 'delta_w_in', 'delta_b_forget', 'delta_conv_w', 'delta_conv_b', 'delta_w_r', 'delta_b_r', 'delta_w_i', 'delta_b_i', 'delta_lru_lambda', 'delta_rel_bias', 'delta_w_branch', 'delta_w_gate', 'delta_b_gate', 'delta_w_out', 'delta_ln1_g', 'delta_ln1_b', 'delta_w_ff1', 'delta_w_ff2', 'delta_ln2_g', 'delta_ln2_b', 'new_m_ln_in_g', 'new_m_ln_in_b', 'new_m_w_in', 'new_m_b_forget', 'new_m_conv_w', 'new_m_conv_b', 'new_m_w_r', 'new_m_b_r', 'new_m_w_i', 'new_m_b_i', 'new_m_lru_lambda', 'new_m_rel_bias', 'new_m_w_branch', 'new_m_w_gate', 'new_m_b_gate', 'new_m_w_out', 'new_m_ln1_g', 'new_m_ln1_b', 'new_m_w_ff1', 'new_m_w_ff2', 'new_m_ln2_g', 'new_m_ln2_b', 'new_v_ln_in_g', 'new_v_ln_in_b', 'new_v_w_in', 'new_v_b_forget', 'new_v_conv_w', 'new_v_conv_b', 'new_v_w_r', 'new_v_b_r', 'new_v_w_i', 'new_v_b_i', 'new_v_lru_lambda', 'new_v_rel_bias', 'new_v_w_branch', 'new_v_w_gate', 'new_v_b_gate', 'new_v_w_out', 'new_v_ln1_g', 'new_v_ln1_b', 'new_v_w_ff1', 'new_v_w_ff2', 'new_v_ln2_g', 'new_v_ln2_b']
TWIN_LEAF_KINDS = {'loss': 'loss', 'grad_x': 'grad_x', 'grad_ln_in_g': 'grad_w', 'grad_ln_in_b': 'grad_w', 'grad_w_in': 'grad_w', 'grad_b_forget': 'grad_w', 'grad_conv_w': 'grad_w', 'grad_conv_b': 'grad_w', 'grad_w_r': 'grad_w', 'grad_b_r': 'grad_w', 'grad_w_i': 'grad_w', 'grad_b_i': 'grad_w', 'grad_lru_lambda': 'grad_w', 'grad_rel_bias': 'grad_w', 'grad_w_branch': 'grad_w', 'grad_w_gate': 'grad_w', 'grad_b_gate': 'grad_w', 'grad_w_out': 'grad_w', 'grad_ln1_g': 'grad_w', 'grad_ln1_b': 'grad_w', 'grad_w_ff1': 'grad_w', 'grad_w_ff2': 'grad_w', 'grad_ln2_g': 'grad_w', 'grad_ln2_b': 'grad_w', 'delta_ln_in_g': 'delta_w', 'delta_ln_in_b': 'delta_w', 'delta_w_in': 'delta_w', 'delta_b_forget': 'delta_w', 'delta_conv_w': 'delta_w', 'delta_conv_b': 'delta_w', 'delta_w_r': 'delta_w', 'delta_b_r': 'delta_w', 'delta_w_i': 'delta_w', 'delta_b_i': 'delta_w', 'delta_lru_lambda': 'delta_w', 'delta_rel_bias': 'delta_w', 'delta_w_branch': 'delta_w', 'delta_w_gate': 'delta_w', 'delta_b_gate': 'delta_w', 'delta_w_out': 'delta_w', 'delta_ln1_g': 'delta_w', 'delta_ln1_b': 'delta_w', 'delta_w_ff1': 'delta_w', 'delta_w_ff2': 'delta_w', 'delta_ln2_g': 'delta_w', 'delta_ln2_b': 'delta_w', 'new_m_ln_in_g': 'new_m', 'new_m_ln_in_b': 'new_m', 'new_m_w_in': 'new_m', 'new_m_b_forget': 'new_m', 'new_m_conv_w': 'new_m', 'new_m_conv_b': 'new_m', 'new_m_w_r': 'new_m', 'new_m_b_r': 'new_m', 'new_m_w_i': 'new_m', 'new_m_b_i': 'new_m', 'new_m_lru_lambda': 'new_m', 'new_m_rel_bias': 'new_m', 'new_m_w_branch': 'new_m', 'new_m_w_gate': 'new_m', 'new_m_b_gate': 'new_m', 'new_m_w_out': 'new_m', 'new_m_ln1_g': 'new_m', 'new_m_ln1_b': 'new_m', 'new_m_w_ff1': 'new_m', 'new_m_w_ff2': 'new_m', 'new_m_ln2_g': 'new_m', 'new_m_ln2_b': 'new_m', 'new_v_ln_in_g': 'new_v', 'new_v_ln_in_b': 'new_v', 'new_v_w_in': 'new_v', 'new_v_b_forget': 'new_v', 'new_v_conv_w': 'new_v', 'new_v_conv_b': 'new_v', 'new_v_w_r': 'new_v', 'new_v_b_r': 'new_v', 'new_v_w_i': 'new_v', 'new_v_b_i': 'new_v', 'new_v_lru_lambda': 'new_v', 'new_v_rel_bias': 'new_v', 'new_v_w_branch': 'new_v', 'new_v_w_gate': 'new_v', 'new_v_b_gate': 'new_v', 'new_v_w_out': 'new_v', 'new_v_ln1_g': 'new_v', 'new_v_ln1_b': 'new_v', 'new_v_w_ff1': 'new_v', 'new_v_w_ff2': 'new_v', 'new_v_ln2_g': 'new_v', 'new_v_ln2_b': 'new_v'}


def _forward(args):
    return _fwd_reference(*[args[k] for k in FWD_PARAMS])


def _output_shape():
    def fwd():
        inp = _fwd_setup_inputs(0)
        return _fwd_reference(*[inp[k] for k in FWD_PARAMS])
    out = _jax.eval_shape(fwd)
    return out.shape, out.dtype

N_MICROBATCH = 1
ADAM_LR = 0.001
ADAM_B1 = 0.9
ADAM_B2 = 0.999
ADAM_EPS = 1e-08
ADAM_WD = 0.01
ADAM_STEP = 10
PER_EXAMPLE_BATCH_AXIS = {'x': 0, 'loss_target': 0}
SHARED_INPUTS = []
_WEIGHT_DTYPES = {'ln_in_g': _jnp.float32, 'ln_in_b': _jnp.float32, 'w_in': _jnp.float32, 'b_forget': _jnp.float32, 'conv_w': _jnp.float32, 'conv_b': _jnp.float32, 'w_r': _jnp.float32, 'b_r': _jnp.float32, 'w_i': _jnp.float32, 'b_i': _jnp.float32, 'lru_lambda': _jnp.float32, 'rel_bias': _jnp.float32, 'w_branch': _jnp.float32, 'w_gate': _jnp.float32, 'b_gate': _jnp.float32, 'w_out': _jnp.float32, 'ln1_g': _jnp.float32, 'ln1_b': _jnp.float32, 'w_ff1': _jnp.float32, 'w_ff2': _jnp.float32, 'ln2_g': _jnp.float32, 'ln2_b': _jnp.float32}
MOMENT_SCALE = {'ln_in_g': 7.129136e-01, 'ln_in_b': 5.945262e-01, 'w_in': 2.219368e-02, 'b_forget': 2.548970e-01, 'conv_w': 4.874996e-02, 'conv_b': 3.912969e-01, 'w_r': 9.731001e-03, 'b_r': 1.045813e-02, 'w_i': 1.753935e-02, 'b_i': 1.199068e-02, 'lru_lambda': 2.368313e-02, 'rel_bias': 4.298463e-03, 'w_branch': 1.626634e-02, 'w_gate': 4.989981e-03, 'b_gate': 6.671535e-03, 'w_out': 6.524829e-02, 'ln1_g': 8.669056e-01, 'ln1_b': 5.276892e-01, 'w_ff1': 2.987860e-02, 'w_ff2': 1.717441e-01, 'ln2_g': 2.271552e+01, 'ln2_b': 5.229852e+00}


def _to_microbatches(a, axis):
    t = _jnp.moveaxis(a, axis, 0)
    t = t.reshape((N_MICROBATCH, t.shape[0] // N_MICROBATCH) + t.shape[1:])
    return _jnp.moveaxis(t, 1, axis + 1)


def setup_inputs(seed: int = 0) -> dict:
    inp = _fwd_setup_inputs(seed)
    key = _jax.random.fold_in(_jax.random.key(seed), 7919)
    shape, _ = _output_shape()
    out = dict(inp)
    out["loss_target"] = _jax.random.normal(_jax.random.fold_in(key, 0), shape, _jnp.float32)
    for i, name in enumerate(TWIN_WEIGHTS):
        w = inp[name].astype(_jnp.float32)
        if MOMENT_SCALE is None:
            s = _jnp.sqrt(_jnp.mean(_jnp.square(w)) + 1e-30)
        else:
            s = MOMENT_SCALE[name]
        km, kv = _jax.random.split(_jax.random.fold_in(key, i + 1))
        out[name] = w
        out["m_" + name] = s * _jax.random.normal(km, w.shape, _jnp.float32)
        out["v_" + name] = (s * s) * _jax.random.uniform(kv, w.shape, _jnp.float32, 0.5, 1.5)
    if N_MICROBATCH > 1:
        for name, axis in PER_EXAMPLE_BATCH_AXIS.items():
            out[name] = _to_microbatches(out[name], axis)
    return {'x': out['x'], 'ln_in_g': out['ln_in_g'], 'ln_in_b': out['ln_in_b'], 'w_in': out['w_in'], 'b_forget': out['b_forget'], 'conv_w': out['conv_w'], 'conv_b': out['conv_b'], 'w_r': out['w_r'], 'b_r': out['b_r'], 'w_i': out['w_i'], 'b_i': out['b_i'], 'lru_lambda': out['lru_lambda'], 'rel_bias': out['rel_bias'], 'w_branch': out['w_branch'], 'w_gate': out['w_gate'], 'b_gate': out['b_gate'], 'w_out': out['w_out'], 'ln1_g': out['ln1_g'], 'ln1_b': out['ln1_b'], 'w_ff1': out['w_ff1'], 'w_ff2': out['w_ff2'], 'ln2_g': out['ln2_g'], 'ln2_b': out['ln2_b'], 'loss_target': out['loss_target'], 'm_ln_in_g': out['m_ln_in_g'], 'm_ln_in_b': out['m_ln_in_b'], 'm_w_in': out['m_w_in'], 'm_b_forget': out['m_b_forget'], 'm_conv_w': out['m_conv_w'], 'm_conv_b': out['m_conv_b'], 'm_w_r': out['m_w_r'], 'm_b_r': out['m_b_r'], 'm_w_i': out['m_w_i'], 'm_b_i': out['m_b_i'], 'm_lru_lambda': out['m_lru_lambda'], 'm_rel_bias': out['m_rel_bias'], 'm_w_branch': out['m_w_branch'], 'm_w_gate': out['m_w_gate'], 'm_b_gate': out['m_b_gate'], 'm_w_out': out['m_w_out'], 'm_ln1_g': out['m_ln1_g'], 'm_ln1_b': out['m_ln1_b'], 'm_w_ff1': out['m_w_ff1'], 'm_w_ff2': out['m_w_ff2'], 'm_ln2_g': out['m_ln2_g'], 'm_ln2_b': out['m_ln2_b'], 'v_ln_in_g': out['v_ln_in_g'], 'v_ln_in_b': out['v_ln_in_b'], 'v_w_in': out['v_w_in'], 'v_b_forget': out['v_b_forget'], 'v_conv_w': out['v_conv_w'], 'v_conv_b': out['v_conv_b'], 'v_w_r': out['v_w_r'], 'v_b_r': out['v_b_r'], 'v_w_i': out['v_w_i'], 'v_b_i': out['v_b_i'], 'v_lru_lambda': out['v_lru_lambda'], 'v_rel_bias': out['v_rel_bias'], 'v_w_branch': out['v_w_branch'], 'v_w_gate': out['v_w_gate'], 'v_b_gate': out['v_b_gate'], 'v_w_out': out['v_w_out'], 'v_ln1_g': out['v_ln1_g'], 'v_ln1_b': out['v_ln1_b'], 'v_w_ff1': out['v_w_ff1'], 'v_w_ff2': out['v_w_ff2'], 'v_ln2_g': out['v_ln2_g'], 'v_ln2_b': out['v_ln2_b']}


def _loss(weights, diff, rest, loss_target):
    with _jax.named_scope("forward"):
        args = {**rest, TWIN_DIFF_INPUT: diff, **{k: w.astype(_WEIGHT_DTYPES[k]) for k, w in weights.items()}}
        y = _forward(args)
    with _jax.named_scope("loss_head"):
        err = _jnp.square(y.astype(_jnp.float32) - loss_target)
        return 0.5 * _jnp.sum(_jnp.mean(err, axis=-1)) if err.ndim else 0.5 * err


def _adamw(w, g, m, v):
    m = ADAM_B1 * m + (1.0 - ADAM_B1) * g
    v = ADAM_B2 * v + (1.0 - ADAM_B2) * _jnp.square(g)
    m_hat = m / (1.0 - ADAM_B1 ** ADAM_STEP)
    v_hat = v / (1.0 - ADAM_B2 ** ADAM_STEP)
    delta = -ADAM_LR * (m_hat / (_jnp.sqrt(v_hat) + ADAM_EPS) + ADAM_WD * w)
    return delta, m, v


def reference(x, ln_in_g, ln_in_b, w_in, b_forget, conv_w, conv_b, w_r, b_r, w_i, b_i, lru_lambda, rel_bias, w_branch, w_gate, b_gate, w_out, ln1_g, ln1_b, w_ff1, w_ff2, ln2_g, ln2_b, loss_target, m_ln_in_g, m_ln_in_b, m_w_in, m_b_forget, m_conv_w, m_conv_b, m_w_r, m_b_r, m_w_i, m_b_i, m_lru_lambda, m_rel_bias, m_w_branch, m_w_gate, m_b_gate, m_w_out, m_ln1_g, m_ln1_b, m_w_ff1, m_w_ff2, m_ln2_g, m_ln2_b, v_ln_in_g, v_ln_in_b, v_w_in, v_b_forget, v_conv_w, v_conv_b, v_w_r, v_b_r, v_w_i, v_b_i, v_lru_lambda, v_rel_bias, v_w_branch, v_w_gate, v_b_gate, v_w_out, v_ln1_g, v_ln1_b, v_w_ff1, v_w_ff2, v_ln2_g, v_ln2_b):
    given = dict(x=x, ln_in_g=ln_in_g, ln_in_b=ln_in_b, w_in=w_in, b_forget=b_forget, conv_w=conv_w, conv_b=conv_b, w_r=w_r, b_r=b_r, w_i=w_i, b_i=b_i, lru_lambda=lru_lambda, rel_bias=rel_bias, w_branch=w_branch, w_gate=w_gate, b_gate=b_gate, w_out=w_out, ln1_g=ln1_g, ln1_b=ln1_b, w_ff1=w_ff1, w_ff2=w_ff2, ln2_g=ln2_g, ln2_b=ln2_b, loss_target=loss_target, m_ln_in_g=m_ln_in_g, m_ln_in_b=m_ln_in_b, m_w_in=m_w_in, m_b_forget=m_b_forget, m_conv_w=m_conv_w, m_conv_b=m_conv_b, m_w_r=m_w_r, m_b_r=m_b_r, m_w_i=m_w_i, m_b_i=m_b_i, m_lru_lambda=m_lru_lambda, m_rel_bias=m_rel_bias, m_w_branch=m_w_branch, m_w_gate=m_w_gate, m_b_gate=m_b_gate, m_w_out=m_w_out, m_ln1_g=m_ln1_g, m_ln1_b=m_ln1_b, m_w_ff1=m_w_ff1, m_w_ff2=m_w_ff2, m_ln2_g=m_ln2_g, m_ln2_b=m_ln2_b, v_ln_in_g=v_ln_in_g, v_ln_in_b=v_ln_in_b, v_w_in=v_w_in, v_b_forget=v_b_forget, v_conv_w=v_conv_w, v_conv_b=v_conv_b, v_w_r=v_w_r, v_b_r=v_b_r, v_w_i=v_w_i, v_b_i=v_b_i, v_lru_lambda=v_lru_lambda, v_rel_bias=v_rel_bias, v_w_branch=v_w_branch, v_w_gate=v_w_gate, v_b_gate=v_b_gate, v_w_out=v_w_out, v_ln1_g=v_ln1_g, v_ln1_b=v_ln1_b, v_w_ff1=v_w_ff1, v_w_ff2=v_w_ff2, v_ln2_g=v_ln2_g, v_ln2_b=v_ln2_b)
    weights = {n: given[n] for n in TWIN_WEIGHTS}
    shared = {n: given[n] for n in SHARED_INPUTS}
    per_example = {n: given[n] for n in ['x']}
    grad_fn = _jax.value_and_grad(_loss, argnums=(0, 1))

    def one_microbatch(ex, loss_target):
        ex = dict(ex)
        diff = ex.pop(TWIN_DIFF_INPUT)
        return grad_fn(weights, diff, {**shared, **ex}, loss_target)

    if N_MICROBATCH == 1:
        loss, (grad_w, grad_x) = one_microbatch(per_example, given["loss_target"])
    else:
        def body(carry, xs):
            loss_sum, grad_sum = carry
            l_k, (gw_k, gx_k) = one_microbatch(xs[0], xs[1])
            with _jax.named_scope("update"):
                return (loss_sum + l_k, _jax.tree.map(_jnp.add, grad_sum, gw_k)), gx_k

        init = (_jnp.zeros((), _jnp.float32), _jax.tree.map(_jnp.zeros_like, weights))
        (loss, grad_w), grad_x = _jax.lax.scan(body, init, (per_example, given["loss_target"]))
    with _jax.named_scope("update"):
        delta_w, new_m, new_v = {}, {}, {}
        for n in TWIN_WEIGHTS:
            delta_w[n], new_m[n], new_v[n] = _adamw(weights[n], grad_w[n], given["m_" + n], given["v_" + n])
    return (loss, grad_x, *[grad_w[n] for n in TWIN_WEIGHTS], *[delta_w[n] for n in TWIN_WEIGHTS],
            *[new_m[n] for n in TWIN_WEIGHTS], *[new_v[n] for n in TWIN_WEIGHTS])
```

```python
import functools
import math

import jax
import jax.numpy as jnp
from jax import lax
from jax.experimental import pallas as pl
from jax.experimental.pallas import tpu as pltpu

F32, BF16, I32 = jnp.float32, jnp.bfloat16, jnp.int32
MESH = pl.DeviceIdType.MESH
ANY = pl.BlockSpec(memory_space=pl.ANY)

LANE = 128
VMEM_LIMIT = 56 * 1024 * 1024
N_DEV = 8

HEAD = 128
CHUNK = 64
LOOKBACK = 8
BAND = (LOOKBACK + 1) * CHUNK
QBLK = 2 * CHUNK
WIN = BAND + CHUNK
PADK = LOOKBACK * CHUNK
REL_CLIP = 256
REL_TABLE = REL_CLIP + CHUNK
REL_PAD = 384
CONV_WIDTH = 4
LRU_C = 8.0
LN_EPS = 1e-5
DEPTH = 2
ALPHA = (2.0 * DEPTH) ** 0.25
NEG = -1e30
SB_DEAD = -104.0
GELU_K = math.sqrt(2.0 / math.pi)
GELU_C = 0.044715

ADAM_LR, ADAM_B1, ADAM_B2, ADAM_EPS, ADAM_WD, ADAM_STEP = 0.001, 0.9, 0.999, 1e-08, 0.01, 10

NN = (((1,), (0,)), ((), ()))
NT = (((1,), (1,)), ((), ()))
TN = (((0,), (0,)), ((), ()))

FQ, FK, FV, RX, RY, SQ, SK, SV, CQ, CK, CV = range(11)

NAMES = ['x', 'ln_in_g', 'ln_in_b', 'w_in', 'b_forget', 'conv_w', 'conv_b', 'w_r', 'b_r', 'w_i', 'b_i', 'lru_lambda',
         'rel_bias', 'w_branch', 'w_gate', 'b_gate', 'w_out', 'ln1_g', 'ln1_b', 'w_ff1', 'w_ff2', 'ln2_g', 'ln2_b']
WEIGHTS = NAMES[1:]


def _cp(sem=None):
    return pltpu.CompilerParams(dimension_semantics=sem, vmem_limit_bytes=VMEM_LIMIT)


def _iota(shape, dim):
    return lax.broadcasted_iota(I32, shape, dim)


def _sigmoid(x):
    return 1.0 / (1.0 + jnp.exp(-x))


def _log_sigmoid(x):
    return jnp.minimum(x, 0.0) - jnp.log(1.0 + jnp.exp(-jnp.abs(x)))


def _pow2_rows(rows, cols, elems=262144):
    t = 8
    while t * 2 <= rows and t * 2 * cols <= elems and rows % (t * 2) == 0:
        t *= 2
    return t


def _position():
    return lax.axis_index("x"), lax.axis_index("y"), lax.axis_index("c")


def _exchange_chips(name, xs, gather):
    n = len(xs)

    def body(*refs):
        ins, outs = refs[:n], refs[n:2 * n]
        send_sems, recv_sems, local_sems = refs[2 * n:]
        x, y, c = _position()
        q = 2 * x + y
        chips = [(1 - x, y), (x, 1 - y), (1 - x, 1 - y)]

        def src(t, slot):
            return ins[t] if gather else ins[t].at[slot]

        local = [pltpu.make_async_copy(src(t, q), outs[t].at[q], local_sems.at[t]) for t in range(n)]
        for cp in local:
            cp.start()

        def remote(t, j, landing):
            px, py = chips[j]
            return pltpu.make_async_remote_copy(
                src_ref=src(t, 2 * px + py), dst_ref=outs[t].at[landing], send_sem=send_sems.at[t, j],
                recv_sem=recv_sems.at[t, j], device_id=(px, py, c), device_id_type=MESH)

        sends = [remote(t, j, q) for t in range(n) for j in range(3)]
        for cp in sends:
            cp.start()
        for t in range(n):
            for j, (px, py) in enumerate(chips):
                remote(t, j, 2 * px + py).wait_recv()
        for cp in sends:
            cp.wait_send()
        for cp in local:
            cp.wait()

    out_shape = [jax.ShapeDtypeStruct((4,) + (a.shape if gather else a.shape[1:]), a.dtype) for a in xs]
    return pl.pallas_call(
        body, name=name, out_shape=out_shape, in_specs=[ANY] * n, out_specs=[ANY] * n,
        scratch_shapes=[pltpu.SemaphoreType.DMA((n, 3)), pltpu.SemaphoreType.DMA((n, 3)),
                        pltpu.SemaphoreType.DMA((n,))],
    )(*xs)


def _exchange_cores(name, xs, gather):
    n = len(xs)
    m = 1 if gather else 4

    def body(*refs):
        ins, outs = refs[:n], refs[n:2 * n]
        send_sems, recv_sems, local_sems = refs[2 * n:]
        x, y, c = _position()
        sibling = (x, y, 1 - c)

        def remote(t, j, landing):
            s = ins[t] if gather else ins[t].at[j, 1 - c]
            d = outs[t].at[landing] if gather else outs[t].at[j]
            return pltpu.make_async_remote_copy(
                src_ref=s, dst_ref=d, send_sem=send_sems.at[t, j], recv_sem=recv_sems.at[t, j],
                device_id=sibling, device_id_type=MESH)

        sends = [remote(t, j, c) for t in range(n) for j in range(m)]
        for cp in sends:
            cp.start()
        local = []
        if gather:
            local = [pltpu.make_async_copy(ins[t], outs[t].at[c], local_sems.at[t]) for t in range(n)]
            for cp in local:
                cp.start()
        for t in range(n):
            for j in range(m):
                remote(t, j, 1 - c).wait_recv()
        for cp in sends:
            cp.wait_send()
        for cp in local:
            cp.wait()

    if gather:
        out_shape = [jax.ShapeDtypeStruct((2,) + a.shape, a.dtype) for a in xs]
    else:
        out_shape = [jax.ShapeDtypeStruct((4,) + a.shape[2:], a.dtype) for a in xs]
    return pl.pallas_call(
        body, name=name, out_shape=out_shape, in_specs=[ANY] * n, out_specs=[ANY] * n,
        scratch_shapes=[pltpu.SemaphoreType.DMA((n, m)), pltpu.SemaphoreType.DMA((n, m)),
                        pltpu.SemaphoreType.DMA((n,))],
    )(*xs)


def _add_core_halves(name, mine, other, c, out_dtype):
    _, _, rows, cols = mine.shape
    tr = _pow2_rows(rows, cols)

    def body(c_ref, a_ref, b_ref, o_ref):
        o_ref[...] = (a_ref[...] + b_ref[...]).astype(out_dtype)

    grid_spec = pltpu.PrefetchScalarGridSpec(
        num_scalar_prefetch=1, grid=(4, rows // tr),
        in_specs=[pl.BlockSpec((None, None, tr, cols), lambda j, i, c_ref: (j, c_ref[0], i, 0)),
                  pl.BlockSpec((None, tr, cols), lambda j, i, c_ref: (j, i, 0))],
        out_specs=pl.BlockSpec((None, tr, cols), lambda j, i, c_ref: (j, i, 0)))
    return pl.pallas_call(body, name=name, grid_spec=grid_spec,
                          out_shape=jax.ShapeDtypeStruct((4, rows, cols), out_dtype),
                          compiler_params=_cp(("parallel", "parallel")))(c, mine, other)


def _sum_parts(name, parts):
    p, rows, cols = parts.shape
    tr = _pow2_rows(rows, cols * p)

    def body(a_ref, o_ref):
        acc = a_ref[0]
        for k in range(1, p):
            acc = acc + a_ref[k]
        o_ref[...] = acc

    return pl.pallas_call(body, name=name, grid=(rows // tr,),
                          in_specs=[pl.BlockSpec((p, tr, cols), lambda i: (0, i, 0))],
                          out_specs=pl.BlockSpec((tr, cols), lambda i: (i, 0)),
                          out_shape=jax.ShapeDtypeStruct((rows, cols), F32),
                          compiler_params=_cp(("parallel",)))(parts)


def _mm(name, a, b, *, grid, a_spec, b_spec, dims, acc_shape, out_shape, out_specs, finish,
        extras=(), extra_specs=(), aliases=None, sem=("parallel", "parallel", "arbitrary")):
    nk, ne, no = grid[2], len(extras), len(out_shape)

    def body(*refs):
        a_ref, b_ref = refs[0], refs[1]
        ex, outs, acc = refs[2:2 + ne], refs[2 + ne:2 + ne + no], refs[2 + ne + no]
        ids = (pl.program_id(0), pl.program_id(1))
        k = pl.program_id(2)

        @pl.when(k == 0)
        def _():
            acc[...] = jnp.zeros(acc.shape, F32)

        acc[...] += lax.dot_general(a_ref[...], b_ref[...], dims, preferred_element_type=F32)

        @pl.when(k == nk - 1)
        def _():
            finish(acc[...], ex, outs, ids)

    return pl.pallas_call(
        body, name=name, grid=grid, in_specs=[a_spec, b_spec, *extra_specs], out_specs=out_specs,
        out_shape=out_shape, scratch_shapes=[pltpu.VMEM(acc_shape, F32)],
        input_output_aliases=aliases or {}, compiler_params=_cp(sem))(a, b, *extras)


def _store(dtype):
    def finish(acc, ex, outs, ids):
        outs[0][...] = acc.reshape(outs[0].shape).astype(dtype)
    return finish


def _layer_norm_rows(z, g, b):
    mu = jnp.mean(z, axis=1, keepdims=True)
    zc = z - mu
    var = jnp.mean(zc * zc, axis=1, keepdims=True)
    return zc * lax.rsqrt(var + LN_EPS) * g + b


def _mm_ln(name, a, w, l, resid, g, b):
    s, kdim = a.shape
    d = w.shape[2]
    tm, tk = min(512, s), min(512, kdim)

    def finish(acc, ex, outs, ids):
        z = acc + ALPHA * ex[0][...]
        y = _layer_norm_rows(z, ex[1][...], ex[2][...])
        outs[0][...] = z
        outs[1][...] = y
        outs[2][...] = y.astype(BF16)

    row = pl.BlockSpec((tm, d), lambda m, n, k: (m, 0))
    vec = pl.BlockSpec((1, d), lambda m, n, k: (0, 0))
    return _mm(name, a, w, grid=(s // tm, 1, kdim // tk),
               a_spec=pl.BlockSpec((tm, tk), lambda m, n, k: (m, k)),
               b_spec=pl.BlockSpec((None, tk, d), lambda m, n, k: (l, k, 0)),
               dims=NN, acc_shape=(tm, d),
               out_shape=[jax.ShapeDtypeStruct((s, d), F32), jax.ShapeDtypeStruct((s, d), F32),
                          jax.ShapeDtypeStruct((s, d), BF16)],
               out_specs=[row, row, row], finish=finish,
               extras=(resid, g.reshape(1, d), b.reshape(1, d)), extra_specs=(row, vec, vec))


def _ln_fwd(name, x, g, b):
    s, d = x.shape
    tr = min(256, s)

    def body(x_ref, g_ref, b_ref, y_ref, yb_ref):
        y = _layer_norm_rows(x_ref[...], g_ref[...], b_ref[...])
        y_ref[...] = y
        yb_ref[...] = y.astype(BF16)

    row = pl.BlockSpec((tr, d), lambda i: (i, 0))
    vec = pl.BlockSpec((1, d), lambda i: (0, 0))
    return pl.pallas_call(body, name=name, grid=(s // tr,), in_specs=[row, vec, vec], out_specs=[row, row],
                          out_shape=[jax.ShapeDtypeStruct((s, d), F32), jax.ShapeDtypeStruct((s, d), BF16)],
                          compiler_params=_cp(("parallel",)))(x, g.reshape(1, d), b.reshape(1, d))


def _ln_bwd(name, dy, z, g):
    s, d = z.shape
    tr = min(256, s)

    def body(dy_ref, z_ref, g_ref, dz_ref, dzb_ref, dg_ref, db_ref):
        @pl.when(pl.program_id(0) == 0)
        def _():
            dg_ref[...] = jnp.zeros(dg_ref.shape, F32)
            db_ref[...] = jnp.zeros(db_ref.shape, F32)

        zz, dyv = z_ref[...], dy_ref[...]
        mu = jnp.mean(zz, axis=1, keepdims=True)
        zc = zz - mu
        rstd = lax.rsqrt(jnp.mean(zc * zc, axis=1, keepdims=True) + LN_EPS)
        xhat = zc * rstd
        dg_ref[...] += jnp.sum(dyv * xhat, axis=0, keepdims=True)
        db_ref[...] += jnp.sum(dyv, axis=0, keepdims=True)
        dxh = dyv * g_ref[...]
        dz = rstd * (dxh - jnp.mean(dxh, axis=1, keepdims=True) - xhat * jnp.mean(dxh * xhat, axis=1, keepdims=True))
        dz_ref[...] = dz
        dzb_ref[...] = dz.astype(BF16)

    row = pl.BlockSpec((tr, d), lambda i: (i, 0))
    vec = pl.BlockSpec((1, d), lambda i: (0, 0))
    return pl.pallas_call(
        body, name=name, grid=(s // tr,), in_specs=[row, row, vec], out_specs=[row, row, vec, vec],
        out_shape=[jax.ShapeDtypeStruct((s, d), F32), jax.ShapeDtypeStruct((s, d), BF16),
                   jax.ShapeDtypeStruct((1, d), F32), jax.ShapeDtypeStruct((1, d), F32)],
        compiler_params=_cp(("arbitrary",)))(dy, z, g.reshape(1, d))


def _loss_head(name, y, target):
    s, d = y.shape
    tr = min(256, s)

    def body(y_ref, t_ref, loss_ref, dy_ref):
        @pl.when(pl.program_id(0) == 0)
        def _():
            loss_ref[...] = jnp.zeros(loss_ref.shape, F32)

        e = y_ref[...] - t_ref[...]
        dy_ref[...] = e * (1.0 / d)
        loss_ref[...] += jnp.sum(e * e) * (0.5 / d)

    row = pl.BlockSpec((tr, d), lambda i: (i, 0))
    return pl.pallas_call(
        body, name=name, grid=(s // tr,), in_specs=[row, row],
        out_specs=[pl.BlockSpec((8, LANE), lambda i: (0, 0)), row],
        out_shape=[jax.ShapeDtypeStruct((8, LANE), F32), jax.ShapeDtypeStruct((s, d), F32)],
        compiler_params=_cp(("arbitrary",)))(y, target)


def _scan_add(x, reverse):
    ts = x.shape[0]
    rows = _iota((ts, 1), 0)
    dist = 1
    while dist < ts:
        if reverse:
            x = x + jnp.where(rows < ts - dist, pltpu.roll(x, ts - dist, 0), 0.0)
        else:
            x = x + jnp.where(rows >= dist, pltpu.roll(x, dist, 0), 0.0)
        dist *= 2
    return x


def _scan_affine(a, b, reverse):
    ts = a.shape[0]
    rows = _iota((ts, 1), 0)
    dist = 1
    while dist < ts:
        shift = ts - dist if reverse else dist
        valid = rows < ts - dist if reverse else rows >= dist
        b = b + a * jnp.where(valid, pltpu.roll(b, shift, 0), 0.0)
        a = a * jnp.where(valid, pltpu.roll(a, shift, 0), 1.0)
        dist *= 2
    return a, b


def _cum_forget_fwd(name, fl, bias):
    s = fl.shape[0]
    ts = min(1024, s)

    def body(f_ref, b_ref, o_ref, carry):
        @pl.when(pl.program_id(0) == 0)
        def _():
            carry[...] = jnp.zeros(carry.shape, F32)

        o_ref[...] = _scan_add(_log_sigmoid(f_ref[...] + b_ref[...]), False) + carry[...]
        carry[...] = o_ref[pl.ds(ts - 1, 1), :]

    row = pl.BlockSpec((ts, LANE), lambda i: (i, 0))
    return pl.pallas_call(body, name=name, grid=(s // ts,),
                          in_specs=[row, pl.BlockSpec((1, LANE), lambda i: (0, 0))], out_specs=row,
                          out_shape=jax.ShapeDtypeStruct((s, LANE), F32),
                          scratch_shapes=[pltpu.VMEM((1, LANE), F32)],
                          compiler_params=_cp(("arbitrary",)))(fl, bias)


def _cum_forget_bwd(name, dcf, fl, bias):
    s = fl.shape[0]
    ts = min(1024, s)
    nb = s // ts

    def body(d_ref, f_ref, b_ref, o_ref, db_ref, carry):
        @pl.when(pl.program_id(0) == 0)
        def _():
            carry[...] = jnp.zeros(carry.shape, F32)
            db_ref[...] = jnp.zeros(db_ref.shape, F32)

        run = _scan_add(d_ref[...], True) + carry[...]
        carry[...] = jnp.sum(jnp.where(_iota((ts, 1), 0) == 0, run, 0.0), axis=0, keepdims=True)
        dfl = run * _sigmoid(-(f_ref[...] + b_ref[...]))
        o_ref[...] = dfl.astype(BF16)
        db_ref[...] += jnp.sum(dfl, axis=0, keepdims=True)

    row = pl.BlockSpec((ts, LANE), lambda i: (nb - 1 - i, 0))
    vec = pl.BlockSpec((1, LANE), lambda i: (0, 0))
    return pl.pallas_call(body, name=name, grid=(nb,), in_specs=[row, row, vec], out_specs=[row, vec],
                          out_shape=[jax.ShapeDtypeStruct((s, LANE), BF16), jax.ShapeDtypeStruct((1, LANE), F32)],
                          scratch_shapes=[pltpu.VMEM((1, LANE), F32)],
                          compiler_params=_cp(("arbitrary",)))(dcf, fl, bias)


def _fox_specs(s, nh, tq, tk):
    q = pl.BlockSpec((tq, HEAD), lambda h, i: (i, FQ * nh + h))
    k = pl.BlockSpec((s, HEAD), lambda h, i: (0, FK * nh + h))
    v = pl.BlockSpec((s, HEAD), lambda h, i: (0, FV * nh + h))
    col = pl.BlockSpec((None, tq, 1), lambda h, i: (h, i, 0))
    rowv = pl.BlockSpec((None, s // tk, 1, tk), lambda h, i: (h, 0, 0, 0))
    tile = pl.BlockSpec((tq, HEAD), lambda h, i: (i, h))
    full = pl.BlockSpec((s, HEAD), lambda h, i: (0, h))
    return q, k, v, col, rowv, tile, full


def _fox_scores(q, k_ref, cfq, cfr_ref, kb, tk, qpos, scale):
    off = pl.multiple_of(kb * tk, tk)
    k = k_ref[pl.ds(off, tk), :]
    sc = lax.dot_general(q, k, NT, preferred_element_type=F32) * scale + cfq - cfr_ref[kb]
    mask = kb * tk + _iota((1, tk), 1) <= qpos
    return jnp.where(mask, sc, NEG), mask, k, off


def _fox_fwd(name, u, cf_col, cf_row, bw):
    s, nh = u.shape[0], bw // HEAD
    tq = tk = min(256, s)
    scale = HEAD ** -0.5

    def body(q_ref, k_ref, v_ref, cfc_ref, cfr_ref, o_ref, lse_ref):
        i = pl.program_id(1)
        q, cfq = q_ref[...], cfc_ref[...]
        qpos = i * tq + _iota((tq, 1), 0)

        def step(kb, carry):
            m, l, acc = carry
            sc, _, _, off = _fox_scores(q, k_ref, cfq, cfr_ref, kb, tk, qpos, scale)
            m2 = jnp.maximum(m, jnp.max(sc, axis=1, keepdims=True))
            p = jnp.exp(sc - m2)
            al = jnp.exp(m - m2)
            l2 = al * l + jnp.sum(p, axis=1, keepdims=True)
            acc2 = al * acc + jnp.dot(p.astype(BF16), v_ref[pl.ds(off, tk), :], preferred_element_type=F32)
            return m2, l2, acc2

        init = (jnp.full((tq, 1), NEG, F32), jnp.zeros((tq, 1), F32), jnp.zeros((tq, HEAD), F32))
        m, l, acc = lax.fori_loop(0, (i * tq + tq + tk - 1) // tk, step, init)
        o_ref[...] = (acc / l).astype(BF16)
        lse_ref[...] = m + jnp.log(l)

    q, k, v, col, rowv, tile, _ = _fox_specs(s, nh, tq, tk)
    return pl.pallas_call(
        body, name=name, grid=(nh, s // tq), in_specs=[q, k, v, col, rowv], out_specs=[tile, col],
        out_shape=[jax.ShapeDtypeStruct((s, bw), BF16), jax.ShapeDtypeStruct((nh, s, 1), F32)],
        compiler_params=_cp(("parallel", "parallel")))(u, u, u, cf_col, cf_row)


def _fox_bwd(name, u, cf_col, cf_row, o, do, lse, bw):
    s, nh = u.shape[0], bw // HEAD
    tq = tk = min(256, s)
    nq = s // tq
    scale = HEAD ** -0.5

    def body(q_ref, k_ref, v_ref, cfc_ref, cfr_ref, o_ref, do_ref, lse_ref,
             dq_ref, dk_ref, dv_ref, dcc_ref, dcr_ref, dk_s, dv_s):
        i = pl.program_id(1)

        @pl.when(i == 0)
        def _():
            dk_s[...] = jnp.zeros(dk_s.shape, F32)
            dv_s[...] = jnp.zeros(dv_s.shape, F32)
            dcr_ref[...] = jnp.zeros(dcr_ref.shape, F32)

        q, dov, cfq, lse_q = q_ref[...], do_ref[...], cfc_ref[...], lse_ref[...]
        delta = jnp.sum(dov.astype(F32) * o_ref[...].astype(F32), axis=1, keepdims=True)
        qpos = i * tq + _iota((tq, 1), 0)

        def step(kb, carry):
            dq, dcq = carry
            sc, mask, k, off = _fox_scores(q, k_ref, cfq, cfr_ref, kb, tk, qpos, scale)
            p = jnp.where(mask, jnp.exp(sc - lse_q), 0.0)
            dp = lax.dot_general(dov, v_ref[pl.ds(off, tk), :], NT, preferred_element_type=F32)
            ds = p * (dp - delta)
            dsb = ds.astype(BF16)
            dk_s[pl.ds(off, tk), :] += lax.dot_general(dsb, q, TN, preferred_element_type=F32)
            dv_s[pl.ds(off, tk), :] += lax.dot_general(p.astype(BF16), dov, TN, preferred_element_type=F32)
            dcr_ref[kb] += -jnp.sum(ds, axis=0, keepdims=True)
            return (dq + jnp.dot(dsb, k, preferred_element_type=F32), dcq + jnp.sum(ds, axis=1, keepdims=True))

        init = (jnp.zeros((tq, HEAD), F32), jnp.zeros((tq, 1), F32))
        dq, dcq = lax.fori_loop(0, (i * tq + tq + tk - 1) // tk, step, init)
        dq_ref[...] = (dq * scale).astype(BF16)
        dcc_ref[...] = dcq

        @pl.when(i == nq - 1)
        def _():
            dk_ref[...] = (dk_s[...] * scale).astype(BF16)
            dv_ref[...] = dv_s[...].astype(BF16)

    q, k, v, col, rowv, tile, full = _fox_specs(s, nh, tq, tk)
    return pl.pallas_call(
        body, name=name, grid=(nh, nq), in_specs=[q, k, v, col, rowv, tile, tile, col],
        out_specs=[tile, full, full, col, rowv],
        out_shape=[jax.ShapeDtypeStruct((s, bw), BF16)] * 3
        + [jax.ShapeDtypeStruct((nh, s, 1), F32), jax.ShapeDtypeStruct((nh, s // tk, 1, tk), F32)],
        scratch_shapes=[pltpu.VMEM((s, HEAD), F32), pltpu.VMEM((s, HEAD), F32)],
        compiler_params=_cp(("arbitrary", "arbitrary")))(u, u, u, cf_col, cf_row, o, do, lse)


def _suffix_mm(x, ones_below):
    hi = x.astype(BF16)
    lo = (x - hi.astype(F32)).astype(BF16)
    return (jnp.dot(hi, ones_below, preferred_element_type=F32) + jnp.dot(lo, ones_below, preferred_element_type=F32))


def _sb_tile(q, k_ref, kb, tk, qpos, scale):
    off = pl.multiple_of(kb * tk, tk)
    k = k_ref[pl.ds(off, tk), :]
    z = lax.dot_general(q, k, NT, preferred_element_type=F32) * scale
    mask = kb * tk + _iota((1, tk), 1) < qpos
    lsn = -jnp.maximum(z, 0.0) - jnp.log(1.0 + jnp.exp(-jnp.abs(z)))
    return z, mask, lsn, jnp.where(mask, lsn, 0.0), k, off


def _sb_specs(s, nh, tq):
    q = pl.BlockSpec((tq, HEAD), lambda h, i: (i, SQ * nh + h))
    k = pl.BlockSpec((s, HEAD), lambda h, i: (0, SK * nh + h))
    v = pl.BlockSpec((s, HEAD), lambda h, i: (0, SV * nh + h))
    tile = pl.BlockSpec((tq, HEAD), lambda h, i: (i, h))
    full = pl.BlockSpec((s, HEAD), lambda h, i: (0, h))
    return q, k, v, tile, full


def _sb_fwd(name, u, bw):
    s, nh = u.shape[0], bw // HEAD
    tq = tk = 128
    scale = HEAD ** -0.5

    def body(q_ref, k_ref, v_ref, o_ref):
        i = pl.program_id(1)
        q = q_ref[...]
        qpos = i * tq + _iota((tq, 1), 0)
        later_keys = (_iota((tk, tk), 0) > _iota((tk, tk), 1)).astype(BF16)
        nk = (i * tq + tq + tk - 2) // tk

        def cond(st):
            return jnp.logical_and(st[0] < nk, st[3] > SB_DEAD)

        def step(st):
            j, c, acc, _ = st
            z, mask, lsn, lm, _, off = _sb_tile(q, k_ref, nk - 1 - j, tk, qpos, scale)
            a = jnp.where(mask, jnp.exp(lsn + z + c + _suffix_mm(lm, later_keys)), 0.0)
            acc = acc + jnp.dot(a.astype(BF16), v_ref[pl.ds(off, tk), :], preferred_element_type=F32)
            c = c + jnp.sum(lm, axis=1, keepdims=True)
            return j + 1, c, acc, jnp.max(c)

        init = (jnp.int32(0), jnp.zeros((tq, 1), F32), jnp.zeros((tq, HEAD), F32), jnp.float32(0.0))
        o_ref[...] = lax.while_loop(cond, step, init)[2].astype(BF16)

    q, k, v, tile, _ = _sb_specs(s, nh, tq)
    return pl.pallas_call(body, name=name, grid=(nh, s // tq), in_specs=[q, k, v], out_specs=tile,
                          out_shape=jax.ShapeDtypeStruct((s, bw), BF16),
                          compiler_params=_cp(("parallel", "parallel")))(u, u, u)


def _sb_bwd(name, u, do, bw):
    s, nh = u.shape[0], bw // HEAD
    tq = tk = 128
    nq = s // tq
    scale = HEAD ** -0.5

    def body(q_ref, k_ref, v_ref, do_ref, dq_ref, dk_ref, dv_ref, dk_s, dv_s):
        i = pl.program_id(1)

        @pl.when(i == 0)
        def _():
            dk_s[...] = jnp.zeros(dk_s.shape, F32)
            dv_s[...] = jnp.zeros(dv_s.shape, F32)

        q, dov = q_ref[...], do_ref[...]
        qpos = i * tq + _iota((tq, 1), 0)
        later_keys = (_iota((tk, tk), 0) > _iota((tk, tk), 1)).astype(BF16)
        this_and_later = (_iota((tk, tk), 0) >= _iota((tk, tk), 1)).astype(BF16)
        nk = (i * tq + tq + tk - 2) // tk

        def weights(j, c):
            z, mask, lsn, lm, k, off = _sb_tile(q, k_ref, nk - 1 - j, tk, qpos, scale)
            a = jnp.where(mask, jnp.exp(lsn + z + c + _suffix_mm(lm, later_keys)), 0.0)
            w = a * lax.dot_general(dov, v_ref[pl.ds(off, tk), :], NT, preferred_element_type=F32)
            return z, mask, lsn, lm, k, off, a, w

        def cond(st):
            return jnp.logical_and(st[0] < nk, st[3] > SB_DEAD)

        def step1(st):
            j, c, wc, _ = st
            _, _, _, lm, _, _, _, w = weights(j, c)
            c = c + jnp.sum(lm, axis=1, keepdims=True)
            return j + 1, c, wc + jnp.sum(w, axis=1, keepdims=True), jnp.max(c)

        zero = jnp.zeros((tq, 1), F32)
        live, _, total, _ = lax.while_loop(cond, step1, (jnp.int32(0), zero, zero, jnp.float32(0.0)))

        def step2(j, st):
            c, wc, dq = st
            z, mask, lsn, lm, k, off, a, w = weights(j, c)
            earlier = total - (wc + _suffix_mm(w, this_and_later))
            dz = jnp.where(mask, w * jnp.exp(lsn) - jnp.exp(lsn + z) * earlier, 0.0)
            dzb = dz.astype(BF16)
            dk_s[pl.ds(off, tk), :] += lax.dot_general(dzb, q, TN, preferred_element_type=F32)
            dv_s[pl.ds(off, tk), :] += lax.dot_general(a.astype(BF16), dov, TN, preferred_element_type=F32)
            return (c + jnp.sum(lm, axis=1, keepdims=True), wc + jnp.sum(w, axis=1, keepdims=True),
                    dq + jnp.dot(dzb, k, preferred_element_type=F32))

        dq = lax.fori_loop(0, live, step2, (zero, zero, jnp.zeros((tq, HEAD), F32)))[2]
        dq_ref[...] = (dq * scale).astype(BF16)

        @pl.when(i == nq - 1)
        def _():
            dk_ref[...] = (dk_s[...] * scale).astype(BF16)
            dv_ref[...] = dv_s[...].astype(BF16)

    q, k, v, tile, full = _sb_specs(s, nh, tq)
    return pl.pallas_call(
        body, name=name, grid=(nh, nq), in_specs=[q, k, v, tile], out_specs=[tile, full, full],
        out_shape=[jax.ShapeDtypeStruct((s, bw), BF16)] * 3,
        scratch_shapes=[pltpu.VMEM((s, HEAD), F32), pltpu.VMEM((s, HEAD), F32)],
        compiler_params=_cp(("arbitrary", "arbitrary")))(u, u, u, do)


def _band_onehot(r):
    kl = _iota((1, WIN), 1)
    ridx = jnp.clip(PADK + r - kl, -(CHUNK - 1), REL_CLIP) + (CHUNK - 1)
    first = (r // CHUNK) * CHUNK
    valid = jnp.logical_and(kl >= first, kl < first + BAND)
    onehot = jnp.logical_and(_iota((REL_PAD, WIN), 0) == ridx, valid)
    return onehot.astype(BF16), valid


def _bias_expand(name, table):
    def body(t_ref, o_ref):
        t = t_ref[...]
        hi = t.astype(BF16)
        r1 = t - hi.astype(F32)
        mid = r1.astype(BF16)
        lo = (r1 - mid.astype(F32)).astype(BF16)

        def row(r, carry):
            onehot, valid = _band_onehot(r)
            val = (jnp.dot(hi, onehot, preferred_element_type=F32) + jnp.dot(mid, onehot, preferred_element_type=F32)
                   + jnp.dot(lo, onehot, preferred_element_type=F32))
            o_ref[r] = jnp.where(valid, val, NEG)
            return carry

        lax.fori_loop(0, QBLK, row, 0)

    return pl.pallas_call(body, name=name, out_shape=jax.ShapeDtypeStruct((QBLK, 16, WIN), F32),
                          in_specs=[pl.BlockSpec(memory_space=pltpu.VMEM)],
                          out_specs=pl.BlockSpec(memory_space=pltpu.VMEM), compiler_params=_cp())(table)


def _bias_reduce(name, ds_rows):
    def body(x_ref, o_ref):
        def row(r, acc):
            onehot, _ = _band_onehot(r)
            x = x_ref[r]
            hi = x.astype(BF16)
            lo = (x - hi.astype(F32)).astype(BF16)
            return (acc + lax.dot_general(hi, onehot, NT, preferred_element_type=F32)
                    + lax.dot_general(lo, onehot, NT, preferred_element_type=F32))

        o_ref[...] = lax.fori_loop(0, QBLK, row, jnp.zeros((16, REL_PAD), F32))

    return pl.pallas_call(body, name=name, out_shape=jax.ShapeDtypeStruct((16, REL_PAD), F32),
                          in_specs=[pl.BlockSpec(memory_space=pltpu.VMEM)],
                          out_specs=pl.BlockSpec(memory_space=pltpu.VMEM), compiler_params=_cp())(ds_rows)


def _chunk_specs(s, nh):
    q = pl.BlockSpec((QBLK, HEAD), lambda h, i: (i, CQ * nh + h))
    kv = pl.BlockSpec((s + PADK, HEAD), lambda h, i: (0, h))
    bias = pl.BlockSpec((None, QBLK, WIN), lambda h, i: (h, 0, 0))
    tile = pl.BlockSpec((QBLK, HEAD), lambda h, i: (i, h))
    full = pl.BlockSpec((s, HEAD), lambda h, i: (0, h))
    return q, kv, bias, tile, full


def _chunk_probs(q, k_ref, b_ref, i, scale):
    off = pl.multiple_of(i * QBLK, QBLK)
    kw = k_ref[pl.ds(off, WIN), :]
    sc = lax.dot_general(q, kw, NT, preferred_element_type=F32) * scale + b_ref[...]
    sc = jnp.where(i * QBLK + _iota((1, WIN), 1) >= PADK, sc, NEG)
    p = jnp.exp(sc - jnp.max(sc, axis=1, keepdims=True))
    return p, jnp.sum(p, axis=1, keepdims=True), kw, off


def _chunk_fwd(name, u, kpad, vpad, bias, bw):
    s, nh = u.shape[0], bw // HEAD
    scale = HEAD ** -0.5

    def body(q_ref, k_ref, v_ref, b_ref, o_ref):
        p, l, _, off = _chunk_probs(q_ref[...], k_ref, b_ref, pl.program_id(1), scale)
        o = jnp.dot(p.astype(BF16), v_ref[pl.ds(off, WIN), :], preferred_element_type=F32)
        o_ref[...] = (o / l).astype(BF16)

    q, kv, bs, tile, _ = _chunk_specs(s, nh)
    return pl.pallas_call(body, name=name, grid=(nh, s // QBLK), in_specs=[q, kv, kv, bs], out_specs=tile,
                          out_shape=jax.ShapeDtypeStruct((s, bw), BF16),
                          compiler_params=_cp(("parallel", "parallel")))(u, kpad, vpad, bias)


def _chunk_bwd(name, u, kpad, vpad, bias, do, bw):
    s, nh = u.shape[0], bw // HEAD
    nq = s // QBLK
    scale = HEAD ** -0.5

    def body(q_ref, k_ref, v_ref, b_ref, do_ref, dq_ref, dk_ref, dv_ref, dss_ref, dk_s, dv_s):
        i = pl.program_id(1)

        @pl.when(i == 0)
        def _():
            dk_s[...] = jnp.zeros(dk_s.shape, F32)
            dv_s[...] = jnp.zeros(dv_s.shape, F32)
            dss_ref[...] = jnp.zeros(dss_ref.shape, F32)

        q, dov = q_ref[...], do_ref[...]
        p, l, kw, off = _chunk_probs(q, k_ref, b_ref, i, scale)
        p = p / l
        dp = lax.dot_general(dov, v_ref[pl.ds(off, WIN), :], NT, preferred_element_type=F32)
        ds = p * (dp - jnp.sum(p * dp, axis=1, keepdims=True))
        dsb = ds.astype(BF16)
        dq_ref[...] = (jnp.dot(dsb, kw, preferred_element_type=F32) * scale).astype(BF16)
        dk_s[pl.ds(off, WIN), :] += lax.dot_general(dsb, q, TN, preferred_element_type=F32)
        dv_s[pl.ds(off, WIN), :] += lax.dot_general(p.astype(BF16), dov, TN, preferred_element_type=F32)
        dss_ref[...] += ds

        @pl.when(i == nq - 1)
        def _():
            dk_ref[...] = (dk_s[pl.ds(PADK, s), :] * scale).astype(BF16)
            dv_ref[...] = dv_s[pl.ds(PADK, s), :].astype(BF16)

    q, kv, bs, tile, full = _chunk_specs(s, nh)
    return pl.pallas_call(
        body, name=name, grid=(nh, nq), in_specs=[q, kv, kv, bs, tile], out_specs=[tile, full, full, bs],
        out_shape=[jax.ShapeDtypeStruct((s, bw), BF16)] * 3 + [jax.ShapeDtypeStruct((nh, QBLK, WIN), F32)],
        scratch_shapes=[pltpu.VMEM((s + PADK, HEAD), F32), pltpu.VMEM((s + PADK, HEAD), F32)],
        compiler_params=_cp(("arbitrary", "arbitrary")))(u, kpad, vpad, bias, do)


def _gelu_parts(y):
    th = jnp.tanh(GELU_K * (y + GELU_C * y * y * y))
    return 0.5 * y * (1.0 + th), th


def _block_diag(xb16, w_ref, nh, dims):
    return jnp.concatenate(
        [lax.dot_general(xb16[:, n * HEAD:(n + 1) * HEAD], w_ref[n], dims, preferred_element_type=F32)
         for n in range(nh)], axis=1)


def _lru_gates(ext, cw_ref, cb_ref, wr_ref, br_ref, wi_ref, bi_ref, lam_ref, ts, nh):
    shifted = [pltpu.roll(ext, CONV_WIDTH - 1 - j, 0)[8:, :] if j < CONV_WIDTH - 1 else ext[8:, :]
               for j in range(CONV_WIDTH)]
    xc = cb_ref[...]
    for j in range(CONV_WIDTH):
        xc = xc + shifted[j] * cw_ref[pl.ds(j, 1), :]
    xcb = xc.astype(BF16)
    r = _sigmoid(_block_diag(xcb, wr_ref, nh, NN) + br_ref[...])
    gi = _sigmoid(_block_diag(xcb, wi_ref, nh, NN) + bi_ref[...])
    lsl = _log_sigmoid(lam_ref[...])
    la = LRU_C * r * lsl
    a = jnp.exp(la)
    e2 = jnp.exp(2.0 * la)
    mult = jnp.sqrt(-jnp.tanh(la) * (e2 + 1.0))
    return shifted, xc, xcb, r, gi, lsl, a, e2, mult


def _lru_param_specs(bw, nh):
    vec = pl.BlockSpec((1, bw), lambda i: (0, 0))
    conv = pl.BlockSpec((8, bw), lambda i: (0, 0))
    blocks = pl.BlockSpec((nh, HEAD, HEAD), lambda i: (0, 0, 0))
    return [conv, vec, blocks, vec, blocks, vec, vec]


def _lru_fwd(name, u, params, bw):
    s, nh = u.shape[0], bw // HEAD
    ts = min(512, s)

    def body(rx_ref, ry_ref, cw_ref, cb_ref, wr_ref, br_ref, wi_ref, bi_ref, lam_ref, o_ref, h_ref, tail, hcar):
        @pl.when(pl.program_id(0) == 0)
        def _():
            tail[...] = jnp.zeros(tail.shape, F32)
            hcar[...] = jnp.zeros(hcar.shape, F32)

        rx = rx_ref[...].astype(F32)
        ext = jnp.concatenate([tail[...], rx], axis=0)
        tail[...] = rx[ts - 8:, :]
        _, xc, _, _, gi, _, a, _, mult = _lru_gates(ext, cw_ref, cb_ref, wr_ref, br_ref, wi_ref, bi_ref, lam_ref, ts, nh)
        acum, bcum = _scan_affine(a, mult * (gi * xc), False)
        h_ref[...] = bcum + acum * hcar[...]
        hcar[...] = h_ref[pl.ds(ts - 1, 1), :]
        o_ref[...] = (h_ref[...] * _gelu_parts(ry_ref[...].astype(F32))[0]).astype(BF16)

    row = pl.BlockSpec((ts, bw), lambda i: (i, 0))
    return pl.pallas_call(
        body, name=name, grid=(s // ts,),
        in_specs=[pl.BlockSpec((ts, bw), lambda i: (i, RX)), pl.BlockSpec((ts, bw), lambda i: (i, RY))]
        + _lru_param_specs(bw, nh),
        out_specs=[row, row],
        out_shape=[jax.ShapeDtypeStruct((s, bw), BF16), jax.ShapeDtypeStruct((s, bw), F32)],
        scratch_shapes=[pltpu.VMEM((8, bw), F32), pltpu.VMEM((1, bw), F32)],
        compiler_params=_cp(("arbitrary",)))(u, u, *params)


def _lru_bwd(name, u, h, do, params, bw):
    s, nh = u.shape[0], bw // HEAD
    ts = min(512, s)
    nb = s // ts
    t8 = ts // 8

    def body(rx_ref, rxp_ref, ry_ref, h_ref, hp_ref, do_ref, cw_ref, cb_ref, wr_ref, br_ref, wi_ref, bi_ref, lam_ref,
             drx_ref, dry_ref, dcw_ref, dcb_ref, dwr_ref, dbr_ref, dwi_ref, dbi_ref, dlam_ref, gcar, head):
        i = pl.program_id(0)
        first = i == nb - 1

        @pl.when(i == 0)
        def _():
            gcar[...] = jnp.zeros(gcar.shape, F32)
            head[...] = jnp.zeros(head.shape, F32)
            for ref in (dcw_ref, dcb_ref, dwr_ref, dbr_ref, dwi_ref, dbi_ref, dlam_ref):
                ref[...] = jnp.zeros(ref.shape, F32)

        rows = _iota((ts, 1), 0)
        rx = rx_ref[...].astype(F32)
        before = jnp.where(first, 0.0, rxp_ref[...].astype(F32))
        ext = jnp.concatenate([before, rx], axis=0)
        shifted, xc, xcb, r, gi, lsl, a, e2, mult = _lru_gates(
            ext, cw_ref, cb_ref, wr_ref, br_ref, wi_ref, bi_ref, lam_ref, ts, nh)

        ry = ry_ref[...].astype(F32)
        gel, th = _gelu_parts(ry)
        dgel = 0.5 * (1.0 + th) + 0.5 * ry * (1.0 - th * th) * GELU_K * (1.0 + 3.0 * GELU_C * ry * ry)
        dov = do_ref[...].astype(F32)
        hv = h_ref[...]
        dry_ref[...] = (dov * hv * dgel).astype(BF16)

        coef = jnp.where(rows < ts - 1, pltpu.roll(a, ts - 1, 0), 0.0)
        dh_in = dov * gel + jnp.where(rows == ts - 1, gcar[...], 0.0)
        dh = _scan_affine(coef, dh_in, True)[1]
        gcar[...] = jnp.sum(jnp.where(rows == 0, a * dh, 0.0), axis=0, keepdims=True)

        hprev = jnp.where(first, 0.0, hp_ref[...])
        hm1 = pltpu.roll(jnp.concatenate([hprev, hv], axis=0), 1, 0)[8:, :]
        dgx = dh * mult
        dla = dh * hm1 * a - dh * gi * xc * (e2 / mult)
        dpre_r = dla * (LRU_C * lsl) * r * (1.0 - r)
        dpre_i = dgx * xc * gi * (1.0 - gi)
        dlam_ref[...] += jnp.sum(dla * r, axis=0, keepdims=True) * (LRU_C * _sigmoid(-lam_ref[...]))
        dbr_ref[...] += jnp.sum(dpre_r, axis=0, keepdims=True)
        dbi_ref[...] += jnp.sum(dpre_i, axis=0, keepdims=True)
        drb, dib = dpre_r.astype(BF16), dpre_i.astype(BF16)
        for n in range(nh):
            cols = slice(n * HEAD, (n + 1) * HEAD)
            dwr_ref[n] += lax.dot_general(xcb[:, cols], drb[:, cols], TN, preferred_element_type=F32)
            dwi_ref[n] += lax.dot_general(xcb[:, cols], dib[:, cols], TN, preferred_element_type=F32)
        dxc = dgx * gi + _block_diag(drb, wr_ref, nh, NT) + _block_diag(dib, wi_ref, nh, NT)

        dcb_ref[...] += jnp.sum(dxc, axis=0, keepdims=True)
        for j in range(CONV_WIDTH):
            dcw_ref[pl.ds(j, 1), :] += jnp.sum(dxc * shifted[j], axis=0, keepdims=True)
        ext2 = jnp.concatenate([dxc, head[...]], axis=0)
        head[...] = dxc[:8, :]
        drx = dxc * cw_ref[pl.ds(CONV_WIDTH - 1, 1), :]
        for j in range(CONV_WIDTH - 1):
            up = CONV_WIDTH - 1 - j
            drx = drx + pltpu.roll(ext2, ts + 8 - up, 0)[:ts, :] * cw_ref[pl.ds(j, 1), :]
        drx_ref[...] = drx.astype(BF16)

    def blk(col):
        return lambda i: (nb - 1 - i, col)

    def prev8(col):
        return lambda i: (jnp.maximum((nb - 1 - i) * t8 - 1, 0), col)

    vec = pl.BlockSpec((1, bw), lambda i: (0, 0))
    conv = pl.BlockSpec((8, bw), lambda i: (0, 0))
    blocks = pl.BlockSpec((nh, HEAD, HEAD), lambda i: (0, 0, 0))
    return pl.pallas_call(
        body, name=name, grid=(nb,),
        in_specs=[pl.BlockSpec((ts, bw), blk(RX)), pl.BlockSpec((8, bw), prev8(RX)), pl.BlockSpec((ts, bw), blk(RY)),
                  pl.BlockSpec((ts, bw), blk(0)), pl.BlockSpec((8, bw), prev8(0)), pl.BlockSpec((ts, bw), blk(0))]
        + _lru_param_specs(bw, nh),
        out_specs=[pl.BlockSpec((ts, bw), blk(0)), pl.BlockSpec((ts, bw), blk(0)), conv, vec, blocks, vec, blocks, vec, vec],
        out_shape=[jax.ShapeDtypeStruct((s, bw), BF16)] * 2
        + [jax.ShapeDtypeStruct((8, bw), F32), jax.ShapeDtypeStruct((1, bw), F32),
           jax.ShapeDtypeStruct((nh, HEAD, HEAD), F32), jax.ShapeDtypeStruct((1, bw), F32),
           jax.ShapeDtypeStruct((nh, HEAD, HEAD), F32), jax.ShapeDtypeStruct((1, bw), F32),
           jax.ShapeDtypeStruct((1, bw), F32)],
        scratch_shapes=[pltpu.VMEM((1, bw), F32), pltpu.VMEM((8, bw), F32)],
        compiler_params=_cp(("arbitrary",)))(u, u, u, h, h, do, *params)


def _gate_merge(name, xb, w_gate, b_gate, o_all, w_branch, l):
    s, d = xb.shape
    bw = o_all.shape[2]
    tm, tn = min(512, s), min(256, d)

    def body(x_ref, wg_ref, bg_ref, o_ref, wb_ref, m_ref, g_ref, p_ref):
        x = x_ref[...]
        acc = jnp.zeros((tm, tn), F32)
        for g in range(4):
            gate = _sigmoid(jnp.dot(x, wg_ref[g], preferred_element_type=F32) + bg_ref[g])
            proj = jnp.dot(o_ref[g], wb_ref[g], preferred_element_type=F32)
            g_ref[g] = gate
            p_ref[g] = proj.astype(BF16)
            acc = acc + gate * proj
        m_ref[...] = acc.astype(BF16)

    quad = pl.BlockSpec((4, tm, tn), lambda n, m: (0, m, n))
    return pl.pallas_call(
        body, name=name, grid=(d // tn, s // tm),
        in_specs=[pl.BlockSpec((tm, d), lambda n, m: (m, 0)),
                  pl.BlockSpec((None, 4, d, tn), lambda n, m: (l, 0, 0, n)),
                  pl.BlockSpec((None, 4, 1, tn), lambda n, m: (l, 0, 0, n)),
                  pl.BlockSpec((4, tm, bw), lambda n, m: (0, m, 0)),
                  pl.BlockSpec((None, 4, bw, tn), lambda n, m: (l, 0, 0, n))],
        out_specs=[pl.BlockSpec((tm, tn), lambda n, m: (m, n)), quad, quad],
        out_shape=[jax.ShapeDtypeStruct((s, d), BF16), jax.ShapeDtypeStruct((4, s, d), F32),
                   jax.ShapeDtypeStruct((4, s, d), BF16)],
        compiler_params=_cp(("parallel", "parallel")))(xb, w_gate, b_gate, o_all, w_branch)


def _adamw(name, w, m, v, parts):
    rows, cols = w.shape
    p = parts.shape[0]
    tr = _pow2_rows(rows, cols * max(1, p // 2), 131072)
    c1 = 1.0 - ADAM_B1 ** ADAM_STEP
    c2 = 1.0 - ADAM_B2 ** ADAM_STEP

    def body(w_ref, m_ref, v_ref, g_ref, go_ref, do_ref, mo_ref, vo_ref):
        g = g_ref[0].astype(F32)
        for k in range(1, p):
            g = g + g_ref[k].astype(F32)
        m2 = ADAM_B1 * m_ref[...] + (1.0 - ADAM_B1) * g
        v2 = ADAM_B2 * v_ref[...] + (1.0 - ADAM_B2) * (g * g)
        go_ref[...] = g
        do_ref[...] = -ADAM_LR * ((m2 / c1) / (jnp.sqrt(v2 / c2) + ADAM_EPS) + ADAM_WD * w_ref[...])
        mo_ref[...] = m2
        vo_ref[...] = v2

    row = pl.BlockSpec((tr, cols), lambda i: (i, 0))
    return pl.pallas_call(
        body, name=name, grid=(rows // tr,),
        in_specs=[row, row, row, pl.BlockSpec((p, tr, cols), lambda i: (0, i, 0))], out_specs=[row] * 4,
        out_shape=[jax.ShapeDtypeStruct((rows, cols), F32)] * 4,
        compiler_params=_cp(("parallel",)))(w, m, v, parts)


def _pack(arrays):
    flat = jnp.concatenate([a.astype(F32).reshape(-1) for a in arrays])
    pad = (-flat.shape[0]) % (8 * LANE)
    return jnp.pad(flat, (0, pad)).reshape(-1, LANE)


def _unpack(packed, shapes):
    flat, out, off = packed.reshape(-1), [], 0
    for shp in shapes:
        n = math.prod(shp)
        out.append(flat[off:off + n].reshape(shp))
        off += n
    return out


def _unshard(gathered, axis):
    block = gathered.shape[2:]
    full = jnp.swapaxes(gathered, 0, 1).reshape((N_DEV,) + block)
    full = jnp.moveaxis(full, 0, axis)
    return full.reshape(block[:axis] + (N_DEV * block[axis],) + block[axis + 1:])


def kernel(x, ln_in_g, ln_in_b, w_in, b_forget, conv_w, conv_b, w_r, b_r, w_i, b_i, lru_lambda, rel_bias, w_branch, w_gate, b_gate, w_out, ln1_g, ln1_b, w_ff1, w_ff2, ln2_g, ln2_b, loss_target, m_ln_in_g, m_ln_in_b, m_w_in, m_b_forget, m_conv_w, m_conv_b, m_w_r, m_b_r, m_w_i, m_b_i, m_lru_lambda, m_rel_bias, m_w_branch, m_w_gate, m_b_gate, m_w_out, m_ln1_g, m_ln1_b, m_w_ff1, m_w_ff2, m_ln2_g, m_ln2_b, v_ln_in_g, v_ln_in_b, v_w_in, v_b_forget, v_conv_w, v_conv_b, v_w_r, v_b_r, v_w_i, v_b_i, v_lru_lambda, v_rel_bias, v_w_branch, v_w_gate, v_b_gate, v_w_out, v_ln1_g, v_ln1_b, v_w_ff1, v_w_ff2, v_ln2_g, v_ln2_b):
    given = dict(zip(
        NAMES + ['loss_target'] + ['m_' + n for n in WEIGHTS] + ['v_' + n for n in WEIGHTS],
        (x, ln_in_g, ln_in_b, w_in, b_forget, conv_w, conv_b, w_r, b_r, w_i, b_i, lru_lambda, rel_bias, w_branch, w_gate, b_gate, w_out, ln1_g, ln1_b, w_ff1, w_ff2, ln2_g, ln2_b, loss_target, m_ln_in_g, m_ln_in_b, m_w_in, m_b_forget, m_conv_w, m_conv_b, m_w_r, m_b_r, m_w_i, m_b_i, m_lru_lambda, m_rel_bias, m_w_branch, m_w_gate, m_b_gate, m_w_out, m_ln1_g, m_ln1_b, m_w_ff1, m_w_ff2, m_ln2_g, m_ln2_b, v_ln_in_g, v_ln_in_b, v_w_in, v_b_forget, v_conv_w, v_conv_b, v_w_r, v_b_r, v_w_i, v_b_i, v_lru_lambda, v_rel_bias, v_w_branch, v_w_gate, v_b_gate, v_w_out, v_ln1_g, v_ln1_b, v_w_ff1, v_w_ff2, v_ln2_g, v_ln2_b)))

    s, d = x.shape[1], x.shape[2]
    nl = w_in.shape[0]
    bw = d // 4
    nh = bw // HEAD
    nu = 11 * bw
    rs = d // N_DEV
    dff = w_ff1.shape[2] * N_DEV
    fs = dff // N_DEV
    cs = d // N_DEV
    assert nl == DEPTH and nh * HEAD == bw and s % 256 == 0 and d % 1024 == 0

    xi, yi, ci = _position()
    dev = 4 * xi + 2 * yi + ci
    c_arr = jnp.reshape(ci, (1,)).astype(I32)

    w_main = jnp.concatenate([w_in[..., :3 * bw], w_in[..., 3 * bw + nh:]], axis=-1).astype(BF16)
    w_fcol = jnp.pad(w_in[..., 3 * bw:3 * bw + nh], ((0, 0), (0, 0), (0, LANE - nh))).astype(BF16)
    small_shapes = [conv_w.shape, rel_bias.shape, b_gate.shape]
    shards = [w_main, w_fcol, w_branch.astype(BF16), w_gate.astype(BF16), w_out.astype(BF16),
              w_ff1.astype(BF16), w_ff2.astype(BF16), _pack([conv_w, rel_bias, b_gate])]
    gathered = _exchange_cores("gather_cores", _exchange_chips("gather_chips", shards, True), True)
    W_main, W_f = _unshard(gathered[0], 1), _unshard(gathered[1], 1)
    W_branch, W_gate, W_out = _unshard(gathered[2], 3), _unshard(gathered[3], 2), _unshard(gathered[4], 1)
    W_ff1, W_ff2 = _unshard(gathered[5], 2), _unshard(gathered[6], 1)
    small = jnp.swapaxes(gathered[7], 0, 1).reshape((N_DEV,) + gathered[7].shape[2:])
    small = [_unpack(small[j], small_shapes) for j in range(N_DEV)]
    conv_w_full = jnp.concatenate([small[j][0] for j in range(N_DEV)], axis=-1)
    rel_bias_full = jnp.concatenate([small[j][1] for j in range(N_DEV)], axis=-1)
    b_gate_full = jnp.concatenate([small[j][2] for j in range(N_DEV)], axis=-1)
    b_gate4 = b_gate_full.reshape(nl, 4, 1, d)

    def lru_params(l):
        return (jnp.pad(conv_w_full[l], ((0, 8 - CONV_WIDTH), (0, 0))), conv_b[l].reshape(1, bw),
                w_r[l].astype(BF16), b_r[l].reshape(1, bw), w_i[l].astype(BF16), b_i[l].reshape(1, bw),
                lru_lambda[l].reshape(1, bw))

    def bias_rows(l):
        return jnp.pad(rel_bias_full[l], ((0, 16 - nh), (0, REL_PAD - REL_TABLE)))

    tm = min(1024, s)
    tkk = min(512, d)

    xs = x[0]
    h0, h0b = _ln_fwd("ln_in", xs, ln_in_g, ln_in_b)
    saved = []
    cur, curb = h0, h0b
    for l in range(nl):
        u = _mm(f"w_in_{l}", curb, W_main, grid=(s // tm, nu // bw, d // tkk),
                a_spec=pl.BlockSpec((tm, tkk), lambda m, n, k: (m, k)),
                b_spec=pl.BlockSpec((None, tkk, bw), lambda m, n, k, l=l: (l, k, n)),
                dims=NN, acc_shape=(tm, bw), out_shape=[jax.ShapeDtypeStruct((s, nu), BF16)],
                out_specs=[pl.BlockSpec((tm, bw), lambda m, n, k: (m, n))], finish=_store(BF16))[0]
        fl = _mm(f"w_forget_{l}", curb, W_f, grid=(s // tm, 1, d // tkk),
                 a_spec=pl.BlockSpec((tm, tkk), lambda m, n, k: (m, k)),
                 b_spec=pl.BlockSpec((None, tkk, LANE), lambda m, n, k, l=l: (l, k, 0)),
                 dims=NN, acc_shape=(tm, LANE), out_shape=[jax.ShapeDtypeStruct((s, LANE), F32)],
                 out_specs=[pl.BlockSpec((tm, LANE), lambda m, n, k: (m, 0))], finish=_store(F32))[0]
        bf_row = jnp.pad(b_forget[l], (0, LANE - nh)).reshape(1, LANE)
        cf = _cum_forget_fwd(f"cum_forget_{l}", fl, bf_row)
        tkf = min(256, s)
        cf_heads = cf[:, :nh].T
        cf_col = cf_heads.reshape(nh, s, 1)
        cf_row = cf_heads.reshape(nh, s // tkf, 1, tkf)
        o_fox, lse = _fox_fwd(f"fox_fwd_{l}", u, cf_col, cf_row, bw)
        lp = lru_params(l)
        o_lru, hstate = _lru_fwd(f"lru_fwd_{l}", u, lp, bw)
        o_sb = _sb_fwd(f"sb_fwd_{l}", u, bw)
        bias = jnp.transpose(_bias_expand(f"bias_expand_{l}", bias_rows(l)), (1, 0, 2))[:nh]
        kpad = jnp.pad(u[:, CK * bw:(CK + 1) * bw], ((PADK, 0), (0, 0)))
        vpad = jnp.pad(u[:, CV * bw:(CV + 1) * bw], ((PADK, 0), (0, 0)))
        o_ch = _chunk_fwd(f"chunk_fwd_{l}", u, kpad, vpad, bias, bw)
        o_all = jnp.stack([o_fox, o_lru, o_sb, o_ch])
        merged, gates, projs = _gate_merge(f"gate_merge_{l}", curb, W_gate, b_gate4, o_all, W_branch, l)
        z1, x1, x1b = _mm_ln(f"w_out_ln1_{l}", merged, W_out, l, cur, ln1_g[l], ln1_b[l])
        tn1 = min(1024, dff)

        def ff1_finish(acc, ex, outs, ids):
            outs[0][...] = acc.astype(BF16)
            r = jnp.maximum(acc, 0.0)
            outs[1][...] = (r * r).astype(BF16)

        hp, hid = _mm(f"w_ff1_{l}", x1b, W_ff1, grid=(s // tm, dff // tn1, d // tkk),
                      a_spec=pl.BlockSpec((tm, tkk), lambda m, n, k: (m, k)),
                      b_spec=pl.BlockSpec((None, tkk, tn1), lambda m, n, k, l=l: (l, k, n)),
                      dims=NN, acc_shape=(tm, tn1),
                      out_shape=[jax.ShapeDtypeStruct((s, dff), BF16)] * 2,
                      out_specs=[pl.BlockSpec((tm, tn1), lambda m, n, k: (m, n))] * 2, finish=ff1_finish)
        z2, x2, x2b = _mm_ln(f"w_ff2_ln2_{l}", hid, W_ff2, l, x1, ln2_g[l], ln2_b[l])
        saved.append(dict(xin=cur, xinb=curb, u=u, fl=fl, bf_row=bf_row, cf_col=cf_col, cf_row=cf_row, o_fox=o_fox,
                          lse=lse, lp=lp, hstate=hstate, bias=bias, kpad=kpad, vpad=vpad, o_all=o_all, merged=merged,
                          gates=gates, projs=projs, z1=z1, x1=x1, x1b=x1b, hp=hp, hid=hid, z2=z2))
        cur, curb = x2, x2b

    loss_tile, dcur = _loss_head("loss_head", cur, loss_target[0])
    loss = lax.psum(loss_tile[0, 0], ("x", "y", "c"))

    big = {}
    sm = {n: [None] * nl for n in ['b_forget', 'conv_w', 'conv_b', 'w_r', 'b_r', 'w_i', 'b_i', 'lru_lambda', 'rel_bias',
                                   'b_gate', 'ln1_g', 'ln1_b', 'ln2_g', 'ln2_b']}

    def grad_mm(key, name, a, b, *, shape, grid, a_spec, b_spec, out_spec, acc_shape, sem=("parallel", "parallel", "arbitrary")):
        has = key in big
        extras = (big[key],) if has else ()
        res = _mm(name, a, b, grid=grid, a_spec=a_spec, b_spec=b_spec, dims=TN, acc_shape=acc_shape,
                  out_shape=[jax.ShapeDtypeStruct(shape, F32)], out_specs=[out_spec], finish=_store(F32),
                  extras=extras, extra_specs=(ANY,) * len(extras), aliases={2: 0} if has else None, sem=sem)[0]
        big[key] = res

    tks = min(512, s)
    tmr = min(1024, d)
    nsh = tmr // rs

    for l in reversed(range(nl)):
        sv = saved[l]
        dz2, dz2b, dg, db = _ln_bwd(f"ln2_bwd_{l}", dcur, sv['z2'], ln2_g[l])
        sm['ln2_g'][l], sm['ln2_b'][l] = dg[0], db[0]
        tn1 = min(1024, dff)

        def dhp_finish(acc, ex, outs, ids):
            outs[0][...] = (acc * (2.0 * jnp.maximum(ex[0][...].astype(F32), 0.0))).astype(BF16)

        dhp = _mm(f"d_hidden_{l}", dz2b, W_ff2, grid=(s // tm, dff // tn1, d // tkk),
                  a_spec=pl.BlockSpec((tm, tkk), lambda m, n, k: (m, k)),
                  b_spec=pl.BlockSpec((None, tn1, tkk), lambda m, n, k, l=l: (l, n, k)),
                  dims=NT, acc_shape=(tm, tn1), out_shape=[jax.ShapeDtypeStruct((s, dff), BF16)],
                  out_specs=[pl.BlockSpec((tm, tn1), lambda m, n, k: (m, n))], finish=dhp_finish,
                  extras=(sv['hp'],), extra_specs=(pl.BlockSpec((tm, tn1), lambda m, n, k: (m, n)),))[0]
        grad_mm('w_ff2', f"g_w_ff2_{l}", sv['hid'], dz2b, shape=(N_DEV, nl, fs, d), grid=(N_DEV, 1, s // tks),
                a_spec=pl.BlockSpec((tks, fs), lambda m, n, k: (k, m)),
                b_spec=pl.BlockSpec((tks, d), lambda m, n, k: (k, 0)),
                out_spec=pl.BlockSpec((None, None, fs, d), lambda m, n, k, l=l: (m, l, 0, 0)), acc_shape=(fs, d))
        grad_mm('w_ff1', f"g_w_ff1_{l}", sv['x1b'], dhp, shape=(N_DEV, nl, d, fs), grid=(d // tmr, N_DEV, s // tks),
                a_spec=pl.BlockSpec((tks, tmr), lambda m, n, k: (k, m)),
                b_spec=pl.BlockSpec((tks, fs), lambda m, n, k: (k, n)),
                out_spec=pl.BlockSpec((None, None, tmr, fs), lambda m, n, k, l=l: (n, l, m, 0)), acc_shape=(tmr, fs))
        tnd = min(1024, d)

        def resid_finish(scale):
            def finish(acc, ex, outs, ids):
                outs[0][...] = acc + scale * ex[0][...]
            return finish

        tile_md = pl.BlockSpec((tm, tnd), lambda m, n, k: (m, n))
        dx1 = _mm(f"d_x1_{l}", dhp, W_ff1, grid=(s // tm, d // tnd, dff // tkk),
                  a_spec=pl.BlockSpec((tm, tkk), lambda m, n, k: (m, k)),
                  b_spec=pl.BlockSpec((None, tnd, tkk), lambda m, n, k, l=l: (l, n, k)),
                  dims=NT, acc_shape=(tm, tnd), out_shape=[jax.ShapeDtypeStruct((s, d), F32)],
                  out_specs=[tile_md], finish=resid_finish(ALPHA), extras=(dz2,), extra_specs=(tile_md,))[0]

        dz1, dz1b, dg, db = _ln_bwd(f"ln1_bwd_{l}", dx1, sv['z1'], ln1_g[l])
        sm['ln1_g'][l], sm['ln1_b'][l] = dg[0], db[0]
        tmg, tng = min(512, s), min(512, d)

        def gate_finish(acc, ex, outs, ids):
            @pl.when(ids[1] == 0)
            def _():
                outs[2][...] = jnp.zeros(outs[2].shape, F32)

            for g in range(4):
                gate = ex[0][g]
                dproj = acc * gate
                dpre = acc * ex[1][g].astype(F32) * gate * (1.0 - gate)
                outs[0][g] = dproj.astype(BF16)
                outs[1][g] = dpre.astype(BF16)
                outs[2][g] += jnp.sum(dpre, axis=0, keepdims=True)

        quad = pl.BlockSpec((4, tmg, tng), lambda n, m, k: (0, m, n))
        dproj, dpre, dbg = _mm(
            f"d_merged_{l}", dz1b, W_out, grid=(d // tng, s // tmg, d // tkk),
            a_spec=pl.BlockSpec((tmg, tkk), lambda n, m, k: (m, k)),
            b_spec=pl.BlockSpec((None, tng, tkk), lambda n, m, k, l=l: (l, n, k)),
            dims=NT, acc_shape=(tmg, tng),
            out_shape=[jax.ShapeDtypeStruct((4, s, d), BF16), jax.ShapeDtypeStruct((4, s, d), BF16),
                       jax.ShapeDtypeStruct((4, 1, d), F32)],
            out_specs=[quad, quad, pl.BlockSpec((4, 1, tng), lambda n, m, k: (0, 0, n))], finish=gate_finish,
            extras=(sv['gates'], sv['projs']), extra_specs=(quad, quad), sem=("arbitrary", "arbitrary", "arbitrary"))
        sm['b_gate'][l] = dbg.reshape(4, d)
        grad_mm('w_out', f"g_w_out_{l}", sv['merged'], dz1b, shape=(N_DEV, nl, rs, d), grid=(d // tmr, 1, s // tks),
                a_spec=pl.BlockSpec((tks, tmr), lambda m, n, k: (k, m)),
                b_spec=pl.BlockSpec((tks, d), lambda m, n, k: (k, 0)),
                out_spec=pl.BlockSpec((nsh, None, rs, d), lambda m, n, k, l=l: (m, l, 0, 0)), acc_shape=(tmr, d))

        nm = s // tm
        do_all = _mm(f"d_branch_{l}", dproj, W_branch, grid=(4 * nm, 1, d // tkk),
                     a_spec=pl.BlockSpec((None, tm, tkk), lambda m, n, k: (m // nm, m % nm, k)),
                     b_spec=pl.BlockSpec((None, None, bw, tkk), lambda m, n, k, l=l: (l, m // nm, 0, k)),
                     dims=NT, acc_shape=(tm, bw), out_shape=[jax.ShapeDtypeStruct((4, s, bw), BF16)],
                     out_specs=[pl.BlockSpec((None, tm, bw), lambda m, n, k: (m // nm, m % nm, 0))],
                     finish=_store(BF16))[0]
        grad_mm('w_branch', f"g_w_branch_{l}", sv['o_all'], dproj, shape=(N_DEV, nl, 4, bw, cs),
                grid=(4, N_DEV, s // tks),
                a_spec=pl.BlockSpec((None, tks, bw), lambda m, n, k: (m, k, 0)),
                b_spec=pl.BlockSpec((None, tks, cs), lambda m, n, k: (m, k, n)),
                out_spec=pl.BlockSpec((None, None, None, bw, cs), lambda m, n, k, l=l: (n, l, m, 0, 0)),
                acc_shape=(bw, cs))
        grad_mm('w_gate', f"g_w_gate_{l}", sv['xinb'], dpre, shape=(N_DEV, nl, 4, rs, d), grid=(d // tmr, 4, s // tks),
                a_spec=pl.BlockSpec((tks, tmr), lambda m, n, k: (k, m)),
                b_spec=pl.BlockSpec((None, tks, d), lambda m, n, k: (n, k, 0)),
                out_spec=pl.BlockSpec((nsh, None, None, rs, d), lambda m, n, k, l=l: (m, l, n, 0, 0)),
                acc_shape=(tmr, d))
        nkg = d // tkk
        dx_gate = _mm(f"d_x_gates_{l}", dpre, W_gate, grid=(s // tm, d // tnd, 4 * nkg),
                      a_spec=pl.BlockSpec((None, tm, tkk), lambda m, n, k: (k // nkg, m, k % nkg)),
                      b_spec=pl.BlockSpec((None, None, tnd, tkk), lambda m, n, k, l=l: (l, k // nkg, n, k % nkg)),
                      dims=NT, acc_shape=(tm, tnd), out_shape=[jax.ShapeDtypeStruct((s, d), F32)],
                      out_specs=[tile_md], finish=resid_finish(ALPHA), extras=(dz1,), extra_specs=(tile_md,))[0]

        u = sv['u']
        dfq, dfk, dfv, dcc, dcr = _fox_bwd(f"fox_bwd_{l}", u, sv['cf_col'], sv['cf_row'], sv['o_fox'], do_all[0],
                                          sv['lse'], bw)
        dcf = (dcc.reshape(nh, s) + dcr.reshape(nh, s)).T
        dflb, dbf = _cum_forget_bwd(f"cum_forget_bwd_{l}", jnp.pad(dcf, ((0, 0), (0, LANE - nh))), sv['fl'],
                                    sv['bf_row'])
        sm['b_forget'][l] = dbf[0, :nh]
        drx, dry, dcw, dcb, dwr, dbr, dwi, dbi, dlam = _lru_bwd(f"lru_bwd_{l}", u, sv['hstate'], do_all[1], sv['lp'], bw)
        sm['conv_w'][l], sm['conv_b'][l], sm['w_r'][l], sm['b_r'][l] = dcw[:CONV_WIDTH], dcb[0], dwr, dbr[0]
        sm['w_i'][l], sm['b_i'][l], sm['lru_lambda'][l] = dwi, dbi[0], dlam[0]
        dsq, dsk, dsv = _sb_bwd(f"sb_bwd_{l}", u, do_all[2], bw)
        dcq, dck, dcv, dss = _chunk_bwd(f"chunk_bwd_{l}", u, sv['kpad'], sv['vpad'], sv['bias'], do_all[3], bw)
        dss_rows = jnp.pad(jnp.transpose(dss, (1, 0, 2)), ((0, 0), (0, 16 - nh), (0, 0)))
        sm['rel_bias'][l] = _bias_reduce(f"bias_reduce_{l}", dss_rows)[:nh, :REL_TABLE]
        du = jnp.concatenate([dfq, dfk, dfv, drx, dry, dsq, dsk, dsv, dcq, dck, dcv], axis=1)

        tnu = 11 * LANE
        grad_mm('w_main', f"g_w_in_{l}", sv['xinb'], du, shape=(N_DEV, nl, rs, nu), grid=(d // tmr, nu // tnu, s // tks),
                a_spec=pl.BlockSpec((tks, tmr), lambda m, n, k: (k, m)),
                b_spec=pl.BlockSpec((tks, tnu), lambda m, n, k: (k, n)),
                out_spec=pl.BlockSpec((nsh, None, rs, tnu), lambda m, n, k, l=l: (m, l, 0, n)), acc_shape=(tmr, tnu))
        grad_mm('w_f', f"g_w_forget_{l}", sv['xinb'], dflb, shape=(N_DEV, nl, rs, LANE), grid=(d // tmr, 1, s // tks),
                a_spec=pl.BlockSpec((tks, tmr), lambda m, n, k: (k, m)),
                b_spec=pl.BlockSpec((tks, LANE), lambda m, n, k: (k, 0)),
                out_spec=pl.BlockSpec((nsh, None, rs, LANE), lambda m, n, k, l=l: (m, l, 0, 0)), acc_shape=(tmr, LANE))
        dxa = _mm(f"d_x_in_{l}", du, W_main, grid=(s // tm, d // tnd, nu // bw),
                  a_spec=pl.BlockSpec((tm, bw), lambda m, n, k: (m, k)),
                  b_spec=pl.BlockSpec((None, tnd, bw), lambda m, n, k, l=l: (l, n, k)),
                  dims=NT, acc_shape=(tm, tnd), out_shape=[jax.ShapeDtypeStruct((s, d), F32)],
                  out_specs=[tile_md], finish=resid_finish(1.0), extras=(dx_gate,), extra_specs=(tile_md,))[0]
        dcur = _mm(f"d_x_forget_{l}", dflb, W_f, grid=(s // tm, d // tnd, 1),
                   a_spec=pl.BlockSpec((tm, LANE), lambda m, n, k: (m, 0)),
                   b_spec=pl.BlockSpec((None, tnd, LANE), lambda m, n, k, l=l: (l, n, 0)),
                   dims=NT, acc_shape=(tm, tnd), out_shape=[jax.ShapeDtypeStruct((s, d), F32)],
                   out_specs=[tile_md], finish=resid_finish(1.0), extras=(dxa,), extra_specs=(tile_md,))[0]

    grad_x, _, dg_in, db_in = _ln_bwd("ln_in_bwd", dcur, xs, ln_in_g)

    keys = ['w_main', 'w_f', 'w_branch', 'w_gate', 'w_out', 'w_ff1', 'w_ff2']
    slabs = {k: big[k].shape[1:] for k in keys}
    halves = [big[k].reshape((4, 2) + slabs[k]) for k in keys]
    from_core = _exchange_cores("reduce_cores", halves, False)
    partial = []
    for k, mine, other in zip(keys, halves, from_core):
        cols = slabs[k][-1]
        rows = math.prod(slabs[k]) // cols
        partial.append(_add_core_halves(f"add_cores_{k}", mine.reshape(4, 2, rows, cols), other.reshape(4, rows, cols),
                                        c_arr, BF16))
    from_chips = dict(zip(keys, _exchange_chips("reduce_chips", partial, False)))

    small_names = ['ln_in_g', 'ln_in_b', 'b_forget', 'conv_w', 'conv_b', 'w_r', 'b_r', 'w_i', 'b_i', 'lru_lambda',
                   'rel_bias', 'b_gate', 'ln1_g', 'ln1_b', 'ln2_g', 'ln2_b']
    local = {'ln_in_g': dg_in[0], 'ln_in_b': db_in[0]}
    for n in small_names[2:]:
        local[n] = jnp.stack(sm[n])
    full_shapes = [local[n].shape for n in small_names]
    packed = _pack([local[n] for n in small_names])
    every = _exchange_cores("gather_small_cores", _exchange_chips("gather_small_chips", [packed], True), True)[0]
    every = jnp.swapaxes(every, 0, 1).reshape((N_DEV,) + packed.shape)
    total = dict(zip(small_names, _unpack(_sum_parts("sum_small", every), full_shapes)))
    for n, width in (('conv_w', bw // N_DEV), ('rel_bias', REL_TABLE // N_DEV), ('b_gate', cs)):
        total[n] = lax.dynamic_slice_in_dim(total[n], dev * width, width, axis=2)

    out = {}

    def update(n, parts, shape2d):
        res = _adamw(f"adamw_{n}", given[n].reshape(shape2d), given['m_' + n].reshape(shape2d),
                     given['v_' + n].reshape(shape2d), parts)
        out[n] = [r.reshape(given[n].shape) for r in res]

    pm, pf = from_chips['w_main'], from_chips['w_f']
    parts_in = jnp.concatenate([pm[..., :3 * bw], pf[..., :nh], pm[..., 3 * bw:]], axis=-1)
    update('w_in', parts_in.reshape(4, nl * rs, w_in.shape[2]), (nl * rs, w_in.shape[2]))
    update('w_branch', from_chips['w_branch'].reshape(4, nl * 4 * bw, cs), (nl * 4 * bw, cs))
    update('w_gate', from_chips['w_gate'].reshape(4, nl * 4 * rs, d), (nl * 4 * rs, d))
    update('w_out', from_chips['w_out'].reshape(4, nl * rs, d), (nl * rs, d))
    update('w_ff1', from_chips['w_ff1'].reshape(4, nl * d, fs), (nl * d, fs))
    update('w_ff2', from_chips['w_ff2'].reshape(4, nl * fs, d), (nl * fs, d))

    small_shapes2 = [given[n].shape for n in small_names]
    res = _adamw("adamw_small", _pack([given[n] for n in small_names]), _pack([given['m_' + n] for n in small_names]),
                 _pack([given['v_' + n] for n in small_names]), _pack([total[n] for n in small_names])[None])
    res = [_unpack(r, small_shapes2) for r in res]
    for j, n in enumerate(small_names):
        out[n] = [res[k][j] for k in range(4)]

    return (loss, grad_x[None], *[out[n][0] for n in WEIGHTS], *[out[n][1] for n in WEIGHTS],
            *[out[n][2] for n in WEIGHTS], *[out[n][3] for n in WEIGHTS])
```

```python
import functools
import math

import jax
import jax.numpy as jnp
from jax import lax
from jax.experimental import pallas as pl
from jax.experimental.pallas import tpu as pltpu

F32, BF16, I32 = jnp.float32, jnp.bfloat16, jnp.int32
MESH = pl.DeviceIdType.MESH
ANY = pl.BlockSpec(memory_space=pl.ANY)

LANE = 128
VMEM_LIMIT = 56 * 1024 * 1024
N_DEV = 8

HEAD = 128
CHUNK = 64
LOOKBACK = 8
BAND = (LOOKBACK + 1) * CHUNK
QBLK = 2 * CHUNK
WIN = BAND + CHUNK
PADK = LOOKBACK * CHUNK
REL_CLIP = 256
REL_TABLE = REL_CLIP + CHUNK
REL_PAD = 384
CONV_WIDTH = 4
LRU_C = 8.0
LN_EPS = 1e-5
DEPTH = 2
ALPHA = (2.0 * DEPTH) ** 0.25
NEG = -1e30
SB_DEAD = -104.0
GELU_K = math.sqrt(2.0 / math.pi)
GELU_C = 0.044715

ADAM_LR, ADAM_B1, ADAM_B2, ADAM_EPS, ADAM_WD, ADAM_STEP = 0.001, 0.9, 0.999, 1e-08, 0.01, 10

NN = (((1,), (0,)), ((), ()))
NT = (((1,), (1,)), ((), ()))
TN = (((0,), (0,)), ((), ()))

FQ, FK, FV, RX, RY, SQ, SK, SV, CQ, CK, CV = range(11)

NAMES = ['x', 'ln_in_g', 'ln_in_b', 'w_in', 'b_forget', 'conv_w', 'conv_b', 'w_r', 'b_r', 'w_i', 'b_i', 'lru_lambda',
         'rel_bias', 'w_branch', 'w_gate', 'b_gate', 'w_out', 'ln1_g', 'ln1_b', 'w_ff1', 'w_ff2', 'ln2_g', 'ln2_b']
WEIGHTS = NAMES[1:]


def _cp(sem=None):
    return pltpu.CompilerParams(dimension_semantics=sem, vmem_limit_bytes=VMEM_LIMIT)


def _iota(shape, dim):
    return lax.broadcasted_iota(I32, shape, dim)


def _sigmoid(x):
    return 1.0 / (1.0 + jnp.exp(-x))


def _log_sigmoid(x):
    return jnp.minimum(x, 0.0) - jnp.log(1.0 + jnp.exp(-jnp.abs(x)))


def _pow2_rows(rows, cols, elems=262144):
    t = 8
    while t * 2 <= rows and t * 2 * cols <= elems and rows % (t * 2) == 0:
        t *= 2
    return t


def _position():
    return lax.axis_index("x"), lax.axis_index("y"), lax.axis_index("c")


def _exchange_chips(name, xs, gather):
    n = len(xs)

    def body(*refs):
        ins, outs = refs[:n], refs[n:2 * n]
        send_sems, recv_sems, local_sems = refs[2 * n:]
        x, y, c = _position()
        q = 2 * x + y
        chips = [(1 - x, y), (x, 1 - y), (1 - x, 1 - y)]

        def src(t, slot):
            return ins[t] if gather else ins[t].at[slot]

        def dst(t, slot):
            return outs[t].at[c, slot] if gather else outs[t].at[slot]

        local = [pltpu.make_async_copy(src(t, q), dst(t, q), local_sems.at[t]) for t in range(n)]
        for cp in local:
            cp.start()

        def remote(t, j, landing):
            px, py = chips[j]
            return pltpu.make_async_remote_copy(
                src_ref=src(t, 2 * px + py), dst_ref=dst(t, landing), send_sem=send_sems.at[t, j],
                recv_sem=recv_sems.at[t, j], device_id=(px, py, c), device_id_type=MESH)

        sends = [remote(t, j, q) for t in range(n) for j in range(3)]
        for cp in sends:
            cp.start()
        for t in range(n):
            for j, (px, py) in enumerate(chips):
                remote(t, j, 2 * px + py).wait_recv()
        for cp in sends:
            cp.wait_send()
        for cp in local:
            cp.wait()

    out_shape = [jax.ShapeDtypeStruct((2, 4) + a.shape if gather else a.shape, a.dtype) for a in xs]
    return pl.pallas_call(
        body, name=name, out_shape=out_shape, in_specs=[ANY] * n, out_specs=[ANY] * n,
        scratch_shapes=[pltpu.SemaphoreType.DMA((n, 3)), pltpu.SemaphoreType.DMA((n, 3)),
                        pltpu.SemaphoreType.DMA((n,))],
    )(*xs)


def _exchange_cores(name, xs, gather):
    n = len(xs)
    m = 1 if gather else 4

    def body(*refs):
        ins, outs = refs[:n], refs[n:2 * n]
        send_sems, recv_sems = refs[2 * n:]
        x, y, c = _position()
        sibling = (x, y, 1 - c)

        def remote(t, j, landing):
            s = outs[t].at[c] if gather else ins[t].at[j, 1 - c]
            d = outs[t].at[landing] if gather else outs[t].at[j]
            return pltpu.make_async_remote_copy(
                src_ref=s, dst_ref=d, send_sem=send_sems.at[t, j], recv_sem=recv_sems.at[t, j],
                device_id=sibling, device_id_type=MESH)

        sends = [remote(t, j, c) for t in range(n) for j in range(m)]
        for cp in sends:
            cp.start()
        for t in range(n):
            for j in range(m):
                remote(t, j, 1 - c).wait_recv()
        for cp in sends:
            cp.wait_send()

    if gather:
        out_shape = [jax.ShapeDtypeStruct(a.shape, a.dtype) for a in xs]
    else:
        out_shape = [jax.ShapeDtypeStruct((4,) + a.shape[2:], a.dtype) for a in xs]
    return pl.pallas_call(
        body, name=name, out_shape=out_shape, in_specs=[ANY] * n, out_specs=[ANY] * n,
        input_output_aliases={t: t for t in range(n)} if gather else {},
        scratch_shapes=[pltpu.SemaphoreType.DMA((n, m)), pltpu.SemaphoreType.DMA((n, m))],
    )(*xs)


def _add_core_halves(name, mine, other, c, out_dtype):
    _, _, rows, cols = mine.shape
    tr = _pow2_rows(rows, cols)

    def body(c_ref, a_ref, b_ref, o_ref):
        o_ref[...] = (a_ref[...] + b_ref[...]).astype(out_dtype)

    grid_spec = pltpu.PrefetchScalarGridSpec(
        num_scalar_prefetch=1, grid=(4, rows // tr),
        in_specs=[pl.BlockSpec((None, None, tr, cols), lambda j, i, c_ref: (j, c_ref[0], i, 0)),
                  pl.BlockSpec((None, tr, cols), lambda j, i, c_ref: (j, i, 0))],
        out_specs=pl.BlockSpec((None, tr, cols), lambda j, i, c_ref: (j, i, 0)))
    return pl.pallas_call(body, name=name, grid_spec=grid_spec,
                          out_shape=jax.ShapeDtypeStruct((4, rows, cols), out_dtype),
                          compiler_params=_cp(("parallel", "parallel")))(c, mine, other)


def _sum_parts(name, parts):
    p, rows, cols = parts.shape
    tr = _pow2_rows(rows, cols * p)

    def body(a_ref, o_ref):
        acc = a_ref[0]
        for k in range(1, p):
            acc = acc + a_ref[k]
        o_ref[...] = acc

    return pl.pallas_call(body, name=name, grid=(rows // tr,),
                          in_specs=[pl.BlockSpec((p, tr, cols), lambda i: (0, i, 0))],
                          out_specs=pl.BlockSpec((tr, cols), lambda i: (i, 0)),
                          out_shape=jax.ShapeDtypeStruct((rows, cols), F32),
                          compiler_params=_cp(("parallel",)))(parts)


def _mm(name, a, b, *, grid, a_spec, b_spec, dims, acc_shape, out_shape, out_specs, finish,
        extras=(), extra_specs=(), aliases=None, sem=("parallel", "parallel", "arbitrary")):
    nk, ne, no = grid[2], len(extras), len(out_shape)

    def body(*refs):
        a_ref, b_ref = refs[0], refs[1]
        ex, outs = refs[2:2 + ne], refs[2 + ne:2 + ne + no]
        ids = (pl.program_id(0), pl.program_id(1))
        def prod():
            return lax.dot_general(a_ref[...], b_ref[...], dims, preferred_element_type=F32)

        if nk == 1:
            finish(prod(), ex, outs, ids)
            return
        acc = refs[2 + ne + no]
        k = pl.program_id(2)

        @pl.when(k == 0)
        def _():
            acc[...] = prod()

        @pl.when(jnp.logical_and(k > 0, k < nk - 1))
        def _():
            acc[...] += prod()

        @pl.when(k == nk - 1)
        def _():
            finish(acc[...] + prod(), ex, outs, ids)

    return pl.pallas_call(
        body, name=name, grid=grid, in_specs=[a_spec, b_spec, *extra_specs], out_specs=out_specs,
        out_shape=out_shape, scratch_shapes=[pltpu.VMEM(acc_shape, F32)] if nk > 1 else [],
        input_output_aliases=aliases or {}, compiler_params=_cp(sem))(a, b, *extras)


def _store(dtype):
    def finish(acc, ex, outs, ids):
        outs[0][...] = acc.reshape(outs[0].shape).astype(dtype)
    return finish


def _layer_norm_rows(z, g, b):
    mu = jnp.mean(z, axis=1, keepdims=True)
    zc = z - mu
    var = jnp.mean(zc * zc, axis=1, keepdims=True)
    return zc * lax.rsqrt(var + LN_EPS) * g + b


def _mm_ln(name, a, w, l, resid, g, b):
    s, kdim = a.shape
    d = w.shape[2]
    tm, tk = min(512, s), min(1024, kdim)

    def finish(acc, ex, outs, ids):
        z = acc + ALPHA * ex[0][...]
        y = _layer_norm_rows(z, ex[1][...], ex[2][...])
        outs[0][...] = z
        outs[1][...] = y
        outs[2][...] = y.astype(BF16)

    row = pl.BlockSpec((tm, d), lambda m, n, k: (m, 0))
    vec = pl.BlockSpec((1, d), lambda m, n, k: (0, 0))
    return _mm(name, a, w, grid=(s // tm, 1, kdim // tk),
               a_spec=pl.BlockSpec((tm, tk), lambda m, n, k: (m, k)),
               b_spec=pl.BlockSpec((None, tk, d), lambda m, n, k: (l, k, 0)),
               dims=NN, acc_shape=(tm, d),
               out_shape=[jax.ShapeDtypeStruct((s, d), F32), jax.ShapeDtypeStruct((s, d), F32),
                          jax.ShapeDtypeStruct((s, d), BF16)],
               out_specs=[row, row, row], finish=finish,
               extras=(resid, g.reshape(1, d), b.reshape(1, d)), extra_specs=(row, vec, vec))


def _ln_fwd(name, x, g, b):
    s, d = x.shape
    tr = min(256, s)

    def body(x_ref, g_ref, b_ref, y_ref, yb_ref):
        y = _layer_norm_rows(x_ref[...], g_ref[...], b_ref[...])
        y_ref[...] = y
        yb_ref[...] = y.astype(BF16)

    row = pl.BlockSpec((tr, d), lambda i: (i, 0))
    vec = pl.BlockSpec((1, d), lambda i: (0, 0))
    return pl.pallas_call(body, name=name, grid=(s // tr,), in_specs=[row, vec, vec], out_specs=[row, row],
                          out_shape=[jax.ShapeDtypeStruct((s, d), F32), jax.ShapeDtypeStruct((s, d), BF16)],
                          compiler_params=_cp(("parallel",)))(x, g.reshape(1, d), b.reshape(1, d))


def _ln_bwd(name, dy, z, g):
    s, d = z.shape
    tr = min(256, s)

    def body(dy_ref, z_ref, g_ref, dz_ref, dzb_ref, dg_ref, db_ref):
        @pl.when(pl.program_id(0) == 0)
        def _():
            dg_ref[...] = jnp.zeros(dg_ref.shape, F32)
            db_ref[...] = jnp.zeros(db_ref.shape, F32)

        zz, dyv = z_ref[...], dy_ref[...]
        mu = jnp.mean(zz, axis=1, keepdims=True)
        zc = zz - mu
        rstd = lax.rsqrt(jnp.mean(zc * zc, axis=1, keepdims=True) + LN_EPS)
        xhat = zc * rstd
        dg_ref[...] += jnp.sum(dyv * xhat, axis=0, keepdims=True)
        db_ref[...] += jnp.sum(dyv, axis=0, keepdims=True)
        dxh = dyv * g_ref[...]
        dz = rstd * (dxh - jnp.mean(dxh, axis=1, keepdims=True) - xhat * jnp.mean(dxh * xhat, axis=1, keepdims=True))
        dz_ref[...] = dz
        dzb_ref[...] = dz.astype(BF16)

    row = pl.BlockSpec((tr, d), lambda i: (i, 0))
    vec = pl.BlockSpec((1, d), lambda i: (0, 0))
    return pl.pallas_call(
        body, name=name, grid=(s // tr,), in_specs=[row, row, vec], out_specs=[row, row, vec, vec],
        out_shape=[jax.ShapeDtypeStruct((s, d), F32), jax.ShapeDtypeStruct((s, d), BF16),
                   jax.ShapeDtypeStruct((1, d), F32), jax.ShapeDtypeStruct((1, d), F32)],
        compiler_params=_cp(("arbitrary",)))(dy, z, g.reshape(1, d))


def _loss_head(name, y, target):
    s, d = y.shape
    tr = min(256, s)

    def body(y_ref, t_ref, loss_ref, dy_ref):
        @pl.when(pl.program_id(0) == 0)
        def _():
            loss_ref[...] = jnp.zeros(loss_ref.shape, F32)

        e = y_ref[...] - t_ref[...]
        dy_ref[...] = e * (1.0 / d)
        loss_ref[...] += jnp.sum(e * e) * (0.5 / d)

    row = pl.BlockSpec((tr, d), lambda i: (i, 0))
    return pl.pallas_call(
        body, name=name, grid=(s // tr,), in_specs=[row, row],
        out_specs=[pl.BlockSpec((8, LANE), lambda i: (0, 0)), row],
        out_shape=[jax.ShapeDtypeStruct((8, LANE), F32), jax.ShapeDtypeStruct((s, d), F32)],
        compiler_params=_cp(("arbitrary",)))(y, target)


def _scan_add(x, reverse):
    ts = x.shape[0]
    rows = _iota((ts, 1), 0)
    dist = 1
    while dist < ts:
        if reverse:
            x = x + jnp.where(rows < ts - dist, pltpu.roll(x, ts - dist, 0), 0.0)
        else:
            x = x + jnp.where(rows >= dist, pltpu.roll(x, dist, 0), 0.0)
        dist *= 2
    return x


def _scan_affine(a, b, reverse):
    ts = a.shape[0]
    rows = _iota((ts, 1), 0)
    dist = 1
    while dist < ts:
        shift = ts - dist if reverse else dist
        valid = rows < ts - dist if reverse else rows >= dist
        b = b + a * jnp.where(valid, pltpu.roll(b, shift, 0), 0.0)
        a = a * jnp.where(valid, pltpu.roll(a, shift, 0), 1.0)
        dist *= 2
    return a, b


def _cum_forget_fwd(name, fl, bias):
    s = fl.shape[0]
    ts = min(1024, s)

    def body(f_ref, b_ref, o_ref, carry):
        @pl.when(pl.program_id(0) == 0)
        def _():
            carry[...] = jnp.zeros(carry.shape, F32)

        o_ref[...] = _scan_add(_log_sigmoid(f_ref[...] + b_ref[...]), False) + carry[...]
        carry[...] = o_ref[pl.ds(ts - 1, 1), :]

    row = pl.BlockSpec((ts, LANE), lambda i: (i, 0))
    return pl.pallas_call(body, name=name, grid=(s // ts,),
                          in_specs=[row, pl.BlockSpec((1, LANE), lambda i: (0, 0))], out_specs=row,
                          out_shape=jax.ShapeDtypeStruct((s, LANE), F32),
                          scratch_shapes=[pltpu.VMEM((1, LANE), F32)],
                          compiler_params=_cp(("arbitrary",)))(fl, bias)


def _cum_forget_bwd(name, dcf, fl, bias):
    s = fl.shape[0]
    ts = min(1024, s)
    nb = s // ts

    def body(d_ref, f_ref, b_ref, o_ref, db_ref, carry):
        @pl.when(pl.program_id(0) == 0)
        def _():
            carry[...] = jnp.zeros(carry.shape, F32)
            db_ref[...] = jnp.zeros(db_ref.shape, F32)

        run = _scan_add(d_ref[...], True) + carry[...]
        carry[...] = jnp.sum(jnp.where(_iota((ts, 1), 0) == 0, run, 0.0), axis=0, keepdims=True)
        dfl = run * _sigmoid(-(f_ref[...] + b_ref[...]))
        o_ref[...] = dfl.astype(BF16)
        db_ref[...] += jnp.sum(dfl, axis=0, keepdims=True)

    row = pl.BlockSpec((ts, LANE), lambda i: (nb - 1 - i, 0))
    vec = pl.BlockSpec((1, LANE), lambda i: (0, 0))
    return pl.pallas_call(body, name=name, grid=(nb,), in_specs=[row, row, vec], out_specs=[row, vec],
                          out_shape=[jax.ShapeDtypeStruct((s, LANE), BF16), jax.ShapeDtypeStruct((1, LANE), F32)],
                          scratch_shapes=[pltpu.VMEM((1, LANE), F32)],
                          compiler_params=_cp(("arbitrary",)))(dcf, fl, bias)


def _fox_specs(s, nh, tq, tk):
    q = pl.BlockSpec((tq, HEAD), lambda h, i: (i, FQ * nh + h))
    k = pl.BlockSpec((s, HEAD), lambda h, i: (0, FK * nh + h))
    v = pl.BlockSpec((s, HEAD), lambda h, i: (0, FV * nh + h))
    col = pl.BlockSpec((None, tq, 1), lambda h, i: (h, i, 0))
    rowv = pl.BlockSpec((None, s // tk, 1, tk), lambda h, i: (h, 0, 0, 0))
    tile = pl.BlockSpec((tq, HEAD), lambda h, i: (i, h))
    full = pl.BlockSpec((s, HEAD), lambda h, i: (0, h))
    return q, k, v, col, rowv, tile, full


def _fox_scores(q, k, cfq, cfr, kpos, qpos, scale):
    sc = lax.dot_general(q, k, NT, preferred_element_type=F32) * scale + cfq - cfr
    mask = kpos <= qpos
    return jnp.where(mask, sc, NEG), mask


def _fox_fwd(name, u, cf_col, cf_row, bw):
    s, nh = u.shape[0], bw // HEAD
    tq = tk = min(256, s)
    hq = tq // 2
    scale = HEAD ** -0.5

    def body(q_ref, k_ref, v_ref, cfc_ref, cfr_ref, o_ref, lse_ref):
        i = pl.program_id(1)
        half = [(q_ref[pl.ds(r * hq, hq), :], cfc_ref[pl.ds(r * hq, hq), :], i * tq + r * hq + _iota((hq, 1), 0))
                for r in range(2)]

        def step(kb, carry):
            off = pl.multiple_of(kb * tk, tk)
            k, v, cfr = k_ref[pl.ds(off, tk), :], v_ref[pl.ds(off, tk), :], cfr_ref[kb]
            kpos = kb * tk + _iota((1, tk), 1)
            out = []
            for (q, cfq, qpos), (m, l, acc) in zip(half, carry):
                sc, _ = _fox_scores(q, k, cfq, cfr, kpos, qpos, scale)
                m2 = jnp.maximum(m, jnp.max(sc, axis=1, keepdims=True))
                p = jnp.exp(sc - m2)
                al = jnp.exp(m - m2)
                out.append((m2, al * l + jnp.sum(p, axis=1, keepdims=True),
                            al * acc + jnp.dot(p.astype(BF16), v, preferred_element_type=F32)))
            return tuple(out)

        init = ((jnp.full((hq, 1), NEG, F32), jnp.zeros((hq, 1), F32), jnp.zeros((hq, HEAD), F32)),) * 2
        res = lax.fori_loop(0, (i * tq + tq + tk - 1) // tk, step, init)
        for r, (m, l, acc) in enumerate(res):
            o_ref[pl.ds(r * hq, hq), :] = (acc / l).astype(BF16)
            lse_ref[pl.ds(r * hq, hq), :] = m + jnp.log(l)

    q, k, v, col, rowv, tile, _ = _fox_specs(s, nh, tq, tk)
    return pl.pallas_call(
        body, name=name, grid=(nh, s // tq), in_specs=[q, k, v, col, rowv], out_specs=[tile, col],
        out_shape=[jax.ShapeDtypeStruct((s, bw), BF16), jax.ShapeDtypeStruct((nh, s, 1), F32)],
        compiler_params=_cp(("parallel", "parallel")))(u, u, u, cf_col, cf_row)


def _fox_bwd(name, u, cf_col, cf_row, o, do, lse, bw):
    s, nh = u.shape[0], bw // HEAD
    tq = tk = min(256, s)
    hq = tq // 2
    nq = s // tq
    scale = HEAD ** -0.5

    def body(q_ref, k_ref, v_ref, cfc_ref, cfr_ref, o_ref, do_ref, lse_ref,
             dq_ref, dk_ref, dv_ref, dcc_ref, dcr_ref, dk_s, dv_s):
        i = pl.program_id(1)

        @pl.when(i == 0)
        def _():
            dk_s[...] = jnp.zeros(dk_s.shape, F32)
            dv_s[...] = jnp.zeros(dv_s.shape, F32)
            dcr_ref[...] = jnp.zeros(dcr_ref.shape, F32)

        q_all, do_all = q_ref[...], do_ref[...]
        half = []
        for r in range(2):
            rows = pl.ds(r * hq, hq)
            dov = do_ref[rows, :]
            delta = jnp.sum(dov.astype(F32) * o_ref[rows, :].astype(F32), axis=1, keepdims=True)
            half.append((q_ref[rows, :], dov, cfc_ref[rows, :], lse_ref[rows, :], delta,
                         i * tq + r * hq + _iota((hq, 1), 0)))

        def step(kb, carry):
            off = pl.multiple_of(kb * tk, tk)
            k, v, cfr = k_ref[pl.ds(off, tk), :], v_ref[pl.ds(off, tk), :], cfr_ref[kb]
            kpos = kb * tk + _iota((1, tk), 1)
            out, pbs, dsbs, dcol = [], [], [], jnp.zeros((1, tk), F32)
            for (q, dov, cfq, lse_q, delta, qpos), (dq, dcq) in zip(half, carry):
                sc, mask = _fox_scores(q, k, cfq, cfr, kpos, qpos, scale)
                p = jnp.where(mask, jnp.exp(sc - lse_q), 0.0)
                ds = p * (lax.dot_general(dov, v, NT, preferred_element_type=F32) - delta)
                dsb = ds.astype(BF16)
                pbs.append(p.astype(BF16))
                dsbs.append(dsb)
                dcol = dcol - jnp.sum(ds, axis=0, keepdims=True)
                out.append((dq + jnp.dot(dsb, k, preferred_element_type=F32),
                            dcq + jnp.sum(ds, axis=1, keepdims=True)))
            dk_s[pl.ds(off, tk), :] += lax.dot_general(jnp.concatenate(dsbs, axis=0), q_all, TN,
                                                       preferred_element_type=F32)
            dv_s[pl.ds(off, tk), :] += lax.dot_general(jnp.concatenate(pbs, axis=0), do_all, TN,
                                                       preferred_element_type=F32)
            dcr_ref[kb] += dcol
            return tuple(out)

        init = ((jnp.zeros((hq, HEAD), F32), jnp.zeros((hq, 1), F32)),) * 2
        res = lax.fori_loop(0, (i * tq + tq + tk - 1) // tk, step, init)
        for r, (dq, dcq) in enumerate(res):
            dq_ref[pl.ds(r * hq, hq), :] = (dq * scale).astype(BF16)
            dcc_ref[pl.ds(r * hq, hq), :] = dcq

        @pl.when(i == nq - 1)
        def _():
            dk_ref[...] = (dk_s[...] * scale).astype(BF16)
            dv_ref[...] = dv_s[...].astype(BF16)

    q, k, v, col, rowv, tile, full = _fox_specs(s, nh, tq, tk)
    return pl.pallas_call(
        body, name=name, grid=(nh, nq), in_specs=[q, k, v, col, rowv, tile, tile, col],
        out_specs=[tile, full, full, col, rowv],
        out_shape=[jax.ShapeDtypeStruct((s, bw), BF16)] * 3
        + [jax.ShapeDtypeStruct((nh, s, 1), F32), jax.ShapeDtypeStruct((nh, s // tk, 1, tk), F32)],
        scratch_shapes=[pltpu.VMEM((s, HEAD), F32), pltpu.VMEM((s, HEAD), F32)],
        compiler_params=_cp(("arbitrary", "arbitrary")))(u, u, u, cf_col, cf_row, o, do, lse)


def _suffix_mm(x, ones_below):
    hi = x.astype(BF16)
    lo = (x - hi.astype(F32)).astype(BF16)
    return (jnp.dot(hi, ones_below, preferred_element_type=F32) + jnp.dot(lo, ones_below, preferred_element_type=F32))


def _sb_tile(q, k_ref, kb, tk, qpos, scale):
    off = pl.multiple_of(kb * tk, tk)
    k = k_ref[pl.ds(off, tk), :]
    z = lax.dot_general(q, k, NT, preferred_element_type=F32) * scale
    mask = kb * tk + _iota((1, tk), 1) < qpos
    lsn = -jnp.maximum(z, 0.0) - jnp.log(1.0 + jnp.exp(-jnp.abs(z)))
    return z, mask, lsn, jnp.where(mask, lsn, 0.0), k, off


def _sb_specs(s, nh, tq):
    q = pl.BlockSpec((tq, HEAD), lambda h, i: (i, SQ * nh + h))
    k = pl.BlockSpec((s, HEAD), lambda h, i: (0, SK * nh + h))
    v = pl.BlockSpec((s, HEAD), lambda h, i: (0, SV * nh + h))
    tile = pl.BlockSpec((tq, HEAD), lambda h, i: (i, h))
    full = pl.BlockSpec((s, HEAD), lambda h, i: (0, h))
    return q, k, v, tile, full


def _sb_fwd(name, u, bw):
    s, nh = u.shape[0], bw // HEAD
    tq = tk = 128
    scale = HEAD ** -0.5

    def body(q_ref, k_ref, v_ref, o_ref):
        i = pl.program_id(1)
        q = q_ref[...]
        qpos = i * tq + _iota((tq, 1), 0)
        later_keys = (_iota((tk, tk), 0) > _iota((tk, tk), 1)).astype(BF16)
        nk = (i * tq + tq + tk - 2) // tk

        def cond(st):
            return jnp.logical_and(st[0] < nk, st[3] > SB_DEAD)

        def step(st):
            j, c, acc, _ = st
            z, mask, lsn, lm, _, off = _sb_tile(q, k_ref, nk - 1 - j, tk, qpos, scale)
            a = jnp.where(mask, jnp.exp(lsn + z + c + _suffix_mm(lm, later_keys)), 0.0)
            acc = acc + jnp.dot(a.astype(BF16), v_ref[pl.ds(off, tk), :], preferred_element_type=F32)
            c = c + jnp.sum(lm, axis=1, keepdims=True)
            return j + 1, c, acc, jnp.max(c)

        init = (jnp.int32(0), jnp.zeros((tq, 1), F32), jnp.zeros((tq, HEAD), F32), jnp.float32(0.0))
        o_ref[...] = lax.while_loop(cond, step, init)[2].astype(BF16)

    q, k, v, tile, _ = _sb_specs(s, nh, tq)
    return pl.pallas_call(body, name=name, grid=(nh, s // tq), in_specs=[q, k, v], out_specs=tile,
                          out_shape=jax.ShapeDtypeStruct((s, bw), BF16),
                          compiler_params=_cp(("parallel", "parallel")))(u, u, u)


def _sb_bwd(name, u, do, bw):
    s, nh = u.shape[0], bw // HEAD
    tq = tk = 128
    nq = s // tq
    scale = HEAD ** -0.5

    def body(q_ref, k_ref, v_ref, do_ref, dq_ref, dk_ref, dv_ref, dk_s, dv_s):
        i = pl.program_id(1)

        @pl.when(i == 0)
        def _():
            dk_s[...] = jnp.zeros(dk_s.shape, F32)
            dv_s[...] = jnp.zeros(dv_s.shape, F32)

        q, dov = q_ref[...], do_ref[...]
        qpos = i * tq + _iota((tq, 1), 0)
        later_keys = (_iota((tk, tk), 0) > _iota((tk, tk), 1)).astype(BF16)
        this_and_later = (_iota((tk, tk), 0) >= _iota((tk, tk), 1)).astype(BF16)
        nk = (i * tq + tq + tk - 2) // tk

        def weights(j, c):
            z, mask, lsn, lm, k, off = _sb_tile(q, k_ref, nk - 1 - j, tk, qpos, scale)
            a = jnp.where(mask, jnp.exp(lsn + z + c + _suffix_mm(lm, later_keys)), 0.0)
            w = a * lax.dot_general(dov, v_ref[pl.ds(off, tk), :], NT, preferred_element_type=F32)
            return z, mask, lsn, lm, k, off, a, w

        def cond(st):
            return jnp.logical_and(st[0] < nk, st[3] > SB_DEAD)

        def step1(st):
            j, c, wc, _ = st
            _, _, _, lm, _, _, _, w = weights(j, c)
            c = c + jnp.sum(lm, axis=1, keepdims=True)
            return j + 1, c, wc + jnp.sum(w, axis=1, keepdims=True), jnp.max(c)

        zero = jnp.zeros((tq, 1), F32)
        live, _, total, _ = lax.while_loop(cond, step1, (jnp.int32(0), zero, zero, jnp.float32(0.0)))

        def step2(j, st):
            c, wc, dq = st
            z, mask, lsn, lm, k, off, a, w = weights(j, c)
            earlier = total - (wc + _suffix_mm(w, this_and_later))
            dz = jnp.where(mask, w * jnp.exp(lsn) - jnp.exp(lsn + z) * earlier, 0.0)
            dzb = dz.astype(BF16)
            dk_s[pl.ds(off, tk), :] += lax.dot_general(dzb, q, TN, preferred_element_type=F32)
            dv_s[pl.ds(off, tk), :] += lax.dot_general(a.astype(BF16), dov, TN, preferred_element_type=F32)
            return (c + jnp.sum(lm, axis=1, keepdims=True), wc + jnp.sum(w, axis=1, keepdims=True),
                    dq + jnp.dot(dzb, k, preferred_element_type=F32))

        dq = lax.fori_loop(0, live, step2, (zero, zero, jnp.zeros((tq, HEAD), F32)))[2]
        dq_ref[...] = (dq * scale).astype(BF16)

        @pl.when(i == nq - 1)
        def _():
            dk_ref[...] = (dk_s[...] * scale).astype(BF16)
            dv_ref[...] = dv_s[...].astype(BF16)

    q, k, v, tile, full = _sb_specs(s, nh, tq)
    return pl.pallas_call(
        body, name=name, grid=(nh, nq), in_specs=[q, k, v, tile], out_specs=[tile, full, full],
        out_shape=[jax.ShapeDtypeStruct((s, bw), BF16)] * 3,
        scratch_shapes=[pltpu.VMEM((s, HEAD), F32), pltpu.VMEM((s, HEAD), F32)],
        compiler_params=_cp(("arbitrary", "arbitrary")))(u, u, u, do)


def _band_onehot(r):
    kl = _iota((1, WIN), 1)
    ridx = jnp.clip(PADK + r - kl, -(CHUNK - 1), REL_CLIP) + (CHUNK - 1)
    first = (r // CHUNK) * CHUNK
    valid = jnp.logical_and(kl >= first, kl < first + BAND)
    onehot = jnp.logical_and(_iota((REL_PAD, WIN), 0) == ridx, valid)
    return onehot.astype(BF16), valid


def _bias_expand(name, table):
    def body(t_ref, o_ref):
        t = t_ref[...]
        hi = t.astype(BF16)
        r1 = t - hi.astype(F32)
        mid = r1.astype(BF16)
        lo = (r1 - mid.astype(F32)).astype(BF16)

        def row(r, carry):
            onehot, valid = _band_onehot(r)
            val = (jnp.dot(hi, onehot, preferred_element_type=F32) + jnp.dot(mid, onehot, preferred_element_type=F32)
                   + jnp.dot(lo, onehot, preferred_element_type=F32))
            o_ref[r] = jnp.where(valid, val, NEG)
            return carry

        lax.fori_loop(0, QBLK, row, 0)

    return pl.pallas_call(body, name=name, out_shape=jax.ShapeDtypeStruct((QBLK, 16, WIN), F32),
                          in_specs=[pl.BlockSpec(memory_space=pltpu.VMEM)],
                          out_specs=pl.BlockSpec(memory_space=pltpu.VMEM), compiler_params=_cp())(table)


def _bias_reduce(name, ds_rows):
    def body(x_ref, o_ref):
        def row(r, acc):
            onehot, _ = _band_onehot(r)
            x = x_ref[r]
            hi = x.astype(BF16)
            lo = (x - hi.astype(F32)).astype(BF16)
            return (acc + lax.dot_general(hi, onehot, NT, preferred_element_type=F32)
                    + lax.dot_general(lo, onehot, NT, preferred_element_type=F32))

        o_ref[...] = lax.fori_loop(0, QBLK, row, jnp.zeros((16, REL_PAD), F32))

    return pl.pallas_call(body, name=name, out_shape=jax.ShapeDtypeStruct((16, REL_PAD), F32),
                          in_specs=[pl.BlockSpec(memory_space=pltpu.VMEM)],
                          out_specs=pl.BlockSpec(memory_space=pltpu.VMEM), compiler_params=_cp())(ds_rows)


def _chunk_specs(s, nh):
    q = pl.BlockSpec((QBLK, HEAD), lambda h, i: (i, CQ * nh + h))
    kv = pl.BlockSpec((s + PADK, HEAD), lambda h, i: (0, h))
    bias = pl.BlockSpec((None, QBLK, WIN), lambda h, i: (h, 0, 0))
    tile = pl.BlockSpec((QBLK, HEAD), lambda h, i: (i, h))
    full = pl.BlockSpec((s, HEAD), lambda h, i: (0, h))
    return q, kv, bias, tile, full


def _chunk_probs(q, k_ref, b_ref, i, scale):
    off = pl.multiple_of(i * QBLK, QBLK)
    kw = k_ref[pl.ds(off, WIN), :]
    sc = lax.dot_general(q, kw, NT, preferred_element_type=F32) * scale + b_ref[...]
    sc = jnp.where(i * QBLK + _iota((1, WIN), 1) >= PADK, sc, NEG)
    p = jnp.exp(sc - jnp.max(sc, axis=1, keepdims=True))
    return p, jnp.sum(p, axis=1, keepdims=True), kw, off


def _chunk_fwd(name, u, kpad, vpad, bias, bw):
    s, nh = u.shape[0], bw // HEAD
    scale = HEAD ** -0.5

    def body(q_ref, k_ref, v_ref, b_ref, o_ref):
        p, l, _, off = _chunk_probs(q_ref[...], k_ref, b_ref, pl.program_id(1), scale)
        o = jnp.dot(p.astype(BF16), v_ref[pl.ds(off, WIN), :], preferred_element_type=F32)
        o_ref[...] = (o / l).astype(BF16)

    q, kv, bs, tile, _ = _chunk_specs(s, nh)
    return pl.pallas_call(body, name=name, grid=(nh, s // QBLK), in_specs=[q, kv, kv, bs], out_specs=tile,
                          out_shape=jax.ShapeDtypeStruct((s, bw), BF16),
                          compiler_params=_cp(("parallel", "parallel")))(u, kpad, vpad, bias)


def _chunk_bwd(name, u, kpad, vpad, bias, do, bw):
    s, nh = u.shape[0], bw // HEAD
    nq = s // QBLK
    scale = HEAD ** -0.5

    def body(q_ref, k_ref, v_ref, b_ref, do_ref, dq_ref, dk_ref, dv_ref, dss_ref, dk_s, dv_s):
        i = pl.program_id(1)

        @pl.when(i == 0)
        def _():
            dk_s[...] = jnp.zeros(dk_s.shape, F32)
            dv_s[...] = jnp.zeros(dv_s.shape, F32)
            dss_ref[...] = jnp.zeros(dss_ref.shape, F32)

        q, dov = q_ref[...], do_ref[...]
        p, l, kw, off = _chunk_probs(q, k_ref, b_ref, i, scale)
        p = p / l
        dp = lax.dot_general(dov, v_ref[pl.ds(off, WIN), :], NT, preferred_element_type=F32)
        ds = p * (dp - jnp.sum(p * dp, axis=1, keepdims=True))
        dsb = ds.astype(BF16)
        dq_ref[...] = (jnp.dot(dsb, kw, preferred_element_type=F32) * scale).astype(BF16)
        dk_s[pl.ds(off, WIN), :] += lax.dot_general(dsb, q, TN, preferred_element_type=F32)
        dv_s[pl.ds(off, WIN), :] += lax.dot_general(p.astype(BF16), dov, TN, preferred_element_type=F32)
        dss_ref[...] += ds

        @pl.when(i == nq - 1)
        def _():
            dk_ref[...] = (dk_s[pl.ds(PADK, s), :] * scale).astype(BF16)
            dv_ref[...] = dv_s[pl.ds(PADK, s), :].astype(BF16)

    q, kv, bs, tile, full = _chunk_specs(s, nh)
    return pl.pallas_call(
        body, name=name, grid=(nh, nq), in_specs=[q, kv, kv, bs, tile], out_specs=[tile, full, full, bs],
        out_shape=[jax.ShapeDtypeStruct((s, bw), BF16)] * 3 + [jax.ShapeDtypeStruct((nh, QBLK, WIN), F32)],
        scratch_shapes=[pltpu.VMEM((s + PADK, HEAD), F32), pltpu.VMEM((s + PADK, HEAD), F32)],
        compiler_params=_cp(("arbitrary", "arbitrary")))(u, kpad, vpad, bias, do)


def _gelu_parts(y):
    th = jnp.tanh(GELU_K * (y + GELU_C * y * y * y))
    return 0.5 * y * (1.0 + th), th


def _block_diag(xb16, w_ref, nh, dims):
    return jnp.concatenate(
        [lax.dot_general(xb16[:, n * HEAD:(n + 1) * HEAD], w_ref[n], dims, preferred_element_type=F32)
         for n in range(nh)], axis=1)


def _lru_gates(ext, cw_ref, cb_ref, wr_ref, br_ref, wi_ref, bi_ref, lam_ref, ts, nh):
    shifted = [pltpu.roll(ext, CONV_WIDTH - 1 - j, 0)[8:, :] if j < CONV_WIDTH - 1 else ext[8:, :]
               for j in range(CONV_WIDTH)]
    xc = cb_ref[...]
    for j in range(CONV_WIDTH):
        xc = xc + shifted[j] * cw_ref[pl.ds(j, 1), :]
    xcb = xc.astype(BF16)
    r = _sigmoid(_block_diag(xcb, wr_ref, nh, NN) + br_ref[...])
    gi = _sigmoid(_block_diag(xcb, wi_ref, nh, NN) + bi_ref[...])
    lsl = _log_sigmoid(lam_ref[...])
    la = LRU_C * r * lsl
    a = jnp.exp(la)
    e2 = jnp.exp(2.0 * la)
    mult = jnp.sqrt(-jnp.tanh(la) * (e2 + 1.0))
    return shifted, xc, xcb, r, gi, lsl, a, e2, mult


def _lru_param_specs(bw, nh):
    vec = pl.BlockSpec((1, bw), lambda i: (0, 0))
    conv = pl.BlockSpec((8, bw), lambda i: (0, 0))
    blocks = pl.BlockSpec((nh, HEAD, HEAD), lambda i: (0, 0, 0))
    return [conv, vec, blocks, vec, blocks, vec, vec]


def _lru_fwd(name, u, params, bw):
    s, nh = u.shape[0], bw // HEAD
    ts = min(512, s)

    def body(rx_ref, ry_ref, cw_ref, cb_ref, wr_ref, br_ref, wi_ref, bi_ref, lam_ref, o_ref, h_ref, tail, hcar):
        @pl.when(pl.program_id(0) == 0)
        def _():
            tail[...] = jnp.zeros(tail.shape, F32)
            hcar[...] = jnp.zeros(hcar.shape, F32)

        rx = rx_ref[...].astype(F32)
        ext = jnp.concatenate([tail[...], rx], axis=0)
        tail[...] = rx[ts - 8:, :]
        _, xc, _, _, gi, _, a, _, mult = _lru_gates(ext, cw_ref, cb_ref, wr_ref, br_ref, wi_ref, bi_ref, lam_ref, ts, nh)
        acum, bcum = _scan_affine(a, mult * (gi * xc), False)
        h_ref[...] = bcum + acum * hcar[...]
        hcar[...] = h_ref[pl.ds(ts - 1, 1), :]
        o_ref[...] = (h_ref[...] * _gelu_parts(ry_ref[...].astype(F32))[0]).astype(BF16)

    row = pl.BlockSpec((ts, bw), lambda i: (i, 0))
    return pl.pallas_call(
        body, name=name, grid=(s // ts,),
        in_specs=[pl.BlockSpec((ts, bw), lambda i: (i, RX)), pl.BlockSpec((ts, bw), lambda i: (i, RY))]
        + _lru_param_specs(bw, nh),
        out_specs=[row, row],
        out_shape=[jax.ShapeDtypeStruct((s, bw), BF16), jax.ShapeDtypeStruct((s, bw), F32)],
        scratch_shapes=[pltpu.VMEM((8, bw), F32), pltpu.VMEM((1, bw), F32)],
        compiler_params=_cp(("arbitrary",)))(u, u, *params)


def _lru_bwd(name, u, h, do, params, bw):
    s, nh = u.shape[0], bw // HEAD
    ts = min(512, s)
    nb = s // ts
    t8 = ts // 8

    def body(rx_ref, rxp_ref, ry_ref, h_ref, hp_ref, do_ref, cw_ref, cb_ref, wr_ref, br_ref, wi_ref, bi_ref, lam_ref,
             drx_ref, dry_ref, dcw_ref, dcb_ref, dwr_ref, dbr_ref, dwi_ref, dbi_ref, dlam_ref, gcar, head):
        i = pl.program_id(0)
        first = i == nb - 1

        @pl.when(i == 0)
        def _():
            gcar[...] = jnp.zeros(gcar.shape, F32)
            head[...] = jnp.zeros(head.shape, F32)
            for ref in (dcw_ref, dcb_ref, dwr_ref, dbr_ref, dwi_ref, dbi_ref, dlam_ref):
                ref[...] = jnp.zeros(ref.shape, F32)

        rows = _iota((ts, 1), 0)
        rx = rx_ref[...].astype(F32)
        before = jnp.where(first, 0.0, rxp_ref[...].astype(F32))
        ext = jnp.concatenate([before, rx], axis=0)
        shifted, xc, xcb, r, gi, lsl, a, e2, mult = _lru_gates(
            ext, cw_ref, cb_ref, wr_ref, br_ref, wi_ref, bi_ref, lam_ref, ts, nh)

        ry = ry_ref[...].astype(F32)
        gel, th = _gelu_parts(ry)
        dgel = 0.5 * (1.0 + th) + 0.5 * ry * (1.0 - th * th) * GELU_K * (1.0 + 3.0 * GELU_C * ry * ry)
        dov = do_ref[...].astype(F32)
        hv = h_ref[...]
        dry_ref[...] = (dov * hv * dgel).astype(BF16)

        coef = jnp.where(rows < ts - 1, pltpu.roll(a, ts - 1, 0), 0.0)
        dh_in = dov * gel + jnp.where(rows == ts - 1, gcar[...], 0.0)
        dh = _scan_affine(coef, dh_in, True)[1]
        gcar[...] = jnp.sum(jnp.where(rows == 0, a * dh, 0.0), axis=0, keepdims=True)

        hprev = jnp.where(first, 0.0, hp_ref[...])
        hm1 = pltpu.roll(jnp.concatenate([hprev, hv], axis=0), 1, 0)[8:, :]
        dgx = dh * mult
        dla = dh * hm1 * a - dh * gi * xc * (e2 / mult)
        dpre_r = dla * (LRU_C * lsl) * r * (1.0 - r)
        dpre_i = dgx * xc * gi * (1.0 - gi)
        dlam_ref[...] += jnp.sum(dla * r, axis=0, keepdims=True) * (LRU_C * _sigmoid(-lam_ref[...]))
        dbr_ref[...] += jnp.sum(dpre_r, axis=0, keepdims=True)
        dbi_ref[...] += jnp.sum(dpre_i, axis=0, keepdims=True)
        drb, dib = dpre_r.astype(BF16), dpre_i.astype(BF16)
        for n in range(nh):
            cols = slice(n * HEAD, (n + 1) * HEAD)
            dwr_ref[n] += lax.dot_general(xcb[:, cols], drb[:, cols], TN, preferred_element_type=F32)
            dwi_ref[n] += lax.dot_general(xcb[:, cols], dib[:, cols], TN, preferred_element_type=F32)
        dxc = dgx * gi + _block_diag(drb, wr_ref, nh, NT) + _block_diag(dib, wi_ref, nh, NT)

        dcb_ref[...] += jnp.sum(dxc, axis=0, keepdims=True)
        for j in range(CONV_WIDTH):
            dcw_ref[pl.ds(j, 1), :] += jnp.sum(dxc * shifted[j], axis=0, keepdims=True)
        ext2 = jnp.concatenate([dxc, head[...]], axis=0)
        head[...] = dxc[:8, :]
        drx = dxc * cw_ref[pl.ds(CONV_WIDTH - 1, 1), :]
        for j in range(CONV_WIDTH - 1):
            up = CONV_WIDTH - 1 - j
            drx = drx + pltpu.roll(ext2, ts + 8 - up, 0)[:ts, :] * cw_ref[pl.ds(j, 1), :]
        drx_ref[...] = drx.astype(BF16)

    def blk(col):
        return lambda i: (nb - 1 - i, col)

    def prev8(col):
        return lambda i: (jnp.maximum((nb - 1 - i) * t8 - 1, 0), col)

    vec = pl.BlockSpec((1, bw), lambda i: (0, 0))
    conv = pl.BlockSpec((8, bw), lambda i: (0, 0))
    blocks = pl.BlockSpec((nh, HEAD, HEAD), lambda i: (0, 0, 0))
    return pl.pallas_call(
        body, name=name, grid=(nb,),
        in_specs=[pl.BlockSpec((ts, bw), blk(RX)), pl.BlockSpec((8, bw), prev8(RX)), pl.BlockSpec((ts, bw), blk(RY)),
                  pl.BlockSpec((ts, bw), blk(0)), pl.BlockSpec((8, bw), prev8(0)), pl.BlockSpec((ts, bw), blk(0))]
        + _lru_param_specs(bw, nh),
        out_specs=[pl.BlockSpec((ts, bw), blk(0)), pl.BlockSpec((ts, bw), blk(0)), conv, vec, blocks, vec, blocks, vec, vec],
        out_shape=[jax.ShapeDtypeStruct((s, bw), BF16)] * 2
        + [jax.ShapeDtypeStruct((8, bw), F32), jax.ShapeDtypeStruct((1, bw), F32),
           jax.ShapeDtypeStruct((nh, HEAD, HEAD), F32), jax.ShapeDtypeStruct((1, bw), F32),
           jax.ShapeDtypeStruct((nh, HEAD, HEAD), F32), jax.ShapeDtypeStruct((1, bw), F32),
           jax.ShapeDtypeStruct((1, bw), F32)],
        scratch_shapes=[pltpu.VMEM((1, bw), F32), pltpu.VMEM((8, bw), F32)],
        compiler_params=_cp(("arbitrary",)))(u, u, u, h, h, do, *params)


def _gate_merge(name, xb, w_gate, b_gate, o_all, w_branch, l):
    s, d = xb.shape
    bw = o_all.shape[2]
    tm, tn = min(512, s), min(256, d)

    def body(x_ref, wg_ref, bg_ref, o_ref, wb_ref, m_ref, g_ref, p_ref):
        x = x_ref[...]
        acc = jnp.zeros((tm, tn), F32)
        for g in range(4):
            gate = _sigmoid(jnp.dot(x, wg_ref[g], preferred_element_type=F32) + bg_ref[g])
            proj = jnp.dot(o_ref[g], wb_ref[g], preferred_element_type=F32)
            g_ref[g] = gate
            p_ref[g] = proj.astype(BF16)
            acc = acc + gate * proj
        m_ref[...] = acc.astype(BF16)

    quad = pl.BlockSpec((4, tm, tn), lambda n, m: (0, m, n))
    return pl.pallas_call(
        body, name=name, grid=(d // tn, s // tm),
        in_specs=[pl.BlockSpec((tm, d), lambda n, m: (m, 0)),
                  pl.BlockSpec((None, 4, d, tn), lambda n, m: (l, 0, 0, n)),
                  pl.BlockSpec((None, 4, 1, tn), lambda n, m: (l, 0, 0, n)),
                  pl.BlockSpec((4, tm, bw), lambda n, m: (0, m, 0)),
                  pl.BlockSpec((None, 4, bw, tn), lambda n, m: (l, 0, 0, n))],
        out_specs=[pl.BlockSpec((tm, tn), lambda n, m: (m, n)), quad, quad],
        out_shape=[jax.ShapeDtypeStruct((s, d), BF16), jax.ShapeDtypeStruct((4, s, d), F32),
                   jax.ShapeDtypeStruct((4, s, d), BF16)],
        compiler_params=_cp(("parallel", "parallel")))(xb, w_gate, b_gate, o_all, w_branch)


def _adamw(name, w, m, v, parts):
    rows, cols = w.shape
    p = parts.shape[0]
    tr = _pow2_rows(rows, cols * max(1, p // 2), 131072)
    c1 = 1.0 - ADAM_B1 ** ADAM_STEP
    c2 = 1.0 - ADAM_B2 ** ADAM_STEP

    def body(w_ref, m_ref, v_ref, g_ref, go_ref, do_ref, mo_ref, vo_ref):
        g = g_ref[0].astype(F32)
        for k in range(1, p):
            g = g + g_ref[k].astype(F32)
        m2 = ADAM_B1 * m_ref[...] + (1.0 - ADAM_B1) * g
        v2 = ADAM_B2 * v_ref[...] + (1.0 - ADAM_B2) * (g * g)
        go_ref[...] = g
        do_ref[...] = -ADAM_LR * ((m2 / c1) / (jnp.sqrt(v2 / c2) + ADAM_EPS) + ADAM_WD * w_ref[...])
        mo_ref[...] = m2
        vo_ref[...] = v2

    row = pl.BlockSpec((tr, cols), lambda i: (i, 0))
    return pl.pallas_call(
        body, name=name, grid=(rows // tr,),
        in_specs=[row, row, row, pl.BlockSpec((p, tr, cols), lambda i: (0, i, 0))], out_specs=[row] * 4,
        out_shape=[jax.ShapeDtypeStruct((rows, cols), F32)] * 4,
        compiler_params=_cp(("parallel",)))(w, m, v, parts)


def _pack(arrays):
    flat = jnp.concatenate([a.astype(F32).reshape(-1) for a in arrays])
    pad = (-flat.shape[0]) % (8 * LANE)
    return jnp.pad(flat, (0, pad)).reshape(-1, LANE)


def _unpack(packed, shapes):
    flat, out, off = packed.reshape(-1), [], 0
    for shp in shapes:
        n = math.prod(shp)
        out.append(flat[off:off + n].reshape(shp))
        off += n
    return out


def _unshard(gathered, axis):
    block = gathered.shape[2:]
    full = jnp.swapaxes(gathered, 0, 1).reshape((N_DEV,) + block)
    full = jnp.moveaxis(full, 0, axis)
    return full.reshape(block[:axis] + (N_DEV * block[axis],) + block[axis + 1:])


def kernel(x, ln_in_g, ln_in_b, w_in, b_forget, conv_w, conv_b, w_r, b_r, w_i, b_i, lru_lambda, rel_bias, w_branch, w_gate, b_gate, w_out, ln1_g, ln1_b, w_ff1, w_ff2, ln2_g, ln2_b, loss_target, m_ln_in_g, m_ln_in_b, m_w_in, m_b_forget, m_conv_w, m_conv_b, m_w_r, m_b_r, m_w_i, m_b_i, m_lru_lambda, m_rel_bias, m_w_branch, m_w_gate, m_b_gate, m_w_out, m_ln1_g, m_ln1_b, m_w_ff1, m_w_ff2, m_ln2_g, m_ln2_b, v_ln_in_g, v_ln_in_b, v_w_in, v_b_forget, v_conv_w, v_conv_b, v_w_r, v_b_r, v_w_i, v_b_i, v_lru_lambda, v_rel_bias, v_w_branch, v_w_gate, v_b_gate, v_w_out, v_ln1_g, v_ln1_b, v_w_ff1, v_w_ff2, v_ln2_g, v_ln2_b):
    given = dict(zip(
        NAMES + ['loss_target'] + ['m_' + n for n in WEIGHTS] + ['v_' + n for n in WEIGHTS],
        (x, ln_in_g, ln_in_b, w_in, b_forget, conv_w, conv_b, w_r, b_r, w_i, b_i, lru_lambda, rel_bias, w_branch, w_gate, b_gate, w_out, ln1_g, ln1_b, w_ff1, w_ff2, ln2_g, ln2_b, loss_target, m_ln_in_g, m_ln_in_b, m_w_in, m_b_forget, m_conv_w, m_conv_b, m_w_r, m_b_r, m_w_i, m_b_i, m_lru_lambda, m_rel_bias, m_w_branch, m_w_gate, m_b_gate, m_w_out, m_ln1_g, m_ln1_b, m_w_ff1, m_w_ff2, m_ln2_g, m_ln2_b, v_ln_in_g, v_ln_in_b, v_w_in, v_b_forget, v_conv_w, v_conv_b, v_w_r, v_b_r, v_w_i, v_b_i, v_lru_lambda, v_rel_bias, v_w_branch, v_w_gate, v_b_gate, v_w_out, v_ln1_g, v_ln1_b, v_w_ff1, v_w_ff2, v_ln2_g, v_ln2_b)))

    s, d = x.shape[1], x.shape[2]
    nl = w_in.shape[0]
    bw = d // 4
    nh = bw // HEAD
    nu = 11 * bw
    rs = d // N_DEV
    dff = w_ff1.shape[2] * N_DEV
    fs = dff // N_DEV
    cs = d // N_DEV
    assert nl == DEPTH and nh * HEAD == bw and s % 256 == 0 and d % 1024 == 0

    xi, yi, ci = _position()
    dev = 4 * xi + 2 * yi + ci
    c_arr = jnp.reshape(ci, (1,)).astype(I32)

    w_main = jnp.concatenate([w_in[..., :3 * bw], w_in[..., 3 * bw + nh:]], axis=-1).astype(BF16)
    w_fcol = jnp.pad(w_in[..., 3 * bw:3 * bw + nh], ((0, 0), (0, 0), (0, LANE - nh))).astype(BF16)
    small_shapes = [conv_w.shape, rel_bias.shape, b_gate.shape]
    shards = [w_main, w_fcol, w_branch.astype(BF16), w_gate.astype(BF16), w_out.astype(BF16),
              w_ff1.astype(BF16), w_ff2.astype(BF16), _pack([conv_w, rel_bias, b_gate])]
    gathered = _exchange_cores("gather_cores", _exchange_chips("gather_chips", shards, True), True)
    W_main, W_f = _unshard(gathered[0], 1), _unshard(gathered[1], 1)
    W_branch, W_gate, W_out = _unshard(gathered[2], 3), _unshard(gathered[3], 2), _unshard(gathered[4], 1)
    W_ff1, W_ff2 = _unshard(gathered[5], 2), _unshard(gathered[6], 1)
    small = jnp.swapaxes(gathered[7], 0, 1).reshape((N_DEV,) + gathered[7].shape[2:])
    small = [_unpack(small[j], small_shapes) for j in range(N_DEV)]
    conv_w_full = jnp.concatenate([small[j][0] for j in range(N_DEV)], axis=-1)
    rel_bias_full = jnp.concatenate([small[j][1] for j in range(N_DEV)], axis=-1)
    b_gate_full = jnp.concatenate([small[j][2] for j in range(N_DEV)], axis=-1)
    b_gate4 = b_gate_full.reshape(nl, 4, 1, d)

    def lru_params(l):
        return (jnp.pad(conv_w_full[l], ((0, 8 - CONV_WIDTH), (0, 0))), conv_b[l].reshape(1, bw),
                w_r[l].astype(BF16), b_r[l].reshape(1, bw), w_i[l].astype(BF16), b_i[l].reshape(1, bw),
                lru_lambda[l].reshape(1, bw))

    def bias_rows(l):
        return jnp.pad(rel_bias_full[l], ((0, 16 - nh), (0, REL_PAD - REL_TABLE)))

    tm = min(1024, s)
    tkk = min(2048, d)

    xs = x[0]
    h0, h0b = _ln_fwd("ln_in", xs, ln_in_g, ln_in_b)
    saved = []
    cur, curb = h0, h0b
    for l in range(nl):
        u = _mm(f"w_in_{l}", curb, W_main, grid=(s // tm, nu // bw, d // tkk),
                a_spec=pl.BlockSpec((tm, tkk), lambda m, n, k: (m, k)),
                b_spec=pl.BlockSpec((None, tkk, bw), lambda m, n, k, l=l: (l, k, n)),
                dims=NN, acc_shape=(tm, bw), out_shape=[jax.ShapeDtypeStruct((s, nu), BF16)],
                out_specs=[pl.BlockSpec((tm, bw), lambda m, n, k: (m, n))], finish=_store(BF16))[0]
        fl = _mm(f"w_forget_{l}", curb, W_f, grid=(s // tm, 1, d // tkk),
                 a_spec=pl.BlockSpec((tm, tkk), lambda m, n, k: (m, k)),
                 b_spec=pl.BlockSpec((None, tkk, LANE), lambda m, n, k, l=l: (l, k, 0)),
                 dims=NN, acc_shape=(tm, LANE), out_shape=[jax.ShapeDtypeStruct((s, LANE), F32)],
                 out_specs=[pl.BlockSpec((tm, LANE), lambda m, n, k: (m, 0))], finish=_store(F32))[0]
        bf_row = jnp.pad(b_forget[l], (0, LANE - nh)).reshape(1, LANE)
        cf = _cum_forget_fwd(f"cum_forget_{l}", fl, bf_row)
        tkf = min(256, s)
        cf_heads = cf[:, :nh].T
        cf_col = cf_heads.reshape(nh, s, 1)
        cf_row = cf_heads.reshape(nh, s // tkf, 1, tkf)
        o_fox, lse = _fox_fwd(f"fox_fwd_{l}", u, cf_col, cf_row, bw)
        lp = lru_params(l)
        o_lru, hstate = _lru_fwd(f"lru_fwd_{l}", u, lp, bw)
        o_sb = _sb_fwd(f"sb_fwd_{l}", u, bw)
        bias = jnp.transpose(_bias_expand(f"bias_expand_{l}", bias_rows(l)), (1, 0, 2))[:nh]
        kpad = jnp.pad(u[:, CK * bw:(CK + 1) * bw], ((PADK, 0), (0, 0)))
        vpad = jnp.pad(u[:, CV * bw:(CV + 1) * bw], ((PADK, 0), (0, 0)))
        o_ch = _chunk_fwd(f"chunk_fwd_{l}", u, kpad, vpad, bias, bw)
        o_all = jnp.stack([o_fox, o_lru, o_sb, o_ch])
        merged, gates, projs = _gate_merge(f"gate_merge_{l}", curb, W_gate, b_gate4, o_all, W_branch, l)
        z1, x1, x1b = _mm_ln(f"w_out_ln1_{l}", merged, W_out, l, cur, ln1_g[l], ln1_b[l])
        tn1 = min(1024, dff)

        def ff1_finish(acc, ex, outs, ids):
            outs[0][...] = acc.astype(BF16)
            r = jnp.maximum(acc, 0.0)
            outs[1][...] = (r * r).astype(BF16)

        hp, hid = _mm(f"w_ff1_{l}", x1b, W_ff1, grid=(s // tm, dff // tn1, d // tkk),
                      a_spec=pl.BlockSpec((tm, tkk), lambda m, n, k: (m, k)),
                      b_spec=pl.BlockSpec((None, tkk, tn1), lambda m, n, k, l=l: (l, k, n)),
                      dims=NN, acc_shape=(tm, tn1),
                      out_shape=[jax.ShapeDtypeStruct((s, dff), BF16)] * 2,
                      out_specs=[pl.BlockSpec((tm, tn1), lambda m, n, k: (m, n))] * 2, finish=ff1_finish)
        z2, x2, x2b = _mm_ln(f"w_ff2_ln2_{l}", hid, W_ff2, l, x1, ln2_g[l], ln2_b[l])
        saved.append(dict(xin=cur, xinb=curb, u=u, fl=fl, bf_row=bf_row, cf_col=cf_col, cf_row=cf_row, o_fox=o_fox,
                          lse=lse, lp=lp, hstate=hstate, bias=bias, kpad=kpad, vpad=vpad, o_all=o_all, merged=merged,
                          gates=gates, projs=projs, z1=z1, x1=x1, x1b=x1b, hp=hp, hid=hid, z2=z2))
        cur, curb = x2, x2b

    loss_tile, dcur = _loss_head("loss_head", cur, loss_target[0])
    loss = lax.psum(loss_tile[0, 0], ("x", "y", "c"))

    big = {}
    sm = {n: [None] * nl for n in ['b_forget', 'conv_w', 'conv_b', 'w_r', 'b_r', 'w_i', 'b_i', 'lru_lambda', 'rel_bias',
                                   'b_gate', 'ln1_g', 'ln1_b', 'ln2_g', 'ln2_b']}

    def split_columns(acc, ex, outs, ids):
        for j in range(N_DEV):
            outs[0][j] = acc[:, j * cs:(j + 1) * cs]

    def grad_mm(key, name, a, b, *, shape, grid, a_spec, b_spec, out_spec, acc_shape, finish=_store(F32)):
        sem = ("parallel", "parallel", "arbitrary")
        has = key in big
        extras = (big[key],) if has else ()
        res = _mm(name, a, b, grid=grid, a_spec=a_spec, b_spec=b_spec, dims=TN, acc_shape=acc_shape,
                  out_shape=[jax.ShapeDtypeStruct(shape, F32)], out_specs=[out_spec], finish=finish,
                  extras=extras, extra_specs=(ANY,) * len(extras), aliases={2: 0} if has else None, sem=sem)[0]
        big[key] = res

    tks = min(1024, s)
    tmr = min(1024, d)
    nsh = tmr // rs

    for l in reversed(range(nl)):
        sv = saved[l]
        dz2, dz2b, dg, db = _ln_bwd(f"ln2_bwd_{l}", dcur, sv['z2'], ln2_g[l])
        sm['ln2_g'][l], sm['ln2_b'][l] = dg[0], db[0]
        tn1 = min(1024, dff)

        def dhp_finish(acc, ex, outs, ids):
            outs[0][...] = (acc * (2.0 * jnp.maximum(ex[0][...].astype(F32), 0.0))).astype(BF16)

        dhp = _mm(f"d_hidden_{l}", dz2b, W_ff2, grid=(s // tm, dff // tn1, d // tkk),
                  a_spec=pl.BlockSpec((tm, tkk), lambda m, n, k: (m, k)),
                  b_spec=pl.BlockSpec((None, tn1, tkk), lambda m, n, k, l=l: (l, n, k)),
                  dims=NT, acc_shape=(tm, tn1), out_shape=[jax.ShapeDtypeStruct((s, dff), BF16)],
                  out_specs=[pl.BlockSpec((tm, tn1), lambda m, n, k: (m, n))], finish=dhp_finish,
                  extras=(sv['hp'],), extra_specs=(pl.BlockSpec((tm, tn1), lambda m, n, k: (m, n)),))[0]
        grad_mm('w_ff2', f"g_w_ff2_{l}", sv['hid'], dz2b, shape=(N_DEV, nl, fs, d), grid=(N_DEV, 1, s // tks),
                a_spec=pl.BlockSpec((tks, fs), lambda m, n, k: (k, m)),
                b_spec=pl.BlockSpec((tks, d), lambda m, n, k: (k, 0)),
                out_spec=pl.BlockSpec((None, None, fs, d), lambda m, n, k, l=l: (m, l, 0, 0)), acc_shape=(fs, d))
        grad_mm('w_ff1', f"g_w_ff1_{l}", sv['x1b'], dhp, shape=(N_DEV, nl, d, fs), grid=(d // tmr, N_DEV, s // tks),
                a_spec=pl.BlockSpec((tks, tmr), lambda m, n, k: (k, m)),
                b_spec=pl.BlockSpec((tks, fs), lambda m, n, k: (k, n)),
                out_spec=pl.BlockSpec((None, None, tmr, fs), lambda m, n, k, l=l: (n, l, m, 0)), acc_shape=(tmr, fs))
        tnd = min(1024, d)

        def resid_finish(scale):
            def finish(acc, ex, outs, ids):
                outs[0][...] = acc + scale * ex[0][...]
            return finish

        tile_md = pl.BlockSpec((tm, tnd), lambda m, n, k: (m, n))
        dx1 = _mm(f"d_x1_{l}", dhp, W_ff1, grid=(s // tm, d // tnd, dff // tkk),
                  a_spec=pl.BlockSpec((tm, tkk), lambda m, n, k: (m, k)),
                  b_spec=pl.BlockSpec((None, tnd, tkk), lambda m, n, k, l=l: (l, n, k)),
                  dims=NT, acc_shape=(tm, tnd), out_shape=[jax.ShapeDtypeStruct((s, d), F32)],
                  out_specs=[tile_md], finish=resid_finish(ALPHA), extras=(dz2,), extra_specs=(tile_md,))[0]

        dz1, dz1b, dg, db = _ln_bwd(f"ln1_bwd_{l}", dx1, sv['z1'], ln1_g[l])
        sm['ln1_g'][l], sm['ln1_b'][l] = dg[0], db[0]
        tmg, tng = min(512, s), min(512, d)

        def gate_finish(acc, ex, outs, ids):
            @pl.when(ids[1] == 0)
            def _():
                outs[2][...] = jnp.zeros(outs[2].shape, F32)

            for g in range(4):
                gate = ex[0][g]
                dproj = acc * gate
                dpre = acc * ex[1][g].astype(F32) * gate * (1.0 - gate)
                outs[0][g] = dproj.astype(BF16)
                outs[1][g] = dpre.astype(BF16)
                outs[2][g] += jnp.sum(dpre, axis=0, keepdims=True)

        quad = pl.BlockSpec((4, tmg, tng), lambda n, m, k: (0, m, n))
        dproj, dpre, dbg = _mm(
            f"d_merged_{l}", dz1b, W_out, grid=(d // tng, s // tmg, d // tkk),
            a_spec=pl.BlockSpec((tmg, tkk), lambda n, m, k: (m, k)),
            b_spec=pl.BlockSpec((None, tng, tkk), lambda n, m, k, l=l: (l, n, k)),
            dims=NT, acc_shape=(tmg, tng),
            out_shape=[jax.ShapeDtypeStruct((4, s, d), BF16), jax.ShapeDtypeStruct((4, s, d), BF16),
                       jax.ShapeDtypeStruct((4, 1, d), F32)],
            out_specs=[quad, quad, pl.BlockSpec((4, 1, tng), lambda n, m, k: (0, 0, n))], finish=gate_finish,
            extras=(sv['gates'], sv['projs']), extra_specs=(quad, quad), sem=("arbitrary", "arbitrary", "arbitrary"))
        sm['b_gate'][l] = dbg.reshape(4, d)
        grad_mm('w_out', f"g_w_out_{l}", sv['merged'], dz1b, shape=(N_DEV, nl, rs, d), grid=(d // tmr, 1, s // tks),
                a_spec=pl.BlockSpec((tks, tmr), lambda m, n, k: (k, m)),
                b_spec=pl.BlockSpec((tks, d), lambda m, n, k: (k, 0)),
                out_spec=pl.BlockSpec((nsh, None, rs, d), lambda m, n, k, l=l: (m, l, 0, 0)), acc_shape=(tmr, d))

        nm = s // tm
        do_all = _mm(f"d_branch_{l}", dproj, W_branch, grid=(4 * nm, 1, d // tkk),
                     a_spec=pl.BlockSpec((None, tm, tkk), lambda m, n, k: (m // nm, m % nm, k)),
                     b_spec=pl.BlockSpec((None, None, bw, tkk), lambda m, n, k, l=l: (l, m // nm, 0, k)),
                     dims=NT, acc_shape=(tm, bw), out_shape=[jax.ShapeDtypeStruct((4, s, bw), BF16)],
                     out_specs=[pl.BlockSpec((None, tm, bw), lambda m, n, k: (m // nm, m % nm, 0))],
                     finish=_store(BF16))[0]
        grad_mm('w_branch', f"g_w_branch_{l}", sv['o_all'], dproj, shape=(N_DEV, nl, 4, bw, cs),
                grid=(4, 1, s // tks), finish=split_columns,
                a_spec=pl.BlockSpec((None, tks, bw), lambda m, n, k: (m, k, 0)),
                b_spec=pl.BlockSpec((None, tks, d), lambda m, n, k: (m, k, 0)),
                out_spec=pl.BlockSpec((N_DEV, None, None, bw, cs), lambda m, n, k, l=l: (0, l, m, 0, 0)),
                acc_shape=(bw, d))
        grad_mm('w_gate', f"g_w_gate_{l}", sv['xinb'], dpre, shape=(N_DEV, nl, 4, rs, d), grid=(d // tmr, 4, s // tks),
                a_spec=pl.BlockSpec((tks, tmr), lambda m, n, k: (k, m)),
                b_spec=pl.BlockSpec((None, tks, d), lambda m, n, k: (n, k, 0)),
                out_spec=pl.BlockSpec((nsh, None, None, rs, d), lambda m, n, k, l=l: (m, l, n, 0, 0)),
                acc_shape=(tmr, d))
        nkg = d // tkk
        dx_gate = _mm(f"d_x_gates_{l}", dpre, W_gate, grid=(s // tm, d // tnd, 4 * nkg),
                      a_spec=pl.BlockSpec((None, tm, tkk), lambda m, n, k: (k // nkg, m, k % nkg)),
                      b_spec=pl.BlockSpec((None, None, tnd, tkk), lambda m, n, k, l=l: (l, k // nkg, n, k % nkg)),
                      dims=NT, acc_shape=(tm, tnd), out_shape=[jax.ShapeDtypeStruct((s, d), F32)],
                      out_specs=[tile_md], finish=resid_finish(ALPHA), extras=(dz1,), extra_specs=(tile_md,))[0]

        u = sv['u']
        dfq, dfk, dfv, dcc, dcr = _fox_bwd(f"fox_bwd_{l}", u, sv['cf_col'], sv['cf_row'], sv['o_fox'], do_all[0],
                                          sv['lse'], bw)
        dcf = (dcc.reshape(nh, s) + dcr.reshape(nh, s)).T
        dflb, dbf = _cum_forget_bwd(f"cum_forget_bwd_{l}", jnp.pad(dcf, ((0, 0), (0, LANE - nh))), sv['fl'],
                                    sv['bf_row'])
        sm['b_forget'][l] = dbf[0, :nh]
        drx, dry, dcw, dcb, dwr, dbr, dwi, dbi, dlam = _lru_bwd(f"lru_bwd_{l}", u, sv['hstate'], do_all[1], sv['lp'], bw)
        sm['conv_w'][l], sm['conv_b'][l], sm['w_r'][l], sm['b_r'][l] = dcw[:CONV_WIDTH], dcb[0], dwr, dbr[0]
        sm['w_i'][l], sm['b_i'][l], sm['lru_lambda'][l] = dwi, dbi[0], dlam[0]
        dsq, dsk, dsv = _sb_bwd(f"sb_bwd_{l}", u, do_all[2], bw)
        dcq, dck, dcv, dss = _chunk_bwd(f"chunk_bwd_{l}", u, sv['kpad'], sv['vpad'], sv['bias'], do_all[3], bw)
        dss_rows = jnp.pad(jnp.transpose(dss, (1, 0, 2)), ((0, 0), (0, 16 - nh), (0, 0)))
        sm['rel_bias'][l] = _bias_reduce(f"bias_reduce_{l}", dss_rows)[:nh, :REL_TABLE]
        du = jnp.concatenate([dfq, dfk, dfv, drx, dry, dsq, dsk, dsv, dcq, dck, dcv], axis=1)

        tnu = 11 * LANE
        grad_mm('w_main', f"g_w_in_{l}", sv['xinb'], du, shape=(N_DEV, nl, rs, nu), grid=(d // tmr, nu // tnu, s // tks),
                a_spec=pl.BlockSpec((tks, tmr), lambda m, n, k: (k, m)),
                b_spec=pl.BlockSpec((tks, tnu), lambda m, n, k: (k, n)),
                out_spec=pl.BlockSpec((nsh, None, rs, tnu), lambda m, n, k, l=l: (m, l, 0, n)), acc_shape=(tmr, tnu))
        grad_mm('w_f', f"g_w_forget_{l}", sv['xinb'], dflb, shape=(N_DEV, nl, rs, LANE), grid=(d // tmr, 1, s // tks),
                a_spec=pl.BlockSpec((tks, tmr), lambda m, n, k: (k, m)),
                b_spec=pl.BlockSpec((tks, LANE), lambda m, n, k: (k, 0)),
                out_spec=pl.BlockSpec((nsh, None, rs, LANE), lambda m, n, k, l=l: (m, l, 0, 0)), acc_shape=(tmr, LANE))
        tku = 11 * LANE
        dxa = _mm(f"d_x_in_{l}", du, W_main, grid=(s // tm, d // tnd, nu // tku),
                  a_spec=pl.BlockSpec((tm, tku), lambda m, n, k: (m, k)),
                  b_spec=pl.BlockSpec((None, tnd, tku), lambda m, n, k, l=l: (l, n, k)),
                  dims=NT, acc_shape=(tm, tnd), out_shape=[jax.ShapeDtypeStruct((s, d), F32)],
                  out_specs=[tile_md], finish=resid_finish(1.0), extras=(dx_gate,), extra_specs=(tile_md,))[0]
        dcur = _mm(f"d_x_forget_{l}", dflb, W_f, grid=(s // tm, d // tnd, 1),
                   a_spec=pl.BlockSpec((tm, LANE), lambda m, n, k: (m, 0)),
                   b_spec=pl.BlockSpec((None, tnd, LANE), lambda m, n, k, l=l: (l, n, 0)),
                   dims=NT, acc_shape=(tm, tnd), out_shape=[jax.ShapeDtypeStruct((s, d), F32)],
                   out_specs=[tile_md], finish=resid_finish(1.0), extras=(dxa,), extra_specs=(tile_md,))[0]

    grad_x, _, dg_in, db_in = _ln_bwd("ln_in_bwd", dcur, xs, ln_in_g)

    keys = ['w_main', 'w_f', 'w_branch', 'w_gate', 'w_out', 'w_ff1', 'w_ff2']
    slabs = {k: big[k].shape[1:] for k in keys}
    halves = [big[k].reshape((4, 2) + slabs[k]) for k in keys]
    from_core = _exchange_cores("reduce_cores", halves, False)
    partial = []
    for k, mine, other in zip(keys, halves, from_core):
        cols = slabs[k][-1]
        rows = math.prod(slabs[k]) // cols
        partial.append(_add_core_halves(f"add_cores_{k}", mine.reshape(4, 2, rows, cols), other.reshape(4, rows, cols),
                                        c_arr, BF16))
    from_chips = dict(zip(keys, _exchange_chips("reduce_chips", partial, False)))

    small_names = ['ln_in_g', 'ln_in_b', 'b_forget', 'conv_w', 'conv_b', 'w_r', 'b_r', 'w_i', 'b_i', 'lru_lambda',
                   'rel_bias', 'b_gate', 'ln1_g', 'ln1_b', 'ln2_g', 'ln2_b']
    local = {'ln_in_g': dg_in[0], 'ln_in_b': db_in[0]}
    for n in small_names[2:]:
        local[n] = jnp.stack(sm[n])
    full_shapes = [local[n].shape for n in small_names]
    packed = _pack([local[n] for n in small_names])
    every = _exchange_cores("gather_small_cores", _exchange_chips("gather_small_chips", [packed], True), True)[0]
    every = jnp.swapaxes(every, 0, 1).reshape((N_DEV,) + packed.shape)
    total = dict(zip(small_names, _unpack(_sum_parts("sum_small", every), full_shapes)))
    for n, width in (('conv_w', bw // N_DEV), ('rel_bias', REL_TABLE // N_DEV), ('b_gate', cs)):
        total[n] = lax.dynamic_slice_in_dim(total[n], dev * width, width, axis=2)

    out = {}

    def update(n, parts, shape2d):
        res = _adamw(f"adamw_{n}", given[n].reshape(shape2d), given['m_' + n].reshape(shape2d),
                     given['v_' + n].reshape(shape2d), parts)
        out[n] = [r.reshape(given[n].shape) for r in res]

    pm, pf = from_chips['w_main'], from_chips['w_f']
    parts_in = jnp.concatenate([pm[..., :3 * bw], pf[..., :nh], pm[..., 3 * bw:]], axis=-1)
    update('w_in', parts_in.reshape(4, nl * rs, w_in.shape[2]), (nl * rs, w_in.shape[2]))
    update('w_branch', from_chips['w_branch'].reshape(4, nl * 4 * bw, cs), (nl * 4 * bw, cs))
    update('w_gate', from_chips['w_gate'].reshape(4, nl * 4 * rs, d), (nl * 4 * rs, d))
    update('w_out', from_chips['w_out'].reshape(4, nl * rs, d), (nl * rs, d))
    update('w_ff1', from_chips['w_ff1'].reshape(4, nl * d, fs), (nl * d, fs))
    update('w_ff2', from_chips['w_ff2'].reshape(4, nl * fs, d), (nl * fs, d))

    small_shapes2 = [given[n].shape for n in small_names]
    res = _adamw("adamw_small", _pack([given[n] for n in small_names]), _pack([given['m_' + n] for n in small_names]),
                 _pack([given['v_' + n] for n in small_names]), _pack([total[n] for n in small_names])[None])
    res = [_unpack(r, small_shapes2) for r in res]
    for j, n in enumerate(small_names):
        out[n] = [res[k][j] for k in range(4)]

    return (loss, grad_x[None], *[out[n][0] for n in WEIGHTS], *[out[n][1] for n in WEIGHTS],
            *[out[n][2] for n in WEIGHTS], *[out[n][3] for n in WEIGHTS])
```

```python
import functools
import math

import jax
import jax.numpy as jnp
from jax import lax
from jax.experimental import pallas as pl
from jax.experimental.pallas import tpu as pltpu

F32, BF16, I32 = jnp.float32, jnp.bfloat16, jnp.int32
MESH = pl.DeviceIdType.MESH
ANY = pl.BlockSpec(memory_space=pl.ANY)

LANE = 128
VMEM_LIMIT = 56 * 1024 * 1024
N_DEV = 8

HEAD = 128
CHUNK = 64
LOOKBACK = 8
BAND = (LOOKBACK + 1) * CHUNK
QBLK = 2 * CHUNK
WIN = BAND + CHUNK
PADK = LOOKBACK * CHUNK
REL_CLIP = 256
REL_TABLE = REL_CLIP + CHUNK
REL_PAD = 384
CONV_WIDTH = 4
LRU_C = 8.0
LN_EPS = 1e-5
DEPTH = 2
ALPHA = (2.0 * DEPTH) ** 0.25
NEG = -1e30
SB_DEAD = -104.0
GELU_K = math.sqrt(2.0 / math.pi)
GELU_C = 0.044715

ADAM_LR, ADAM_B1, ADAM_B2, ADAM_EPS, ADAM_WD, ADAM_STEP = 0.001, 0.9, 0.999, 1e-08, 0.01, 10

NN = (((1,), (0,)), ((), ()))
NT = (((1,), (1,)), ((), ()))
TN = (((0,), (0,)), ((), ()))

FQ, FK, FV, RX, RY, SQ, SK, SV, CQ, CK, CV = range(11)

NAMES = ['x', 'ln_in_g', 'ln_in_b', 'w_in', 'b_forget', 'conv_w', 'conv_b', 'w_r', 'b_r', 'w_i', 'b_i', 'lru_lambda',
         'rel_bias', 'w_branch', 'w_gate', 'b_gate', 'w_out', 'ln1_g', 'ln1_b', 'w_ff1', 'w_ff2', 'ln2_g', 'ln2_b']
WEIGHTS = NAMES[1:]


def _cp(sem=None):
    return pltpu.CompilerParams(dimension_semantics=sem, vmem_limit_bytes=VMEM_LIMIT)


def _iota(shape, dim):
    return lax.broadcasted_iota(I32, shape, dim)


def _sigmoid(x):
    return 1.0 / (1.0 + jnp.exp(-x))


def _log_sigmoid(x):
    return jnp.minimum(x, 0.0) - jnp.log(1.0 + jnp.exp(-jnp.abs(x)))


def _pow2_rows(rows, cols, elems=262144):
    t = 8
    while t * 2 <= rows and t * 2 * cols <= elems and rows % (t * 2) == 0:
        t *= 2
    return t


def _position():
    return lax.axis_index("x"), lax.axis_index("y"), lax.axis_index("c")


def _exchange_chips(name, xs, gather):
    n = len(xs)

    def body(*refs):
        ins, outs = refs[:n], refs[n:2 * n]
        send_sems, recv_sems, local_sems = refs[2 * n:]
        x, y, c = _position()
        q = 2 * x + y
        chips = [(1 - x, y), (x, 1 - y), (1 - x, 1 - y)]

        def src(t, slot):
            return ins[t] if gather else ins[t].at[slot]

        def dst(t, slot):
            return outs[t].at[c, slot] if gather else outs[t].at[slot]

        local = [pltpu.make_async_copy(src(t, q), dst(t, q), local_sems.at[t]) for t in range(n)]
        for cp in local:
            cp.start()

        def remote(t, j, landing):
            px, py = chips[j]
            return pltpu.make_async_remote_copy(
                src_ref=src(t, 2 * px + py), dst_ref=dst(t, landing), send_sem=send_sems.at[t, j],
                recv_sem=recv_sems.at[t, j], device_id=(px, py, c), device_id_type=MESH)

        sends = [remote(t, j, q) for t in range(n) for j in range(3)]
        for cp in sends:
            cp.start()
        for t in range(n):
            for j, (px, py) in enumerate(chips):
                remote(t, j, 2 * px + py).wait_recv()
        for cp in sends:
            cp.wait_send()
        for cp in local:
            cp.wait()

    out_shape = [jax.ShapeDtypeStruct((2, 4) + a.shape if gather else a.shape, a.dtype) for a in xs]
    return pl.pallas_call(
        body, name=name, out_shape=out_shape, in_specs=[ANY] * n, out_specs=[ANY] * n,
        scratch_shapes=[pltpu.SemaphoreType.DMA((n, 3)), pltpu.SemaphoreType.DMA((n, 3)),
                        pltpu.SemaphoreType.DMA((n,))],
    )(*xs)


def _exchange_cores(name, xs, gather):
    n = len(xs)
    m = 1 if gather else 4

    def body(*refs):
        ins, outs = refs[:n], refs[n:2 * n]
        send_sems, recv_sems = refs[2 * n:]
        x, y, c = _position()
        sibling = (x, y, 1 - c)

        def remote(t, j, landing):
            s = outs[t].at[c] if gather else ins[t].at[j, 1 - c]
            d = outs[t].at[landing] if gather else outs[t].at[j]
            return pltpu.make_async_remote_copy(
                src_ref=s, dst_ref=d, send_sem=send_sems.at[t, j], recv_sem=recv_sems.at[t, j],
                device_id=sibling, device_id_type=MESH)

        sends = [remote(t, j, c) for t in range(n) for j in range(m)]
        for cp in sends:
            cp.start()
        for t in range(n):
            for j in range(m):
                remote(t, j, 1 - c).wait_recv()
        for cp in sends:
            cp.wait_send()

    if gather:
        out_shape = [jax.ShapeDtypeStruct(a.shape, a.dtype) for a in xs]
    else:
        out_shape = [jax.ShapeDtypeStruct((4,) + a.shape[2:], a.dtype) for a in xs]
    return pl.pallas_call(
        body, name=name, out_shape=out_shape, in_specs=[ANY] * n, out_specs=[ANY] * n,
        input_output_aliases={t: t for t in range(n)} if gather else {},
        scratch_shapes=[pltpu.SemaphoreType.DMA((n, m)), pltpu.SemaphoreType.DMA((n, m))],
    )(*xs)


def _add_core_halves(name, mine, other, c, out_dtype):
    _, _, rows, cols = mine.shape
    tr = _pow2_rows(rows, cols)

    def body(c_ref, a_ref, b_ref, o_ref):
        o_ref[...] = (a_ref[...] + b_ref[...]).astype(out_dtype)

    grid_spec = pltpu.PrefetchScalarGridSpec(
        num_scalar_prefetch=1, grid=(4, rows // tr),
        in_specs=[pl.BlockSpec((None, None, tr, cols), lambda j, i, c_ref: (j, c_ref[0], i, 0)),
                  pl.BlockSpec((None, tr, cols), lambda j, i, c_ref: (j, i, 0))],
        out_specs=pl.BlockSpec((None, tr, cols), lambda j, i, c_ref: (j, i, 0)))
    return pl.pallas_call(body, name=name, grid_spec=grid_spec,
                          out_shape=jax.ShapeDtypeStruct((4, rows, cols), out_dtype),
                          compiler_params=_cp(("parallel", "parallel")))(c, mine, other)


def _sum_parts(name, parts):
    p, rows, cols = parts.shape
    tr = _pow2_rows(rows, cols * p)

    def body(a_ref, o_ref):
        acc = a_ref[0]
        for k in range(1, p):
            acc = acc + a_ref[k]
        o_ref[...] = acc

    return pl.pallas_call(body, name=name, grid=(rows // tr,),
                          in_specs=[pl.BlockSpec((p, tr, cols), lambda i: (0, i, 0))],
                          out_specs=pl.BlockSpec((tr, cols), lambda i: (i, 0)),
                          out_shape=jax.ShapeDtypeStruct((rows, cols), F32),
                          compiler_params=_cp(("parallel",)))(parts)


def _mm(name, a, b, *, grid, a_spec, b_spec, dims, acc_shape, out_shape, out_specs, finish,
        extras=(), extra_specs=(), aliases=None, sem=("parallel", "parallel", "arbitrary")):
    nk, ne, no = grid[2], len(extras), len(out_shape)

    def body(*refs):
        a_ref, b_ref = refs[0], refs[1]
        ex, outs = refs[2:2 + ne], refs[2 + ne:2 + ne + no]
        ids = (pl.program_id(0), pl.program_id(1))
        def prod():
            return lax.dot_general(a_ref[...], b_ref[...], dims, preferred_element_type=F32)

        if nk == 1:
            finish(prod(), ex, outs, ids)
            return
        acc = refs[2 + ne + no]
        k = pl.program_id(2)

        @pl.when(k == 0)
        def _():
            acc[...] = prod()

        @pl.when(jnp.logical_and(k > 0, k < nk - 1))
        def _():
            acc[...] += prod()

        @pl.when(k == nk - 1)
        def _():
            finish(acc[...] + prod(), ex, outs, ids)

    return pl.pallas_call(
        body, name=name, grid=grid, in_specs=[a_spec, b_spec, *extra_specs], out_specs=out_specs,
        out_shape=out_shape, scratch_shapes=[pltpu.VMEM(acc_shape, F32)] if nk > 1 else [],
        input_output_aliases=aliases or {}, compiler_params=_cp(sem))(a, b, *extras)


def _store(dtype):
    def finish(acc, ex, outs, ids):
        outs[0][...] = acc.reshape(outs[0].shape).astype(dtype)
    return finish


def _layer_norm_rows(z, g, b):
    mu = jnp.mean(z, axis=1, keepdims=True)
    zc = z - mu
    var = jnp.mean(zc * zc, axis=1, keepdims=True)
    return zc * lax.rsqrt(var + LN_EPS) * g + b


def _mm_ln(name, a, w, l, resid, g, b):
    s, kdim = a.shape
    d = w.shape[2]
    tm, tk = min(512, s), min(1024, kdim)

    def finish(acc, ex, outs, ids):
        z = acc + ALPHA * ex[0][...]
        y = _layer_norm_rows(z, ex[1][...], ex[2][...])
        outs[0][...] = z
        outs[1][...] = y
        outs[2][...] = y.astype(BF16)

    row = pl.BlockSpec((tm, d), lambda m, n, k: (m, 0))
    vec = pl.BlockSpec((1, d), lambda m, n, k: (0, 0))
    return _mm(name, a, w, grid=(s // tm, 1, kdim // tk),
               a_spec=pl.BlockSpec((tm, tk), lambda m, n, k: (m, k)),
               b_spec=pl.BlockSpec((None, tk, d), lambda m, n, k: (l, k, 0)),
               dims=NN, acc_shape=(tm, d),
               out_shape=[jax.ShapeDtypeStruct((s, d), F32), jax.ShapeDtypeStruct((s, d), F32),
                          jax.ShapeDtypeStruct((s, d), BF16)],
               out_specs=[row, row, row], finish=finish,
               extras=(resid, g.reshape(1, d), b.reshape(1, d)), extra_specs=(row, vec, vec))


def _ln_fwd(name, x, g, b):
    s, d = x.shape
    tr = min(256, s)

    def body(x_ref, g_ref, b_ref, y_ref, yb_ref):
        y = _layer_norm_rows(x_ref[...], g_ref[...], b_ref[...])
        y_ref[...] = y
        yb_ref[...] = y.astype(BF16)

    row = pl.BlockSpec((tr, d), lambda i: (i, 0))
    vec = pl.BlockSpec((1, d), lambda i: (0, 0))
    return pl.pallas_call(body, name=name, grid=(s // tr,), in_specs=[row, vec, vec], out_specs=[row, row],
                          out_shape=[jax.ShapeDtypeStruct((s, d), F32), jax.ShapeDtypeStruct((s, d), BF16)],
                          compiler_params=_cp(("parallel",)))(x, g.reshape(1, d), b.reshape(1, d))


def _ln_bwd(name, dy, z, g):
    s, d = z.shape
    tr = min(256, s)

    def body(dy_ref, z_ref, g_ref, dz_ref, dzb_ref, dg_ref, db_ref):
        @pl.when(pl.program_id(0) == 0)
        def _():
            dg_ref[...] = jnp.zeros(dg_ref.shape, F32)
            db_ref[...] = jnp.zeros(db_ref.shape, F32)

        zz, dyv = z_ref[...], dy_ref[...]
        mu = jnp.mean(zz, axis=1, keepdims=True)
        zc = zz - mu
        rstd = lax.rsqrt(jnp.mean(zc * zc, axis=1, keepdims=True) + LN_EPS)
        xhat = zc * rstd
        dg_ref[...] += jnp.sum(dyv * xhat, axis=0, keepdims=True)
        db_ref[...] += jnp.sum(dyv, axis=0, keepdims=True)
        dxh = dyv * g_ref[...]
        dz = rstd * (dxh - jnp.mean(dxh, axis=1, keepdims=True) - xhat * jnp.mean(dxh * xhat, axis=1, keepdims=True))
        dz_ref[...] = dz
        dzb_ref[...] = dz.astype(BF16)

    row = pl.BlockSpec((tr, d), lambda i: (i, 0))
    vec = pl.BlockSpec((1, d), lambda i: (0, 0))
    return pl.pallas_call(
        body, name=name, grid=(s // tr,), in_specs=[row, row, vec], out_specs=[row, row, vec, vec],
        out_shape=[jax.ShapeDtypeStruct((s, d), F32), jax.ShapeDtypeStruct((s, d), BF16),
                   jax.ShapeDtypeStruct((1, d), F32), jax.ShapeDtypeStruct((1, d), F32)],
        compiler_params=_cp(("arbitrary",)))(dy, z, g.reshape(1, d))


def _loss_head(name, y, target):
    s, d = y.shape
    tr = min(256, s)

    def body(y_ref, t_ref, loss_ref, dy_ref):
        @pl.when(pl.program_id(0) == 0)
        def _():
            loss_ref[...] = jnp.zeros(loss_ref.shape, F32)

        e = y_ref[...] - t_ref[...]
        dy_ref[...] = e * (1.0 / d)
        loss_ref[...] += jnp.sum(e * e) * (0.5 / d)

    row = pl.BlockSpec((tr, d), lambda i: (i, 0))
    return pl.pallas_call(
        body, name=name, grid=(s // tr,), in_specs=[row, row],
        out_specs=[pl.BlockSpec((8, LANE), lambda i: (0, 0)), row],
        out_shape=[jax.ShapeDtypeStruct((8, LANE), F32), jax.ShapeDtypeStruct((s, d), F32)],
        compiler_params=_cp(("arbitrary",)))(y, target)


def _scan_add(x, reverse):
    ts = x.shape[0]
    rows = _iota((ts, 1), 0)
    dist = 1
    while dist < ts:
        if reverse:
            x = x + jnp.where(rows < ts - dist, pltpu.roll(x, ts - dist, 0), 0.0)
        else:
            x = x + jnp.where(rows >= dist, pltpu.roll(x, dist, 0), 0.0)
        dist *= 2
    return x


def _scan_affine(a, b, reverse):
    ts = a.shape[0]
    rows = _iota((ts, 1), 0)
    dist = 1
    while dist < ts:
        shift = ts - dist if reverse else dist
        valid = rows < ts - dist if reverse else rows >= dist
        b = b + a * jnp.where(valid, pltpu.roll(b, shift, 0), 0.0)
        a = a * jnp.where(valid, pltpu.roll(a, shift, 0), 1.0)
        dist *= 2
    return a, b


def _cum_forget_fwd(name, fl, bias):
    s = fl.shape[0]
    ts = min(1024, s)

    def body(f_ref, b_ref, o_ref, carry):
        @pl.when(pl.program_id(0) == 0)
        def _():
            carry[...] = jnp.zeros(carry.shape, F32)

        o_ref[...] = _scan_add(_log_sigmoid(f_ref[...] + b_ref[...]), False) + carry[...]
        carry[...] = o_ref[pl.ds(ts - 1, 1), :]

    row = pl.BlockSpec((ts, LANE), lambda i: (i, 0))
    return pl.pallas_call(body, name=name, grid=(s // ts,),
                          in_specs=[row, pl.BlockSpec((1, LANE), lambda i: (0, 0))], out_specs=row,
                          out_shape=jax.ShapeDtypeStruct((s, LANE), F32),
                          scratch_shapes=[pltpu.VMEM((1, LANE), F32)],
                          compiler_params=_cp(("arbitrary",)))(fl, bias)


def _cum_forget_bwd(name, dcf, fl, bias):
    s = fl.shape[0]
    ts = min(1024, s)
    nb = s // ts

    def body(d_ref, f_ref, b_ref, o_ref, db_ref, carry):
        @pl.when(pl.program_id(0) == 0)
        def _():
            carry[...] = jnp.zeros(carry.shape, F32)
            db_ref[...] = jnp.zeros(db_ref.shape, F32)

        run = _scan_add(d_ref[...], True) + carry[...]
        carry[...] = jnp.sum(jnp.where(_iota((ts, 1), 0) == 0, run, 0.0), axis=0, keepdims=True)
        dfl = run * _sigmoid(-(f_ref[...] + b_ref[...]))
        o_ref[...] = dfl.astype(BF16)
        db_ref[...] += jnp.sum(dfl, axis=0, keepdims=True)

    row = pl.BlockSpec((ts, LANE), lambda i: (nb - 1 - i, 0))
    vec = pl.BlockSpec((1, LANE), lambda i: (0, 0))
    return pl.pallas_call(body, name=name, grid=(nb,), in_specs=[row, row, vec], out_specs=[row, vec],
                          out_shape=[jax.ShapeDtypeStruct((s, LANE), BF16), jax.ShapeDtypeStruct((1, LANE), F32)],
                          scratch_shapes=[pltpu.VMEM((1, LANE), F32)],
                          compiler_params=_cp(("arbitrary",)))(dcf, fl, bias)


def _fox_specs(s, nh, tq, tk):
    q = pl.BlockSpec((tq, HEAD), lambda h, i: (i, FQ * nh + h))
    k = pl.BlockSpec((s, HEAD), lambda h, i: (0, FK * nh + h))
    v = pl.BlockSpec((s, HEAD), lambda h, i: (0, FV * nh + h))
    col = pl.BlockSpec((None, tq, 1), lambda h, i: (h, i, 0))
    rowv = pl.BlockSpec((None, s // tk, 1, tk), lambda h, i: (h, 0, 0, 0))
    tile = pl.BlockSpec((tq, HEAD), lambda h, i: (i, h))
    full = pl.BlockSpec((s, HEAD), lambda h, i: (0, h))
    return q, k, v, col, rowv, tile, full


def _fox_tile(s):
    return min(512, s)


def _fox_scores(q, k_ref, cfq, cfr_ref, kb, tk, scale, diagonal):
    off = pl.multiple_of(kb * tk, tk)
    k = k_ref[pl.ds(off, tk), :]
    sc = lax.dot_general(q, k, NT, preferred_element_type=F32) * scale + cfq - cfr_ref[kb]
    mask = None
    if diagonal:
        mask = _iota((1, tk), 1) <= _iota((tk, 1), 0)
        sc = jnp.where(mask, sc, NEG)
    return sc, mask, k, off


def _fox_fwd(name, u, cf_col, cf_row, bw):
    s, nh = u.shape[0], bw // HEAD
    tq = tk = _fox_tile(s)
    scale = HEAD ** -0.5

    def body(q_ref, k_ref, v_ref, cfc_ref, cfr_ref, o_ref, lse_ref):
        i = pl.program_id(1)
        q, cfq = q_ref[...], cfc_ref[...]

        def step(kb, carry, diagonal=False):
            m, l, acc = carry
            sc, _, _, off = _fox_scores(q, k_ref, cfq, cfr_ref, kb, tk, scale, diagonal)
            m2 = jnp.maximum(m, jnp.max(sc, axis=1, keepdims=True))
            p = jnp.exp(sc - m2)
            al = jnp.exp(m - m2)
            return (m2, al * l + jnp.sum(p, axis=1, keepdims=True),
                    al * acc + jnp.dot(p.astype(BF16), v_ref[pl.ds(off, tk), :], preferred_element_type=F32))

        init = (jnp.full((tq, 1), NEG, F32), jnp.zeros((tq, 1), F32), jnp.zeros((tq, HEAD), F32))
        m, l, acc = step(i, lax.fori_loop(0, i, step, init), True)
        o_ref[...] = (acc / l).astype(BF16)
        lse_ref[...] = m + jnp.log(l)

    q, k, v, col, rowv, tile, _ = _fox_specs(s, nh, tq, tk)
    return pl.pallas_call(
        body, name=name, grid=(nh, s // tq), in_specs=[q, k, v, col, rowv], out_specs=[tile, col],
        out_shape=[jax.ShapeDtypeStruct((s, bw), BF16), jax.ShapeDtypeStruct((nh, s, 1), F32)],
        compiler_params=_cp(("parallel", "parallel")))(u, u, u, cf_col, cf_row)


def _fox_bwd(name, u, cf_col, cf_row, o, do, lse, bw):
    s, nh = u.shape[0], bw // HEAD
    tq = tk = _fox_tile(s)
    nq = s // tq
    scale = HEAD ** -0.5

    def body(q_ref, k_ref, v_ref, cfc_ref, cfr_ref, o_ref, do_ref, lse_ref,
             dq_ref, dk_ref, dv_ref, dcc_ref, dcr_ref, dk_s, dv_s):
        i = pl.program_id(1)

        @pl.when(i == 0)
        def _():
            dk_s[...] = jnp.zeros(dk_s.shape, F32)
            dv_s[...] = jnp.zeros(dv_s.shape, F32)
            dcr_ref[...] = jnp.zeros(dcr_ref.shape, F32)

        q, dov, cfq, lse_q = q_ref[...], do_ref[...], cfc_ref[...], lse_ref[...]
        delta = jnp.sum(dov.astype(F32) * o_ref[...].astype(F32), axis=1, keepdims=True)

        def step(kb, carry, diagonal=False):
            dq, dcq = carry
            sc, mask, k, off = _fox_scores(q, k_ref, cfq, cfr_ref, kb, tk, scale, diagonal)
            p = jnp.exp(sc - lse_q)
            if diagonal:
                p = jnp.where(mask, p, 0.0)
            dp = lax.dot_general(dov, v_ref[pl.ds(off, tk), :], NT, preferred_element_type=F32)
            ds = p * (dp - delta)
            dsb = ds.astype(BF16)
            dk_s[pl.ds(off, tk), :] += lax.dot_general(dsb, q, TN, preferred_element_type=F32)
            dv_s[pl.ds(off, tk), :] += lax.dot_general(p.astype(BF16), dov, TN, preferred_element_type=F32)
            dcr_ref[kb] += -jnp.sum(ds, axis=0, keepdims=True)
            return (dq + jnp.dot(dsb, k, preferred_element_type=F32), dcq + jnp.sum(ds, axis=1, keepdims=True))

        init = (jnp.zeros((tq, HEAD), F32), jnp.zeros((tq, 1), F32))
        dq, dcq = step(i, lax.fori_loop(0, i, step, init), True)
        dq_ref[...] = (dq * scale).astype(BF16)
        dcc_ref[...] = dcq

        @pl.when(i == nq - 1)
        def _():
            dk_ref[...] = (dk_s[...] * scale).astype(BF16)
            dv_ref[...] = dv_s[...].astype(BF16)

    q, k, v, col, rowv, tile, full = _fox_specs(s, nh, tq, tk)
    return pl.pallas_call(
        body, name=name, grid=(nh, nq), in_specs=[q, k, v, col, rowv, tile, tile, col],
        out_specs=[tile, full, full, col, rowv],
        out_shape=[jax.ShapeDtypeStruct((s, bw), BF16)] * 3
        + [jax.ShapeDtypeStruct((nh, s, 1), F32), jax.ShapeDtypeStruct((nh, s // tk, 1, tk), F32)],
        scratch_shapes=[pltpu.VMEM((s, HEAD), F32), pltpu.VMEM((s, HEAD), F32)],
        compiler_params=_cp(("arbitrary", "arbitrary")))(u, u, u, cf_col, cf_row, o, do, lse)


def _suffix_mm(x, ones_below):
    hi = x.astype(BF16)
    lo = (x - hi.astype(F32)).astype(BF16)
    return (jnp.dot(hi, ones_below, preferred_element_type=F32) + jnp.dot(lo, ones_below, preferred_element_type=F32))


def _sb_tile(q, k_ref, kb, tk, qpos, scale):
    off = pl.multiple_of(kb * tk, tk)
    k = k_ref[pl.ds(off, tk), :]
    z = lax.dot_general(q, k, NT, preferred_element_type=F32) * scale
    mask = kb * tk + _iota((1, tk), 1) < qpos
    lsn = -jnp.maximum(z, 0.0) - jnp.log(1.0 + jnp.exp(-jnp.abs(z)))
    return z, mask, lsn, jnp.where(mask, lsn, 0.0), k, off


def _sb_specs(s, nh, tq):
    q = pl.BlockSpec((tq, HEAD), lambda h, i: (i, SQ * nh + h))
    k = pl.BlockSpec((s, HEAD), lambda h, i: (0, SK * nh + h))
    v = pl.BlockSpec((s, HEAD), lambda h, i: (0, SV * nh + h))
    tile = pl.BlockSpec((tq, HEAD), lambda h, i: (i, h))
    full = pl.BlockSpec((s, HEAD), lambda h, i: (0, h))
    return q, k, v, tile, full


def _sb_fwd(name, u, bw):
    s, nh = u.shape[0], bw // HEAD
    tq = tk = 256
    scale = HEAD ** -0.5

    def body(q_ref, k_ref, v_ref, o_ref):
        i = pl.program_id(1)
        q = q_ref[...]
        qpos = i * tq + _iota((tq, 1), 0)
        later_keys = (_iota((tk, tk), 0) > _iota((tk, tk), 1)).astype(BF16)
        nk = (i * tq + tq + tk - 2) // tk

        def cond(st):
            return jnp.logical_and(st[0] < nk, st[3] > SB_DEAD)

        def step(st):
            j, c, acc, _ = st
            z, mask, lsn, lm, _, off = _sb_tile(q, k_ref, nk - 1 - j, tk, qpos, scale)
            a = jnp.where(mask, jnp.exp(lsn + z + c + _suffix_mm(lm, later_keys)), 0.0)
            acc = acc + jnp.dot(a.astype(BF16), v_ref[pl.ds(off, tk), :], preferred_element_type=F32)
            c = c + jnp.sum(lm, axis=1, keepdims=True)
            return j + 1, c, acc, jnp.max(c)

        init = (jnp.int32(0), jnp.zeros((tq, 1), F32), jnp.zeros((tq, HEAD), F32), jnp.float32(0.0))
        o_ref[...] = lax.while_loop(cond, step, init)[2].astype(BF16)

    q, k, v, tile, _ = _sb_specs(s, nh, tq)
    return pl.pallas_call(body, name=name, grid=(nh, s // tq), in_specs=[q, k, v], out_specs=tile,
                          out_shape=jax.ShapeDtypeStruct((s, bw), BF16),
                          compiler_params=_cp(("parallel", "parallel")))(u, u, u)


def _sb_bwd(name, u, do, bw):
    s, nh = u.shape[0], bw // HEAD
    tq = tk = 256
    nq = s // tq
    scale = HEAD ** -0.5

    def body(q_ref, k_ref, v_ref, do_ref, dq_ref, dk_ref, dv_ref, dk_s, dv_s):
        i = pl.program_id(1)

        @pl.when(i == 0)
        def _():
            dk_s[...] = jnp.zeros(dk_s.shape, F32)
            dv_s[...] = jnp.zeros(dv_s.shape, F32)

        q, dov = q_ref[...], do_ref[...]
        qpos = i * tq + _iota((tq, 1), 0)
        later_keys = (_iota((tk, tk), 0) > _iota((tk, tk), 1)).astype(BF16)
        this_and_later = (_iota((tk, tk), 0) >= _iota((tk, tk), 1)).astype(BF16)
        nk = (i * tq + tq + tk - 2) // tk

        def weights(j, c):
            z, mask, lsn, lm, k, off = _sb_tile(q, k_ref, nk - 1 - j, tk, qpos, scale)
            a = jnp.where(mask, jnp.exp(lsn + z + c + _suffix_mm(lm, later_keys)), 0.0)
            w = a * lax.dot_general(dov, v_ref[pl.ds(off, tk), :], NT, preferred_element_type=F32)
            return z, mask, lsn, lm, k, off, a, w

        def cond(st):
            return jnp.logical_and(st[0] < nk, st[3] > SB_DEAD)

        def step1(st):
            j, c, wc, _ = st
            _, _, _, lm, _, _, _, w = weights(j, c)
            c = c + jnp.sum(lm, axis=1, keepdims=True)
            return j + 1, c, wc + jnp.sum(w, axis=1, keepdims=True), jnp.max(c)

        zero = jnp.zeros((tq, 1), F32)
        live, _, total, _ = lax.while_loop(cond, step1, (jnp.int32(0), zero, zero, jnp.float32(0.0)))

        def step2(j, st):
            c, wc, dq = st
            z, mask, lsn, lm, k, off, a, w = weights(j, c)
            earlier = total - (wc + _suffix_mm(w, this_and_later))
            dz = jnp.where(mask, w * jnp.exp(lsn) - jnp.exp(lsn + z) * earlier, 0.0)
            dzb = dz.astype(BF16)
            dk_s[pl.ds(off, tk), :] += lax.dot_general(dzb, q, TN, preferred_element_type=F32)
            dv_s[pl.ds(off, tk), :] += lax.dot_general(a.astype(BF16), dov, TN, preferred_element_type=F32)
            return (c + jnp.sum(lm, axis=1, keepdims=True), wc + jnp.sum(w, axis=1, keepdims=True),
                    dq + jnp.dot(dzb, k, preferred_element_type=F32))

        dq = lax.fori_loop(0, live, step2, (zero, zero, jnp.zeros((tq, HEAD), F32)))[2]
        dq_ref[...] = (dq * scale).astype(BF16)

        @pl.when(i == nq - 1)
        def _():
            dk_ref[...] = (dk_s[...] * scale).astype(BF16)
            dv_ref[...] = dv_s[...].astype(BF16)

    q, k, v, tile, full = _sb_specs(s, nh, tq)
    return pl.pallas_call(
        body, name=name, grid=(nh, nq), in_specs=[q, k, v, tile], out_specs=[tile, full, full],
        out_shape=[jax.ShapeDtypeStruct((s, bw), BF16)] * 3,
        scratch_shapes=[pltpu.VMEM((s, HEAD), F32), pltpu.VMEM((s, HEAD), F32)],
        compiler_params=_cp(("arbitrary", "arbitrary")))(u, u, u, do)


def _band_onehot(r):
    kl = _iota((1, WIN), 1)
    ridx = jnp.clip(PADK + r - kl, -(CHUNK - 1), REL_CLIP) + (CHUNK - 1)
    first = (r // CHUNK) * CHUNK
    valid = jnp.logical_and(kl >= first, kl < first + BAND)
    onehot = jnp.logical_and(_iota((REL_PAD, WIN), 0) == ridx, valid)
    return onehot.astype(BF16), valid


def _bias_expand(name, table):
    def body(t_ref, o_ref):
        t = t_ref[...]
        hi = t.astype(BF16)
        r1 = t - hi.astype(F32)
        mid = r1.astype(BF16)
        lo = (r1 - mid.astype(F32)).astype(BF16)

        def row(r, carry):
            onehot, valid = _band_onehot(r)
            val = (jnp.dot(hi, onehot, preferred_element_type=F32) + jnp.dot(mid, onehot, preferred_element_type=F32)
                   + jnp.dot(lo, onehot, preferred_element_type=F32))
            o_ref[r] = jnp.where(valid, val, NEG)
            return carry

        lax.fori_loop(0, QBLK, row, 0)

    return pl.pallas_call(body, name=name, out_shape=jax.ShapeDtypeStruct((QBLK, 16, WIN), F32),
                          in_specs=[pl.BlockSpec(memory_space=pltpu.VMEM)],
                          out_specs=pl.BlockSpec(memory_space=pltpu.VMEM), compiler_params=_cp())(table)


def _bias_reduce(name, ds_rows):
    def body(x_ref, o_ref):
        def row(r, acc):
            onehot, _ = _band_onehot(r)
            x = x_ref[r]
            hi = x.astype(BF16)
            lo = (x - hi.astype(F32)).astype(BF16)
            return (acc + lax.dot_general(hi, onehot, NT, preferred_element_type=F32)
                    + lax.dot_general(lo, onehot, NT, preferred_element_type=F32))

        o_ref[...] = lax.fori_loop(0, QBLK, row, jnp.zeros((16, REL_PAD), F32))

    return pl.pallas_call(body, name=name, out_shape=jax.ShapeDtypeStruct((16, REL_PAD), F32),
                          in_specs=[pl.BlockSpec(memory_space=pltpu.VMEM)],
                          out_specs=pl.BlockSpec(memory_space=pltpu.VMEM), compiler_params=_cp())(ds_rows)


def _chunk_specs(s, nh):
    q = pl.BlockSpec((QBLK, HEAD), lambda h, i: (i, CQ * nh + h))
    kv = pl.BlockSpec((s + PADK, HEAD), lambda h, i: (0, h))
    bias = pl.BlockSpec((None, QBLK, WIN), lambda h, i: (h, 0, 0))
    tile = pl.BlockSpec((QBLK, HEAD), lambda h, i: (i, h))
    full = pl.BlockSpec((s, HEAD), lambda h, i: (0, h))
    return q, kv, bias, tile, full


def _chunk_probs(q, k_ref, b_ref, i, scale):
    off = pl.multiple_of(i * QBLK, QBLK)
    kw = k_ref[pl.ds(off, WIN), :]
    sc = lax.dot_general(q, kw, NT, preferred_element_type=F32) * scale + b_ref[...]
    sc = jnp.where(i * QBLK + _iota((1, WIN), 1) >= PADK, sc, NEG)
    p = jnp.exp(sc - jnp.max(sc, axis=1, keepdims=True))
    return p, jnp.sum(p, axis=1, keepdims=True), kw, off


def _chunk_fwd(name, u, kpad, vpad, bias, bw):
    s, nh = u.shape[0], bw // HEAD
    scale = HEAD ** -0.5

    def body(q_ref, k_ref, v_ref, b_ref, o_ref):
        p, l, _, off = _chunk_probs(q_ref[...], k_ref, b_ref, pl.program_id(1), scale)
        o = jnp.dot(p.astype(BF16), v_ref[pl.ds(off, WIN), :], preferred_element_type=F32)
        o_ref[...] = (o / l).astype(BF16)

    q, kv, bs, tile, _ = _chunk_specs(s, nh)
    return pl.pallas_call(body, name=name, grid=(nh, s // QBLK), in_specs=[q, kv, kv, bs], out_specs=tile,
                          out_shape=jax.ShapeDtypeStruct((s, bw), BF16),
                          compiler_params=_cp(("parallel", "parallel")))(u, kpad, vpad, bias)


def _chunk_bwd(name, u, kpad, vpad, bias, do, bw):
    s, nh = u.shape[0], bw // HEAD
    nq = s // QBLK
    scale = HEAD ** -0.5

    def body(q_ref, k_ref, v_ref, b_ref, do_ref, dq_ref, dk_ref, dv_ref, dss_ref, dk_s, dv_s):
        i = pl.program_id(1)

        @pl.when(i == 0)
        def _():
            dk_s[...] = jnp.zeros(dk_s.shape, F32)
            dv_s[...] = jnp.zeros(dv_s.shape, F32)
            dss_ref[...] = jnp.zeros(dss_ref.shape, F32)

        q, dov = q_ref[...], do_ref[...]
        p, l, kw, off = _chunk_probs(q, k_ref, b_ref, i, scale)
        p = p / l
        dp = lax.dot_general(dov, v_ref[pl.ds(off, WIN), :], NT, preferred_element_type=F32)
        ds = p * (dp - jnp.sum(p * dp, axis=1, keepdims=True))
        dsb = ds.astype(BF16)
        dq_ref[...] = (jnp.dot(dsb, kw, preferred_element_type=F32) * scale).astype(BF16)
        dk_s[pl.ds(off, WIN), :] += lax.dot_general(dsb, q, TN, preferred_element_type=F32)
        dv_s[pl.ds(off, WIN), :] += lax.dot_general(p.astype(BF16), dov, TN, preferred_element_type=F32)
        dss_ref[...] += ds

        @pl.when(i == nq - 1)
        def _():
            dk_ref[...] = (dk_s[pl.ds(PADK, s), :] * scale).astype(BF16)
            dv_ref[...] = dv_s[pl.ds(PADK, s), :].astype(BF16)

    q, kv, bs, tile, full = _chunk_specs(s, nh)
    return pl.pallas_call(
        body, name=name, grid=(nh, nq), in_specs=[q, kv, kv, bs, tile], out_specs=[tile, full, full, bs],
        out_shape=[jax.ShapeDtypeStruct((s, bw), BF16)] * 3 + [jax.ShapeDtypeStruct((nh, QBLK, WIN), F32)],
        scratch_shapes=[pltpu.VMEM((s + PADK, HEAD), F32), pltpu.VMEM((s + PADK, HEAD), F32)],
        compiler_params=_cp(("arbitrary", "arbitrary")))(u, kpad, vpad, bias, do)


def _gelu_parts(y):
    th = jnp.tanh(GELU_K * (y + GELU_C * y * y * y))
    return 0.5 * y * (1.0 + th), th


def _block_diag(xb16, w_ref, nh, dims):
    return jnp.concatenate(
        [lax.dot_general(xb16[:, n * HEAD:(n + 1) * HEAD], w_ref[n], dims, preferred_element_type=F32)
         for n in range(nh)], axis=1)


def _lru_gates(ext, cw_ref, cb_ref, wr_ref, br_ref, wi_ref, bi_ref, lam_ref, ts, nh):
    shifted = [pltpu.roll(ext, CONV_WIDTH - 1 - j, 0)[8:, :] if j < CONV_WIDTH - 1 else ext[8:, :]
               for j in range(CONV_WIDTH)]
    xc = cb_ref[...]
    for j in range(CONV_WIDTH):
        xc = xc + shifted[j] * cw_ref[pl.ds(j, 1), :]
    xcb = xc.astype(BF16)
    r = _sigmoid(_block_diag(xcb, wr_ref, nh, NN) + br_ref[...])
    gi = _sigmoid(_block_diag(xcb, wi_ref, nh, NN) + bi_ref[...])
    lsl = _log_sigmoid(lam_ref[...])
    la = LRU_C * r * lsl
    a = jnp.exp(la)
    e2 = jnp.exp(2.0 * la)
    mult = jnp.sqrt(-jnp.tanh(la) * (e2 + 1.0))
    return shifted, xc, xcb, r, gi, lsl, a, e2, mult


def _lru_param_specs(bw, nh):
    vec = pl.BlockSpec((1, bw), lambda i: (0, 0))
    conv = pl.BlockSpec((8, bw), lambda i: (0, 0))
    blocks = pl.BlockSpec((nh, HEAD, HEAD), lambda i: (0, 0, 0))
    return [conv, vec, blocks, vec, blocks, vec, vec]


def _lru_fwd(name, u, params, bw):
    s, nh = u.shape[0], bw // HEAD
    ts = min(512, s)

    def body(rx_ref, ry_ref, cw_ref, cb_ref, wr_ref, br_ref, wi_ref, bi_ref, lam_ref, o_ref, h_ref, tail, hcar):
        @pl.when(pl.program_id(0) == 0)
        def _():
            tail[...] = jnp.zeros(tail.shape, F32)
            hcar[...] = jnp.zeros(hcar.shape, F32)

        rx = rx_ref[...].astype(F32)
        ext = jnp.concatenate([tail[...], rx], axis=0)
        tail[...] = rx[ts - 8:, :]
        _, xc, _, _, gi, _, a, _, mult = _lru_gates(ext, cw_ref, cb_ref, wr_ref, br_ref, wi_ref, bi_ref, lam_ref, ts, nh)
        acum, bcum = _scan_affine(a, mult * (gi * xc), False)
        h_ref[...] = bcum + acum * hcar[...]
        hcar[...] = h_ref[pl.ds(ts - 1, 1), :]
        o_ref[...] = (h_ref[...] * _gelu_parts(ry_ref[...].astype(F32))[0]).astype(BF16)

    row = pl.BlockSpec((ts, bw), lambda i: (i, 0))
    return pl.pallas_call(
        body, name=name, grid=(s // ts,),
        in_specs=[pl.BlockSpec((ts, bw), lambda i: (i, RX)), pl.BlockSpec((ts, bw), lambda i: (i, RY))]
        + _lru_param_specs(bw, nh),
        out_specs=[row, row],
        out_shape=[jax.ShapeDtypeStruct((s, bw), BF16), jax.ShapeDtypeStruct((s, bw), F32)],
        scratch_shapes=[pltpu.VMEM((8, bw), F32), pltpu.VMEM((1, bw), F32)],
        compiler_params=_cp(("arbitrary",)))(u, u, *params)


def _lru_bwd(name, u, h, do, params, bw):
    s, nh = u.shape[0], bw // HEAD
    ts = min(512, s)
    nb = s // ts
    t8 = ts // 8

    def body(rx_ref, rxp_ref, ry_ref, h_ref, hp_ref, do_ref, cw_ref, cb_ref, wr_ref, br_ref, wi_ref, bi_ref, lam_ref,
             drx_ref, dry_ref, dcw_ref, dcb_ref, dwr_ref, dbr_ref, dwi_ref, dbi_ref, dlam_ref, gcar, head):
        i = pl.program_id(0)
        first = i == nb - 1

        @pl.when(i == 0)
        def _():
            gcar[...] = jnp.zeros(gcar.shape, F32)
            head[...] = jnp.zeros(head.shape, F32)
            for ref in (dcw_ref, dcb_ref, dwr_ref, dbr_ref, dwi_ref, dbi_ref, dlam_ref):
                ref[...] = jnp.zeros(ref.shape, F32)

        rows = _iota((ts, 1), 0)
        rx = rx_ref[...].astype(F32)
        before = jnp.where(first, 0.0, rxp_ref[...].astype(F32))
        ext = jnp.concatenate([before, rx], axis=0)
        shifted, xc, xcb, r, gi, lsl, a, e2, mult = _lru_gates(
            ext, cw_ref, cb_ref, wr_ref, br_ref, wi_ref, bi_ref, lam_ref, ts, nh)

        ry = ry_ref[...].astype(F32)
        gel, th = _gelu_parts(ry)
        dgel = 0.5 * (1.0 + th) + 0.5 * ry * (1.0 - th * th) * GELU_K * (1.0 + 3.0 * GELU_C * ry * ry)
        dov = do_ref[...].astype(F32)
        hv = h_ref[...]
        dry_ref[...] = (dov * hv * dgel).astype(BF16)

        coef = jnp.where(rows < ts - 1, pltpu.roll(a, ts - 1, 0), 0.0)
        dh_in = dov * gel + jnp.where(rows == ts - 1, gcar[...], 0.0)
        dh = _scan_affine(coef, dh_in, True)[1]
        gcar[...] = jnp.sum(jnp.where(rows == 0, a * dh, 0.0), axis=0, keepdims=True)

        hprev = jnp.where(first, 0.0, hp_ref[...])
        hm1 = pltpu.roll(jnp.concatenate([hprev, hv], axis=0), 1, 0)[8:, :]
        dgx = dh * mult
        dla = dh * hm1 * a - dh * gi * xc * (e2 / mult)
        dpre_r = dla * (LRU_C * lsl) * r * (1.0 - r)
        dpre_i = dgx * xc * gi * (1.0 - gi)
        dlam_ref[...] += jnp.sum(dla * r, axis=0, keepdims=True) * (LRU_C * _sigmoid(-lam_ref[...]))
        dbr_ref[...] += jnp.sum(dpre_r, axis=0, keepdims=True)
        dbi_ref[...] += jnp.sum(dpre_i, axis=0, keepdims=True)
        drb, dib = dpre_r.astype(BF16), dpre_i.astype(BF16)
        for n in range(nh):
            cols = slice(n * HEAD, (n + 1) * HEAD)
            dwr_ref[n] += lax.dot_general(xcb[:, cols], drb[:, cols], TN, preferred_element_type=F32)
            dwi_ref[n] += lax.dot_general(xcb[:, cols], dib[:, cols], TN, preferred_element_type=F32)
        dxc = dgx * gi + _block_diag(drb, wr_ref, nh, NT) + _block_diag(dib, wi_ref, nh, NT)

        dcb_ref[...] += jnp.sum(dxc, axis=0, keepdims=True)
        for j in range(CONV_WIDTH):
            dcw_ref[pl.ds(j, 1), :] += jnp.sum(dxc * shifted[j], axis=0, keepdims=True)
        ext2 = jnp.concatenate([dxc, head[...]], axis=0)
        head[...] = dxc[:8, :]
        drx = dxc * cw_ref[pl.ds(CONV_WIDTH - 1, 1), :]
        for j in range(CONV_WIDTH - 1):
            up = CONV_WIDTH - 1 - j
            drx = drx + pltpu.roll(ext2, ts + 8 - up, 0)[:ts, :] * cw_ref[pl.ds(j, 1), :]
        drx_ref[...] = drx.astype(BF16)

    def blk(col):
        return lambda i: (nb - 1 - i, col)

    def prev8(col):
        return lambda i: (jnp.maximum((nb - 1 - i) * t8 - 1, 0), col)

    vec = pl.BlockSpec((1, bw), lambda i: (0, 0))
    conv = pl.BlockSpec((8, bw), lambda i: (0, 0))
    blocks = pl.BlockSpec((nh, HEAD, HEAD), lambda i: (0, 0, 0))
    return pl.pallas_call(
        body, name=name, grid=(nb,),
        in_specs=[pl.BlockSpec((ts, bw), blk(RX)), pl.BlockSpec((8, bw), prev8(RX)), pl.BlockSpec((ts, bw), blk(RY)),
                  pl.BlockSpec((ts, bw), blk(0)), pl.BlockSpec((8, bw), prev8(0)), pl.BlockSpec((ts, bw), blk(0))]
        + _lru_param_specs(bw, nh),
        out_specs=[pl.BlockSpec((ts, bw), blk(0)), pl.BlockSpec((ts, bw), blk(0)), conv, vec, blocks, vec, blocks, vec, vec],
        out_shape=[jax.ShapeDtypeStruct((s, bw), BF16)] * 2
        + [jax.ShapeDtypeStruct((8, bw), F32), jax.ShapeDtypeStruct((1, bw), F32),
           jax.ShapeDtypeStruct((nh, HEAD, HEAD), F32), jax.ShapeDtypeStruct((1, bw), F32),
           jax.ShapeDtypeStruct((nh, HEAD, HEAD), F32), jax.ShapeDtypeStruct((1, bw), F32),
           jax.ShapeDtypeStruct((1, bw), F32)],
        scratch_shapes=[pltpu.VMEM((1, bw), F32), pltpu.VMEM((8, bw), F32)],
        compiler_params=_cp(("arbitrary",)))(u, u, u, h, h, do, *params)


def _gate_merge(name, xb, w_gate, b_gate, o_all, w_branch, l):
    s, d = xb.shape
    bw = o_all.shape[2]
    tm, tn = min(512, s), min(256, d)

    def body(x_ref, wg_ref, bg_ref, o_ref, wb_ref, m_ref, g_ref, p_ref):
        x = x_ref[...]
        acc = jnp.zeros((tm, tn), F32)
        for g in range(4):
            gate = _sigmoid(jnp.dot(x, wg_ref[g], preferred_element_type=F32) + bg_ref[g])
            proj = jnp.dot(o_ref[g], wb_ref[g], preferred_element_type=F32)
            g_ref[g] = gate
            p_ref[g] = proj.astype(BF16)
            acc = acc + gate * proj
        m_ref[...] = acc.astype(BF16)

    quad = pl.BlockSpec((4, tm, tn), lambda n, m: (0, m, n))
    return pl.pallas_call(
        body, name=name, grid=(d // tn, s // tm),
        in_specs=[pl.BlockSpec((tm, d), lambda n, m: (m, 0)),
                  pl.BlockSpec((None, 4, d, tn), lambda n, m: (l, 0, 0, n)),
                  pl.BlockSpec((None, 4, 1, tn), lambda n, m: (l, 0, 0, n)),
                  pl.BlockSpec((4, tm, bw), lambda n, m: (0, m, 0)),
                  pl.BlockSpec((None, 4, bw, tn), lambda n, m: (l, 0, 0, n))],
        out_specs=[pl.BlockSpec((tm, tn), lambda n, m: (m, n)), quad, quad],
        out_shape=[jax.ShapeDtypeStruct((s, d), BF16), jax.ShapeDtypeStruct((4, s, d), F32),
                   jax.ShapeDtypeStruct((4, s, d), BF16)],
        compiler_params=_cp(("parallel", "parallel")))(xb, w_gate, b_gate, o_all, w_branch)


def _adamw(name, w, m, v, parts):
    rows, cols = w.shape
    p = parts.shape[0]
    tr = _pow2_rows(rows, cols * max(1, p // 2), 131072)
    c1 = 1.0 - ADAM_B1 ** ADAM_STEP
    c2 = 1.0 - ADAM_B2 ** ADAM_STEP

    def body(w_ref, m_ref, v_ref, g_ref, go_ref, do_ref, mo_ref, vo_ref):
        g = g_ref[0].astype(F32)
        for k in range(1, p):
            g = g + g_ref[k].astype(F32)
        m2 = ADAM_B1 * m_ref[...] + (1.0 - ADAM_B1) * g
        v2 = ADAM_B2 * v_ref[...] + (1.0 - ADAM_B2) * (g * g)
        go_ref[...] = g
        do_ref[...] = -ADAM_LR * ((m2 / c1) / (jnp.sqrt(v2 / c2) + ADAM_EPS) + ADAM_WD * w_ref[...])
        mo_ref[...] = m2
        vo_ref[...] = v2

    row = pl.BlockSpec((tr, cols), lambda i: (i, 0))
    return pl.pallas_call(
        body, name=name, grid=(rows // tr,),
        in_specs=[row, row, row, pl.BlockSpec((p, tr, cols), lambda i: (0, i, 0))], out_specs=[row] * 4,
        out_shape=[jax.ShapeDtypeStruct((rows, cols), F32)] * 4,
        compiler_params=_cp(("parallel",)))(w, m, v, parts)


def _pack(arrays):
    flat = jnp.concatenate([a.astype(F32).reshape(-1) for a in arrays])
    pad = (-flat.shape[0]) % (8 * LANE)
    return jnp.pad(flat, (0, pad)).reshape(-1, LANE)


def _unpack(packed, shapes):
    flat, out, off = packed.reshape(-1), [], 0
    for shp in shapes:
        n = math.prod(shp)
        out.append(flat[off:off + n].reshape(shp))
        off += n
    return out


def _unshard(gathered, axis):
    block = gathered.shape[2:]
    full = jnp.swapaxes(gathered, 0, 1).reshape((N_DEV,) + block)
    full = jnp.moveaxis(full, 0, axis)
    return full.reshape(block[:axis] + (N_DEV * block[axis],) + block[axis + 1:])


def kernel(x, ln_in_g, ln_in_b, w_in, b_forget, conv_w, conv_b, w_r, b_r, w_i, b_i, lru_lambda, rel_bias, w_branch, w_gate, b_gate, w_out, ln1_g, ln1_b, w_ff1, w_ff2, ln2_g, ln2_b, loss_target, m_ln_in_g, m_ln_in_b, m_w_in, m_b_forget, m_conv_w, m_conv_b, m_w_r, m_b_r, m_w_i, m_b_i, m_lru_lambda, m_rel_bias, m_w_branch, m_w_gate, m_b_gate, m_w_out, m_ln1_g, m_ln1_b, m_w_ff1, m_w_ff2, m_ln2_g, m_ln2_b, v_ln_in_g, v_ln_in_b, v_w_in, v_b_forget, v_conv_w, v_conv_b, v_w_r, v_b_r, v_w_i, v_b_i, v_lru_lambda, v_rel_bias, v_w_branch, v_w_gate, v_b_gate, v_w_out, v_ln1_g, v_ln1_b, v_w_ff1, v_w_ff2, v_ln2_g, v_ln2_b):
    given = dict(zip(
        NAMES + ['loss_target'] + ['m_' + n for n in WEIGHTS] + ['v_' + n for n in WEIGHTS],
        (x, ln_in_g, ln_in_b, w_in, b_forget, conv_w, conv_b, w_r, b_r, w_i, b_i, lru_lambda, rel_bias, w_branch, w_gate, b_gate, w_out, ln1_g, ln1_b, w_ff1, w_ff2, ln2_g, ln2_b, loss_target, m_ln_in_g, m_ln_in_b, m_w_in, m_b_forget, m_conv_w, m_conv_b, m_w_r, m_b_r, m_w_i, m_b_i, m_lru_lambda, m_rel_bias, m_w_branch, m_w_gate, m_b_gate, m_w_out, m_ln1_g, m_ln1_b, m_w_ff1, m_w_ff2, m_ln2_g, m_ln2_b, v_ln_in_g, v_ln_in_b, v_w_in, v_b_forget, v_conv_w, v_conv_b, v_w_r, v_b_r, v_w_i, v_b_i, v_lru_lambda, v_rel_bias, v_w_branch, v_w_gate, v_b_gate, v_w_out, v_ln1_g, v_ln1_b, v_w_ff1, v_w_ff2, v_ln2_g, v_ln2_b)))

    s, d = x.shape[1], x.shape[2]
    nl = w_in.shape[0]
    bw = d // 4
    nh = bw // HEAD
    nu = 11 * bw
    rs = d // N_DEV
    dff = w_ff1.shape[2] * N_DEV
    fs = dff // N_DEV
    cs = d // N_DEV
    assert nl == DEPTH and nh * HEAD == bw and s % 256 == 0 and d % 1024 == 0

    xi, yi, ci = _position()
    dev = 4 * xi + 2 * yi + ci
    c_arr = jnp.reshape(ci, (1,)).astype(I32)

    w_main = jnp.concatenate([w_in[..., :3 * bw], w_in[..., 3 * bw + nh:]], axis=-1).astype(BF16)
    w_fcol = jnp.pad(w_in[..., 3 * bw:3 * bw + nh], ((0, 0), (0, 0), (0, LANE - nh))).astype(BF16)
    small_shapes = [conv_w.shape, rel_bias.shape, b_gate.shape]
    shards = [w_main, w_fcol, w_branch.astype(BF16), w_gate.astype(BF16), w_out.astype(BF16),
              w_ff1.astype(BF16), w_ff2.astype(BF16), _pack([conv_w, rel_bias, b_gate])]
    gathered = _exchange_cores("gather_cores", _exchange_chips("gather_chips", shards, True), True)
    W_main, W_f = _unshard(gathered[0], 1), _unshard(gathered[1], 1)
    W_branch, W_gate, W_out = _unshard(gathered[2], 3), _unshard(gathered[3], 2), _unshard(gathered[4], 1)
    W_ff1, W_ff2 = _unshard(gathered[5], 2), _unshard(gathered[6], 1)
    small = jnp.swapaxes(gathered[7], 0, 1).reshape((N_DEV,) + gathered[7].shape[2:])
    small = [_unpack(small[j], small_shapes) for j in range(N_DEV)]
    conv_w_full = jnp.concatenate([small[j][0] for j in range(N_DEV)], axis=-1)
    rel_bias_full = jnp.concatenate([small[j][1] for j in range(N_DEV)], axis=-1)
    b_gate_full = jnp.concatenate([small[j][2] for j in range(N_DEV)], axis=-1)
    b_gate4 = b_gate_full.reshape(nl, 4, 1, d)

    def lru_params(l):
        return (jnp.pad(conv_w_full[l], ((0, 8 - CONV_WIDTH), (0, 0))), conv_b[l].reshape(1, bw),
                w_r[l].astype(BF16), b_r[l].reshape(1, bw), w_i[l].astype(BF16), b_i[l].reshape(1, bw),
                lru_lambda[l].reshape(1, bw))

    def bias_rows(l):
        return jnp.pad(rel_bias_full[l], ((0, 16 - nh), (0, REL_PAD - REL_TABLE)))

    tm = min(1024, s)
    tkk = min(2048, d)

    xs = x[0]
    h0, h0b = _ln_fwd("ln_in", xs, ln_in_g, ln_in_b)
    saved = []
    cur, curb = h0, h0b
    for l in range(nl):
        u = _mm(f"w_in_{l}", curb, W_main, grid=(s // tm, nu // bw, d // tkk),
                a_spec=pl.BlockSpec((tm, tkk), lambda m, n, k: (m, k)),
                b_spec=pl.BlockSpec((None, tkk, bw), lambda m, n, k, l=l: (l, k, n)),
                dims=NN, acc_shape=(tm, bw), out_shape=[jax.ShapeDtypeStruct((s, nu), BF16)],
                out_specs=[pl.BlockSpec((tm, bw), lambda m, n, k: (m, n))], finish=_store(BF16))[0]
        fl = _mm(f"w_forget_{l}", curb, W_f, grid=(s // tm, 1, d // tkk),
                 a_spec=pl.BlockSpec((tm, tkk), lambda m, n, k: (m, k)),
                 b_spec=pl.BlockSpec((None, tkk, LANE), lambda m, n, k, l=l: (l, k, 0)),
                 dims=NN, acc_shape=(tm, LANE), out_shape=[jax.ShapeDtypeStruct((s, LANE), F32)],
                 out_specs=[pl.BlockSpec((tm, LANE), lambda m, n, k: (m, 0))], finish=_store(F32))[0]
        bf_row = jnp.pad(b_forget[l], (0, LANE - nh)).reshape(1, LANE)
        cf = _cum_forget_fwd(f"cum_forget_{l}", fl, bf_row)
        tkf = _fox_tile(s)
        cf_heads = cf[:, :nh].T
        cf_col = cf_heads.reshape(nh, s, 1)
        cf_row = cf_heads.reshape(nh, s // tkf, 1, tkf)
        o_fox, lse = _fox_fwd(f"fox_fwd_{l}", u, cf_col, cf_row, bw)
        lp = lru_params(l)
        o_lru, hstate = _lru_fwd(f"lru_fwd_{l}", u, lp, bw)
        o_sb = _sb_fwd(f"sb_fwd_{l}", u, bw)
        bias = jnp.transpose(_bias_expand(f"bias_expand_{l}", bias_rows(l)), (1, 0, 2))[:nh]
        kpad = jnp.pad(u[:, CK * bw:(CK + 1) * bw], ((PADK, 0), (0, 0)))
        vpad = jnp.pad(u[:, CV * bw:(CV + 1) * bw], ((PADK, 0), (0, 0)))
        o_ch = _chunk_fwd(f"chunk_fwd_{l}", u, kpad, vpad, bias, bw)
        o_all = jnp.stack([o_fox, o_lru, o_sb, o_ch])
        merged, gates, projs = _gate_merge(f"gate_merge_{l}", curb, W_gate, b_gate4, o_all, W_branch, l)
        z1, x1, x1b = _mm_ln(f"w_out_ln1_{l}", merged, W_out, l, cur, ln1_g[l], ln1_b[l])
        tn1 = min(1024, dff)

        def ff1_finish(acc, ex, outs, ids):
            outs[0][...] = acc.astype(BF16)
            r = jnp.maximum(acc, 0.0)
            outs[1][...] = (r * r).astype(BF16)

        hp, hid = _mm(f"w_ff1_{l}", x1b, W_ff1, grid=(s // tm, dff // tn1, d // tkk),
                      a_spec=pl.BlockSpec((tm, tkk), lambda m, n, k: (m, k)),
                      b_spec=pl.BlockSpec((None, tkk, tn1), lambda m, n, k, l=l: (l, k, n)),
                      dims=NN, acc_shape=(tm, tn1),
                      out_shape=[jax.ShapeDtypeStruct((s, dff), BF16)] * 2,
                      out_specs=[pl.BlockSpec((tm, tn1), lambda m, n, k: (m, n))] * 2, finish=ff1_finish)
        z2, x2, x2b = _mm_ln(f"w_ff2_ln2_{l}", hid, W_ff2, l, x1, ln2_g[l], ln2_b[l])
        saved.append(dict(xin=cur, xinb=curb, u=u, fl=fl, bf_row=bf_row, cf_col=cf_col, cf_row=cf_row, o_fox=o_fox,
                          lse=lse, lp=lp, hstate=hstate, bias=bias, kpad=kpad, vpad=vpad, o_all=o_all, merged=merged,
                          gates=gates, projs=projs, z1=z1, x1=x1, x1b=x1b, hp=hp, hid=hid, z2=z2))
        cur, curb = x2, x2b

    loss_tile, dcur = _loss_head("loss_head", cur, loss_target[0])
    loss = lax.psum(loss_tile[0, 0], ("x", "y", "c"))

    big = {}
    sm = {n: [None] * nl for n in ['b_forget', 'conv_w', 'conv_b', 'w_r', 'b_r', 'w_i', 'b_i', 'lru_lambda', 'rel_bias',
                                   'b_gate', 'ln1_g', 'ln1_b', 'ln2_g', 'ln2_b']}

    def split_columns(acc, ex, outs, ids):
        for j in range(N_DEV):
            outs[0][j] = acc[:, j * cs:(j + 1) * cs]

    def grad_mm(key, name, a, b, *, shape, grid, a_spec, b_spec, out_spec, acc_shape, finish=_store(F32)):
        sem = ("parallel", "parallel", "arbitrary")
        has = key in big
        extras = (big[key],) if has else ()
        res = _mm(name, a, b, grid=grid, a_spec=a_spec, b_spec=b_spec, dims=TN, acc_shape=acc_shape,
                  out_shape=[jax.ShapeDtypeStruct(shape, F32)], out_specs=[out_spec], finish=finish,
                  extras=extras, extra_specs=(ANY,) * len(extras), aliases={2: 0} if has else None, sem=sem)[0]
        big[key] = res

    tks = min(1024, s)
    tmr = min(1024, d)
    nsh = tmr // rs

    for l in reversed(range(nl)):
        sv = saved[l]
        dz2, dz2b, dg, db = _ln_bwd(f"ln2_bwd_{l}", dcur, sv['z2'], ln2_g[l])
        sm['ln2_g'][l], sm['ln2_b'][l] = dg[0], db[0]
        tn1 = min(1024, dff)

        def dhp_finish(acc, ex, outs, ids):
            outs[0][...] = (acc * (2.0 * jnp.maximum(ex[0][...].astype(F32), 0.0))).astype(BF16)

        dhp = _mm(f"d_hidden_{l}", dz2b, W_ff2, grid=(s // tm, dff // tn1, d // tkk),
                  a_spec=pl.BlockSpec((tm, tkk), lambda m, n, k: (m, k)),
                  b_spec=pl.BlockSpec((None, tn1, tkk), lambda m, n, k, l=l: (l, n, k)),
                  dims=NT, acc_shape=(tm, tn1), out_shape=[jax.ShapeDtypeStruct((s, dff), BF16)],
                  out_specs=[pl.BlockSpec((tm, tn1), lambda m, n, k: (m, n))], finish=dhp_finish,
                  extras=(sv['hp'],), extra_specs=(pl.BlockSpec((tm, tn1), lambda m, n, k: (m, n)),))[0]
        grad_mm('w_ff2', f"g_w_ff2_{l}", sv['hid'], dz2b, shape=(N_DEV, nl, fs, d), grid=(N_DEV, 1, s // tks),
                a_spec=pl.BlockSpec((tks, fs), lambda m, n, k: (k, m)),
                b_spec=pl.BlockSpec((tks, d), lambda m, n, k: (k, 0)),
                out_spec=pl.BlockSpec((None, None, fs, d), lambda m, n, k, l=l: (m, l, 0, 0)), acc_shape=(fs, d))
        grad_mm('w_ff1', f"g_w_ff1_{l}", sv['x1b'], dhp, shape=(N_DEV, nl, d, fs), grid=(d // tmr, N_DEV, s // tks),
                a_spec=pl.BlockSpec((tks, tmr), lambda m, n, k: (k, m)),
                b_spec=pl.BlockSpec((tks, fs), lambda m, n, k: (k, n)),
                out_spec=pl.BlockSpec((None, None, tmr, fs), lambda m, n, k, l=l: (n, l, m, 0)), acc_shape=(tmr, fs))
        tnd = min(1024, d)

        def resid_finish(scale):
            def finish(acc, ex, outs, ids):
                outs[0][...] = acc + scale * ex[0][...]
            return finish

        tile_md = pl.BlockSpec((tm, tnd), lambda m, n, k: (m, n))
        dx1 = _mm(f"d_x1_{l}", dhp, W_ff1, grid=(s // tm, d // tnd, dff // tkk),
                  a_spec=pl.BlockSpec((tm, tkk), lambda m, n, k: (m, k)),
                  b_spec=pl.BlockSpec((None, tnd, tkk), lambda m, n, k, l=l: (l, n, k)),
                  dims=NT, acc_shape=(tm, tnd), out_shape=[jax.ShapeDtypeStruct((s, d), F32)],
                  out_specs=[tile_md], finish=resid_finish(ALPHA), extras=(dz2,), extra_specs=(tile_md,))[0]

        dz1, dz1b, dg, db = _ln_bwd(f"ln1_bwd_{l}", dx1, sv['z1'], ln1_g[l])
        sm['ln1_g'][l], sm['ln1_b'][l] = dg[0], db[0]
        tmg, tng = min(512, s), min(512, d)

        def gate_finish(acc, ex, outs, ids):
            @pl.when(ids[1] == 0)
            def _():
                outs[2][...] = jnp.zeros(outs[2].shape, F32)

            for g in range(4):
                gate = ex[0][g]
                dproj = acc * gate
                dpre = acc * ex[1][g].astype(F32) * gate * (1.0 - gate)
                outs[0][g] = dproj.astype(BF16)
                outs[1][g] = dpre.astype(BF16)
                outs[2][g] += jnp.sum(dpre, axis=0, keepdims=True)

        quad = pl.BlockSpec((4, tmg, tng), lambda n, m, k: (0, m, n))
        dproj, dpre, dbg = _mm(
            f"d_merged_{l}", dz1b, W_out, grid=(d // tng, s // tmg, d // tkk),
            a_spec=pl.BlockSpec((tmg, tkk), lambda n, m, k: (m, k)),
            b_spec=pl.BlockSpec((None, tng, tkk), lambda n, m, k, l=l: (l, n, k)),
            dims=NT, acc_shape=(tmg, tng),
            out_shape=[jax.ShapeDtypeStruct((4, s, d), BF16), jax.ShapeDtypeStruct((4, s, d), BF16),
                       jax.ShapeDtypeStruct((4, 1, d), F32)],
            out_specs=[quad, quad, pl.BlockSpec((4, 1, tng), lambda n, m, k: (0, 0, n))], finish=gate_finish,
            extras=(sv['gates'], sv['projs']), extra_specs=(quad, quad), sem=("arbitrary", "arbitrary", "arbitrary"))
        sm['b_gate'][l] = dbg.reshape(4, d)
        grad_mm('w_out', f"g_w_out_{l}", sv['merged'], dz1b, shape=(N_DEV, nl, rs, d), grid=(d // tmr, 1, s // tks),
                a_spec=pl.BlockSpec((tks, tmr), lambda m, n, k: (k, m)),
                b_spec=pl.BlockSpec((tks, d), lambda m, n, k: (k, 0)),
                out_spec=pl.BlockSpec((nsh, None, rs, d), lambda m, n, k, l=l: (m, l, 0, 0)), acc_shape=(tmr, d))

        nm = s // tm
        do_all = _mm(f"d_branch_{l}", dproj, W_branch, grid=(4 * nm, 1, d // tkk),
                     a_spec=pl.BlockSpec((None, tm, tkk), lambda m, n, k: (m // nm, m % nm, k)),
                     b_spec=pl.BlockSpec((None, None, bw, tkk), lambda m, n, k, l=l: (l, m // nm, 0, k)),
                     dims=NT, acc_shape=(tm, bw), out_shape=[jax.ShapeDtypeStruct((4, s, bw), BF16)],
                     out_specs=[pl.BlockSpec((None, tm, bw), lambda m, n, k: (m // nm, m % nm, 0))],
                     finish=_store(BF16))[0]
        grad_mm('w_branch', f"g_w_branch_{l}", sv['o_all'], dproj, shape=(N_DEV, nl, 4, bw, cs),
                grid=(4, 1, s // tks), finish=split_columns,
                a_spec=pl.BlockSpec((None, tks, bw), lambda m, n, k: (m, k, 0)),
                b_spec=pl.BlockSpec((None, tks, d), lambda m, n, k: (m, k, 0)),
                out_spec=pl.BlockSpec((N_DEV, None, None, bw, cs), lambda m, n, k, l=l: (0, l, m, 0, 0)),
                acc_shape=(bw, d))
        grad_mm('w_gate', f"g_w_gate_{l}", sv['xinb'], dpre, shape=(N_DEV, nl, 4, rs, d), grid=(d // tmr, 4, s // tks),
                a_spec=pl.BlockSpec((tks, tmr), lambda m, n, k: (k, m)),
                b_spec=pl.BlockSpec((None, tks, d), lambda m, n, k: (n, k, 0)),
                out_spec=pl.BlockSpec((nsh, None, None, rs, d), lambda m, n, k, l=l: (m, l, n, 0, 0)),
                acc_shape=(tmr, d))
        nkg = d // tkk
        dx_gate = _mm(f"d_x_gates_{l}", dpre, W_gate, grid=(s // tm, d // tnd, 4 * nkg),
                      a_spec=pl.BlockSpec((None, tm, tkk), lambda m, n, k: (k // nkg, m, k % nkg)),
                      b_spec=pl.BlockSpec((None, None, tnd, tkk), lambda m, n, k, l=l: (l, k // nkg, n, k % nkg)),
                      dims=NT, acc_shape=(tm, tnd), out_shape=[jax.ShapeDtypeStruct((s, d), F32)],
                      out_specs=[tile_md], finish=resid_finish(ALPHA), extras=(dz1,), extra_specs=(tile_md,))[0]

        u = sv['u']
        dfq, dfk, dfv, dcc, dcr = _fox_bwd(f"fox_bwd_{l}", u, sv['cf_col'], sv['cf_row'], sv['o_fox'], do_all[0],
                                          sv['lse'], bw)
        dcf = (dcc.reshape(nh, s) + dcr.reshape(nh, s)).T
        dflb, dbf = _cum_forget_bwd(f"cum_forget_bwd_{l}", jnp.pad(dcf, ((0, 0), (0, LANE - nh))), sv['fl'],
                                    sv['bf_row'])
        sm['b_forget'][l] = dbf[0, :nh]
        drx, dry, dcw, dcb, dwr, dbr, dwi, dbi, dlam = _lru_bwd(f"lru_bwd_{l}", u, sv['hstate'], do_all[1], sv['lp'], bw)
        sm['conv_w'][l], sm['conv_b'][l], sm['w_r'][l], sm['b_r'][l] = dcw[:CONV_WIDTH], dcb[0], dwr, dbr[0]
        sm['w_i'][l], sm['b_i'][l], sm['lru_lambda'][l] = dwi, dbi[0], dlam[0]
        dsq, dsk, dsv = _sb_bwd(f"sb_bwd_{l}", u, do_all[2], bw)
        dcq, dck, dcv, dss = _chunk_bwd(f"chunk_bwd_{l}", u, sv['kpad'], sv['vpad'], sv['bias'], do_all[3], bw)
        dss_rows = jnp.pad(jnp.transpose(dss, (1, 0, 2)), ((0, 0), (0, 16 - nh), (0, 0)))
        sm['rel_bias'][l] = _bias_reduce(f"bias_reduce_{l}", dss_rows)[:nh, :REL_TABLE]
        du = jnp.concatenate([dfq, dfk, dfv, drx, dry, dsq, dsk, dsv, dcq, dck, dcv], axis=1)

        tnu = 11 * LANE
        grad_mm('w_main', f"g_w_in_{l}", sv['xinb'], du, shape=(N_DEV, nl, rs, nu), grid=(d // tmr, nu // tnu, s // tks),
                a_spec=pl.BlockSpec((tks, tmr), lambda m, n, k: (k, m)),
                b_spec=pl.BlockSpec((tks, tnu), lambda m, n, k: (k, n)),
                out_spec=pl.BlockSpec((nsh, None, rs, tnu), lambda m, n, k, l=l: (m, l, 0, n)), acc_shape=(tmr, tnu))
        grad_mm('w_f', f"g_w_forget_{l}", sv['xinb'], dflb, shape=(N_DEV, nl, rs, LANE), grid=(d // tmr, 1, s // tks),
                a_spec=pl.BlockSpec((tks, tmr), lambda m, n, k: (k, m)),
                b_spec=pl.BlockSpec((tks, LANE), lambda m, n, k: (k, 0)),
                out_spec=pl.BlockSpec((nsh, None, rs, LANE), lambda m, n, k, l=l: (m, l, 0, 0)), acc_shape=(tmr, LANE))
        tku = 11 * LANE
        dxa = _mm(f"d_x_in_{l}", du, W_main, grid=(s // tm, d // tnd, nu // tku),
                  a_spec=pl.BlockSpec((tm, tku), lambda m, n, k: (m, k)),
                  b_spec=pl.BlockSpec((None, tnd, tku), lambda m, n, k, l=l: (l, n, k)),
                  dims=NT, acc_shape=(tm, tnd), out_shape=[jax.ShapeDtypeStruct((s, d), F32)],
                  out_specs=[tile_md], finish=resid_finish(1.0), extras=(dx_gate,), extra_specs=(tile_md,))[0]
        dcur = _mm(f"d_x_forget_{l}", dflb, W_f, grid=(s // tm, d // tnd, 1),
                   a_spec=pl.BlockSpec((tm, LANE), lambda m, n, k: (m, 0)),
                   b_spec=pl.BlockSpec((None, tnd, LANE), lambda m, n, k, l=l: (l, n, 0)),
                   dims=NT, acc_shape=(tm, tnd), out_shape=[jax.ShapeDtypeStruct((s, d), F32)],
                   out_specs=[tile_md], finish=resid_finish(1.0), extras=(dxa,), extra_specs=(tile_md,))[0]

    grad_x, _, dg_in, db_in = _ln_bwd("ln_in_bwd", dcur, xs, ln_in_g)

    keys = ['w_main', 'w_f', 'w_branch', 'w_gate', 'w_out', 'w_ff1', 'w_ff2']
    slabs = {k: big[k].shape[1:] for k in keys}
    halves = [big[k].reshape((4, 2) + slabs[k]) for k in keys]
    from_core = _exchange_cores("reduce_cores", halves, False)
    partial = []
    for k, mine, other in zip(keys, halves, from_core):
        cols = slabs[k][-1]
        rows = math.prod(slabs[k]) // cols
        partial.append(_add_core_halves(f"add_cores_{k}", mine.reshape(4, 2, rows, cols), other.reshape(4, rows, cols),
                                        c_arr, BF16))
    from_chips = dict(zip(keys, _exchange_chips("reduce_chips", partial, False)))

    small_names = ['ln_in_g', 'ln_in_b', 'b_forget', 'conv_w', 'conv_b', 'w_r', 'b_r', 'w_i', 'b_i', 'lru_lambda',
                   'rel_bias', 'b_gate', 'ln1_g', 'ln1_b', 'ln2_g', 'ln2_b']
    local = {'ln_in_g': dg_in[0], 'ln_in_b': db_in[0]}
    for n in small_names[2:]:
        local[n] = jnp.stack(sm[n])
    full_shapes = [local[n].shape for n in small_names]
    packed = _pack([local[n] for n in small_names])
    every = _exchange_cores("gather_small_cores", _exchange_chips("gather_small_chips", [packed], True), True)[0]
    every = jnp.swapaxes(every, 0, 1).reshape((N_DEV,) + packed.shape)
    total = dict(zip(small_names, _unpack(_sum_parts("sum_small", every), full_shapes)))
    for n, width in (('conv_w', bw // N_DEV), ('rel_bias', REL_TABLE // N_DEV), ('b_gate', cs)):
        total[n] = lax.dynamic_slice_in_dim(total[n], dev * width, width, axis=2)

    out = {}

    def update(n, parts, shape2d):
        res = _adamw(f"adamw_{n}", given[n].reshape(shape2d), given['m_' + n].reshape(shape2d),
                     given['v_' + n].reshape(shape2d), parts)
        out[n] = [r.reshape(given[n].shape) for r in res]

    pm, pf = from_chips['w_main'], from_chips['w_f']
    parts_in = jnp.concatenate([pm[..., :3 * bw], pf[..., :nh], pm[..., 3 * bw:]], axis=-1)
    update('w_in', parts_in.reshape(4, nl * rs, w_in.shape[2]), (nl * rs, w_in.shape[2]))
    update('w_branch', from_chips['w_branch'].reshape(4, nl * 4 * bw, cs), (nl * 4 * bw, cs))
    update('w_gate', from_chips['w_gate'].reshape(4, nl * 4 * rs, d), (nl * 4 * rs, d))
    update('w_out', from_chips['w_out'].reshape(4, nl * rs, d), (nl * rs, d))
    update('w_ff1', from_chips['w_ff1'].reshape(4, nl * d, fs), (nl * d, fs))
    update('w_ff2', from_chips['w_ff2'].reshape(4, nl * fs, d), (nl * fs, d))

    small_shapes2 = [given[n].shape for n in small_names]
    res = _adamw("adamw_small", _pack([given[n] for n in small_names]), _pack([given['m_' + n] for n in small_names]),
                 _pack([given['v_' + n] for n in small_names]), _pack([total[n] for n in small_names])[None])
    res = [_unpack(r, small_shapes2) for r in res]
    for j, n in enumerate(small_names):
        out[n] = [res[k][j] for k in range(4)]

    return (loss, grad_x[None], *[out[n][0] for n in WEIGHTS], *[out[n][1] for n in WEIGHTS],
            *[out[n][2] for n in WEIGHTS], *[out[n][3] for n in WEIGHTS])
```

```python
import functools
import math

import jax
import jax.numpy as jnp
from jax import lax
from jax.experimental import pallas as pl
from jax.experimental.pallas import tpu as pltpu

F32, BF16, I32 = jnp.float32, jnp.bfloat16, jnp.int32
MESH = pl.DeviceIdType.MESH
ANY = pl.BlockSpec(memory_space=pl.ANY)

LANE = 128
VMEM_LIMIT = 56 * 1024 * 1024
N_DEV = 8

HEAD = 128
CHUNK = 64
LOOKBACK = 8
BAND = (LOOKBACK + 1) * CHUNK
QBLK = 2 * CHUNK
WIN = BAND + CHUNK
PADK = LOOKBACK * CHUNK
REL_CLIP = 256
REL_TABLE = REL_CLIP + CHUNK
REL_PAD = 384
CONV_WIDTH = 4
LRU_C = 8.0
LN_EPS = 1e-5
DEPTH = 2
ALPHA = (2.0 * DEPTH) ** 0.25
NEG = -1e30
SB_DEAD = -104.0
GELU_K = math.sqrt(2.0 / math.pi)
GELU_C = 0.044715

ADAM_LR, ADAM_B1, ADAM_B2, ADAM_EPS, ADAM_WD, ADAM_STEP = 0.001, 0.9, 0.999, 1e-08, 0.01, 10

NN = (((1,), (0,)), ((), ()))
NT = (((1,), (1,)), ((), ()))
TN = (((0,), (0,)), ((), ()))

FQ, FK, FV, RX, RY, SQ, SK, SV, CQ, CK, CV = range(11)

NAMES = ['x', 'ln_in_g', 'ln_in_b', 'w_in', 'b_forget', 'conv_w', 'conv_b', 'w_r', 'b_r', 'w_i', 'b_i', 'lru_lambda',
         'rel_bias', 'w_branch', 'w_gate', 'b_gate', 'w_out', 'ln1_g', 'ln1_b', 'w_ff1', 'w_ff2', 'ln2_g', 'ln2_b']
WEIGHTS = NAMES[1:]


def _cp(sem=None):
    return pltpu.CompilerParams(dimension_semantics=sem, vmem_limit_bytes=VMEM_LIMIT)


def _iota(shape, dim):
    return lax.broadcasted_iota(I32, shape, dim)


def _sigmoid(x):
    return 1.0 / (1.0 + jnp.exp(-x))


def _log_sigmoid(x):
    return jnp.minimum(x, 0.0) - jnp.log(1.0 + jnp.exp(-jnp.abs(x)))


def _pow2_rows(rows, cols, elems=262144):
    t = 8
    while t * 2 <= rows and t * 2 * cols <= elems and rows % (t * 2) == 0:
        t *= 2
    return t


def _position():
    return lax.axis_index("x"), lax.axis_index("y"), lax.axis_index("c")


class _Side:
    def __init__(self, operands, out_shape, sems, start, finish, aliases=(), parts=()):
        self.operands, self.out_shape, self.sems = list(operands), list(out_shape), list(sems)
        self.start, self.finish, self.aliases, self.parts = start, finish, list(aliases), parts
        self.results = None

    def set_results(self, res):
        self.results = list(res)
        off = 0
        for part in self.parts:
            part.set_results(res[off:off + len(part.out_shape)])
            off += len(part.out_shape)


def _merge_sides(a, b):
    ai, ao, asm = len(a.operands), len(a.out_shape), len(a.sems)

    def start(ins, outs, sems):
        a.start(ins[:ai], outs[:ao], sems[:asm])
        b.start(ins[ai:], outs[ao:], sems[asm:])

    def finish(ins, outs, sems):
        a.finish(ins[:ai], outs[:ao], sems[:asm])
        b.finish(ins[ai:], outs[ao:], sems[asm:])

    return _Side(a.operands + b.operands, a.out_shape + b.out_shape, a.sems + b.sems, start, finish,
                 a.aliases + [(i + ai, o + ao) for i, o in b.aliases], parts=(a, b))


def _chips_side(xs, gather):
    n = len(xs)

    def copies(ins, outs, sems, arrivals):
        send_sems, recv_sems, local_sems = sems
        x, y, c = _position()
        q = 2 * x + y
        chips = [(1 - x, y), (x, 1 - y), (1 - x, 1 - y)]

        def src(t, slot):
            return ins[t] if gather else ins[t].at[slot]

        def dst(t, slot):
            return outs[t].at[c, slot] if gather else outs[t].at[slot]

        def remote(t, j, landing):
            px, py = chips[j]
            return pltpu.make_async_remote_copy(
                src_ref=src(t, 2 * px + py), dst_ref=dst(t, landing), send_sem=send_sems.at[t, j],
                recv_sem=recv_sems.at[t, j], device_id=(px, py, c), device_id_type=MESH)

        local = [pltpu.make_async_copy(src(t, q), dst(t, q), local_sems.at[t]) for t in range(n)]
        sends = [remote(t, j, q) for t in range(n) for j in range(3)]
        if not arrivals:
            return local, sends, []
        return local, sends, [remote(t, j, 2 * px + py) for t in range(n) for j, (px, py) in enumerate(chips)]

    def start(ins, outs, sems):
        local, sends, _ = copies(ins, outs, sems, False)
        for cp in local + sends:
            cp.start()

    def finish(ins, outs, sems):
        local, sends, recvs = copies(ins, outs, sems, True)
        for cp in recvs:
            cp.wait_recv()
        for cp in sends:
            cp.wait_send()
        for cp in local:
            cp.wait()

    out_shape = [jax.ShapeDtypeStruct((2, 4) + a.shape if gather else a.shape, a.dtype) for a in xs]
    sems = [pltpu.SemaphoreType.DMA((n, 3)), pltpu.SemaphoreType.DMA((n, 3)), pltpu.SemaphoreType.DMA((n,))]
    return _Side(xs, out_shape, sems, start, finish)


def _cores_side(xs, gather):
    n = len(xs)
    m = 1 if gather else 4

    def copies(ins, outs, sems, arrivals):
        send_sems, recv_sems = sems
        x, y, c = _position()

        def remote(t, j, landing):
            s = outs[t].at[c] if gather else ins[t].at[j, 1 - c]
            d = outs[t].at[landing] if gather else outs[t].at[j]
            return pltpu.make_async_remote_copy(
                src_ref=s, dst_ref=d, send_sem=send_sems.at[t, j], recv_sem=recv_sems.at[t, j],
                device_id=(x, y, 1 - c), device_id_type=MESH)

        sends = [remote(t, j, c) for t in range(n) for j in range(m)]
        return sends, [remote(t, j, 1 - c) for t in range(n) for j in range(m)] if arrivals else []

    def start(ins, outs, sems):
        for cp in copies(ins, outs, sems, False)[0]:
            cp.start()

    def finish(ins, outs, sems):
        sends, recvs = copies(ins, outs, sems, True)
        for cp in recvs:
            cp.wait_recv()
        for cp in sends:
            cp.wait_send()

    if gather:
        out_shape = [jax.ShapeDtypeStruct(a.shape, a.dtype) for a in xs]
    else:
        out_shape = [jax.ShapeDtypeStruct((4,) + a.shape[2:], a.dtype) for a in xs]
    sems = [pltpu.SemaphoreType.DMA((n, m)), pltpu.SemaphoreType.DMA((n, m))]
    return _Side(xs, out_shape, sems, start, finish, aliases=[(t, t) for t in range(n)] if gather else [])


def _run_side(name, side):
    ni, no = len(side.operands), len(side.out_shape)

    def body(*refs):
        ins, outs, sems = refs[:ni], refs[ni:ni + no], refs[ni + no:]
        side.start(ins, outs, sems)
        side.finish(ins, outs, sems)

    side.set_results(pl.pallas_call(
        body, name=name, out_shape=side.out_shape, in_specs=[ANY] * ni, out_specs=[ANY] * no,
        input_output_aliases=dict(side.aliases), scratch_shapes=side.sems)(*side.operands))
    return side.results


def _pcall(body, operands, *, name, grid, in_specs, out_specs, out_shape, scratch_shapes=(), sem=None, aliases=None,
           side=None):
    if side is None:
        return pl.pallas_call(body, name=name, grid=grid, in_specs=list(in_specs), out_specs=list(out_specs),
                              out_shape=list(out_shape), scratch_shapes=list(scratch_shapes),
                              input_output_aliases=aliases or {}, compiler_params=_cp(sem))(*operands)
    ni, no, ns = len(in_specs), len(out_shape), len(scratch_shapes)
    si, so = len(side.operands), len(side.out_shape)

    def carrying(*refs):
        ins, sins = refs[:ni], refs[ni:ni + si]
        outs, souts = refs[ni + si:ni + si + no], refs[ni + si + no:ni + si + no + so]
        scratch, ssems = refs[ni + si + no + so:ni + si + no + so + ns], refs[ni + si + no + so + ns:]
        ids = [pl.program_id(a) for a in range(len(grid))]
        first = functools.reduce(jnp.logical_and, [i == 0 for i in ids])
        last = functools.reduce(jnp.logical_and, [i == g - 1 for i, g in zip(ids, grid)])

        @pl.when(first)
        def _():
            side.start(sins, souts, ssems)

        body(*ins, *outs, *scratch)

        @pl.when(last)
        def _():
            side.finish(sins, souts, ssems)

    joined = dict(aliases or {})
    joined.update({ni + i: no + o for i, o in side.aliases})
    res = pl.pallas_call(
        carrying, name=name, grid=grid, in_specs=[*in_specs, *[ANY] * si], out_specs=[*out_specs, *[ANY] * so],
        out_shape=[*out_shape, *side.out_shape], scratch_shapes=[*scratch_shapes, *side.sems],
        input_output_aliases=joined, compiler_params=_cp(("arbitrary",) * len(grid)))(*operands, *side.operands)
    side.set_results(res[no:])
    return res[:no]


def _add_core_halves(name, mine, other, c, out_dtype):
    _, _, rows, cols = mine.shape
    tr = _pow2_rows(rows, cols)

    def body(c_ref, a_ref, b_ref, o_ref):
        o_ref[...] = (a_ref[...] + b_ref[...]).astype(out_dtype)

    grid_spec = pltpu.PrefetchScalarGridSpec(
        num_scalar_prefetch=1, grid=(4, rows // tr),
        in_specs=[pl.BlockSpec((None, None, tr, cols), lambda j, i, c_ref: (j, c_ref[0], i, 0)),
                  pl.BlockSpec((None, tr, cols), lambda j, i, c_ref: (j, i, 0))],
        out_specs=pl.BlockSpec((None, tr, cols), lambda j, i, c_ref: (j, i, 0)))
    return pl.pallas_call(body, name=name, grid_spec=grid_spec,
                          out_shape=jax.ShapeDtypeStruct((4, rows, cols), out_dtype),
                          compiler_params=_cp(("parallel", "parallel")))(c, mine, other)


def _sum_parts(name, parts):
    p, rows, cols = parts.shape
    tr = _pow2_rows(rows, cols * p)

    def body(a_ref, o_ref):
        acc = a_ref[0]
        for k in range(1, p):
            acc = acc + a_ref[k]
        o_ref[...] = acc

    return pl.pallas_call(body, name=name, grid=(rows // tr,),
                          in_specs=[pl.BlockSpec((p, tr, cols), lambda i: (0, i, 0))],
                          out_specs=pl.BlockSpec((tr, cols), lambda i: (i, 0)),
                          out_shape=jax.ShapeDtypeStruct((rows, cols), F32),
                          compiler_params=_cp(("parallel",)))(parts)


def _mm(name, a, b, *, grid, a_spec, b_spec, dims, acc_shape, out_shape, out_specs, finish,
        extras=(), extra_specs=(), aliases=None, sem=("parallel", "parallel", "arbitrary"), side=None):
    nk, ne, no = grid[2], len(extras), len(out_shape)

    def body(*refs):
        a_ref, b_ref = refs[0], refs[1]
        ex, outs = refs[2:2 + ne], refs[2 + ne:2 + ne + no]
        ids = (pl.program_id(0), pl.program_id(1))
        def prod():
            return lax.dot_general(a_ref[...], b_ref[...], dims, preferred_element_type=F32)

        if nk == 1:
            finish(prod(), ex, outs, ids)
            return
        acc = refs[2 + ne + no]
        k = pl.program_id(2)

        @pl.when(k == 0)
        def _():
            acc[...] = prod()

        @pl.when(jnp.logical_and(k > 0, k < nk - 1))
        def _():
            acc[...] += prod()

        @pl.when(k == nk - 1)
        def _():
            finish(acc[...] + prod(), ex, outs, ids)

    return _pcall(body, (a, b, *extras), name=name, grid=grid, in_specs=[a_spec, b_spec, *extra_specs],
                  out_specs=out_specs, out_shape=out_shape,
                  scratch_shapes=[pltpu.VMEM(acc_shape, F32)] if nk > 1 else [], sem=sem, aliases=aliases, side=side)


def _store(dtype):
    def finish(acc, ex, outs, ids):
        outs[0][...] = acc.reshape(outs[0].shape).astype(dtype)
    return finish


def _layer_norm_rows(z, g, b):
    mu = jnp.mean(z, axis=1, keepdims=True)
    zc = z - mu
    var = jnp.mean(zc * zc, axis=1, keepdims=True)
    return zc * lax.rsqrt(var + LN_EPS) * g + b


def _mm_ln(name, a, w, l, resid, g, b, side=None):
    s, kdim = a.shape
    d = w.shape[2]
    tm, tk = min(512, s), min(1024, kdim)

    def finish(acc, ex, outs, ids):
        z = acc + ALPHA * ex[0][...]
        y = _layer_norm_rows(z, ex[1][...], ex[2][...])
        outs[0][...] = z
        outs[1][...] = y
        outs[2][...] = y.astype(BF16)

    row = pl.BlockSpec((tm, d), lambda m, n, k: (m, 0))
    vec = pl.BlockSpec((1, d), lambda m, n, k: (0, 0))
    return _mm(name, a, w, grid=(s // tm, 1, kdim // tk),
               a_spec=pl.BlockSpec((tm, tk), lambda m, n, k: (m, k)),
               b_spec=pl.BlockSpec((None, tk, d), lambda m, n, k: (l, k, 0)),
               dims=NN, acc_shape=(tm, d),
               out_shape=[jax.ShapeDtypeStruct((s, d), F32), jax.ShapeDtypeStruct((s, d), F32),
                          jax.ShapeDtypeStruct((s, d), BF16)],
               out_specs=[row, row, row], finish=finish,
               extras=(resid, g.reshape(1, d), b.reshape(1, d)), extra_specs=(row, vec, vec), side=side)


def _ln_fwd(name, x, g, b):
    s, d = x.shape
    tr = min(256, s)

    def body(x_ref, g_ref, b_ref, y_ref, yb_ref):
        y = _layer_norm_rows(x_ref[...], g_ref[...], b_ref[...])
        y_ref[...] = y
        yb_ref[...] = y.astype(BF16)

    row = pl.BlockSpec((tr, d), lambda i: (i, 0))
    vec = pl.BlockSpec((1, d), lambda i: (0, 0))
    return pl.pallas_call(body, name=name, grid=(s // tr,), in_specs=[row, vec, vec], out_specs=[row, row],
                          out_shape=[jax.ShapeDtypeStruct((s, d), F32), jax.ShapeDtypeStruct((s, d), BF16)],
                          compiler_params=_cp(("parallel",)))(x, g.reshape(1, d), b.reshape(1, d))


def _ln_bwd(name, dy, z, g):
    s, d = z.shape
    tr = min(256, s)

    def body(dy_ref, z_ref, g_ref, dz_ref, dzb_ref, dg_ref, db_ref):
        @pl.when(pl.program_id(0) == 0)
        def _():
            dg_ref[...] = jnp.zeros(dg_ref.shape, F32)
            db_ref[...] = jnp.zeros(db_ref.shape, F32)

        zz, dyv = z_ref[...], dy_ref[...]
        mu = jnp.mean(zz, axis=1, keepdims=True)
        zc = zz - mu
        rstd = lax.rsqrt(jnp.mean(zc * zc, axis=1, keepdims=True) + LN_EPS)
        xhat = zc * rstd
        dg_ref[...] += jnp.sum(dyv * xhat, axis=0, keepdims=True)
        db_ref[...] += jnp.sum(dyv, axis=0, keepdims=True)
        dxh = dyv * g_ref[...]
        dz = rstd * (dxh - jnp.mean(dxh, axis=1, keepdims=True) - xhat * jnp.mean(dxh * xhat, axis=1, keepdims=True))
        dz_ref[...] = dz
        dzb_ref[...] = dz.astype(BF16)

    row = pl.BlockSpec((tr, d), lambda i: (i, 0))
    vec = pl.BlockSpec((1, d), lambda i: (0, 0))
    return pl.pallas_call(
        body, name=name, grid=(s // tr,), in_specs=[row, row, vec], out_specs=[row, row, vec, vec],
        out_shape=[jax.ShapeDtypeStruct((s, d), F32), jax.ShapeDtypeStruct((s, d), BF16),
                   jax.ShapeDtypeStruct((1, d), F32), jax.ShapeDtypeStruct((1, d), F32)],
        compiler_params=_cp(("arbitrary",)))(dy, z, g.reshape(1, d))


def _loss_head(name, y, target):
    s, d = y.shape
    tr = min(256, s)

    def body(y_ref, t_ref, loss_ref, dy_ref):
        @pl.when(pl.program_id(0) == 0)
        def _():
            loss_ref[...] = jnp.zeros(loss_ref.shape, F32)

        e = y_ref[...] - t_ref[...]
        dy_ref[...] = e * (1.0 / d)
        loss_ref[...] += jnp.sum(e * e) * (0.5 / d)

    row = pl.BlockSpec((tr, d), lambda i: (i, 0))
    return pl.pallas_call(
        body, name=name, grid=(s // tr,), in_specs=[row, row],
        out_specs=[pl.BlockSpec((8, LANE), lambda i: (0, 0)), row],
        out_shape=[jax.ShapeDtypeStruct((8, LANE), F32), jax.ShapeDtypeStruct((s, d), F32)],
        compiler_params=_cp(("arbitrary",)))(y, target)


def _scan_add(x, reverse):
    ts = x.shape[0]
    rows = _iota((ts, 1), 0)
    dist = 1
    while dist < ts:
        if reverse:
            x = x + jnp.where(rows < ts - dist, pltpu.roll(x, ts - dist, 0), 0.0)
        else:
            x = x + jnp.where(rows >= dist, pltpu.roll(x, dist, 0), 0.0)
        dist *= 2
    return x


def _scan_affine(a, b, reverse):
    ts = a.shape[0]
    rows = _iota((ts, 1), 0)
    dist = 1
    while dist < ts:
        shift = ts - dist if reverse else dist
        valid = rows < ts - dist if reverse else rows >= dist
        b = b + a * jnp.where(valid, pltpu.roll(b, shift, 0), 0.0)
        a = a * jnp.where(valid, pltpu.roll(a, shift, 0), 1.0)
        dist *= 2
    return a, b


def _cum_forget_fwd(name, fl, bias):
    s = fl.shape[0]
    ts = min(1024, s)

    def body(f_ref, b_ref, o_ref, carry):
        @pl.when(pl.program_id(0) == 0)
        def _():
            carry[...] = jnp.zeros(carry.shape, F32)

        o_ref[...] = _scan_add(_log_sigmoid(f_ref[...] + b_ref[...]), False) + carry[...]
        carry[...] = o_ref[pl.ds(ts - 1, 1), :]

    row = pl.BlockSpec((ts, LANE), lambda i: (i, 0))
    return pl.pallas_call(body, name=name, grid=(s // ts,),
                          in_specs=[row, pl.BlockSpec((1, LANE), lambda i: (0, 0))], out_specs=row,
                          out_shape=jax.ShapeDtypeStruct((s, LANE), F32),
                          scratch_shapes=[pltpu.VMEM((1, LANE), F32)],
                          compiler_params=_cp(("arbitrary",)))(fl, bias)


def _cum_forget_bwd(name, dcf, fl, bias):
    s = fl.shape[0]
    ts = min(1024, s)
    nb = s // ts

    def body(d_ref, f_ref, b_ref, o_ref, db_ref, carry):
        @pl.when(pl.program_id(0) == 0)
        def _():
            carry[...] = jnp.zeros(carry.shape, F32)
            db_ref[...] = jnp.zeros(db_ref.shape, F32)

        run = _scan_add(d_ref[...], True) + carry[...]
        carry[...] = jnp.sum(jnp.where(_iota((ts, 1), 0) == 0, run, 0.0), axis=0, keepdims=True)
        dfl = run * _sigmoid(-(f_ref[...] + b_ref[...]))
        o_ref[...] = dfl.astype(BF16)
        db_ref[...] += jnp.sum(dfl, axis=0, keepdims=True)

    row = pl.BlockSpec((ts, LANE), lambda i: (nb - 1 - i, 0))
    vec = pl.BlockSpec((1, LANE), lambda i: (0, 0))
    return pl.pallas_call(body, name=name, grid=(nb,), in_specs=[row, row, vec], out_specs=[row, vec],
                          out_shape=[jax.ShapeDtypeStruct((s, LANE), BF16), jax.ShapeDtypeStruct((1, LANE), F32)],
                          scratch_shapes=[pltpu.VMEM((1, LANE), F32)],
                          compiler_params=_cp(("arbitrary",)))(dcf, fl, bias)


def _fox_specs(s, nh, tq, tk):
    q = pl.BlockSpec((tq, HEAD), lambda h, i: (i, FQ * nh + h))
    k = pl.BlockSpec((s, HEAD), lambda h, i: (0, FK * nh + h))
    v = pl.BlockSpec((s, HEAD), lambda h, i: (0, FV * nh + h))
    col = pl.BlockSpec((None, tq, 1), lambda h, i: (h, i, 0))
    rowv = pl.BlockSpec((None, s // tk, 1, tk), lambda h, i: (h, 0, 0, 0))
    tile = pl.BlockSpec((tq, HEAD), lambda h, i: (i, h))
    full = pl.BlockSpec((s, HEAD), lambda h, i: (0, h))
    return q, k, v, col, rowv, tile, full


def _fox_tile(s):
    return min(512, s)


def _fox_scores(q, k_ref, cfq, cfr_ref, kb, tk, scale, diagonal):
    off = pl.multiple_of(kb * tk, tk)
    k = k_ref[pl.ds(off, tk), :]
    sc = lax.dot_general(q, k, NT, preferred_element_type=F32) * scale + cfq - cfr_ref[kb]
    mask = None
    if diagonal:
        mask = _iota((1, tk), 1) <= _iota((tk, 1), 0)
        sc = jnp.where(mask, sc, NEG)
    return sc, mask, k, off


def _fox_fwd(name, u, cf_col, cf_row, bw, side=None):
    s, nh = u.shape[0], bw // HEAD
    tq = tk = _fox_tile(s)
    scale = HEAD ** -0.5

    def body(q_ref, k_ref, v_ref, cfc_ref, cfr_ref, o_ref, lse_ref):
        i = pl.program_id(1)
        q, cfq = q_ref[...], cfc_ref[...]

        def step(kb, carry, diagonal=False):
            m, l, acc = carry
            sc, _, _, off = _fox_scores(q, k_ref, cfq, cfr_ref, kb, tk, scale, diagonal)
            m2 = jnp.maximum(m, jnp.max(sc, axis=1, keepdims=True))
            p = jnp.exp(sc - m2)
            al = jnp.exp(m - m2)
            return (m2, al * l + jnp.sum(p, axis=1, keepdims=True),
                    al * acc + jnp.dot(p.astype(BF16), v_ref[pl.ds(off, tk), :], preferred_element_type=F32))

        init = (jnp.full((tq, 1), NEG, F32), jnp.zeros((tq, 1), F32), jnp.zeros((tq, HEAD), F32))
        m, l, acc = step(i, lax.fori_loop(0, i, step, init), True)
        o_ref[...] = (acc / l).astype(BF16)
        lse_ref[...] = m + jnp.log(l)

    q, k, v, col, rowv, tile, _ = _fox_specs(s, nh, tq, tk)
    return _pcall(body, (u, u, u, cf_col, cf_row), name=name, grid=(nh, s // tq), in_specs=[q, k, v, col, rowv],
                  out_specs=[tile, col],
                  out_shape=[jax.ShapeDtypeStruct((s, bw), BF16), jax.ShapeDtypeStruct((nh, s, 1), F32)],
                  sem=("parallel", "parallel"), side=side)


def _fox_bwd(name, u, cf_col, cf_row, o, do, lse, bw, side=None):
    s, nh = u.shape[0], bw // HEAD
    tq = tk = _fox_tile(s)
    nq = s // tq
    scale = HEAD ** -0.5

    def body(q_ref, k_ref, v_ref, cfc_ref, cfr_ref, o_ref, do_ref, lse_ref,
             dq_ref, dk_ref, dv_ref, dcc_ref, dcr_ref, dk_s, dv_s):
        i = pl.program_id(1)

        @pl.when(i == 0)
        def _():
            dk_s[...] = jnp.zeros(dk_s.shape, F32)
            dv_s[...] = jnp.zeros(dv_s.shape, F32)
            dcr_ref[...] = jnp.zeros(dcr_ref.shape, F32)

        q, dov, cfq, lse_q = q_ref[...], do_ref[...], cfc_ref[...], lse_ref[...]
        delta = jnp.sum(dov.astype(F32) * o_ref[...].astype(F32), axis=1, keepdims=True)

        def step(kb, carry, diagonal=False):
            dq, dcq = carry
            sc, mask, k, off = _fox_scores(q, k_ref, cfq, cfr_ref, kb, tk, scale, diagonal)
            p = jnp.exp(sc - lse_q)
            if diagonal:
                p = jnp.where(mask, p, 0.0)
            dp = lax.dot_general(dov, v_ref[pl.ds(off, tk), :], NT, preferred_element_type=F32)
            ds = p * (dp - delta)
            dsb = ds.astype(BF16)
            dk_s[pl.ds(off, tk), :] += lax.dot_general(dsb, q, TN, preferred_element_type=F32)
            dv_s[pl.ds(off, tk), :] += lax.dot_general(p.astype(BF16), dov, TN, preferred_element_type=F32)
            dcr_ref[kb] += -jnp.sum(ds, axis=0, keepdims=True)
            return (dq + jnp.dot(dsb, k, preferred_element_type=F32), dcq + jnp.sum(ds, axis=1, keepdims=True))

        init = (jnp.zeros((tq, HEAD), F32), jnp.zeros((tq, 1), F32))
        dq, dcq = step(i, lax.fori_loop(0, i, step, init), True)
        dq_ref[...] = (dq * scale).astype(BF16)
        dcc_ref[...] = dcq

        @pl.when(i == nq - 1)
        def _():
            dk_ref[...] = (dk_s[...] * scale).astype(BF16)
            dv_ref[...] = dv_s[...].astype(BF16)

    q, k, v, col, rowv, tile, full = _fox_specs(s, nh, tq, tk)
    return _pcall(
        body, (u, u, u, cf_col, cf_row, o, do, lse), name=name, grid=(nh, nq),
        in_specs=[q, k, v, col, rowv, tile, tile, col], out_specs=[tile, full, full, col, rowv],
        out_shape=[jax.ShapeDtypeStruct((s, bw), BF16)] * 3
        + [jax.ShapeDtypeStruct((nh, s, 1), F32), jax.ShapeDtypeStruct((nh, s // tk, 1, tk), F32)],
        scratch_shapes=[pltpu.VMEM((s, HEAD), F32), pltpu.VMEM((s, HEAD), F32)],
        sem=("arbitrary", "arbitrary"), side=side)


def _suffix_mm(x, ones_below):
    hi = x.astype(BF16)
    lo = (x - hi.astype(F32)).astype(BF16)
    return (jnp.dot(hi, ones_below, preferred_element_type=F32) + jnp.dot(lo, ones_below, preferred_element_type=F32))


def _sb_tile(q, k_ref, kb, tk, qpos, scale):
    off = pl.multiple_of(kb * tk, tk)
    k = k_ref[pl.ds(off, tk), :]
    z = lax.dot_general(q, k, NT, preferred_element_type=F32) * scale
    mask = kb * tk + _iota((1, tk), 1) < qpos
    lsn = -jnp.maximum(z, 0.0) - jnp.log(1.0 + jnp.exp(-jnp.abs(z)))
    return z, mask, lsn, jnp.where(mask, lsn, 0.0), k, off


def _sb_specs(s, nh, tq):
    q = pl.BlockSpec((tq, HEAD), lambda h, i: (i, SQ * nh + h))
    k = pl.BlockSpec((s, HEAD), lambda h, i: (0, SK * nh + h))
    v = pl.BlockSpec((s, HEAD), lambda h, i: (0, SV * nh + h))
    tile = pl.BlockSpec((tq, HEAD), lambda h, i: (i, h))
    full = pl.BlockSpec((s, HEAD), lambda h, i: (0, h))
    return q, k, v, tile, full


def _sb_fwd(name, u, bw, side=None):
    s, nh = u.shape[0], bw // HEAD
    tq = tk = 256
    scale = HEAD ** -0.5

    def body(q_ref, k_ref, v_ref, o_ref):
        i = pl.program_id(1)
        q = q_ref[...]
        qpos = i * tq + _iota((tq, 1), 0)
        later_keys = (_iota((tk, tk), 0) > _iota((tk, tk), 1)).astype(BF16)
        nk = (i * tq + tq + tk - 2) // tk

        def cond(st):
            return jnp.logical_and(st[0] < nk, st[3] > SB_DEAD)

        def step(st):
            j, c, acc, _ = st
            z, mask, lsn, lm, _, off = _sb_tile(q, k_ref, nk - 1 - j, tk, qpos, scale)
            a = jnp.where(mask, jnp.exp(lsn + z + c + _suffix_mm(lm, later_keys)), 0.0)
            acc = acc + jnp.dot(a.astype(BF16), v_ref[pl.ds(off, tk), :], preferred_element_type=F32)
            c = c + jnp.sum(lm, axis=1, keepdims=True)
            return j + 1, c, acc, jnp.max(c)

        init = (jnp.int32(0), jnp.zeros((tq, 1), F32), jnp.zeros((tq, HEAD), F32), jnp.float32(0.0))
        o_ref[...] = lax.while_loop(cond, step, init)[2].astype(BF16)

    q, k, v, tile, _ = _sb_specs(s, nh, tq)
    return _pcall(body, (u, u, u), name=name, grid=(nh, s // tq), in_specs=[q, k, v], out_specs=[tile],
                  out_shape=[jax.ShapeDtypeStruct((s, bw), BF16)], sem=("parallel", "parallel"), side=side)[0]


def _sb_bwd(name, u, do, bw):
    s, nh = u.shape[0], bw // HEAD
    tq = tk = 256
    nq = s // tq
    scale = HEAD ** -0.5

    def body(q_ref, k_ref, v_ref, do_ref, dq_ref, dk_ref, dv_ref, dk_s, dv_s):
        i = pl.program_id(1)

        @pl.when(i == 0)
        def _():
            dk_s[...] = jnp.zeros(dk_s.shape, F32)
            dv_s[...] = jnp.zeros(dv_s.shape, F32)

        q, dov = q_ref[...], do_ref[...]
        qpos = i * tq + _iota((tq, 1), 0)
        later_keys = (_iota((tk, tk), 0) > _iota((tk, tk), 1)).astype(BF16)
        this_and_later = (_iota((tk, tk), 0) >= _iota((tk, tk), 1)).astype(BF16)
        nk = (i * tq + tq + tk - 2) // tk

        def weights(j, c):
            z, mask, lsn, lm, k, off = _sb_tile(q, k_ref, nk - 1 - j, tk, qpos, scale)
            a = jnp.where(mask, jnp.exp(lsn + z + c + _suffix_mm(lm, later_keys)), 0.0)
            w = a * lax.dot_general(dov, v_ref[pl.ds(off, tk), :], NT, preferred_element_type=F32)
            return z, mask, lsn, lm, k, off, a, w

        def cond(st):
            return jnp.logical_and(st[0] < nk, st[3] > SB_DEAD)

        def step1(st):
            j, c, wc, _ = st
            _, _, _, lm, _, _, _, w = weights(j, c)
            c = c + jnp.sum(lm, axis=1, keepdims=True)
            return j + 1, c, wc + jnp.sum(w, axis=1, keepdims=True), jnp.max(c)

        zero = jnp.zeros((tq, 1), F32)
        live, _, total, _ = lax.while_loop(cond, step1, (jnp.int32(0), zero, zero, jnp.float32(0.0)))

        def step2(j, st):
            c, wc, dq = st
            z, mask, lsn, lm, k, off, a, w = weights(j, c)
            earlier = total - (wc + _suffix_mm(w, this_and_later))
            dz = jnp.where(mask, w * jnp.exp(lsn) - jnp.exp(lsn + z) * earlier, 0.0)
            dzb = dz.astype(BF16)
            dk_s[pl.ds(off, tk), :] += lax.dot_general(dzb, q, TN, preferred_element_type=F32)
            dv_s[pl.ds(off, tk), :] += lax.dot_general(a.astype(BF16), dov, TN, preferred_element_type=F32)
            return (c + jnp.sum(lm, axis=1, keepdims=True), wc + jnp.sum(w, axis=1, keepdims=True),
                    dq + jnp.dot(dzb, k, preferred_element_type=F32))

        dq = lax.fori_loop(0, live, step2, (zero, zero, jnp.zeros((tq, HEAD), F32)))[2]
        dq_ref[...] = (dq * scale).astype(BF16)

        @pl.when(i == nq - 1)
        def _():
            dk_ref[...] = (dk_s[...] * scale).astype(BF16)
            dv_ref[...] = dv_s[...].astype(BF16)

    q, k, v, tile, full = _sb_specs(s, nh, tq)
    return pl.pallas_call(
        body, name=name, grid=(nh, nq), in_specs=[q, k, v, tile], out_specs=[tile, full, full],
        out_shape=[jax.ShapeDtypeStruct((s, bw), BF16)] * 3,
        scratch_shapes=[pltpu.VMEM((s, HEAD), F32), pltpu.VMEM((s, HEAD), F32)],
        compiler_params=_cp(("arbitrary", "arbitrary")))(u, u, u, do)


def _band_onehot(r):
    kl = _iota((1, WIN), 1)
    ridx = jnp.clip(PADK + r - kl, -(CHUNK - 1), REL_CLIP) + (CHUNK - 1)
    first = (r // CHUNK) * CHUNK
    valid = jnp.logical_and(kl >= first, kl < first + BAND)
    onehot = jnp.logical_and(_iota((REL_PAD, WIN), 0) == ridx, valid)
    return onehot.astype(BF16), valid


def _bias_expand(name, table):
    def body(t_ref, o_ref):
        t = t_ref[...]
        hi = t.astype(BF16)
        r1 = t - hi.astype(F32)
        mid = r1.astype(BF16)
        lo = (r1 - mid.astype(F32)).astype(BF16)

        def row(r, carry):
            onehot, valid = _band_onehot(r)
            val = (jnp.dot(hi, onehot, preferred_element_type=F32) + jnp.dot(mid, onehot, preferred_element_type=F32)
                   + jnp.dot(lo, onehot, preferred_element_type=F32))
            o_ref[r] = jnp.where(valid, val, NEG)
            return carry

        lax.fori_loop(0, QBLK, row, 0)

    return pl.pallas_call(body, name=name, out_shape=jax.ShapeDtypeStruct((QBLK, 16, WIN), F32),
                          in_specs=[pl.BlockSpec(memory_space=pltpu.VMEM)],
                          out_specs=pl.BlockSpec(memory_space=pltpu.VMEM), compiler_params=_cp())(table)


def _bias_reduce(name, ds_rows):
    def body(x_ref, o_ref):
        def row(r, acc):
            onehot, _ = _band_onehot(r)
            x = x_ref[r]
            hi = x.astype(BF16)
            lo = (x - hi.astype(F32)).astype(BF16)
            return (acc + lax.dot_general(hi, onehot, NT, preferred_element_type=F32)
                    + lax.dot_general(lo, onehot, NT, preferred_element_type=F32))

        o_ref[...] = lax.fori_loop(0, QBLK, row, jnp.zeros((16, REL_PAD), F32))

    return pl.pallas_call(body, name=name, out_shape=jax.ShapeDtypeStruct((16, REL_PAD), F32),
                          in_specs=[pl.BlockSpec(memory_space=pltpu.VMEM)],
                          out_specs=pl.BlockSpec(memory_space=pltpu.VMEM), compiler_params=_cp())(ds_rows)


def _chunk_specs(s, nh):
    q = pl.BlockSpec((QBLK, HEAD), lambda h, i: (i, CQ * nh + h))
    kv = pl.BlockSpec((s + PADK, HEAD), lambda h, i: (0, h))
    bias = pl.BlockSpec((None, QBLK, WIN), lambda h, i: (h, 0, 0))
    tile = pl.BlockSpec((QBLK, HEAD), lambda h, i: (i, h))
    full = pl.BlockSpec((s, HEAD), lambda h, i: (0, h))
    return q, kv, bias, tile, full


def _chunk_probs(q, k_ref, b_ref, i, scale):
    off = pl.multiple_of(i * QBLK, QBLK)
    kw = k_ref[pl.ds(off, WIN), :]
    sc = lax.dot_general(q, kw, NT, preferred_element_type=F32) * scale + b_ref[...]
    sc = jnp.where(i * QBLK + _iota((1, WIN), 1) >= PADK, sc, NEG)
    p = jnp.exp(sc - jnp.max(sc, axis=1, keepdims=True))
    return p, jnp.sum(p, axis=1, keepdims=True), kw, off


def _chunk_fwd(name, u, kpad, vpad, bias, bw, side=None):
    s, nh = u.shape[0], bw // HEAD
    scale = HEAD ** -0.5

    def body(q_ref, k_ref, v_ref, b_ref, o_ref):
        p, l, _, off = _chunk_probs(q_ref[...], k_ref, b_ref, pl.program_id(1), scale)
        o = jnp.dot(p.astype(BF16), v_ref[pl.ds(off, WIN), :], preferred_element_type=F32)
        o_ref[...] = (o / l).astype(BF16)

    q, kv, bs, tile, _ = _chunk_specs(s, nh)
    return _pcall(body, (u, kpad, vpad, bias), name=name, grid=(nh, s // QBLK), in_specs=[q, kv, kv, bs],
                  out_specs=[tile], out_shape=[jax.ShapeDtypeStruct((s, bw), BF16)], sem=("parallel", "parallel"),
                  side=side)[0]


def _chunk_bwd(name, u, kpad, vpad, bias, do, bw):
    s, nh = u.shape[0], bw // HEAD
    nq = s // QBLK
    scale = HEAD ** -0.5

    def body(q_ref, k_ref, v_ref, b_ref, do_ref, dq_ref, dk_ref, dv_ref, dss_ref, dk_s, dv_s):
        i = pl.program_id(1)

        @pl.when(i == 0)
        def _():
            dk_s[...] = jnp.zeros(dk_s.shape, F32)
            dv_s[...] = jnp.zeros(dv_s.shape, F32)
            dss_ref[...] = jnp.zeros(dss_ref.shape, F32)

        q, dov = q_ref[...], do_ref[...]
        p, l, kw, off = _chunk_probs(q, k_ref, b_ref, i, scale)
        p = p / l
        dp = lax.dot_general(dov, v_ref[pl.ds(off, WIN), :], NT, preferred_element_type=F32)
        ds = p * (dp - jnp.sum(p * dp, axis=1, keepdims=True))
        dsb = ds.astype(BF16)
        dq_ref[...] = (jnp.dot(dsb, kw, preferred_element_type=F32) * scale).astype(BF16)
        dk_s[pl.ds(off, WIN), :] += lax.dot_general(dsb, q, TN, preferred_element_type=F32)
        dv_s[pl.ds(off, WIN), :] += lax.dot_general(p.astype(BF16), dov, TN, preferred_element_type=F32)
        dss_ref[...] += ds

        @pl.when(i == nq - 1)
        def _():
            dk_ref[...] = (dk_s[pl.ds(PADK, s), :] * scale).astype(BF16)
            dv_ref[...] = dv_s[pl.ds(PADK, s), :].astype(BF16)

    q, kv, bs, tile, full = _chunk_specs(s, nh)
    return pl.pallas_call(
        body, name=name, grid=(nh, nq), in_specs=[q, kv, kv, bs, tile], out_specs=[tile, full, full, bs],
        out_shape=[jax.ShapeDtypeStruct((s, bw), BF16)] * 3 + [jax.ShapeDtypeStruct((nh, QBLK, WIN), F32)],
        scratch_shapes=[pltpu.VMEM((s + PADK, HEAD), F32), pltpu.VMEM((s + PADK, HEAD), F32)],
        compiler_params=_cp(("arbitrary", "arbitrary")))(u, kpad, vpad, bias, do)


def _gelu_parts(y):
    th = jnp.tanh(GELU_K * (y + GELU_C * y * y * y))
    return 0.5 * y * (1.0 + th), th


def _block_diag(xb16, w_ref, nh, dims):
    return jnp.concatenate(
        [lax.dot_general(xb16[:, n * HEAD:(n + 1) * HEAD], w_ref[n], dims, preferred_element_type=F32)
         for n in range(nh)], axis=1)


def _lru_gates(ext, cw_ref, cb_ref, wr_ref, br_ref, wi_ref, bi_ref, lam_ref, ts, nh):
    shifted = [pltpu.roll(ext, CONV_WIDTH - 1 - j, 0)[8:, :] if j < CONV_WIDTH - 1 else ext[8:, :]
               for j in range(CONV_WIDTH)]
    xc = cb_ref[...]
    for j in range(CONV_WIDTH):
        xc = xc + shifted[j] * cw_ref[pl.ds(j, 1), :]
    xcb = xc.astype(BF16)
    r = _sigmoid(_block_diag(xcb, wr_ref, nh, NN) + br_ref[...])
    gi = _sigmoid(_block_diag(xcb, wi_ref, nh, NN) + bi_ref[...])
    lsl = _log_sigmoid(lam_ref[...])
    la = LRU_C * r * lsl
    a = jnp.exp(la)
    e2 = jnp.exp(2.0 * la)
    mult = jnp.sqrt(-jnp.tanh(la) * (e2 + 1.0))
    return shifted, xc, xcb, r, gi, lsl, a, e2, mult


def _lru_param_specs(bw, nh):
    vec = pl.BlockSpec((1, bw), lambda i: (0, 0))
    conv = pl.BlockSpec((8, bw), lambda i: (0, 0))
    blocks = pl.BlockSpec((nh, HEAD, HEAD), lambda i: (0, 0, 0))
    return [conv, vec, blocks, vec, blocks, vec, vec]


def _lru_fwd(name, u, params, bw):
    s, nh = u.shape[0], bw // HEAD
    ts = min(512, s)

    def body(rx_ref, ry_ref, cw_ref, cb_ref, wr_ref, br_ref, wi_ref, bi_ref, lam_ref, o_ref, h_ref, tail, hcar):
        @pl.when(pl.program_id(0) == 0)
        def _():
            tail[...] = jnp.zeros(tail.shape, F32)
            hcar[...] = jnp.zeros(hcar.shape, F32)

        rx = rx_ref[...].astype(F32)
        ext = jnp.concatenate([tail[...], rx], axis=0)
        tail[...] = rx[ts - 8:, :]
        _, xc, _, _, gi, _, a, _, mult = _lru_gates(ext, cw_ref, cb_ref, wr_ref, br_ref, wi_ref, bi_ref, lam_ref, ts, nh)
        acum, bcum = _scan_affine(a, mult * (gi * xc), False)
        h_ref[...] = bcum + acum * hcar[...]
        hcar[...] = h_ref[pl.ds(ts - 1, 1), :]
        o_ref[...] = (h_ref[...] * _gelu_parts(ry_ref[...].astype(F32))[0]).astype(BF16)

    row = pl.BlockSpec((ts, bw), lambda i: (i, 0))
    return pl.pallas_call(
        body, name=name, grid=(s // ts,),
        in_specs=[pl.BlockSpec((ts, bw), lambda i: (i, RX)), pl.BlockSpec((ts, bw), lambda i: (i, RY))]
        + _lru_param_specs(bw, nh),
        out_specs=[row, row],
        out_shape=[jax.ShapeDtypeStruct((s, bw), BF16), jax.ShapeDtypeStruct((s, bw), F32)],
        scratch_shapes=[pltpu.VMEM((8, bw), F32), pltpu.VMEM((1, bw), F32)],
        compiler_params=_cp(("arbitrary",)))(u, u, *params)


def _lru_bwd(name, u, h, do, params, bw):
    s, nh = u.shape[0], bw // HEAD
    ts = min(512, s)
    nb = s // ts
    t8 = ts // 8

    def body(rx_ref, rxp_ref, ry_ref, h_ref, hp_ref, do_ref, cw_ref, cb_ref, wr_ref, br_ref, wi_ref, bi_ref, lam_ref,
             drx_ref, dry_ref, dcw_ref, dcb_ref, dwr_ref, dbr_ref, dwi_ref, dbi_ref, dlam_ref, gcar, head):
        i = pl.program_id(0)
        first = i == nb - 1

        @pl.when(i == 0)
        def _():
            gcar[...] = jnp.zeros(gcar.shape, F32)
            head[...] = jnp.zeros(head.shape, F32)
            for ref in (dcw_ref, dcb_ref, dwr_ref, dbr_ref, dwi_ref, dbi_ref, dlam_ref):
                ref[...] = jnp.zeros(ref.shape, F32)

        rows = _iota((ts, 1), 0)
        rx = rx_ref[...].astype(F32)
        before = jnp.where(first, 0.0, rxp_ref[...].astype(F32))
        ext = jnp.concatenate([before, rx], axis=0)
        shifted, xc, xcb, r, gi, lsl, a, e2, mult = _lru_gates(
            ext, cw_ref, cb_ref, wr_ref, br_ref, wi_ref, bi_ref, lam_ref, ts, nh)

        ry = ry_ref[...].astype(F32)
        gel, th = _gelu_parts(ry)
        dgel = 0.5 * (1.0 + th) + 0.5 * ry * (1.0 - th * th) * GELU_K * (1.0 + 3.0 * GELU_C * ry * ry)
        dov = do_ref[...].astype(F32)
        hv = h_ref[...]
        dry_ref[...] = (dov * hv * dgel).astype(BF16)

        coef = jnp.where(rows < ts - 1, pltpu.roll(a, ts - 1, 0), 0.0)
        dh_in = dov * gel + jnp.where(rows == ts - 1, gcar[...], 0.0)
        dh = _scan_affine(coef, dh_in, True)[1]
        gcar[...] = jnp.sum(jnp.where(rows == 0, a * dh, 0.0), axis=0, keepdims=True)

        hprev = jnp.where(first, 0.0, hp_ref[...])
        hm1 = pltpu.roll(jnp.concatenate([hprev, hv], axis=0), 1, 0)[8:, :]
        dgx = dh * mult
        dla = dh * hm1 * a - dh * gi * xc * (e2 / mult)
        dpre_r = dla * (LRU_C * lsl) * r * (1.0 - r)
        dpre_i = dgx * xc * gi * (1.0 - gi)
        dlam_ref[...] += jnp.sum(dla * r, axis=0, keepdims=True) * (LRU_C * _sigmoid(-lam_ref[...]))
        dbr_ref[...] += jnp.sum(dpre_r, axis=0, keepdims=True)
        dbi_ref[...] += jnp.sum(dpre_i, axis=0, keepdims=True)
        drb, dib = dpre_r.astype(BF16), dpre_i.astype(BF16)
        for n in range(nh):
            cols = slice(n * HEAD, (n + 1) * HEAD)
            dwr_ref[n] += lax.dot_general(xcb[:, cols], drb[:, cols], TN, preferred_element_type=F32)
            dwi_ref[n] += lax.dot_general(xcb[:, cols], dib[:, cols], TN, preferred_element_type=F32)
        dxc = dgx * gi + _block_diag(drb, wr_ref, nh, NT) + _block_diag(dib, wi_ref, nh, NT)

        dcb_ref[...] += jnp.sum(dxc, axis=0, keepdims=True)
        for j in range(CONV_WIDTH):
            dcw_ref[pl.ds(j, 1), :] += jnp.sum(dxc * shifted[j], axis=0, keepdims=True)
        ext2 = jnp.concatenate([dxc, head[...]], axis=0)
        head[...] = dxc[:8, :]
        drx = dxc * cw_ref[pl.ds(CONV_WIDTH - 1, 1), :]
        for j in range(CONV_WIDTH - 1):
            up = CONV_WIDTH - 1 - j
            drx = drx + pltpu.roll(ext2, ts + 8 - up, 0)[:ts, :] * cw_ref[pl.ds(j, 1), :]
        drx_ref[...] = drx.astype(BF16)

    def blk(col):
        return lambda i: (nb - 1 - i, col)

    def prev8(col):
        return lambda i: (jnp.maximum((nb - 1 - i) * t8 - 1, 0), col)

    vec = pl.BlockSpec((1, bw), lambda i: (0, 0))
    conv = pl.BlockSpec((8, bw), lambda i: (0, 0))
    blocks = pl.BlockSpec((nh, HEAD, HEAD), lambda i: (0, 0, 0))
    return pl.pallas_call(
        body, name=name, grid=(nb,),
        in_specs=[pl.BlockSpec((ts, bw), blk(RX)), pl.BlockSpec((8, bw), prev8(RX)), pl.BlockSpec((ts, bw), blk(RY)),
                  pl.BlockSpec((ts, bw), blk(0)), pl.BlockSpec((8, bw), prev8(0)), pl.BlockSpec((ts, bw), blk(0))]
        + _lru_param_specs(bw, nh),
        out_specs=[pl.BlockSpec((ts, bw), blk(0)), pl.BlockSpec((ts, bw), blk(0)), conv, vec, blocks, vec, blocks, vec, vec],
        out_shape=[jax.ShapeDtypeStruct((s, bw), BF16)] * 2
        + [jax.ShapeDtypeStruct((8, bw), F32), jax.ShapeDtypeStruct((1, bw), F32),
           jax.ShapeDtypeStruct((nh, HEAD, HEAD), F32), jax.ShapeDtypeStruct((1, bw), F32),
           jax.ShapeDtypeStruct((nh, HEAD, HEAD), F32), jax.ShapeDtypeStruct((1, bw), F32),
           jax.ShapeDtypeStruct((1, bw), F32)],
        scratch_shapes=[pltpu.VMEM((1, bw), F32), pltpu.VMEM((8, bw), F32)],
        compiler_params=_cp(("arbitrary",)))(u, u, u, h, h, do, *params)


def _gate_merge(name, xb, w_gate, b_gate, o_all, w_branch, l, side=None):
    s, d = xb.shape
    bw = o_all.shape[2]
    tm, tn = min(512, s), min(256, d)

    def body(x_ref, wg_ref, bg_ref, o_ref, wb_ref, m_ref, g_ref, p_ref):
        x = x_ref[...]
        acc = jnp.zeros((tm, tn), F32)
        for g in range(4):
            gate = _sigmoid(jnp.dot(x, wg_ref[g], preferred_element_type=F32) + bg_ref[g])
            proj = jnp.dot(o_ref[g], wb_ref[g], preferred_element_type=F32)
            g_ref[g] = gate
            p_ref[g] = proj.astype(BF16)
            acc = acc + gate * proj
        m_ref[...] = acc.astype(BF16)

    quad = pl.BlockSpec((4, tm, tn), lambda n, m: (0, m, n))
    return _pcall(
        body, (xb, w_gate, b_gate, o_all, w_branch), name=name, grid=(d // tn, s // tm),
        in_specs=[pl.BlockSpec((tm, d), lambda n, m: (m, 0)),
                  pl.BlockSpec((None, 4, d, tn), lambda n, m: (0, 0, 0, n)),
                  pl.BlockSpec((None, 4, 1, tn), lambda n, m: (l, 0, 0, n)),
                  pl.BlockSpec((4, tm, bw), lambda n, m: (0, m, 0)),
                  pl.BlockSpec((None, 4, bw, tn), lambda n, m: (0, 0, 0, n))],
        out_specs=[pl.BlockSpec((tm, tn), lambda n, m: (m, n)), quad, quad],
        out_shape=[jax.ShapeDtypeStruct((s, d), BF16), jax.ShapeDtypeStruct((4, s, d), F32),
                   jax.ShapeDtypeStruct((4, s, d), BF16)],
        sem=("parallel", "parallel"), side=side)


def _adamw(name, w, m, v, parts, layer=0, layers=1, earlier=None):
    cols = w.shape[1]
    p, rows = parts.shape[0], parts.shape[1]
    tr = _pow2_rows(rows, cols * max(1, p // 2), 131072)
    nb = rows // tr
    c1 = 1.0 - ADAM_B1 ** ADAM_STEP
    c2 = 1.0 - ADAM_B2 ** ADAM_STEP

    def body(w_ref, m_ref, v_ref, g_ref, *rest):
        go_ref, do_ref, mo_ref, vo_ref = rest[-4:]
        g = g_ref[0].astype(F32)
        for k in range(1, p):
            g = g + g_ref[k].astype(F32)
        m2 = ADAM_B1 * m_ref[...] + (1.0 - ADAM_B1) * g
        v2 = ADAM_B2 * v_ref[...] + (1.0 - ADAM_B2) * (g * g)
        go_ref[...] = g
        do_ref[...] = -ADAM_LR * ((m2 / c1) / (jnp.sqrt(v2 / c2) + ADAM_EPS) + ADAM_WD * w_ref[...])
        mo_ref[...] = m2
        vo_ref[...] = v2

    row = pl.BlockSpec((tr, cols), lambda i: (layer * nb + i, 0))
    held = list(earlier) if earlier is not None else []
    return pl.pallas_call(
        body, name=name, grid=(nb,),
        in_specs=[row, row, row, pl.BlockSpec((p, tr, cols), lambda i: (0, i, 0))] + [ANY] * len(held),
        out_specs=[row] * 4, out_shape=[jax.ShapeDtypeStruct((layers * rows, cols), F32)] * 4,
        input_output_aliases={4 + k: k for k in range(len(held))},
        compiler_params=_cp(("parallel",)))(w, m, v, parts, *held)


def _pack(arrays):
    flat = jnp.concatenate([a.astype(F32).reshape(-1) for a in arrays])
    pad = (-flat.shape[0]) % (8 * LANE)
    return jnp.pad(flat, (0, pad)).reshape(-1, LANE)


def _unpack(packed, shapes):
    flat, out, off = packed.reshape(-1), [], 0
    for shp in shapes:
        n = math.prod(shp)
        out.append(flat[off:off + n].reshape(shp))
        off += n
    return out


def _unshard(gathered, axis):
    block = gathered.shape[2:]
    full = jnp.swapaxes(gathered, 0, 1).reshape((N_DEV,) + block)
    full = jnp.moveaxis(full, 0, axis)
    return full.reshape(block[:axis] + (N_DEV * block[axis],) + block[axis + 1:])


def kernel(x, ln_in_g, ln_in_b, w_in, b_forget, conv_w, conv_b, w_r, b_r, w_i, b_i, lru_lambda, rel_bias, w_branch, w_gate, b_gate, w_out, ln1_g, ln1_b, w_ff1, w_ff2, ln2_g, ln2_b, loss_target, m_ln_in_g, m_ln_in_b, m_w_in, m_b_forget, m_conv_w, m_conv_b, m_w_r, m_b_r, m_w_i, m_b_i, m_lru_lambda, m_rel_bias, m_w_branch, m_w_gate, m_b_gate, m_w_out, m_ln1_g, m_ln1_b, m_w_ff1, m_w_ff2, m_ln2_g, m_ln2_b, v_ln_in_g, v_ln_in_b, v_w_in, v_b_forget, v_conv_w, v_conv_b, v_w_r, v_b_r, v_w_i, v_b_i, v_lru_lambda, v_rel_bias, v_w_branch, v_w_gate, v_b_gate, v_w_out, v_ln1_g, v_ln1_b, v_w_ff1, v_w_ff2, v_ln2_g, v_ln2_b):
    given = dict(zip(
        NAMES + ['loss_target'] + ['m_' + n for n in WEIGHTS] + ['v_' + n for n in WEIGHTS],
        (x, ln_in_g, ln_in_b, w_in, b_forget, conv_w, conv_b, w_r, b_r, w_i, b_i, lru_lambda, rel_bias, w_branch, w_gate, b_gate, w_out, ln1_g, ln1_b, w_ff1, w_ff2, ln2_g, ln2_b, loss_target, m_ln_in_g, m_ln_in_b, m_w_in, m_b_forget, m_conv_w, m_conv_b, m_w_r, m_b_r, m_w_i, m_b_i, m_lru_lambda, m_rel_bias, m_w_branch, m_w_gate, m_b_gate, m_w_out, m_ln1_g, m_ln1_b, m_w_ff1, m_w_ff2, m_ln2_g, m_ln2_b, v_ln_in_g, v_ln_in_b, v_w_in, v_b_forget, v_conv_w, v_conv_b, v_w_r, v_b_r, v_w_i, v_b_i, v_lru_lambda, v_rel_bias, v_w_branch, v_w_gate, v_b_gate, v_w_out, v_ln1_g, v_ln1_b, v_w_ff1, v_w_ff2, v_ln2_g, v_ln2_b)))

    s, d = x.shape[1], x.shape[2]
    nl = w_in.shape[0]
    bw = d // 4
    nh = bw // HEAD
    nu = 11 * bw
    rs = d // N_DEV
    dff = w_ff1.shape[2] * N_DEV
    fs = dff // N_DEV
    cs = d // N_DEV
    assert nl == DEPTH and nh * HEAD == bw and s % 256 == 0 and d % 1024 == 0

    xi, yi, ci = _position()
    dev = 4 * xi + 2 * yi + ci
    c_arr = jnp.reshape(ci, (1,)).astype(I32)

    w_main = jnp.concatenate([w_in[..., :3 * bw], w_in[..., 3 * bw + nh:]], axis=-1).astype(BF16)
    w_fcol = jnp.pad(w_in[..., 3 * bw:3 * bw + nh], ((0, 0), (0, 0), (0, LANE - nh))).astype(BF16)
    small_shapes = [conv_w.shape, rel_bias.shape, b_gate.shape]
    shard = {'main': w_main, 'f': w_fcol, 'branch': w_branch.astype(BF16), 'gate': w_gate.astype(BF16),
             'out': w_out.astype(BF16), 'ff1': w_ff1.astype(BF16), 'ff2': w_ff2.astype(BF16)}
    shard_axis = {'main': 1, 'f': 1, 'branch': 3, 'gate': 2, 'out': 1, 'ff1': 2, 'ff2': 1}
    W = [dict() for _ in range(nl)]

    def gather_chips(l, keys, extra=()):
        side = _chips_side([shard[k][l:l + 1] for k in keys] + list(extra), True)
        side.todo = (l, keys)
        return side

    def gather_cores(chips):
        side = _cores_side(chips.results, True)
        side.todo = chips.todo
        return side

    def arrived(cores):
        l, keys = cores.todo
        for k, res in zip(keys, cores.results):
            W[l][k] = _unshard(res, shard_axis[k])
        return cores.results[len(keys):]

    first = gather_chips(0, ['main', 'f'], [_pack([conv_w, rel_bias, b_gate])])
    _run_side("gather_chips_first", first)
    first = gather_cores(first)
    _run_side("gather_cores_first", first)
    small = arrived(first)[0]
    small = jnp.swapaxes(small, 0, 1).reshape((N_DEV,) + small.shape[2:])
    small = [_unpack(small[j], small_shapes) for j in range(N_DEV)]
    conv_w_full = jnp.concatenate([small[j][0] for j in range(N_DEV)], axis=-1)
    rel_bias_full = jnp.concatenate([small[j][1] for j in range(N_DEV)], axis=-1)
    b_gate_full = jnp.concatenate([small[j][2] for j in range(N_DEV)], axis=-1)
    b_gate4 = b_gate_full.reshape(nl, 4, 1, d)

    def lru_params(l):
        return (jnp.pad(conv_w_full[l], ((0, 8 - CONV_WIDTH), (0, 0))), conv_b[l].reshape(1, bw),
                w_r[l].astype(BF16), b_r[l].reshape(1, bw), w_i[l].astype(BF16), b_i[l].reshape(1, bw),
                lru_lambda[l].reshape(1, bw))

    def bias_rows(l):
        return jnp.pad(rel_bias_full[l], ((0, 16 - nh), (0, REL_PAD - REL_TABLE)))

    tm = min(1024, s)
    tkk = min(2048, d)

    xs = x[0]
    h0, h0b = _ln_fwd("ln_in", xs, ln_in_g, ln_in_b)
    saved = []
    cur, curb = h0, h0b
    chips = {}
    for l in range(nl):
        side = None
        if l == 0:
            side = chips['b0'] = gather_chips(0, ['gate', 'branch'])
        else:
            side = last_cores = gather_cores(chips.pop('d1'))
        u = _mm(f"w_in_{l}", curb, W[l]['main'], grid=(s // tm, nu // bw, d // tkk),
                a_spec=pl.BlockSpec((tm, tkk), lambda m, n, k: (m, k)),
                b_spec=pl.BlockSpec((None, tkk, bw), lambda m, n, k: (0, k, n)),
                dims=NN, acc_shape=(tm, bw), out_shape=[jax.ShapeDtypeStruct((s, nu), BF16)],
                out_specs=[pl.BlockSpec((tm, bw), lambda m, n, k: (m, n))], finish=_store(BF16), side=side)[0]
        if l == 1:
            arrived(last_cores)
        fl = _mm(f"w_forget_{l}", curb, W[l]['f'], grid=(s // tm, 1, d // tkk),
                 a_spec=pl.BlockSpec((tm, tkk), lambda m, n, k: (m, k)),
                 b_spec=pl.BlockSpec((None, tkk, LANE), lambda m, n, k: (0, k, 0)),
                 dims=NN, acc_shape=(tm, LANE), out_shape=[jax.ShapeDtypeStruct((s, LANE), F32)],
                 out_specs=[pl.BlockSpec((tm, LANE), lambda m, n, k: (m, 0))], finish=_store(F32))[0]
        bf_row = jnp.pad(b_forget[l], (0, LANE - nh)).reshape(1, LANE)
        cf = _cum_forget_fwd(f"cum_forget_{l}", fl, bf_row)
        tkf = _fox_tile(s)
        cf_heads = cf[:, :nh].T
        cf_col = cf_heads.reshape(nh, s, 1)
        cf_row = cf_heads.reshape(nh, s // tkf, 1, tkf)
        side = None
        if l == 0:
            b0 = gather_cores(chips.pop('b0'))
            chips['c0'] = gather_chips(0, ['out', 'ff1', 'ff2'])
            side = _merge_sides(b0, chips['c0'])
        o_fox, lse = _fox_fwd(f"fox_fwd_{l}", u, cf_col, cf_row, bw, side=side)
        if l == 0:
            arrived(b0)
        lp = lru_params(l)
        o_lru, hstate = _lru_fwd(f"lru_fwd_{l}", u, lp, bw)
        o_sb = _sb_fwd(f"sb_fwd_{l}", u, bw)
        bias = jnp.transpose(_bias_expand(f"bias_expand_{l}", bias_rows(l)), (1, 0, 2))[:nh]
        kpad = jnp.pad(u[:, CK * bw:(CK + 1) * bw], ((PADK, 0), (0, 0)))
        vpad = jnp.pad(u[:, CV * bw:(CV + 1) * bw], ((PADK, 0), (0, 0)))
        side = gather_cores(chips.pop('c0')) if l == 0 else None
        o_ch = _chunk_fwd(f"chunk_fwd_{l}", u, kpad, vpad, bias, bw, side=side)
        if l == 0:
            arrived(side)
        o_all = jnp.stack([o_fox, o_lru, o_sb, o_ch])
        side = None
        if l == 0:
            side = chips['a1'] = gather_chips(1, ['main', 'f', 'gate', 'branch'])
        merged, gates, projs = _gate_merge(f"gate_merge_{l}", curb, W[l]['gate'], b_gate4, o_all, W[l]['branch'], l,
                                           side=side)
        side = gather_cores(chips.pop('a1')) if l == 0 else None
        z1, x1, x1b = _mm_ln(f"w_out_ln1_{l}", merged, W[l]['out'], 0, cur, ln1_g[l], ln1_b[l], side=side)
        if l == 0:
            arrived(side)
        tn1 = min(1024, dff)

        def ff1_finish(acc, ex, outs, ids):
            outs[0][...] = acc.astype(BF16)
            r = jnp.maximum(acc, 0.0)
            outs[1][...] = (r * r).astype(BF16)

        side = None
        if l == 0:
            side = chips['c1'] = gather_chips(1, ['out', 'ff1'])
        hp, hid = _mm(f"w_ff1_{l}", x1b, W[l]['ff1'], grid=(s // tm, dff // tn1, d // tkk),
                      a_spec=pl.BlockSpec((tm, tkk), lambda m, n, k: (m, k)),
                      b_spec=pl.BlockSpec((None, tkk, tn1), lambda m, n, k: (0, k, n)),
                      dims=NN, acc_shape=(tm, tn1),
                      out_shape=[jax.ShapeDtypeStruct((s, dff), BF16)] * 2,
                      out_specs=[pl.BlockSpec((tm, tn1), lambda m, n, k: (m, n))] * 2, finish=ff1_finish, side=side)
        side = None
        if l == 0:
            c1 = gather_cores(chips.pop('c1'))
            chips['d1'] = gather_chips(1, ['ff2'])
            side = _merge_sides(c1, chips['d1'])
        z2, x2, x2b = _mm_ln(f"w_ff2_ln2_{l}", hid, W[l]['ff2'], 0, x1, ln2_g[l], ln2_b[l], side=side)
        if l == 0:
            arrived(c1)
        saved.append(dict(xin=cur, xinb=curb, u=u, fl=fl, bf_row=bf_row, cf_col=cf_col, cf_row=cf_row, o_fox=o_fox,
                          lse=lse, lp=lp, hstate=hstate, bias=bias, kpad=kpad, vpad=vpad, o_all=o_all, merged=merged,
                          gates=gates, projs=projs, z1=z1, x1=x1, x1b=x1b, hp=hp, hid=hid, z2=z2))
        cur, curb = x2, x2b

    loss_tile, dcur = _loss_head("loss_head", cur, loss_target[0])
    loss = lax.psum(loss_tile[0, 0], ("x", "y", "c"))

    big = [dict() for _ in range(nl)]
    reduced = [dict() for _ in range(nl)]
    sm = {n: [None] * nl for n in ['b_forget', 'conv_w', 'conv_b', 'w_r', 'b_r', 'w_i', 'b_i', 'lru_lambda', 'rel_bias',
                                   'b_gate', 'ln1_g', 'ln1_b', 'ln2_g', 'ln2_b']}

    def split_columns(acc, ex, outs, ids):
        for j in range(N_DEV):
            outs[0][j] = acc[:, j * cs:(j + 1) * cs]

    def grad_mm(key, name, a, b, *, shape, grid, a_spec, b_spec, out_spec, acc_shape, finish=_store(F32)):
        l = int(name[-1])
        big[l][key] = _mm(name, a, b, grid=grid, a_spec=a_spec, b_spec=b_spec, dims=TN, acc_shape=acc_shape,
                          out_shape=[jax.ShapeDtypeStruct(shape, F32)], out_specs=[out_spec], finish=finish)[0]

    def reduce_cores(l, keys):
        side = _cores_side([big[l][k].reshape((4, 2) + big[l][k].shape[1:]) for k in keys], False)
        side.todo = (l, keys)
        return side

    def reduce_chips(cores):
        l, keys = cores.todo
        partial = []
        for k, mine, other in zip(keys, cores.operands, cores.results):
            cols = mine.shape[-1]
            rows = math.prod(mine.shape[2:]) // cols
            partial.append(_add_core_halves(f"add_cores_{k}_{l}", mine.reshape(4, 2, rows, cols),
                                            other.reshape(4, rows, cols), c_arr, BF16))
        side = _chips_side(partial, False)
        side.todo = (l, keys)
        return side

    def reduction_done(chips_side):
        l, keys = chips_side.todo
        reduced[l].update(zip(keys, chips_side.results))

    GROUP1, GROUP2 = ['w_ff2', 'w_ff1', 'w_out', 'w_branch', 'w_gate'], ['w_main', 'w_f']
    pending = None

    tks = min(1024, s)
    tmr = min(1024, d)
    nsh = tmr // rs

    for l in reversed(range(nl)):
        sv = saved[l]
        Wl = W[l]
        dz2, dz2b, dg, db = _ln_bwd(f"ln2_bwd_{l}", dcur, sv['z2'], ln2_g[l])
        sm['ln2_g'][l], sm['ln2_b'][l] = dg[0], db[0]
        tn1 = min(1024, dff)

        def dhp_finish(acc, ex, outs, ids):
            outs[0][...] = (acc * (2.0 * jnp.maximum(ex[0][...].astype(F32), 0.0))).astype(BF16)

        dhp = _mm(f"d_hidden_{l}", dz2b, Wl['ff2'], grid=(s // tm, dff // tn1, d // tkk),
                  a_spec=pl.BlockSpec((tm, tkk), lambda m, n, k: (m, k)),
                  b_spec=pl.BlockSpec((None, tn1, tkk), lambda m, n, k: (0, n, k)),
                  dims=NT, acc_shape=(tm, tn1), out_shape=[jax.ShapeDtypeStruct((s, dff), BF16)],
                  out_specs=[pl.BlockSpec((tm, tn1), lambda m, n, k: (m, n))], finish=dhp_finish,
                  extras=(sv['hp'],), extra_specs=(pl.BlockSpec((tm, tn1), lambda m, n, k: (m, n)),),
                  side=pending)[0]
        if pending is not None:
            reduction_done(pending)
            pending = None
        grad_mm('w_ff2', f"g_w_ff2_{l}", sv['hid'], dz2b, shape=(N_DEV, 1, fs, d), grid=(N_DEV, 1, s // tks),
                a_spec=pl.BlockSpec((tks, fs), lambda m, n, k: (k, m)),
                b_spec=pl.BlockSpec((tks, d), lambda m, n, k: (k, 0)),
                out_spec=pl.BlockSpec((None, None, fs, d), lambda m, n, k: (m, 0, 0, 0)), acc_shape=(fs, d))
        grad_mm('w_ff1', f"g_w_ff1_{l}", sv['x1b'], dhp, shape=(N_DEV, 1, d, fs), grid=(d // tmr, N_DEV, s // tks),
                a_spec=pl.BlockSpec((tks, tmr), lambda m, n, k: (k, m)),
                b_spec=pl.BlockSpec((tks, fs), lambda m, n, k: (k, n)),
                out_spec=pl.BlockSpec((None, None, tmr, fs), lambda m, n, k: (n, 0, m, 0)), acc_shape=(tmr, fs))
        tnd = min(1024, d)

        def resid_finish(scale):
            def finish(acc, ex, outs, ids):
                outs[0][...] = acc + scale * ex[0][...]
            return finish

        tile_md = pl.BlockSpec((tm, tnd), lambda m, n, k: (m, n))
        dx1 = _mm(f"d_x1_{l}", dhp, Wl['ff1'], grid=(s // tm, d // tnd, dff // tkk),
                  a_spec=pl.BlockSpec((tm, tkk), lambda m, n, k: (m, k)),
                  b_spec=pl.BlockSpec((None, tnd, tkk), lambda m, n, k: (0, n, k)),
                  dims=NT, acc_shape=(tm, tnd), out_shape=[jax.ShapeDtypeStruct((s, d), F32)],
                  out_specs=[tile_md], finish=resid_finish(ALPHA), extras=(dz2,), extra_specs=(tile_md,))[0]

        dz1, dz1b, dg, db = _ln_bwd(f"ln1_bwd_{l}", dx1, sv['z1'], ln1_g[l])
        sm['ln1_g'][l], sm['ln1_b'][l] = dg[0], db[0]
        tmg, tng = min(512, s), min(512, d)

        def gate_finish(acc, ex, outs, ids):
            @pl.when(ids[1] == 0)
            def _():
                outs[2][...] = jnp.zeros(outs[2].shape, F32)

            for g in range(4):
                gate = ex[0][g]
                dproj = acc * gate
                dpre = acc * ex[1][g].astype(F32) * gate * (1.0 - gate)
                outs[0][g] = dproj.astype(BF16)
                outs[1][g] = dpre.astype(BF16)
                outs[2][g] += jnp.sum(dpre, axis=0, keepdims=True)

        quad = pl.BlockSpec((4, tmg, tng), lambda n, m, k: (0, m, n))
        dproj, dpre, dbg = _mm(
            f"d_merged_{l}", dz1b, Wl['out'], grid=(d // tng, s // tmg, d // tkk),
            a_spec=pl.BlockSpec((tmg, tkk), lambda n, m, k: (m, k)),
            b_spec=pl.BlockSpec((None, tng, tkk), lambda n, m, k: (0, n, k)),
            dims=NT, acc_shape=(tmg, tng),
            out_shape=[jax.ShapeDtypeStruct((4, s, d), BF16), jax.ShapeDtypeStruct((4, s, d), BF16),
                       jax.ShapeDtypeStruct((4, 1, d), F32)],
            out_specs=[quad, quad, pl.BlockSpec((4, 1, tng), lambda n, m, k: (0, 0, n))], finish=gate_finish,
            extras=(sv['gates'], sv['projs']), extra_specs=(quad, quad), sem=("arbitrary", "arbitrary", "arbitrary"))
        sm['b_gate'][l] = dbg.reshape(4, d)
        grad_mm('w_out', f"g_w_out_{l}", sv['merged'], dz1b, shape=(N_DEV, 1, rs, d), grid=(d // tmr, 1, s // tks),
                a_spec=pl.BlockSpec((tks, tmr), lambda m, n, k: (k, m)),
                b_spec=pl.BlockSpec((tks, d), lambda m, n, k: (k, 0)),
                out_spec=pl.BlockSpec((nsh, None, rs, d), lambda m, n, k: (m, 0, 0, 0)), acc_shape=(tmr, d))

        nm = s // tm
        do_all = _mm(f"d_branch_{l}", dproj, Wl['branch'], grid=(4 * nm, 1, d // tkk),
                     a_spec=pl.BlockSpec((None, tm, tkk), lambda m, n, k: (m // nm, m % nm, k)),
                     b_spec=pl.BlockSpec((None, None, bw, tkk), lambda m, n, k: (0, m // nm, 0, k)),
                     dims=NT, acc_shape=(tm, bw), out_shape=[jax.ShapeDtypeStruct((4, s, bw), BF16)],
                     out_specs=[pl.BlockSpec((None, tm, bw), lambda m, n, k: (m // nm, m % nm, 0))],
                     finish=_store(BF16))[0]
        grad_mm('w_branch', f"g_w_branch_{l}", sv['o_all'], dproj, shape=(N_DEV, 1, 4, bw, cs),
                grid=(4, 1, s // tks), finish=split_columns,
                a_spec=pl.BlockSpec((None, tks, bw), lambda m, n, k: (m, k, 0)),
                b_spec=pl.BlockSpec((None, tks, d), lambda m, n, k: (m, k, 0)),
                out_spec=pl.BlockSpec((N_DEV, None, None, bw, cs), lambda m, n, k: (0, 0, m, 0, 0)),
                acc_shape=(bw, d))
        grad_mm('w_gate', f"g_w_gate_{l}", sv['xinb'], dpre, shape=(N_DEV, 1, 4, rs, d), grid=(d // tmr, 4, s // tks),
                a_spec=pl.BlockSpec((tks, tmr), lambda m, n, k: (k, m)),
                b_spec=pl.BlockSpec((None, tks, d), lambda m, n, k: (n, k, 0)),
                out_spec=pl.BlockSpec((nsh, None, None, rs, d), lambda m, n, k: (m, 0, n, 0, 0)),
                acc_shape=(tmr, d))
        nkg = d // tkk
        cores1 = reduce_cores(l, GROUP1)
        dx_gate = _mm(f"d_x_gates_{l}", dpre, Wl['gate'], grid=(s // tm, d // tnd, 4 * nkg),
                      a_spec=pl.BlockSpec((None, tm, tkk), lambda m, n, k: (k // nkg, m, k % nkg)),
                      b_spec=pl.BlockSpec((None, None, tnd, tkk), lambda m, n, k: (0, k // nkg, n, k % nkg)),
                      dims=NT, acc_shape=(tm, tnd), out_shape=[jax.ShapeDtypeStruct((s, d), F32)],
                      out_specs=[tile_md], finish=resid_finish(ALPHA), extras=(dz1,), extra_specs=(tile_md,),
                      side=cores1)[0]

        u = sv['u']
        chips1 = reduce_chips(cores1)
        dfq, dfk, dfv, dcc, dcr = _fox_bwd(f"fox_bwd_{l}", u, sv['cf_col'], sv['cf_row'], sv['o_fox'], do_all[0],
                                          sv['lse'], bw, side=chips1)
        reduction_done(chips1)
        dcf = (dcc.reshape(nh, s) + dcr.reshape(nh, s)).T
        dflb, dbf = _cum_forget_bwd(f"cum_forget_bwd_{l}", jnp.pad(dcf, ((0, 0), (0, LANE - nh))), sv['fl'],
                                    sv['bf_row'])
        sm['b_forget'][l] = dbf[0, :nh]
        drx, dry, dcw, dcb, dwr, dbr, dwi, dbi, dlam = _lru_bwd(f"lru_bwd_{l}", u, sv['hstate'], do_all[1], sv['lp'], bw)
        sm['conv_w'][l], sm['conv_b'][l], sm['w_r'][l], sm['b_r'][l] = dcw[:CONV_WIDTH], dcb[0], dwr, dbr[0]
        sm['w_i'][l], sm['b_i'][l], sm['lru_lambda'][l] = dwi, dbi[0], dlam[0]
        dsq, dsk, dsv = _sb_bwd(f"sb_bwd_{l}", u, do_all[2], bw)
        dcq, dck, dcv, dss = _chunk_bwd(f"chunk_bwd_{l}", u, sv['kpad'], sv['vpad'], sv['bias'], do_all[3], bw)
        dss_rows = jnp.pad(jnp.transpose(dss, (1, 0, 2)), ((0, 0), (0, 16 - nh), (0, 0)))
        sm['rel_bias'][l] = _bias_reduce(f"bias_reduce_{l}", dss_rows)[:nh, :REL_TABLE]
        du = jnp.concatenate([dfq, dfk, dfv, drx, dry, dsq, dsk, dsv, dcq, dck, dcv], axis=1)

        tnu = 11 * LANE
        grad_mm('w_main', f"g_w_in_{l}", sv['xinb'], du, shape=(N_DEV, 1, rs, nu), grid=(d // tmr, nu // tnu, s // tks),
                a_spec=pl.BlockSpec((tks, tmr), lambda m, n, k: (k, m)),
                b_spec=pl.BlockSpec((tks, tnu), lambda m, n, k: (k, n)),
                out_spec=pl.BlockSpec((nsh, None, rs, tnu), lambda m, n, k: (m, 0, 0, n)), acc_shape=(tmr, tnu))
        grad_mm('w_f', f"g_w_forget_{l}", sv['xinb'], dflb, shape=(N_DEV, 1, rs, LANE), grid=(d // tmr, 1, s // tks),
                a_spec=pl.BlockSpec((tks, tmr), lambda m, n, k: (k, m)),
                b_spec=pl.BlockSpec((tks, LANE), lambda m, n, k: (k, 0)),
                out_spec=pl.BlockSpec((nsh, None, rs, LANE), lambda m, n, k: (m, 0, 0, 0)), acc_shape=(tmr, LANE))
        tku = 11 * LANE
        cores2 = reduce_cores(l, GROUP2)
        dxa = _mm(f"d_x_in_{l}", du, Wl['main'], grid=(s // tm, d // tnd, nu // tku),
                  a_spec=pl.BlockSpec((tm, tku), lambda m, n, k: (m, k)),
                  b_spec=pl.BlockSpec((None, tnd, tku), lambda m, n, k: (0, n, k)),
                  dims=NT, acc_shape=(tm, tnd), out_shape=[jax.ShapeDtypeStruct((s, d), F32)],
                  out_specs=[tile_md], finish=resid_finish(1.0), extras=(dx_gate,), extra_specs=(tile_md,),
                  side=cores2)[0]
        dcur = _mm(f"d_x_forget_{l}", dflb, Wl['f'], grid=(s // tm, d // tnd, 1),
                   a_spec=pl.BlockSpec((tm, LANE), lambda m, n, k: (m, 0)),
                   b_spec=pl.BlockSpec((None, tnd, LANE), lambda m, n, k: (0, n, 0)),
                   dims=NT, acc_shape=(tm, tnd), out_shape=[jax.ShapeDtypeStruct((s, d), F32)],
                   out_specs=[tile_md], finish=resid_finish(1.0), extras=(dxa,), extra_specs=(tile_md,))[0]
        pending = reduce_chips(cores2)

    grad_x, _, dg_in, db_in = _ln_bwd("ln_in_bwd", dcur, xs, ln_in_g)
    _run_side("reduce_chips_last", pending)
    reduction_done(pending)

    small_names = ['ln_in_g', 'ln_in_b', 'b_forget', 'conv_w', 'conv_b', 'w_r', 'b_r', 'w_i', 'b_i', 'lru_lambda',
                   'rel_bias', 'b_gate', 'ln1_g', 'ln1_b', 'ln2_g', 'ln2_b']
    local = {'ln_in_g': dg_in[0], 'ln_in_b': db_in[0]}
    for n in small_names[2:]:
        local[n] = jnp.stack(sm[n])
    full_shapes = [local[n].shape for n in small_names]
    packed = _pack([local[n] for n in small_names])
    every = _run_side("gather_small_cores",
                      _cores_side(_run_side("gather_small_chips", _chips_side([packed], True)), True))[0]
    every = jnp.swapaxes(every, 0, 1).reshape((N_DEV,) + packed.shape)
    total = dict(zip(small_names, _unpack(_sum_parts("sum_small", every), full_shapes)))
    for n, width in (('conv_w', bw // N_DEV), ('rel_bias', REL_TABLE // N_DEV), ('b_gate', cs)):
        total[n] = lax.dynamic_slice_in_dim(total[n], dev * width, width, axis=2)

    out = {}

    def update(n, parts):
        cols = parts[0].shape[2]
        w2, m2, v2 = (given[p + n].reshape(-1, cols) for p in ('', 'm_', 'v_'))
        res = None
        for l in range(nl):
            res = _adamw(f"adamw_{n}_{l}", w2, m2, v2, parts[l], layer=l, layers=nl, earlier=res)
        out[n] = [r.reshape(given[n].shape) for r in res]

    def parts_w_in(l):
        pm, pf = reduced[l]['w_main'], reduced[l]['w_f']
        return jnp.concatenate([pm[..., :3 * bw], pf[..., :nh], pm[..., 3 * bw:]], axis=-1)

    update('w_in', [parts_w_in(l) for l in range(nl)])
    for n in ('w_branch', 'w_gate', 'w_out', 'w_ff1', 'w_ff2'):
        update(n, [reduced[l][n] for l in range(nl)])

    small_shapes2 = [given[n].shape for n in small_names]
    res = _adamw("adamw_small", _pack([given[n] for n in small_names]), _pack([given['m_' + n] for n in small_names]),
                 _pack([given['v_' + n] for n in small_names]), _pack([total[n] for n in small_names])[None])
    res = [_unpack(r, small_shapes2) for r in res]
    for j, n in enumerate(small_names):
        out[n] = [res[k][j] for k in range(4)]

    return (loss, grad_x[None], *[out[n][0] for n in WEIGHTS], *[out[n][1] for n in WEIGHTS],
            *[out[n][2] for n in WEIGHTS], *[out[n][3] for n in WEIGHTS])
```

```python
import functools
import math

import jax
import jax.numpy as jnp
from jax import lax
from jax.experimental import pallas as pl
from jax.experimental.pallas import tpu as pltpu

F32, BF16, I32 = jnp.float32, jnp.bfloat16, jnp.int32
MESH = pl.DeviceIdType.MESH
ANY = pl.BlockSpec(memory_space=pl.ANY)

LANE = 128
VMEM_LIMIT = 56 * 1024 * 1024
N_DEV = 8

HEAD = 128
CHUNK = 64
LOOKBACK = 8
BAND = (LOOKBACK + 1) * CHUNK
QBLK = 2 * CHUNK
WIN = BAND + CHUNK
PADK = LOOKBACK * CHUNK
REL_CLIP = 256
REL_TABLE = REL_CLIP + CHUNK
REL_PAD = 384
CONV_WIDTH = 4
LRU_C = 8.0
LN_EPS = 1e-5
DEPTH = 2
ALPHA = (2.0 * DEPTH) ** 0.25
NEG = -1e30
SB_DEAD = -104.0
GELU_K = math.sqrt(2.0 / math.pi)
GELU_C = 0.044715

ADAM_LR, ADAM_B1, ADAM_B2, ADAM_EPS, ADAM_WD, ADAM_STEP = 0.001, 0.9, 0.999, 1e-08, 0.01, 10

NN = (((1,), (0,)), ((), ()))
NT = (((1,), (1,)), ((), ()))
TN = (((0,), (0,)), ((), ()))

FQ, FK, FV, RX, RY, SQ, SK, SV, CQ, CK, CV = range(11)

NAMES = ['x', 'ln_in_g', 'ln_in_b', 'w_in', 'b_forget', 'conv_w', 'conv_b', 'w_r', 'b_r', 'w_i', 'b_i', 'lru_lambda',
         'rel_bias', 'w_branch', 'w_gate', 'b_gate', 'w_out', 'ln1_g', 'ln1_b', 'w_ff1', 'w_ff2', 'ln2_g', 'ln2_b']
WEIGHTS = NAMES[1:]


def _cp(sem=None):
    return pltpu.CompilerParams(dimension_semantics=sem, vmem_limit_bytes=VMEM_LIMIT)


def _iota(shape, dim):
    return lax.broadcasted_iota(I32, shape, dim)


def _sigmoid(x):
    return 1.0 / (1.0 + jnp.exp(-x))


def _log_sigmoid(x):
    return jnp.minimum(x, 0.0) - jnp.log(1.0 + jnp.exp(-jnp.abs(x)))


def _pow2_rows(rows, cols, elems=262144):
    t = 8
    while t * 2 <= rows and t * 2 * cols <= elems and rows % (t * 2) == 0:
        t *= 2
    return t


def _position():
    return lax.axis_index("x"), lax.axis_index("y"), lax.axis_index("c")


class _Side:
    def __init__(self, operands, out_shape, sems, start, finish, aliases=(), parts=()):
        self.operands, self.out_shape, self.sems = list(operands), list(out_shape), list(sems)
        self.start, self.finish, self.aliases, self.parts = start, finish, list(aliases), parts
        self.results = None

    def set_results(self, res):
        self.results = list(res)
        off = 0
        for part in self.parts:
            part.set_results(res[off:off + len(part.out_shape)])
            off += len(part.out_shape)


def _merge_sides(a, b):
    ai, ao, asm = len(a.operands), len(a.out_shape), len(a.sems)

    def start(ins, outs, sems):
        a.start(ins[:ai], outs[:ao], sems[:asm])
        b.start(ins[ai:], outs[ao:], sems[asm:])

    def finish(ins, outs, sems):
        a.finish(ins[:ai], outs[:ao], sems[:asm])
        b.finish(ins[ai:], outs[ao:], sems[asm:])

    return _Side(a.operands + b.operands, a.out_shape + b.out_shape, a.sems + b.sems, start, finish,
                 a.aliases + [(i + ai, o + ao) for i, o in b.aliases], parts=(a, b))


def _chips_side(xs, gather):
    n = len(xs)

    def copies(ins, outs, sems, arrivals):
        send_sems, recv_sems, local_sems = sems
        x, y, c = _position()
        q = 2 * x + y
        chips = [(1 - x, y), (x, 1 - y), (1 - x, 1 - y)]

        def src(t, slot):
            return ins[t] if gather else ins[t].at[slot]

        def dst(t, slot):
            return outs[t].at[c, slot] if gather else outs[t].at[slot]

        def remote(t, j, landing):
            px, py = chips[j]
            return pltpu.make_async_remote_copy(
                src_ref=src(t, 2 * px + py), dst_ref=dst(t, landing), send_sem=send_sems.at[t, j],
                recv_sem=recv_sems.at[t, j], device_id=(px, py, c), device_id_type=MESH)

        local = [pltpu.make_async_copy(src(t, q), dst(t, q), local_sems.at[t]) for t in range(n)]
        sends = [remote(t, j, q) for t in range(n) for j in range(3)]
        if not arrivals:
            return local, sends, []
        return local, sends, [remote(t, j, 2 * px + py) for t in range(n) for j, (px, py) in enumerate(chips)]

    def start(ins, outs, sems):
        local, sends, _ = copies(ins, outs, sems, False)
        for cp in local + sends:
            cp.start()

    def finish(ins, outs, sems):
        local, sends, recvs = copies(ins, outs, sems, True)
        for cp in recvs:
            cp.wait_recv()
        for cp in sends:
            cp.wait_send()
        for cp in local:
            cp.wait()

    out_shape = [jax.ShapeDtypeStruct((2, 4) + a.shape if gather else a.shape, a.dtype) for a in xs]
    sems = [pltpu.SemaphoreType.DMA((n, 3)), pltpu.SemaphoreType.DMA((n, 3)), pltpu.SemaphoreType.DMA((n,))]
    return _Side(xs, out_shape, sems, start, finish)


def _cores_side(xs, gather):
    n = len(xs)
    m = 1 if gather else 4

    def copies(ins, outs, sems, arrivals):
        send_sems, recv_sems = sems
        x, y, c = _position()

        def remote(t, j, landing):
            s = outs[t].at[c] if gather else ins[t].at[j, 1 - c]
            d = outs[t].at[landing] if gather else outs[t].at[j]
            return pltpu.make_async_remote_copy(
                src_ref=s, dst_ref=d, send_sem=send_sems.at[t, j], recv_sem=recv_sems.at[t, j],
                device_id=(x, y, 1 - c), device_id_type=MESH)

        sends = [remote(t, j, c) for t in range(n) for j in range(m)]
        return sends, [remote(t, j, 1 - c) for t in range(n) for j in range(m)] if arrivals else []

    def start(ins, outs, sems):
        for cp in copies(ins, outs, sems, False)[0]:
            cp.start()

    def finish(ins, outs, sems):
        sends, recvs = copies(ins, outs, sems, True)
        for cp in recvs:
            cp.wait_recv()
        for cp in sends:
            cp.wait_send()

    if gather:
        out_shape = [jax.ShapeDtypeStruct(a.shape, a.dtype) for a in xs]
    else:
        out_shape = [jax.ShapeDtypeStruct((4,) + a.shape[2:], a.dtype) for a in xs]
    sems = [pltpu.SemaphoreType.DMA((n, m)), pltpu.SemaphoreType.DMA((n, m))]
    return _Side(xs, out_shape, sems, start, finish, aliases=[(t, t) for t in range(n)] if gather else [])


def _run_side(name, side):
    ni, no = len(side.operands), len(side.out_shape)

    def body(*refs):
        ins, outs, sems = refs[:ni], refs[ni:ni + no], refs[ni + no:]
        side.start(ins, outs, sems)
        side.finish(ins, outs, sems)

    side.set_results(pl.pallas_call(
        body, name=name, out_shape=side.out_shape, in_specs=[ANY] * ni, out_specs=[ANY] * no,
        input_output_aliases=dict(side.aliases), scratch_shapes=side.sems)(*side.operands))
    return side.results


def _pcall(body, operands, *, name, grid, in_specs, out_specs, out_shape, scratch_shapes=(), sem=None, aliases=None,
           side=None):
    if side is None:
        return pl.pallas_call(body, name=name, grid=grid, in_specs=list(in_specs), out_specs=list(out_specs),
                              out_shape=list(out_shape), scratch_shapes=list(scratch_shapes),
                              input_output_aliases=aliases or {}, compiler_params=_cp(sem))(*operands)
    ni, no, ns = len(in_specs), len(out_shape), len(scratch_shapes)
    si, so = len(side.operands), len(side.out_shape)

    def carrying(*refs):
        ins, sins = refs[:ni], refs[ni:ni + si]
        outs, souts = refs[ni + si:ni + si + no], refs[ni + si + no:ni + si + no + so]
        scratch, ssems = refs[ni + si + no + so:ni + si + no + so + ns], refs[ni + si + no + so + ns:]
        ids = [pl.program_id(a) for a in range(len(grid))]
        first = functools.reduce(jnp.logical_and, [i == 0 for i in ids])
        last = functools.reduce(jnp.logical_and, [i == g - 1 for i, g in zip(ids, grid)])

        @pl.when(first)
        def _():
            side.start(sins, souts, ssems)

        body(*ins, *outs, *scratch)

        @pl.when(last)
        def _():
            side.finish(sins, souts, ssems)

    joined = dict(aliases or {})
    joined.update({ni + i: no + o for i, o in side.aliases})
    res = pl.pallas_call(
        carrying, name=name, grid=grid, in_specs=[*in_specs, *[ANY] * si], out_specs=[*out_specs, *[ANY] * so],
        out_shape=[*out_shape, *side.out_shape], scratch_shapes=[*scratch_shapes, *side.sems],
        input_output_aliases=joined, compiler_params=_cp(("arbitrary",) * len(grid)))(*operands, *side.operands)
    side.set_results(res[no:])
    return res[:no]


def _add_core_halves(name, mine, other, c, out_dtype):
    _, _, rows, cols = mine.shape
    tr = _pow2_rows(rows, cols)

    def body(c_ref, a_ref, b_ref, o_ref):
        o_ref[...] = (a_ref[...] + b_ref[...]).astype(out_dtype)

    grid_spec = pltpu.PrefetchScalarGridSpec(
        num_scalar_prefetch=1, grid=(4, rows // tr),
        in_specs=[pl.BlockSpec((None, None, tr, cols), lambda j, i, c_ref: (j, c_ref[0], i, 0)),
                  pl.BlockSpec((None, tr, cols), lambda j, i, c_ref: (j, i, 0))],
        out_specs=pl.BlockSpec((None, tr, cols), lambda j, i, c_ref: (j, i, 0)))
    return pl.pallas_call(body, name=name, grid_spec=grid_spec,
                          out_shape=jax.ShapeDtypeStruct((4, rows, cols), out_dtype),
                          compiler_params=_cp(("parallel", "parallel")))(c, mine, other)


def _sum_parts(name, parts):
    p, rows, cols = parts.shape
    tr = _pow2_rows(rows, cols * p)

    def body(a_ref, o_ref):
        acc = a_ref[0]
        for k in range(1, p):
            acc = acc + a_ref[k]
        o_ref[...] = acc

    return pl.pallas_call(body, name=name, grid=(rows // tr,),
                          in_specs=[pl.BlockSpec((p, tr, cols), lambda i: (0, i, 0))],
                          out_specs=pl.BlockSpec((tr, cols), lambda i: (i, 0)),
                          out_shape=jax.ShapeDtypeStruct((rows, cols), F32),
                          compiler_params=_cp(("parallel",)))(parts)


def _mm(name, a, b, *, grid, a_spec, b_spec, dims, acc_shape, out_shape, out_specs, finish,
        extras=(), extra_specs=(), aliases=None, sem=("parallel", "parallel", "arbitrary"), side=None):
    nk, ne, no = grid[2], len(extras), len(out_shape)

    def body(*refs):
        a_ref, b_ref = refs[0], refs[1]
        ex, outs = refs[2:2 + ne], refs[2 + ne:2 + ne + no]
        ids = (pl.program_id(0), pl.program_id(1))
        def prod():
            return lax.dot_general(a_ref[...], b_ref[...], dims, preferred_element_type=F32)

        if nk == 1:
            finish(prod(), ex, outs, ids)
            return
        acc = refs[2 + ne + no]
        k = pl.program_id(2)

        @pl.when(k == 0)
        def _():
            acc[...] = prod()

        @pl.when(jnp.logical_and(k > 0, k < nk - 1))
        def _():
            acc[...] += prod()

        @pl.when(k == nk - 1)
        def _():
            finish(acc[...] + prod(), ex, outs, ids)

    return _pcall(body, (a, b, *extras), name=name, grid=grid, in_specs=[a_spec, b_spec, *extra_specs],
                  out_specs=out_specs, out_shape=out_shape,
                  scratch_shapes=[pltpu.VMEM(acc_shape, F32)] if nk > 1 else [], sem=sem, aliases=aliases, side=side)


def _store(dtype):
    def finish(acc, ex, outs, ids):
        outs[0][...] = acc.reshape(outs[0].shape).astype(dtype)
    return finish


def _layer_norm_rows(z, g, b):
    mu = jnp.mean(z, axis=1, keepdims=True)
    zc = z - mu
    var = jnp.mean(zc * zc, axis=1, keepdims=True)
    return zc * lax.rsqrt(var + LN_EPS) * g + b


def _mm_ln(name, a, w, l, resid, g, b, side=None):
    s, kdim = a.shape
    d = w.shape[2]
    tm, tk = min(512, s), min(1024, kdim)

    def finish(acc, ex, outs, ids):
        z = acc + ALPHA * ex[0][...]
        y = _layer_norm_rows(z, ex[1][...], ex[2][...])
        outs[0][...] = z
        outs[1][...] = y
        outs[2][...] = y.astype(BF16)

    row = pl.BlockSpec((tm, d), lambda m, n, k: (m, 0))
    vec = pl.BlockSpec((1, d), lambda m, n, k: (0, 0))
    return _mm(name, a, w, grid=(s // tm, 1, kdim // tk),
               a_spec=pl.BlockSpec((tm, tk), lambda m, n, k: (m, k)),
               b_spec=pl.BlockSpec((None, tk, d), lambda m, n, k: (l, k, 0)),
               dims=NN, acc_shape=(tm, d),
               out_shape=[jax.ShapeDtypeStruct((s, d), F32), jax.ShapeDtypeStruct((s, d), F32),
                          jax.ShapeDtypeStruct((s, d), BF16)],
               out_specs=[row, row, row], finish=finish,
               extras=(resid, g.reshape(1, d), b.reshape(1, d)), extra_specs=(row, vec, vec), side=side)


def _ln_fwd(name, x, g, b):
    s, d = x.shape
    tr = min(256, s)

    def body(x_ref, g_ref, b_ref, y_ref, yb_ref):
        y = _layer_norm_rows(x_ref[...], g_ref[...], b_ref[...])
        y_ref[...] = y
        yb_ref[...] = y.astype(BF16)

    row = pl.BlockSpec((tr, d), lambda i: (i, 0))
    vec = pl.BlockSpec((1, d), lambda i: (0, 0))
    return pl.pallas_call(body, name=name, grid=(s // tr,), in_specs=[row, vec, vec], out_specs=[row, row],
                          out_shape=[jax.ShapeDtypeStruct((s, d), F32), jax.ShapeDtypeStruct((s, d), BF16)],
                          compiler_params=_cp(("parallel",)))(x, g.reshape(1, d), b.reshape(1, d))


def _ln_bwd(name, dy, z, g):
    s, d = z.shape
    tr = min(256, s)

    def body(dy_ref, z_ref, g_ref, dz_ref, dzb_ref, dg_ref, db_ref):
        @pl.when(pl.program_id(0) == 0)
        def _():
            dg_ref[...] = jnp.zeros(dg_ref.shape, F32)
            db_ref[...] = jnp.zeros(db_ref.shape, F32)

        zz, dyv = z_ref[...], dy_ref[...]
        mu = jnp.mean(zz, axis=1, keepdims=True)
        zc = zz - mu
        rstd = lax.rsqrt(jnp.mean(zc * zc, axis=1, keepdims=True) + LN_EPS)
        xhat = zc * rstd
        dg_ref[...] += jnp.sum(dyv * xhat, axis=0, keepdims=True)
        db_ref[...] += jnp.sum(dyv, axis=0, keepdims=True)
        dxh = dyv * g_ref[...]
        dz = rstd * (dxh - jnp.mean(dxh, axis=1, keepdims=True) - xhat * jnp.mean(dxh * xhat, axis=1, keepdims=True))
        dz_ref[...] = dz
        dzb_ref[...] = dz.astype(BF16)

    row = pl.BlockSpec((tr, d), lambda i: (i, 0))
    vec = pl.BlockSpec((1, d), lambda i: (0, 0))
    return pl.pallas_call(
        body, name=name, grid=(s // tr,), in_specs=[row, row, vec], out_specs=[row, row, vec, vec],
        out_shape=[jax.ShapeDtypeStruct((s, d), F32), jax.ShapeDtypeStruct((s, d), BF16),
                   jax.ShapeDtypeStruct((1, d), F32), jax.ShapeDtypeStruct((1, d), F32)],
        compiler_params=_cp(("arbitrary",)))(dy, z, g.reshape(1, d))


def _loss_head(name, y, target):
    s, d = y.shape
    tr = min(256, s)

    def body(y_ref, t_ref, loss_ref, dy_ref):
        @pl.when(pl.program_id(0) == 0)
        def _():
            loss_ref[...] = jnp.zeros(loss_ref.shape, F32)

        e = y_ref[...] - t_ref[...]
        dy_ref[...] = e * (1.0 / d)
        loss_ref[...] += jnp.sum(e * e) * (0.5 / d)

    row = pl.BlockSpec((tr, d), lambda i: (i, 0))
    return pl.pallas_call(
        body, name=name, grid=(s // tr,), in_specs=[row, row],
        out_specs=[pl.BlockSpec((8, LANE), lambda i: (0, 0)), row],
        out_shape=[jax.ShapeDtypeStruct((8, LANE), F32), jax.ShapeDtypeStruct((s, d), F32)],
        compiler_params=_cp(("arbitrary",)))(y, target)


def _scan_add(x, reverse):
    ts = x.shape[0]
    rows = _iota((ts, 1), 0)
    dist = 1
    while dist < ts:
        if reverse:
            x = x + jnp.where(rows < ts - dist, pltpu.roll(x, ts - dist, 0), 0.0)
        else:
            x = x + jnp.where(rows >= dist, pltpu.roll(x, dist, 0), 0.0)
        dist *= 2
    return x


def _scan_affine(a, b, reverse):
    ts = a.shape[0]
    rows = _iota((ts, 1), 0)
    dist = 1
    while dist < ts:
        shift = ts - dist if reverse else dist
        valid = rows < ts - dist if reverse else rows >= dist
        b = b + a * jnp.where(valid, pltpu.roll(b, shift, 0), 0.0)
        a = a * jnp.where(valid, pltpu.roll(a, shift, 0), 1.0)
        dist *= 2
    return a, b


def _cum_forget_fwd(name, fl, bias):
    s = fl.shape[0]
    ts = min(1024, s)

    def body(f_ref, b_ref, o_ref, carry):
        @pl.when(pl.program_id(0) == 0)
        def _():
            carry[...] = jnp.zeros(carry.shape, F32)

        o_ref[...] = _scan_add(_log_sigmoid(f_ref[...] + b_ref[...]), False) + carry[...]
        carry[...] = o_ref[pl.ds(ts - 1, 1), :]

    row = pl.BlockSpec((ts, LANE), lambda i: (i, 0))
    return pl.pallas_call(body, name=name, grid=(s // ts,),
                          in_specs=[row, pl.BlockSpec((1, LANE), lambda i: (0, 0))], out_specs=row,
                          out_shape=jax.ShapeDtypeStruct((s, LANE), F32),
                          scratch_shapes=[pltpu.VMEM((1, LANE), F32)],
                          compiler_params=_cp(("arbitrary",)))(fl, bias)


def _cum_forget_bwd(name, dcf, fl, bias):
    s = fl.shape[0]
    ts = min(1024, s)
    nb = s // ts

    def body(d_ref, f_ref, b_ref, o_ref, db_ref, carry):
        @pl.when(pl.program_id(0) == 0)
        def _():
            carry[...] = jnp.zeros(carry.shape, F32)
            db_ref[...] = jnp.zeros(db_ref.shape, F32)

        run = _scan_add(d_ref[...], True) + carry[...]
        carry[...] = jnp.sum(jnp.where(_iota((ts, 1), 0) == 0, run, 0.0), axis=0, keepdims=True)
        dfl = run * _sigmoid(-(f_ref[...] + b_ref[...]))
        o_ref[...] = dfl.astype(BF16)
        db_ref[...] += jnp.sum(dfl, axis=0, keepdims=True)

    row = pl.BlockSpec((ts, LANE), lambda i: (nb - 1 - i, 0))
    vec = pl.BlockSpec((1, LANE), lambda i: (0, 0))
    return pl.pallas_call(body, name=name, grid=(nb,), in_specs=[row, row, vec], out_specs=[row, vec],
                          out_shape=[jax.ShapeDtypeStruct((s, LANE), BF16), jax.ShapeDtypeStruct((1, LANE), F32)],
                          scratch_shapes=[pltpu.VMEM((1, LANE), F32)],
                          compiler_params=_cp(("arbitrary",)))(dcf, fl, bias)


def _fox_specs(s, nh, tq, tk):
    q = pl.BlockSpec((tq, HEAD), lambda h, i: (i, FQ * nh + h))
    k = pl.BlockSpec((s, HEAD), lambda h, i: (0, FK * nh + h))
    v = pl.BlockSpec((s, HEAD), lambda h, i: (0, FV * nh + h))
    col = pl.BlockSpec((None, tq, 1), lambda h, i: (h, i, 0))
    rowv = pl.BlockSpec((None, s // tk, 1, tk), lambda h, i: (h, 0, 0, 0))
    tile = pl.BlockSpec((tq, HEAD), lambda h, i: (i, h))
    full = pl.BlockSpec((s, HEAD), lambda h, i: (0, h))
    return q, k, v, col, rowv, tile, full


def _fox_tile(s):
    return min(512, s)


def _fox_scores(q, k_ref, cfq, cfr_ref, kb, tk, scale, diagonal):
    off = pl.multiple_of(kb * tk, tk)
    k = k_ref[pl.ds(off, tk), :]
    sc = lax.dot_general(q, k, NT, preferred_element_type=F32) * scale + cfq - cfr_ref[kb]
    mask = None
    if diagonal:
        mask = _iota((1, tk), 1) <= _iota((tk, 1), 0)
        sc = jnp.where(mask, sc, NEG)
    return sc, mask, k, off


def _fox_fwd(name, u, cf_col, cf_row, bw, side=None):
    s, nh = u.shape[0], bw // HEAD
    tq = tk = _fox_tile(s)
    scale = HEAD ** -0.5

    def body(q_ref, k_ref, v_ref, cfc_ref, cfr_ref, o_ref, lse_ref):
        i = pl.program_id(1)
        q, cfq = q_ref[...], cfc_ref[...]

        def step(kb, carry, diagonal=False):
            m, l, acc = carry
            sc, _, _, off = _fox_scores(q, k_ref, cfq, cfr_ref, kb, tk, scale, diagonal)
            m2 = jnp.maximum(m, jnp.max(sc, axis=1, keepdims=True))
            p = jnp.exp(sc - m2)
            al = jnp.exp(m - m2)
            return (m2, al * l + jnp.sum(p, axis=1, keepdims=True),
                    al * acc + jnp.dot(p.astype(BF16), v_ref[pl.ds(off, tk), :], preferred_element_type=F32))

        init = (jnp.full((tq, 1), NEG, F32), jnp.zeros((tq, 1), F32), jnp.zeros((tq, HEAD), F32))
        m, l, acc = step(i, lax.fori_loop(0, i, step, init), True)
        o_ref[...] = (acc / l).astype(BF16)
        lse_ref[...] = m + jnp.log(l)

    q, k, v, col, rowv, tile, _ = _fox_specs(s, nh, tq, tk)
    return _pcall(body, (u, u, u, cf_col, cf_row), name=name, grid=(nh, s // tq), in_specs=[q, k, v, col, rowv],
                  out_specs=[tile, col],
                  out_shape=[jax.ShapeDtypeStruct((s, bw), BF16), jax.ShapeDtypeStruct((nh, s, 1), F32)],
                  sem=("parallel", "parallel"), side=side)


def _fox_bwd(name, u, cf_col, cf_row, o, do, lse, bw, side=None):
    s, nh = u.shape[0], bw // HEAD
    tq = tk = _fox_tile(s)
    nq = s // tq
    scale = HEAD ** -0.5

    def body(q_ref, k_ref, v_ref, cfc_ref, cfr_ref, o_ref, do_ref, lse_ref,
             dq_ref, dk_ref, dv_ref, dcc_ref, dcr_ref, dk_s, dv_s):
        i = pl.program_id(1)

        @pl.when(i == 0)
        def _():
            dk_s[...] = jnp.zeros(dk_s.shape, F32)
            dv_s[...] = jnp.zeros(dv_s.shape, F32)
            dcr_ref[...] = jnp.zeros(dcr_ref.shape, F32)

        q, dov, cfq, lse_q = q_ref[...], do_ref[...], cfc_ref[...], lse_ref[...]
        delta = jnp.sum(dov.astype(F32) * o_ref[...].astype(F32), axis=1, keepdims=True)

        def step(kb, carry, diagonal=False):
            dq, dcq = carry
            sc, mask, k, off = _fox_scores(q, k_ref, cfq, cfr_ref, kb, tk, scale, diagonal)
            p = jnp.exp(sc - lse_q)
            if diagonal:
                p = jnp.where(mask, p, 0.0)
            dp = lax.dot_general(dov, v_ref[pl.ds(off, tk), :], NT, preferred_element_type=F32)
            ds = p * (dp - delta)
            dsb = ds.astype(BF16)
            dk_s[pl.ds(off, tk), :] += lax.dot_general(dsb, q, TN, preferred_element_type=F32)
            dv_s[pl.ds(off, tk), :] += lax.dot_general(p.astype(BF16), dov, TN, preferred_element_type=F32)
            dcr_ref[kb] += -jnp.sum(ds, axis=0, keepdims=True)
            return (dq + jnp.dot(dsb, k, preferred_element_type=F32), dcq + jnp.sum(ds, axis=1, keepdims=True))

        init = (jnp.zeros((tq, HEAD), F32), jnp.zeros((tq, 1), F32))
        dq, dcq = step(i, lax.fori_loop(0, i, step, init), True)
        dq_ref[...] = (dq * scale).astype(BF16)
        dcc_ref[...] = jnp.transpose(jnp.broadcast_to(dcq, (tq, LANE)))[:8, :]

        @pl.when(i == nq - 1)
        def _():
            dk_ref[...] = (dk_s[...] * scale).astype(BF16)
            dv_ref[...] = dv_s[...].astype(BF16)

    q, k, v, col, rowv, tile, full = _fox_specs(s, nh, tq, tk)
    by_query = pl.BlockSpec((None, None, 8, tq), lambda h, i: (h, i, 0, 0))
    return _pcall(
        body, (u, u, u, cf_col, cf_row, o, do, lse), name=name, grid=(nh, nq),
        in_specs=[q, k, v, col, rowv, tile, tile, col], out_specs=[tile, full, full, by_query, rowv],
        out_shape=[jax.ShapeDtypeStruct((s, bw), BF16)] * 3
        + [jax.ShapeDtypeStruct((nh, nq, 8, tq), F32), jax.ShapeDtypeStruct((nh, s // tk, 1, tk), F32)],
        scratch_shapes=[pltpu.VMEM((s, HEAD), F32), pltpu.VMEM((s, HEAD), F32)],
        sem=("arbitrary", "arbitrary"), side=side)


def _suffix_mm(x, ones_below):
    hi = x.astype(BF16)
    lo = (x - hi.astype(F32)).astype(BF16)
    return (jnp.dot(hi, ones_below, preferred_element_type=F32) + jnp.dot(lo, ones_below, preferred_element_type=F32))


def _sb_tile(q, k_ref, kb, tk, qpos, scale):
    off = pl.multiple_of(kb * tk, tk)
    k = k_ref[pl.ds(off, tk), :]
    z = lax.dot_general(q, k, NT, preferred_element_type=F32) * scale
    mask = kb * tk + _iota((1, tk), 1) < qpos
    lsn = -jnp.maximum(z, 0.0) - jnp.log(1.0 + jnp.exp(-jnp.abs(z)))
    return z, mask, lsn, jnp.where(mask, lsn, 0.0), k, off


def _sb_specs(s, nh, tq):
    q = pl.BlockSpec((tq, HEAD), lambda h, i: (i, SQ * nh + h))
    k = pl.BlockSpec((s, HEAD), lambda h, i: (0, SK * nh + h))
    v = pl.BlockSpec((s, HEAD), lambda h, i: (0, SV * nh + h))
    tile = pl.BlockSpec((tq, HEAD), lambda h, i: (i, h))
    full = pl.BlockSpec((s, HEAD), lambda h, i: (0, h))
    return q, k, v, tile, full


def _sb_fwd(name, u, bw, side=None):
    s, nh = u.shape[0], bw // HEAD
    tq = tk = 256
    scale = HEAD ** -0.5

    def body(q_ref, k_ref, v_ref, o_ref):
        i = pl.program_id(1)
        q = q_ref[...]
        qpos = i * tq + _iota((tq, 1), 0)
        later_keys = (_iota((tk, tk), 0) > _iota((tk, tk), 1)).astype(BF16)
        nk = (i * tq + tq + tk - 2) // tk

        def cond(st):
            return jnp.logical_and(st[0] < nk, st[3] > SB_DEAD)

        def step(st):
            j, c, acc, _ = st
            z, mask, lsn, lm, _, off = _sb_tile(q, k_ref, nk - 1 - j, tk, qpos, scale)
            a = jnp.where(mask, jnp.exp(lsn + z + c + _suffix_mm(lm, later_keys)), 0.0)
            acc = acc + jnp.dot(a.astype(BF16), v_ref[pl.ds(off, tk), :], preferred_element_type=F32)
            c = c + jnp.sum(lm, axis=1, keepdims=True)
            return j + 1, c, acc, jnp.max(c)

        init = (jnp.int32(0), jnp.zeros((tq, 1), F32), jnp.zeros((tq, HEAD), F32), jnp.float32(0.0))
        o_ref[...] = lax.while_loop(cond, step, init)[2].astype(BF16)

    q, k, v, tile, _ = _sb_specs(s, nh, tq)
    return _pcall(body, (u, u, u), name=name, grid=(nh, s // tq), in_specs=[q, k, v], out_specs=[tile],
                  out_shape=[jax.ShapeDtypeStruct((s, bw), BF16)], sem=("parallel", "parallel"), side=side)[0]


def _sb_bwd(name, u, do, bw):
    s, nh = u.shape[0], bw // HEAD
    tq = tk = 256
    nq = s // tq
    scale = HEAD ** -0.5

    def body(q_ref, k_ref, v_ref, do_ref, dq_ref, dk_ref, dv_ref, dk_s, dv_s):
        i = pl.program_id(1)

        @pl.when(i == 0)
        def _():
            dk_s[...] = jnp.zeros(dk_s.shape, F32)
            dv_s[...] = jnp.zeros(dv_s.shape, F32)

        q, dov = q_ref[...], do_ref[...]
        qpos = i * tq + _iota((tq, 1), 0)
        later_keys = (_iota((tk, tk), 0) > _iota((tk, tk), 1)).astype(BF16)
        this_and_later = (_iota((tk, tk), 0) >= _iota((tk, tk), 1)).astype(BF16)
        nk = (i * tq + tq + tk - 2) // tk

        def weights(j, c):
            z, mask, lsn, lm, k, off = _sb_tile(q, k_ref, nk - 1 - j, tk, qpos, scale)
            a = jnp.where(mask, jnp.exp(lsn + z + c + _suffix_mm(lm, later_keys)), 0.0)
            w = a * lax.dot_general(dov, v_ref[pl.ds(off, tk), :], NT, preferred_element_type=F32)
            return z, mask, lsn, lm, k, off, a, w

        def cond(st):
            return jnp.logical_and(st[0] < nk, st[3] > SB_DEAD)

        def step1(st):
            j, c, wc, _ = st
            _, _, _, lm, _, _, _, w = weights(j, c)
            c = c + jnp.sum(lm, axis=1, keepdims=True)
            return j + 1, c, wc + jnp.sum(w, axis=1, keepdims=True), jnp.max(c)

        zero = jnp.zeros((tq, 1), F32)
        live, _, total, _ = lax.while_loop(cond, step1, (jnp.int32(0), zero, zero, jnp.float32(0.0)))

        def step2(j, st):
            c, wc, dq = st
            z, mask, lsn, lm, k, off, a, w = weights(j, c)
            earlier = total - (wc + _suffix_mm(w, this_and_later))
            dz = jnp.where(mask, w * jnp.exp(lsn) - jnp.exp(lsn + z) * earlier, 0.0)
            dzb = dz.astype(BF16)
            dk_s[pl.ds(off, tk), :] += lax.dot_general(dzb, q, TN, preferred_element_type=F32)
            dv_s[pl.ds(off, tk), :] += lax.dot_general(a.astype(BF16), dov, TN, preferred_element_type=F32)
            return (c + jnp.sum(lm, axis=1, keepdims=True), wc + jnp.sum(w, axis=1, keepdims=True),
                    dq + jnp.dot(dzb, k, preferred_element_type=F32))

        dq = lax.fori_loop(0, live, step2, (zero, zero, jnp.zeros((tq, HEAD), F32)))[2]
        dq_ref[...] = (dq * scale).astype(BF16)

        @pl.when(i == nq - 1)
        def _():
            dk_ref[...] = (dk_s[...] * scale).astype(BF16)
            dv_ref[...] = dv_s[...].astype(BF16)

    q, k, v, tile, full = _sb_specs(s, nh, tq)
    return pl.pallas_call(
        body, name=name, grid=(nh, nq), in_specs=[q, k, v, tile], out_specs=[tile, full, full],
        out_shape=[jax.ShapeDtypeStruct((s, bw), BF16)] * 3,
        scratch_shapes=[pltpu.VMEM((s, HEAD), F32), pltpu.VMEM((s, HEAD), F32)],
        compiler_params=_cp(("arbitrary", "arbitrary")))(u, u, u, do)


def _band_onehot(r):
    kl = _iota((1, WIN), 1)
    ridx = jnp.clip(PADK + r - kl, -(CHUNK - 1), REL_CLIP) + (CHUNK - 1)
    first = (r // CHUNK) * CHUNK
    valid = jnp.logical_and(kl >= first, kl < first + BAND)
    onehot = jnp.logical_and(_iota((REL_PAD, WIN), 0) == ridx, valid)
    return onehot.astype(BF16), valid


def _bias_expand(name, table):
    def body(t_ref, o_ref):
        t = t_ref[...]
        hi = t.astype(BF16)
        r1 = t - hi.astype(F32)
        mid = r1.astype(BF16)
        lo = (r1 - mid.astype(F32)).astype(BF16)

        def row(r, carry):
            onehot, valid = _band_onehot(r)
            val = (jnp.dot(hi, onehot, preferred_element_type=F32) + jnp.dot(mid, onehot, preferred_element_type=F32)
                   + jnp.dot(lo, onehot, preferred_element_type=F32))
            o_ref[r] = jnp.where(valid, val, NEG)
            return carry

        lax.fori_loop(0, QBLK, row, 0)

    return pl.pallas_call(body, name=name, out_shape=jax.ShapeDtypeStruct((QBLK, 16, WIN), F32),
                          in_specs=[pl.BlockSpec(memory_space=pltpu.VMEM)],
                          out_specs=pl.BlockSpec(memory_space=pltpu.VMEM), compiler_params=_cp())(table)


def _bias_reduce(name, ds_rows):
    def body(x_ref, o_ref):
        def row(r, acc):
            onehot, _ = _band_onehot(r)
            x = x_ref[r]
            hi = x.astype(BF16)
            lo = (x - hi.astype(F32)).astype(BF16)
            return (acc + lax.dot_general(hi, onehot, NT, preferred_element_type=F32)
                    + lax.dot_general(lo, onehot, NT, preferred_element_type=F32))

        o_ref[...] = lax.fori_loop(0, QBLK, row, jnp.zeros((16, REL_PAD), F32))

    return pl.pallas_call(body, name=name, out_shape=jax.ShapeDtypeStruct((16, REL_PAD), F32),
                          in_specs=[pl.BlockSpec(memory_space=pltpu.VMEM)],
                          out_specs=pl.BlockSpec(memory_space=pltpu.VMEM), compiler_params=_cp())(ds_rows)


def _chunk_specs(s, nh):
    q = pl.BlockSpec((QBLK, HEAD), lambda h, i: (i, CQ * nh + h))
    kv = pl.BlockSpec((s + PADK, HEAD), lambda h, i: (0, h))
    bias = pl.BlockSpec((None, QBLK, WIN), lambda h, i: (h, 0, 0))
    tile = pl.BlockSpec((QBLK, HEAD), lambda h, i: (i, h))
    full = pl.BlockSpec((s, HEAD), lambda h, i: (0, h))
    return q, kv, bias, tile, full


def _chunk_probs(q, k_ref, b_ref, i, scale):
    off = pl.multiple_of(i * QBLK, QBLK)
    kw = k_ref[pl.ds(off, WIN), :]
    sc = lax.dot_general(q, kw, NT, preferred_element_type=F32) * scale + b_ref[...]
    sc = jnp.where(i * QBLK + _iota((1, WIN), 1) >= PADK, sc, NEG)
    p = jnp.exp(sc - jnp.max(sc, axis=1, keepdims=True))
    return p, jnp.sum(p, axis=1, keepdims=True), kw, off


def _chunk_fwd(name, u, kpad, vpad, bias, bw, side=None):
    s, nh = u.shape[0], bw // HEAD
    scale = HEAD ** -0.5

    def body(q_ref, k_ref, v_ref, b_ref, o_ref):
        p, l, _, off = _chunk_probs(q_ref[...], k_ref, b_ref, pl.program_id(1), scale)
        o = jnp.dot(p.astype(BF16), v_ref[pl.ds(off, WIN), :], preferred_element_type=F32)
        o_ref[...] = (o / l).astype(BF16)

    q, kv, bs, tile, _ = _chunk_specs(s, nh)
    return _pcall(body, (u, kpad, vpad, bias), name=name, grid=(nh, s // QBLK), in_specs=[q, kv, kv, bs],
                  out_specs=[tile], out_shape=[jax.ShapeDtypeStruct((s, bw), BF16)], sem=("parallel", "parallel"),
                  side=side)[0]


def _chunk_bwd(name, u, kpad, vpad, bias, do, bw):
    s, nh = u.shape[0], bw // HEAD
    nq = s // QBLK
    scale = HEAD ** -0.5

    def body(q_ref, k_ref, v_ref, b_ref, do_ref, dq_ref, dk_ref, dv_ref, dss_ref, dk_s, dv_s):
        i = pl.program_id(1)

        @pl.when(i == 0)
        def _():
            dk_s[...] = jnp.zeros(dk_s.shape, F32)
            dv_s[...] = jnp.zeros(dv_s.shape, F32)
            dss_ref[...] = jnp.zeros(dss_ref.shape, F32)

        q, dov = q_ref[...], do_ref[...]
        p, l, kw, off = _chunk_probs(q, k_ref, b_ref, i, scale)
        p = p / l
        dp = lax.dot_general(dov, v_ref[pl.ds(off, WIN), :], NT, preferred_element_type=F32)
        ds = p * (dp - jnp.sum(p * dp, axis=1, keepdims=True))
        dsb = ds.astype(BF16)
        dq_ref[...] = (jnp.dot(dsb, kw, preferred_element_type=F32) * scale).astype(BF16)
        dk_s[pl.ds(off, WIN), :] += lax.dot_general(dsb, q, TN, preferred_element_type=F32)
        dv_s[pl.ds(off, WIN), :] += lax.dot_general(p.astype(BF16), dov, TN, preferred_element_type=F32)
        dss_ref[...] += ds

        @pl.when(i == nq - 1)
        def _():
            dk_ref[...] = (dk_s[pl.ds(PADK, s), :] * scale).astype(BF16)
            dv_ref[...] = dv_s[pl.ds(PADK, s), :].astype(BF16)

    q, kv, bs, tile, full = _chunk_specs(s, nh)
    return pl.pallas_call(
        body, name=name, grid=(nh, nq), in_specs=[q, kv, kv, bs, tile], out_specs=[tile, full, full, bs],
        out_shape=[jax.ShapeDtypeStruct((s, bw), BF16)] * 3 + [jax.ShapeDtypeStruct((nh, QBLK, WIN), F32)],
        scratch_shapes=[pltpu.VMEM((s + PADK, HEAD), F32), pltpu.VMEM((s + PADK, HEAD), F32)],
        compiler_params=_cp(("arbitrary", "arbitrary")))(u, kpad, vpad, bias, do)


def _gelu_parts(y):
    th = jnp.tanh(GELU_K * (y + GELU_C * y * y * y))
    return 0.5 * y * (1.0 + th), th


def _block_diag(xb16, w_ref, nh, dims):
    return jnp.concatenate(
        [lax.dot_general(xb16[:, n * HEAD:(n + 1) * HEAD], w_ref[n], dims, preferred_element_type=F32)
         for n in range(nh)], axis=1)


def _lru_gates(ext, cw_ref, cb_ref, wr_ref, br_ref, wi_ref, bi_ref, lam_ref, ts, nh):
    shifted = [pltpu.roll(ext, CONV_WIDTH - 1 - j, 0)[8:, :] if j < CONV_WIDTH - 1 else ext[8:, :]
               for j in range(CONV_WIDTH)]
    xc = cb_ref[...]
    for j in range(CONV_WIDTH):
        xc = xc + shifted[j] * cw_ref[pl.ds(j, 1), :]
    xcb = xc.astype(BF16)
    r = _sigmoid(_block_diag(xcb, wr_ref, nh, NN) + br_ref[...])
    gi = _sigmoid(_block_diag(xcb, wi_ref, nh, NN) + bi_ref[...])
    lsl = _log_sigmoid(lam_ref[...])
    la = LRU_C * r * lsl
    a = jnp.exp(la)
    e2 = jnp.exp(2.0 * la)
    mult = jnp.sqrt(-jnp.tanh(la) * (e2 + 1.0))
    return shifted, xc, xcb, r, gi, lsl, a, e2, mult


def _lru_param_specs(bw, nh):
    vec = pl.BlockSpec((1, bw), lambda i: (0, 0))
    conv = pl.BlockSpec((8, bw), lambda i: (0, 0))
    blocks = pl.BlockSpec((nh, HEAD, HEAD), lambda i: (0, 0, 0))
    return [conv, vec, blocks, vec, blocks, vec, vec]


def _lru_fwd(name, u, params, bw):
    s, nh = u.shape[0], bw // HEAD
    ts = min(512, s)

    def body(rx_ref, ry_ref, cw_ref, cb_ref, wr_ref, br_ref, wi_ref, bi_ref, lam_ref, o_ref, h_ref, tail, hcar):
        @pl.when(pl.program_id(0) == 0)
        def _():
            tail[...] = jnp.zeros(tail.shape, F32)
            hcar[...] = jnp.zeros(hcar.shape, F32)

        rx = rx_ref[...].astype(F32)
        ext = jnp.concatenate([tail[...], rx], axis=0)
        tail[...] = rx[ts - 8:, :]
        _, xc, _, _, gi, _, a, _, mult = _lru_gates(ext, cw_ref, cb_ref, wr_ref, br_ref, wi_ref, bi_ref, lam_ref, ts, nh)
        acum, bcum = _scan_affine(a, mult * (gi * xc), False)
        h_ref[...] = bcum + acum * hcar[...]
        hcar[...] = h_ref[pl.ds(ts - 1, 1), :]
        o_ref[...] = (h_ref[...] * _gelu_parts(ry_ref[...].astype(F32))[0]).astype(BF16)

    row = pl.BlockSpec((ts, bw), lambda i: (i, 0))
    return pl.pallas_call(
        body, name=name, grid=(s // ts,),
        in_specs=[pl.BlockSpec((ts, bw), lambda i: (i, RX)), pl.BlockSpec((ts, bw), lambda i: (i, RY))]
        + _lru_param_specs(bw, nh),
        out_specs=[row, row],
        out_shape=[jax.ShapeDtypeStruct((s, bw), BF16), jax.ShapeDtypeStruct((s, bw), F32)],
        scratch_shapes=[pltpu.VMEM((8, bw), F32), pltpu.VMEM((1, bw), F32)],
        compiler_params=_cp(("arbitrary",)))(u, u, *params)


def _lru_bwd(name, u, h, do, params, bw):
    s, nh = u.shape[0], bw // HEAD
    ts = min(512, s)
    nb = s // ts
    t8 = ts // 8

    def body(rx_ref, rxp_ref, ry_ref, h_ref, hp_ref, do_ref, cw_ref, cb_ref, wr_ref, br_ref, wi_ref, bi_ref, lam_ref,
             drx_ref, dry_ref, dcw_ref, dcb_ref, dwr_ref, dbr_ref, dwi_ref, dbi_ref, dlam_ref, gcar, head):
        i = pl.program_id(0)
        first = i == nb - 1

        @pl.when(i == 0)
        def _():
            gcar[...] = jnp.zeros(gcar.shape, F32)
            head[...] = jnp.zeros(head.shape, F32)
            for ref in (dcw_ref, dcb_ref, dwr_ref, dbr_ref, dwi_ref, dbi_ref, dlam_ref):
                ref[...] = jnp.zeros(ref.shape, F32)

        rows = _iota((ts, 1), 0)
        rx = rx_ref[...].astype(F32)
        before = jnp.where(first, 0.0, rxp_ref[...].astype(F32))
        ext = jnp.concatenate([before, rx], axis=0)
        shifted, xc, xcb, r, gi, lsl, a, e2, mult = _lru_gates(
            ext, cw_ref, cb_ref, wr_ref, br_ref, wi_ref, bi_ref, lam_ref, ts, nh)

        ry = ry_ref[...].astype(F32)
        gel, th = _gelu_parts(ry)
        dgel = 0.5 * (1.0 + th) + 0.5 * ry * (1.0 - th * th) * GELU_K * (1.0 + 3.0 * GELU_C * ry * ry)
        dov = do_ref[...].astype(F32)
        hv = h_ref[...]
        dry_ref[...] = (dov * hv * dgel).astype(BF16)

        coef = jnp.where(rows < ts - 1, pltpu.roll(a, ts - 1, 0), 0.0)
        dh_in = dov * gel + jnp.where(rows == ts - 1, gcar[...], 0.0)
        dh = _scan_affine(coef, dh_in, True)[1]
        gcar[...] = jnp.sum(jnp.where(rows == 0, a * dh, 0.0), axis=0, keepdims=True)

        hprev = jnp.where(first, 0.0, hp_ref[...])
        hm1 = pltpu.roll(jnp.concatenate([hprev, hv], axis=0), 1, 0)[8:, :]
        dgx = dh * mult
        dla = dh * hm1 * a - dh * gi * xc * (e2 / mult)
        dpre_r = dla * (LRU_C * lsl) * r * (1.0 - r)
        dpre_i = dgx * xc * gi * (1.0 - gi)
        dlam_ref[...] += jnp.sum(dla * r, axis=0, keepdims=True) * (LRU_C * _sigmoid(-lam_ref[...]))
        dbr_ref[...] += jnp.sum(dpre_r, axis=0, keepdims=True)
        dbi_ref[...] += jnp.sum(dpre_i, axis=0, keepdims=True)
        drb, dib = dpre_r.astype(BF16), dpre_i.astype(BF16)
        for n in range(nh):
            cols = slice(n * HEAD, (n + 1) * HEAD)
            dwr_ref[n] += lax.dot_general(xcb[:, cols], drb[:, cols], TN, preferred_element_type=F32)
            dwi_ref[n] += lax.dot_general(xcb[:, cols], dib[:, cols], TN, preferred_element_type=F32)
        dxc = dgx * gi + _block_diag(drb, wr_ref, nh, NT) + _block_diag(dib, wi_ref, nh, NT)

        dcb_ref[...] += jnp.sum(dxc, axis=0, keepdims=True)
        for j in range(CONV_WIDTH):
            dcw_ref[pl.ds(j, 1), :] += jnp.sum(dxc * shifted[j], axis=0, keepdims=True)
        ext2 = jnp.concatenate([dxc, head[...]], axis=0)
        head[...] = dxc[:8, :]
        drx = dxc * cw_ref[pl.ds(CONV_WIDTH - 1, 1), :]
        for j in range(CONV_WIDTH - 1):
            up = CONV_WIDTH - 1 - j
            drx = drx + pltpu.roll(ext2, ts + 8 - up, 0)[:ts, :] * cw_ref[pl.ds(j, 1), :]
        drx_ref[...] = drx.astype(BF16)

    def blk(col):
        return lambda i: (nb - 1 - i, col)

    def prev8(col):
        return lambda i: (jnp.maximum((nb - 1 - i) * t8 - 1, 0), col)

    vec = pl.BlockSpec((1, bw), lambda i: (0, 0))
    conv = pl.BlockSpec((8, bw), lambda i: (0, 0))
    blocks = pl.BlockSpec((nh, HEAD, HEAD), lambda i: (0, 0, 0))
    return pl.pallas_call(
        body, name=name, grid=(nb,),
        in_specs=[pl.BlockSpec((ts, bw), blk(RX)), pl.BlockSpec((8, bw), prev8(RX)), pl.BlockSpec((ts, bw), blk(RY)),
                  pl.BlockSpec((ts, bw), blk(0)), pl.BlockSpec((8, bw), prev8(0)), pl.BlockSpec((ts, bw), blk(0))]
        + _lru_param_specs(bw, nh),
        out_specs=[pl.BlockSpec((ts, bw), blk(0)), pl.BlockSpec((ts, bw), blk(0)), conv, vec, blocks, vec, blocks, vec, vec],
        out_shape=[jax.ShapeDtypeStruct((s, bw), BF16)] * 2
        + [jax.ShapeDtypeStruct((8, bw), F32), jax.ShapeDtypeStruct((1, bw), F32),
           jax.ShapeDtypeStruct((nh, HEAD, HEAD), F32), jax.ShapeDtypeStruct((1, bw), F32),
           jax.ShapeDtypeStruct((nh, HEAD, HEAD), F32), jax.ShapeDtypeStruct((1, bw), F32),
           jax.ShapeDtypeStruct((1, bw), F32)],
        scratch_shapes=[pltpu.VMEM((1, bw), F32), pltpu.VMEM((8, bw), F32)],
        compiler_params=_cp(("arbitrary",)))(u, u, u, h, h, do, *params)


def _gate_merge(name, xb, w_gate, b_gate, o_all, w_branch, l, side=None):
    s, d = xb.shape
    bw = o_all.shape[2]
    tm, tn = min(512, s), min(256, d)

    def body(x_ref, wg_ref, bg_ref, o_ref, wb_ref, m_ref, g_ref, p_ref):
        x = x_ref[...]
        acc = jnp.zeros((tm, tn), F32)
        for g in range(4):
            gate = _sigmoid(jnp.dot(x, wg_ref[g], preferred_element_type=F32) + bg_ref[g])
            proj = jnp.dot(o_ref[g], wb_ref[g], preferred_element_type=F32)
            term = gate * proj
            g_ref[g] = gate.astype(BF16)
            p_ref[g] = (term * (1.0 - gate)).astype(BF16)
            acc = acc + term
        m_ref[...] = acc.astype(BF16)

    quad = pl.BlockSpec((4, tm, tn), lambda n, m: (0, m, n))
    return _pcall(
        body, (xb, w_gate, b_gate, o_all, w_branch), name=name, grid=(d // tn, s // tm),
        in_specs=[pl.BlockSpec((tm, d), lambda n, m: (m, 0)),
                  pl.BlockSpec((None, 4, d, tn), lambda n, m: (0, 0, 0, n)),
                  pl.BlockSpec((None, 4, 1, tn), lambda n, m: (l, 0, 0, n)),
                  pl.BlockSpec((4, tm, bw), lambda n, m: (0, m, 0)),
                  pl.BlockSpec((None, 4, bw, tn), lambda n, m: (0, 0, 0, n))],
        out_specs=[pl.BlockSpec((tm, tn), lambda n, m: (m, n)), quad, quad],
        out_shape=[jax.ShapeDtypeStruct((s, d), BF16), jax.ShapeDtypeStruct((4, s, d), BF16),
                   jax.ShapeDtypeStruct((4, s, d), BF16)],
        sem=("parallel", "parallel"), side=side)


def _adamw(name, w, m, v, parts, layer=0, layers=1, earlier=None):
    cols = w.shape[1]
    p, rows = parts.shape[0], parts.shape[1]
    tr = _pow2_rows(rows, cols * max(1, p // 2))
    nb = rows // tr
    c1 = 1.0 - ADAM_B1 ** ADAM_STEP
    c2 = 1.0 - ADAM_B2 ** ADAM_STEP

    def body(w_ref, m_ref, v_ref, g_ref, *rest):
        go_ref, do_ref, mo_ref, vo_ref = rest[-4:]
        g = g_ref[0].astype(F32)
        for k in range(1, p):
            g = g + g_ref[k].astype(F32)
        m2 = ADAM_B1 * m_ref[...] + (1.0 - ADAM_B1) * g
        v2 = ADAM_B2 * v_ref[...] + (1.0 - ADAM_B2) * (g * g)
        go_ref[...] = g
        do_ref[...] = -ADAM_LR * ((m2 / c1) / (jnp.sqrt(v2 / c2) + ADAM_EPS) + ADAM_WD * w_ref[...])
        mo_ref[...] = m2
        vo_ref[...] = v2

    row = pl.BlockSpec((tr, cols), lambda i: (layer * nb + i, 0))
    held = list(earlier) if earlier is not None else []
    return pl.pallas_call(
        body, name=name, grid=(nb,),
        in_specs=[row, row, row, pl.BlockSpec((p, tr, cols), lambda i: (0, i, 0))] + [ANY] * len(held),
        out_specs=[row] * 4, out_shape=[jax.ShapeDtypeStruct((layers * rows, cols), F32)] * 4,
        input_output_aliases={4 + k: k for k in range(len(held))},
        compiler_params=_cp(("parallel",)))(w, m, v, parts, *held)


PACK_ROWS = 512


def _pack(arrays):
    rows = []
    for a in arrays:
        flat = a.astype(F32).reshape(-1)
        rows.append(jnp.pad(flat, (0, (-flat.shape[0]) % LANE)).reshape(-1, LANE))
    rows = jnp.concatenate(rows)
    return jnp.pad(rows, ((0, (-rows.shape[0]) % PACK_ROWS), (0, 0)))


def _unpack(packed, shapes):
    out, row = [], 0
    for shp in shapes:
        n = math.prod(shp)
        nrows = -(-n // LANE)
        out.append(packed[row:row + nrows].reshape(-1)[:n].reshape(shp))
        row += nrows
    return out


def _unshard(gathered, axis):
    block = gathered.shape[2:]
    full = jnp.swapaxes(gathered, 0, 1).reshape((N_DEV,) + block)
    full = jnp.moveaxis(full, 0, axis)
    return full.reshape(block[:axis] + (N_DEV * block[axis],) + block[axis + 1:])


def kernel(x, ln_in_g, ln_in_b, w_in, b_forget, conv_w, conv_b, w_r, b_r, w_i, b_i, lru_lambda, rel_bias, w_branch, w_gate, b_gate, w_out, ln1_g, ln1_b, w_ff1, w_ff2, ln2_g, ln2_b, loss_target, m_ln_in_g, m_ln_in_b, m_w_in, m_b_forget, m_conv_w, m_conv_b, m_w_r, m_b_r, m_w_i, m_b_i, m_lru_lambda, m_rel_bias, m_w_branch, m_w_gate, m_b_gate, m_w_out, m_ln1_g, m_ln1_b, m_w_ff1, m_w_ff2, m_ln2_g, m_ln2_b, v_ln_in_g, v_ln_in_b, v_w_in, v_b_forget, v_conv_w, v_conv_b, v_w_r, v_b_r, v_w_i, v_b_i, v_lru_lambda, v_rel_bias, v_w_branch, v_w_gate, v_b_gate, v_w_out, v_ln1_g, v_ln1_b, v_w_ff1, v_w_ff2, v_ln2_g, v_ln2_b):
    given = dict(zip(
        NAMES + ['loss_target'] + ['m_' + n for n in WEIGHTS] + ['v_' + n for n in WEIGHTS],
        (x, ln_in_g, ln_in_b, w_in, b_forget, conv_w, conv_b, w_r, b_r, w_i, b_i, lru_lambda, rel_bias, w_branch, w_gate, b_gate, w_out, ln1_g, ln1_b, w_ff1, w_ff2, ln2_g, ln2_b, loss_target, m_ln_in_g, m_ln_in_b, m_w_in, m_b_forget, m_conv_w, m_conv_b, m_w_r, m_b_r, m_w_i, m_b_i, m_lru_lambda, m_rel_bias, m_w_branch, m_w_gate, m_b_gate, m_w_out, m_ln1_g, m_ln1_b, m_w_ff1, m_w_ff2, m_ln2_g, m_ln2_b, v_ln_in_g, v_ln_in_b, v_w_in, v_b_forget, v_conv_w, v_conv_b, v_w_r, v_b_r, v_w_i, v_b_i, v_lru_lambda, v_rel_bias, v_w_branch, v_w_gate, v_b_gate, v_w_out, v_ln1_g, v_ln1_b, v_w_ff1, v_w_ff2, v_ln2_g, v_ln2_b)))

    s, d = x.shape[1], x.shape[2]
    nl = w_in.shape[0]
    bw = d // 4
    nh = bw // HEAD
    nu = 11 * bw
    rs = d // N_DEV
    dff = w_ff1.shape[2] * N_DEV
    fs = dff // N_DEV
    cs = d // N_DEV
    assert nl == DEPTH and nh * HEAD == bw and s % 256 == 0 and d % 1024 == 0

    xi, yi, ci = _position()
    dev = 4 * xi + 2 * yi + ci
    c_arr = jnp.reshape(ci, (1,)).astype(I32)

    w_main = jnp.concatenate([w_in[..., :3 * bw], w_in[..., 3 * bw + nh:]], axis=-1).astype(BF16)
    w_fcol = jnp.pad(w_in[..., 3 * bw:3 * bw + nh], ((0, 0), (0, 0), (0, LANE - nh))).astype(BF16)
    small_shapes = [conv_w.shape, rel_bias.shape, b_gate.shape]
    shard = {'main': w_main, 'f': w_fcol, 'branch': w_branch.astype(BF16), 'gate': w_gate.astype(BF16),
             'out': w_out.astype(BF16), 'ff1': w_ff1.astype(BF16), 'ff2': w_ff2.astype(BF16)}
    shard_axis = {'main': 1, 'f': 1, 'branch': 3, 'gate': 2, 'out': 1, 'ff1': 2, 'ff2': 1}
    W = [dict() for _ in range(nl)]

    def gather_chips(l, keys, extra=()):
        side = _chips_side([shard[k][l:l + 1] for k in keys] + list(extra), True)
        side.todo = (l, keys)
        return side

    def gather_cores(chips):
        side = _cores_side(chips.results, True)
        side.todo = chips.todo
        return side

    def arrived(cores):
        l, keys = cores.todo
        for k, res in zip(keys, cores.results):
            W[l][k] = _unshard(res, shard_axis[k])
        return cores.results[len(keys):]

    first = gather_chips(0, ['main', 'f'], [_pack([conv_w, rel_bias, b_gate])])
    _run_side("gather_chips_first", first)
    first = gather_cores(first)
    _run_side("gather_cores_first", first)
    small = arrived(first)[0]
    small = jnp.swapaxes(small, 0, 1).reshape((N_DEV,) + small.shape[2:])
    small = [_unpack(small[j], small_shapes) for j in range(N_DEV)]
    conv_w_full = jnp.concatenate([small[j][0] for j in range(N_DEV)], axis=-1)
    rel_bias_full = jnp.concatenate([small[j][1] for j in range(N_DEV)], axis=-1)
    b_gate_full = jnp.concatenate([small[j][2] for j in range(N_DEV)], axis=-1)
    b_gate4 = b_gate_full.reshape(nl, 4, 1, d)

    def lru_params(l):
        return (jnp.pad(conv_w_full[l], ((0, 8 - CONV_WIDTH), (0, 0))), conv_b[l].reshape(1, bw),
                w_r[l].astype(BF16), b_r[l].reshape(1, bw), w_i[l].astype(BF16), b_i[l].reshape(1, bw),
                lru_lambda[l].reshape(1, bw))

    def bias_rows(l):
        return jnp.pad(rel_bias_full[l], ((0, 16 - nh), (0, REL_PAD - REL_TABLE)))

    tm = min(1024, s)
    tkk = min(2048, d)

    xs = x[0]
    h0, h0b = _ln_fwd("ln_in", xs, ln_in_g, ln_in_b)
    saved = []
    cur, curb = h0, h0b
    chips = {}
    for l in range(nl):
        side = None
        if l == 0:
            side = chips['b0'] = gather_chips(0, ['gate', 'branch'])
        else:
            side = last_cores = gather_cores(chips.pop('d1'))
        u = _mm(f"w_in_{l}", curb, W[l]['main'], grid=(s // tm, nu // bw, d // tkk),
                a_spec=pl.BlockSpec((tm, tkk), lambda m, n, k: (m, k)),
                b_spec=pl.BlockSpec((None, tkk, bw), lambda m, n, k: (0, k, n)),
                dims=NN, acc_shape=(tm, bw), out_shape=[jax.ShapeDtypeStruct((s, nu), BF16)],
                out_specs=[pl.BlockSpec((tm, bw), lambda m, n, k: (m, n))], finish=_store(BF16), side=side)[0]
        if l == 1:
            arrived(last_cores)
        fl = _mm(f"w_forget_{l}", curb, W[l]['f'], grid=(s // tm, 1, d // tkk),
                 a_spec=pl.BlockSpec((tm, tkk), lambda m, n, k: (m, k)),
                 b_spec=pl.BlockSpec((None, tkk, LANE), lambda m, n, k: (0, k, 0)),
                 dims=NN, acc_shape=(tm, LANE), out_shape=[jax.ShapeDtypeStruct((s, LANE), F32)],
                 out_specs=[pl.BlockSpec((tm, LANE), lambda m, n, k: (m, 0))], finish=_store(F32))[0]
        bf_row = jnp.pad(b_forget[l], (0, LANE - nh)).reshape(1, LANE)
        cf = _cum_forget_fwd(f"cum_forget_{l}", fl, bf_row)
        tkf = _fox_tile(s)
        cf_heads = cf[:, :nh].T
        cf_col = cf_heads.reshape(nh, s, 1)
        cf_row = cf_heads.reshape(nh, s // tkf, 1, tkf)
        side = None
        if l == 0:
            b0 = gather_cores(chips.pop('b0'))
            chips['c0'] = gather_chips(0, ['out', 'ff1', 'ff2'])
            side = _merge_sides(b0, chips['c0'])
        o_fox, lse = _fox_fwd(f"fox_fwd_{l}", u, cf_col, cf_row, bw, side=side)
        if l == 0:
            arrived(b0)
        lp = lru_params(l)
        o_lru, hstate = _lru_fwd(f"lru_fwd_{l}", u, lp, bw)
        o_sb = _sb_fwd(f"sb_fwd_{l}", u, bw)
        bias = jnp.transpose(_bias_expand(f"bias_expand_{l}", bias_rows(l)), (1, 0, 2))[:nh]
        kpad = jnp.pad(u[:, CK * bw:(CK + 1) * bw], ((PADK, 0), (0, 0)))
        vpad = jnp.pad(u[:, CV * bw:(CV + 1) * bw], ((PADK, 0), (0, 0)))
        side = gather_cores(chips.pop('c0')) if l == 0 else None
        o_ch = _chunk_fwd(f"chunk_fwd_{l}", u, kpad, vpad, bias, bw, side=side)
        if l == 0:
            arrived(side)
        o_all = jnp.stack([o_fox, o_lru, o_sb, o_ch])
        side = None
        if l == 0:
            side = chips['a1'] = gather_chips(1, ['main', 'f', 'gate', 'branch'])
        merged, gates, projs = _gate_merge(f"gate_merge_{l}", curb, W[l]['gate'], b_gate4, o_all, W[l]['branch'], l,
                                           side=side)
        side = gather_cores(chips.pop('a1')) if l == 0 else None
        z1, x1, x1b = _mm_ln(f"w_out_ln1_{l}", merged, W[l]['out'], 0, cur, ln1_g[l], ln1_b[l], side=side)
        if l == 0:
            arrived(side)
        tn1 = min(1024, dff)

        def ff1_finish(acc, ex, outs, ids):
            outs[0][...] = acc.astype(BF16)
            r = jnp.maximum(acc, 0.0)
            outs[1][...] = (r * r).astype(BF16)

        side = None
        if l == 0:
            side = chips['c1'] = gather_chips(1, ['out', 'ff1'])
        hp, hid = _mm(f"w_ff1_{l}", x1b, W[l]['ff1'], grid=(s // tm, dff // tn1, d // tkk),
                      a_spec=pl.BlockSpec((tm, tkk), lambda m, n, k: (m, k)),
                      b_spec=pl.BlockSpec((None, tkk, tn1), lambda m, n, k: (0, k, n)),
                      dims=NN, acc_shape=(tm, tn1),
                      out_shape=[jax.ShapeDtypeStruct((s, dff), BF16)] * 2,
                      out_specs=[pl.BlockSpec((tm, tn1), lambda m, n, k: (m, n))] * 2, finish=ff1_finish, side=side)
        side = None
        if l == 0:
            c1 = gather_cores(chips.pop('c1'))
            chips['d1'] = gather_chips(1, ['ff2'])
            side = _merge_sides(c1, chips['d1'])
        z2, x2, x2b = _mm_ln(f"w_ff2_ln2_{l}", hid, W[l]['ff2'], 0, x1, ln2_g[l], ln2_b[l], side=side)
        if l == 0:
            arrived(c1)
        saved.append(dict(xin=cur, xinb=curb, u=u, fl=fl, bf_row=bf_row, cf_col=cf_col, cf_row=cf_row, o_fox=o_fox,
                          lse=lse, lp=lp, hstate=hstate, bias=bias, kpad=kpad, vpad=vpad, o_all=o_all, merged=merged,
                          gates=gates, projs=projs, z1=z1, x1=x1, x1b=x1b, hp=hp, hid=hid, z2=z2))
        cur, curb = x2, x2b

    loss_tile, dcur = _loss_head("loss_head", cur, loss_target[0])
    loss = lax.psum(loss_tile[0, 0], ("x", "y", "c"))

    big = [dict() for _ in range(nl)]
    reduced = [dict() for _ in range(nl)]
    sm = {n: [None] * nl for n in ['b_forget', 'conv_w', 'conv_b', 'w_r', 'b_r', 'w_i', 'b_i', 'lru_lambda', 'rel_bias',
                                   'b_gate', 'ln1_g', 'ln1_b', 'ln2_g', 'ln2_b']}

    def split_columns(acc, ex, outs, ids):
        for j in range(N_DEV):
            outs[0][j] = acc[:, j * cs:(j + 1) * cs]

    def grad_mm(key, name, a, b, *, shape, grid, a_spec, b_spec, out_spec, acc_shape, finish=_store(F32)):
        l = int(name[-1])
        big[l][key] = _mm(name, a, b, grid=grid, a_spec=a_spec, b_spec=b_spec, dims=TN, acc_shape=acc_shape,
                          out_shape=[jax.ShapeDtypeStruct(shape, F32)], out_specs=[out_spec], finish=finish)[0]

    def reduce_cores(l, keys):
        side = _cores_side([big[l][k].reshape((4, 2) + big[l][k].shape[1:]) for k in keys], False)
        side.todo = (l, keys)
        return side

    def reduce_chips(cores):
        l, keys = cores.todo
        partial = []
        for k, mine, other in zip(keys, cores.operands, cores.results):
            cols = mine.shape[-1]
            rows = math.prod(mine.shape[2:]) // cols
            partial.append(_add_core_halves(f"add_cores_{k}_{l}", mine.reshape(4, 2, rows, cols),
                                            other.reshape(4, rows, cols), c_arr, BF16))
        side = _chips_side(partial, False)
        side.todo = (l, keys)
        return side

    def reduction_done(chips_side):
        l, keys = chips_side.todo
        reduced[l].update(zip(keys, chips_side.results))

    GROUP1, GROUP2 = ['w_ff2', 'w_ff1', 'w_out', 'w_branch', 'w_gate'], ['w_main', 'w_f']
    pending = None

    tks = min(1024, s)
    tmr = min(1024, d)
    nsh = tmr // rs

    for l in reversed(range(nl)):
        sv = saved[l]
        Wl = W[l]
        dz2, dz2b, dg, db = _ln_bwd(f"ln2_bwd_{l}", dcur, sv['z2'], ln2_g[l])
        sm['ln2_g'][l], sm['ln2_b'][l] = dg[0], db[0]
        tn1 = min(1024, dff)

        def dhp_finish(acc, ex, outs, ids):
            outs[0][...] = (acc * (2.0 * jnp.maximum(ex[0][...].astype(F32), 0.0))).astype(BF16)

        dhp = _mm(f"d_hidden_{l}", dz2b, Wl['ff2'], grid=(s // tm, dff // tn1, d // tkk),
                  a_spec=pl.BlockSpec((tm, tkk), lambda m, n, k: (m, k)),
                  b_spec=pl.BlockSpec((None, tn1, tkk), lambda m, n, k: (0, n, k)),
                  dims=NT, acc_shape=(tm, tn1), out_shape=[jax.ShapeDtypeStruct((s, dff), BF16)],
                  out_specs=[pl.BlockSpec((tm, tn1), lambda m, n, k: (m, n))], finish=dhp_finish,
                  extras=(sv['hp'],), extra_specs=(pl.BlockSpec((tm, tn1), lambda m, n, k: (m, n)),),
                  side=pending)[0]
        if pending is not None:
            reduction_done(pending)
            pending = None
        grad_mm('w_ff2', f"g_w_ff2_{l}", sv['hid'], dz2b, shape=(N_DEV, 1, fs, d), grid=(N_DEV, 1, s // tks),
                a_spec=pl.BlockSpec((tks, fs), lambda m, n, k: (k, m)),
                b_spec=pl.BlockSpec((tks, d), lambda m, n, k: (k, 0)),
                out_spec=pl.BlockSpec((None, None, fs, d), lambda m, n, k: (m, 0, 0, 0)), acc_shape=(fs, d))
        grad_mm('w_ff1', f"g_w_ff1_{l}", sv['x1b'], dhp, shape=(N_DEV, 1, d, fs), grid=(d // tmr, N_DEV, s // tks),
                a_spec=pl.BlockSpec((tks, tmr), lambda m, n, k: (k, m)),
                b_spec=pl.BlockSpec((tks, fs), lambda m, n, k: (k, n)),
                out_spec=pl.BlockSpec((None, None, tmr, fs), lambda m, n, k: (n, 0, m, 0)), acc_shape=(tmr, fs))
        tnd = min(1024, d)

        def resid_finish(scale):
            def finish(acc, ex, outs, ids):
                outs[0][...] = acc + scale * ex[0][...]
            return finish

        tile_md = pl.BlockSpec((tm, tnd), lambda m, n, k: (m, n))
        dx1 = _mm(f"d_x1_{l}", dhp, Wl['ff1'], grid=(s // tm, d // tnd, dff // tkk),
                  a_spec=pl.BlockSpec((tm, tkk), lambda m, n, k: (m, k)),
                  b_spec=pl.BlockSpec((None, tnd, tkk), lambda m, n, k: (0, n, k)),
                  dims=NT, acc_shape=(tm, tnd), out_shape=[jax.ShapeDtypeStruct((s, d), F32)],
                  out_specs=[tile_md], finish=resid_finish(ALPHA), extras=(dz2,), extra_specs=(tile_md,))[0]

        dz1, dz1b, dg, db = _ln_bwd(f"ln1_bwd_{l}", dx1, sv['z1'], ln1_g[l])
        sm['ln1_g'][l], sm['ln1_b'][l] = dg[0], db[0]
        tmg, tng = min(512, s), min(512, d)

        def gate_finish(acc, ex, outs, ids):
            @pl.when(ids[1] == 0)
            def _():
                outs[2][...] = jnp.zeros(outs[2].shape, F32)

            ones = jnp.ones((8, tmg), BF16)
            for g in range(4):
                dpre = (acc * ex[1][g].astype(F32)).astype(BF16)
                outs[0][g] = (acc * ex[0][g].astype(F32)).astype(BF16)
                outs[1][g] = dpre
                outs[2][g] += jnp.dot(ones, dpre, preferred_element_type=F32)

        quad = pl.BlockSpec((4, tmg, tng), lambda n, m, k: (0, m, n))
        dproj, dpre, dbg = _mm(
            f"d_merged_{l}", dz1b, Wl['out'], grid=(d // tng, s // tmg, d // tkk),
            a_spec=pl.BlockSpec((tmg, tkk), lambda n, m, k: (m, k)),
            b_spec=pl.BlockSpec((None, tng, tkk), lambda n, m, k: (0, n, k)),
            dims=NT, acc_shape=(tmg, tng),
            out_shape=[jax.ShapeDtypeStruct((4, s, d), BF16), jax.ShapeDtypeStruct((4, s, d), BF16),
                       jax.ShapeDtypeStruct((4, 8, d), F32)],
            out_specs=[quad, quad, pl.BlockSpec((4, 8, tng), lambda n, m, k: (0, 0, n))], finish=gate_finish,
            extras=(sv['gates'], sv['projs']), extra_specs=(quad, quad), sem=("arbitrary", "arbitrary", "arbitrary"))
        sm['b_gate'][l] = dbg[:, 0, :]
        grad_mm('w_out', f"g_w_out_{l}", sv['merged'], dz1b, shape=(N_DEV, 1, rs, d), grid=(d // tmr, 1, s // tks),
                a_spec=pl.BlockSpec((tks, tmr), lambda m, n, k: (k, m)),
                b_spec=pl.BlockSpec((tks, d), lambda m, n, k: (k, 0)),
                out_spec=pl.BlockSpec((nsh, None, rs, d), lambda m, n, k: (m, 0, 0, 0)), acc_shape=(tmr, d))

        nm = s // tm
        do_all = _mm(f"d_branch_{l}", dproj, Wl['branch'], grid=(4 * nm, 1, d // tkk),
                     a_spec=pl.BlockSpec((None, tm, tkk), lambda m, n, k: (m // nm, m % nm, k)),
                     b_spec=pl.BlockSpec((None, None, bw, tkk), lambda m, n, k: (0, m // nm, 0, k)),
                     dims=NT, acc_shape=(tm, bw), out_shape=[jax.ShapeDtypeStruct((4, s, bw), BF16)],
                     out_specs=[pl.BlockSpec((None, tm, bw), lambda m, n, k: (m // nm, m % nm, 0))],
                     finish=_store(BF16))[0]
        grad_mm('w_branch', f"g_w_branch_{l}", sv['o_all'], dproj, shape=(N_DEV, 1, 4, bw, cs),
                grid=(4, 1, s // tks), finish=split_columns,
                a_spec=pl.BlockSpec((None, tks, bw), lambda m, n, k: (m, k, 0)),
                b_spec=pl.BlockSpec((None, tks, d), lambda m, n, k: (m, k, 0)),
                out_spec=pl.BlockSpec((N_DEV, None, None, bw, cs), lambda m, n, k: (0, 0, m, 0, 0)),
                acc_shape=(bw, d))
        grad_mm('w_gate', f"g_w_gate_{l}", sv['xinb'], dpre, shape=(N_DEV, 1, 4, rs, d), grid=(d // tmr, 4, s // tks),
                a_spec=pl.BlockSpec((tks, tmr), lambda m, n, k: (k, m)),
                b_spec=pl.BlockSpec((None, tks, d), lambda m, n, k: (n, k, 0)),
                out_spec=pl.BlockSpec((nsh, None, None, rs, d), lambda m, n, k: (m, 0, n, 0, 0)),
                acc_shape=(tmr, d))
        nkg = d // tkk
        cores1 = reduce_cores(l, GROUP1)
        dx_gate = _mm(f"d_x_gates_{l}", dpre, Wl['gate'], grid=(s // tm, d // tnd, 4 * nkg),
                      a_spec=pl.BlockSpec((None, tm, tkk), lambda m, n, k: (k // nkg, m, k % nkg)),
                      b_spec=pl.BlockSpec((None, None, tnd, tkk), lambda m, n, k: (0, k // nkg, n, k % nkg)),
                      dims=NT, acc_shape=(tm, tnd), out_shape=[jax.ShapeDtypeStruct((s, d), F32)],
                      out_specs=[tile_md], finish=resid_finish(ALPHA), extras=(dz1,), extra_specs=(tile_md,),
                      side=cores1)[0]

        u = sv['u']
        chips1 = reduce_chips(cores1)
        dfq, dfk, dfv, dcc, dcr = _fox_bwd(f"fox_bwd_{l}", u, sv['cf_col'], sv['cf_row'], sv['o_fox'], do_all[0],
                                          sv['lse'], bw, side=chips1)
        reduction_done(chips1)
        dcf = (dcc[:, :, 0, :].reshape(nh, s) + dcr.reshape(nh, s)).T
        dflb, dbf = _cum_forget_bwd(f"cum_forget_bwd_{l}", jnp.pad(dcf, ((0, 0), (0, LANE - nh))), sv['fl'],
                                    sv['bf_row'])
        sm['b_forget'][l] = dbf[0, :nh]
        drx, dry, dcw, dcb, dwr, dbr, dwi, dbi, dlam = _lru_bwd(f"lru_bwd_{l}", u, sv['hstate'], do_all[1], sv['lp'], bw)
        sm['conv_w'][l], sm['conv_b'][l], sm['w_r'][l], sm['b_r'][l] = dcw[:CONV_WIDTH], dcb[0], dwr, dbr[0]
        sm['w_i'][l], sm['b_i'][l], sm['lru_lambda'][l] = dwi, dbi[0], dlam[0]
        dsq, dsk, dsv = _sb_bwd(f"sb_bwd_{l}", u, do_all[2], bw)
        dcq, dck, dcv, dss = _chunk_bwd(f"chunk_bwd_{l}", u, sv['kpad'], sv['vpad'], sv['bias'], do_all[3], bw)
        dss_rows = jnp.pad(jnp.transpose(dss, (1, 0, 2)), ((0, 0), (0, 16 - nh), (0, 0)))
        sm['rel_bias'][l] = _bias_reduce(f"bias_reduce_{l}", dss_rows)[:nh, :REL_TABLE]
        du = jnp.concatenate([dfq, dfk, dfv, drx, dry, dsq, dsk, dsv, dcq, dck, dcv], axis=1)

        tnu = 11 * LANE
        grad_mm('w_main', f"g_w_in_{l}", sv['xinb'], du, shape=(N_DEV, 1, rs, nu), grid=(d // tmr, nu // tnu, s // tks),
                a_spec=pl.BlockSpec((tks, tmr), lambda m, n, k: (k, m)),
                b_spec=pl.BlockSpec((tks, tnu), lambda m, n, k: (k, n)),
                out_spec=pl.BlockSpec((nsh, None, rs, tnu), lambda m, n, k: (m, 0, 0, n)), acc_shape=(tmr, tnu))
        grad_mm('w_f', f"g_w_forget_{l}", sv['xinb'], dflb, shape=(N_DEV, 1, rs, LANE), grid=(d // tmr, 1, s // tks),
                a_spec=pl.BlockSpec((tks, tmr), lambda m, n, k: (k, m)),
                b_spec=pl.BlockSpec((tks, LANE), lambda m, n, k: (k, 0)),
                out_spec=pl.BlockSpec((nsh, None, rs, LANE), lambda m, n, k: (m, 0, 0, 0)), acc_shape=(tmr, LANE))
        tku = 11 * LANE
        cores2 = reduce_cores(l, GROUP2)
        dxa = _mm(f"d_x_in_{l}", du, Wl['main'], grid=(s // tm, d // tnd, nu // tku),
                  a_spec=pl.BlockSpec((tm, tku), lambda m, n, k: (m, k)),
                  b_spec=pl.BlockSpec((None, tnd, tku), lambda m, n, k: (0, n, k)),
                  dims=NT, acc_shape=(tm, tnd), out_shape=[jax.ShapeDtypeStruct((s, d), F32)],
                  out_specs=[tile_md], finish=resid_finish(1.0), extras=(dx_gate,), extra_specs=(tile_md,),
                  side=cores2)[0]
        dcur = _mm(f"d_x_forget_{l}", dflb, Wl['f'], grid=(s // tm, d // tnd, 1),
                   a_spec=pl.BlockSpec((tm, LANE), lambda m, n, k: (m, 0)),
                   b_spec=pl.BlockSpec((None, tnd, LANE), lambda m, n, k: (0, n, 0)),
                   dims=NT, acc_shape=(tm, tnd), out_shape=[jax.ShapeDtypeStruct((s, d), F32)],
                   out_specs=[tile_md], finish=resid_finish(1.0), extras=(dxa,), extra_specs=(tile_md,))[0]
        pending = reduce_chips(cores2)

    grad_x, _, dg_in, db_in = _ln_bwd("ln_in_bwd", dcur, xs, ln_in_g)
    _run_side("reduce_chips_last", pending)
    reduction_done(pending)

    small_names = ['ln_in_g', 'ln_in_b', 'b_forget', 'conv_w', 'conv_b', 'w_r', 'b_r', 'w_i', 'b_i', 'lru_lambda',
                   'rel_bias', 'b_gate', 'ln1_g', 'ln1_b', 'ln2_g', 'ln2_b']
    local = {'ln_in_g': dg_in[0], 'ln_in_b': db_in[0]}
    for n in small_names[2:]:
        local[n] = jnp.stack(sm[n])
    full_shapes = [local[n].shape for n in small_names]
    packed = _pack([local[n] for n in small_names])
    every = _run_side("gather_small_cores",
                      _cores_side(_run_side("gather_small_chips", _chips_side([packed], True)), True))[0]
    every = jnp.swapaxes(every, 0, 1).reshape((N_DEV,) + packed.shape)
    total = dict(zip(small_names, _unpack(_sum_parts("sum_small", every), full_shapes)))
    for n, width in (('conv_w', bw // N_DEV), ('rel_bias', REL_TABLE // N_DEV), ('b_gate', cs)):
        total[n] = lax.dynamic_slice_in_dim(total[n], dev * width, width, axis=2)

    out = {}

    def update(n, parts):
        cols = parts[0].shape[2]
        w2, m2, v2 = (given[p + n].reshape(-1, cols) for p in ('', 'm_', 'v_'))
        res = None
        for l in range(nl):
            res = _adamw(f"adamw_{n}_{l}", w2, m2, v2, parts[l], layer=l, layers=nl, earlier=res)
        out[n] = [r.reshape(given[n].shape) for r in res]

    def parts_w_in(l):
        pm, pf = reduced[l]['w_main'], reduced[l]['w_f']
        return jnp.concatenate([pm[..., :3 * bw], pf[..., :nh], pm[..., 3 * bw:]], axis=-1)

    update('w_in', [parts_w_in(l) for l in range(nl)])
    for n in ('w_branch', 'w_gate', 'w_out', 'w_ff1', 'w_ff2'):
        update(n, [reduced[l][n] for l in range(nl)])

    small_shapes2 = [given[n].shape for n in small_names]
    res = _adamw("adamw_small", _pack([given[n] for n in small_names]), _pack([given['m_' + n] for n in small_names]),
                 _pack([given['v_' + n] for n in small_names]), _pack([total[n] for n in small_names])[None])
    res = [_unpack(r, small_shapes2) for r in res]
    for j, n in enumerate(small_names):
        out[n] = [res[k][j] for k in range(4)]

    return (loss, grad_x[None], *[out[n][0] for n in WEIGHTS], *[out[n][1] for n in WEIGHTS],
            *[out[n][2] for n in WEIGHTS], *[out[n][3] for n in WEIGHTS])
```

```python
import functools
import math

import jax
import jax.numpy as jnp
from jax import lax
from jax.experimental import pallas as pl
from jax.experimental.pallas import tpu as pltpu

F32, BF16, I32 = jnp.float32, jnp.bfloat16, jnp.int32
MESH = pl.DeviceIdType.MESH
ANY = pl.BlockSpec(memory_space=pl.ANY)

LANE = 128
VMEM_LIMIT = 56 * 1024 * 1024
N_DEV = 8

HEAD = 128
CHUNK = 64
LOOKBACK = 8
BAND = (LOOKBACK + 1) * CHUNK
QBLK = 2 * CHUNK
WIN = BAND + CHUNK
PADK = LOOKBACK * CHUNK
REL_CLIP = 256
REL_TABLE = REL_CLIP + CHUNK
REL_PAD = 384
CONV_WIDTH = 4
LRU_C = 8.0
LN_EPS = 1e-5
DEPTH = 2
ALPHA = (2.0 * DEPTH) ** 0.25
NEG = -1e30
SB_DEAD = -104.0
GELU_K = math.sqrt(2.0 / math.pi)
GELU_C = 0.044715

ADAM_LR, ADAM_B1, ADAM_B2, ADAM_EPS, ADAM_WD, ADAM_STEP = 0.001, 0.9, 0.999, 1e-08, 0.01, 10

NN = (((1,), (0,)), ((), ()))
NT = (((1,), (1,)), ((), ()))
TN = (((0,), (0,)), ((), ()))

FQ, FK, FV, RX, RY, SQ, SK, SV, CQ, CK, CV = range(11)

NAMES = ['x', 'ln_in_g', 'ln_in_b', 'w_in', 'b_forget', 'conv_w', 'conv_b', 'w_r', 'b_r', 'w_i', 'b_i', 'lru_lambda',
         'rel_bias', 'w_branch', 'w_gate', 'b_gate', 'w_out', 'ln1_g', 'ln1_b', 'w_ff1', 'w_ff2', 'ln2_g', 'ln2_b']
WEIGHTS = NAMES[1:]


def _cp(sem=None):
    return pltpu.CompilerParams(dimension_semantics=sem, vmem_limit_bytes=VMEM_LIMIT)


def _iota(shape, dim):
    return lax.broadcasted_iota(I32, shape, dim)


def _sigmoid(x):
    return 1.0 / (1.0 + jnp.exp(-x))


def _log_sigmoid(x):
    return jnp.minimum(x, 0.0) - jnp.log(1.0 + jnp.exp(-jnp.abs(x)))


def _lane_tile(n, cap=1536):
    best = max(t for t in range(LANE, cap + 1, LANE) if n % t == 0)
    return n if best == LANE else best


def _pow2_rows(rows, cols, elems=262144):
    t = 8
    while t * 2 <= rows and t * 2 * cols <= elems and rows % (t * 2) == 0:
        t *= 2
    return t


def _position():
    return lax.axis_index("x"), lax.axis_index("y"), lax.axis_index("c")


class _Side:
    def __init__(self, operands, out_shape, sems, start, finish, aliases=(), parts=()):
        self.operands, self.out_shape, self.sems = list(operands), list(out_shape), list(sems)
        self.start, self.finish, self.aliases, self.parts = start, finish, list(aliases), parts
        self.results = None

    def set_results(self, res):
        self.results = list(res)
        off = 0
        for part in self.parts:
            part.set_results(res[off:off + len(part.out_shape)])
            off += len(part.out_shape)


def _merge_sides(a, b):
    ai, ao, asm = len(a.operands), len(a.out_shape), len(a.sems)

    def start(ins, outs, sems):
        a.start(ins[:ai], outs[:ao], sems[:asm])
        b.start(ins[ai:], outs[ao:], sems[asm:])

    def finish(ins, outs, sems):
        a.finish(ins[:ai], outs[:ao], sems[:asm])
        b.finish(ins[ai:], outs[ao:], sems[asm:])

    return _Side(a.operands + b.operands, a.out_shape + b.out_shape, a.sems + b.sems, start, finish,
                 a.aliases + [(i + ai, o + ao) for i, o in b.aliases], parts=(a, b))


def _chips_side(xs, gather):
    n = len(xs)

    def copies(ins, outs, sems, arrivals):
        send_sems, recv_sems, local_sems = sems
        x, y, c = _position()
        q = 2 * x + y
        chips = [(1 - x, y), (x, 1 - y), (1 - x, 1 - y)]

        def src(t, slot):
            return ins[t] if gather else ins[t].at[slot]

        def dst(t, slot):
            return outs[t].at[c, slot] if gather else outs[t].at[slot]

        def remote(t, j, landing):
            px, py = chips[j]
            return pltpu.make_async_remote_copy(
                src_ref=src(t, 2 * px + py), dst_ref=dst(t, landing), send_sem=send_sems.at[t, j],
                recv_sem=recv_sems.at[t, j], device_id=(px, py, c), device_id_type=MESH)

        local = [pltpu.make_async_copy(src(t, q), dst(t, q), local_sems.at[t]) for t in range(n)]
        sends = [remote(t, j, q) for t in range(n) for j in range(3)]
        if not arrivals:
            return local, sends, []
        return local, sends, [remote(t, j, 2 * px + py) for t in range(n) for j, (px, py) in enumerate(chips)]

    def start(ins, outs, sems):
        local, sends, _ = copies(ins, outs, sems, False)
        for cp in local + sends:
            cp.start()

    def finish(ins, outs, sems):
        local, sends, recvs = copies(ins, outs, sems, True)
        for cp in recvs:
            cp.wait_recv()
        for cp in sends:
            cp.wait_send()
        for cp in local:
            cp.wait()

    out_shape = [jax.ShapeDtypeStruct((2, 4) + a.shape if gather else a.shape, a.dtype) for a in xs]
    sems = [pltpu.SemaphoreType.DMA((n, 3)), pltpu.SemaphoreType.DMA((n, 3)), pltpu.SemaphoreType.DMA((n,))]
    return _Side(xs, out_shape, sems, start, finish)


def _cores_side(xs, gather):
    n = len(xs)
    m = 1 if gather else 4

    def copies(ins, outs, sems, arrivals):
        send_sems, recv_sems = sems
        x, y, c = _position()

        def remote(t, j, landing):
            s = outs[t].at[c] if gather else ins[t].at[j, 1 - c]
            d = outs[t].at[landing] if gather else outs[t].at[j]
            return pltpu.make_async_remote_copy(
                src_ref=s, dst_ref=d, send_sem=send_sems.at[t, j], recv_sem=recv_sems.at[t, j],
                device_id=(x, y, 1 - c), device_id_type=MESH)

        sends = [remote(t, j, c) for t in range(n) for j in range(m)]
        return sends, [remote(t, j, 1 - c) for t in range(n) for j in range(m)] if arrivals else []

    def start(ins, outs, sems):
        for cp in copies(ins, outs, sems, False)[0]:
            cp.start()

    def finish(ins, outs, sems):
        sends, recvs = copies(ins, outs, sems, True)
        for cp in recvs:
            cp.wait_recv()
        for cp in sends:
            cp.wait_send()

    if gather:
        out_shape = [jax.ShapeDtypeStruct(a.shape, a.dtype) for a in xs]
    else:
        out_shape = [jax.ShapeDtypeStruct((4,) + a.shape[2:], a.dtype) for a in xs]
    sems = [pltpu.SemaphoreType.DMA((n, m)), pltpu.SemaphoreType.DMA((n, m))]
    return _Side(xs, out_shape, sems, start, finish, aliases=[(t, t) for t in range(n)] if gather else [])


def _run_side(name, side):
    ni, no = len(side.operands), len(side.out_shape)

    def body(*refs):
        ins, outs, sems = refs[:ni], refs[ni:ni + no], refs[ni + no:]
        side.start(ins, outs, sems)
        side.finish(ins, outs, sems)

    side.set_results(pl.pallas_call(
        body, name=name, out_shape=side.out_shape, in_specs=[ANY] * ni, out_specs=[ANY] * no,
        input_output_aliases=dict(side.aliases), scratch_shapes=side.sems)(*side.operands))
    return side.results


def _pcall(body, operands, *, name, grid, in_specs, out_specs, out_shape, scratch_shapes=(), sem=None, aliases=None,
           side=None):
    if side is None:
        return pl.pallas_call(body, name=name, grid=grid, in_specs=list(in_specs), out_specs=list(out_specs),
                              out_shape=list(out_shape), scratch_shapes=list(scratch_shapes),
                              input_output_aliases=aliases or {}, compiler_params=_cp(sem))(*operands)
    ni, no, ns = len(in_specs), len(out_shape), len(scratch_shapes)
    si, so = len(side.operands), len(side.out_shape)

    def carrying(*refs):
        ins, sins = refs[:ni], refs[ni:ni + si]
        outs, souts = refs[ni + si:ni + si + no], refs[ni + si + no:ni + si + no + so]
        scratch, ssems = refs[ni + si + no + so:ni + si + no + so + ns], refs[ni + si + no + so + ns:]
        ids = [pl.program_id(a) for a in range(len(grid))]
        first = functools.reduce(jnp.logical_and, [i == 0 for i in ids])
        last = functools.reduce(jnp.logical_and, [i == g - 1 for i, g in zip(ids, grid)])

        @pl.when(first)
        def _():
            side.start(sins, souts, ssems)

        body(*ins, *outs, *scratch)

        @pl.when(last)
        def _():
            side.finish(sins, souts, ssems)

    joined = dict(aliases or {})
    joined.update({ni + i: no + o for i, o in side.aliases})
    res = pl.pallas_call(
        carrying, name=name, grid=grid, in_specs=[*in_specs, *[ANY] * si], out_specs=[*out_specs, *[ANY] * so],
        out_shape=[*out_shape, *side.out_shape], scratch_shapes=[*scratch_shapes, *side.sems],
        input_output_aliases=joined, compiler_params=_cp(("arbitrary",) * len(grid)))(*operands, *side.operands)
    side.set_results(res[no:])
    return res[:no]


def _add_core_halves(name, mine, other, c, out_dtype):
    _, _, rows, cols = mine.shape
    tr = _pow2_rows(rows, cols)

    def body(c_ref, a_ref, b_ref, o_ref):
        o_ref[...] = (a_ref[...] + b_ref[...]).astype(out_dtype)

    grid_spec = pltpu.PrefetchScalarGridSpec(
        num_scalar_prefetch=1, grid=(4, rows // tr),
        in_specs=[pl.BlockSpec((None, None, tr, cols), lambda j, i, c_ref: (j, c_ref[0], i, 0)),
                  pl.BlockSpec((None, tr, cols), lambda j, i, c_ref: (j, i, 0))],
        out_specs=pl.BlockSpec((None, tr, cols), lambda j, i, c_ref: (j, i, 0)))
    return pl.pallas_call(body, name=name, grid_spec=grid_spec,
                          out_shape=jax.ShapeDtypeStruct((4, rows, cols), out_dtype),
                          compiler_params=_cp(("parallel", "parallel")))(c, mine, other)


def _sum_parts(name, parts):
    p, rows, cols = parts.shape
    tr = _pow2_rows(rows, cols * p)

    def body(a_ref, o_ref):
        acc = a_ref[0]
        for k in range(1, p):
            acc = acc + a_ref[k]
        o_ref[...] = acc

    return pl.pallas_call(body, name=name, grid=(rows // tr,),
                          in_specs=[pl.BlockSpec((p, tr, cols), lambda i: (0, i, 0))],
                          out_specs=pl.BlockSpec((tr, cols), lambda i: (i, 0)),
                          out_shape=jax.ShapeDtypeStruct((rows, cols), F32),
                          compiler_params=_cp(("parallel",)))(parts)


def _mm(name, a, b, *, grid, a_spec, b_spec, dims, acc_shape, out_shape, out_specs, finish,
        extras=(), extra_specs=(), aliases=None, sem=("parallel", "parallel", "arbitrary"), side=None):
    nk, ne, no = grid[2], len(extras), len(out_shape)

    def body(*refs):
        a_ref, b_ref = refs[0], refs[1]
        ex, outs = refs[2:2 + ne], refs[2 + ne:2 + ne + no]
        ids = (pl.program_id(0), pl.program_id(1))
        def prod():
            return lax.dot_general(a_ref[...], b_ref[...], dims, preferred_element_type=F32)

        if nk == 1:
            finish(prod(), ex, outs, ids)
            return
        acc = refs[2 + ne + no]
        k = pl.program_id(2)

        @pl.when(k == 0)
        def _():
            acc[...] = prod()

        @pl.when(jnp.logical_and(k > 0, k < nk - 1))
        def _():
            acc[...] += prod()

        @pl.when(k == nk - 1)
        def _():
            finish(acc[...] + prod(), ex, outs, ids)

    return _pcall(body, (a, b, *extras), name=name, grid=grid, in_specs=[a_spec, b_spec, *extra_specs],
                  out_specs=out_specs, out_shape=out_shape,
                  scratch_shapes=[pltpu.VMEM(acc_shape, F32)] if nk > 1 else [], sem=sem, aliases=aliases, side=side)


def _store(dtype):
    def finish(acc, ex, outs, ids):
        outs[0][...] = acc.reshape(outs[0].shape).astype(dtype)
    return finish


def _layer_norm_rows(z, g, b):
    mu = jnp.mean(z, axis=1, keepdims=True)
    zc = z - mu
    var = jnp.mean(zc * zc, axis=1, keepdims=True)
    return zc * lax.rsqrt(var + LN_EPS) * g + b


def _mm_ln(name, a, w, l, resid, g, b, side=None):
    s, kdim = a.shape
    d = w.shape[2]
    tm, tk = min(512, s), min(1024, kdim)

    def finish(acc, ex, outs, ids):
        z = acc + ALPHA * ex[0][...]
        y = _layer_norm_rows(z, ex[1][...], ex[2][...])
        outs[0][...] = z
        outs[1][...] = y
        outs[2][...] = y.astype(BF16)

    row = pl.BlockSpec((tm, d), lambda m, n, k: (m, 0))
    vec = pl.BlockSpec((1, d), lambda m, n, k: (0, 0))
    return _mm(name, a, w, grid=(s // tm, 1, kdim // tk),
               a_spec=pl.BlockSpec((tm, tk), lambda m, n, k: (m, k)),
               b_spec=pl.BlockSpec((None, tk, d), lambda m, n, k: (l, k, 0)),
               dims=NN, acc_shape=(tm, d),
               out_shape=[jax.ShapeDtypeStruct((s, d), F32), jax.ShapeDtypeStruct((s, d), F32),
                          jax.ShapeDtypeStruct((s, d), BF16)],
               out_specs=[row, row, row], finish=finish,
               extras=(resid, g.reshape(1, d), b.reshape(1, d)), extra_specs=(row, vec, vec), side=side)


def _ln_fwd(name, x, g, b, side=None):
    s, d = x.shape
    tr = min(256, s)

    def body(x_ref, g_ref, b_ref, y_ref, yb_ref):
        y = _layer_norm_rows(x_ref[...], g_ref[...], b_ref[...])
        y_ref[...] = y
        yb_ref[...] = y.astype(BF16)

    row = pl.BlockSpec((tr, d), lambda i: (i, 0))
    vec = pl.BlockSpec((1, d), lambda i: (0, 0))
    return _pcall(body, (x, g.reshape(1, d), b.reshape(1, d)), name=name, grid=(s // tr,), in_specs=[row, vec, vec],
                  out_specs=[row, row],
                  out_shape=[jax.ShapeDtypeStruct((s, d), F32), jax.ShapeDtypeStruct((s, d), BF16)],
                  sem=("parallel",), side=side)


def _ln_bwd(name, dy, z, g, side=None):
    s, d = z.shape
    tr = min(256, s)

    def body(dy_ref, z_ref, g_ref, dz_ref, dzb_ref, dg_ref, db_ref):
        @pl.when(pl.program_id(0) == 0)
        def _():
            dg_ref[...] = jnp.zeros(dg_ref.shape, F32)
            db_ref[...] = jnp.zeros(db_ref.shape, F32)

        zz, dyv = z_ref[...], dy_ref[...]
        mu = jnp.mean(zz, axis=1, keepdims=True)
        zc = zz - mu
        rstd = lax.rsqrt(jnp.mean(zc * zc, axis=1, keepdims=True) + LN_EPS)
        xhat = zc * rstd
        dg_ref[...] += jnp.sum(dyv * xhat, axis=0, keepdims=True)
        db_ref[...] += jnp.sum(dyv, axis=0, keepdims=True)
        dxh = dyv * g_ref[...]
        dz = rstd * (dxh - jnp.mean(dxh, axis=1, keepdims=True) - xhat * jnp.mean(dxh * xhat, axis=1, keepdims=True))
        dz_ref[...] = dz
        dzb_ref[...] = dz.astype(BF16)

    row = pl.BlockSpec((tr, d), lambda i: (i, 0))
    vec = pl.BlockSpec((1, d), lambda i: (0, 0))
    return _pcall(
        body, (dy, z, g.reshape(1, d)), name=name, grid=(s // tr,), in_specs=[row, row, vec],
        out_specs=[row, row, vec, vec],
        out_shape=[jax.ShapeDtypeStruct((s, d), F32), jax.ShapeDtypeStruct((s, d), BF16),
                   jax.ShapeDtypeStruct((1, d), F32), jax.ShapeDtypeStruct((1, d), F32)],
        sem=("arbitrary",), side=side)


def _loss_head(name, y, target):
    s, d = y.shape
    tr = min(256, s)

    def body(y_ref, t_ref, loss_ref, dy_ref):
        @pl.when(pl.program_id(0) == 0)
        def _():
            loss_ref[...] = jnp.zeros(loss_ref.shape, F32)

        e = y_ref[...] - t_ref[...]
        dy_ref[...] = e * (1.0 / d)
        loss_ref[...] += jnp.sum(e * e) * (0.5 / d)

    row = pl.BlockSpec((tr, d), lambda i: (i, 0))
    return pl.pallas_call(
        body, name=name, grid=(s // tr,), in_specs=[row, row],
        out_specs=[pl.BlockSpec((8, LANE), lambda i: (0, 0)), row],
        out_shape=[jax.ShapeDtypeStruct((8, LANE), F32), jax.ShapeDtypeStruct((s, d), F32)],
        compiler_params=_cp(("arbitrary",)))(y, target)


def _scan_add(x, reverse):
    ts = x.shape[0]
    rows = _iota((ts, 1), 0)
    dist = 1
    while dist < ts:
        if reverse:
            x = x + jnp.where(rows < ts - dist, pltpu.roll(x, ts - dist, 0), 0.0)
        else:
            x = x + jnp.where(rows >= dist, pltpu.roll(x, dist, 0), 0.0)
        dist *= 2
    return x


def _scan_affine(a, b, reverse):
    ts = a.shape[0]
    rows = _iota((ts, 1), 0)
    dist = 1
    while dist < ts:
        shift = ts - dist if reverse else dist
        valid = rows < ts - dist if reverse else rows >= dist
        b = b + a * jnp.where(valid, pltpu.roll(b, shift, 0), 0.0)
        a = a * jnp.where(valid, pltpu.roll(a, shift, 0), 1.0)
        dist *= 2
    return a, b


def _cum_forget_fwd(name, fl, bias):
    s = fl.shape[0]
    ts = min(1024, s)

    def body(f_ref, b_ref, o_ref, carry):
        @pl.when(pl.program_id(0) == 0)
        def _():
            carry[...] = jnp.zeros(carry.shape, F32)

        o_ref[...] = _scan_add(_log_sigmoid(f_ref[...] + b_ref[...]), False) + carry[...]
        carry[...] = o_ref[pl.ds(ts - 1, 1), :]

    row = pl.BlockSpec((ts, LANE), lambda i: (i, 0))
    return pl.pallas_call(body, name=name, grid=(s // ts,),
                          in_specs=[row, pl.BlockSpec((1, LANE), lambda i: (0, 0))], out_specs=row,
                          out_shape=jax.ShapeDtypeStruct((s, LANE), F32),
                          scratch_shapes=[pltpu.VMEM((1, LANE), F32)],
                          compiler_params=_cp(("arbitrary",)))(fl, bias)


def _cum_forget_bwd(name, dcf, fl, bias):
    s = fl.shape[0]
    ts = min(1024, s)
    nb = s // ts

    def body(d_ref, f_ref, b_ref, o_ref, db_ref, carry):
        @pl.when(pl.program_id(0) == 0)
        def _():
            carry[...] = jnp.zeros(carry.shape, F32)
            db_ref[...] = jnp.zeros(db_ref.shape, F32)

        run = _scan_add(d_ref[...], True) + carry[...]
        carry[...] = jnp.sum(jnp.where(_iota((ts, 1), 0) == 0, run, 0.0), axis=0, keepdims=True)
        dfl = run * _sigmoid(-(f_ref[...] + b_ref[...]))
        o_ref[...] = dfl.astype(BF16)
        db_ref[...] += jnp.sum(dfl, axis=0, keepdims=True)

    row = pl.BlockSpec((ts, LANE), lambda i: (nb - 1 - i, 0))
    vec = pl.BlockSpec((1, LANE), lambda i: (0, 0))
    return pl.pallas_call(body, name=name, grid=(nb,), in_specs=[row, row, vec], out_specs=[row, vec],
                          out_shape=[jax.ShapeDtypeStruct((s, LANE), BF16), jax.ShapeDtypeStruct((1, LANE), F32)],
                          scratch_shapes=[pltpu.VMEM((1, LANE), F32)],
                          compiler_params=_cp(("arbitrary",)))(dcf, fl, bias)


def _fox_specs(s, nh, tq, tk):
    q = pl.BlockSpec((tq, HEAD), lambda h, i: (i, FQ * nh + h))
    k = pl.BlockSpec((s, HEAD), lambda h, i: (0, FK * nh + h))
    v = pl.BlockSpec((s, HEAD), lambda h, i: (0, FV * nh + h))
    col = pl.BlockSpec((None, tq, 1), lambda h, i: (h, i, 0))
    rowv = pl.BlockSpec((None, s // tk, 1, tk), lambda h, i: (h, 0, 0, 0))
    tile = pl.BlockSpec((tq, HEAD), lambda h, i: (i, h))
    full = pl.BlockSpec((s, HEAD), lambda h, i: (0, h))
    return q, k, v, col, rowv, tile, full


def _fox_tile(s):
    return min(512, s)


def _fox_scores(q, k_ref, cfq, cfr_ref, kb, tk, scale, diagonal):
    off = pl.multiple_of(kb * tk, tk)
    k = k_ref[pl.ds(off, tk), :]
    sc = lax.dot_general(q, k, NT, preferred_element_type=F32) * scale + cfq - cfr_ref[kb]
    mask = None
    if diagonal:
        mask = _iota((1, tk), 1) <= _iota((tk, 1), 0)
        sc = jnp.where(mask, sc, NEG)
    return sc, mask, k, off


def _fox_fwd(name, u, cf_col, cf_row, bw, side=None):
    s, nh = u.shape[0], bw // HEAD
    tq = tk = _fox_tile(s)
    scale = HEAD ** -0.5

    def body(q_ref, k_ref, v_ref, cfc_ref, cfr_ref, o_ref, lse_ref):
        i = pl.program_id(1)
        q, cfq = q_ref[...], cfc_ref[...]

        def step(kb, carry, diagonal=False):
            m, l, acc = carry
            sc, _, _, off = _fox_scores(q, k_ref, cfq, cfr_ref, kb, tk, scale, diagonal)
            m2 = jnp.maximum(m, jnp.max(sc, axis=1, keepdims=True))
            p = jnp.exp(sc - m2)
            al = jnp.exp(m - m2)
            return (m2, al * l + jnp.sum(p, axis=1, keepdims=True),
                    al * acc + jnp.dot(p.astype(BF16), v_ref[pl.ds(off, tk), :], preferred_element_type=F32))

        init = (jnp.full((tq, 1), NEG, F32), jnp.zeros((tq, 1), F32), jnp.zeros((tq, HEAD), F32))
        m, l, acc = step(i, lax.fori_loop(0, i, step, init), True)
        o_ref[...] = (acc / l).astype(BF16)
        lse_ref[...] = m + jnp.log(l)

    q, k, v, col, rowv, tile, _ = _fox_specs(s, nh, tq, tk)
    return _pcall(body, (u, u, u, cf_col, cf_row), name=name, grid=(nh, s // tq), in_specs=[q, k, v, col, rowv],
                  out_specs=[tile, col],
                  out_shape=[jax.ShapeDtypeStruct((s, bw), BF16), jax.ShapeDtypeStruct((nh, s, 1), F32)],
                  sem=("parallel", "parallel"), side=side)


def _fox_bwd(name, u, cf_col, cf_row, o, do, lse, bw, side=None):
    s, nh = u.shape[0], bw // HEAD
    tq = tk = _fox_tile(s)
    nq = s // tq
    scale = HEAD ** -0.5

    def body(q_ref, k_ref, v_ref, cfc_ref, cfr_ref, o_ref, do_ref, lse_ref,
             dq_ref, dk_ref, dv_ref, dcc_ref, dcr_ref, dk_s, dv_s):
        i = pl.program_id(1)

        @pl.when(i == 0)
        def _():
            dk_s[...] = jnp.zeros(dk_s.shape, F32)
            dv_s[...] = jnp.zeros(dv_s.shape, F32)
            dcr_ref[...] = jnp.zeros(dcr_ref.shape, F32)

        q, dov, cfq, lse_q = q_ref[...], do_ref[...], cfc_ref[...], lse_ref[...]
        delta = jnp.sum(dov.astype(F32) * o_ref[...].astype(F32), axis=1, keepdims=True)

        def step(kb, carry, diagonal=False):
            dq, dcq = carry
            sc, mask, k, off = _fox_scores(q, k_ref, cfq, cfr_ref, kb, tk, scale, diagonal)
            p = jnp.exp(sc - lse_q)
            if diagonal:
                p = jnp.where(mask, p, 0.0)
            dp = lax.dot_general(dov, v_ref[pl.ds(off, tk), :], NT, preferred_element_type=F32)
            ds = p * (dp - delta)
            dsb = ds.astype(BF16)
            dk_s[pl.ds(off, tk), :] += lax.dot_general(dsb, q, TN, preferred_element_type=F32)
            dv_s[pl.ds(off, tk), :] += lax.dot_general(p.astype(BF16), dov, TN, preferred_element_type=F32)
            dcr_ref[kb] += -jnp.sum(ds, axis=0, keepdims=True)
            return (dq + jnp.dot(dsb, k, preferred_element_type=F32), dcq + jnp.sum(ds, axis=1, keepdims=True))

        init = (jnp.zeros((tq, HEAD), F32), jnp.zeros((tq, 1), F32))
        dq, dcq = step(i, lax.fori_loop(0, i, step, init), True)
        dq_ref[...] = (dq * scale).astype(BF16)
        dcc_ref[...] = jnp.transpose(jnp.broadcast_to(dcq, (tq, LANE)))[:8, :]

        @pl.when(i == nq - 1)
        def _():
            dk_ref[...] = (dk_s[...] * scale).astype(BF16)
            dv_ref[...] = dv_s[...].astype(BF16)

    q, k, v, col, rowv, tile, full = _fox_specs(s, nh, tq, tk)
    by_query = pl.BlockSpec((None, None, 8, tq), lambda h, i: (h, i, 0, 0))
    return _pcall(
        body, (u, u, u, cf_col, cf_row, o, do, lse), name=name, grid=(nh, nq),
        in_specs=[q, k, v, col, rowv, tile, tile, col], out_specs=[tile, full, full, by_query, rowv],
        out_shape=[jax.ShapeDtypeStruct((s, bw), BF16)] * 3
        + [jax.ShapeDtypeStruct((nh, nq, 8, tq), F32), jax.ShapeDtypeStruct((nh, s // tk, 1, tk), F32)],
        scratch_shapes=[pltpu.VMEM((s, HEAD), F32), pltpu.VMEM((s, HEAD), F32)],
        sem=("arbitrary", "arbitrary"), side=side)


def _suffix_mm(x, ones_below):
    hi = x.astype(BF16)
    lo = (x - hi.astype(F32)).astype(BF16)
    return (jnp.dot(hi, ones_below, preferred_element_type=F32) + jnp.dot(lo, ones_below, preferred_element_type=F32))


def _sb_tile(q, k_ref, kb, tk, qpos, scale):
    off = pl.multiple_of(kb * tk, tk)
    k = k_ref[pl.ds(off, tk), :]
    z = lax.dot_general(q, k, NT, preferred_element_type=F32) * scale
    mask = kb * tk + _iota((1, tk), 1) < qpos
    lsn = -jnp.maximum(z, 0.0) - jnp.log(1.0 + jnp.exp(-jnp.abs(z)))
    return z, mask, lsn, jnp.where(mask, lsn, 0.0), k, off


def _sb_specs(s, nh, tq):
    q = pl.BlockSpec((tq, HEAD), lambda h, i: (i, SQ * nh + h))
    k = pl.BlockSpec((s, HEAD), lambda h, i: (0, SK * nh + h))
    v = pl.BlockSpec((s, HEAD), lambda h, i: (0, SV * nh + h))
    tile = pl.BlockSpec((tq, HEAD), lambda h, i: (i, h))
    full = pl.BlockSpec((s, HEAD), lambda h, i: (0, h))
    return q, k, v, tile, full


def _sb_fwd(name, u, bw, side=None):
    s, nh = u.shape[0], bw // HEAD
    tq = tk = 256
    scale = HEAD ** -0.5

    def body(q_ref, k_ref, v_ref, o_ref):
        i = pl.program_id(1)
        q = q_ref[...]
        qpos = i * tq + _iota((tq, 1), 0)
        later_keys = (_iota((tk, tk), 0) > _iota((tk, tk), 1)).astype(BF16)
        nk = (i * tq + tq + tk - 2) // tk

        def cond(st):
            return jnp.logical_and(st[0] < nk, st[3] > SB_DEAD)

        def step(st):
            j, c, acc, _ = st
            z, mask, lsn, lm, _, off = _sb_tile(q, k_ref, nk - 1 - j, tk, qpos, scale)
            a = jnp.where(mask, jnp.exp(lsn + z + c + _suffix_mm(lm, later_keys)), 0.0)
            acc = acc + jnp.dot(a.astype(BF16), v_ref[pl.ds(off, tk), :], preferred_element_type=F32)
            c = c + jnp.sum(lm, axis=1, keepdims=True)
            return j + 1, c, acc, jnp.max(c)

        init = (jnp.int32(0), jnp.zeros((tq, 1), F32), jnp.zeros((tq, HEAD), F32), jnp.float32(0.0))
        o_ref[...] = lax.while_loop(cond, step, init)[2].astype(BF16)

    q, k, v, tile, _ = _sb_specs(s, nh, tq)
    return _pcall(body, (u, u, u), name=name, grid=(nh, s // tq), in_specs=[q, k, v], out_specs=[tile],
                  out_shape=[jax.ShapeDtypeStruct((s, bw), BF16)], sem=("parallel", "parallel"), side=side)[0]


def _sb_bwd(name, u, do, bw):
    s, nh = u.shape[0], bw // HEAD
    tq = tk = 256
    nq = s // tq
    scale = HEAD ** -0.5

    def body(q_ref, k_ref, v_ref, do_ref, dq_ref, dk_ref, dv_ref, dk_s, dv_s):
        i = pl.program_id(1)

        @pl.when(i == 0)
        def _():
            dk_s[...] = jnp.zeros(dk_s.shape, F32)
            dv_s[...] = jnp.zeros(dv_s.shape, F32)

        q, dov = q_ref[...], do_ref[...]
        qpos = i * tq + _iota((tq, 1), 0)
        later_keys = (_iota((tk, tk), 0) > _iota((tk, tk), 1)).astype(BF16)
        this_and_later = (_iota((tk, tk), 0) >= _iota((tk, tk), 1)).astype(BF16)
        nk = (i * tq + tq + tk - 2) // tk

        def weights(j, c):
            z, mask, lsn, lm, k, off = _sb_tile(q, k_ref, nk - 1 - j, tk, qpos, scale)
            a = jnp.where(mask, jnp.exp(lsn + z + c + _suffix_mm(lm, later_keys)), 0.0)
            w = a * lax.dot_general(dov, v_ref[pl.ds(off, tk), :], NT, preferred_element_type=F32)
            return z, mask, lsn, lm, k, off, a, w

        def cond(st):
            return jnp.logical_and(st[0] < nk, st[3] > SB_DEAD)

        def step1(st):
            j, c, wc, _ = st
            _, _, _, lm, _, _, _, w = weights(j, c)
            c = c + jnp.sum(lm, axis=1, keepdims=True)
            return j + 1, c, wc + jnp.sum(w, axis=1, keepdims=True), jnp.max(c)

        zero = jnp.zeros((tq, 1), F32)
        live, _, total, _ = lax.while_loop(cond, step1, (jnp.int32(0), zero, zero, jnp.float32(0.0)))

        def step2(j, st):
            c, wc, dq = st
            z, mask, lsn, lm, k, off, a, w = weights(j, c)
            earlier = total - (wc + _suffix_mm(w, this_and_later))
            dz = jnp.where(mask, w * jnp.exp(lsn) - jnp.exp(lsn + z) * earlier, 0.0)
            dzb = dz.astype(BF16)
            dk_s[pl.ds(off, tk), :] += lax.dot_general(dzb, q, TN, preferred_element_type=F32)
            dv_s[pl.ds(off, tk), :] += lax.dot_general(a.astype(BF16), dov, TN, preferred_element_type=F32)
            return (c + jnp.sum(lm, axis=1, keepdims=True), wc + jnp.sum(w, axis=1, keepdims=True),
                    dq + jnp.dot(dzb, k, preferred_element_type=F32))

        dq = lax.fori_loop(0, live, step2, (zero, zero, jnp.zeros((tq, HEAD), F32)))[2]
        dq_ref[...] = (dq * scale).astype(BF16)

        @pl.when(i == nq - 1)
        def _():
            dk_ref[...] = (dk_s[...] * scale).astype(BF16)
            dv_ref[...] = dv_s[...].astype(BF16)

    q, k, v, tile, full = _sb_specs(s, nh, tq)
    return pl.pallas_call(
        body, name=name, grid=(nh, nq), in_specs=[q, k, v, tile], out_specs=[tile, full, full],
        out_shape=[jax.ShapeDtypeStruct((s, bw), BF16)] * 3,
        scratch_shapes=[pltpu.VMEM((s, HEAD), F32), pltpu.VMEM((s, HEAD), F32)],
        compiler_params=_cp(("arbitrary", "arbitrary")))(u, u, u, do)


BIAS_W = 768


def _strip_onehot():
    col = _iota((1, BIAS_W), 1)
    ridx = jnp.clip(PADK + (QBLK - 1) - col, -(CHUNK - 1), REL_CLIP) + (CHUNK - 1)
    return (_iota((REL_PAD, BIAS_W), 0) == ridx).astype(BF16)


def _split2(x):
    hi = x.astype(BF16)
    return hi, (x - hi.astype(F32)).astype(BF16)


def _bias_expand(name, table, nh):
    def body(t_ref, o_ref, strip):
        table_f32 = t_ref[...]
        hi = table_f32.astype(BF16)
        mid, lo = _split2(table_f32 - hi.astype(F32))
        onehot = _strip_onehot()
        strip[...] = (jnp.dot(hi, onehot, preferred_element_type=F32) + jnp.dot(mid, onehot, preferred_element_type=F32)
                      + jnp.dot(lo, onehot, preferred_element_type=F32))
        row, kl = _iota((QBLK, 1), 0), _iota((1, WIN), 1)
        first = row - jnp.bitwise_and(row, CHUNK - 1)
        valid = jnp.logical_and(kl >= first, kl < first + BAND)
        for h in range(nh):
            rows = jnp.broadcast_to(strip[pl.ds(h, 1), :], (QBLK, BIAS_W))
            rolled = pltpu.roll(rows, BIAS_W - (QBLK - 1), 1, stride=1, stride_axis=0)
            o_ref[h] = jnp.where(valid, rolled[:, :WIN], NEG)

    return pl.pallas_call(body, name=name, out_shape=jax.ShapeDtypeStruct((nh, QBLK, WIN), F32),
                          in_specs=[pl.BlockSpec(memory_space=pltpu.VMEM)],
                          out_specs=pl.BlockSpec(memory_space=pltpu.VMEM),
                          scratch_shapes=[pltpu.VMEM((16, BIAS_W), F32)], compiler_params=_cp())(table)


def _bias_reduce(name, dss, nh):
    def body(x_ref, o_ref):
        onehot = _strip_onehot()
        flip = (_iota((QBLK, QBLK), 0) + _iota((QBLK, QBLK), 1) == QBLK - 1).astype(BF16)
        for h in range(nh):
            x = jnp.concatenate([x_ref[h], jnp.zeros((QBLK, BIAS_W - WIN), F32)], axis=1)
            hi, lo = _split2(x)
            back = jnp.dot(flip, hi, preferred_element_type=F32) + jnp.dot(flip, lo, preferred_element_type=F32)
            lined = pltpu.roll(back, 0, 1, stride=1, stride_axis=0)
            hi, lo = _split2(jnp.broadcast_to(jnp.sum(lined, axis=0, keepdims=True), (8, BIAS_W)))
            o_ref[h] = (lax.dot_general(hi, onehot, NT, preferred_element_type=F32)
                        + lax.dot_general(lo, onehot, NT, preferred_element_type=F32))

    return pl.pallas_call(body, name=name, out_shape=jax.ShapeDtypeStruct((nh, 8, REL_PAD), F32),
                          in_specs=[pl.BlockSpec(memory_space=pltpu.VMEM)],
                          out_specs=pl.BlockSpec(memory_space=pltpu.VMEM), compiler_params=_cp())(dss)


def _chunk_specs(s, nh):
    q = pl.BlockSpec((QBLK, HEAD), lambda h, i: (i, CQ * nh + h))
    kv = pl.BlockSpec((s + PADK, HEAD), lambda h, i: (0, h))
    bias = pl.BlockSpec((None, QBLK, WIN), lambda h, i: (h, 0, 0))
    tile = pl.BlockSpec((QBLK, HEAD), lambda h, i: (i, h))
    full = pl.BlockSpec((s, HEAD), lambda h, i: (0, h))
    return q, kv, bias, tile, full


def _chunk_probs(q, k_ref, b_ref, i, scale):
    off = pl.multiple_of(i * QBLK, QBLK)
    kw = k_ref[pl.ds(off, WIN), :]
    sc = lax.dot_general(q, kw, NT, preferred_element_type=F32) * scale + b_ref[...]
    sc = jnp.where(i * QBLK + _iota((1, WIN), 1) >= PADK, sc, NEG)
    p = jnp.exp(sc - jnp.max(sc, axis=1, keepdims=True))
    return p, jnp.sum(p, axis=1, keepdims=True), kw, off


def _chunk_fwd(name, u, kpad, vpad, bias, bw, side=None):
    s, nh = u.shape[0], bw // HEAD
    scale = HEAD ** -0.5

    def body(q_ref, k_ref, v_ref, b_ref, o_ref):
        p, l, _, off = _chunk_probs(q_ref[...], k_ref, b_ref, pl.program_id(1), scale)
        o = jnp.dot(p.astype(BF16), v_ref[pl.ds(off, WIN), :], preferred_element_type=F32)
        o_ref[...] = (o / l).astype(BF16)

    q, kv, bs, tile, _ = _chunk_specs(s, nh)
    return _pcall(body, (u, kpad, vpad, bias), name=name, grid=(nh, s // QBLK), in_specs=[q, kv, kv, bs],
                  out_specs=[tile], out_shape=[jax.ShapeDtypeStruct((s, bw), BF16)], sem=("parallel", "parallel"),
                  side=side)[0]


def _chunk_bwd(name, u, kpad, vpad, bias, do, bw):
    s, nh = u.shape[0], bw // HEAD
    nq = s // QBLK
    scale = HEAD ** -0.5

    def body(q_ref, k_ref, v_ref, b_ref, do_ref, dq_ref, dk_ref, dv_ref, dss_ref, dk_s, dv_s):
        i = pl.program_id(1)

        @pl.when(i == 0)
        def _():
            dk_s[...] = jnp.zeros(dk_s.shape, F32)
            dv_s[...] = jnp.zeros(dv_s.shape, F32)
            dss_ref[...] = jnp.zeros(dss_ref.shape, F32)

        q, dov = q_ref[...], do_ref[...]
        p, l, kw, off = _chunk_probs(q, k_ref, b_ref, i, scale)
        p = p / l
        dp = lax.dot_general(dov, v_ref[pl.ds(off, WIN), :], NT, preferred_element_type=F32)
        ds = p * (dp - jnp.sum(p * dp, axis=1, keepdims=True))
        dsb = ds.astype(BF16)
        dq_ref[...] = (jnp.dot(dsb, kw, preferred_element_type=F32) * scale).astype(BF16)
        dk_s[pl.ds(off, WIN), :] += lax.dot_general(dsb, q, TN, preferred_element_type=F32)
        dv_s[pl.ds(off, WIN), :] += lax.dot_general(p.astype(BF16), dov, TN, preferred_element_type=F32)
        dss_ref[...] += ds

        @pl.when(i == nq - 1)
        def _():
            dk_ref[...] = (dk_s[pl.ds(PADK, s), :] * scale).astype(BF16)
            dv_ref[...] = dv_s[pl.ds(PADK, s), :].astype(BF16)

    q, kv, bs, tile, full = _chunk_specs(s, nh)
    return pl.pallas_call(
        body, name=name, grid=(nh, nq), in_specs=[q, kv, kv, bs, tile], out_specs=[tile, full, full, bs],
        out_shape=[jax.ShapeDtypeStruct((s, bw), BF16)] * 3 + [jax.ShapeDtypeStruct((nh, QBLK, WIN), F32)],
        scratch_shapes=[pltpu.VMEM((s + PADK, HEAD), F32), pltpu.VMEM((s + PADK, HEAD), F32)],
        compiler_params=_cp(("arbitrary", "arbitrary")))(u, kpad, vpad, bias, do)


def _gelu_parts(y):
    th = jnp.tanh(GELU_K * (y + GELU_C * y * y * y))
    return 0.5 * y * (1.0 + th), th


def _block_diag(xb16, w_ref, nh, dims):
    return jnp.concatenate(
        [lax.dot_general(xb16[:, n * HEAD:(n + 1) * HEAD], w_ref[n], dims, preferred_element_type=F32)
         for n in range(nh)], axis=1)


def _lru_gates(ext, cw_ref, cb_ref, wr_ref, br_ref, wi_ref, bi_ref, lam_ref, ts, nh):
    shifted = [pltpu.roll(ext, CONV_WIDTH - 1 - j, 0)[8:, :] if j < CONV_WIDTH - 1 else ext[8:, :]
               for j in range(CONV_WIDTH)]
    xc = cb_ref[...]
    for j in range(CONV_WIDTH):
        xc = xc + shifted[j] * cw_ref[pl.ds(j, 1), :]
    xcb = xc.astype(BF16)
    r = _sigmoid(_block_diag(xcb, wr_ref, nh, NN) + br_ref[...])
    gi = _sigmoid(_block_diag(xcb, wi_ref, nh, NN) + bi_ref[...])
    lsl = _log_sigmoid(lam_ref[...])
    la = LRU_C * r * lsl
    a = jnp.exp(la)
    e2 = jnp.exp(2.0 * la)
    mult = jnp.sqrt(-jnp.tanh(la) * (e2 + 1.0))
    return shifted, xc, xcb, r, gi, lsl, a, e2, mult


def _lru_param_specs(bw, nh):
    vec = pl.BlockSpec((1, bw), lambda i: (0, 0))
    conv = pl.BlockSpec((8, bw), lambda i: (0, 0))
    blocks = pl.BlockSpec((nh, HEAD, HEAD), lambda i: (0, 0, 0))
    return [conv, vec, blocks, vec, blocks, vec, vec]


def _lru_fwd(name, u, params, bw):
    s, nh = u.shape[0], bw // HEAD
    ts = min(512, s)

    def body(rx_ref, ry_ref, cw_ref, cb_ref, wr_ref, br_ref, wi_ref, bi_ref, lam_ref, o_ref, h_ref, tail, hcar):
        @pl.when(pl.program_id(0) == 0)
        def _():
            tail[...] = jnp.zeros(tail.shape, F32)
            hcar[...] = jnp.zeros(hcar.shape, F32)

        rx = rx_ref[...].astype(F32)
        ext = jnp.concatenate([tail[...], rx], axis=0)
        tail[...] = rx[ts - 8:, :]
        _, xc, _, _, gi, _, a, _, mult = _lru_gates(ext, cw_ref, cb_ref, wr_ref, br_ref, wi_ref, bi_ref, lam_ref, ts, nh)
        acum, bcum = _scan_affine(a, mult * (gi * xc), False)
        h_ref[...] = bcum + acum * hcar[...]
        hcar[...] = h_ref[pl.ds(ts - 1, 1), :]
        o_ref[...] = (h_ref[...] * _gelu_parts(ry_ref[...].astype(F32))[0]).astype(BF16)

    row = pl.BlockSpec((ts, bw), lambda i: (i, 0))
    return pl.pallas_call(
        body, name=name, grid=(s // ts,),
        in_specs=[pl.BlockSpec((ts, bw), lambda i: (i, RX)), pl.BlockSpec((ts, bw), lambda i: (i, RY))]
        + _lru_param_specs(bw, nh),
        out_specs=[row, row],
        out_shape=[jax.ShapeDtypeStruct((s, bw), BF16), jax.ShapeDtypeStruct((s, bw), F32)],
        scratch_shapes=[pltpu.VMEM((8, bw), F32), pltpu.VMEM((1, bw), F32)],
        compiler_params=_cp(("arbitrary",)))(u, u, *params)


def _lru_bwd(name, u, h, do, params, bw):
    s, nh = u.shape[0], bw // HEAD
    ts = min(512, s)
    nb = s // ts
    t8 = ts // 8

    def body(rx_ref, rxp_ref, ry_ref, h_ref, hp_ref, do_ref, cw_ref, cb_ref, wr_ref, br_ref, wi_ref, bi_ref, lam_ref,
             drx_ref, dry_ref, dcw_ref, dcb_ref, dwr_ref, dbr_ref, dwi_ref, dbi_ref, dlam_ref, gcar, head):
        i = pl.program_id(0)
        first = i == nb - 1

        @pl.when(i == 0)
        def _():
            gcar[...] = jnp.zeros(gcar.shape, F32)
            head[...] = jnp.zeros(head.shape, F32)
            for ref in (dcw_ref, dcb_ref, dwr_ref, dbr_ref, dwi_ref, dbi_ref, dlam_ref):
                ref[...] = jnp.zeros(ref.shape, F32)

        rows = _iota((ts, 1), 0)
        rx = rx_ref[...].astype(F32)
        before = jnp.where(first, 0.0, rxp_ref[...].astype(F32))
        ext = jnp.concatenate([before, rx], axis=0)
        shifted, xc, xcb, r, gi, lsl, a, e2, mult = _lru_gates(
            ext, cw_ref, cb_ref, wr_ref, br_ref, wi_ref, bi_ref, lam_ref, ts, nh)

        ry = ry_ref[...].astype(F32)
        gel, th = _gelu_parts(ry)
        dgel = 0.5 * (1.0 + th) + 0.5 * ry * (1.0 - th * th) * GELU_K * (1.0 + 3.0 * GELU_C * ry * ry)
        dov = do_ref[...].astype(F32)
        hv = h_ref[...]
        dry_ref[...] = (dov * hv * dgel).astype(BF16)

        coef = jnp.where(rows < ts - 1, pltpu.roll(a, ts - 1, 0), 0.0)
        dh_in = dov * gel + jnp.where(rows == ts - 1, gcar[...], 0.0)
        dh = _scan_affine(coef, dh_in, True)[1]
        gcar[...] = jnp.sum(jnp.where(rows == 0, a * dh, 0.0), axis=0, keepdims=True)

        hprev = jnp.where(first, 0.0, hp_ref[...])
        hm1 = pltpu.roll(jnp.concatenate([hprev, hv], axis=0), 1, 0)[8:, :]
        dgx = dh * mult
        dla = dh * hm1 * a - dh * gi * xc * (e2 / mult)
        dpre_r = dla * (LRU_C * lsl) * r * (1.0 - r)
        dpre_i = dgx * xc * gi * (1.0 - gi)
        dlam_ref[...] += jnp.sum(dla * r, axis=0, keepdims=True) * (LRU_C * _sigmoid(-lam_ref[...]))
        dbr_ref[...] += jnp.sum(dpre_r, axis=0, keepdims=True)
        dbi_ref[...] += jnp.sum(dpre_i, axis=0, keepdims=True)
        drb, dib = dpre_r.astype(BF16), dpre_i.astype(BF16)
        for n in range(nh):
            cols = slice(n * HEAD, (n + 1) * HEAD)
            dwr_ref[n] += lax.dot_general(xcb[:, cols], drb[:, cols], TN, preferred_element_type=F32)
            dwi_ref[n] += lax.dot_general(xcb[:, cols], dib[:, cols], TN, preferred_element_type=F32)
        dxc = dgx * gi + _block_diag(drb, wr_ref, nh, NT) + _block_diag(dib, wi_ref, nh, NT)

        dcb_ref[...] += jnp.sum(dxc, axis=0, keepdims=True)
        for j in range(CONV_WIDTH):
            dcw_ref[pl.ds(j, 1), :] += jnp.sum(dxc * shifted[j], axis=0, keepdims=True)
        ext2 = jnp.concatenate([dxc, head[...]], axis=0)
        head[...] = dxc[:8, :]
        drx = dxc * cw_ref[pl.ds(CONV_WIDTH - 1, 1), :]
        for j in range(CONV_WIDTH - 1):
            up = CONV_WIDTH - 1 - j
            drx = drx + pltpu.roll(ext2, ts + 8 - up, 0)[:ts, :] * cw_ref[pl.ds(j, 1), :]
        drx_ref[...] = drx.astype(BF16)

    def blk(col):
        return lambda i: (nb - 1 - i, col)

    def prev8(col):
        return lambda i: (jnp.maximum((nb - 1 - i) * t8 - 1, 0), col)

    vec = pl.BlockSpec((1, bw), lambda i: (0, 0))
    conv = pl.BlockSpec((8, bw), lambda i: (0, 0))
    blocks = pl.BlockSpec((nh, HEAD, HEAD), lambda i: (0, 0, 0))
    return pl.pallas_call(
        body, name=name, grid=(nb,),
        in_specs=[pl.BlockSpec((ts, bw), blk(RX)), pl.BlockSpec((8, bw), prev8(RX)), pl.BlockSpec((ts, bw), blk(RY)),
                  pl.BlockSpec((ts, bw), blk(0)), pl.BlockSpec((8, bw), prev8(0)), pl.BlockSpec((ts, bw), blk(0))]
        + _lru_param_specs(bw, nh),
        out_specs=[pl.BlockSpec((ts, bw), blk(0)), pl.BlockSpec((ts, bw), blk(0)), conv, vec, blocks, vec, blocks, vec, vec],
        out_shape=[jax.ShapeDtypeStruct((s, bw), BF16)] * 2
        + [jax.ShapeDtypeStruct((8, bw), F32), jax.ShapeDtypeStruct((1, bw), F32),
           jax.ShapeDtypeStruct((nh, HEAD, HEAD), F32), jax.ShapeDtypeStruct((1, bw), F32),
           jax.ShapeDtypeStruct((nh, HEAD, HEAD), F32), jax.ShapeDtypeStruct((1, bw), F32),
           jax.ShapeDtypeStruct((1, bw), F32)],
        scratch_shapes=[pltpu.VMEM((1, bw), F32), pltpu.VMEM((8, bw), F32)],
        compiler_params=_cp(("arbitrary",)))(u, u, u, h, h, do, *params)


def _gate_merge(name, xb, w_gate, b_gate, o_all, w_branch, l, side=None):
    s, d = xb.shape
    bw = o_all.shape[2]
    tm, tn = min(512, s), min(256, d)

    def body(x_ref, wg_ref, bg_ref, o_ref, wb_ref, m_ref, g_ref, p_ref):
        x = x_ref[...]
        acc = jnp.zeros((tm, tn), F32)
        for g in range(4):
            gate = _sigmoid(jnp.dot(x, wg_ref[g], preferred_element_type=F32) + bg_ref[g])
            proj = jnp.dot(o_ref[g], wb_ref[g], preferred_element_type=F32)
            term = gate * proj
            g_ref[g] = gate.astype(BF16)
            p_ref[g] = (term * (1.0 - gate)).astype(BF16)
            acc = acc + term
        m_ref[...] = acc.astype(BF16)

    quad = pl.BlockSpec((4, tm, tn), lambda n, m: (0, m, n))
    return _pcall(
        body, (xb, w_gate, b_gate, o_all, w_branch), name=name, grid=(d // tn, s // tm),
        in_specs=[pl.BlockSpec((tm, d), lambda n, m: (m, 0)),
                  pl.BlockSpec((None, 4, d, tn), lambda n, m: (0, 0, 0, n)),
                  pl.BlockSpec((None, 4, 1, tn), lambda n, m: (l, 0, 0, n)),
                  pl.BlockSpec((4, tm, bw), lambda n, m: (0, m, 0)),
                  pl.BlockSpec((None, 4, bw, tn), lambda n, m: (0, 0, 0, n))],
        out_specs=[pl.BlockSpec((tm, tn), lambda n, m: (m, n)), quad, quad],
        out_shape=[jax.ShapeDtypeStruct((s, d), BF16), jax.ShapeDtypeStruct((4, s, d), BF16),
                   jax.ShapeDtypeStruct((4, s, d), BF16)],
        sem=("parallel", "parallel"), side=side)


def _adamw(name, w, m, v, parts, layer=0, layers=1, earlier=None):
    cols = w.shape[1]
    p, rows = parts.shape[0], parts.shape[1]
    tr = _pow2_rows(rows, cols * max(1, p // 2))
    nb = rows // tr
    c1 = 1.0 - ADAM_B1 ** ADAM_STEP
    c2 = 1.0 - ADAM_B2 ** ADAM_STEP

    def body(w_ref, m_ref, v_ref, g_ref, *rest):
        go_ref, do_ref, mo_ref, vo_ref = rest[-4:]
        g = g_ref[0].astype(F32)
        for k in range(1, p):
            g = g + g_ref[k].astype(F32)
        m2 = ADAM_B1 * m_ref[...] + (1.0 - ADAM_B1) * g
        v2 = ADAM_B2 * v_ref[...] + (1.0 - ADAM_B2) * (g * g)
        go_ref[...] = g
        do_ref[...] = -ADAM_LR * ((m2 / c1) / (jnp.sqrt(v2 / c2) + ADAM_EPS) + ADAM_WD * w_ref[...])
        mo_ref[...] = m2
        vo_ref[...] = v2

    row = pl.BlockSpec((tr, cols), lambda i: (layer * nb + i, 0))
    held = list(earlier) if earlier is not None else []
    return pl.pallas_call(
        body, name=name, grid=(nb,),
        in_specs=[row, row, row, pl.BlockSpec((p, tr, cols), lambda i: (0, i, 0))] + [ANY] * len(held),
        out_specs=[row] * 4, out_shape=[jax.ShapeDtypeStruct((layers * rows, cols), F32)] * 4,
        input_output_aliases={4 + k: k for k in range(len(held))},
        compiler_params=_cp(("parallel",)))(w, m, v, parts, *held)


PACK_ROWS = 512


def _pack(arrays):
    rows = []
    for a in arrays:
        flat = a.astype(F32).reshape(-1)
        rows.append(jnp.pad(flat, (0, (-flat.shape[0]) % LANE)).reshape(-1, LANE))
    rows = jnp.concatenate(rows)
    return jnp.pad(rows, ((0, (-rows.shape[0]) % PACK_ROWS), (0, 0)))


def _unpack(packed, shapes):
    out, row = [], 0
    for shp in shapes:
        n = math.prod(shp)
        nrows = -(-n // LANE)
        out.append(packed[row:row + nrows].reshape(-1)[:n].reshape(shp))
        row += nrows
    return out


def _unshard(gathered, axis):
    block = gathered.shape[2:]
    full = jnp.swapaxes(gathered, 0, 1).reshape((N_DEV,) + block)
    full = jnp.moveaxis(full, 0, axis)
    return full.reshape(block[:axis] + (N_DEV * block[axis],) + block[axis + 1:])


def kernel(x, ln_in_g, ln_in_b, w_in, b_forget, conv_w, conv_b, w_r, b_r, w_i, b_i, lru_lambda, rel_bias, w_branch, w_gate, b_gate, w_out, ln1_g, ln1_b, w_ff1, w_ff2, ln2_g, ln2_b, loss_target, m_ln_in_g, m_ln_in_b, m_w_in, m_b_forget, m_conv_w, m_conv_b, m_w_r, m_b_r, m_w_i, m_b_i, m_lru_lambda, m_rel_bias, m_w_branch, m_w_gate, m_b_gate, m_w_out, m_ln1_g, m_ln1_b, m_w_ff1, m_w_ff2, m_ln2_g, m_ln2_b, v_ln_in_g, v_ln_in_b, v_w_in, v_b_forget, v_conv_w, v_conv_b, v_w_r, v_b_r, v_w_i, v_b_i, v_lru_lambda, v_rel_bias, v_w_branch, v_w_gate, v_b_gate, v_w_out, v_ln1_g, v_ln1_b, v_w_ff1, v_w_ff2, v_ln2_g, v_ln2_b):
    given = dict(zip(
        NAMES + ['loss_target'] + ['m_' + n for n in WEIGHTS] + ['v_' + n for n in WEIGHTS],
        (x, ln_in_g, ln_in_b, w_in, b_forget, conv_w, conv_b, w_r, b_r, w_i, b_i, lru_lambda, rel_bias, w_branch, w_gate, b_gate, w_out, ln1_g, ln1_b, w_ff1, w_ff2, ln2_g, ln2_b, loss_target, m_ln_in_g, m_ln_in_b, m_w_in, m_b_forget, m_conv_w, m_conv_b, m_w_r, m_b_r, m_w_i, m_b_i, m_lru_lambda, m_rel_bias, m_w_branch, m_w_gate, m_b_gate, m_w_out, m_ln1_g, m_ln1_b, m_w_ff1, m_w_ff2, m_ln2_g, m_ln2_b, v_ln_in_g, v_ln_in_b, v_w_in, v_b_forget, v_conv_w, v_conv_b, v_w_r, v_b_r, v_w_i, v_b_i, v_lru_lambda, v_rel_bias, v_w_branch, v_w_gate, v_b_gate, v_w_out, v_ln1_g, v_ln1_b, v_w_ff1, v_w_ff2, v_ln2_g, v_ln2_b)))

    s, d = x.shape[1], x.shape[2]
    nl = w_in.shape[0]
    bw = d // 4
    nh = bw // HEAD
    nu = 11 * bw
    rs = d // N_DEV
    dff = w_ff1.shape[2] * N_DEV
    fs = dff // N_DEV
    cs = d // N_DEV
    assert nl == DEPTH and nh * HEAD == bw and s % 256 == 0 and d % 1024 == 0

    xi, yi, ci = _position()
    dev = 4 * xi + 2 * yi + ci
    c_arr = jnp.reshape(ci, (1,)).astype(I32)

    w_main = jnp.concatenate(
        [w_in[..., :3 * bw], w_in[..., 3 * bw + nh:],
         jnp.pad(w_in[..., 3 * bw:3 * bw + nh], ((0, 0), (0, 0), (0, LANE - nh)))], axis=-1).astype(BF16)
    nue = nu + LANE
    small_shapes = [conv_w.shape, rel_bias.shape, b_gate.shape]
    shard = {'main': w_main, 'branch': w_branch.astype(BF16), 'gate': w_gate.astype(BF16),
             'out': w_out.astype(BF16), 'ff1': w_ff1.astype(BF16), 'ff2': w_ff2.astype(BF16)}
    shard_axis = {'main': 1, 'branch': 3, 'gate': 2, 'out': 1, 'ff1': 2, 'ff2': 1}
    W = [dict() for _ in range(nl)]

    def gather_chips(l, keys, extra=()):
        side = _chips_side([shard[k][l:l + 1] for k in keys] + list(extra), True)
        side.todo = (l, keys)
        return side

    def gather_cores(chips):
        side = _cores_side(chips.results, True)
        side.todo = chips.todo
        return side

    def arrived(cores):
        l, keys = cores.todo
        for k, res in zip(keys, cores.results):
            W[l][k] = _unshard(res, shard_axis[k])
        return cores.results[len(keys):]

    xs = x[0]
    first = gather_chips(0, ['main'], [_pack([conv_w, rel_bias, b_gate])])
    h0, h0b = _ln_fwd("ln_in", xs, ln_in_g, ln_in_b, side=first)
    first = gather_cores(first)
    _run_side("gather_cores_first", first)
    small = arrived(first)[0]
    small = jnp.swapaxes(small, 0, 1).reshape((N_DEV,) + small.shape[2:])
    small = [_unpack(small[j], small_shapes) for j in range(N_DEV)]
    conv_w_full = jnp.concatenate([small[j][0] for j in range(N_DEV)], axis=-1)
    rel_bias_full = jnp.concatenate([small[j][1] for j in range(N_DEV)], axis=-1)
    b_gate_full = jnp.concatenate([small[j][2] for j in range(N_DEV)], axis=-1)
    b_gate4 = b_gate_full.reshape(nl, 4, 1, d)

    def lru_params(l):
        return (jnp.pad(conv_w_full[l], ((0, 8 - CONV_WIDTH), (0, 0))), conv_b[l].reshape(1, bw),
                w_r[l].astype(BF16), b_r[l].reshape(1, bw), w_i[l].astype(BF16), b_i[l].reshape(1, bw),
                lru_lambda[l].reshape(1, bw))

    def bias_rows(l):
        return jnp.pad(rel_bias_full[l], ((0, 16 - nh), (0, REL_PAD - REL_TABLE)))

    tm = min(1024, s)
    tkk = min(2048, d)

    saved = []
    cur, curb = h0, h0b
    chips = {}
    for l in range(nl):
        side = None
        if l == 0:
            side = chips['b0'] = gather_chips(0, ['gate', 'branch'])
        else:
            side = last_cores = gather_cores(chips.pop('d1'))
        u = _mm(f"w_in_{l}", curb, W[l]['main'], grid=(s // tm, nu // bw, d // tkk),
                a_spec=pl.BlockSpec((tm, tkk), lambda m, n, k: (m, k)),
                b_spec=pl.BlockSpec((None, tkk, bw), lambda m, n, k: (0, k, n)),
                dims=NN, acc_shape=(tm, bw), out_shape=[jax.ShapeDtypeStruct((s, nu), BF16)],
                out_specs=[pl.BlockSpec((tm, bw), lambda m, n, k: (m, n))], finish=_store(BF16), side=side)[0]
        if l == 1:
            arrived(last_cores)
        fl = _mm(f"w_forget_{l}", curb, W[l]['main'], grid=(s // tm, 1, d // tkk),
                 a_spec=pl.BlockSpec((tm, tkk), lambda m, n, k: (m, k)),
                 b_spec=pl.BlockSpec((None, tkk, LANE), lambda m, n, k: (0, k, nu // LANE)),
                 dims=NN, acc_shape=(tm, LANE), out_shape=[jax.ShapeDtypeStruct((s, LANE), F32)],
                 out_specs=[pl.BlockSpec((tm, LANE), lambda m, n, k: (m, 0))], finish=_store(F32))[0]
        bf_row = jnp.pad(b_forget[l], (0, LANE - nh)).reshape(1, LANE)
        cf = _cum_forget_fwd(f"cum_forget_{l}", fl, bf_row)
        tkf = _fox_tile(s)
        cf_heads = cf[:, :nh].T
        cf_col = cf_heads.reshape(nh, s, 1)
        cf_row = cf_heads.reshape(nh, s // tkf, 1, tkf)
        side = None
        if l == 0:
            b0 = gather_cores(chips.pop('b0'))
            chips['c0'] = gather_chips(0, ['out', 'ff1', 'ff2'])
            side = _merge_sides(b0, chips['c0'])
        o_fox, lse = _fox_fwd(f"fox_fwd_{l}", u, cf_col, cf_row, bw, side=side)
        if l == 0:
            arrived(b0)
        lp = lru_params(l)
        o_lru, hstate = _lru_fwd(f"lru_fwd_{l}", u, lp, bw)
        o_sb = _sb_fwd(f"sb_fwd_{l}", u, bw)
        bias = _bias_expand(f"bias_expand_{l}", bias_rows(l), nh)
        kpad = jnp.pad(u[:, CK * bw:(CK + 1) * bw], ((PADK, 0), (0, 0)))
        vpad = jnp.pad(u[:, CV * bw:(CV + 1) * bw], ((PADK, 0), (0, 0)))
        side = gather_cores(chips.pop('c0')) if l == 0 else None
        o_ch = _chunk_fwd(f"chunk_fwd_{l}", u, kpad, vpad, bias, bw, side=side)
        if l == 0:
            arrived(side)
        o_all = jnp.stack([o_fox, o_lru, o_sb, o_ch])
        side = None
        if l == 0:
            side = chips['a1'] = gather_chips(1, ['main', 'gate', 'branch'])
        merged, gates, projs = _gate_merge(f"gate_merge_{l}", curb, W[l]['gate'], b_gate4, o_all, W[l]['branch'], l,
                                           side=side)
        side = gather_cores(chips.pop('a1')) if l == 0 else None
        z1, x1, x1b = _mm_ln(f"w_out_ln1_{l}", merged, W[l]['out'], 0, cur, ln1_g[l], ln1_b[l], side=side)
        if l == 0:
            arrived(side)
        tn1 = min(1024, dff)

        def ff1_finish(acc, ex, outs, ids):
            outs[0][...] = acc.astype(BF16)
            r = jnp.maximum(acc, 0.0)
            outs[1][...] = (r * r).astype(BF16)

        side = None
        if l == 0:
            side = chips['c1'] = gather_chips(1, ['out', 'ff1'])
        hp, hid = _mm(f"w_ff1_{l}", x1b, W[l]['ff1'], grid=(s // tm, dff // tn1, d // tkk),
                      a_spec=pl.BlockSpec((tm, tkk), lambda m, n, k: (m, k)),
                      b_spec=pl.BlockSpec((None, tkk, tn1), lambda m, n, k: (0, k, n)),
                      dims=NN, acc_shape=(tm, tn1),
                      out_shape=[jax.ShapeDtypeStruct((s, dff), BF16)] * 2,
                      out_specs=[pl.BlockSpec((tm, tn1), lambda m, n, k: (m, n))] * 2, finish=ff1_finish, side=side)
        side = None
        if l == 0:
            c1 = gather_cores(chips.pop('c1'))
            chips['d1'] = gather_chips(1, ['ff2'])
            side = _merge_sides(c1, chips['d1'])
        z2, x2, x2b = _mm_ln(f"w_ff2_ln2_{l}", hid, W[l]['ff2'], 0, x1, ln2_g[l], ln2_b[l], side=side)
        if l == 0:
            arrived(c1)
        saved.append(dict(xin=cur, xinb=curb, u=u, fl=fl, bf_row=bf_row, cf_col=cf_col, cf_row=cf_row, o_fox=o_fox,
                          lse=lse, lp=lp, hstate=hstate, bias=bias, kpad=kpad, vpad=vpad, o_all=o_all, merged=merged,
                          gates=gates, projs=projs, z1=z1, x1=x1, x1b=x1b, hp=hp, hid=hid, z2=z2))
        cur, curb = x2, x2b

    loss_tile, dcur = _loss_head("loss_head", cur, loss_target[0])
    loss = lax.psum(loss_tile[0, 0], ("x", "y", "c"))

    big = [dict() for _ in range(nl)]
    reduced = [dict() for _ in range(nl)]
    sm = {n: [None] * nl for n in ['b_forget', 'conv_w', 'conv_b', 'w_r', 'b_r', 'w_i', 'b_i', 'lru_lambda', 'rel_bias',
                                   'b_gate', 'ln1_g', 'ln1_b', 'ln2_g', 'ln2_b']}

    def split_columns(acc, ex, outs, ids):
        for j in range(N_DEV):
            outs[0][j] = acc[:, j * cs:(j + 1) * cs]

    def grad_mm(key, name, a, b, *, shape, grid, a_spec, b_spec, out_spec, acc_shape, finish=_store(F32)):
        l = int(name[-1])
        big[l][key] = _mm(name, a, b, grid=grid, a_spec=a_spec, b_spec=b_spec, dims=TN, acc_shape=acc_shape,
                          out_shape=[jax.ShapeDtypeStruct(shape, F32)], out_specs=[out_spec], finish=finish)[0]

    def reduce_cores(l, keys):
        side = _cores_side([big[l][k].reshape((4, 2) + big[l][k].shape[1:]) for k in keys], False)
        side.todo = (l, keys)
        return side

    def reduce_chips(cores):
        l, keys = cores.todo
        partial = []
        for k, mine, other in zip(keys, cores.operands, cores.results):
            cols = mine.shape[-1]
            rows = math.prod(mine.shape[2:]) // cols
            partial.append(_add_core_halves(f"add_cores_{k}_{l}", mine.reshape(4, 2, rows, cols),
                                            other.reshape(4, rows, cols), c_arr, BF16))
        side = _chips_side(partial, False)
        side.todo = (l, keys)
        return side

    def reduction_done(chips_side):
        l, keys = chips_side.todo
        reduced[l].update(zip(keys, chips_side.results))

    GROUP1, GROUP2 = ['w_ff2', 'w_ff1', 'w_out', 'w_branch', 'w_gate'], ['w_main']
    pending = None

    tks = min(1024, s)
    tmr = min(1024, d)
    nsh = tmr // rs

    for l in reversed(range(nl)):
        sv = saved[l]
        Wl = W[l]
        dz2, dz2b, dg, db = _ln_bwd(f"ln2_bwd_{l}", dcur, sv['z2'], ln2_g[l])
        sm['ln2_g'][l], sm['ln2_b'][l] = dg[0], db[0]
        tn1 = min(1024, dff)

        def dhp_finish(acc, ex, outs, ids):
            outs[0][...] = (acc * (2.0 * jnp.maximum(ex[0][...].astype(F32), 0.0))).astype(BF16)

        dhp = _mm(f"d_hidden_{l}", dz2b, Wl['ff2'], grid=(s // tm, dff // tn1, d // tkk),
                  a_spec=pl.BlockSpec((tm, tkk), lambda m, n, k: (m, k)),
                  b_spec=pl.BlockSpec((None, tn1, tkk), lambda m, n, k: (0, n, k)),
                  dims=NT, acc_shape=(tm, tn1), out_shape=[jax.ShapeDtypeStruct((s, dff), BF16)],
                  out_specs=[pl.BlockSpec((tm, tn1), lambda m, n, k: (m, n))], finish=dhp_finish,
                  extras=(sv['hp'],), extra_specs=(pl.BlockSpec((tm, tn1), lambda m, n, k: (m, n)),),
                  side=pending)[0]
        if pending is not None:
            reduction_done(pending)
            pending = None
        grad_mm('w_ff2', f"g_w_ff2_{l}", sv['hid'], dz2b, shape=(N_DEV, 1, fs, d), grid=(N_DEV, 1, s // tks),
                a_spec=pl.BlockSpec((tks, fs), lambda m, n, k: (k, m)),
                b_spec=pl.BlockSpec((tks, d), lambda m, n, k: (k, 0)),
                out_spec=pl.BlockSpec((None, None, fs, d), lambda m, n, k: (m, 0, 0, 0)), acc_shape=(fs, d))
        grad_mm('w_ff1', f"g_w_ff1_{l}", sv['x1b'], dhp, shape=(N_DEV, 1, d, fs), grid=(d // tmr, N_DEV, s // tks),
                a_spec=pl.BlockSpec((tks, tmr), lambda m, n, k: (k, m)),
                b_spec=pl.BlockSpec((tks, fs), lambda m, n, k: (k, n)),
                out_spec=pl.BlockSpec((None, None, tmr, fs), lambda m, n, k: (n, 0, m, 0)), acc_shape=(tmr, fs))
        tnd = min(1024, d)

        def resid_finish(scale):
            def finish(acc, ex, outs, ids):
                outs[0][...] = acc + scale * ex[0][...]
            return finish

        tile_md = pl.BlockSpec((tm, tnd), lambda m, n, k: (m, n))
        dx1 = _mm(f"d_x1_{l}", dhp, Wl['ff1'], grid=(s // tm, d // tnd, dff // tkk),
                  a_spec=pl.BlockSpec((tm, tkk), lambda m, n, k: (m, k)),
                  b_spec=pl.BlockSpec((None, tnd, tkk), lambda m, n, k: (0, n, k)),
                  dims=NT, acc_shape=(tm, tnd), out_shape=[jax.ShapeDtypeStruct((s, d), F32)],
                  out_specs=[tile_md], finish=resid_finish(ALPHA), extras=(dz2,), extra_specs=(tile_md,))[0]

        dz1, dz1b, dg, db = _ln_bwd(f"ln1_bwd_{l}", dx1, sv['z1'], ln1_g[l])
        sm['ln1_g'][l], sm['ln1_b'][l] = dg[0], db[0]
        tmg, tng = min(512, s), min(512, d)

        def gate_finish(acc, ex, outs, ids):
            @pl.when(ids[1] == 0)
            def _():
                outs[2][...] = jnp.zeros(outs[2].shape, F32)

            ones = jnp.ones((8, tmg), BF16)
            for g in range(4):
                dpre = (acc * ex[1][g].astype(F32)).astype(BF16)
                outs[0][g] = (acc * ex[0][g].astype(F32)).astype(BF16)
                outs[1][g] = dpre
                outs[2][g] += jnp.dot(ones, dpre, preferred_element_type=F32)

        quad = pl.BlockSpec((4, tmg, tng), lambda n, m, k: (0, m, n))
        dproj, dpre, dbg = _mm(
            f"d_merged_{l}", dz1b, Wl['out'], grid=(d // tng, s // tmg, d // tkk),
            a_spec=pl.BlockSpec((tmg, tkk), lambda n, m, k: (m, k)),
            b_spec=pl.BlockSpec((None, tng, tkk), lambda n, m, k: (0, n, k)),
            dims=NT, acc_shape=(tmg, tng),
            out_shape=[jax.ShapeDtypeStruct((4, s, d), BF16), jax.ShapeDtypeStruct((4, s, d), BF16),
                       jax.ShapeDtypeStruct((4, 8, d), F32)],
            out_specs=[quad, quad, pl.BlockSpec((4, 8, tng), lambda n, m, k: (0, 0, n))], finish=gate_finish,
            extras=(sv['gates'], sv['projs']), extra_specs=(quad, quad), sem=("arbitrary", "arbitrary", "arbitrary"))
        sm['b_gate'][l] = dbg[:, 0, :]
        grad_mm('w_out', f"g_w_out_{l}", sv['merged'], dz1b, shape=(N_DEV, 1, rs, d), grid=(d // tmr, 1, s // tks),
                a_spec=pl.BlockSpec((tks, tmr), lambda m, n, k: (k, m)),
                b_spec=pl.BlockSpec((tks, d), lambda m, n, k: (k, 0)),
                out_spec=pl.BlockSpec((nsh, None, rs, d), lambda m, n, k: (m, 0, 0, 0)), acc_shape=(tmr, d))

        nm = s // tm
        do_all = _mm(f"d_branch_{l}", dproj, Wl['branch'], grid=(4 * nm, 1, d // tkk),
                     a_spec=pl.BlockSpec((None, tm, tkk), lambda m, n, k: (m // nm, m % nm, k)),
                     b_spec=pl.BlockSpec((None, None, bw, tkk), lambda m, n, k: (0, m // nm, 0, k)),
                     dims=NT, acc_shape=(tm, bw), out_shape=[jax.ShapeDtypeStruct((4, s, bw), BF16)],
                     out_specs=[pl.BlockSpec((None, tm, bw), lambda m, n, k: (m // nm, m % nm, 0))],
                     finish=_store(BF16))[0]
        grad_mm('w_branch', f"g_w_branch_{l}", sv['o_all'], dproj, shape=(N_DEV, 1, 4, bw, cs),
                grid=(4, 1, s // tks), finish=split_columns,
                a_spec=pl.BlockSpec((None, tks, bw), lambda m, n, k: (m, k, 0)),
                b_spec=pl.BlockSpec((None, tks, d), lambda m, n, k: (m, k, 0)),
                out_spec=pl.BlockSpec((N_DEV, None, None, bw, cs), lambda m, n, k: (0, 0, m, 0, 0)),
                acc_shape=(bw, d))
        grad_mm('w_gate', f"g_w_gate_{l}", sv['xinb'], dpre, shape=(N_DEV, 1, 4, rs, d), grid=(d // tmr, 4, s // tks),
                a_spec=pl.BlockSpec((tks, tmr), lambda m, n, k: (k, m)),
                b_spec=pl.BlockSpec((None, tks, d), lambda m, n, k: (n, k, 0)),
                out_spec=pl.BlockSpec((nsh, None, None, rs, d), lambda m, n, k: (m, 0, n, 0, 0)),
                acc_shape=(tmr, d))
        nkg = d // tkk
        cores1 = reduce_cores(l, GROUP1)
        dx_gate = _mm(f"d_x_gates_{l}", dpre, Wl['gate'], grid=(s // tm, d // tnd, 4 * nkg),
                      a_spec=pl.BlockSpec((None, tm, tkk), lambda m, n, k: (k // nkg, m, k % nkg)),
                      b_spec=pl.BlockSpec((None, None, tnd, tkk), lambda m, n, k: (0, k // nkg, n, k % nkg)),
                      dims=NT, acc_shape=(tm, tnd), out_shape=[jax.ShapeDtypeStruct((s, d), F32)],
                      out_specs=[tile_md], finish=resid_finish(ALPHA), extras=(dz1,), extra_specs=(tile_md,),
                      side=cores1)[0]

        u = sv['u']
        chips1 = reduce_chips(cores1)
        dfq, dfk, dfv, dcc, dcr = _fox_bwd(f"fox_bwd_{l}", u, sv['cf_col'], sv['cf_row'], sv['o_fox'], do_all[0],
                                          sv['lse'], bw, side=chips1)
        reduction_done(chips1)
        dcf = (dcc[:, :, 0, :].reshape(nh, s) + dcr.reshape(nh, s)).T
        dflb, dbf = _cum_forget_bwd(f"cum_forget_bwd_{l}", jnp.pad(dcf, ((0, 0), (0, LANE - nh))), sv['fl'],
                                    sv['bf_row'])
        sm['b_forget'][l] = dbf[0, :nh]
        drx, dry, dcw, dcb, dwr, dbr, dwi, dbi, dlam = _lru_bwd(f"lru_bwd_{l}", u, sv['hstate'], do_all[1], sv['lp'], bw)
        sm['conv_w'][l], sm['conv_b'][l], sm['w_r'][l], sm['b_r'][l] = dcw[:CONV_WIDTH], dcb[0], dwr, dbr[0]
        sm['w_i'][l], sm['b_i'][l], sm['lru_lambda'][l] = dwi, dbi[0], dlam[0]
        dsq, dsk, dsv = _sb_bwd(f"sb_bwd_{l}", u, do_all[2], bw)
        dcq, dck, dcv, dss = _chunk_bwd(f"chunk_bwd_{l}", u, sv['kpad'], sv['vpad'], sv['bias'], do_all[3], bw)
        sm['rel_bias'][l] = _bias_reduce(f"bias_reduce_{l}", dss, nh)[:, 0, :REL_TABLE]
        du = jnp.concatenate([dfq, dfk, dfv, drx, dry, dsq, dsk, dsv, dcq, dck, dcv, dflb], axis=1)

        tnu = _lane_tile(nue)
        grad_mm('w_main', f"g_w_in_{l}", sv['xinb'], du, shape=(N_DEV, 1, rs, nue),
                grid=(d // tmr, nue // tnu, s // tks),
                a_spec=pl.BlockSpec((tks, tmr), lambda m, n, k: (k, m)),
                b_spec=pl.BlockSpec((tks, tnu), lambda m, n, k: (k, n)),
                out_spec=pl.BlockSpec((nsh, None, rs, tnu), lambda m, n, k: (m, 0, 0, n)), acc_shape=(tmr, tnu))
        cores2 = reduce_cores(l, GROUP2)
        dcur = _mm(f"d_x_in_{l}", du, Wl['main'], grid=(s // tm, d // tnd, nue // tnu),
                   a_spec=pl.BlockSpec((tm, tnu), lambda m, n, k: (m, k)),
                   b_spec=pl.BlockSpec((None, tnd, tnu), lambda m, n, k: (0, n, k)),
                   dims=NT, acc_shape=(tm, tnd), out_shape=[jax.ShapeDtypeStruct((s, d), F32)],
                   out_specs=[tile_md], finish=resid_finish(1.0), extras=(dx_gate,), extra_specs=(tile_md,),
                   side=cores2)[0]
        pending = reduce_chips(cores2)

    grad_x, _, dg_in, db_in = _ln_bwd("ln_in_bwd", dcur, xs, ln_in_g, side=pending)
    reduction_done(pending)

    small_names = ['ln_in_g', 'ln_in_b', 'b_forget', 'conv_w', 'conv_b', 'w_r', 'b_r', 'w_i', 'b_i', 'lru_lambda',
                   'rel_bias', 'b_gate', 'ln1_g', 'ln1_b', 'ln2_g', 'ln2_b']
    local = {'ln_in_g': dg_in[0], 'ln_in_b': db_in[0]}
    for n in small_names[2:]:
        local[n] = jnp.stack(sm[n])
    full_shapes = [local[n].shape for n in small_names]
    packed = _pack([local[n] for n in small_names])
    every = _run_side("gather_small_cores",
                      _cores_side(_run_side("gather_small_chips", _chips_side([packed], True)), True))[0]
    every = jnp.swapaxes(every, 0, 1).reshape((N_DEV,) + packed.shape)
    total = dict(zip(small_names, _unpack(_sum_parts("sum_small", every), full_shapes)))
    for n, width in (('conv_w', bw // N_DEV), ('rel_bias', REL_TABLE // N_DEV), ('b_gate', cs)):
        total[n] = lax.dynamic_slice_in_dim(total[n], dev * width, width, axis=2)

    out = {}

    def update(n, parts):
        cols = parts[0].shape[2]
        w2, m2, v2 = (given[p + n].reshape(-1, cols) for p in ('', 'm_', 'v_'))
        res = None
        for l in range(nl):
            res = _adamw(f"adamw_{n}_{l}", w2, m2, v2, parts[l], layer=l, layers=nl, earlier=res)
        out[n] = [r.reshape(given[n].shape) for r in res]

    def parts_w_in(l):
        pm = reduced[l]['w_main']
        return jnp.concatenate([pm[..., :3 * bw], pm[..., nu:nu + nh], pm[..., 3 * bw:nu]], axis=-1)

    update('w_in', [parts_w_in(l) for l in range(nl)])
    for n in ('w_branch', 'w_gate', 'w_out', 'w_ff1', 'w_ff2'):
        update(n, [reduced[l][n] for l in range(nl)])

    small_shapes2 = [given[n].shape for n in small_names]
    res = _adamw("adamw_small", _pack([given[n] for n in small_names]), _pack([given['m_' + n] for n in small_names]),
                 _pack([given['v_' + n] for n in small_names]), _pack([total[n] for n in small_names])[None])
    res = [_unpack(r, small_shapes2) for r in res]
    for j, n in enumerate(small_names):
        out[n] = [res[k][j] for k in range(4)]

    return (loss, grad_x[None], *[out[n][0] for n in WEIGHTS], *[out[n][1] for n in WEIGHTS],
            *[out[n][2] for n in WEIGHTS], *[out[n][3] for n in WEIGHTS])
```

```python
import functools
import math

import jax
import jax.numpy as jnp
from jax import lax
from jax.experimental import pallas as pl
from jax.experimental.pallas import tpu as pltpu

F32, BF16, I32 = jnp.float32, jnp.bfloat16, jnp.int32
MESH = pl.DeviceIdType.MESH
ANY = pl.BlockSpec(memory_space=pl.ANY)

LANE = 128
VMEM_LIMIT = 56 * 1024 * 1024
N_DEV = 8

HEAD = 128
CHUNK = 64
LOOKBACK = 8
BAND = (LOOKBACK + 1) * CHUNK
QCHUNKS = 4
QBLK = QCHUNKS * CHUNK
WIN = BAND + (QCHUNKS - 1) * CHUNK
PADK = LOOKBACK * CHUNK
REL_CLIP = 256
REL_TABLE = REL_CLIP + CHUNK
REL_PAD = 384
CONV_WIDTH = 4
LRU_C = 8.0
LN_EPS = 1e-5
DEPTH = 2
ALPHA = (2.0 * DEPTH) ** 0.25
NEG = -1e30
SB_DEAD = -104.0
GELU_K = math.sqrt(2.0 / math.pi)
GELU_C = 0.044715

ADAM_LR, ADAM_B1, ADAM_B2, ADAM_EPS, ADAM_WD, ADAM_STEP = 0.001, 0.9, 0.999, 1e-08, 0.01, 10

NN = (((1,), (0,)), ((), ()))
NT = (((1,), (1,)), ((), ()))
TN = (((0,), (0,)), ((), ()))

FQ, FK, FV, RX, RY, SQ, SK, SV, CQ, CK, CV = range(11)

NAMES = ['x', 'ln_in_g', 'ln_in_b', 'w_in', 'b_forget', 'conv_w', 'conv_b', 'w_r', 'b_r', 'w_i', 'b_i', 'lru_lambda',
         'rel_bias', 'w_branch', 'w_gate', 'b_gate', 'w_out', 'ln1_g', 'ln1_b', 'w_ff1', 'w_ff2', 'ln2_g', 'ln2_b']
WEIGHTS = NAMES[1:]


def _cp(sem=None):
    return pltpu.CompilerParams(dimension_semantics=sem, vmem_limit_bytes=VMEM_LIMIT)


def _iota(shape, dim):
    return lax.broadcasted_iota(I32, shape, dim)


def _sigmoid(x):
    return 1.0 / (1.0 + jnp.exp(-x))


def _log_sigmoid(x):
    return jnp.minimum(x, 0.0) - jnp.log(1.0 + jnp.exp(-jnp.abs(x)))


def _lane_tile(n, cap=1536):
    best = max(t for t in range(LANE, cap + 1, LANE) if n % t == 0)
    return n if best == LANE else best


def _pow2_rows(rows, cols, elems=262144):
    t = 8
    while t * 2 <= rows and t * 2 * cols <= elems and rows % (t * 2) == 0:
        t *= 2
    return t


def _position():
    return lax.axis_index("x"), lax.axis_index("y"), lax.axis_index("c")


class _Side:
    def __init__(self, operands, out_shape, sems, start, finish, aliases=(), parts=()):
        self.operands, self.out_shape, self.sems = list(operands), list(out_shape), list(sems)
        self.start, self.finish, self.aliases, self.parts = start, finish, list(aliases), parts
        self.results = None

    def set_results(self, res):
        self.results = list(res)
        off = 0
        for part in self.parts:
            part.set_results(res[off:off + len(part.out_shape)])
            off += len(part.out_shape)


def _merge_sides(a, b):
    ai, ao, asm = len(a.operands), len(a.out_shape), len(a.sems)

    def start(ins, outs, sems):
        a.start(ins[:ai], outs[:ao], sems[:asm])
        b.start(ins[ai:], outs[ao:], sems[asm:])

    def finish(ins, outs, sems):
        a.finish(ins[:ai], outs[:ao], sems[:asm])
        b.finish(ins[ai:], outs[ao:], sems[asm:])

    return _Side(a.operands + b.operands, a.out_shape + b.out_shape, a.sems + b.sems, start, finish,
                 a.aliases + [(i + ai, o + ao) for i, o in b.aliases], parts=(a, b))


def _chips_side(xs, gather):
    n = len(xs)

    def copies(ins, outs, sems, arrivals):
        send_sems, recv_sems, local_sems = sems
        x, y, c = _position()
        q = 2 * x + y
        chips = [(1 - x, y), (x, 1 - y), (1 - x, 1 - y)]

        def src(t, slot):
            return ins[t] if gather else ins[t].at[slot]

        def dst(t, slot):
            return outs[t].at[c, slot] if gather else outs[t].at[slot]

        def remote(t, j, landing):
            px, py = chips[j]
            return pltpu.make_async_remote_copy(
                src_ref=src(t, 2 * px + py), dst_ref=dst(t, landing), send_sem=send_sems.at[t, j],
                recv_sem=recv_sems.at[t, j], device_id=(px, py, c), device_id_type=MESH)

        local = [pltpu.make_async_copy(src(t, q), dst(t, q), local_sems.at[t]) for t in range(n)]
        sends = [remote(t, j, q) for t in range(n) for j in range(3)]
        if not arrivals:
            return local, sends, []
        return local, sends, [remote(t, j, 2 * px + py) for t in range(n) for j, (px, py) in enumerate(chips)]

    def start(ins, outs, sems):
        local, sends, _ = copies(ins, outs, sems, False)
        for cp in local + sends:
            cp.start()

    def finish(ins, outs, sems):
        local, sends, recvs = copies(ins, outs, sems, True)
        for cp in recvs:
            cp.wait_recv()
        for cp in sends:
            cp.wait_send()
        for cp in local:
            cp.wait()

    out_shape = [jax.ShapeDtypeStruct((2, 4) + a.shape if gather else a.shape, a.dtype) for a in xs]
    sems = [pltpu.SemaphoreType.DMA((n, 3)), pltpu.SemaphoreType.DMA((n, 3)), pltpu.SemaphoreType.DMA((n,))]
    return _Side(xs, out_shape, sems, start, finish)


def _cores_side(xs, gather):
    n = len(xs)
    m = 1 if gather else 4

    def copies(ins, outs, sems, arrivals):
        send_sems, recv_sems = sems
        x, y, c = _position()

        def remote(t, j, landing):
            s = outs[t].at[c] if gather else ins[t].at[j, 1 - c]
            d = outs[t].at[landing] if gather else outs[t].at[j]
            return pltpu.make_async_remote_copy(
                src_ref=s, dst_ref=d, send_sem=send_sems.at[t, j], recv_sem=recv_sems.at[t, j],
                device_id=(x, y, 1 - c), device_id_type=MESH)

        sends = [remote(t, j, c) for t in range(n) for j in range(m)]
        return sends, [remote(t, j, 1 - c) for t in range(n) for j in range(m)] if arrivals else []

    def start(ins, outs, sems):
        for cp in copies(ins, outs, sems, False)[0]:
            cp.start()

    def finish(ins, outs, sems):
        sends, recvs = copies(ins, outs, sems, True)
        for cp in recvs:
            cp.wait_recv()
        for cp in sends:
            cp.wait_send()

    if gather:
        out_shape = [jax.ShapeDtypeStruct(a.shape, a.dtype) for a in xs]
    else:
        out_shape = [jax.ShapeDtypeStruct((4,) + a.shape[2:], a.dtype) for a in xs]
    sems = [pltpu.SemaphoreType.DMA((n, m)), pltpu.SemaphoreType.DMA((n, m))]
    return _Side(xs, out_shape, sems, start, finish, aliases=[(t, t) for t in range(n)] if gather else [])


def _run_side(name, side):
    ni, no = len(side.operands), len(side.out_shape)

    def body(*refs):
        ins, outs, sems = refs[:ni], refs[ni:ni + no], refs[ni + no:]
        side.start(ins, outs, sems)
        side.finish(ins, outs, sems)

    side.set_results(pl.pallas_call(
        body, name=name, out_shape=side.out_shape, in_specs=[ANY] * ni, out_specs=[ANY] * no,
        input_output_aliases=dict(side.aliases), scratch_shapes=side.sems)(*side.operands))
    return side.results


def _pcall(body, operands, *, name, grid, in_specs, out_specs, out_shape, scratch_shapes=(), sem=None, aliases=None,
           side=None):
    if side is None:
        return pl.pallas_call(body, name=name, grid=grid, in_specs=list(in_specs), out_specs=list(out_specs),
                              out_shape=list(out_shape), scratch_shapes=list(scratch_shapes),
                              input_output_aliases=aliases or {}, compiler_params=_cp(sem))(*operands)
    ni, no, ns = len(in_specs), len(out_shape), len(scratch_shapes)
    si, so = len(side.operands), len(side.out_shape)

    def carrying(*refs):
        ins, sins = refs[:ni], refs[ni:ni + si]
        outs, souts = refs[ni + si:ni + si + no], refs[ni + si + no:ni + si + no + so]
        scratch, ssems = refs[ni + si + no + so:ni + si + no + so + ns], refs[ni + si + no + so + ns:]
        ids = [pl.program_id(a) for a in range(len(grid))]
        first = functools.reduce(jnp.logical_and, [i == 0 for i in ids])
        last = functools.reduce(jnp.logical_and, [i == g - 1 for i, g in zip(ids, grid)])

        @pl.when(first)
        def _():
            side.start(sins, souts, ssems)

        body(*ins, *outs, *scratch)

        @pl.when(last)
        def _():
            side.finish(sins, souts, ssems)

    joined = dict(aliases or {})
    joined.update({ni + i: no + o for i, o in side.aliases})
    res = pl.pallas_call(
        carrying, name=name, grid=grid, in_specs=[*in_specs, *[ANY] * si], out_specs=[*out_specs, *[ANY] * so],
        out_shape=[*out_shape, *side.out_shape], scratch_shapes=[*scratch_shapes, *side.sems],
        input_output_aliases=joined, compiler_params=_cp(("arbitrary",) * len(grid)))(*operands, *side.operands)
    side.set_results(res[no:])
    return res[:no]


def _add_core_halves(name, mine, other, c, out_dtype):
    _, _, rows, cols = mine.shape
    tr = _pow2_rows(rows, cols)

    def body(c_ref, a_ref, b_ref, o_ref):
        o_ref[...] = (a_ref[...].astype(F32) + b_ref[...].astype(F32)).astype(out_dtype)

    grid_spec = pltpu.PrefetchScalarGridSpec(
        num_scalar_prefetch=1, grid=(4, rows // tr),
        in_specs=[pl.BlockSpec((None, None, tr, cols), lambda j, i, c_ref: (j, c_ref[0], i, 0)),
                  pl.BlockSpec((None, tr, cols), lambda j, i, c_ref: (j, i, 0))],
        out_specs=pl.BlockSpec((None, tr, cols), lambda j, i, c_ref: (j, i, 0)))
    return pl.pallas_call(body, name=name, grid_spec=grid_spec,
                          out_shape=jax.ShapeDtypeStruct((4, rows, cols), out_dtype),
                          compiler_params=_cp(("parallel", "parallel")))(c, mine, other)


def _sum_parts(name, parts):
    p, rows, cols = parts.shape
    tr = _pow2_rows(rows, cols * p)

    def body(a_ref, o_ref):
        acc = a_ref[0]
        for k in range(1, p):
            acc = acc + a_ref[k]
        o_ref[...] = acc

    return pl.pallas_call(body, name=name, grid=(rows // tr,),
                          in_specs=[pl.BlockSpec((p, tr, cols), lambda i: (0, i, 0))],
                          out_specs=pl.BlockSpec((tr, cols), lambda i: (i, 0)),
                          out_shape=jax.ShapeDtypeStruct((rows, cols), F32),
                          compiler_params=_cp(("parallel",)))(parts)


def _mm(name, a, b, *, grid, a_spec, b_spec, dims, acc_shape, out_shape, out_specs, finish,
        extras=(), extra_specs=(), aliases=None, sem=("parallel", "parallel", "arbitrary"), side=None):
    nk, ne, no = grid[2], len(extras), len(out_shape)

    def body(*refs):
        a_ref, b_ref = refs[0], refs[1]
        ex, outs = refs[2:2 + ne], refs[2 + ne:2 + ne + no]
        ids = (pl.program_id(0), pl.program_id(1))
        def prod():
            return lax.dot_general(a_ref[...], b_ref[...], dims, preferred_element_type=F32)

        if nk == 1:
            finish(prod(), ex, outs, ids)
            return
        acc = refs[2 + ne + no]
        k = pl.program_id(2)

        @pl.when(k == 0)
        def _():
            acc[...] = prod()

        @pl.when(jnp.logical_and(k > 0, k < nk - 1))
        def _():
            acc[...] += prod()

        @pl.when(k == nk - 1)
        def _():
            finish(acc[...] + prod(), ex, outs, ids)

    return _pcall(body, (a, b, *extras), name=name, grid=grid, in_specs=[a_spec, b_spec, *extra_specs],
                  out_specs=out_specs, out_shape=out_shape,
                  scratch_shapes=[pltpu.VMEM(acc_shape, F32)] if nk > 1 else [], sem=sem, aliases=aliases, side=side)


def _store(dtype):
    def finish(acc, ex, outs, ids):
        outs[0][...] = acc.reshape(outs[0].shape).astype(dtype)
    return finish


def _layer_norm_rows(z, g, b):
    mu = jnp.mean(z, axis=1, keepdims=True)
    zc = z - mu
    var = jnp.mean(zc * zc, axis=1, keepdims=True)
    return zc * lax.rsqrt(var + LN_EPS) * g + b


def _mm_ln(name, a, w, l, resid, g, b, side=None):
    s, kdim = a.shape
    d = w.shape[2]
    tm, tk = min(512, s), min(1024, kdim)

    def finish(acc, ex, outs, ids):
        z = acc + ALPHA * ex[0][...]
        y = _layer_norm_rows(z, ex[1][...], ex[2][...])
        outs[0][...] = z
        outs[1][...] = y
        outs[2][...] = y.astype(BF16)

    row = pl.BlockSpec((tm, d), lambda m, n, k: (m, 0))
    vec = pl.BlockSpec((1, d), lambda m, n, k: (0, 0))
    return _mm(name, a, w, grid=(s // tm, 1, kdim // tk),
               a_spec=pl.BlockSpec((tm, tk), lambda m, n, k: (m, k)),
               b_spec=pl.BlockSpec((None, tk, d), lambda m, n, k: (l, k, 0)),
               dims=NN, acc_shape=(tm, d),
               out_shape=[jax.ShapeDtypeStruct((s, d), F32), jax.ShapeDtypeStruct((s, d), F32),
                          jax.ShapeDtypeStruct((s, d), BF16)],
               out_specs=[row, row, row], finish=finish,
               extras=(resid, g.reshape(1, d), b.reshape(1, d)), extra_specs=(row, vec, vec), side=side)


def _ln_fwd(name, x, g, b, side=None):
    s, d = x.shape
    tr = min(256, s)

    def body(x_ref, g_ref, b_ref, y_ref, yb_ref):
        y = _layer_norm_rows(x_ref[...], g_ref[...], b_ref[...])
        y_ref[...] = y
        yb_ref[...] = y.astype(BF16)

    row = pl.BlockSpec((tr, d), lambda i: (i, 0))
    vec = pl.BlockSpec((1, d), lambda i: (0, 0))
    return _pcall(body, (x, g.reshape(1, d), b.reshape(1, d)), name=name, grid=(s // tr,), in_specs=[row, vec, vec],
                  out_specs=[row, row],
                  out_shape=[jax.ShapeDtypeStruct((s, d), F32), jax.ShapeDtypeStruct((s, d), BF16)],
                  sem=("parallel",), side=side)


def _ln_bwd_rows(dyv, zz, g, dz_ref, dzb_ref, dg_ref, db_ref):
    mu = jnp.mean(zz, axis=1, keepdims=True)
    zc = zz - mu
    rstd = lax.rsqrt(jnp.mean(zc * zc, axis=1, keepdims=True) + LN_EPS)
    xhat = zc * rstd
    dg_ref[...] += jnp.sum(dyv * xhat, axis=0, keepdims=True)
    db_ref[...] += jnp.sum(dyv, axis=0, keepdims=True)
    dxh = dyv * g
    dz = rstd * (dxh - jnp.mean(dxh, axis=1, keepdims=True) - xhat * jnp.mean(dxh * xhat, axis=1, keepdims=True))
    dz_ref[...] = dz
    dzb_ref[...] = dz.astype(BF16)


def _ln_bwd(name, dy, z, g, side=None):
    s, d = z.shape
    tr = min(256, s)

    def body(dy_ref, z_ref, g_ref, dz_ref, dzb_ref, dg_ref, db_ref):
        @pl.when(pl.program_id(0) == 0)
        def _():
            dg_ref[...] = jnp.zeros(dg_ref.shape, F32)
            db_ref[...] = jnp.zeros(db_ref.shape, F32)

        _ln_bwd_rows(dy_ref[...], z_ref[...], g_ref[...], dz_ref, dzb_ref, dg_ref, db_ref)

    row = pl.BlockSpec((tr, d), lambda i: (i, 0))
    vec = pl.BlockSpec((1, d), lambda i: (0, 0))
    return _pcall(
        body, (dy, z, g.reshape(1, d)), name=name, grid=(s // tr,), in_specs=[row, row, vec],
        out_specs=[row, row, vec, vec],
        out_shape=[jax.ShapeDtypeStruct((s, d), F32), jax.ShapeDtypeStruct((s, d), BF16),
                   jax.ShapeDtypeStruct((1, d), F32), jax.ShapeDtypeStruct((1, d), F32)],
        sem=("arbitrary",), side=side)


def _loss_ln_bwd(name, y, target, z, g):
    s, d = y.shape
    tr = min(256, s)

    def body(y_ref, t_ref, z_ref, g_ref, loss_ref, dz_ref, dzb_ref, dg_ref, db_ref):
        @pl.when(pl.program_id(0) == 0)
        def _():
            loss_ref[...] = jnp.zeros(loss_ref.shape, F32)
            dg_ref[...] = jnp.zeros(dg_ref.shape, F32)
            db_ref[...] = jnp.zeros(db_ref.shape, F32)

        e = y_ref[...] - t_ref[...]
        loss_ref[...] += jnp.sum(e * e) * (0.5 / d)
        _ln_bwd_rows(e * (1.0 / d), z_ref[...], g_ref[...], dz_ref, dzb_ref, dg_ref, db_ref)

    row = pl.BlockSpec((tr, d), lambda i: (i, 0))
    vec = pl.BlockSpec((1, d), lambda i: (0, 0))
    return pl.pallas_call(
        body, name=name, grid=(s // tr,), in_specs=[row, row, row, vec],
        out_specs=[pl.BlockSpec((8, LANE), lambda i: (0, 0)), row, row, vec, vec],
        out_shape=[jax.ShapeDtypeStruct((8, LANE), F32), jax.ShapeDtypeStruct((s, d), F32),
                   jax.ShapeDtypeStruct((s, d), BF16), jax.ShapeDtypeStruct((1, d), F32),
                   jax.ShapeDtypeStruct((1, d), F32)],
        compiler_params=_cp(("arbitrary",)))(y, target, z, g.reshape(1, d))


def _scan_add(x, reverse):
    ts = x.shape[0]
    rows = _iota((ts, 1), 0)
    dist = 1
    while dist < ts:
        if reverse:
            x = x + jnp.where(rows < ts - dist, pltpu.roll(x, ts - dist, 0), 0.0)
        else:
            x = x + jnp.where(rows >= dist, pltpu.roll(x, dist, 0), 0.0)
        dist *= 2
    return x


def _scan_affine(a, b, reverse):
    ts = a.shape[0]
    rows = _iota((ts, 1), 0)
    dist = 1
    while dist < ts:
        shift = ts - dist if reverse else dist
        valid = rows < ts - dist if reverse else rows >= dist
        b = b + a * jnp.where(valid, pltpu.roll(b, shift, 0), 0.0)
        a = a * jnp.where(valid, pltpu.roll(a, shift, 0), 1.0)
        dist *= 2
    return a, b


def _cum_forget_fwd(name, fl, bias):
    s = fl.shape[0]
    ts = min(1024, s)

    def body(f_ref, b_ref, o_ref, carry):
        @pl.when(pl.program_id(0) == 0)
        def _():
            carry[...] = jnp.zeros(carry.shape, F32)

        o_ref[...] = _scan_add(_log_sigmoid(f_ref[...] + b_ref[...]), False) + carry[...]
        carry[...] = o_ref[pl.ds(ts - 1, 1), :]

    row = pl.BlockSpec((ts, LANE), lambda i: (i, 0))
    return pl.pallas_call(body, name=name, grid=(s // ts,),
                          in_specs=[row, pl.BlockSpec((1, LANE), lambda i: (0, 0))], out_specs=row,
                          out_shape=jax.ShapeDtypeStruct((s, LANE), F32),
                          scratch_shapes=[pltpu.VMEM((1, LANE), F32)],
                          compiler_params=_cp(("arbitrary",)))(fl, bias)


def _cum_forget_bwd(name, dcf, fl, bias):
    s = fl.shape[0]
    ts = min(1024, s)
    nb = s // ts

    def body(d_ref, f_ref, b_ref, o_ref, db_ref, carry):
        @pl.when(pl.program_id(0) == 0)
        def _():
            carry[...] = jnp.zeros(carry.shape, F32)
            db_ref[...] = jnp.zeros(db_ref.shape, F32)

        run = _scan_add(d_ref[...], True) + carry[...]
        carry[...] = jnp.sum(jnp.where(_iota((ts, 1), 0) == 0, run, 0.0), axis=0, keepdims=True)
        dfl = run * _sigmoid(-(f_ref[...] + b_ref[...]))
        o_ref[...] = dfl.astype(BF16)
        db_ref[...] += jnp.sum(dfl, axis=0, keepdims=True)

    row = pl.BlockSpec((ts, LANE), lambda i: (nb - 1 - i, 0))
    vec = pl.BlockSpec((1, LANE), lambda i: (0, 0))
    return pl.pallas_call(body, name=name, grid=(nb,), in_specs=[row, row, vec], out_specs=[row, vec],
                          out_shape=[jax.ShapeDtypeStruct((s, LANE), BF16), jax.ShapeDtypeStruct((1, LANE), F32)],
                          scratch_shapes=[pltpu.VMEM((1, LANE), F32)],
                          compiler_params=_cp(("arbitrary",)))(dcf, fl, bias)


def _fox_specs(s, nh, tq, tk):
    q = pl.BlockSpec((tq, HEAD), lambda h, i: (i, FQ * nh + h))
    k = pl.BlockSpec((s, HEAD), lambda h, i: (0, FK * nh + h))
    v = pl.BlockSpec((s, HEAD), lambda h, i: (0, FV * nh + h))
    col = pl.BlockSpec((None, tq, 1), lambda h, i: (h, i, 0))
    rowv = pl.BlockSpec((None, s // tk, 1, tk), lambda h, i: (h, 0, 0, 0))
    tile = pl.BlockSpec((tq, HEAD), lambda h, i: (i, h))
    full = pl.BlockSpec((s, HEAD), lambda h, i: (0, h))
    return q, k, v, col, rowv, tile, full


def _fox_tile(s):
    return min(512, s)


def _fox_scores(q, k_ref, cfq, cfr_ref, kb, tk, scale, diagonal):
    off = pl.multiple_of(kb * tk, tk)
    k = k_ref[pl.ds(off, tk), :]
    sc = lax.dot_general(q, k, NT, preferred_element_type=F32) * scale + cfq - cfr_ref[kb]
    mask = None
    if diagonal:
        mask = _iota((1, tk), 1) <= _iota((tk, 1), 0)
        sc = jnp.where(mask, sc, NEG)
    return sc, mask, k, off


def _fox_fwd(name, u, cf_col, cf_row, bw, side=None):
    s, nh = u.shape[0], bw // HEAD
    tq = tk = _fox_tile(s)
    scale = HEAD ** -0.5

    def body(q_ref, k_ref, v_ref, cfc_ref, cfr_ref, o_ref, lse_ref):
        i = pl.program_id(1)
        q, cfq = q_ref[...], cfc_ref[...]

        def step(kb, carry, diagonal=False):
            m, l, acc = carry
            sc, _, _, off = _fox_scores(q, k_ref, cfq, cfr_ref, kb, tk, scale, diagonal)
            m2 = jnp.maximum(m, jnp.max(sc, axis=1, keepdims=True))
            p = jnp.exp(sc - m2)
            al = jnp.exp(m - m2)
            return (m2, al * l + jnp.sum(p, axis=1, keepdims=True),
                    al * acc + jnp.dot(p.astype(BF16), v_ref[pl.ds(off, tk), :], preferred_element_type=F32))

        init = (jnp.full((tq, 1), NEG, F32), jnp.zeros((tq, 1), F32), jnp.zeros((tq, HEAD), F32))
        m, l, acc = step(i, lax.fori_loop(0, i, step, init), True)
        o_ref[...] = (acc / l).astype(BF16)
        lse_ref[...] = m + jnp.log(l)

    q, k, v, col, rowv, tile, _ = _fox_specs(s, nh, tq, tk)
    return _pcall(body, (u, u, u, cf_col, cf_row), name=name, grid=(nh, s // tq), in_specs=[q, k, v, col, rowv],
                  out_specs=[tile, col],
                  out_shape=[jax.ShapeDtypeStruct((s, bw), BF16), jax.ShapeDtypeStruct((nh, s, 1), F32)],
                  sem=("parallel", "parallel"), side=side)


def _fox_bwd(name, u, cf_col, cf_row, o, do, lse, bw, side=None):
    s, nh = u.shape[0], bw // HEAD
    tq = tk = _fox_tile(s)
    nq = s // tq
    scale = HEAD ** -0.5

    def body(q_ref, k_ref, v_ref, cfc_ref, cfr_ref, o_ref, do_ref, lse_ref,
             dq_ref, dk_ref, dv_ref, dcc_ref, dcr_ref, dk_s, dv_s):
        i = pl.program_id(1)

        @pl.when(i == 0)
        def _():
            dk_s[...] = jnp.zeros(dk_s.shape, F32)
            dv_s[...] = jnp.zeros(dv_s.shape, F32)
            dcr_ref[...] = jnp.zeros(dcr_ref.shape, F32)

        q, dov, cfq, lse_q = q_ref[...], do_ref[...], cfc_ref[...], lse_ref[...]
        delta = jnp.sum(dov.astype(F32) * o_ref[...].astype(F32), axis=1, keepdims=True)

        def step(kb, carry, diagonal=False):
            dq, dcq = carry
            sc, mask, k, off = _fox_scores(q, k_ref, cfq, cfr_ref, kb, tk, scale, diagonal)
            p = jnp.exp(sc - lse_q)
            if diagonal:
                p = jnp.where(mask, p, 0.0)
            dp = lax.dot_general(dov, v_ref[pl.ds(off, tk), :], NT, preferred_element_type=F32)
            ds = p * (dp - delta)
            dsb = ds.astype(BF16)
            dk_s[pl.ds(off, tk), :] += lax.dot_general(dsb, q, TN, preferred_element_type=F32)
            dv_s[pl.ds(off, tk), :] += lax.dot_general(p.astype(BF16), dov, TN, preferred_element_type=F32)
            dcr_ref[kb] += -jnp.sum(ds, axis=0, keepdims=True)
            return (dq + jnp.dot(dsb, k, preferred_element_type=F32), dcq + jnp.sum(ds, axis=1, keepdims=True))

        init = (jnp.zeros((tq, HEAD), F32), jnp.zeros((tq, 1), F32))
        dq, dcq = step(i, lax.fori_loop(0, i, step, init), True)
        dq_ref[...] = (dq * scale).astype(BF16)
        dcc_ref[...] = jnp.transpose(jnp.broadcast_to(dcq, (tq, LANE)))[:8, :]

        @pl.when(i == nq - 1)
        def _():
            dk_ref[...] = (dk_s[...] * scale).astype(BF16)
            dv_ref[...] = dv_s[...].astype(BF16)

    q, k, v, col, rowv, tile, full = _fox_specs(s, nh, tq, tk)
    by_query = pl.BlockSpec((None, None, 8, tq), lambda h, i: (h, i, 0, 0))
    return _pcall(
        body, (u, u, u, cf_col, cf_row, o, do, lse), name=name, grid=(nh, nq),
        in_specs=[q, k, v, col, rowv, tile, tile, col], out_specs=[tile, full, full, by_query, rowv],
        out_shape=[jax.ShapeDtypeStruct((s, bw), BF16)] * 3
        + [jax.ShapeDtypeStruct((nh, nq, 8, tq), F32), jax.ShapeDtypeStruct((nh, s // tk, 1, tk), F32)],
        scratch_shapes=[pltpu.VMEM((s, HEAD), F32), pltpu.VMEM((s, HEAD), F32)],
        sem=("arbitrary", "arbitrary"), side=side)


def _suffix_mm(x, ones_below):
    hi = x.astype(BF16)
    lo = (x - hi.astype(F32)).astype(BF16)
    return (jnp.dot(hi, ones_below, preferred_element_type=F32) + jnp.dot(lo, ones_below, preferred_element_type=F32))


def _sb_tile(q, k_ref, kb, tk, qpos, scale):
    off = pl.multiple_of(kb * tk, tk)
    k = k_ref[pl.ds(off, tk), :]
    z = lax.dot_general(q, k, NT, preferred_element_type=F32) * scale
    mask = kb * tk + _iota((1, tk), 1) < qpos
    lsn = -jnp.maximum(z, 0.0) - jnp.log(1.0 + jnp.exp(-jnp.abs(z)))
    return z, mask, lsn, jnp.where(mask, lsn, 0.0), k, off


def _sb_specs(s, nh, tq):
    q = pl.BlockSpec((tq, HEAD), lambda h, i: (i, SQ * nh + h))
    k = pl.BlockSpec((s, HEAD), lambda h, i: (0, SK * nh + h))
    v = pl.BlockSpec((s, HEAD), lambda h, i: (0, SV * nh + h))
    tile = pl.BlockSpec((tq, HEAD), lambda h, i: (i, h))
    full = pl.BlockSpec((s, HEAD), lambda h, i: (0, h))
    return q, k, v, tile, full


def _sb_fwd(name, u, bw, side=None):
    s, nh = u.shape[0], bw // HEAD
    tq = tk = 256
    scale = HEAD ** -0.5

    def body(q_ref, k_ref, v_ref, o_ref):
        i = pl.program_id(1)
        q = q_ref[...]
        qpos = i * tq + _iota((tq, 1), 0)
        later_keys = (_iota((tk, tk), 0) > _iota((tk, tk), 1)).astype(BF16)
        nk = (i * tq + tq + tk - 2) // tk

        def cond(st):
            return jnp.logical_and(st[0] < nk, st[3] > SB_DEAD)

        def step(st):
            j, c, acc, _ = st
            z, mask, lsn, lm, _, off = _sb_tile(q, k_ref, nk - 1 - j, tk, qpos, scale)
            a = jnp.where(mask, jnp.exp(lsn + z + c + _suffix_mm(lm, later_keys)), 0.0)
            acc = acc + jnp.dot(a.astype(BF16), v_ref[pl.ds(off, tk), :], preferred_element_type=F32)
            c = c + jnp.sum(lm, axis=1, keepdims=True)
            return j + 1, c, acc, jnp.max(c)

        init = (jnp.int32(0), jnp.zeros((tq, 1), F32), jnp.zeros((tq, HEAD), F32), jnp.float32(0.0))
        o_ref[...] = lax.while_loop(cond, step, init)[2].astype(BF16)

    q, k, v, tile, _ = _sb_specs(s, nh, tq)
    return _pcall(body, (u, u, u), name=name, grid=(nh, s // tq), in_specs=[q, k, v], out_specs=[tile],
                  out_shape=[jax.ShapeDtypeStruct((s, bw), BF16)], sem=("parallel", "parallel"), side=side)[0]


def _sb_bwd(name, u, do, bw):
    s, nh = u.shape[0], bw // HEAD
    tq = tk = 256
    nq = s // tq
    scale = HEAD ** -0.5

    def body(q_ref, k_ref, v_ref, do_ref, dq_ref, dk_ref, dv_ref, dk_s, dv_s):
        i = pl.program_id(1)

        @pl.when(i == 0)
        def _():
            dk_s[...] = jnp.zeros(dk_s.shape, F32)
            dv_s[...] = jnp.zeros(dv_s.shape, F32)

        q, dov = q_ref[...], do_ref[...]
        qpos = i * tq + _iota((tq, 1), 0)
        later_keys = (_iota((tk, tk), 0) > _iota((tk, tk), 1)).astype(BF16)
        this_and_later = (_iota((tk, tk), 0) >= _iota((tk, tk), 1)).astype(BF16)
        nk = (i * tq + tq + tk - 2) // tk

        def weights(j, c):
            z, mask, lsn, lm, k, off = _sb_tile(q, k_ref, nk - 1 - j, tk, qpos, scale)
            a = jnp.where(mask, jnp.exp(lsn + z + c + _suffix_mm(lm, later_keys)), 0.0)
            w = a * lax.dot_general(dov, v_ref[pl.ds(off, tk), :], NT, preferred_element_type=F32)
            return z, mask, lsn, lm, k, off, a, w

        def cond(st):
            return jnp.logical_and(st[0] < nk, st[3] > SB_DEAD)

        def step1(st):
            j, c, wc, _ = st
            _, _, _, lm, _, _, _, w = weights(j, c)
            c = c + jnp.sum(lm, axis=1, keepdims=True)
            return j + 1, c, wc + jnp.sum(w, axis=1, keepdims=True), jnp.max(c)

        zero = jnp.zeros((tq, 1), F32)
        live, _, total, _ = lax.while_loop(cond, step1, (jnp.int32(0), zero, zero, jnp.float32(0.0)))

        def step2(j, st):
            c, wc, dq = st
            z, mask, lsn, lm, k, off, a, w = weights(j, c)
            earlier = total - (wc + _suffix_mm(w, this_and_later))
            dz = jnp.where(mask, w * jnp.exp(lsn) - jnp.exp(lsn + z) * earlier, 0.0)
            dzb = dz.astype(BF16)
            dk_s[pl.ds(off, tk), :] += lax.dot_general(dzb, q, TN, preferred_element_type=F32)
            dv_s[pl.ds(off, tk), :] += lax.dot_general(a.astype(BF16), dov, TN, preferred_element_type=F32)
            return (c + jnp.sum(lm, axis=1, keepdims=True), wc + jnp.sum(w, axis=1, keepdims=True),
                    dq + jnp.dot(dzb, k, preferred_element_type=F32))

        dq = lax.fori_loop(0, live, step2, (zero, zero, jnp.zeros((tq, HEAD), F32)))[2]
        dq_ref[...] = (dq * scale).astype(BF16)

        @pl.when(i == nq - 1)
        def _():
            dk_ref[...] = (dk_s[...] * scale).astype(BF16)
            dv_ref[...] = dv_s[...].astype(BF16)

    q, k, v, tile, full = _sb_specs(s, nh, tq)
    return pl.pallas_call(
        body, name=name, grid=(nh, nq), in_specs=[q, k, v, tile], out_specs=[tile, full, full],
        out_shape=[jax.ShapeDtypeStruct((s, bw), BF16)] * 3,
        scratch_shapes=[pltpu.VMEM((s, HEAD), F32), pltpu.VMEM((s, HEAD), F32)],
        compiler_params=_cp(("arbitrary", "arbitrary")))(u, u, u, do)


BIAS_W = -(-(WIN + QBLK - 1) // LANE) * LANE


def _strip_onehot():
    col = _iota((1, BIAS_W), 1)
    ridx = jnp.clip(PADK + (QBLK - 1) - col, -(CHUNK - 1), REL_CLIP) + (CHUNK - 1)
    return (_iota((REL_PAD, BIAS_W), 0) == ridx).astype(BF16)


def _split2(x):
    hi = x.astype(BF16)
    return hi, (x - hi.astype(F32)).astype(BF16)


def _bias_expand(name, table, nh):
    def body(t_ref, o_ref, strip):
        table_f32 = t_ref[...]
        hi = table_f32.astype(BF16)
        mid, lo = _split2(table_f32 - hi.astype(F32))
        onehot = _strip_onehot()
        strip[...] = (jnp.dot(hi, onehot, preferred_element_type=F32) + jnp.dot(mid, onehot, preferred_element_type=F32)
                      + jnp.dot(lo, onehot, preferred_element_type=F32))
        row, kl = _iota((QBLK, 1), 0), _iota((1, WIN), 1)
        first = row - jnp.bitwise_and(row, CHUNK - 1)
        valid = jnp.logical_and(kl >= first, kl < first + BAND)
        for h in range(nh):
            rows = jnp.broadcast_to(strip[pl.ds(h, 1), :], (QBLK, BIAS_W))
            rolled = pltpu.roll(rows, BIAS_W - (QBLK - 1), 1, stride=1, stride_axis=0)
            o_ref[h] = jnp.where(valid, rolled[:, :WIN], NEG)

    return pl.pallas_call(body, name=name, out_shape=jax.ShapeDtypeStruct((nh, QBLK, WIN), F32),
                          in_specs=[pl.BlockSpec(memory_space=pltpu.VMEM)],
                          out_specs=pl.BlockSpec(memory_space=pltpu.VMEM),
                          scratch_shapes=[pltpu.VMEM((16, BIAS_W), F32)], compiler_params=_cp())(table)


def _bias_reduce(name, dss, nh):
    def body(x_ref, o_ref):
        onehot = _strip_onehot()
        flip = (_iota((QBLK, QBLK), 0) + _iota((QBLK, QBLK), 1) == QBLK - 1).astype(BF16)
        for h in range(nh):
            x = jnp.concatenate([x_ref[h], jnp.zeros((QBLK, BIAS_W - WIN), F32)], axis=1)
            hi, lo = _split2(x)
            back = jnp.dot(flip, hi, preferred_element_type=F32) + jnp.dot(flip, lo, preferred_element_type=F32)
            lined = pltpu.roll(back, 0, 1, stride=1, stride_axis=0)
            hi, lo = _split2(jnp.broadcast_to(jnp.sum(lined, axis=0, keepdims=True), (8, BIAS_W)))
            o_ref[h] = (lax.dot_general(hi, onehot, NT, preferred_element_type=F32)
                        + lax.dot_general(lo, onehot, NT, preferred_element_type=F32))

    return pl.pallas_call(body, name=name, out_shape=jax.ShapeDtypeStruct((nh, 8, REL_PAD), F32),
                          in_specs=[pl.BlockSpec(memory_space=pltpu.VMEM)],
                          out_specs=pl.BlockSpec(memory_space=pltpu.VMEM), compiler_params=_cp())(dss)


def _chunk_specs(s, nh):
    q = pl.BlockSpec((QBLK, HEAD), lambda h, i: (i, CQ * nh + h))
    kv = pl.BlockSpec((s + PADK, HEAD), lambda h, i: (0, h))
    bias = pl.BlockSpec((None, QBLK, WIN), lambda h, i: (h, 0, 0))
    tile = pl.BlockSpec((QBLK, HEAD), lambda h, i: (i, h))
    full = pl.BlockSpec((s, HEAD), lambda h, i: (0, h))
    return q, kv, bias, tile, full


def _chunk_probs(q, k_ref, b_ref, i, scale):
    off = pl.multiple_of(i * QBLK, QBLK)
    kw = k_ref[pl.ds(off, WIN), :]
    sc = lax.dot_general(q, kw, NT, preferred_element_type=F32) * scale + b_ref[...]
    sc = jnp.where(i * QBLK + _iota((1, WIN), 1) >= PADK, sc, NEG)
    p = jnp.exp(sc - jnp.max(sc, axis=1, keepdims=True))
    return p, jnp.sum(p, axis=1, keepdims=True), kw, off


def _chunk_fwd(name, u, kpad, vpad, bias, bw, side=None):
    s, nh = u.shape[0], bw // HEAD
    scale = HEAD ** -0.5

    def body(q_ref, k_ref, v_ref, b_ref, o_ref):
        p, l, _, off = _chunk_probs(q_ref[...], k_ref, b_ref, pl.program_id(1), scale)
        o = jnp.dot(p.astype(BF16), v_ref[pl.ds(off, WIN), :], preferred_element_type=F32)
        o_ref[...] = (o / l).astype(BF16)

    q, kv, bs, tile, _ = _chunk_specs(s, nh)
    return _pcall(body, (u, kpad, vpad, bias), name=name, grid=(nh, s // QBLK), in_specs=[q, kv, kv, bs],
                  out_specs=[tile], out_shape=[jax.ShapeDtypeStruct((s, bw), BF16)], sem=("parallel", "parallel"),
                  side=side)[0]


def _chunk_bwd(name, u, kpad, vpad, bias, do, bw):
    s, nh = u.shape[0], bw // HEAD
    nq = s // QBLK
    scale = HEAD ** -0.5

    def body(q_ref, k_ref, v_ref, b_ref, do_ref, dq_ref, dk_ref, dv_ref, dss_ref, dk_s, dv_s):
        i = pl.program_id(1)

        @pl.when(i == 0)
        def _():
            dk_s[...] = jnp.zeros(dk_s.shape, F32)
            dv_s[...] = jnp.zeros(dv_s.shape, F32)
            dss_ref[...] = jnp.zeros(dss_ref.shape, F32)

        q, dov = q_ref[...], do_ref[...]
        p, l, kw, off = _chunk_probs(q, k_ref, b_ref, i, scale)
        p = p / l
        dp = lax.dot_general(dov, v_ref[pl.ds(off, WIN), :], NT, preferred_element_type=F32)
        ds = p * (dp - jnp.sum(p * dp, axis=1, keepdims=True))
        dsb = ds.astype(BF16)
        dq_ref[...] = (jnp.dot(dsb, kw, preferred_element_type=F32) * scale).astype(BF16)
        dk_s[pl.ds(off, WIN), :] += lax.dot_general(dsb, q, TN, preferred_element_type=F32)
        dv_s[pl.ds(off, WIN), :] += lax.dot_general(p.astype(BF16), dov, TN, preferred_element_type=F32)
        dss_ref[...] += ds

        @pl.when(i == nq - 1)
        def _():
            dk_ref[...] = (dk_s[pl.ds(PADK, s), :] * scale).astype(BF16)
            dv_ref[...] = dv_s[pl.ds(PADK, s), :].astype(BF16)

    q, kv, bs, tile, full = _chunk_specs(s, nh)
    return pl.pallas_call(
        body, name=name, grid=(nh, nq), in_specs=[q, kv, kv, bs, tile], out_specs=[tile, full, full, bs],
        out_shape=[jax.ShapeDtypeStruct((s, bw), BF16)] * 3 + [jax.ShapeDtypeStruct((nh, QBLK, WIN), F32)],
        scratch_shapes=[pltpu.VMEM((s + PADK, HEAD), F32), pltpu.VMEM((s + PADK, HEAD), F32)],
        compiler_params=_cp(("arbitrary", "arbitrary")))(u, kpad, vpad, bias, do)


def _gelu_parts(y):
    th = jnp.tanh(GELU_K * (y + GELU_C * y * y * y))
    return 0.5 * y * (1.0 + th), th


def _block_diag(xb16, w_ref, nh, dims):
    return jnp.concatenate(
        [lax.dot_general(xb16[:, n * HEAD:(n + 1) * HEAD], w_ref[n], dims, preferred_element_type=F32)
         for n in range(nh)], axis=1)


def _lru_gates(ext, cw_ref, cb_ref, wr_ref, br_ref, wi_ref, bi_ref, lam_ref, ts, nh):
    shifted = [pltpu.roll(ext, CONV_WIDTH - 1 - j, 0)[8:, :] if j < CONV_WIDTH - 1 else ext[8:, :]
               for j in range(CONV_WIDTH)]
    xc = cb_ref[...]
    for j in range(CONV_WIDTH):
        xc = xc + shifted[j] * cw_ref[pl.ds(j, 1), :]
    xcb = xc.astype(BF16)
    r = _sigmoid(_block_diag(xcb, wr_ref, nh, NN) + br_ref[...])
    gi = _sigmoid(_block_diag(xcb, wi_ref, nh, NN) + bi_ref[...])
    lsl = _log_sigmoid(lam_ref[...])
    la = LRU_C * r * lsl
    a = jnp.exp(la)
    e2 = jnp.exp(2.0 * la)
    mult = jnp.sqrt(-jnp.tanh(la) * (e2 + 1.0))
    return shifted, xc, xcb, r, gi, lsl, a, e2, mult


def _lru_param_specs(bw, nh):
    vec = pl.BlockSpec((1, bw), lambda i: (0, 0))
    conv = pl.BlockSpec((8, bw), lambda i: (0, 0))
    blocks = pl.BlockSpec((nh, HEAD, HEAD), lambda i: (0, 0, 0))
    return [conv, vec, blocks, vec, blocks, vec, vec]


def _lru_fwd(name, u, params, bw):
    s, nh = u.shape[0], bw // HEAD
    ts = min(512, s)

    def body(rx_ref, ry_ref, cw_ref, cb_ref, wr_ref, br_ref, wi_ref, bi_ref, lam_ref, o_ref, h_ref, tail, hcar):
        @pl.when(pl.program_id(0) == 0)
        def _():
            tail[...] = jnp.zeros(tail.shape, F32)
            hcar[...] = jnp.zeros(hcar.shape, F32)

        rx = rx_ref[...].astype(F32)
        ext = jnp.concatenate([tail[...], rx], axis=0)
        tail[...] = rx[ts - 8:, :]
        _, xc, _, _, gi, _, a, _, mult = _lru_gates(ext, cw_ref, cb_ref, wr_ref, br_ref, wi_ref, bi_ref, lam_ref, ts, nh)
        acum, bcum = _scan_affine(a, mult * (gi * xc), False)
        h_ref[...] = bcum + acum * hcar[...]
        hcar[...] = h_ref[pl.ds(ts - 1, 1), :]
        o_ref[...] = (h_ref[...] * _gelu_parts(ry_ref[...].astype(F32))[0]).astype(BF16)

    row = pl.BlockSpec((ts, bw), lambda i: (i, 0))
    return pl.pallas_call(
        body, name=name, grid=(s // ts,),
        in_specs=[pl.BlockSpec((ts, bw), lambda i: (i, RX)), pl.BlockSpec((ts, bw), lambda i: (i, RY))]
        + _lru_param_specs(bw, nh),
        out_specs=[row, row],
        out_shape=[jax.ShapeDtypeStruct((s, bw), BF16), jax.ShapeDtypeStruct((s, bw), F32)],
        scratch_shapes=[pltpu.VMEM((8, bw), F32), pltpu.VMEM((1, bw), F32)],
        compiler_params=_cp(("arbitrary",)))(u, u, *params)


def _lru_bwd(name, u, h, do, params, bw):
    s, nh = u.shape[0], bw // HEAD
    ts = min(512, s)
    nb = s // ts
    t8 = ts // 8

    def body(rx_ref, rxp_ref, ry_ref, h_ref, hp_ref, do_ref, cw_ref, cb_ref, wr_ref, br_ref, wi_ref, bi_ref, lam_ref,
             drx_ref, dry_ref, dcw_ref, dcb_ref, dwr_ref, dbr_ref, dwi_ref, dbi_ref, dlam_ref, gcar, head):
        i = pl.program_id(0)
        first = i == nb - 1

        @pl.when(i == 0)
        def _():
            gcar[...] = jnp.zeros(gcar.shape, F32)
            head[...] = jnp.zeros(head.shape, F32)
            for ref in (dcw_ref, dcb_ref, dwr_ref, dbr_ref, dwi_ref, dbi_ref, dlam_ref):
                ref[...] = jnp.zeros(ref.shape, F32)

        rows = _iota((ts, 1), 0)
        rx = rx_ref[...].astype(F32)
        before = jnp.where(first, 0.0, rxp_ref[...].astype(F32))
        ext = jnp.concatenate([before, rx], axis=0)
        shifted, xc, xcb, r, gi, lsl, a, e2, mult = _lru_gates(
            ext, cw_ref, cb_ref, wr_ref, br_ref, wi_ref, bi_ref, lam_ref, ts, nh)

        ry = ry_ref[...].astype(F32)
        gel, th = _gelu_parts(ry)
        dgel = 0.5 * (1.0 + th) + 0.5 * ry * (1.0 - th * th) * GELU_K * (1.0 + 3.0 * GELU_C * ry * ry)
        dov = do_ref[...].astype(F32)
        hv = h_ref[...]
        dry_ref[...] = (dov * hv * dgel).astype(BF16)

        coef = jnp.where(rows < ts - 1, pltpu.roll(a, ts - 1, 0), 0.0)
        dh_in = dov * gel + jnp.where(rows == ts - 1, gcar[...], 0.0)
        dh = _scan_affine(coef, dh_in, True)[1]
        gcar[...] = jnp.sum(jnp.where(rows == 0, a * dh, 0.0), axis=0, keepdims=True)

        hprev = jnp.where(first, 0.0, hp_ref[...])
        hm1 = pltpu.roll(jnp.concatenate([hprev, hv], axis=0), 1, 0)[8:, :]
        dgx = dh * mult
        dla = dh * hm1 * a - dh * gi * xc * (e2 / mult)
        dpre_r = dla * (LRU_C * lsl) * r * (1.0 - r)
        dpre_i = dgx * xc * gi * (1.0 - gi)
        dlam_ref[...] += jnp.sum(dla * r, axis=0, keepdims=True) * (LRU_C * _sigmoid(-lam_ref[...]))
        dbr_ref[...] += jnp.sum(dpre_r, axis=0, keepdims=True)
        dbi_ref[...] += jnp.sum(dpre_i, axis=0, keepdims=True)
        drb, dib = dpre_r.astype(BF16), dpre_i.astype(BF16)
        for n in range(nh):
            cols = slice(n * HEAD, (n + 1) * HEAD)
            dwr_ref[n] += lax.dot_general(xcb[:, cols], drb[:, cols], TN, preferred_element_type=F32)
            dwi_ref[n] += lax.dot_general(xcb[:, cols], dib[:, cols], TN, preferred_element_type=F32)
        dxc = dgx * gi + _block_diag(drb, wr_ref, nh, NT) + _block_diag(dib, wi_ref, nh, NT)

        dcb_ref[...] += jnp.sum(dxc, axis=0, keepdims=True)
        for j in range(CONV_WIDTH):
            dcw_ref[pl.ds(j, 1), :] += jnp.sum(dxc * shifted[j], axis=0, keepdims=True)
        ext2 = jnp.concatenate([dxc, head[...]], axis=0)
        head[...] = dxc[:8, :]
        drx = dxc * cw_ref[pl.ds(CONV_WIDTH - 1, 1), :]
        for j in range(CONV_WIDTH - 1):
            up = CONV_WIDTH - 1 - j
            drx = drx + pltpu.roll(ext2, ts + 8 - up, 0)[:ts, :] * cw_ref[pl.ds(j, 1), :]
        drx_ref[...] = drx.astype(BF16)

    def blk(col):
        return lambda i: (nb - 1 - i, col)

    def prev8(col):
        return lambda i: (jnp.maximum((nb - 1 - i) * t8 - 1, 0), col)

    vec = pl.BlockSpec((1, bw), lambda i: (0, 0))
    conv = pl.BlockSpec((8, bw), lambda i: (0, 0))
    blocks = pl.BlockSpec((nh, HEAD, HEAD), lambda i: (0, 0, 0))
    return pl.pallas_call(
        body, name=name, grid=(nb,),
        in_specs=[pl.BlockSpec((ts, bw), blk(RX)), pl.BlockSpec((8, bw), prev8(RX)), pl.BlockSpec((ts, bw), blk(RY)),
                  pl.BlockSpec((ts, bw), blk(0)), pl.BlockSpec((8, bw), prev8(0)), pl.BlockSpec((ts, bw), blk(0))]
        + _lru_param_specs(bw, nh),
        out_specs=[pl.BlockSpec((ts, bw), blk(0)), pl.BlockSpec((ts, bw), blk(0)), conv, vec, blocks, vec, blocks, vec, vec],
        out_shape=[jax.ShapeDtypeStruct((s, bw), BF16)] * 2
        + [jax.ShapeDtypeStruct((8, bw), F32), jax.ShapeDtypeStruct((1, bw), F32),
           jax.ShapeDtypeStruct((nh, HEAD, HEAD), F32), jax.ShapeDtypeStruct((1, bw), F32),
           jax.ShapeDtypeStruct((nh, HEAD, HEAD), F32), jax.ShapeDtypeStruct((1, bw), F32),
           jax.ShapeDtypeStruct((1, bw), F32)],
        scratch_shapes=[pltpu.VMEM((1, bw), F32), pltpu.VMEM((8, bw), F32)],
        compiler_params=_cp(("arbitrary",)))(u, u, u, h, h, do, *params)


def _gate_merge(name, xb, w_gate, b_gate, o_all, w_branch, l, side=None):
    s, d = xb.shape
    bw = o_all.shape[2]
    tm, tn = min(512, s), min(256, d)

    def body(x_ref, wg_ref, bg_ref, o_ref, wb_ref, m_ref, g_ref, p_ref):
        x = x_ref[...]
        acc = jnp.zeros((tm, tn), F32)
        for g in range(4):
            gate = _sigmoid(jnp.dot(x, wg_ref[g], preferred_element_type=F32) + bg_ref[g])
            proj = jnp.dot(o_ref[g], wb_ref[g], preferred_element_type=F32)
            term = gate * proj
            g_ref[g] = gate.astype(BF16)
            p_ref[g] = (term * (1.0 - gate)).astype(BF16)
            acc = acc + term
        m_ref[...] = acc.astype(BF16)

    quad = pl.BlockSpec((4, tm, tn), lambda n, m: (0, m, n))
    return _pcall(
        body, (xb, w_gate, b_gate, o_all, w_branch), name=name, grid=(d // tn, s // tm),
        in_specs=[pl.BlockSpec((tm, d), lambda n, m: (m, 0)),
                  pl.BlockSpec((None, 4, d, tn), lambda n, m: (0, 0, 0, n)),
                  pl.BlockSpec((None, 4, 1, tn), lambda n, m: (l, 0, 0, n)),
                  pl.BlockSpec((4, tm, bw), lambda n, m: (0, m, 0)),
                  pl.BlockSpec((None, 4, bw, tn), lambda n, m: (0, 0, 0, n))],
        out_specs=[pl.BlockSpec((tm, tn), lambda n, m: (m, n)), quad, quad],
        out_shape=[jax.ShapeDtypeStruct((s, d), BF16), jax.ShapeDtypeStruct((4, s, d), BF16),
                   jax.ShapeDtypeStruct((4, s, d), BF16)],
        sem=("parallel", "parallel"), side=side)


def _adamw(name, w, m, v, parts, layer=0, layers=1, earlier=None):
    cols = w.shape[1]
    p, rows = parts.shape[0], parts.shape[1]
    tr = _pow2_rows(rows, cols * max(1, p // 2))
    nb = rows // tr
    c1 = 1.0 - ADAM_B1 ** ADAM_STEP
    c2 = 1.0 - ADAM_B2 ** ADAM_STEP

    def body(w_ref, m_ref, v_ref, g_ref, *rest):
        go_ref, do_ref, mo_ref, vo_ref = rest[-4:]
        g = g_ref[0].astype(F32)
        for k in range(1, p):
            g = g + g_ref[k].astype(F32)
        m2 = ADAM_B1 * m_ref[...] + (1.0 - ADAM_B1) * g
        v2 = ADAM_B2 * v_ref[...] + (1.0 - ADAM_B2) * (g * g)
        go_ref[...] = g
        do_ref[...] = -ADAM_LR * ((m2 / c1) / (jnp.sqrt(v2 / c2) + ADAM_EPS) + ADAM_WD * w_ref[...])
        mo_ref[...] = m2
        vo_ref[...] = v2

    row = pl.BlockSpec((tr, cols), lambda i: (layer * nb + i, 0))
    held = list(earlier) if earlier is not None else []
    return pl.pallas_call(
        body, name=name, grid=(nb,),
        in_specs=[row, row, row, pl.BlockSpec((p, tr, cols), lambda i: (0, i, 0))] + [ANY] * len(held),
        out_specs=[row] * 4, out_shape=[jax.ShapeDtypeStruct((layers * rows, cols), F32)] * 4,
        input_output_aliases={4 + k: k for k in range(len(held))},
        compiler_params=_cp(("parallel",)))(w, m, v, parts, *held)


PACK_ROWS = 512


def _pack(arrays):
    rows = []
    for a in arrays:
        flat = a.astype(F32).reshape(-1)
        rows.append(jnp.pad(flat, (0, (-flat.shape[0]) % LANE)).reshape(-1, LANE))
    rows = jnp.concatenate(rows)
    return jnp.pad(rows, ((0, (-rows.shape[0]) % PACK_ROWS), (0, 0)))


def _unpack(packed, shapes):
    out, row = [], 0
    for shp in shapes:
        n = math.prod(shp)
        nrows = -(-n // LANE)
        out.append(packed[row:row + nrows].reshape(-1)[:n].reshape(shp))
        row += nrows
    return out


def _unshard(gathered, axis):
    block = gathered.shape[2:]
    full = jnp.swapaxes(gathered, 0, 1).reshape((N_DEV,) + block)
    full = jnp.moveaxis(full, 0, axis)
    return full.reshape(block[:axis] + (N_DEV * block[axis],) + block[axis + 1:])


def kernel(x, ln_in_g, ln_in_b, w_in, b_forget, conv_w, conv_b, w_r, b_r, w_i, b_i, lru_lambda, rel_bias, w_branch, w_gate, b_gate, w_out, ln1_g, ln1_b, w_ff1, w_ff2, ln2_g, ln2_b, loss_target, m_ln_in_g, m_ln_in_b, m_w_in, m_b_forget, m_conv_w, m_conv_b, m_w_r, m_b_r, m_w_i, m_b_i, m_lru_lambda, m_rel_bias, m_w_branch, m_w_gate, m_b_gate, m_w_out, m_ln1_g, m_ln1_b, m_w_ff1, m_w_ff2, m_ln2_g, m_ln2_b, v_ln_in_g, v_ln_in_b, v_w_in, v_b_forget, v_conv_w, v_conv_b, v_w_r, v_b_r, v_w_i, v_b_i, v_lru_lambda, v_rel_bias, v_w_branch, v_w_gate, v_b_gate, v_w_out, v_ln1_g, v_ln1_b, v_w_ff1, v_w_ff2, v_ln2_g, v_ln2_b):
    given = dict(zip(
        NAMES + ['loss_target'] + ['m_' + n for n in WEIGHTS] + ['v_' + n for n in WEIGHTS],
        (x, ln_in_g, ln_in_b, w_in, b_forget, conv_w, conv_b, w_r, b_r, w_i, b_i, lru_lambda, rel_bias, w_branch, w_gate, b_gate, w_out, ln1_g, ln1_b, w_ff1, w_ff2, ln2_g, ln2_b, loss_target, m_ln_in_g, m_ln_in_b, m_w_in, m_b_forget, m_conv_w, m_conv_b, m_w_r, m_b_r, m_w_i, m_b_i, m_lru_lambda, m_rel_bias, m_w_branch, m_w_gate, m_b_gate, m_w_out, m_ln1_g, m_ln1_b, m_w_ff1, m_w_ff2, m_ln2_g, m_ln2_b, v_ln_in_g, v_ln_in_b, v_w_in, v_b_forget, v_conv_w, v_conv_b, v_w_r, v_b_r, v_w_i, v_b_i, v_lru_lambda, v_rel_bias, v_w_branch, v_w_gate, v_b_gate, v_w_out, v_ln1_g, v_ln1_b, v_w_ff1, v_w_ff2, v_ln2_g, v_ln2_b)))

    s, d = x.shape[1], x.shape[2]
    nl = w_in.shape[0]
    bw = d // 4
    nh = bw // HEAD
    nu = 11 * bw
    rs = d // N_DEV
    dff = w_ff1.shape[2] * N_DEV
    fs = dff // N_DEV
    cs = d // N_DEV
    assert nl == DEPTH and nh * HEAD == bw and s % 256 == 0 and d % 1024 == 0

    xi, yi, ci = _position()
    dev = 4 * xi + 2 * yi + ci
    c_arr = jnp.reshape(ci, (1,)).astype(I32)

    w_main = jnp.concatenate(
        [w_in[..., :3 * bw], w_in[..., 3 * bw + nh:],
         jnp.pad(w_in[..., 3 * bw:3 * bw + nh], ((0, 0), (0, 0), (0, LANE - nh)))], axis=-1).astype(BF16)
    nue = nu + LANE
    small_shapes = [conv_w.shape, rel_bias.shape, b_gate.shape]
    shard = {'main': w_main, 'branch': w_branch.astype(BF16), 'gate': w_gate.astype(BF16),
             'out': w_out.astype(BF16), 'ff1': w_ff1.astype(BF16), 'ff2': w_ff2.astype(BF16)}
    shard_axis = {'main': 1, 'branch': 3, 'gate': 2, 'out': 1, 'ff1': 2, 'ff2': 1}
    W = [dict() for _ in range(nl)]

    def gather_chips(l, keys, extra=()):
        side = _chips_side([shard[k][l:l + 1] for k in keys] + list(extra), True)
        side.todo = (l, keys)
        return side

    def gather_cores(chips):
        side = _cores_side(chips.results, True)
        side.todo = chips.todo
        return side

    def arrived(cores):
        l, keys = cores.todo
        for k, res in zip(keys, cores.results):
            W[l][k] = _unshard(res, shard_axis[k])
        return cores.results[len(keys):]

    xs = x[0]
    first = gather_chips(0, ['main'], [_pack([conv_w, rel_bias, b_gate])])
    h0, h0b = _ln_fwd("ln_in", xs, ln_in_g, ln_in_b, side=first)
    first = gather_cores(first)
    _run_side("gather_cores_first", first)
    small = arrived(first)[0]
    small = jnp.swapaxes(small, 0, 1).reshape((N_DEV,) + small.shape[2:])
    small = [_unpack(small[j], small_shapes) for j in range(N_DEV)]
    conv_w_full = jnp.concatenate([small[j][0] for j in range(N_DEV)], axis=-1)
    rel_bias_full = jnp.concatenate([small[j][1] for j in range(N_DEV)], axis=-1)
    b_gate_full = jnp.concatenate([small[j][2] for j in range(N_DEV)], axis=-1)
    b_gate4 = b_gate_full.reshape(nl, 4, 1, d)

    def lru_params(l):
        return (jnp.pad(conv_w_full[l], ((0, 8 - CONV_WIDTH), (0, 0))), conv_b[l].reshape(1, bw),
                w_r[l].astype(BF16), b_r[l].reshape(1, bw), w_i[l].astype(BF16), b_i[l].reshape(1, bw),
                lru_lambda[l].reshape(1, bw))

    def bias_rows(l):
        return jnp.pad(rel_bias_full[l], ((0, 16 - nh), (0, REL_PAD - REL_TABLE)))

    tm = min(1024, s)
    tkk = min(2048, d)

    saved = []
    cur, curb = h0, h0b
    chips = {}
    for l in range(nl):
        side = None
        if l == 0:
            side = chips['b0'] = gather_chips(0, ['gate', 'branch'])
        else:
            side = last_cores = gather_cores(chips.pop('d1'))
        u = _mm(f"w_in_{l}", curb, W[l]['main'], grid=(s // tm, nu // bw, d // tkk),
                a_spec=pl.BlockSpec((tm, tkk), lambda m, n, k: (m, k)),
                b_spec=pl.BlockSpec((None, tkk, bw), lambda m, n, k: (0, k, n)),
                dims=NN, acc_shape=(tm, bw), out_shape=[jax.ShapeDtypeStruct((s, nu), BF16)],
                out_specs=[pl.BlockSpec((tm, bw), lambda m, n, k: (m, n))], finish=_store(BF16), side=side)[0]
        if l == 1:
            arrived(last_cores)
        fl = _mm(f"w_forget_{l}", curb, W[l]['main'], grid=(s // tm, 1, d // tkk),
                 a_spec=pl.BlockSpec((tm, tkk), lambda m, n, k: (m, k)),
                 b_spec=pl.BlockSpec((None, tkk, LANE), lambda m, n, k: (0, k, nu // LANE)),
                 dims=NN, acc_shape=(tm, LANE), out_shape=[jax.ShapeDtypeStruct((s, LANE), F32)],
                 out_specs=[pl.BlockSpec((tm, LANE), lambda m, n, k: (m, 0))], finish=_store(F32))[0]
        bf_row = jnp.pad(b_forget[l], (0, LANE - nh)).reshape(1, LANE)
        cf = _cum_forget_fwd(f"cum_forget_{l}", fl, bf_row)
        tkf = _fox_tile(s)
        cf_heads = cf[:, :nh].T
        cf_col = cf_heads.reshape(nh, s, 1)
        cf_row = cf_heads.reshape(nh, s // tkf, 1, tkf)
        side = None
        if l == 0:
            b0 = gather_cores(chips.pop('b0'))
            chips['c0'] = gather_chips(0, ['out', 'ff1', 'ff2'])
            side = _merge_sides(b0, chips['c0'])
        o_fox, lse = _fox_fwd(f"fox_fwd_{l}", u, cf_col, cf_row, bw, side=side)
        if l == 0:
            arrived(b0)
        lp = lru_params(l)
        o_lru, hstate = _lru_fwd(f"lru_fwd_{l}", u, lp, bw)
        o_sb = _sb_fwd(f"sb_fwd_{l}", u, bw)
        bias = _bias_expand(f"bias_expand_{l}", bias_rows(l), nh)
        kpad = jnp.pad(u[:, CK * bw:(CK + 1) * bw], ((PADK, 0), (0, 0)))
        vpad = jnp.pad(u[:, CV * bw:(CV + 1) * bw], ((PADK, 0), (0, 0)))
        side = gather_cores(chips.pop('c0')) if l == 0 else None
        o_ch = _chunk_fwd(f"chunk_fwd_{l}", u, kpad, vpad, bias, bw, side=side)
        if l == 0:
            arrived(side)
        o_all = jnp.stack([o_fox, o_lru, o_sb, o_ch])
        side = None
        if l == 0:
            side = chips['a1'] = gather_chips(1, ['main', 'gate', 'branch'])
        merged, gates, projs = _gate_merge(f"gate_merge_{l}", curb, W[l]['gate'], b_gate4, o_all, W[l]['branch'], l,
                                           side=side)
        side = gather_cores(chips.pop('a1')) if l == 0 else None
        z1, x1, x1b = _mm_ln(f"w_out_ln1_{l}", merged, W[l]['out'], 0, cur, ln1_g[l], ln1_b[l], side=side)
        if l == 0:
            arrived(side)
        tn1 = min(1024, dff)

        def ff1_finish(acc, ex, outs, ids):
            outs[0][...] = acc.astype(BF16)
            r = jnp.maximum(acc, 0.0)
            outs[1][...] = (r * r).astype(BF16)

        side = None
        if l == 0:
            side = chips['c1'] = gather_chips(1, ['out', 'ff1'])
        hp, hid = _mm(f"w_ff1_{l}", x1b, W[l]['ff1'], grid=(s // tm, dff // tn1, d // tkk),
                      a_spec=pl.BlockSpec((tm, tkk), lambda m, n, k: (m, k)),
                      b_spec=pl.BlockSpec((None, tkk, tn1), lambda m, n, k: (0, k, n)),
                      dims=NN, acc_shape=(tm, tn1),
                      out_shape=[jax.ShapeDtypeStruct((s, dff), BF16)] * 2,
                      out_specs=[pl.BlockSpec((tm, tn1), lambda m, n, k: (m, n))] * 2, finish=ff1_finish, side=side)
        side = None
        if l == 0:
            c1 = gather_cores(chips.pop('c1'))
            chips['d1'] = gather_chips(1, ['ff2'])
            side = _merge_sides(c1, chips['d1'])
        z2, x2, x2b = _mm_ln(f"w_ff2_ln2_{l}", hid, W[l]['ff2'], 0, x1, ln2_g[l], ln2_b[l], side=side)
        if l == 0:
            arrived(c1)
        saved.append(dict(xin=cur, xinb=curb, u=u, fl=fl, bf_row=bf_row, cf_col=cf_col, cf_row=cf_row, o_fox=o_fox,
                          lse=lse, lp=lp, hstate=hstate, bias=bias, kpad=kpad, vpad=vpad, o_all=o_all, merged=merged,
                          gates=gates, projs=projs, z1=z1, x1=x1, x1b=x1b, hp=hp, hid=hid, z2=z2))
        cur, curb = x2, x2b

    loss_tile, *last_ln = _loss_ln_bwd("loss_ln2_bwd", cur, loss_target[0], saved[nl - 1]['z2'], ln2_g[nl - 1])
    loss = lax.psum(loss_tile[0, 0], ("x", "y", "c"))
    dcur = None

    big = [dict() for _ in range(nl)]
    reduced = [dict() for _ in range(nl)]
    sm = {n: [None] * nl for n in ['b_forget', 'conv_w', 'conv_b', 'w_r', 'b_r', 'w_i', 'b_i', 'lru_lambda', 'rel_bias',
                                   'b_gate', 'ln1_g', 'ln1_b', 'ln2_g', 'ln2_b']}

    def split_columns(acc, ex, outs, ids):
        for j in range(N_DEV):
            outs[0][j] = acc[:, j * cs:(j + 1) * cs].astype(BF16)

    def grad_mm(key, name, a, b, *, shape, grid, a_spec, b_spec, out_spec, acc_shape, finish=_store(BF16)):
        l = int(name[-1])
        big[l][key] = _mm(name, a, b, grid=grid, a_spec=a_spec, b_spec=b_spec, dims=TN, acc_shape=acc_shape,
                          out_shape=[jax.ShapeDtypeStruct(shape, BF16)], out_specs=[out_spec], finish=finish)[0]

    def reduce_cores(l, keys):
        side = _cores_side([big[l][k].reshape((4, 2) + big[l][k].shape[1:]) for k in keys], False)
        side.todo = (l, keys)
        return side

    def reduce_chips(cores):
        l, keys = cores.todo
        partial = []
        for k, mine, other in zip(keys, cores.operands, cores.results):
            cols = mine.shape[-1]
            rows = math.prod(mine.shape[2:]) // cols
            partial.append(_add_core_halves(f"add_cores_{k}_{l}", mine.reshape(4, 2, rows, cols),
                                            other.reshape(4, rows, cols), c_arr, BF16))
        side = _chips_side(partial, False)
        side.todo = (l, keys)
        return side

    def reduction_done(chips_side):
        l, keys = chips_side.todo
        reduced[l].update(zip(keys, chips_side.results))

    GROUP1, GROUP2 = ['w_ff2', 'w_ff1', 'w_out', 'w_branch', 'w_gate'], ['w_main']
    pending = None

    tks = min(1024, s)
    tmr = min(1024, d)
    nsh = tmr // rs

    for l in reversed(range(nl)):
        sv = saved[l]
        Wl = W[l]
        dz2, dz2b, dg, db = last_ln if l == nl - 1 else _ln_bwd(f"ln2_bwd_{l}", dcur, sv['z2'], ln2_g[l])
        sm['ln2_g'][l], sm['ln2_b'][l] = dg[0], db[0]
        tn1 = min(1024, dff)

        def dhp_finish(acc, ex, outs, ids):
            outs[0][...] = (acc * (2.0 * jnp.maximum(ex[0][...].astype(F32), 0.0))).astype(BF16)

        dhp = _mm(f"d_hidden_{l}", dz2b, Wl['ff2'], grid=(s // tm, dff // tn1, d // tkk),
                  a_spec=pl.BlockSpec((tm, tkk), lambda m, n, k: (m, k)),
                  b_spec=pl.BlockSpec((None, tn1, tkk), lambda m, n, k: (0, n, k)),
                  dims=NT, acc_shape=(tm, tn1), out_shape=[jax.ShapeDtypeStruct((s, dff), BF16)],
                  out_specs=[pl.BlockSpec((tm, tn1), lambda m, n, k: (m, n))], finish=dhp_finish,
                  extras=(sv['hp'],), extra_specs=(pl.BlockSpec((tm, tn1), lambda m, n, k: (m, n)),),
                  side=pending)[0]
        if pending is not None:
            reduction_done(pending)
            pending = None
        grad_mm('w_ff2', f"g_w_ff2_{l}", sv['hid'], dz2b, shape=(N_DEV, 1, fs, d), grid=(N_DEV, 1, s // tks),
                a_spec=pl.BlockSpec((tks, fs), lambda m, n, k: (k, m)),
                b_spec=pl.BlockSpec((tks, d), lambda m, n, k: (k, 0)),
                out_spec=pl.BlockSpec((None, None, fs, d), lambda m, n, k: (m, 0, 0, 0)), acc_shape=(fs, d))
        grad_mm('w_ff1', f"g_w_ff1_{l}", sv['x1b'], dhp, shape=(N_DEV, 1, d, fs), grid=(d // tmr, N_DEV, s // tks),
                a_spec=pl.BlockSpec((tks, tmr), lambda m, n, k: (k, m)),
                b_spec=pl.BlockSpec((tks, fs), lambda m, n, k: (k, n)),
                out_spec=pl.BlockSpec((None, None, tmr, fs), lambda m, n, k: (n, 0, m, 0)), acc_shape=(tmr, fs))
        tnd = min(1024, d)

        def resid_finish(scale):
            def finish(acc, ex, outs, ids):
                outs[0][...] = acc + scale * ex[0][...]
            return finish

        tile_md = pl.BlockSpec((tm, tnd), lambda m, n, k: (m, n))
        dx1 = _mm(f"d_x1_{l}", dhp, Wl['ff1'], grid=(s // tm, d // tnd, dff // tkk),
                  a_spec=pl.BlockSpec((tm, tkk), lambda m, n, k: (m, k)),
                  b_spec=pl.BlockSpec((None, tnd, tkk), lambda m, n, k: (0, n, k)),
                  dims=NT, acc_shape=(tm, tnd), out_shape=[jax.ShapeDtypeStruct((s, d), F32)],
                  out_specs=[tile_md], finish=resid_finish(ALPHA), extras=(dz2,), extra_specs=(tile_md,))[0]

        dz1, dz1b, dg, db = _ln_bwd(f"ln1_bwd_{l}", dx1, sv['z1'], ln1_g[l])
        sm['ln1_g'][l], sm['ln1_b'][l] = dg[0], db[0]
        tmg, tng = min(512, s), min(512, d)

        def gate_finish(acc, ex, outs, ids):
            @pl.when(ids[1] == 0)
            def _():
                outs[2][...] = jnp.zeros(outs[2].shape, F32)

            ones = jnp.ones((8, tmg), BF16)
            for g in range(4):
                dpre = (acc * ex[1][g].astype(F32)).astype(BF16)
                outs[0][g] = (acc * ex[0][g].astype(F32)).astype(BF16)
                outs[1][g] = dpre
                outs[2][g] += jnp.dot(ones, dpre, preferred_element_type=F32)

        quad = pl.BlockSpec((4, tmg, tng), lambda n, m, k: (0, m, n))
        dproj, dpre, dbg = _mm(
            f"d_merged_{l}", dz1b, Wl['out'], grid=(d // tng, s // tmg, d // tkk),
            a_spec=pl.BlockSpec((tmg, tkk), lambda n, m, k: (m, k)),
            b_spec=pl.BlockSpec((None, tng, tkk), lambda n, m, k: (0, n, k)),
            dims=NT, acc_shape=(tmg, tng),
            out_shape=[jax.ShapeDtypeStruct((4, s, d), BF16), jax.ShapeDtypeStruct((4, s, d), BF16),
                       jax.ShapeDtypeStruct((4, 8, d), F32)],
            out_specs=[quad, quad, pl.BlockSpec((4, 8, tng), lambda n, m, k: (0, 0, n))], finish=gate_finish,
            extras=(sv['gates'], sv['projs']), extra_specs=(quad, quad), sem=("arbitrary", "arbitrary", "arbitrary"))
        sm['b_gate'][l] = dbg[:, 0, :]
        grad_mm('w_out', f"g_w_out_{l}", sv['merged'], dz1b, shape=(N_DEV, 1, rs, d), grid=(d // tmr, 1, s // tks),
                a_spec=pl.BlockSpec((tks, tmr), lambda m, n, k: (k, m)),
                b_spec=pl.BlockSpec((tks, d), lambda m, n, k: (k, 0)),
                out_spec=pl.BlockSpec((nsh, None, rs, d), lambda m, n, k: (m, 0, 0, 0)), acc_shape=(tmr, d))

        nm = s // tm
        do_all = _mm(f"d_branch_{l}", dproj, Wl['branch'], grid=(4 * nm, 1, d // tkk),
                     a_spec=pl.BlockSpec((None, tm, tkk), lambda m, n, k: (m // nm, m % nm, k)),
                     b_spec=pl.BlockSpec((None, None, bw, tkk), lambda m, n, k: (0, m // nm, 0, k)),
                     dims=NT, acc_shape=(tm, bw), out_shape=[jax.ShapeDtypeStruct((4, s, bw), BF16)],
                     out_specs=[pl.BlockSpec((None, tm, bw), lambda m, n, k: (m // nm, m % nm, 0))],
                     finish=_store(BF16))[0]
        grad_mm('w_branch', f"g_w_branch_{l}", sv['o_all'], dproj, shape=(N_DEV, 1, 4, bw, cs),
                grid=(4, 1, s // tks), finish=split_columns,
                a_spec=pl.BlockSpec((None, tks, bw), lambda m, n, k: (m, k, 0)),
                b_spec=pl.BlockSpec((None, tks, d), lambda m, n, k: (m, k, 0)),
                out_spec=pl.BlockSpec((N_DEV, None, None, bw, cs), lambda m, n, k: (0, 0, m, 0, 0)),
                acc_shape=(bw, d))
        grad_mm('w_gate', f"g_w_gate_{l}", sv['xinb'], dpre, shape=(N_DEV, 1, 4, rs, d), grid=(d // tmr, 4, s // tks),
                a_spec=pl.BlockSpec((tks, tmr), lambda m, n, k: (k, m)),
                b_spec=pl.BlockSpec((None, tks, d), lambda m, n, k: (n, k, 0)),
                out_spec=pl.BlockSpec((nsh, None, None, rs, d), lambda m, n, k: (m, 0, n, 0, 0)),
                acc_shape=(tmr, d))
        nkg = d // tkk
        cores1 = reduce_cores(l, GROUP1)
        dx_gate = _mm(f"d_x_gates_{l}", dpre, Wl['gate'], grid=(s // tm, d // tnd, 4 * nkg),
                      a_spec=pl.BlockSpec((None, tm, tkk), lambda m, n, k: (k // nkg, m, k % nkg)),
                      b_spec=pl.BlockSpec((None, None, tnd, tkk), lambda m, n, k: (0, k // nkg, n, k % nkg)),
                      dims=NT, acc_shape=(tm, tnd), out_shape=[jax.ShapeDtypeStruct((s, d), F32)],
                      out_specs=[tile_md], finish=resid_finish(ALPHA), extras=(dz1,), extra_specs=(tile_md,),
                      side=cores1)[0]

        u = sv['u']
        chips1 = reduce_chips(cores1)
        dfq, dfk, dfv, dcc, dcr = _fox_bwd(f"fox_bwd_{l}", u, sv['cf_col'], sv['cf_row'], sv['o_fox'], do_all[0],
                                          sv['lse'], bw, side=chips1)
        reduction_done(chips1)
        dcf = (dcc[:, :, 0, :].reshape(nh, s) + dcr.reshape(nh, s)).T
        dflb, dbf = _cum_forget_bwd(f"cum_forget_bwd_{l}", jnp.pad(dcf, ((0, 0), (0, LANE - nh))), sv['fl'],
                                    sv['bf_row'])
        sm['b_forget'][l] = dbf[0, :nh]
        drx, dry, dcw, dcb, dwr, dbr, dwi, dbi, dlam = _lru_bwd(f"lru_bwd_{l}", u, sv['hstate'], do_all[1], sv['lp'], bw)
        sm['conv_w'][l], sm['conv_b'][l], sm['w_r'][l], sm['b_r'][l] = dcw[:CONV_WIDTH], dcb[0], dwr, dbr[0]
        sm['w_i'][l], sm['b_i'][l], sm['lru_lambda'][l] = dwi, dbi[0], dlam[0]
        dsq, dsk, dsv = _sb_bwd(f"sb_bwd_{l}", u, do_all[2], bw)
        dcq, dck, dcv, dss = _chunk_bwd(f"chunk_bwd_{l}", u, sv['kpad'], sv['vpad'], sv['bias'], do_all[3], bw)
        sm['rel_bias'][l] = _bias_reduce(f"bias_reduce_{l}", dss, nh)[:, 0, :REL_TABLE]
        du = jnp.concatenate([dfq, dfk, dfv, drx, dry, dsq, dsk, dsv, dcq, dck, dcv, dflb], axis=1)

        tnu = _lane_tile(nue)
        grad_mm('w_main', f"g_w_in_{l}", sv['xinb'], du, shape=(N_DEV, 1, rs, nue),
                grid=(d // tmr, nue // tnu, s // tks),
                a_spec=pl.BlockSpec((tks, tmr), lambda m, n, k: (k, m)),
                b_spec=pl.BlockSpec((tks, tnu), lambda m, n, k: (k, n)),
                out_spec=pl.BlockSpec((nsh, None, rs, tnu), lambda m, n, k: (m, 0, 0, n)), acc_shape=(tmr, tnu))
        cores2 = reduce_cores(l, GROUP2)
        dcur = _mm(f"d_x_in_{l}", du, Wl['main'], grid=(s // tm, d // tnd, nue // tnu),
                   a_spec=pl.BlockSpec((tm, tnu), lambda m, n, k: (m, k)),
                   b_spec=pl.BlockSpec((None, tnd, tnu), lambda m, n, k: (0, n, k)),
                   dims=NT, acc_shape=(tm, tnd), out_shape=[jax.ShapeDtypeStruct((s, d), F32)],
                   out_specs=[tile_md], finish=resid_finish(1.0), extras=(dx_gate,), extra_specs=(tile_md,),
                   side=cores2)[0]
        pending = reduce_chips(cores2)

    grad_x, _, dg_in, db_in = _ln_bwd("ln_in_bwd", dcur, xs, ln_in_g, side=pending)
    reduction_done(pending)

    small_names = ['ln_in_g', 'ln_in_b', 'b_forget', 'conv_w', 'conv_b', 'w_r', 'b_r', 'w_i', 'b_i', 'lru_lambda',
                   'rel_bias', 'b_gate', 'ln1_g', 'ln1_b', 'ln2_g', 'ln2_b']
    local = {'ln_in_g': dg_in[0], 'ln_in_b': db_in[0]}
    for n in small_names[2:]:
        local[n] = jnp.stack(sm[n])
    full_shapes = [local[n].shape for n in small_names]
    packed = _pack([local[n] for n in small_names])
    every = _run_side("gather_small_cores",
                      _cores_side(_run_side("gather_small_chips", _chips_side([packed], True)), True))[0]
    every = jnp.swapaxes(every, 0, 1).reshape((N_DEV,) + packed.shape)
    total = dict(zip(small_names, _unpack(_sum_parts("sum_small", every), full_shapes)))
    for n, width in (('conv_w', bw // N_DEV), ('rel_bias', REL_TABLE // N_DEV), ('b_gate', cs)):
        total[n] = lax.dynamic_slice_in_dim(total[n], dev * width, width, axis=2)

    out = {}

    def update(n, parts):
        cols = parts[0].shape[2]
        w2, m2, v2 = (given[p + n].reshape(-1, cols) for p in ('', 'm_', 'v_'))
        res = None
        for l in range(nl):
            res = _adamw(f"adamw_{n}_{l}", w2, m2, v2, parts[l], layer=l, layers=nl, earlier=res)
        out[n] = [r.reshape(given[n].shape) for r in res]

    def parts_w_in(l):
        pm = reduced[l]['w_main']
        return jnp.concatenate([pm[..., :3 * bw], pm[..., nu:nu + nh], pm[..., 3 * bw:nu]], axis=-1)

    update('w_in', [parts_w_in(l) for l in range(nl)])
    for n in ('w_branch', 'w_gate', 'w_out', 'w_ff1', 'w_ff2'):
        update(n, [reduced[l][n] for l in range(nl)])

    small_shapes2 = [given[n].shape for n in small_names]
    res = _adamw("adamw_small", _pack([given[n] for n in small_names]), _pack([given['m_' + n] for n in small_names]),
                 _pack([given['v_' + n] for n in small_names]), _pack([total[n] for n in small_names])[None])
    res = [_unpack(r, small_shapes2) for r in res]
    for j, n in enumerate(small_names):
        out[n] = [res[k][j] for k in range(4)]

    return (loss, grad_x[None], *[out[n][0] for n in WEIGHTS], *[out[n][1] for n in WEIGHTS],
            *[out[n][2] for n in WEIGHTS], *[out[n][3] for n in WEIGHTS])
```

```python
import functools
import math

import jax
import jax.numpy as jnp
from jax import lax
from jax.experimental import pallas as pl
from jax.experimental.pallas import tpu as pltpu

F32, BF16, I32 = jnp.float32, jnp.bfloat16, jnp.int32
MESH = pl.DeviceIdType.MESH
ANY = pl.BlockSpec(memory_space=pl.ANY)

LANE = 128
VMEM_LIMIT = 56 * 1024 * 1024
N_DEV = 8

HEAD = 128
CHUNK = 64
LOOKBACK = 8
BAND = (LOOKBACK + 1) * CHUNK
QCHUNKS = 4
QBLK = QCHUNKS * CHUNK
WIN = BAND + (QCHUNKS - 1) * CHUNK
PADK = LOOKBACK * CHUNK
REL_CLIP = 256
REL_TABLE = REL_CLIP + CHUNK
REL_PAD = 384
CONV_WIDTH = 4
LRU_C = 8.0
LN_EPS = 1e-5
DEPTH = 2
ALPHA = (2.0 * DEPTH) ** 0.25
NEG = -1e30
SB_DEAD = -104.0
GELU_K = math.sqrt(2.0 / math.pi)
GELU_C = 0.044715

ADAM_LR, ADAM_B1, ADAM_B2, ADAM_EPS, ADAM_WD, ADAM_STEP = 0.001, 0.9, 0.999, 1e-08, 0.01, 10

NN = (((1,), (0,)), ((), ()))
NT = (((1,), (1,)), ((), ()))
TN = (((0,), (0,)), ((), ()))

FQ, FK, FV, RX, RY, SQ, SK, SV, CQ, CK, CV = range(11)

NAMES = ['x', 'ln_in_g', 'ln_in_b', 'w_in', 'b_forget', 'conv_w', 'conv_b', 'w_r', 'b_r', 'w_i', 'b_i', 'lru_lambda',
         'rel_bias', 'w_branch', 'w_gate', 'b_gate', 'w_out', 'ln1_g', 'ln1_b', 'w_ff1', 'w_ff2', 'ln2_g', 'ln2_b']
WEIGHTS = NAMES[1:]


def _cp(sem=None):
    return pltpu.CompilerParams(dimension_semantics=sem, vmem_limit_bytes=VMEM_LIMIT)


def _iota(shape, dim):
    return lax.broadcasted_iota(I32, shape, dim)


def _sigmoid(x):
    return 1.0 / (1.0 + jnp.exp(-x))


def _log_sigmoid(x):
    return jnp.minimum(x, 0.0) - jnp.log(1.0 + jnp.exp(-jnp.abs(x)))


def _lane_tile(n, cap=1536):
    best = max(t for t in range(LANE, cap + 1, LANE) if n % t == 0)
    return n if best == LANE else best


def _pow2_rows(rows, cols, elems=262144):
    t = 8
    while t * 2 <= rows and t * 2 * cols <= elems and rows % (t * 2) == 0:
        t *= 2
    return t


def _position():
    return lax.axis_index("x"), lax.axis_index("y"), lax.axis_index("c")


class _Side:
    def __init__(self, operands, out_shape, sems, start, finish, aliases=(), parts=()):
        self.operands, self.out_shape, self.sems = list(operands), list(out_shape), list(sems)
        self.start, self.finish, self.aliases, self.parts = start, finish, list(aliases), parts
        self.results = None

    def set_results(self, res):
        self.results = list(res)
        off = 0
        for part in self.parts:
            part.set_results(res[off:off + len(part.out_shape)])
            off += len(part.out_shape)


def _merge_sides(a, b):
    ai, ao, asm = len(a.operands), len(a.out_shape), len(a.sems)

    def start(ins, outs, sems):
        a.start(ins[:ai], outs[:ao], sems[:asm])
        b.start(ins[ai:], outs[ao:], sems[asm:])

    def finish(ins, outs, sems):
        a.finish(ins[:ai], outs[:ao], sems[:asm])
        b.finish(ins[ai:], outs[ao:], sems[asm:])

    return _Side(a.operands + b.operands, a.out_shape + b.out_shape, a.sems + b.sems, start, finish,
                 a.aliases + [(i + ai, o + ao) for i, o in b.aliases], parts=(a, b))


def _chips_side(xs, gather):
    n = len(xs)

    def copies(ins, outs, sems, arrivals):
        send_sems, recv_sems, local_sems = sems
        x, y, c = _position()
        q = 2 * x + y
        chips = [(1 - x, y), (x, 1 - y), (1 - x, 1 - y)]

        def src(t, slot):
            return ins[t] if gather else ins[t].at[slot]

        def dst(t, slot):
            return outs[t].at[c, slot] if gather else outs[t].at[slot]

        def remote(t, j, landing):
            px, py = chips[j]
            return pltpu.make_async_remote_copy(
                src_ref=src(t, 2 * px + py), dst_ref=dst(t, landing), send_sem=send_sems.at[t, j],
                recv_sem=recv_sems.at[t, j], device_id=(px, py, c), device_id_type=MESH)

        local = [pltpu.make_async_copy(src(t, q), dst(t, q), local_sems.at[t]) for t in range(n)]
        sends = [remote(t, j, q) for t in range(n) for j in range(3)]
        if not arrivals:
            return local, sends, []
        return local, sends, [remote(t, j, 2 * px + py) for t in range(n) for j, (px, py) in enumerate(chips)]

    def start(ins, outs, sems):
        local, sends, _ = copies(ins, outs, sems, False)
        for cp in local + sends:
            cp.start()

    def finish(ins, outs, sems):
        local, sends, recvs = copies(ins, outs, sems, True)
        for cp in recvs:
            cp.wait_recv()
        for cp in sends:
            cp.wait_send()
        for cp in local:
            cp.wait()

    out_shape = [jax.ShapeDtypeStruct((2, 4) + a.shape if gather else a.shape, a.dtype) for a in xs]
    sems = [pltpu.SemaphoreType.DMA((n, 3)), pltpu.SemaphoreType.DMA((n, 3)), pltpu.SemaphoreType.DMA((n,))]
    return _Side(xs, out_shape, sems, start, finish)


def _cores_side(xs, gather):
    n = len(xs)
    m = 1 if gather else 4

    def copies(ins, outs, sems, arrivals):
        send_sems, recv_sems = sems
        x, y, c = _position()

        def remote(t, j, landing):
            s = outs[t].at[c] if gather else ins[t].at[j, 1 - c]
            d = outs[t].at[landing] if gather else outs[t].at[j]
            return pltpu.make_async_remote_copy(
                src_ref=s, dst_ref=d, send_sem=send_sems.at[t, j], recv_sem=recv_sems.at[t, j],
                device_id=(x, y, 1 - c), device_id_type=MESH)

        sends = [remote(t, j, c) for t in range(n) for j in range(m)]
        return sends, [remote(t, j, 1 - c) for t in range(n) for j in range(m)] if arrivals else []

    def start(ins, outs, sems):
        for cp in copies(ins, outs, sems, False)[0]:
            cp.start()

    def finish(ins, outs, sems):
        sends, recvs = copies(ins, outs, sems, True)
        for cp in recvs:
            cp.wait_recv()
        for cp in sends:
            cp.wait_send()

    if gather:
        out_shape = [jax.ShapeDtypeStruct(a.shape, a.dtype) for a in xs]
    else:
        out_shape = [jax.ShapeDtypeStruct((4,) + a.shape[2:], a.dtype) for a in xs]
    sems = [pltpu.SemaphoreType.DMA((n, m)), pltpu.SemaphoreType.DMA((n, m))]
    return _Side(xs, out_shape, sems, start, finish, aliases=[(t, t) for t in range(n)] if gather else [])


def _run_side(name, side):
    ni, no = len(side.operands), len(side.out_shape)

    def body(*refs):
        ins, outs, sems = refs[:ni], refs[ni:ni + no], refs[ni + no:]
        side.start(ins, outs, sems)
        side.finish(ins, outs, sems)

    side.set_results(pl.pallas_call(
        body, name=name, out_shape=side.out_shape, in_specs=[ANY] * ni, out_specs=[ANY] * no,
        input_output_aliases=dict(side.aliases), scratch_shapes=side.sems)(*side.operands))
    return side.results


def _pcall(body, operands, *, name, grid, in_specs, out_specs, out_shape, scratch_shapes=(), sem=None, aliases=None,
           side=None):
    if side is None:
        return pl.pallas_call(body, name=name, grid=grid, in_specs=list(in_specs), out_specs=list(out_specs),
                              out_shape=list(out_shape), scratch_shapes=list(scratch_shapes),
                              input_output_aliases=aliases or {}, compiler_params=_cp(sem))(*operands)
    ni, no, ns = len(in_specs), len(out_shape), len(scratch_shapes)
    si, so = len(side.operands), len(side.out_shape)

    def carrying(*refs):
        ins, sins = refs[:ni], refs[ni:ni + si]
        outs, souts = refs[ni + si:ni + si + no], refs[ni + si + no:ni + si + no + so]
        scratch, ssems = refs[ni + si + no + so:ni + si + no + so + ns], refs[ni + si + no + so + ns:]
        ids = [pl.program_id(a) for a in range(len(grid))]
        first = functools.reduce(jnp.logical_and, [i == 0 for i in ids])
        last = functools.reduce(jnp.logical_and, [i == g - 1 for i, g in zip(ids, grid)])

        @pl.when(first)
        def _():
            side.start(sins, souts, ssems)

        body(*ins, *outs, *scratch)

        @pl.when(last)
        def _():
            side.finish(sins, souts, ssems)

    joined = dict(aliases or {})
    joined.update({ni + i: no + o for i, o in side.aliases})
    res = pl.pallas_call(
        carrying, name=name, grid=grid, in_specs=[*in_specs, *[ANY] * si], out_specs=[*out_specs, *[ANY] * so],
        out_shape=[*out_shape, *side.out_shape], scratch_shapes=[*scratch_shapes, *side.sems],
        input_output_aliases=joined, compiler_params=_cp(("arbitrary",) * len(grid)))(*operands, *side.operands)
    side.set_results(res[no:])
    return res[:no]


def _add_core_halves(name, mine, other, c, out_dtype):
    _, _, rows, cols = mine.shape
    tr = _pow2_rows(rows, cols)

    def body(c_ref, a_ref, b_ref, o_ref):
        o_ref[...] = (a_ref[...].astype(F32) + b_ref[...].astype(F32)).astype(out_dtype)

    grid_spec = pltpu.PrefetchScalarGridSpec(
        num_scalar_prefetch=1, grid=(4, rows // tr),
        in_specs=[pl.BlockSpec((None, None, tr, cols), lambda j, i, c_ref: (j, c_ref[0], i, 0)),
                  pl.BlockSpec((None, tr, cols), lambda j, i, c_ref: (j, i, 0))],
        out_specs=pl.BlockSpec((None, tr, cols), lambda j, i, c_ref: (j, i, 0)))
    return pl.pallas_call(body, name=name, grid_spec=grid_spec,
                          out_shape=jax.ShapeDtypeStruct((4, rows, cols), out_dtype),
                          compiler_params=_cp(("parallel", "parallel")))(c, mine, other)


def _sum_parts(name, parts):
    p, rows, cols = parts.shape
    tr = _pow2_rows(rows, cols * p)

    def body(a_ref, o_ref):
        acc = a_ref[0]
        for k in range(1, p):
            acc = acc + a_ref[k]
        o_ref[...] = acc

    return pl.pallas_call(body, name=name, grid=(rows // tr,),
                          in_specs=[pl.BlockSpec((p, tr, cols), lambda i: (0, i, 0))],
                          out_specs=pl.BlockSpec((tr, cols), lambda i: (i, 0)),
                          out_shape=jax.ShapeDtypeStruct((rows, cols), F32),
                          compiler_params=_cp(("parallel",)))(parts)


def _mm(name, a, b, *, grid, a_spec, b_spec, dims, acc_shape, out_shape, out_specs, finish,
        extras=(), extra_specs=(), aliases=None, sem=("parallel", "parallel", "arbitrary"), side=None):
    nk, ne, no = grid[2], len(extras), len(out_shape)

    def body(*refs):
        a_ref, b_ref = refs[0], refs[1]
        ex, outs = refs[2:2 + ne], refs[2 + ne:2 + ne + no]
        ids = (pl.program_id(0), pl.program_id(1))
        def prod():
            return lax.dot_general(a_ref[...], b_ref[...], dims, preferred_element_type=F32)

        if nk == 1:
            finish(prod(), ex, outs, ids)
            return
        acc = refs[2 + ne + no]
        k = pl.program_id(2)

        @pl.when(k == 0)
        def _():
            acc[...] = prod()

        @pl.when(jnp.logical_and(k > 0, k < nk - 1))
        def _():
            acc[...] += prod()

        @pl.when(k == nk - 1)
        def _():
            finish(acc[...] + prod(), ex, outs, ids)

    return _pcall(body, (a, b, *extras), name=name, grid=grid, in_specs=[a_spec, b_spec, *extra_specs],
                  out_specs=out_specs, out_shape=out_shape,
                  scratch_shapes=[pltpu.VMEM(acc_shape, F32)] if nk > 1 else [], sem=sem, aliases=aliases, side=side)


def _store(dtype):
    def finish(acc, ex, outs, ids):
        outs[0][...] = acc.reshape(outs[0].shape).astype(dtype)
    return finish


def _layer_norm_rows(z, g, b):
    mu = jnp.mean(z, axis=1, keepdims=True)
    zc = z - mu
    var = jnp.mean(zc * zc, axis=1, keepdims=True)
    return zc * lax.rsqrt(var + LN_EPS) * g + b


def _mm_ln(name, a, w, l, resid, g, b, side=None):
    s, kdim = a.shape
    d = w.shape[2]
    tm, tk = min(512, s), min(1024, kdim)

    def finish(acc, ex, outs, ids):
        z = acc + ALPHA * ex[0][...]
        y = _layer_norm_rows(z, ex[1][...], ex[2][...])
        outs[0][...] = z
        outs[1][...] = y
        outs[2][...] = y.astype(BF16)

    row = pl.BlockSpec((tm, d), lambda m, n, k: (m, 0))
    vec = pl.BlockSpec((1, d), lambda m, n, k: (0, 0))
    return _mm(name, a, w, grid=(s // tm, 1, kdim // tk),
               a_spec=pl.BlockSpec((tm, tk), lambda m, n, k: (m, k)),
               b_spec=pl.BlockSpec((None, tk, d), lambda m, n, k: (l, k, 0)),
               dims=NN, acc_shape=(tm, d),
               out_shape=[jax.ShapeDtypeStruct((s, d), F32), jax.ShapeDtypeStruct((s, d), F32),
                          jax.ShapeDtypeStruct((s, d), BF16)],
               out_specs=[row, row, row], finish=finish,
               extras=(resid, g.reshape(1, d), b.reshape(1, d)), extra_specs=(row, vec, vec), side=side)


def _ln_fwd(name, x, g, b, side=None):
    s, d = x.shape
    tr = min(256, s)

    def body(x_ref, g_ref, b_ref, y_ref, yb_ref):
        y = _layer_norm_rows(x_ref[...], g_ref[...], b_ref[...])
        y_ref[...] = y
        yb_ref[...] = y.astype(BF16)

    row = pl.BlockSpec((tr, d), lambda i: (i, 0))
    vec = pl.BlockSpec((1, d), lambda i: (0, 0))
    return _pcall(body, (x, g.reshape(1, d), b.reshape(1, d)), name=name, grid=(s // tr,), in_specs=[row, vec, vec],
                  out_specs=[row, row],
                  out_shape=[jax.ShapeDtypeStruct((s, d), F32), jax.ShapeDtypeStruct((s, d), BF16)],
                  sem=("parallel",), side=side)


def _ln_bwd_rows(dyv, zz, g, dz_ref, dzb_ref, dg_ref, db_ref):
    mu = jnp.mean(zz, axis=1, keepdims=True)
    zc = zz - mu
    rstd = lax.rsqrt(jnp.mean(zc * zc, axis=1, keepdims=True) + LN_EPS)
    xhat = zc * rstd
    dg_ref[...] += jnp.sum(dyv * xhat, axis=0, keepdims=True)
    db_ref[...] += jnp.sum(dyv, axis=0, keepdims=True)
    dxh = dyv * g
    dz = rstd * (dxh - jnp.mean(dxh, axis=1, keepdims=True) - xhat * jnp.mean(dxh * xhat, axis=1, keepdims=True))
    dz_ref[...] = dz
    dzb_ref[...] = dz.astype(BF16)


def _ln_bwd(name, dy, z, g, side=None):
    s, d = z.shape
    tr = min(256, s)

    def body(dy_ref, z_ref, g_ref, dz_ref, dzb_ref, dg_ref, db_ref):
        @pl.when(pl.program_id(0) == 0)
        def _():
            dg_ref[...] = jnp.zeros(dg_ref.shape, F32)
            db_ref[...] = jnp.zeros(db_ref.shape, F32)

        _ln_bwd_rows(dy_ref[...], z_ref[...], g_ref[...], dz_ref, dzb_ref, dg_ref, db_ref)

    row = pl.BlockSpec((tr, d), lambda i: (i, 0))
    vec = pl.BlockSpec((1, d), lambda i: (0, 0))
    return _pcall(
        body, (dy, z, g.reshape(1, d)), name=name, grid=(s // tr,), in_specs=[row, row, vec],
        out_specs=[row, row, vec, vec],
        out_shape=[jax.ShapeDtypeStruct((s, d), F32), jax.ShapeDtypeStruct((s, d), BF16),
                   jax.ShapeDtypeStruct((1, d), F32), jax.ShapeDtypeStruct((1, d), F32)],
        sem=("arbitrary",), side=side)


def _loss_ln_bwd(name, y, target, z, g):
    s, d = y.shape
    tr = min(256, s)

    def body(y_ref, t_ref, z_ref, g_ref, loss_ref, dz_ref, dzb_ref, dg_ref, db_ref):
        @pl.when(pl.program_id(0) == 0)
        def _():
            loss_ref[...] = jnp.zeros(loss_ref.shape, F32)
            dg_ref[...] = jnp.zeros(dg_ref.shape, F32)
            db_ref[...] = jnp.zeros(db_ref.shape, F32)

        e = y_ref[...] - t_ref[...]
        loss_ref[...] += jnp.sum(e * e) * (0.5 / d)
        _ln_bwd_rows(e * (1.0 / d), z_ref[...], g_ref[...], dz_ref, dzb_ref, dg_ref, db_ref)

    row = pl.BlockSpec((tr, d), lambda i: (i, 0))
    vec = pl.BlockSpec((1, d), lambda i: (0, 0))
    return pl.pallas_call(
        body, name=name, grid=(s // tr,), in_specs=[row, row, row, vec],
        out_specs=[pl.BlockSpec((8, LANE), lambda i: (0, 0)), row, row, vec, vec],
        out_shape=[jax.ShapeDtypeStruct((8, LANE), F32), jax.ShapeDtypeStruct((s, d), F32),
                   jax.ShapeDtypeStruct((s, d), BF16), jax.ShapeDtypeStruct((1, d), F32),
                   jax.ShapeDtypeStruct((1, d), F32)],
        compiler_params=_cp(("arbitrary",)))(y, target, z, g.reshape(1, d))


def _scan_add(x, reverse):
    ts = x.shape[0]
    rows = _iota((ts, 1), 0)
    dist = 1
    while dist < ts:
        if reverse:
            x = x + jnp.where(rows < ts - dist, pltpu.roll(x, ts - dist, 0), 0.0)
        else:
            x = x + jnp.where(rows >= dist, pltpu.roll(x, dist, 0), 0.0)
        dist *= 2
    return x


def _scan_affine(a, b, reverse):
    ts = a.shape[0]
    rows = _iota((ts, 1), 0)
    dist = 1
    while dist < ts:
        shift = ts - dist if reverse else dist
        valid = rows < ts - dist if reverse else rows >= dist
        b = b + a * jnp.where(valid, pltpu.roll(b, shift, 0), 0.0)
        a = a * jnp.where(valid, pltpu.roll(a, shift, 0), 1.0)
        dist *= 2
    return a, b


def _cum_forget_fwd(name, fl, bias):
    s = fl.shape[0]
    ts = min(1024, s)

    def body(f_ref, b_ref, o_ref, carry):
        @pl.when(pl.program_id(0) == 0)
        def _():
            carry[...] = jnp.zeros(carry.shape, F32)

        o_ref[...] = _scan_add(_log_sigmoid(f_ref[...] + b_ref[...]), False) + carry[...]
        carry[...] = o_ref[pl.ds(ts - 1, 1), :]

    row = pl.BlockSpec((ts, LANE), lambda i: (i, 0))
    return pl.pallas_call(body, name=name, grid=(s // ts,),
                          in_specs=[row, pl.BlockSpec((1, LANE), lambda i: (0, 0))], out_specs=row,
                          out_shape=jax.ShapeDtypeStruct((s, LANE), F32),
                          scratch_shapes=[pltpu.VMEM((1, LANE), F32)],
                          compiler_params=_cp(("arbitrary",)))(fl, bias)


def _cum_forget_bwd(name, dcf, fl, bias):
    s = fl.shape[0]
    ts = min(1024, s)
    nb = s // ts

    def body(d_ref, f_ref, b_ref, o_ref, db_ref, carry):
        @pl.when(pl.program_id(0) == 0)
        def _():
            carry[...] = jnp.zeros(carry.shape, F32)
            db_ref[...] = jnp.zeros(db_ref.shape, F32)

        run = _scan_add(d_ref[...], True) + carry[...]
        carry[...] = jnp.sum(jnp.where(_iota((ts, 1), 0) == 0, run, 0.0), axis=0, keepdims=True)
        dfl = run * _sigmoid(-(f_ref[...] + b_ref[...]))
        o_ref[...] = dfl.astype(BF16)
        db_ref[...] += jnp.sum(dfl, axis=0, keepdims=True)

    row = pl.BlockSpec((ts, LANE), lambda i: (nb - 1 - i, 0))
    vec = pl.BlockSpec((1, LANE), lambda i: (0, 0))
    return pl.pallas_call(body, name=name, grid=(nb,), in_specs=[row, row, vec], out_specs=[row, vec],
                          out_shape=[jax.ShapeDtypeStruct((s, LANE), BF16), jax.ShapeDtypeStruct((1, LANE), F32)],
                          scratch_shapes=[pltpu.VMEM((1, LANE), F32)],
                          compiler_params=_cp(("arbitrary",)))(dcf, fl, bias)


def _fox_specs(s, nh, tq, tk):
    q = pl.BlockSpec((tq, HEAD), lambda h, i: (i, FQ * nh + h))
    k = pl.BlockSpec((s, HEAD), lambda h, i: (0, FK * nh + h))
    v = pl.BlockSpec((s, HEAD), lambda h, i: (0, FV * nh + h))
    col = pl.BlockSpec((None, tq, 1), lambda h, i: (h, i, 0))
    rowv = pl.BlockSpec((None, s // tk, 1, tk), lambda h, i: (h, 0, 0, 0))
    tile = pl.BlockSpec((tq, HEAD), lambda h, i: (i, h))
    full = pl.BlockSpec((s, HEAD), lambda h, i: (0, h))
    return q, k, v, col, rowv, tile, full


def _fox_tile(s):
    return min(512, s), min(1024, s)


def _fox_scores(q, k_ref, cfq, cfr_ref, kb, tk, scale, qpos=None):
    off = pl.multiple_of(kb * tk, tk)
    k = k_ref[pl.ds(off, tk), :]
    sc = lax.dot_general(q, k, NT, preferred_element_type=F32) * scale + cfq - cfr_ref[kb]
    mask = None
    if qpos is not None:
        mask = kb * tk + _iota((1, tk), 1) <= qpos
        sc = jnp.where(mask, sc, NEG)
    return sc, mask, k, off


def _fox_fwd(name, u, cf_col, cf_row, bw, side=None):
    s, nh = u.shape[0], bw // HEAD
    tq, tk = _fox_tile(s)
    scale = HEAD ** -0.5

    def body(q_ref, k_ref, v_ref, cfc_ref, cfr_ref, o_ref, lse_ref):
        i = pl.program_id(1)
        q, cfq = q_ref[...], cfc_ref[...]
        last = ((i + 1) * tq - 1) // tk

        def step(kb, carry, qpos=None):
            m, l, acc = carry
            sc, _, _, off = _fox_scores(q, k_ref, cfq, cfr_ref, kb, tk, scale, qpos)
            m2 = jnp.maximum(m, jnp.max(sc, axis=1, keepdims=True))
            p = jnp.exp(sc - m2)
            al = jnp.exp(m - m2)
            return (m2, al * l + jnp.sum(p, axis=1, keepdims=True),
                    al * acc + jnp.dot(p.astype(BF16), v_ref[pl.ds(off, tk), :], preferred_element_type=F32))

        init = (jnp.full((tq, 1), NEG, F32), jnp.zeros((tq, 1), F32), jnp.zeros((tq, HEAD), F32))
        m, l, acc = step(last, lax.fori_loop(0, last, step, init), i * tq + _iota((tq, 1), 0))
        o_ref[...] = (acc / l).astype(BF16)
        lse_ref[...] = m + jnp.log(l)

    q, k, v, col, rowv, tile, _ = _fox_specs(s, nh, tq, tk)
    return _pcall(body, (u, u, u, cf_col, cf_row), name=name, grid=(nh, s // tq), in_specs=[q, k, v, col, rowv],
                  out_specs=[tile, col],
                  out_shape=[jax.ShapeDtypeStruct((s, bw), BF16), jax.ShapeDtypeStruct((nh, s, 1), F32)],
                  sem=("parallel", "parallel"), side=side)


def _fox_bwd(name, u, cf_col, cf_row, o, do, lse, bw, side=None):
    s, nh = u.shape[0], bw // HEAD
    tq, tk = _fox_tile(s)
    nq = s // tq
    scale = HEAD ** -0.5

    def body(q_ref, k_ref, v_ref, cfc_ref, cfr_ref, o_ref, do_ref, lse_ref,
             dq_ref, dk_ref, dv_ref, dcc_ref, dcr_ref, dk_s, dv_s):
        i = pl.program_id(1)

        @pl.when(i == 0)
        def _():
            dk_s[...] = jnp.zeros(dk_s.shape, F32)
            dv_s[...] = jnp.zeros(dv_s.shape, F32)
            dcr_ref[...] = jnp.zeros(dcr_ref.shape, F32)

        q, dov, cfq, lse_q = q_ref[...], do_ref[...], cfc_ref[...], lse_ref[...]
        delta = jnp.sum(dov.astype(F32) * o_ref[...].astype(F32), axis=1, keepdims=True)
        last = ((i + 1) * tq - 1) // tk

        def step(kb, carry, qpos=None):
            dq, dcq = carry
            sc, mask, k, off = _fox_scores(q, k_ref, cfq, cfr_ref, kb, tk, scale, qpos)
            p = jnp.exp(sc - lse_q)
            if qpos is not None:
                p = jnp.where(mask, p, 0.0)
            dp = lax.dot_general(dov, v_ref[pl.ds(off, tk), :], NT, preferred_element_type=F32)
            ds = p * (dp - delta)
            dsb = ds.astype(BF16)
            dk_s[pl.ds(off, tk), :] += lax.dot_general(dsb, q, TN, preferred_element_type=F32)
            dv_s[pl.ds(off, tk), :] += lax.dot_general(p.astype(BF16), dov, TN, preferred_element_type=F32)
            dcr_ref[kb] += -jnp.sum(ds, axis=0, keepdims=True)
            return (dq + jnp.dot(dsb, k, preferred_element_type=F32), dcq + jnp.sum(ds, axis=1, keepdims=True))

        init = (jnp.zeros((tq, HEAD), F32), jnp.zeros((tq, 1), F32))
        dq, dcq = step(last, lax.fori_loop(0, last, step, init), i * tq + _iota((tq, 1), 0))
        dq_ref[...] = (dq * scale).astype(BF16)
        dcc_ref[...] = jnp.transpose(jnp.broadcast_to(dcq, (tq, LANE)))[:8, :]

        @pl.when(i == nq - 1)
        def _():
            dk_ref[...] = (dk_s[...] * scale).astype(BF16)
            dv_ref[...] = dv_s[...].astype(BF16)

    q, k, v, col, rowv, tile, full = _fox_specs(s, nh, tq, tk)
    by_query = pl.BlockSpec((None, None, 8, tq), lambda h, i: (h, i, 0, 0))
    return _pcall(
        body, (u, u, u, cf_col, cf_row, o, do, lse), name=name, grid=(nh, nq),
        in_specs=[q, k, v, col, rowv, tile, tile, col], out_specs=[tile, full, full, by_query, rowv],
        out_shape=[jax.ShapeDtypeStruct((s, bw), BF16)] * 3
        + [jax.ShapeDtypeStruct((nh, nq, 8, tq), F32), jax.ShapeDtypeStruct((nh, s // tk, 1, tk), F32)],
        scratch_shapes=[pltpu.VMEM((s, HEAD), F32), pltpu.VMEM((s, HEAD), F32)],
        sem=("arbitrary", "arbitrary"), side=side)


def _suffix_mm(x, ones_below):
    hi = x.astype(BF16)
    lo = (x - hi.astype(F32)).astype(BF16)
    return (jnp.dot(hi, ones_below, preferred_element_type=F32) + jnp.dot(lo, ones_below, preferred_element_type=F32))


def _sb_tile(q, k_ref, kb, tk, qpos, scale):
    off = pl.multiple_of(kb * tk, tk)
    k = k_ref[pl.ds(off, tk), :]
    z = lax.dot_general(q, k, NT, preferred_element_type=F32) * scale
    mask = kb * tk + _iota((1, tk), 1) < qpos
    lsn = -jnp.maximum(z, 0.0) - jnp.log(1.0 + jnp.exp(-jnp.abs(z)))
    return z, mask, lsn, jnp.where(mask, lsn, 0.0), k, off


def _sb_specs(s, nh, tq):
    q = pl.BlockSpec((tq, HEAD), lambda h, i: (i, SQ * nh + h))
    k = pl.BlockSpec((s, HEAD), lambda h, i: (0, SK * nh + h))
    v = pl.BlockSpec((s, HEAD), lambda h, i: (0, SV * nh + h))
    tile = pl.BlockSpec((tq, HEAD), lambda h, i: (i, h))
    full = pl.BlockSpec((s, HEAD), lambda h, i: (0, h))
    return q, k, v, tile, full


def _sb_fwd(name, u, bw, side=None):
    s, nh = u.shape[0], bw // HEAD
    tq = tk = 256
    scale = HEAD ** -0.5

    def body(q_ref, k_ref, v_ref, o_ref):
        i = pl.program_id(1)
        q = q_ref[...]
        qpos = i * tq + _iota((tq, 1), 0)
        later_keys = (_iota((tk, tk), 0) > _iota((tk, tk), 1)).astype(BF16)
        nk = (i * tq + tq + tk - 2) // tk

        def cond(st):
            return jnp.logical_and(st[0] < nk, st[3] > SB_DEAD)

        def step(st):
            j, c, acc, _ = st
            z, mask, lsn, lm, _, off = _sb_tile(q, k_ref, nk - 1 - j, tk, qpos, scale)
            a = jnp.where(mask, jnp.exp(lsn + z + c + _suffix_mm(lm, later_keys)), 0.0)
            acc = acc + jnp.dot(a.astype(BF16), v_ref[pl.ds(off, tk), :], preferred_element_type=F32)
            c = c + jnp.sum(lm, axis=1, keepdims=True)
            return j + 1, c, acc, jnp.max(c)

        init = (jnp.int32(0), jnp.zeros((tq, 1), F32), jnp.zeros((tq, HEAD), F32), jnp.float32(0.0))
        o_ref[...] = lax.while_loop(cond, step, init)[2].astype(BF16)

    q, k, v, tile, _ = _sb_specs(s, nh, tq)
    return _pcall(body, (u, u, u), name=name, grid=(nh, s // tq), in_specs=[q, k, v], out_specs=[tile],
                  out_shape=[jax.ShapeDtypeStruct((s, bw), BF16)], sem=("parallel", "parallel"), side=side)[0]


def _sb_bwd(name, u, do, bw):
    s, nh = u.shape[0], bw // HEAD
    tq = tk = 256
    nq = s // tq
    scale = HEAD ** -0.5

    def body(q_ref, k_ref, v_ref, do_ref, dq_ref, dk_ref, dv_ref, dk_s, dv_s):
        i = pl.program_id(1)

        @pl.when(i == 0)
        def _():
            dk_s[...] = jnp.zeros(dk_s.shape, F32)
            dv_s[...] = jnp.zeros(dv_s.shape, F32)

        q, dov = q_ref[...], do_ref[...]
        qpos = i * tq + _iota((tq, 1), 0)
        later_keys = (_iota((tk, tk), 0) > _iota((tk, tk), 1)).astype(BF16)
        this_and_later = (_iota((tk, tk), 0) >= _iota((tk, tk), 1)).astype(BF16)
        nk = (i * tq + tq + tk - 2) // tk

        def weights(j, c):
            z, mask, lsn, lm, k, off = _sb_tile(q, k_ref, nk - 1 - j, tk, qpos, scale)
            a = jnp.where(mask, jnp.exp(lsn + z + c + _suffix_mm(lm, later_keys)), 0.0)
            w = a * lax.dot_general(dov, v_ref[pl.ds(off, tk), :], NT, preferred_element_type=F32)
            return z, mask, lsn, lm, k, off, a, w

        def cond(st):
            return jnp.logical_and(st[0] < nk, st[3] > SB_DEAD)

        def step1(st):
            j, c, wc, _ = st
            _, _, _, lm, _, _, _, w = weights(j, c)
            c = c + jnp.sum(lm, axis=1, keepdims=True)
            return j + 1, c, wc + jnp.sum(w, axis=1, keepdims=True), jnp.max(c)

        zero = jnp.zeros((tq, 1), F32)
        live, _, total, _ = lax.while_loop(cond, step1, (jnp.int32(0), zero, zero, jnp.float32(0.0)))

        def step2(j, st):
            c, wc, dq = st
            z, mask, lsn, lm, k, off, a, w = weights(j, c)
            earlier = total - (wc + _suffix_mm(w, this_and_later))
            dz = jnp.where(mask, w * jnp.exp(lsn) - jnp.exp(lsn + z) * earlier, 0.0)
            dzb = dz.astype(BF16)
            dk_s[pl.ds(off, tk), :] += lax.dot_general(dzb, q, TN, preferred_element_type=F32)
            dv_s[pl.ds(off, tk), :] += lax.dot_general(a.astype(BF16), dov, TN, preferred_element_type=F32)
            return (c + jnp.sum(lm, axis=1, keepdims=True), wc + jnp.sum(w, axis=1, keepdims=True),
                    dq + jnp.dot(dzb, k, preferred_element_type=F32))

        dq = lax.fori_loop(0, live, step2, (zero, zero, jnp.zeros((tq, HEAD), F32)))[2]
        dq_ref[...] = (dq * scale).astype(BF16)

        @pl.when(i == nq - 1)
        def _():
            dk_ref[...] = (dk_s[...] * scale).astype(BF16)
            dv_ref[...] = dv_s[...].astype(BF16)

    q, k, v, tile, full = _sb_specs(s, nh, tq)
    return pl.pallas_call(
        body, name=name, grid=(nh, nq), in_specs=[q, k, v, tile], out_specs=[tile, full, full],
        out_shape=[jax.ShapeDtypeStruct((s, bw), BF16)] * 3,
        scratch_shapes=[pltpu.VMEM((s, HEAD), F32), pltpu.VMEM((s, HEAD), F32)],
        compiler_params=_cp(("arbitrary", "arbitrary")))(u, u, u, do)


BIAS_W = -(-(WIN + QBLK - 1) // LANE) * LANE


def _strip_onehot():
    col = _iota((1, BIAS_W), 1)
    ridx = jnp.clip(PADK + (QBLK - 1) - col, -(CHUNK - 1), REL_CLIP) + (CHUNK - 1)
    return (_iota((REL_PAD, BIAS_W), 0) == ridx).astype(BF16)


def _split2(x):
    hi = x.astype(BF16)
    return hi, (x - hi.astype(F32)).astype(BF16)


def _bias_expand(name, table, nh):
    def body(t_ref, o_ref, strip):
        table_f32 = t_ref[...]
        hi = table_f32.astype(BF16)
        mid, lo = _split2(table_f32 - hi.astype(F32))
        onehot = _strip_onehot()
        strip[...] = (jnp.dot(hi, onehot, preferred_element_type=F32) + jnp.dot(mid, onehot, preferred_element_type=F32)
                      + jnp.dot(lo, onehot, preferred_element_type=F32))
        row, kl = _iota((QBLK, 1), 0), _iota((1, WIN), 1)
        first = row - jnp.bitwise_and(row, CHUNK - 1)
        valid = jnp.logical_and(kl >= first, kl < first + BAND)
        for h in range(nh):
            rows = jnp.broadcast_to(strip[pl.ds(h, 1), :], (QBLK, BIAS_W))
            rolled = pltpu.roll(rows, BIAS_W - (QBLK - 1), 1, stride=1, stride_axis=0)
            o_ref[h] = jnp.where(valid, rolled[:, :WIN], NEG)

    return pl.pallas_call(body, name=name, out_shape=jax.ShapeDtypeStruct((nh, QBLK, WIN), F32),
                          in_specs=[pl.BlockSpec(memory_space=pltpu.VMEM)],
                          out_specs=pl.BlockSpec(memory_space=pltpu.VMEM),
                          scratch_shapes=[pltpu.VMEM((16, BIAS_W), F32)], compiler_params=_cp())(table)


def _bias_reduce(name, dss, nh):
    def body(x_ref, o_ref):
        onehot = _strip_onehot()
        flip = (_iota((QBLK, QBLK), 0) + _iota((QBLK, QBLK), 1) == QBLK - 1).astype(BF16)
        for h in range(nh):
            x = jnp.concatenate([x_ref[h], jnp.zeros((QBLK, BIAS_W - WIN), F32)], axis=1)
            hi, lo = _split2(x)
            back = jnp.dot(flip, hi, preferred_element_type=F32) + jnp.dot(flip, lo, preferred_element_type=F32)
            lined = pltpu.roll(back, 0, 1, stride=1, stride_axis=0)
            hi, lo = _split2(jnp.broadcast_to(jnp.sum(lined, axis=0, keepdims=True), (8, BIAS_W)))
            o_ref[h] = (lax.dot_general(hi, onehot, NT, preferred_element_type=F32)
                        + lax.dot_general(lo, onehot, NT, preferred_element_type=F32))

    return pl.pallas_call(body, name=name, out_shape=jax.ShapeDtypeStruct((nh, 8, REL_PAD), F32),
                          in_specs=[pl.BlockSpec(memory_space=pltpu.VMEM)],
                          out_specs=pl.BlockSpec(memory_space=pltpu.VMEM), compiler_params=_cp())(dss)


def _chunk_specs(s, nh):
    q = pl.BlockSpec((QBLK, HEAD), lambda h, i: (i, CQ * nh + h))
    kv = pl.BlockSpec((s + PADK, HEAD), lambda h, i: (0, h))
    bias = pl.BlockSpec((None, QBLK, WIN), lambda h, i: (h, 0, 0))
    tile = pl.BlockSpec((QBLK, HEAD), lambda h, i: (i, h))
    full = pl.BlockSpec((s, HEAD), lambda h, i: (0, h))
    return q, kv, bias, tile, full


def _chunk_probs(q, k_ref, b_ref, i, scale):
    off = pl.multiple_of(i * QBLK, QBLK)
    kw = k_ref[pl.ds(off, WIN), :]
    sc = lax.dot_general(q, kw, NT, preferred_element_type=F32) * scale + b_ref[...]
    sc = jnp.where(i * QBLK + _iota((1, WIN), 1) >= PADK, sc, NEG)
    p = jnp.exp(sc - jnp.max(sc, axis=1, keepdims=True))
    return p, jnp.sum(p, axis=1, keepdims=True), kw, off


def _chunk_fwd(name, u, kpad, vpad, bias, bw, side=None):
    s, nh = u.shape[0], bw // HEAD
    scale = HEAD ** -0.5

    def body(q_ref, k_ref, v_ref, b_ref, o_ref):
        p, l, _, off = _chunk_probs(q_ref[...], k_ref, b_ref, pl.program_id(1), scale)
        o = jnp.dot(p.astype(BF16), v_ref[pl.ds(off, WIN), :], preferred_element_type=F32)
        o_ref[...] = (o / l).astype(BF16)

    q, kv, bs, tile, _ = _chunk_specs(s, nh)
    return _pcall(body, (u, kpad, vpad, bias), name=name, grid=(nh, s // QBLK), in_specs=[q, kv, kv, bs],
                  out_specs=[tile], out_shape=[jax.ShapeDtypeStruct((s, bw), BF16)], sem=("parallel", "parallel"),
                  side=side)[0]


def _chunk_bwd(name, u, kpad, vpad, bias, do, bw):
    s, nh = u.shape[0], bw // HEAD
    nq = s // QBLK
    scale = HEAD ** -0.5

    def body(q_ref, k_ref, v_ref, b_ref, do_ref, dq_ref, dk_ref, dv_ref, dss_ref, dk_s, dv_s):
        i = pl.program_id(1)

        @pl.when(i == 0)
        def _():
            dk_s[...] = jnp.zeros(dk_s.shape, F32)
            dv_s[...] = jnp.zeros(dv_s.shape, F32)
            dss_ref[...] = jnp.zeros(dss_ref.shape, F32)

        q, dov = q_ref[...], do_ref[...]
        p, l, kw, off = _chunk_probs(q, k_ref, b_ref, i, scale)
        p = p / l
        dp = lax.dot_general(dov, v_ref[pl.ds(off, WIN), :], NT, preferred_element_type=F32)
        ds = p * (dp - jnp.sum(p * dp, axis=1, keepdims=True))
        dsb = ds.astype(BF16)
        dq_ref[...] = (jnp.dot(dsb, kw, preferred_element_type=F32) * scale).astype(BF16)
        dk_s[pl.ds(off, WIN), :] += lax.dot_general(dsb, q, TN, preferred_element_type=F32)
        dv_s[pl.ds(off, WIN), :] += lax.dot_general(p.astype(BF16), dov, TN, preferred_element_type=F32)
        dss_ref[...] += ds

        @pl.when(i == nq - 1)
        def _():
            dk_ref[...] = (dk_s[pl.ds(PADK, s), :] * scale).astype(BF16)
            dv_ref[...] = dv_s[pl.ds(PADK, s), :].astype(BF16)

    q, kv, bs, tile, full = _chunk_specs(s, nh)
    return pl.pallas_call(
        body, name=name, grid=(nh, nq), in_specs=[q, kv, kv, bs, tile], out_specs=[tile, full, full, bs],
        out_shape=[jax.ShapeDtypeStruct((s, bw), BF16)] * 3 + [jax.ShapeDtypeStruct((nh, QBLK, WIN), F32)],
        scratch_shapes=[pltpu.VMEM((s + PADK, HEAD), F32), pltpu.VMEM((s + PADK, HEAD), F32)],
        compiler_params=_cp(("arbitrary", "arbitrary")))(u, kpad, vpad, bias, do)


def _gelu_parts(y):
    th = jnp.tanh(GELU_K * (y + GELU_C * y * y * y))
    return 0.5 * y * (1.0 + th), th


def _block_diag(xb16, w_ref, nh, dims):
    return jnp.concatenate(
        [lax.dot_general(xb16[:, n * HEAD:(n + 1) * HEAD], w_ref[n], dims, preferred_element_type=F32)
         for n in range(nh)], axis=1)


def _lru_gates(ext, cw_ref, cb_ref, wr_ref, br_ref, wi_ref, bi_ref, lam_ref, ts, nh):
    shifted = [pltpu.roll(ext, CONV_WIDTH - 1 - j, 0)[8:, :] if j < CONV_WIDTH - 1 else ext[8:, :]
               for j in range(CONV_WIDTH)]
    xc = cb_ref[...]
    for j in range(CONV_WIDTH):
        xc = xc + shifted[j] * cw_ref[pl.ds(j, 1), :]
    xcb = xc.astype(BF16)
    r = _sigmoid(_block_diag(xcb, wr_ref, nh, NN) + br_ref[...])
    gi = _sigmoid(_block_diag(xcb, wi_ref, nh, NN) + bi_ref[...])
    lsl = _log_sigmoid(lam_ref[...])
    la = LRU_C * r * lsl
    a = jnp.exp(la)
    e2 = jnp.exp(2.0 * la)
    mult = jnp.sqrt(-jnp.tanh(la) * (e2 + 1.0))
    return shifted, xc, xcb, r, gi, lsl, a, e2, mult


def _lru_param_specs(bw, nh):
    vec = pl.BlockSpec((1, bw), lambda i: (0, 0))
    conv = pl.BlockSpec((8, bw), lambda i: (0, 0))
    blocks = pl.BlockSpec((nh, HEAD, HEAD), lambda i: (0, 0, 0))
    return [conv, vec, blocks, vec, blocks, vec, vec]


def _lru_fwd(name, u, params, bw):
    s, nh = u.shape[0], bw // HEAD
    ts = min(512, s)

    def body(rx_ref, ry_ref, cw_ref, cb_ref, wr_ref, br_ref, wi_ref, bi_ref, lam_ref, o_ref, h_ref, tail, hcar):
        @pl.when(pl.program_id(0) == 0)
        def _():
            tail[...] = jnp.zeros(tail.shape, F32)
            hcar[...] = jnp.zeros(hcar.shape, F32)

        rx = rx_ref[...].astype(F32)
        ext = jnp.concatenate([tail[...], rx], axis=0)
        tail[...] = rx[ts - 8:, :]
        _, xc, _, _, gi, _, a, _, mult = _lru_gates(ext, cw_ref, cb_ref, wr_ref, br_ref, wi_ref, bi_ref, lam_ref, ts, nh)
        acum, bcum = _scan_affine(a, mult * (gi * xc), False)
        h_ref[...] = bcum + acum * hcar[...]
        hcar[...] = h_ref[pl.ds(ts - 1, 1), :]
        o_ref[...] = (h_ref[...] * _gelu_parts(ry_ref[...].astype(F32))[0]).astype(BF16)

    row = pl.BlockSpec((ts, bw), lambda i: (i, 0))
    return pl.pallas_call(
        body, name=name, grid=(s // ts,),
        in_specs=[pl.BlockSpec((ts, bw), lambda i: (i, RX)), pl.BlockSpec((ts, bw), lambda i: (i, RY))]
        + _lru_param_specs(bw, nh),
        out_specs=[row, row],
        out_shape=[jax.ShapeDtypeStruct((s, bw), BF16), jax.ShapeDtypeStruct((s, bw), F32)],
        scratch_shapes=[pltpu.VMEM((8, bw), F32), pltpu.VMEM((1, bw), F32)],
        compiler_params=_cp(("arbitrary",)))(u, u, *params)


def _lru_bwd(name, u, h, do, params, bw):
    s, nh = u.shape[0], bw // HEAD
    ts = min(512, s)
    nb = s // ts
    t8 = ts // 8

    def body(rx_ref, rxp_ref, ry_ref, h_ref, hp_ref, do_ref, cw_ref, cb_ref, wr_ref, br_ref, wi_ref, bi_ref, lam_ref,
             drx_ref, dry_ref, dcw_ref, dcb_ref, dwr_ref, dbr_ref, dwi_ref, dbi_ref, dlam_ref, gcar, head):
        i = pl.program_id(0)
        first = i == nb - 1

        @pl.when(i == 0)
        def _():
            gcar[...] = jnp.zeros(gcar.shape, F32)
            head[...] = jnp.zeros(head.shape, F32)
            for ref in (dcw_ref, dcb_ref, dwr_ref, dbr_ref, dwi_ref, dbi_ref, dlam_ref):
                ref[...] = jnp.zeros(ref.shape, F32)

        rows = _iota((ts, 1), 0)
        rx = rx_ref[...].astype(F32)
        before = jnp.where(first, 0.0, rxp_ref[...].astype(F32))
        ext = jnp.concatenate([before, rx], axis=0)
        shifted, xc, xcb, r, gi, lsl, a, e2, mult = _lru_gates(
            ext, cw_ref, cb_ref, wr_ref, br_ref, wi_ref, bi_ref, lam_ref, ts, nh)

        ry = ry_ref[...].astype(F32)
        gel, th = _gelu_parts(ry)
        dgel = 0.5 * (1.0 + th) + 0.5 * ry * (1.0 - th * th) * GELU_K * (1.0 + 3.0 * GELU_C * ry * ry)
        dov = do_ref[...].astype(F32)
        hv = h_ref[...]
        dry_ref[...] = (dov * hv * dgel).astype(BF16)

        coef = jnp.where(rows < ts - 1, pltpu.roll(a, ts - 1, 0), 0.0)
        dh_in = dov * gel + jnp.where(rows == ts - 1, gcar[...], 0.0)
        dh = _scan_affine(coef, dh_in, True)[1]
        gcar[...] = jnp.sum(jnp.where(rows == 0, a * dh, 0.0), axis=0, keepdims=True)

        hprev = jnp.where(first, 0.0, hp_ref[...])
        hm1 = pltpu.roll(jnp.concatenate([hprev, hv], axis=0), 1, 0)[8:, :]
        dgx = dh * mult
        dla = dh * hm1 * a - dh * gi * xc * (e2 / mult)
        dpre_r = dla * (LRU_C * lsl) * r * (1.0 - r)
        dpre_i = dgx * xc * gi * (1.0 - gi)
        dlam_ref[...] += jnp.sum(dla * r, axis=0, keepdims=True) * (LRU_C * _sigmoid(-lam_ref[...]))
        dbr_ref[...] += jnp.sum(dpre_r, axis=0, keepdims=True)
        dbi_ref[...] += jnp.sum(dpre_i, axis=0, keepdims=True)
        drb, dib = dpre_r.astype(BF16), dpre_i.astype(BF16)
        for n in range(nh):
            cols = slice(n * HEAD, (n + 1) * HEAD)
            dwr_ref[n] += lax.dot_general(xcb[:, cols], drb[:, cols], TN, preferred_element_type=F32)
            dwi_ref[n] += lax.dot_general(xcb[:, cols], dib[:, cols], TN, preferred_element_type=F32)
        dxc = dgx * gi + _block_diag(drb, wr_ref, nh, NT) + _block_diag(dib, wi_ref, nh, NT)

        dcb_ref[...] += jnp.sum(dxc, axis=0, keepdims=True)
        for j in range(CONV_WIDTH):
            dcw_ref[pl.ds(j, 1), :] += jnp.sum(dxc * shifted[j], axis=0, keepdims=True)
        ext2 = jnp.concatenate([dxc, head[...]], axis=0)
        head[...] = dxc[:8, :]
        drx = dxc * cw_ref[pl.ds(CONV_WIDTH - 1, 1), :]
        for j in range(CONV_WIDTH - 1):
            up = CONV_WIDTH - 1 - j
            drx = drx + pltpu.roll(ext2, ts + 8 - up, 0)[:ts, :] * cw_ref[pl.ds(j, 1), :]
        drx_ref[...] = drx.astype(BF16)

    def blk(col):
        return lambda i: (nb - 1 - i, col)

    def prev8(col):
        return lambda i: (jnp.maximum((nb - 1 - i) * t8 - 1, 0), col)

    vec = pl.BlockSpec((1, bw), lambda i: (0, 0))
    conv = pl.BlockSpec((8, bw), lambda i: (0, 0))
    blocks = pl.BlockSpec((nh, HEAD, HEAD), lambda i: (0, 0, 0))
    return pl.pallas_call(
        body, name=name, grid=(nb,),
        in_specs=[pl.BlockSpec((ts, bw), blk(RX)), pl.BlockSpec((8, bw), prev8(RX)), pl.BlockSpec((ts, bw), blk(RY)),
                  pl.BlockSpec((ts, bw), blk(0)), pl.BlockSpec((8, bw), prev8(0)), pl.BlockSpec((ts, bw), blk(0))]
        + _lru_param_specs(bw, nh),
        out_specs=[pl.BlockSpec((ts, bw), blk(0)), pl.BlockSpec((ts, bw), blk(0)), conv, vec, blocks, vec, blocks, vec, vec],
        out_shape=[jax.ShapeDtypeStruct((s, bw), BF16)] * 2
        + [jax.ShapeDtypeStruct((8, bw), F32), jax.ShapeDtypeStruct((1, bw), F32),
           jax.ShapeDtypeStruct((nh, HEAD, HEAD), F32), jax.ShapeDtypeStruct((1, bw), F32),
           jax.ShapeDtypeStruct((nh, HEAD, HEAD), F32), jax.ShapeDtypeStruct((1, bw), F32),
           jax.ShapeDtypeStruct((1, bw), F32)],
        scratch_shapes=[pltpu.VMEM((1, bw), F32), pltpu.VMEM((8, bw), F32)],
        compiler_params=_cp(("arbitrary",)))(u, u, u, h, h, do, *params)


def _gate_merge(name, xb, w_gate, b_gate, o_all, w_branch, l, side=None):
    s, d = xb.shape
    bw = o_all.shape[2]
    tm, tn = min(512, s), min(256, d)

    def body(x_ref, wg_ref, bg_ref, o_ref, wb_ref, m_ref, g_ref, p_ref):
        x = x_ref[...]
        acc = jnp.zeros((tm, tn), F32)
        for g in range(4):
            gate = _sigmoid(jnp.dot(x, wg_ref[g], preferred_element_type=F32) + bg_ref[g])
            proj = jnp.dot(o_ref[g], wb_ref[g], preferred_element_type=F32)
            term = gate * proj
            g_ref[g] = gate.astype(BF16)
            p_ref[g] = (term * (1.0 - gate)).astype(BF16)
            acc = acc + term
        m_ref[...] = acc.astype(BF16)

    quad = pl.BlockSpec((4, tm, tn), lambda n, m: (0, m, n))
    return _pcall(
        body, (xb, w_gate, b_gate, o_all, w_branch), name=name, grid=(d // tn, s // tm),
        in_specs=[pl.BlockSpec((tm, d), lambda n, m: (m, 0)),
                  pl.BlockSpec((None, 4, d, tn), lambda n, m: (0, 0, 0, n)),
                  pl.BlockSpec((None, 4, 1, tn), lambda n, m: (l, 0, 0, n)),
                  pl.BlockSpec((4, tm, bw), lambda n, m: (0, m, 0)),
                  pl.BlockSpec((None, 4, bw, tn), lambda n, m: (0, 0, 0, n))],
        out_specs=[pl.BlockSpec((tm, tn), lambda n, m: (m, n)), quad, quad],
        out_shape=[jax.ShapeDtypeStruct((s, d), BF16), jax.ShapeDtypeStruct((4, s, d), BF16),
                   jax.ShapeDtypeStruct((4, s, d), BF16)],
        sem=("parallel", "parallel"), side=side)


def _adamw(name, w, m, v, parts, layer=0, layers=1, earlier=None):
    cols = w.shape[1]
    p, rows = parts.shape[0], parts.shape[1]
    tr = _pow2_rows(rows, cols * max(1, p // 2))
    nb = rows // tr
    c1 = 1.0 - ADAM_B1 ** ADAM_STEP
    c2 = 1.0 - ADAM_B2 ** ADAM_STEP

    def body(w_ref, m_ref, v_ref, g_ref, *rest):
        go_ref, do_ref, mo_ref, vo_ref = rest[-4:]
        g = g_ref[0].astype(F32)
        for k in range(1, p):
            g = g + g_ref[k].astype(F32)
        m2 = ADAM_B1 * m_ref[...] + (1.0 - ADAM_B1) * g
        v2 = ADAM_B2 * v_ref[...] + (1.0 - ADAM_B2) * (g * g)
        go_ref[...] = g
        do_ref[...] = -ADAM_LR * ((m2 / c1) / (jnp.sqrt(v2 / c2) + ADAM_EPS) + ADAM_WD * w_ref[...])
        mo_ref[...] = m2
        vo_ref[...] = v2

    row = pl.BlockSpec((tr, cols), lambda i: (layer * nb + i, 0))
    held = list(earlier) if earlier is not None else []
    return pl.pallas_call(
        body, name=name, grid=(nb,),
        in_specs=[row, row, row, pl.BlockSpec((p, tr, cols), lambda i: (0, i, 0))] + [ANY] * len(held),
        out_specs=[row] * 4, out_shape=[jax.ShapeDtypeStruct((layers * rows, cols), F32)] * 4,
        input_output_aliases={4 + k: k for k in range(len(held))},
        compiler_params=_cp(("parallel",)))(w, m, v, parts, *held)


PACK_ROWS = 512


def _pack(arrays):
    rows = []
    for a in arrays:
        flat = a.astype(F32).reshape(-1)
        rows.append(jnp.pad(flat, (0, (-flat.shape[0]) % LANE)).reshape(-1, LANE))
    rows = jnp.concatenate(rows)
    return jnp.pad(rows, ((0, (-rows.shape[0]) % PACK_ROWS), (0, 0)))


def _unpack(packed, shapes):
    out, row = [], 0
    for shp in shapes:
        n = math.prod(shp)
        nrows = -(-n // LANE)
        out.append(packed[row:row + nrows].reshape(-1)[:n].reshape(shp))
        row += nrows
    return out


def _unshard(gathered, axis):
    block = gathered.shape[2:]
    full = jnp.swapaxes(gathered, 0, 1).reshape((N_DEV,) + block)
    full = jnp.moveaxis(full, 0, axis)
    return full.reshape(block[:axis] + (N_DEV * block[axis],) + block[axis + 1:])


def kernel(x, ln_in_g, ln_in_b, w_in, b_forget, conv_w, conv_b, w_r, b_r, w_i, b_i, lru_lambda, rel_bias, w_branch, w_gate, b_gate, w_out, ln1_g, ln1_b, w_ff1, w_ff2, ln2_g, ln2_b, loss_target, m_ln_in_g, m_ln_in_b, m_w_in, m_b_forget, m_conv_w, m_conv_b, m_w_r, m_b_r, m_w_i, m_b_i, m_lru_lambda, m_rel_bias, m_w_branch, m_w_gate, m_b_gate, m_w_out, m_ln1_g, m_ln1_b, m_w_ff1, m_w_ff2, m_ln2_g, m_ln2_b, v_ln_in_g, v_ln_in_b, v_w_in, v_b_forget, v_conv_w, v_conv_b, v_w_r, v_b_r, v_w_i, v_b_i, v_lru_lambda, v_rel_bias, v_w_branch, v_w_gate, v_b_gate, v_w_out, v_ln1_g, v_ln1_b, v_w_ff1, v_w_ff2, v_ln2_g, v_ln2_b):
    given = dict(zip(
        NAMES + ['loss_target'] + ['m_' + n for n in WEIGHTS] + ['v_' + n for n in WEIGHTS],
        (x, ln_in_g, ln_in_b, w_in, b_forget, conv_w, conv_b, w_r, b_r, w_i, b_i, lru_lambda, rel_bias, w_branch, w_gate, b_gate, w_out, ln1_g, ln1_b, w_ff1, w_ff2, ln2_g, ln2_b, loss_target, m_ln_in_g, m_ln_in_b, m_w_in, m_b_forget, m_conv_w, m_conv_b, m_w_r, m_b_r, m_w_i, m_b_i, m_lru_lambda, m_rel_bias, m_w_branch, m_w_gate, m_b_gate, m_w_out, m_ln1_g, m_ln1_b, m_w_ff1, m_w_ff2, m_ln2_g, m_ln2_b, v_ln_in_g, v_ln_in_b, v_w_in, v_b_forget, v_conv_w, v_conv_b, v_w_r, v_b_r, v_w_i, v_b_i, v_lru_lambda, v_rel_bias, v_w_branch, v_w_gate, v_b_gate, v_w_out, v_ln1_g, v_ln1_b, v_w_ff1, v_w_ff2, v_ln2_g, v_ln2_b)))

    s, d = x.shape[1], x.shape[2]
    nl = w_in.shape[0]
    bw = d // 4
    nh = bw // HEAD
    nu = 11 * bw
    rs = d // N_DEV
    dff = w_ff1.shape[2] * N_DEV
    fs = dff // N_DEV
    cs = d // N_DEV
    assert nl == DEPTH and nh * HEAD == bw and s % 1024 == 0 and d % 1024 == 0

    xi, yi, ci = _position()
    dev = 4 * xi + 2 * yi + ci
    c_arr = jnp.reshape(ci, (1,)).astype(I32)

    w_main = jnp.concatenate(
        [w_in[..., :3 * bw], w_in[..., 3 * bw + nh:],
         jnp.pad(w_in[..., 3 * bw:3 * bw + nh], ((0, 0), (0, 0), (0, LANE - nh)))], axis=-1).astype(BF16)
    nue = nu + LANE
    small_shapes = [conv_w.shape, rel_bias.shape, b_gate.shape]
    shard = {'main': w_main, 'branch': w_branch.astype(BF16), 'gate': w_gate.astype(BF16),
             'out': w_out.astype(BF16), 'ff1': w_ff1.astype(BF16), 'ff2': w_ff2.astype(BF16)}
    shard_axis = {'main': 1, 'branch': 3, 'gate': 2, 'out': 1, 'ff1': 2, 'ff2': 1}
    W = [dict() for _ in range(nl)]

    def gather_chips(l, keys, extra=()):
        side = _chips_side([shard[k][l:l + 1] for k in keys] + list(extra), True)
        side.todo = (l, keys)
        return side

    def gather_cores(chips):
        side = _cores_side(chips.results, True)
        side.todo = chips.todo
        return side

    def arrived(cores):
        l, keys = cores.todo
        for k, res in zip(keys, cores.results):
            W[l][k] = _unshard(res, shard_axis[k])
        return cores.results[len(keys):]

    xs = x[0]
    first = gather_chips(0, ['main'], [_pack([conv_w, rel_bias, b_gate])])
    h0, h0b = _ln_fwd("ln_in", xs, ln_in_g, ln_in_b, side=first)
    first = gather_cores(first)
    _run_side("gather_cores_first", first)
    small = arrived(first)[0]
    small = jnp.swapaxes(small, 0, 1).reshape((N_DEV,) + small.shape[2:])
    small = [_unpack(small[j], small_shapes) for j in range(N_DEV)]
    conv_w_full = jnp.concatenate([small[j][0] for j in range(N_DEV)], axis=-1)
    rel_bias_full = jnp.concatenate([small[j][1] for j in range(N_DEV)], axis=-1)
    b_gate_full = jnp.concatenate([small[j][2] for j in range(N_DEV)], axis=-1)
    b_gate4 = b_gate_full.reshape(nl, 4, 1, d)

    def lru_params(l):
        return (jnp.pad(conv_w_full[l], ((0, 8 - CONV_WIDTH), (0, 0))), conv_b[l].reshape(1, bw),
                w_r[l].astype(BF16), b_r[l].reshape(1, bw), w_i[l].astype(BF16), b_i[l].reshape(1, bw),
                lru_lambda[l].reshape(1, bw))

    def bias_rows(l):
        return jnp.pad(rel_bias_full[l], ((0, 16 - nh), (0, REL_PAD - REL_TABLE)))

    tm = min(1024, s)
    tkk = min(2048, d)

    saved = []
    cur, curb = h0, h0b
    chips = {}
    for l in range(nl):
        side = None
        if l == 0:
            side = chips['b0'] = gather_chips(0, ['gate', 'branch'])
        else:
            side = last_cores = gather_cores(chips.pop('d1'))
        u = _mm(f"w_in_{l}", curb, W[l]['main'], grid=(s // tm, nu // bw, d // tkk),
                a_spec=pl.BlockSpec((tm, tkk), lambda m, n, k: (m, k)),
                b_spec=pl.BlockSpec((None, tkk, bw), lambda m, n, k: (0, k, n)),
                dims=NN, acc_shape=(tm, bw), out_shape=[jax.ShapeDtypeStruct((s, nu), BF16)],
                out_specs=[pl.BlockSpec((tm, bw), lambda m, n, k: (m, n))], finish=_store(BF16), side=side)[0]
        if l == 1:
            arrived(last_cores)
        fl = _mm(f"w_forget_{l}", curb, W[l]['main'], grid=(s // tm, 1, d // tkk),
                 a_spec=pl.BlockSpec((tm, tkk), lambda m, n, k: (m, k)),
                 b_spec=pl.BlockSpec((None, tkk, LANE), lambda m, n, k: (0, k, nu // LANE)),
                 dims=NN, acc_shape=(tm, LANE), out_shape=[jax.ShapeDtypeStruct((s, LANE), F32)],
                 out_specs=[pl.BlockSpec((tm, LANE), lambda m, n, k: (m, 0))], finish=_store(F32))[0]
        bf_row = jnp.pad(b_forget[l], (0, LANE - nh)).reshape(1, LANE)
        cf = _cum_forget_fwd(f"cum_forget_{l}", fl, bf_row)
        tkf = _fox_tile(s)[1]
        cf_heads = cf[:, :nh].T
        cf_col = cf_heads.reshape(nh, s, 1)
        cf_row = cf_heads.reshape(nh, s // tkf, 1, tkf)
        side = None
        if l == 0:
            b0 = gather_cores(chips.pop('b0'))
            chips['c0'] = gather_chips(0, ['out', 'ff1', 'ff2'])
            side = _merge_sides(b0, chips['c0'])
        o_fox, lse = _fox_fwd(f"fox_fwd_{l}", u, cf_col, cf_row, bw, side=side)
        if l == 0:
            arrived(b0)
        lp = lru_params(l)
        o_lru, hstate = _lru_fwd(f"lru_fwd_{l}", u, lp, bw)
        o_sb = _sb_fwd(f"sb_fwd_{l}", u, bw)
        bias = _bias_expand(f"bias_expand_{l}", bias_rows(l), nh)
        kpad = jnp.pad(u[:, CK * bw:(CK + 1) * bw], ((PADK, 0), (0, 0)))
        vpad = jnp.pad(u[:, CV * bw:(CV + 1) * bw], ((PADK, 0), (0, 0)))
        side = gather_cores(chips.pop('c0')) if l == 0 else None
        o_ch = _chunk_fwd(f"chunk_fwd_{l}", u, kpad, vpad, bias, bw, side=side)
        if l == 0:
            arrived(side)
        o_all = jnp.stack([o_fox, o_lru, o_sb, o_ch])
        side = None
        if l == 0:
            side = chips['a1'] = gather_chips(1, ['main', 'gate', 'branch'])
        merged, gates, projs = _gate_merge(f"gate_merge_{l}", curb, W[l]['gate'], b_gate4, o_all, W[l]['branch'], l,
                                           side=side)
        side = gather_cores(chips.pop('a1')) if l == 0 else None
        z1, x1, x1b = _mm_ln(f"w_out_ln1_{l}", merged, W[l]['out'], 0, cur, ln1_g[l], ln1_b[l], side=side)
        if l == 0:
            arrived(side)
        tn1 = min(1024, dff)

        def ff1_finish(acc, ex, outs, ids):
            outs[0][...] = acc.astype(BF16)
            r = jnp.maximum(acc, 0.0)
            outs[1][...] = (r * r).astype(BF16)

        side = None
        if l == 0:
            side = chips['c1'] = gather_chips(1, ['out', 'ff1'])
        hp, hid = _mm(f"w_ff1_{l}", x1b, W[l]['ff1'], grid=(s // tm, dff // tn1, d // tkk),
                      a_spec=pl.BlockSpec((tm, tkk), lambda m, n, k: (m, k)),
                      b_spec=pl.BlockSpec((None, tkk, tn1), lambda m, n, k: (0, k, n)),
                      dims=NN, acc_shape=(tm, tn1),
                      out_shape=[jax.ShapeDtypeStruct((s, dff), BF16)] * 2,
                      out_specs=[pl.BlockSpec((tm, tn1), lambda m, n, k: (m, n))] * 2, finish=ff1_finish, side=side)
        side = None
        if l == 0:
            c1 = gather_cores(chips.pop('c1'))
            chips['d1'] = gather_chips(1, ['ff2'])
            side = _merge_sides(c1, chips['d1'])
        z2, x2, x2b = _mm_ln(f"w_ff2_ln2_{l}", hid, W[l]['ff2'], 0, x1, ln2_g[l], ln2_b[l], side=side)
        if l == 0:
            arrived(c1)
        saved.append(dict(xin=cur, xinb=curb, u=u, fl=fl, bf_row=bf_row, cf_col=cf_col, cf_row=cf_row, o_fox=o_fox,
                          lse=lse, lp=lp, hstate=hstate, bias=bias, kpad=kpad, vpad=vpad, o_all=o_all, merged=merged,
                          gates=gates, projs=projs, z1=z1, x1=x1, x1b=x1b, hp=hp, hid=hid, z2=z2))
        cur, curb = x2, x2b

    loss_tile, *last_ln = _loss_ln_bwd("loss_ln2_bwd", cur, loss_target[0], saved[nl - 1]['z2'], ln2_g[nl - 1])
    loss = lax.psum(loss_tile[0, 0], ("x", "y", "c"))
    dcur = None

    big = [dict() for _ in range(nl)]
    reduced = [dict() for _ in range(nl)]
    sm = {n: [None] * nl for n in ['b_forget', 'conv_w', 'conv_b', 'w_r', 'b_r', 'w_i', 'b_i', 'lru_lambda', 'rel_bias',
                                   'b_gate', 'ln1_g', 'ln1_b', 'ln2_g', 'ln2_b']}

    def split_columns(acc, ex, outs, ids):
        for j in range(N_DEV):
            outs[0][j] = acc[:, j * cs:(j + 1) * cs].astype(BF16)

    def grad_mm(key, name, a, b, *, shape, grid, a_spec, b_spec, out_spec, acc_shape, finish=_store(BF16)):
        l = int(name[-1])
        big[l][key] = _mm(name, a, b, grid=grid, a_spec=a_spec, b_spec=b_spec, dims=TN, acc_shape=acc_shape,
                          out_shape=[jax.ShapeDtypeStruct(shape, BF16)], out_specs=[out_spec], finish=finish)[0]

    def reduce_cores(l, keys):
        side = _cores_side([big[l][k].reshape((4, 2) + big[l][k].shape[1:]) for k in keys], False)
        side.todo = (l, keys)
        return side

    def reduce_chips(cores):
        l, keys = cores.todo
        partial = []
        for k, mine, other in zip(keys, cores.operands, cores.results):
            cols = mine.shape[-1]
            rows = math.prod(mine.shape[2:]) // cols
            partial.append(_add_core_halves(f"add_cores_{k}_{l}", mine.reshape(4, 2, rows, cols),
                                            other.reshape(4, rows, cols), c_arr, BF16))
        side = _chips_side(partial, False)
        side.todo = (l, keys)
        return side

    def reduction_done(chips_side):
        l, keys = chips_side.todo
        reduced[l].update(zip(keys, chips_side.results))

    GROUP1, GROUP2 = ['w_ff2', 'w_ff1', 'w_out', 'w_branch', 'w_gate'], ['w_main']
    pending = None

    tks = min(1024, s)
    tmr = min(1024, d)
    nsh = tmr // rs

    for l in reversed(range(nl)):
        sv = saved[l]
        Wl = W[l]
        dz2, dz2b, dg, db = last_ln if l == nl - 1 else _ln_bwd(f"ln2_bwd_{l}", dcur, sv['z2'], ln2_g[l])
        sm['ln2_g'][l], sm['ln2_b'][l] = dg[0], db[0]
        tn1 = min(1024, dff)

        def dhp_finish(acc, ex, outs, ids):
            outs[0][...] = (acc * (2.0 * jnp.maximum(ex[0][...].astype(F32), 0.0))).astype(BF16)

        dhp = _mm(f"d_hidden_{l}", dz2b, Wl['ff2'], grid=(s // tm, dff // tn1, d // tkk),
                  a_spec=pl.BlockSpec((tm, tkk), lambda m, n, k: (m, k)),
                  b_spec=pl.BlockSpec((None, tn1, tkk), lambda m, n, k: (0, n, k)),
                  dims=NT, acc_shape=(tm, tn1), out_shape=[jax.ShapeDtypeStruct((s, dff), BF16)],
                  out_specs=[pl.BlockSpec((tm, tn1), lambda m, n, k: (m, n))], finish=dhp_finish,
                  extras=(sv['hp'],), extra_specs=(pl.BlockSpec((tm, tn1), lambda m, n, k: (m, n)),),
                  side=pending)[0]
        if pending is not None:
            reduction_done(pending)
            pending = None
        grad_mm('w_ff2', f"g_w_ff2_{l}", sv['hid'], dz2b, shape=(N_DEV, 1, fs, d), grid=(N_DEV, 1, s // tks),
                a_spec=pl.BlockSpec((tks, fs), lambda m, n, k: (k, m)),
                b_spec=pl.BlockSpec((tks, d), lambda m, n, k: (k, 0)),
                out_spec=pl.BlockSpec((None, None, fs, d), lambda m, n, k: (m, 0, 0, 0)), acc_shape=(fs, d))
        grad_mm('w_ff1', f"g_w_ff1_{l}", sv['x1b'], dhp, shape=(N_DEV, 1, d, fs), grid=(d // tmr, N_DEV, s // tks),
                a_spec=pl.BlockSpec((tks, tmr), lambda m, n, k: (k, m)),
                b_spec=pl.BlockSpec((tks, fs), lambda m, n, k: (k, n)),
                out_spec=pl.BlockSpec((None, None, tmr, fs), lambda m, n, k: (n, 0, m, 0)), acc_shape=(tmr, fs))
        tnd = min(1024, d)

        def resid_finish(scale):
            def finish(acc, ex, outs, ids):
                outs[0][...] = acc + scale * ex[0][...]
            return finish

        tile_md = pl.BlockSpec((tm, tnd), lambda m, n, k: (m, n))
        dx1 = _mm(f"d_x1_{l}", dhp, Wl['ff1'], grid=(s // tm, d // tnd, dff // tkk),
                  a_spec=pl.BlockSpec((tm, tkk), lambda m, n, k: (m, k)),
                  b_spec=pl.BlockSpec((None, tnd, tkk), lambda m, n, k: (0, n, k)),
                  dims=NT, acc_shape=(tm, tnd), out_shape=[jax.ShapeDtypeStruct((s, d), F32)],
                  out_specs=[tile_md], finish=resid_finish(ALPHA), extras=(dz2,), extra_specs=(tile_md,))[0]

        dz1, dz1b, dg, db = _ln_bwd(f"ln1_bwd_{l}", dx1, sv['z1'], ln1_g[l])
        sm['ln1_g'][l], sm['ln1_b'][l] = dg[0], db[0]
        tmg, tng = min(512, s), min(512, d)

        def gate_finish(acc, ex, outs, ids):
            @pl.when(ids[1] == 0)
            def _():
                outs[2][...] = jnp.zeros(outs[2].shape, F32)

            ones = jnp.ones((8, tmg), BF16)
            for g in range(4):
                dpre = (acc * ex[1][g].astype(F32)).astype(BF16)
                outs[0][g] = (acc * ex[0][g].astype(F32)).astype(BF16)
                outs[1][g] = dpre
                outs[2][g] += jnp.dot(ones, dpre, preferred_element_type=F32)

        quad = pl.BlockSpec((4, tmg, tng), lambda n, m, k: (0, m, n))
        dproj, dpre, dbg = _mm(
            f"d_merged_{l}", dz1b, Wl['out'], grid=(d // tng, s // tmg, d // tkk),
            a_spec=pl.BlockSpec((tmg, tkk), lambda n, m, k: (m, k)),
            b_spec=pl.BlockSpec((None, tng, tkk), lambda n, m, k: (0, n, k)),
            dims=NT, acc_shape=(tmg, tng),
            out_shape=[jax.ShapeDtypeStruct((4, s, d), BF16), jax.ShapeDtypeStruct((4, s, d), BF16),
                       jax.ShapeDtypeStruct((4, 8, d), F32)],
            out_specs=[quad, quad, pl.BlockSpec((4, 8, tng), lambda n, m, k: (0, 0, n))], finish=gate_finish,
            extras=(sv['gates'], sv['projs']), extra_specs=(quad, quad), sem=("arbitrary", "arbitrary", "arbitrary"))
        sm['b_gate'][l] = dbg[:, 0, :]
        grad_mm('w_out', f"g_w_out_{l}", sv['merged'], dz1b, shape=(N_DEV, 1, rs, d), grid=(d // tmr, 1, s // tks),
                a_spec=pl.BlockSpec((tks, tmr), lambda m, n, k: (k, m)),
                b_spec=pl.BlockSpec((tks, d), lambda m, n, k: (k, 0)),
                out_spec=pl.BlockSpec((nsh, None, rs, d), lambda m, n, k: (m, 0, 0, 0)), acc_shape=(tmr, d))

        nm = s // tm
        do_all = _mm(f"d_branch_{l}", dproj, Wl['branch'], grid=(4 * nm, 1, d // tkk),
                     a_spec=pl.BlockSpec((None, tm, tkk), lambda m, n, k: (m // nm, m % nm, k)),
                     b_spec=pl.BlockSpec((None, None, bw, tkk), lambda m, n, k: (0, m // nm, 0, k)),
                     dims=NT, acc_shape=(tm, bw), out_shape=[jax.ShapeDtypeStruct((4, s, bw), BF16)],
                     out_specs=[pl.BlockSpec((None, tm, bw), lambda m, n, k: (m // nm, m % nm, 0))],
                     finish=_store(BF16))[0]
        grad_mm('w_branch', f"g_w_branch_{l}", sv['o_all'], dproj, shape=(N_DEV, 1, 4, bw, cs),
                grid=(4, 1, s // tks), finish=split_columns,
                a_spec=pl.BlockSpec((None, tks, bw), lambda m, n, k: (m, k, 0)),
                b_spec=pl.BlockSpec((None, tks, d), lambda m, n, k: (m, k, 0)),
                out_spec=pl.BlockSpec((N_DEV, None, None, bw, cs), lambda m, n, k: (0, 0, m, 0, 0)),
                acc_shape=(bw, d))
        grad_mm('w_gate', f"g_w_gate_{l}", sv['xinb'], dpre, shape=(N_DEV, 1, 4, rs, d), grid=(d // tmr, 4, s // tks),
                a_spec=pl.BlockSpec((tks, tmr), lambda m, n, k: (k, m)),
                b_spec=pl.BlockSpec((None, tks, d), lambda m, n, k: (n, k, 0)),
                out_spec=pl.BlockSpec((nsh, None, None, rs, d), lambda m, n, k: (m, 0, n, 0, 0)),
                acc_shape=(tmr, d))
        nkg = d // tkk
        cores1 = reduce_cores(l, GROUP1)
        dx_gate = _mm(f"d_x_gates_{l}", dpre, Wl['gate'], grid=(s // tm, d // tnd, 4 * nkg),
                      a_spec=pl.BlockSpec((None, tm, tkk), lambda m, n, k: (k // nkg, m, k % nkg)),
                      b_spec=pl.BlockSpec((None, None, tnd, tkk), lambda m, n, k: (0, k // nkg, n, k % nkg)),
                      dims=NT, acc_shape=(tm, tnd), out_shape=[jax.ShapeDtypeStruct((s, d), F32)],
                      out_specs=[tile_md], finish=resid_finish(ALPHA), extras=(dz1,), extra_specs=(tile_md,),
                      side=cores1)[0]

        u = sv['u']
        chips1 = reduce_chips(cores1)
        dfq, dfk, dfv, dcc, dcr = _fox_bwd(f"fox_bwd_{l}", u, sv['cf_col'], sv['cf_row'], sv['o_fox'], do_all[0],
                                          sv['lse'], bw, side=chips1)
        reduction_done(chips1)
        dcf = (dcc[:, :, 0, :].reshape(nh, s) + dcr.reshape(nh, s)).T
        dflb, dbf = _cum_forget_bwd(f"cum_forget_bwd_{l}", jnp.pad(dcf, ((0, 0), (0, LANE - nh))), sv['fl'],
                                    sv['bf_row'])
        sm['b_forget'][l] = dbf[0, :nh]
        drx, dry, dcw, dcb, dwr, dbr, dwi, dbi, dlam = _lru_bwd(f"lru_bwd_{l}", u, sv['hstate'], do_all[1], sv['lp'], bw)
        sm['conv_w'][l], sm['conv_b'][l], sm['w_r'][l], sm['b_r'][l] = dcw[:CONV_WIDTH], dcb[0], dwr, dbr[0]
        sm['w_i'][l], sm['b_i'][l], sm['lru_lambda'][l] = dwi, dbi[0], dlam[0]
        dsq, dsk, dsv = _sb_bwd(f"sb_bwd_{l}", u, do_all[2], bw)
        dcq, dck, dcv, dss = _chunk_bwd(f"chunk_bwd_{l}", u, sv['kpad'], sv['vpad'], sv['bias'], do_all[3], bw)
        sm['rel_bias'][l] = _bias_reduce(f"bias_reduce_{l}", dss, nh)[:, 0, :REL_TABLE]
        du = jnp.concatenate([dfq, dfk, dfv, drx, dry, dsq, dsk, dsv, dcq, dck, dcv, dflb], axis=1)

        tnu = _lane_tile(nue)
        grad_mm('w_main', f"g_w_in_{l}", sv['xinb'], du, shape=(N_DEV, 1, rs, nue),
                grid=(d // tmr, nue // tnu, s // tks),
                a_spec=pl.BlockSpec((tks, tmr), lambda m, n, k: (k, m)),
                b_spec=pl.BlockSpec((tks, tnu), lambda m, n, k: (k, n)),
                out_spec=pl.BlockSpec((nsh, None, rs, tnu), lambda m, n, k: (m, 0, 0, n)), acc_shape=(tmr, tnu))
        cores2 = reduce_cores(l, GROUP2)
        dcur = _mm(f"d_x_in_{l}", du, Wl['main'], grid=(s // tm, d // tnd, nue // tnu),
                   a_spec=pl.BlockSpec((tm, tnu), lambda m, n, k: (m, k)),
                   b_spec=pl.BlockSpec((None, tnd, tnu), lambda m, n, k: (0, n, k)),
                   dims=NT, acc_shape=(tm, tnd), out_shape=[jax.ShapeDtypeStruct((s, d), F32)],
                   out_specs=[tile_md], finish=resid_finish(1.0), extras=(dx_gate,), extra_specs=(tile_md,),
                   side=cores2)[0]
        pending = reduce_chips(cores2)

    grad_x, _, dg_in, db_in = _ln_bwd("ln_in_bwd", dcur, xs, ln_in_g, side=pending)
    reduction_done(pending)

    small_names = ['ln_in_g', 'ln_in_b', 'b_forget', 'conv_w', 'conv_b', 'w_r', 'b_r', 'w_i', 'b_i', 'lru_lambda',
                   'rel_bias', 'b_gate', 'ln1_g', 'ln1_b', 'ln2_g', 'ln2_b']
    local = {'ln_in_g': dg_in[0], 'ln_in_b': db_in[0]}
    for n in small_names[2:]:
        local[n] = jnp.stack(sm[n])
    full_shapes = [local[n].shape for n in small_names]
    packed = _pack([local[n] for n in small_names])
    every = _run_side("gather_small_cores",
                      _cores_side(_run_side("gather_small_chips", _chips_side([packed], True)), True))[0]
    every = jnp.swapaxes(every, 0, 1).reshape((N_DEV,) + packed.shape)
    total = dict(zip(small_names, _unpack(_sum_parts("sum_small", every), full_shapes)))
    for n, width in (('conv_w', bw // N_DEV), ('rel_bias', REL_TABLE // N_DEV), ('b_gate', cs)):
        total[n] = lax.dynamic_slice_in_dim(total[n], dev * width, width, axis=2)

    out = {}

    def update(n, parts):
        cols = parts[0].shape[2]
        w2, m2, v2 = (given[p + n].reshape(-1, cols) for p in ('', 'm_', 'v_'))
        res = None
        for l in range(nl):
            res = _adamw(f"adamw_{n}_{l}", w2, m2, v2, parts[l], layer=l, layers=nl, earlier=res)
        out[n] = [r.reshape(given[n].shape) for r in res]

    def parts_w_in(l):
        pm = reduced[l]['w_main']
        return jnp.concatenate([pm[..., :3 * bw], pm[..., nu:nu + nh], pm[..., 3 * bw:nu]], axis=-1)

    update('w_in', [parts_w_in(l) for l in range(nl)])
    for n in ('w_branch', 'w_gate', 'w_out', 'w_ff1', 'w_ff2'):
        update(n, [reduced[l][n] for l in range(nl)])

    small_shapes2 = [given[n].shape for n in small_names]
    res = _adamw("adamw_small", _pack([given[n] for n in small_names]), _pack([given['m_' + n] for n in small_names]),
                 _pack([given['v_' + n] for n in small_names]), _pack([total[n] for n in small_names])[None])
    res = [_unpack(r, small_shapes2) for r in res]
    for j, n in enumerate(small_names):
        out[n] = [res[k][j] for k in range(4)]

    return (loss, grad_x[None], *[out[n][0] for n in WEIGHTS], *[out[n][1] for n in WEIGHTS],
            *[out[n][2] for n in WEIGHTS], *[out[n][3] for n in WEIGHTS])
```

```python
import functools
import math

import jax
import jax.numpy as jnp
from jax import lax
from jax.experimental import pallas as pl
from jax.experimental.pallas import tpu as pltpu

F32, BF16, I32 = jnp.float32, jnp.bfloat16, jnp.int32
MESH = pl.DeviceIdType.MESH
ANY = pl.BlockSpec(memory_space=pl.ANY)

LANE = 128
VMEM_LIMIT = 56 * 1024 * 1024
N_DEV = 8

HEAD = 128
CHUNK = 64
LOOKBACK = 8
BAND = (LOOKBACK + 1) * CHUNK
QCHUNKS = 4
QBLK = QCHUNKS * CHUNK
WIN = BAND + (QCHUNKS - 1) * CHUNK
PADK = LOOKBACK * CHUNK
REL_CLIP = 256
REL_TABLE = REL_CLIP + CHUNK
REL_PAD = 384
CONV_WIDTH = 4
LRU_C = 8.0
LN_EPS = 1e-5
DEPTH = 2
ALPHA = (2.0 * DEPTH) ** 0.25
NEG = -1e30
SB_DEAD = -104.0
GELU_K = math.sqrt(2.0 / math.pi)
GELU_C = 0.044715

ADAM_LR, ADAM_B1, ADAM_B2, ADAM_EPS, ADAM_WD, ADAM_STEP = 0.001, 0.9, 0.999, 1e-08, 0.01, 10

NN = (((1,), (0,)), ((), ()))
NT = (((1,), (1,)), ((), ()))
TN = (((0,), (0,)), ((), ()))

FQ, FK, FV, RX, RY, SQ, SK, SV, CQ, CK, CV = range(11)

NAMES = ['x', 'ln_in_g', 'ln_in_b', 'w_in', 'b_forget', 'conv_w', 'conv_b', 'w_r', 'b_r', 'w_i', 'b_i', 'lru_lambda',
         'rel_bias', 'w_branch', 'w_gate', 'b_gate', 'w_out', 'ln1_g', 'ln1_b', 'w_ff1', 'w_ff2', 'ln2_g', 'ln2_b']
WEIGHTS = NAMES[1:]


def _cp(sem=None):
    return pltpu.CompilerParams(dimension_semantics=sem, vmem_limit_bytes=VMEM_LIMIT)


def _iota(shape, dim):
    return lax.broadcasted_iota(I32, shape, dim)


def _sigmoid(x):
    return 1.0 / (1.0 + jnp.exp(-x))


def _log_sigmoid(x):
    return jnp.minimum(x, 0.0) - jnp.log(1.0 + jnp.exp(-jnp.abs(x)))


def _lane_tile(n, cap=1536):
    best = max(t for t in range(LANE, cap + 1, LANE) if n % t == 0)
    return n if best == LANE else best


def _pow2_rows(rows, cols, elems=262144):
    t = 8
    while t * 2 <= rows and t * 2 * cols <= elems and rows % (t * 2) == 0:
        t *= 2
    return t


def _position():
    return lax.axis_index("x"), lax.axis_index("y"), lax.axis_index("c")


class _Side:
    def __init__(self, operands, out_shape, sems, start, finish, aliases=(), parts=()):
        self.operands, self.out_shape, self.sems = list(operands), list(out_shape), list(sems)
        self.start, self.finish, self.aliases, self.parts = start, finish, list(aliases), parts
        self.results = None

    def set_results(self, res):
        self.results = list(res)
        off = 0
        for part in self.parts:
            part.set_results(res[off:off + len(part.out_shape)])
            off += len(part.out_shape)


def _merge_sides(a, b):
    ai, ao, asm = len(a.operands), len(a.out_shape), len(a.sems)

    def start(ins, outs, sems):
        a.start(ins[:ai], outs[:ao], sems[:asm])
        b.start(ins[ai:], outs[ao:], sems[asm:])

    def finish(ins, outs, sems):
        a.finish(ins[:ai], outs[:ao], sems[:asm])
        b.finish(ins[ai:], outs[ao:], sems[asm:])

    return _Side(a.operands + b.operands, a.out_shape + b.out_shape, a.sems + b.sems, start, finish,
                 a.aliases + [(i + ai, o + ao) for i, o in b.aliases], parts=(a, b))


def _chips_side(xs, gather):
    n = len(xs)

    def copies(ins, outs, sems, arrivals):
        send_sems, recv_sems, local_sems = sems
        x, y, c = _position()
        q = 2 * x + y
        chips = [(1 - x, y), (x, 1 - y), (1 - x, 1 - y)]

        def src(t, slot):
            return ins[t] if gather else ins[t].at[slot]

        def dst(t, slot):
            return outs[t].at[c, slot] if gather else outs[t].at[slot]

        def remote(t, j, landing):
            px, py = chips[j]
            return pltpu.make_async_remote_copy(
                src_ref=src(t, 2 * px + py), dst_ref=dst(t, landing), send_sem=send_sems.at[t, j],
                recv_sem=recv_sems.at[t, j], device_id=(px, py, c), device_id_type=MESH)

        local = [pltpu.make_async_copy(src(t, q), dst(t, q), local_sems.at[t]) for t in range(n)]
        sends = [remote(t, j, q) for t in range(n) for j in range(3)]
        if not arrivals:
            return local, sends, []
        return local, sends, [remote(t, j, 2 * px + py) for t in range(n) for j, (px, py) in enumerate(chips)]

    def start(ins, outs, sems):
        local, sends, _ = copies(ins, outs, sems, False)
        for cp in local + sends:
            cp.start()

    def finish(ins, outs, sems):
        local, sends, recvs = copies(ins, outs, sems, True)
        for cp in recvs:
            cp.wait_recv()
        for cp in sends:
            cp.wait_send()
        for cp in local:
            cp.wait()

    out_shape = [jax.ShapeDtypeStruct((2, 4) + a.shape if gather else a.shape, a.dtype) for a in xs]
    sems = [pltpu.SemaphoreType.DMA((n, 3)), pltpu.SemaphoreType.DMA((n, 3)), pltpu.SemaphoreType.DMA((n,))]
    return _Side(xs, out_shape, sems, start, finish)


def _cores_side(xs, gather):
    n = len(xs)
    m = 1 if gather else 4

    def copies(ins, outs, sems, arrivals):
        send_sems, recv_sems = sems
        x, y, c = _position()

        def remote(t, j, landing):
            s = outs[t].at[c] if gather else ins[t].at[j, 1 - c]
            d = outs[t].at[landing] if gather else outs[t].at[j]
            return pltpu.make_async_remote_copy(
                src_ref=s, dst_ref=d, send_sem=send_sems.at[t, j], recv_sem=recv_sems.at[t, j],
                device_id=(x, y, 1 - c), device_id_type=MESH)

        sends = [remote(t, j, c) for t in range(n) for j in range(m)]
        return sends, [remote(t, j, 1 - c) for t in range(n) for j in range(m)] if arrivals else []

    def start(ins, outs, sems):
        for cp in copies(ins, outs, sems, False)[0]:
            cp.start()

    def finish(ins, outs, sems):
        sends, recvs = copies(ins, outs, sems, True)
        for cp in recvs:
            cp.wait_recv()
        for cp in sends:
            cp.wait_send()

    if gather:
        out_shape = [jax.ShapeDtypeStruct(a.shape, a.dtype) for a in xs]
    else:
        out_shape = [jax.ShapeDtypeStruct((4,) + a.shape[2:], a.dtype) for a in xs]
    sems = [pltpu.SemaphoreType.DMA((n, m)), pltpu.SemaphoreType.DMA((n, m))]
    return _Side(xs, out_shape, sems, start, finish, aliases=[(t, t) for t in range(n)] if gather else [])


def _run_side(name, side):
    ni, no = len(side.operands), len(side.out_shape)

    def body(*refs):
        ins, outs, sems = refs[:ni], refs[ni:ni + no], refs[ni + no:]
        side.start(ins, outs, sems)
        side.finish(ins, outs, sems)

    side.set_results(pl.pallas_call(
        body, name=name, out_shape=side.out_shape, in_specs=[ANY] * ni, out_specs=[ANY] * no,
        input_output_aliases=dict(side.aliases), scratch_shapes=side.sems)(*side.operands))
    return side.results


def _pcall(body, operands, *, name, grid, in_specs, out_specs, out_shape, scratch_shapes=(), sem=None, aliases=None,
           side=None):
    if side is None:
        return pl.pallas_call(body, name=name, grid=grid, in_specs=list(in_specs), out_specs=list(out_specs),
                              out_shape=list(out_shape), scratch_shapes=list(scratch_shapes),
                              input_output_aliases=aliases or {}, compiler_params=_cp(sem))(*operands)
    ni, no, ns = len(in_specs), len(out_shape), len(scratch_shapes)
    si, so = len(side.operands), len(side.out_shape)

    def carrying(*refs):
        ins, sins = refs[:ni], refs[ni:ni + si]
        outs, souts = refs[ni + si:ni + si + no], refs[ni + si + no:ni + si + no + so]
        scratch, ssems = refs[ni + si + no + so:ni + si + no + so + ns], refs[ni + si + no + so + ns:]
        ids = [pl.program_id(a) for a in range(len(grid))]
        first = functools.reduce(jnp.logical_and, [i == 0 for i in ids])
        last = functools.reduce(jnp.logical_and, [i == g - 1 for i, g in zip(ids, grid)])

        @pl.when(first)
        def _():
            side.start(sins, souts, ssems)

        body(*ins, *outs, *scratch)

        @pl.when(last)
        def _():
            side.finish(sins, souts, ssems)

    joined = dict(aliases or {})
    joined.update({ni + i: no + o for i, o in side.aliases})
    res = pl.pallas_call(
        carrying, name=name, grid=grid, in_specs=[*in_specs, *[ANY] * si], out_specs=[*out_specs, *[ANY] * so],
        out_shape=[*out_shape, *side.out_shape], scratch_shapes=[*scratch_shapes, *side.sems],
        input_output_aliases=joined, compiler_params=_cp(("arbitrary",) * len(grid)))(*operands, *side.operands)
    side.set_results(res[no:])
    return res[:no]


def _add_core_halves(name, mine, other, c, out_dtype):
    _, _, rows, cols = mine.shape
    tr = _pow2_rows(rows, cols)

    def body(c_ref, a_ref, b_ref, o_ref):
        o_ref[...] = (a_ref[...].astype(F32) + b_ref[...].astype(F32)).astype(out_dtype)

    grid_spec = pltpu.PrefetchScalarGridSpec(
        num_scalar_prefetch=1, grid=(4, rows // tr),
        in_specs=[pl.BlockSpec((None, None, tr, cols), lambda j, i, c_ref: (j, c_ref[0], i, 0)),
                  pl.BlockSpec((None, tr, cols), lambda j, i, c_ref: (j, i, 0))],
        out_specs=pl.BlockSpec((None, tr, cols), lambda j, i, c_ref: (j, i, 0)))
    return pl.pallas_call(body, name=name, grid_spec=grid_spec,
                          out_shape=jax.ShapeDtypeStruct((4, rows, cols), out_dtype),
                          compiler_params=_cp(("parallel", "parallel")))(c, mine, other)


def _sum_parts(name, parts):
    p, rows, cols = parts.shape
    tr = _pow2_rows(rows, cols * p)

    def body(a_ref, o_ref):
        acc = a_ref[0]
        for k in range(1, p):
            acc = acc + a_ref[k]
        o_ref[...] = acc

    return pl.pallas_call(body, name=name, grid=(rows // tr,),
                          in_specs=[pl.BlockSpec((p, tr, cols), lambda i: (0, i, 0))],
                          out_specs=pl.BlockSpec((tr, cols), lambda i: (i, 0)),
                          out_shape=jax.ShapeDtypeStruct((rows, cols), F32),
                          compiler_params=_cp(("parallel",)))(parts)


def _mm(name, a, b, *, grid, a_spec, b_spec, dims, acc_shape, out_shape, out_specs, finish,
        extras=(), extra_specs=(), aliases=None, sem=("parallel", "parallel", "arbitrary"), side=None):
    nk, ne, no = grid[2], len(extras), len(out_shape)

    def body(*refs):
        a_ref, b_ref = refs[0], refs[1]
        ex, outs = refs[2:2 + ne], refs[2 + ne:2 + ne + no]
        ids = (pl.program_id(0), pl.program_id(1))
        def prod():
            return lax.dot_general(a_ref[...], b_ref[...], dims, preferred_element_type=F32)

        if nk == 1:
            finish(prod(), ex, outs, ids)
            return
        acc = refs[2 + ne + no]
        k = pl.program_id(2)

        @pl.when(k == 0)
        def _():
            acc[...] = prod()

        @pl.when(jnp.logical_and(k > 0, k < nk - 1))
        def _():
            acc[...] += prod()

        @pl.when(k == nk - 1)
        def _():
            finish(acc[...] + prod(), ex, outs, ids)

    return _pcall(body, (a, b, *extras), name=name, grid=grid, in_specs=[a_spec, b_spec, *extra_specs],
                  out_specs=out_specs, out_shape=out_shape,
                  scratch_shapes=[pltpu.VMEM(acc_shape, F32)] if nk > 1 else [], sem=sem, aliases=aliases, side=side)


def _store(dtype):
    def finish(acc, ex, outs, ids):
        outs[0][...] = acc.reshape(outs[0].shape).astype(dtype)
    return finish


def _layer_norm_rows(z, g, b):
    mu = jnp.mean(z, axis=1, keepdims=True)
    zc = z - mu
    var = jnp.mean(zc * zc, axis=1, keepdims=True)
    return zc * lax.rsqrt(var + LN_EPS) * g + b


def _mm_ln(name, a, w, l, resid, g, b, side=None):
    s, kdim = a.shape
    d = w.shape[2]
    tm, tk = min(512, s), min(1024, kdim)

    def finish(acc, ex, outs, ids):
        z = acc + ALPHA * ex[0][...]
        y = _layer_norm_rows(z, ex[1][...], ex[2][...])
        outs[0][...] = z
        outs[1][...] = y
        outs[2][...] = y.astype(BF16)

    row = pl.BlockSpec((tm, d), lambda m, n, k: (m, 0))
    vec = pl.BlockSpec((1, d), lambda m, n, k: (0, 0))
    return _mm(name, a, w, grid=(s // tm, 1, kdim // tk),
               a_spec=pl.BlockSpec((tm, tk), lambda m, n, k: (m, k)),
               b_spec=pl.BlockSpec((None, tk, d), lambda m, n, k: (l, k, 0)),
               dims=NN, acc_shape=(tm, d),
               out_shape=[jax.ShapeDtypeStruct((s, d), F32), jax.ShapeDtypeStruct((s, d), F32),
                          jax.ShapeDtypeStruct((s, d), BF16)],
               out_specs=[row, row, row], finish=finish,
               extras=(resid, g.reshape(1, d), b.reshape(1, d)), extra_specs=(row, vec, vec), side=side)


def _ln_fwd(name, x, g, b, side=None):
    s, d = x.shape
    tr = min(256, s)

    def body(x_ref, g_ref, b_ref, y_ref, yb_ref):
        y = _layer_norm_rows(x_ref[...], g_ref[...], b_ref[...])
        y_ref[...] = y
        yb_ref[...] = y.astype(BF16)

    row = pl.BlockSpec((tr, d), lambda i: (i, 0))
    vec = pl.BlockSpec((1, d), lambda i: (0, 0))
    return _pcall(body, (x, g.reshape(1, d), b.reshape(1, d)), name=name, grid=(s // tr,), in_specs=[row, vec, vec],
                  out_specs=[row, row],
                  out_shape=[jax.ShapeDtypeStruct((s, d), F32), jax.ShapeDtypeStruct((s, d), BF16)],
                  sem=("parallel",), side=side)


def _ln_bwd_rows(dyv, zz, g, dz_ref, dzb_ref, dg_ref, db_ref):
    mu = jnp.mean(zz, axis=1, keepdims=True)
    zc = zz - mu
    rstd = lax.rsqrt(jnp.mean(zc * zc, axis=1, keepdims=True) + LN_EPS)
    xhat = zc * rstd
    dg_ref[...] += jnp.sum(dyv * xhat, axis=0, keepdims=True)
    db_ref[...] += jnp.sum(dyv, axis=0, keepdims=True)
    dxh = dyv * g
    dz = rstd * (dxh - jnp.mean(dxh, axis=1, keepdims=True) - xhat * jnp.mean(dxh * xhat, axis=1, keepdims=True))
    dz_ref[...] = dz
    dzb_ref[...] = dz.astype(BF16)


def _ln_bwd(name, dy, z, g):
    s, d = z.shape
    tr = min(256, s)

    def body(dy_ref, z_ref, g_ref, dz_ref, dzb_ref, dg_ref, db_ref):
        @pl.when(pl.program_id(0) == 0)
        def _():
            dg_ref[...] = jnp.zeros(dg_ref.shape, F32)
            db_ref[...] = jnp.zeros(db_ref.shape, F32)

        _ln_bwd_rows(dy_ref[...], z_ref[...], g_ref[...], dz_ref, dzb_ref, dg_ref, db_ref)

    row = pl.BlockSpec((tr, d), lambda i: (i, 0))
    vec = pl.BlockSpec((1, d), lambda i: (0, 0))
    return _pcall(
        body, (dy, z, g.reshape(1, d)), name=name, grid=(s // tr,), in_specs=[row, row, vec],
        out_specs=[row, row, vec, vec],
        out_shape=[jax.ShapeDtypeStruct((s, d), F32), jax.ShapeDtypeStruct((s, d), BF16),
                   jax.ShapeDtypeStruct((1, d), F32), jax.ShapeDtypeStruct((1, d), F32)],
        sem=("arbitrary",))


def _loss_ln_bwd(name, y, target, z, g):
    s, d = y.shape
    tr = min(256, s)

    def body(y_ref, t_ref, z_ref, g_ref, loss_ref, dz_ref, dzb_ref, dg_ref, db_ref):
        @pl.when(pl.program_id(0) == 0)
        def _():
            loss_ref[...] = jnp.zeros(loss_ref.shape, F32)
            dg_ref[...] = jnp.zeros(dg_ref.shape, F32)
            db_ref[...] = jnp.zeros(db_ref.shape, F32)

        e = y_ref[...] - t_ref[...]
        loss_ref[...] += jnp.sum(e * e) * (0.5 / d)
        _ln_bwd_rows(e * (1.0 / d), z_ref[...], g_ref[...], dz_ref, dzb_ref, dg_ref, db_ref)

    row = pl.BlockSpec((tr, d), lambda i: (i, 0))
    vec = pl.BlockSpec((1, d), lambda i: (0, 0))
    return pl.pallas_call(
        body, name=name, grid=(s // tr,), in_specs=[row, row, row, vec],
        out_specs=[pl.BlockSpec((8, LANE), lambda i: (0, 0)), row, row, vec, vec],
        out_shape=[jax.ShapeDtypeStruct((8, LANE), F32), jax.ShapeDtypeStruct((s, d), F32),
                   jax.ShapeDtypeStruct((s, d), BF16), jax.ShapeDtypeStruct((1, d), F32),
                   jax.ShapeDtypeStruct((1, d), F32)],
        compiler_params=_cp(("arbitrary",)))(y, target, z, g.reshape(1, d))


def _scan_add(x, reverse):
    ts = x.shape[0]
    rows = _iota((ts, 1), 0)
    dist = 1
    while dist < ts:
        if reverse:
            x = x + jnp.where(rows < ts - dist, pltpu.roll(x, ts - dist, 0), 0.0)
        else:
            x = x + jnp.where(rows >= dist, pltpu.roll(x, dist, 0), 0.0)
        dist *= 2
    return x


def _scan_affine(a, b, reverse):
    ts = a.shape[0]
    rows = _iota((ts, 1), 0)
    dist = 1
    while dist < ts:
        shift = ts - dist if reverse else dist
        valid = rows < ts - dist if reverse else rows >= dist
        b = b + a * jnp.where(valid, pltpu.roll(b, shift, 0), 0.0)
        a = a * jnp.where(valid, pltpu.roll(a, shift, 0), 1.0)
        dist *= 2
    return a, b


def _cum_forget_fwd(name, fl, bias):
    s = fl.shape[0]
    ts = min(1024, s)

    def body(f_ref, b_ref, o_ref, carry):
        @pl.when(pl.program_id(0) == 0)
        def _():
            carry[...] = jnp.zeros(carry.shape, F32)

        o_ref[...] = _scan_add(_log_sigmoid(f_ref[...] + b_ref[...]), False) + carry[...]
        carry[...] = o_ref[pl.ds(ts - 1, 1), :]

    row = pl.BlockSpec((ts, LANE), lambda i: (i, 0))
    return pl.pallas_call(body, name=name, grid=(s // ts,),
                          in_specs=[row, pl.BlockSpec((1, LANE), lambda i: (0, 0))], out_specs=row,
                          out_shape=jax.ShapeDtypeStruct((s, LANE), F32),
                          scratch_shapes=[pltpu.VMEM((1, LANE), F32)],
                          compiler_params=_cp(("arbitrary",)))(fl, bias)


def _cum_forget_bwd(name, dcf, fl, bias):
    s = fl.shape[0]
    ts = min(1024, s)
    nb = s // ts

    def body(d_ref, f_ref, b_ref, o_ref, db_ref, carry):
        @pl.when(pl.program_id(0) == 0)
        def _():
            carry[...] = jnp.zeros(carry.shape, F32)
            db_ref[...] = jnp.zeros(db_ref.shape, F32)

        run = _scan_add(d_ref[...], True) + carry[...]
        carry[...] = jnp.sum(jnp.where(_iota((ts, 1), 0) == 0, run, 0.0), axis=0, keepdims=True)
        dfl = run * _sigmoid(-(f_ref[...] + b_ref[...]))
        o_ref[...] = dfl.astype(BF16)
        db_ref[...] += jnp.sum(dfl, axis=0, keepdims=True)

    row = pl.BlockSpec((ts, LANE), lambda i: (nb - 1 - i, 0))
    vec = pl.BlockSpec((1, LANE), lambda i: (0, 0))
    return pl.pallas_call(body, name=name, grid=(nb,), in_specs=[row, row, vec], out_specs=[row, vec],
                          out_shape=[jax.ShapeDtypeStruct((s, LANE), BF16), jax.ShapeDtypeStruct((1, LANE), F32)],
                          scratch_shapes=[pltpu.VMEM((1, LANE), F32)],
                          compiler_params=_cp(("arbitrary",)))(dcf, fl, bias)


def _fox_specs(s, nh, tq, tk):
    q = pl.BlockSpec((tq, HEAD), lambda h, i: (i, FQ * nh + h))
    k = pl.BlockSpec((s, HEAD), lambda h, i: (0, FK * nh + h))
    v = pl.BlockSpec((s, HEAD), lambda h, i: (0, FV * nh + h))
    col = pl.BlockSpec((None, tq, 1), lambda h, i: (h, i, 0))
    rowv = pl.BlockSpec((None, s // tk, 1, tk), lambda h, i: (h, 0, 0, 0))
    tile = pl.BlockSpec((tq, HEAD), lambda h, i: (i, h))
    full = pl.BlockSpec((s, HEAD), lambda h, i: (0, h))
    return q, k, v, col, rowv, tile, full


def _fox_tile(s):
    return min(512, s), min(1024, s)


def _fox_scores(q, k_ref, cfq, cfr_ref, kb, tk, scale, qpos=None):
    off = pl.multiple_of(kb * tk, tk)
    k = k_ref[pl.ds(off, tk), :]
    sc = lax.dot_general(q, k, NT, preferred_element_type=F32) * scale + cfq - cfr_ref[kb]
    mask = None
    if qpos is not None:
        mask = kb * tk + _iota((1, tk), 1) <= qpos
        sc = jnp.where(mask, sc, NEG)
    return sc, mask, k, off


def _fox_fwd(name, u, cf_col, cf_row, bw, side=None):
    s, nh = u.shape[0], bw // HEAD
    tq, tk = _fox_tile(s)
    scale = HEAD ** -0.5

    def body(q_ref, k_ref, v_ref, cfc_ref, cfr_ref, o_ref, lse_ref):
        i = pl.program_id(1)
        q, cfq = q_ref[...], cfc_ref[...]
        last = ((i + 1) * tq - 1) // tk

        def step(kb, carry, qpos=None):
            m, l, acc = carry
            sc, _, _, off = _fox_scores(q, k_ref, cfq, cfr_ref, kb, tk, scale, qpos)
            m2 = jnp.maximum(m, jnp.max(sc, axis=1, keepdims=True))
            p = jnp.exp(sc - m2)
            al = jnp.exp(m - m2)
            return (m2, al * l + jnp.sum(p, axis=1, keepdims=True),
                    al * acc + jnp.dot(p.astype(BF16), v_ref[pl.ds(off, tk), :], preferred_element_type=F32))

        init = (jnp.full((tq, 1), NEG, F32), jnp.zeros((tq, 1), F32), jnp.zeros((tq, HEAD), F32))
        m, l, acc = step(last, lax.fori_loop(0, last, step, init), i * tq + _iota((tq, 1), 0))
        o_ref[...] = (acc / l).astype(BF16)
        lse_ref[...] = m + jnp.log(l)

    q, k, v, col, rowv, tile, _ = _fox_specs(s, nh, tq, tk)
    return _pcall(body, (u, u, u, cf_col, cf_row), name=name, grid=(nh, s // tq), in_specs=[q, k, v, col, rowv],
                  out_specs=[tile, col],
                  out_shape=[jax.ShapeDtypeStruct((s, bw), BF16), jax.ShapeDtypeStruct((nh, s, 1), F32)],
                  sem=("parallel", "parallel"), side=side)


def _fox_bwd(name, u, cf_col, cf_row, o, do, lse, bw, side=None):
    s, nh = u.shape[0], bw // HEAD
    tq, tk = _fox_tile(s)
    nq = s // tq
    scale = HEAD ** -0.5

    def body(q_ref, k_ref, v_ref, cfc_ref, cfr_ref, o_ref, do_ref, lse_ref,
             dq_ref, dk_ref, dv_ref, dcc_ref, dcr_ref, dk_s, dv_s):
        i = pl.program_id(1)

        @pl.when(i == 0)
        def _():
            dk_s[...] = jnp.zeros(dk_s.shape, F32)
            dv_s[...] = jnp.zeros(dv_s.shape, F32)
            dcr_ref[...] = jnp.zeros(dcr_ref.shape, F32)

        q, dov, cfq, lse_q = q_ref[...], do_ref[...], cfc_ref[...], lse_ref[...]
        delta = jnp.sum(dov.astype(F32) * o_ref[...].astype(F32), axis=1, keepdims=True)
        last = ((i + 1) * tq - 1) // tk

        def step(kb, carry, qpos=None):
            dq, dcq = carry
            sc, mask, k, off = _fox_scores(q, k_ref, cfq, cfr_ref, kb, tk, scale, qpos)
            p = jnp.exp(sc - lse_q)
            if qpos is not None:
                p = jnp.where(mask, p, 0.0)
            dp = lax.dot_general(dov, v_ref[pl.ds(off, tk), :], NT, preferred_element_type=F32)
            ds = p * (dp - delta)
            dsb = ds.astype(BF16)
            dk_s[pl.ds(off, tk), :] += lax.dot_general(dsb, q, TN, preferred_element_type=F32)
            dv_s[pl.ds(off, tk), :] += lax.dot_general(p.astype(BF16), dov, TN, preferred_element_type=F32)
            dcr_ref[kb] += -jnp.sum(ds, axis=0, keepdims=True)
            return (dq + jnp.dot(dsb, k, preferred_element_type=F32), dcq + jnp.sum(ds, axis=1, keepdims=True))

        init = (jnp.zeros((tq, HEAD), F32), jnp.zeros((tq, 1), F32))
        dq, dcq = step(last, lax.fori_loop(0, last, step, init), i * tq + _iota((tq, 1), 0))
        dq_ref[...] = (dq * scale).astype(BF16)
        dcc_ref[...] = jnp.transpose(jnp.broadcast_to(dcq, (tq, LANE)))[:8, :]

        @pl.when(i == nq - 1)
        def _():
            dk_ref[...] = (dk_s[...] * scale).astype(BF16)
            dv_ref[...] = dv_s[...].astype(BF16)

    q, k, v, col, rowv, tile, full = _fox_specs(s, nh, tq, tk)
    by_query = pl.BlockSpec((None, None, 8, tq), lambda h, i: (h, i, 0, 0))
    return _pcall(
        body, (u, u, u, cf_col, cf_row, o, do, lse), name=name, grid=(nh, nq),
        in_specs=[q, k, v, col, rowv, tile, tile, col], out_specs=[tile, full, full, by_query, rowv],
        out_shape=[jax.ShapeDtypeStruct((s, bw), BF16)] * 3
        + [jax.ShapeDtypeStruct((nh, nq, 8, tq), F32), jax.ShapeDtypeStruct((nh, s // tk, 1, tk), F32)],
        scratch_shapes=[pltpu.VMEM((s, HEAD), F32), pltpu.VMEM((s, HEAD), F32)],
        sem=("arbitrary", "arbitrary"), side=side)


def _suffix_mm(x, ones_below):
    hi = x.astype(BF16)
    lo = (x - hi.astype(F32)).astype(BF16)
    return (jnp.dot(hi, ones_below, preferred_element_type=F32) + jnp.dot(lo, ones_below, preferred_element_type=F32))


def _sb_tile(q, k_ref, kb, tk, qpos, scale):
    off = pl.multiple_of(kb * tk, tk)
    k = k_ref[pl.ds(off, tk), :]
    z = lax.dot_general(q, k, NT, preferred_element_type=F32) * scale
    mask = kb * tk + _iota((1, tk), 1) < qpos
    lsn = -jnp.maximum(z, 0.0) - jnp.log(1.0 + jnp.exp(-jnp.abs(z)))
    return z, mask, lsn, jnp.where(mask, lsn, 0.0), k, off


def _sb_specs(s, nh, tq):
    q = pl.BlockSpec((tq, HEAD), lambda h, i: (i, SQ * nh + h))
    k = pl.BlockSpec((s, HEAD), lambda h, i: (0, SK * nh + h))
    v = pl.BlockSpec((s, HEAD), lambda h, i: (0, SV * nh + h))
    tile = pl.BlockSpec((tq, HEAD), lambda h, i: (i, h))
    full = pl.BlockSpec((s, HEAD), lambda h, i: (0, h))
    return q, k, v, tile, full


def _sb_fwd(name, u, bw):
    s, nh = u.shape[0], bw // HEAD
    tq = tk = 256
    scale = HEAD ** -0.5

    def body(q_ref, k_ref, v_ref, o_ref):
        i = pl.program_id(1)
        q = q_ref[...]
        qpos = i * tq + _iota((tq, 1), 0)
        later_keys = (_iota((tk, tk), 0) > _iota((tk, tk), 1)).astype(BF16)
        nk = (i * tq + tq + tk - 2) // tk

        def cond(st):
            return jnp.logical_and(st[0] < nk, st[3] > SB_DEAD)

        def step(st):
            j, c, acc, _ = st
            z, mask, lsn, lm, _, off = _sb_tile(q, k_ref, nk - 1 - j, tk, qpos, scale)
            a = jnp.where(mask, jnp.exp(lsn + z + c + _suffix_mm(lm, later_keys)), 0.0)
            acc = acc + jnp.dot(a.astype(BF16), v_ref[pl.ds(off, tk), :], preferred_element_type=F32)
            c = c + jnp.sum(lm, axis=1, keepdims=True)
            return j + 1, c, acc, jnp.max(c)

        init = (jnp.int32(0), jnp.zeros((tq, 1), F32), jnp.zeros((tq, HEAD), F32), jnp.float32(0.0))
        o_ref[...] = lax.while_loop(cond, step, init)[2].astype(BF16)

    q, k, v, tile, _ = _sb_specs(s, nh, tq)
    return _pcall(body, (u, u, u), name=name, grid=(nh, s // tq), in_specs=[q, k, v], out_specs=[tile],
                  out_shape=[jax.ShapeDtypeStruct((s, bw), BF16)], sem=("parallel", "parallel"))[0]


def _sb_bwd(name, u, do, bw):
    s, nh = u.shape[0], bw // HEAD
    tq = tk = 256
    nq = s // tq
    scale = HEAD ** -0.5

    def body(q_ref, k_ref, v_ref, do_ref, dq_ref, dk_ref, dv_ref, dk_s, dv_s):
        i = pl.program_id(1)

        @pl.when(i == 0)
        def _():
            dk_s[...] = jnp.zeros(dk_s.shape, F32)
            dv_s[...] = jnp.zeros(dv_s.shape, F32)

        q, dov = q_ref[...], do_ref[...]
        qpos = i * tq + _iota((tq, 1), 0)
        later_keys = (_iota((tk, tk), 0) > _iota((tk, tk), 1)).astype(BF16)
        this_and_later = (_iota((tk, tk), 0) >= _iota((tk, tk), 1)).astype(BF16)
        nk = (i * tq + tq + tk - 2) // tk

        def weights(j, c):
            z, mask, lsn, lm, k, off = _sb_tile(q, k_ref, nk - 1 - j, tk, qpos, scale)
            a = jnp.where(mask, jnp.exp(lsn + z + c + _suffix_mm(lm, later_keys)), 0.0)
            w = a * lax.dot_general(dov, v_ref[pl.ds(off, tk), :], NT, preferred_element_type=F32)
            return z, mask, lsn, lm, k, off, a, w

        def cond(st):
            return jnp.logical_and(st[0] < nk, st[3] > SB_DEAD)

        def step1(st):
            j, c, wc, _ = st
            _, _, _, lm, _, _, _, w = weights(j, c)
            c = c + jnp.sum(lm, axis=1, keepdims=True)
            return j + 1, c, wc + jnp.sum(w, axis=1, keepdims=True), jnp.max(c)

        zero = jnp.zeros((tq, 1), F32)
        live, _, total, _ = lax.while_loop(cond, step1, (jnp.int32(0), zero, zero, jnp.float32(0.0)))

        def step2(j, st):
            c, wc, dq = st
            z, mask, lsn, lm, k, off, a, w = weights(j, c)
            earlier = total - (wc + _suffix_mm(w, this_and_later))
            dz = jnp.where(mask, w * jnp.exp(lsn) - jnp.exp(lsn + z) * earlier, 0.0)
            dzb = dz.astype(BF16)
            dk_s[pl.ds(off, tk), :] += lax.dot_general(dzb, q, TN, preferred_element_type=F32)
            dv_s[pl.ds(off, tk), :] += lax.dot_general(a.astype(BF16), dov, TN, preferred_element_type=F32)
            return (c + jnp.sum(lm, axis=1, keepdims=True), wc + jnp.sum(w, axis=1, keepdims=True),
                    dq + jnp.dot(dzb, k, preferred_element_type=F32))

        dq = lax.fori_loop(0, live, step2, (zero, zero, jnp.zeros((tq, HEAD), F32)))[2]
        dq_ref[...] = (dq * scale).astype(BF16)

        @pl.when(i == nq - 1)
        def _():
            dk_ref[...] = (dk_s[...] * scale).astype(BF16)
            dv_ref[...] = dv_s[...].astype(BF16)

    q, k, v, tile, full = _sb_specs(s, nh, tq)
    return pl.pallas_call(
        body, name=name, grid=(nh, nq), in_specs=[q, k, v, tile], out_specs=[tile, full, full],
        out_shape=[jax.ShapeDtypeStruct((s, bw), BF16)] * 3,
        scratch_shapes=[pltpu.VMEM((s, HEAD), F32), pltpu.VMEM((s, HEAD), F32)],
        compiler_params=_cp(("arbitrary", "arbitrary")))(u, u, u, do)


BIAS_W = -(-(WIN + QBLK - 1) // LANE) * LANE


def _strip_onehot():
    col = _iota((1, BIAS_W), 1)
    ridx = jnp.clip(PADK + (QBLK - 1) - col, -(CHUNK - 1), REL_CLIP) + (CHUNK - 1)
    return (_iota((REL_PAD, BIAS_W), 0) == ridx).astype(BF16)


def _split2(x):
    hi = x.astype(BF16)
    return hi, (x - hi.astype(F32)).astype(BF16)


def _bias_expand(name, table, nh):
    def body(t_ref, o_ref, strip):
        table_f32 = t_ref[...]
        hi = table_f32.astype(BF16)
        mid, lo = _split2(table_f32 - hi.astype(F32))
        onehot = _strip_onehot()
        strip[...] = (jnp.dot(hi, onehot, preferred_element_type=F32) + jnp.dot(mid, onehot, preferred_element_type=F32)
                      + jnp.dot(lo, onehot, preferred_element_type=F32))
        row, kl = _iota((QBLK, 1), 0), _iota((1, WIN), 1)
        first = row - jnp.bitwise_and(row, CHUNK - 1)
        valid = jnp.logical_and(kl >= first, kl < first + BAND)
        for h in range(nh):
            rows = jnp.broadcast_to(strip[pl.ds(h, 1), :], (QBLK, BIAS_W))
            rolled = pltpu.roll(rows, BIAS_W - (QBLK - 1), 1, stride=1, stride_axis=0)
            o_ref[h] = jnp.where(valid, rolled[:, :WIN], NEG)

    return pl.pallas_call(body, name=name, out_shape=jax.ShapeDtypeStruct((nh, QBLK, WIN), F32),
                          in_specs=[pl.BlockSpec(memory_space=pltpu.VMEM)],
                          out_specs=pl.BlockSpec(memory_space=pltpu.VMEM),
                          scratch_shapes=[pltpu.VMEM((16, BIAS_W), F32)], compiler_params=_cp())(table)


def _bias_reduce(name, dss, nh):
    def body(x_ref, o_ref):
        onehot = _strip_onehot()
        flip = (_iota((QBLK, QBLK), 0) + _iota((QBLK, QBLK), 1) == QBLK - 1).astype(BF16)
        for h in range(nh):
            x = jnp.concatenate([x_ref[h], jnp.zeros((QBLK, BIAS_W - WIN), F32)], axis=1)
            hi, lo = _split2(x)
            back = jnp.dot(flip, hi, preferred_element_type=F32) + jnp.dot(flip, lo, preferred_element_type=F32)
            lined = pltpu.roll(back, 0, 1, stride=1, stride_axis=0)
            hi, lo = _split2(jnp.broadcast_to(jnp.sum(lined, axis=0, keepdims=True), (8, BIAS_W)))
            o_ref[h] = (lax.dot_general(hi, onehot, NT, preferred_element_type=F32)
                        + lax.dot_general(lo, onehot, NT, preferred_element_type=F32))

    return pl.pallas_call(body, name=name, out_shape=jax.ShapeDtypeStruct((nh, 8, REL_PAD), F32),
                          in_specs=[pl.BlockSpec(memory_space=pltpu.VMEM)],
                          out_specs=pl.BlockSpec(memory_space=pltpu.VMEM), compiler_params=_cp())(dss)


def _chunk_specs(s, nh):
    q = pl.BlockSpec((QBLK, HEAD), lambda h, i: (i, CQ * nh + h))
    kv = pl.BlockSpec((s + PADK, HEAD), lambda h, i: (0, h))
    bias = pl.BlockSpec((None, QBLK, WIN), lambda h, i: (h, 0, 0))
    tile = pl.BlockSpec((QBLK, HEAD), lambda h, i: (i, h))
    full = pl.BlockSpec((s, HEAD), lambda h, i: (0, h))
    return q, kv, bias, tile, full


def _chunk_probs(q, k_ref, b_ref, i, scale):
    off = pl.multiple_of(i * QBLK, QBLK)
    kw = k_ref[pl.ds(off, WIN), :]
    sc = lax.dot_general(q, kw, NT, preferred_element_type=F32) * scale + b_ref[...]
    sc = jnp.where(i * QBLK + _iota((1, WIN), 1) >= PADK, sc, NEG)
    p = jnp.exp(sc - jnp.max(sc, axis=1, keepdims=True))
    return p, jnp.sum(p, axis=1, keepdims=True), kw, off


def _chunk_fwd(name, u, kpad, vpad, bias, bw, side=None):
    s, nh = u.shape[0], bw // HEAD
    scale = HEAD ** -0.5

    def body(q_ref, k_ref, v_ref, b_ref, o_ref):
        p, l, _, off = _chunk_probs(q_ref[...], k_ref, b_ref, pl.program_id(1), scale)
        o = jnp.dot(p.astype(BF16), v_ref[pl.ds(off, WIN), :], preferred_element_type=F32)
        o_ref[...] = (o / l).astype(BF16)

    q, kv, bs, tile, _ = _chunk_specs(s, nh)
    return _pcall(body, (u, kpad, vpad, bias), name=name, grid=(nh, s // QBLK), in_specs=[q, kv, kv, bs],
                  out_specs=[tile], out_shape=[jax.ShapeDtypeStruct((s, bw), BF16)], sem=("parallel", "parallel"),
                  side=side)[0]


def _chunk_bwd(name, u, kpad, vpad, bias, do, bw):
    s, nh = u.shape[0], bw // HEAD
    nq = s // QBLK
    scale = HEAD ** -0.5

    def body(q_ref, k_ref, v_ref, b_ref, do_ref, dq_ref, dk_ref, dv_ref, dss_ref, dk_s, dv_s):
        i = pl.program_id(1)

        @pl.when(i == 0)
        def _():
            dk_s[...] = jnp.zeros(dk_s.shape, F32)
            dv_s[...] = jnp.zeros(dv_s.shape, F32)
            dss_ref[...] = jnp.zeros(dss_ref.shape, F32)

        q, dov = q_ref[...], do_ref[...]
        p, l, kw, off = _chunk_probs(q, k_ref, b_ref, i, scale)
        p = p / l
        dp = lax.dot_general(dov, v_ref[pl.ds(off, WIN), :], NT, preferred_element_type=F32)
        ds = p * (dp - jnp.sum(p * dp, axis=1, keepdims=True))
        dsb = ds.astype(BF16)
        dq_ref[...] = (jnp.dot(dsb, kw, preferred_element_type=F32) * scale).astype(BF16)
        dk_s[pl.ds(off, WIN), :] += lax.dot_general(dsb, q, TN, preferred_element_type=F32)
        dv_s[pl.ds(off, WIN), :] += lax.dot_general(p.astype(BF16), dov, TN, preferred_element_type=F32)
        dss_ref[...] += ds

        @pl.when(i == nq - 1)
        def _():
            dk_ref[...] = (dk_s[pl.ds(PADK, s), :] * scale).astype(BF16)
            dv_ref[...] = dv_s[pl.ds(PADK, s), :].astype(BF16)

    q, kv, bs, tile, full = _chunk_specs(s, nh)
    return pl.pallas_call(
        body, name=name, grid=(nh, nq), in_specs=[q, kv, kv, bs, tile], out_specs=[tile, full, full, bs],
        out_shape=[jax.ShapeDtypeStruct((s, bw), BF16)] * 3 + [jax.ShapeDtypeStruct((nh, QBLK, WIN), F32)],
        scratch_shapes=[pltpu.VMEM((s + PADK, HEAD), F32), pltpu.VMEM((s + PADK, HEAD), F32)],
        compiler_params=_cp(("arbitrary", "arbitrary")))(u, kpad, vpad, bias, do)


def _gelu_parts(y):
    th = jnp.tanh(GELU_K * (y + GELU_C * y * y * y))
    return 0.5 * y * (1.0 + th), th


def _block_diag(xb16, w_ref, nh, dims):
    return jnp.concatenate(
        [lax.dot_general(xb16[:, n * HEAD:(n + 1) * HEAD], w_ref[n], dims, preferred_element_type=F32)
         for n in range(nh)], axis=1)


def _lru_gates(ext, cw_ref, cb_ref, wr_ref, br_ref, wi_ref, bi_ref, lam_ref, ts, nh):
    shifted = [pltpu.roll(ext, CONV_WIDTH - 1 - j, 0)[8:, :] if j < CONV_WIDTH - 1 else ext[8:, :]
               for j in range(CONV_WIDTH)]
    xc = cb_ref[...]
    for j in range(CONV_WIDTH):
        xc = xc + shifted[j] * cw_ref[pl.ds(j, 1), :]
    xcb = xc.astype(BF16)
    r = _sigmoid(_block_diag(xcb, wr_ref, nh, NN) + br_ref[...])
    gi = _sigmoid(_block_diag(xcb, wi_ref, nh, NN) + bi_ref[...])
    lsl = _log_sigmoid(lam_ref[...])
    la = LRU_C * r * lsl
    a = jnp.exp(la)
    e2 = jnp.exp(2.0 * la)
    mult = jnp.sqrt(-jnp.tanh(la) * (e2 + 1.0))
    return shifted, xc, xcb, r, gi, lsl, a, e2, mult


def _lru_param_specs(bw, nh):
    vec = pl.BlockSpec((1, bw), lambda i: (0, 0))
    conv = pl.BlockSpec((8, bw), lambda i: (0, 0))
    blocks = pl.BlockSpec((nh, HEAD, HEAD), lambda i: (0, 0, 0))
    return [conv, vec, blocks, vec, blocks, vec, vec]


def _lru_fwd(name, u, params, bw):
    s, nh = u.shape[0], bw // HEAD
    ts = min(512, s)

    def body(rx_ref, ry_ref, cw_ref, cb_ref, wr_ref, br_ref, wi_ref, bi_ref, lam_ref, o_ref, h_ref, tail, hcar):
        @pl.when(pl.program_id(0) == 0)
        def _():
            tail[...] = jnp.zeros(tail.shape, F32)
            hcar[...] = jnp.zeros(hcar.shape, F32)

        rx = rx_ref[...].astype(F32)
        ext = jnp.concatenate([tail[...], rx], axis=0)
        tail[...] = rx[ts - 8:, :]
        _, xc, _, _, gi, _, a, _, mult = _lru_gates(ext, cw_ref, cb_ref, wr_ref, br_ref, wi_ref, bi_ref, lam_ref, ts, nh)
        acum, bcum = _scan_affine(a, mult * (gi * xc), False)
        h_ref[...] = bcum + acum * hcar[...]
        hcar[...] = h_ref[pl.ds(ts - 1, 1), :]
        o_ref[...] = (h_ref[...] * _gelu_parts(ry_ref[...].astype(F32))[0]).astype(BF16)

    row = pl.BlockSpec((ts, bw), lambda i: (i, 0))
    return pl.pallas_call(
        body, name=name, grid=(s // ts,),
        in_specs=[pl.BlockSpec((ts, bw), lambda i: (i, RX)), pl.BlockSpec((ts, bw), lambda i: (i, RY))]
        + _lru_param_specs(bw, nh),
        out_specs=[row, row],
        out_shape=[jax.ShapeDtypeStruct((s, bw), BF16), jax.ShapeDtypeStruct((s, bw), F32)],
        scratch_shapes=[pltpu.VMEM((8, bw), F32), pltpu.VMEM((1, bw), F32)],
        compiler_params=_cp(("arbitrary",)))(u, u, *params)


def _lru_bwd(name, u, h, do, params, bw):
    s, nh = u.shape[0], bw // HEAD
    ts = min(512, s)
    nb = s // ts
    t8 = ts // 8

    def body(rx_ref, rxp_ref, ry_ref, h_ref, hp_ref, do_ref, cw_ref, cb_ref, wr_ref, br_ref, wi_ref, bi_ref, lam_ref,
             drx_ref, dry_ref, dcw_ref, dcb_ref, dwr_ref, dbr_ref, dwi_ref, dbi_ref, dlam_ref, gcar, head):
        i = pl.program_id(0)
        first = i == nb - 1

        @pl.when(i == 0)
        def _():
            gcar[...] = jnp.zeros(gcar.shape, F32)
            head[...] = jnp.zeros(head.shape, F32)
            for ref in (dcw_ref, dcb_ref, dwr_ref, dbr_ref, dwi_ref, dbi_ref, dlam_ref):
                ref[...] = jnp.zeros(ref.shape, F32)

        rows = _iota((ts, 1), 0)
        rx = rx_ref[...].astype(F32)
        before = jnp.where(first, 0.0, rxp_ref[...].astype(F32))
        ext = jnp.concatenate([before, rx], axis=0)
        shifted, xc, xcb, r, gi, lsl, a, e2, mult = _lru_gates(
            ext, cw_ref, cb_ref, wr_ref, br_ref, wi_ref, bi_ref, lam_ref, ts, nh)

        ry = ry_ref[...].astype(F32)
        gel, th = _gelu_parts(ry)
        dgel = 0.5 * (1.0 + th) + 0.5 * ry * (1.0 - th * th) * GELU_K * (1.0 + 3.0 * GELU_C * ry * ry)
        dov = do_ref[...].astype(F32)
        hv = h_ref[...]
        dry_ref[...] = (dov * hv * dgel).astype(BF16)

        coef = jnp.where(rows < ts - 1, pltpu.roll(a, ts - 1, 0), 0.0)
        dh_in = dov * gel + jnp.where(rows == ts - 1, gcar[...], 0.0)
        dh = _scan_affine(coef, dh_in, True)[1]
        gcar[...] = jnp.sum(jnp.where(rows == 0, a * dh, 0.0), axis=0, keepdims=True)

        hprev = jnp.where(first, 0.0, hp_ref[...])
        hm1 = pltpu.roll(jnp.concatenate([hprev, hv], axis=0), 1, 0)[8:, :]
        dgx = dh * mult
        dla = dh * hm1 * a - dh * gi * xc * (e2 / mult)
        dpre_r = dla * (LRU_C * lsl) * r * (1.0 - r)
        dpre_i = dgx * xc * gi * (1.0 - gi)
        dlam_ref[...] += jnp.sum(dla * r, axis=0, keepdims=True) * (LRU_C * _sigmoid(-lam_ref[...]))
        dbr_ref[...] += jnp.sum(dpre_r, axis=0, keepdims=True)
        dbi_ref[...] += jnp.sum(dpre_i, axis=0, keepdims=True)
        drb, dib = dpre_r.astype(BF16), dpre_i.astype(BF16)
        for n in range(nh):
            cols = slice(n * HEAD, (n + 1) * HEAD)
            dwr_ref[n] += lax.dot_general(xcb[:, cols], drb[:, cols], TN, preferred_element_type=F32)
            dwi_ref[n] += lax.dot_general(xcb[:, cols], dib[:, cols], TN, preferred_element_type=F32)
        dxc = dgx * gi + _block_diag(drb, wr_ref, nh, NT) + _block_diag(dib, wi_ref, nh, NT)

        dcb_ref[...] += jnp.sum(dxc, axis=0, keepdims=True)
        for j in range(CONV_WIDTH):
            dcw_ref[pl.ds(j, 1), :] += jnp.sum(dxc * shifted[j], axis=0, keepdims=True)
        ext2 = jnp.concatenate([dxc, head[...]], axis=0)
        head[...] = dxc[:8, :]
        drx = dxc * cw_ref[pl.ds(CONV_WIDTH - 1, 1), :]
        for j in range(CONV_WIDTH - 1):
            up = CONV_WIDTH - 1 - j
            drx = drx + pltpu.roll(ext2, ts + 8 - up, 0)[:ts, :] * cw_ref[pl.ds(j, 1), :]
        drx_ref[...] = drx.astype(BF16)

    def blk(col):
        return lambda i: (nb - 1 - i, col)

    def prev8(col):
        return lambda i: (jnp.maximum((nb - 1 - i) * t8 - 1, 0), col)

    vec = pl.BlockSpec((1, bw), lambda i: (0, 0))
    conv = pl.BlockSpec((8, bw), lambda i: (0, 0))
    blocks = pl.BlockSpec((nh, HEAD, HEAD), lambda i: (0, 0, 0))
    return pl.pallas_call(
        body, name=name, grid=(nb,),
        in_specs=[pl.BlockSpec((ts, bw), blk(RX)), pl.BlockSpec((8, bw), prev8(RX)), pl.BlockSpec((ts, bw), blk(RY)),
                  pl.BlockSpec((ts, bw), blk(0)), pl.BlockSpec((8, bw), prev8(0)), pl.BlockSpec((ts, bw), blk(0))]
        + _lru_param_specs(bw, nh),
        out_specs=[pl.BlockSpec((ts, bw), blk(0)), pl.BlockSpec((ts, bw), blk(0)), conv, vec, blocks, vec, blocks, vec, vec],
        out_shape=[jax.ShapeDtypeStruct((s, bw), BF16)] * 2
        + [jax.ShapeDtypeStruct((8, bw), F32), jax.ShapeDtypeStruct((1, bw), F32),
           jax.ShapeDtypeStruct((nh, HEAD, HEAD), F32), jax.ShapeDtypeStruct((1, bw), F32),
           jax.ShapeDtypeStruct((nh, HEAD, HEAD), F32), jax.ShapeDtypeStruct((1, bw), F32),
           jax.ShapeDtypeStruct((1, bw), F32)],
        scratch_shapes=[pltpu.VMEM((1, bw), F32), pltpu.VMEM((8, bw), F32)],
        compiler_params=_cp(("arbitrary",)))(u, u, u, h, h, do, *params)


def _gate_merge(name, xb, w_gate, b_gate, o_all, w_branch, l, side=None):
    s, d = xb.shape
    bw = o_all.shape[2]
    tm, tn = min(512, s), min(256, d)

    def body(x_ref, wg_ref, bg_ref, o_ref, wb_ref, m_ref, g_ref, p_ref):
        x = x_ref[...]
        acc = jnp.zeros((tm, tn), F32)
        for g in range(4):
            gate = _sigmoid(jnp.dot(x, wg_ref[g], preferred_element_type=F32) + bg_ref[g])
            proj = jnp.dot(o_ref[g], wb_ref[g], preferred_element_type=F32)
            term = gate * proj
            g_ref[g] = gate.astype(BF16)
            p_ref[g] = (term * (1.0 - gate)).astype(BF16)
            acc = acc + term
        m_ref[...] = acc.astype(BF16)

    quad = pl.BlockSpec((4, tm, tn), lambda n, m: (0, m, n))
    return _pcall(
        body, (xb, w_gate, b_gate, o_all, w_branch), name=name, grid=(d // tn, s // tm),
        in_specs=[pl.BlockSpec((tm, d), lambda n, m: (m, 0)),
                  pl.BlockSpec((None, 4, d, tn), lambda n, m: (0, 0, 0, n)),
                  pl.BlockSpec((None, 4, 1, tn), lambda n, m: (l, 0, 0, n)),
                  pl.BlockSpec((4, tm, bw), lambda n, m: (0, m, 0)),
                  pl.BlockSpec((None, 4, bw, tn), lambda n, m: (0, 0, 0, n))],
        out_specs=[pl.BlockSpec((tm, tn), lambda n, m: (m, n)), quad, quad],
        out_shape=[jax.ShapeDtypeStruct((s, d), BF16), jax.ShapeDtypeStruct((4, s, d), BF16),
                   jax.ShapeDtypeStruct((4, s, d), BF16)],
        sem=("parallel", "parallel"), side=side)


def _adamw(name, w, m, v, parts, layer=0, layers=1, earlier=None):
    cols = w.shape[1]
    p, rows = parts.shape[0], parts.shape[1]
    tr = _pow2_rows(rows, cols * max(1, p // 2))
    nb = rows // tr
    c1 = 1.0 - ADAM_B1 ** ADAM_STEP
    c2 = 1.0 - ADAM_B2 ** ADAM_STEP

    def body(w_ref, m_ref, v_ref, g_ref, *rest):
        go_ref, do_ref, mo_ref, vo_ref = rest[-4:]
        g = g_ref[0].astype(F32)
        for k in range(1, p):
            g = g + g_ref[k].astype(F32)
        m2 = ADAM_B1 * m_ref[...] + (1.0 - ADAM_B1) * g
        v2 = ADAM_B2 * v_ref[...] + (1.0 - ADAM_B2) * (g * g)
        go_ref[...] = g
        do_ref[...] = -ADAM_LR * ((m2 / c1) / (jnp.sqrt(v2 / c2) + ADAM_EPS) + ADAM_WD * w_ref[...])
        mo_ref[...] = m2
        vo_ref[...] = v2

    row = pl.BlockSpec((tr, cols), lambda i: (layer * nb + i, 0))
    held = list(earlier) if earlier is not None else []
    return pl.pallas_call(
        body, name=name, grid=(nb,),
        in_specs=[row, row, row, pl.BlockSpec((p, tr, cols), lambda i: (0, i, 0))] + [ANY] * len(held),
        out_specs=[row] * 4, out_shape=[jax.ShapeDtypeStruct((layers * rows, cols), F32)] * 4,
        input_output_aliases={4 + k: k for k in range(len(held))},
        compiler_params=_cp(("parallel",)))(w, m, v, parts, *held)


PACK_ROWS = 512


def _pack(arrays):
    rows = []
    for a in arrays:
        flat = a.astype(F32).reshape(-1)
        rows.append(jnp.pad(flat, (0, (-flat.shape[0]) % LANE)).reshape(-1, LANE))
    rows = jnp.concatenate(rows)
    return jnp.pad(rows, ((0, (-rows.shape[0]) % PACK_ROWS), (0, 0)))


def _unpack(packed, shapes):
    out, row = [], 0
    for shp in shapes:
        n = math.prod(shp)
        nrows = -(-n // LANE)
        out.append(packed[row:row + nrows].reshape(-1)[:n].reshape(shp))
        row += nrows
    return out


def _unshard(gathered, axis):
    block = gathered.shape[2:]
    full = jnp.swapaxes(gathered, 0, 1).reshape((N_DEV,) + block)
    full = jnp.moveaxis(full, 0, axis)
    return full.reshape(block[:axis] + (N_DEV * block[axis],) + block[axis + 1:])


def kernel(x, ln_in_g, ln_in_b, w_in, b_forget, conv_w, conv_b, w_r, b_r, w_i, b_i, lru_lambda, rel_bias, w_branch, w_gate, b_gate, w_out, ln1_g, ln1_b, w_ff1, w_ff2, ln2_g, ln2_b, loss_target, m_ln_in_g, m_ln_in_b, m_w_in, m_b_forget, m_conv_w, m_conv_b, m_w_r, m_b_r, m_w_i, m_b_i, m_lru_lambda, m_rel_bias, m_w_branch, m_w_gate, m_b_gate, m_w_out, m_ln1_g, m_ln1_b, m_w_ff1, m_w_ff2, m_ln2_g, m_ln2_b, v_ln_in_g, v_ln_in_b, v_w_in, v_b_forget, v_conv_w, v_conv_b, v_w_r, v_b_r, v_w_i, v_b_i, v_lru_lambda, v_rel_bias, v_w_branch, v_w_gate, v_b_gate, v_w_out, v_ln1_g, v_ln1_b, v_w_ff1, v_w_ff2, v_ln2_g, v_ln2_b):
    given = dict(zip(
        NAMES + ['loss_target'] + ['m_' + n for n in WEIGHTS] + ['v_' + n for n in WEIGHTS],
        (x, ln_in_g, ln_in_b, w_in, b_forget, conv_w, conv_b, w_r, b_r, w_i, b_i, lru_lambda, rel_bias, w_branch, w_gate, b_gate, w_out, ln1_g, ln1_b, w_ff1, w_ff2, ln2_g, ln2_b, loss_target, m_ln_in_g, m_ln_in_b, m_w_in, m_b_forget, m_conv_w, m_conv_b, m_w_r, m_b_r, m_w_i, m_b_i, m_lru_lambda, m_rel_bias, m_w_branch, m_w_gate, m_b_gate, m_w_out, m_ln1_g, m_ln1_b, m_w_ff1, m_w_ff2, m_ln2_g, m_ln2_b, v_ln_in_g, v_ln_in_b, v_w_in, v_b_forget, v_conv_w, v_conv_b, v_w_r, v_b_r, v_w_i, v_b_i, v_lru_lambda, v_rel_bias, v_w_branch, v_w_gate, v_b_gate, v_w_out, v_ln1_g, v_ln1_b, v_w_ff1, v_w_ff2, v_ln2_g, v_ln2_b)))

    s, d = x.shape[1], x.shape[2]
    nl = w_in.shape[0]
    bw = d // 4
    nh = bw // HEAD
    nu = 11 * bw
    rs = d // N_DEV
    dff = w_ff1.shape[2] * N_DEV
    fs = dff // N_DEV
    cs = d // N_DEV
    assert nl == DEPTH and nh * HEAD == bw and s % 1024 == 0 and d % 1024 == 0

    xi, yi, ci = _position()
    dev = 4 * xi + 2 * yi + ci
    c_arr = jnp.reshape(ci, (1,)).astype(I32)

    w_main = jnp.concatenate(
        [w_in[..., :3 * bw], w_in[..., 3 * bw + nh:],
         jnp.pad(w_in[..., 3 * bw:3 * bw + nh], ((0, 0), (0, 0), (0, LANE - nh)))], axis=-1).astype(BF16)
    nue = nu + LANE
    small_shapes = [conv_w.shape, rel_bias.shape, b_gate.shape]
    shard = {'main': w_main, 'branch': w_branch.astype(BF16), 'gate': w_gate.astype(BF16),
             'out': w_out.astype(BF16), 'ff1': w_ff1.astype(BF16), 'ff2': w_ff2.astype(BF16)}
    shard_axis = {'main': 1, 'branch': 3, 'gate': 2, 'out': 1, 'ff1': 2, 'ff2': 1}
    W = [dict() for _ in range(nl)]

    def gather_chips(l, keys, extra=()):
        side = _chips_side([shard[k][l:l + 1] for k in keys] + list(extra), True)
        side.todo = (l, keys)
        return side

    def gather_cores(chips):
        side = _cores_side(chips.results, True)
        side.todo = chips.todo
        return side

    def arrived(cores):
        l, keys = cores.todo
        for k, res in zip(keys, cores.results):
            W[l][k] = _unshard(res, shard_axis[k])
        return cores.results[len(keys):]

    xs = x[0]
    first = gather_chips(0, ['main'], [_pack([conv_w, rel_bias, b_gate])])
    h0, h0b = _ln_fwd("ln_in", xs, ln_in_g, ln_in_b, side=first)
    first = gather_cores(first)
    _run_side("gather_cores_first", first)
    small = arrived(first)[0]
    small = jnp.swapaxes(small, 0, 1).reshape((N_DEV,) + small.shape[2:])
    small = [_unpack(small[j], small_shapes) for j in range(N_DEV)]
    conv_w_full = jnp.concatenate([small[j][0] for j in range(N_DEV)], axis=-1)
    rel_bias_full = jnp.concatenate([small[j][1] for j in range(N_DEV)], axis=-1)
    b_gate_full = jnp.concatenate([small[j][2] for j in range(N_DEV)], axis=-1)
    b_gate4 = b_gate_full.reshape(nl, 4, 1, d)

    def lru_params(l):
        return (jnp.pad(conv_w_full[l], ((0, 8 - CONV_WIDTH), (0, 0))), conv_b[l].reshape(1, bw),
                w_r[l].astype(BF16), b_r[l].reshape(1, bw), w_i[l].astype(BF16), b_i[l].reshape(1, bw),
                lru_lambda[l].reshape(1, bw))

    def bias_rows(l):
        return jnp.pad(rel_bias_full[l], ((0, 16 - nh), (0, REL_PAD - REL_TABLE)))

    tm = min(1024, s)
    tkk = min(2048, d)

    saved = []
    cur, curb = h0, h0b
    chips = {}
    for l in range(nl):
        side = None
        if l == 0:
            side = chips['b0'] = gather_chips(0, ['gate', 'branch'])
        else:
            side = last_cores = gather_cores(chips.pop('d1'))
        u = _mm(f"w_in_{l}", curb, W[l]['main'], grid=(s // tm, nu // bw, d // tkk),
                a_spec=pl.BlockSpec((tm, tkk), lambda m, n, k: (m, k)),
                b_spec=pl.BlockSpec((None, tkk, bw), lambda m, n, k: (0, k, n)),
                dims=NN, acc_shape=(tm, bw), out_shape=[jax.ShapeDtypeStruct((s, nu), BF16)],
                out_specs=[pl.BlockSpec((tm, bw), lambda m, n, k: (m, n))], finish=_store(BF16), side=side)[0]
        if l == 1:
            arrived(last_cores)
        fl = _mm(f"w_forget_{l}", curb, W[l]['main'], grid=(s // tm, 1, d // tkk),
                 a_spec=pl.BlockSpec((tm, tkk), lambda m, n, k: (m, k)),
                 b_spec=pl.BlockSpec((None, tkk, LANE), lambda m, n, k: (0, k, nu // LANE)),
                 dims=NN, acc_shape=(tm, LANE), out_shape=[jax.ShapeDtypeStruct((s, LANE), F32)],
                 out_specs=[pl.BlockSpec((tm, LANE), lambda m, n, k: (m, 0))], finish=_store(F32))[0]
        bf_row = jnp.pad(b_forget[l], (0, LANE - nh)).reshape(1, LANE)
        cf = _cum_forget_fwd(f"cum_forget_{l}", fl, bf_row)
        tkf = _fox_tile(s)[1]
        cf_heads = cf[:, :nh].T
        cf_col = cf_heads.reshape(nh, s, 1)
        cf_row = cf_heads.reshape(nh, s // tkf, 1, tkf)
        side = None
        if l == 0:
            b0 = gather_cores(chips.pop('b0'))
            chips['c0'] = gather_chips(0, ['out', 'ff1', 'ff2'])
            side = _merge_sides(b0, chips['c0'])
        o_fox, lse = _fox_fwd(f"fox_fwd_{l}", u, cf_col, cf_row, bw, side=side)
        if l == 0:
            arrived(b0)
        lp = lru_params(l)
        o_lru, hstate = _lru_fwd(f"lru_fwd_{l}", u, lp, bw)
        o_sb = _sb_fwd(f"sb_fwd_{l}", u, bw)
        bias = _bias_expand(f"bias_expand_{l}", bias_rows(l), nh)
        kpad = jnp.pad(u[:, CK * bw:(CK + 1) * bw], ((PADK, 0), (0, 0)))
        vpad = jnp.pad(u[:, CV * bw:(CV + 1) * bw], ((PADK, 0), (0, 0)))
        side = gather_cores(chips.pop('c0')) if l == 0 else None
        o_ch = _chunk_fwd(f"chunk_fwd_{l}", u, kpad, vpad, bias, bw, side=side)
        if l == 0:
            arrived(side)
        o_all = jnp.stack([o_fox, o_lru, o_sb, o_ch])
        side = None
        if l == 0:
            side = chips['a1'] = gather_chips(1, ['main', 'gate', 'branch'])
        merged, gates, projs = _gate_merge(f"gate_merge_{l}", curb, W[l]['gate'], b_gate4, o_all, W[l]['branch'], l,
                                           side=side)
        side = gather_cores(chips.pop('a1')) if l == 0 else None
        z1, x1, x1b = _mm_ln(f"w_out_ln1_{l}", merged, W[l]['out'], 0, cur, ln1_g[l], ln1_b[l], side=side)
        if l == 0:
            arrived(side)
        tn1 = min(1024, dff)

        def ff1_finish(acc, ex, outs, ids):
            outs[0][...] = acc.astype(BF16)
            r = jnp.maximum(acc, 0.0)
            outs[1][...] = (r * r).astype(BF16)

        side = None
        if l == 0:
            side = chips['c1'] = gather_chips(1, ['out', 'ff1'])
        hp, hid = _mm(f"w_ff1_{l}", x1b, W[l]['ff1'], grid=(s // tm, dff // tn1, d // tkk),
                      a_spec=pl.BlockSpec((tm, tkk), lambda m, n, k: (m, k)),
                      b_spec=pl.BlockSpec((None, tkk, tn1), lambda m, n, k: (0, k, n)),
                      dims=NN, acc_shape=(tm, tn1),
                      out_shape=[jax.ShapeDtypeStruct((s, dff), BF16)] * 2,
                      out_specs=[pl.BlockSpec((tm, tn1), lambda m, n, k: (m, n))] * 2, finish=ff1_finish, side=side)
        side = None
        if l == 0:
            c1 = gather_cores(chips.pop('c1'))
            chips['d1'] = gather_chips(1, ['ff2'])
            side = _merge_sides(c1, chips['d1'])
        z2, x2, x2b = _mm_ln(f"w_ff2_ln2_{l}", hid, W[l]['ff2'], 0, x1, ln2_g[l], ln2_b[l], side=side)
        if l == 0:
            arrived(c1)
        saved.append(dict(xin=cur, xinb=curb, u=u, fl=fl, bf_row=bf_row, cf_col=cf_col, cf_row=cf_row, o_fox=o_fox,
                          lse=lse, lp=lp, hstate=hstate, bias=bias, kpad=kpad, vpad=vpad, o_all=o_all, merged=merged,
                          gates=gates, projs=projs, z1=z1, x1=x1, x1b=x1b, hp=hp, hid=hid, z2=z2))
        cur, curb = x2, x2b

    loss_tile, *last_ln = _loss_ln_bwd("loss_ln2_bwd", cur, loss_target[0], saved[nl - 1]['z2'], ln2_g[nl - 1])
    loss = lax.psum(loss_tile[0, 0], ("x", "y", "c"))
    dcur = None

    big = [dict() for _ in range(nl)]
    reduced = [dict() for _ in range(nl)]
    sm = {n: [None] * nl for n in ['b_forget', 'conv_w', 'conv_b', 'w_r', 'b_r', 'w_i', 'b_i', 'lru_lambda', 'rel_bias',
                                   'b_gate', 'ln1_g', 'ln1_b', 'ln2_g', 'ln2_b']}

    def split_columns(acc, ex, outs, ids):
        for j in range(N_DEV):
            outs[0][j] = acc[:, j * cs:(j + 1) * cs].astype(BF16)

    def grad_mm(key, name, a, b, *, shape, grid, a_spec, b_spec, out_spec, acc_shape, finish=_store(BF16)):
        big[l][key] = _mm(name, a, b, grid=grid, a_spec=a_spec, b_spec=b_spec, dims=TN, acc_shape=acc_shape,
                          out_shape=[jax.ShapeDtypeStruct(shape, BF16)], out_specs=[out_spec], finish=finish)[0]

    def reduce_cores(l, keys):
        side = _cores_side([big[l][k].reshape((4, 2) + big[l][k].shape[1:]) for k in keys], False)
        side.todo = (l, keys)
        return side

    def reduce_chips(cores):
        l, keys = cores.todo
        partial = []
        for k, mine, other in zip(keys, cores.operands, cores.results):
            cols = mine.shape[-1]
            rows = math.prod(mine.shape[2:]) // cols
            partial.append(_add_core_halves(f"add_cores_{k}_{l}", mine.reshape(4, 2, rows, cols),
                                            other.reshape(4, rows, cols), c_arr, BF16))
        side = _chips_side(partial, False)
        side.todo = (l, keys)
        return side

    def reduction_done(chips_side):
        l, keys = chips_side.todo
        reduced[l].update(zip(keys, chips_side.results))

    GROUP1, GROUP2 = ['w_ff2', 'w_ff1', 'w_out', 'w_branch', 'w_gate'], ['w_main']
    pending = None

    tks = min(1024, s)
    tmr = min(1024, d)
    nsh = tmr // rs

    for l in reversed(range(nl)):
        sv = saved[l]
        Wl = W[l]
        dz2, dz2b, dg, db = last_ln if l == nl - 1 else _ln_bwd(f"ln2_bwd_{l}", dcur, sv['z2'], ln2_g[l])
        sm['ln2_g'][l], sm['ln2_b'][l] = dg[0], db[0]
        tn1 = min(1024, dff)

        def dhp_finish(acc, ex, outs, ids):
            outs[0][...] = (acc * (2.0 * jnp.maximum(ex[0][...].astype(F32), 0.0))).astype(BF16)

        dhp = _mm(f"d_hidden_{l}", dz2b, Wl['ff2'], grid=(s // tm, dff // tn1, d // tkk),
                  a_spec=pl.BlockSpec((tm, tkk), lambda m, n, k: (m, k)),
                  b_spec=pl.BlockSpec((None, tn1, tkk), lambda m, n, k: (0, n, k)),
                  dims=NT, acc_shape=(tm, tn1), out_shape=[jax.ShapeDtypeStruct((s, dff), BF16)],
                  out_specs=[pl.BlockSpec((tm, tn1), lambda m, n, k: (m, n))], finish=dhp_finish,
                  extras=(sv['hp'],), extra_specs=(pl.BlockSpec((tm, tn1), lambda m, n, k: (m, n)),),
                  side=pending)[0]
        if pending is not None:
            reduction_done(pending)
            pending = None
        grad_mm('w_ff2', f"g_w_ff2_{l}", sv['hid'], dz2b, shape=(N_DEV, 1, fs, d), grid=(N_DEV, 1, s // tks),
                a_spec=pl.BlockSpec((tks, fs), lambda m, n, k: (k, m)),
                b_spec=pl.BlockSpec((tks, d), lambda m, n, k: (k, 0)),
                out_spec=pl.BlockSpec((None, None, fs, d), lambda m, n, k: (m, 0, 0, 0)), acc_shape=(fs, d))
        grad_mm('w_ff1', f"g_w_ff1_{l}", sv['x1b'], dhp, shape=(N_DEV, 1, d, fs), grid=(d // tmr, N_DEV, s // tks),
                a_spec=pl.BlockSpec((tks, tmr), lambda m, n, k: (k, m)),
                b_spec=pl.BlockSpec((tks, fs), lambda m, n, k: (k, n)),
                out_spec=pl.BlockSpec((None, None, tmr, fs), lambda m, n, k: (n, 0, m, 0)), acc_shape=(tmr, fs))
        tnd = min(1024, d)

        def resid_finish(scale):
            def finish(acc, ex, outs, ids):
                outs[0][...] = acc + scale * ex[0][...]
            return finish

        tile_md = pl.BlockSpec((tm, tnd), lambda m, n, k: (m, n))
        dx1 = _mm(f"d_x1_{l}", dhp, Wl['ff1'], grid=(s // tm, d // tnd, dff // tkk),
                  a_spec=pl.BlockSpec((tm, tkk), lambda m, n, k: (m, k)),
                  b_spec=pl.BlockSpec((None, tnd, tkk), lambda m, n, k: (0, n, k)),
                  dims=NT, acc_shape=(tm, tnd), out_shape=[jax.ShapeDtypeStruct((s, d), F32)],
                  out_specs=[tile_md], finish=resid_finish(ALPHA), extras=(dz2,), extra_specs=(tile_md,))[0]

        dz1, dz1b, dg, db = _ln_bwd(f"ln1_bwd_{l}", dx1, sv['z1'], ln1_g[l])
        sm['ln1_g'][l], sm['ln1_b'][l] = dg[0], db[0]
        tmg, tng = min(512, s), min(512, d)

        def gate_finish(acc, ex, outs, ids):
            @pl.when(ids[1] == 0)
            def _():
                outs[2][...] = jnp.zeros(outs[2].shape, F32)

            ones = jnp.ones((8, tmg), BF16)
            for g in range(4):
                dpre = (acc * ex[1][g].astype(F32)).astype(BF16)
                outs[0][g] = (acc * ex[0][g].astype(F32)).astype(BF16)
                outs[1][g] = dpre
                outs[2][g] += jnp.dot(ones, dpre, preferred_element_type=F32)

        quad = pl.BlockSpec((4, tmg, tng), lambda n, m, k: (0, m, n))
        dproj, dpre, dbg = _mm(
            f"d_merged_{l}", dz1b, Wl['out'], grid=(d // tng, s // tmg, d // tkk),
            a_spec=pl.BlockSpec((tmg, tkk), lambda n, m, k: (m, k)),
            b_spec=pl.BlockSpec((None, tng, tkk), lambda n, m, k: (0, n, k)),
            dims=NT, acc_shape=(tmg, tng),
            out_shape=[jax.ShapeDtypeStruct((4, s, d), BF16), jax.ShapeDtypeStruct((4, s, d), BF16),
                       jax.ShapeDtypeStruct((4, 8, d), F32)],
            out_specs=[quad, quad, pl.BlockSpec((4, 8, tng), lambda n, m, k: (0, 0, n))], finish=gate_finish,
            extras=(sv['gates'], sv['projs']), extra_specs=(quad, quad), sem=("arbitrary", "arbitrary", "arbitrary"))
        sm['b_gate'][l] = dbg[:, 0, :]
        grad_mm('w_out', f"g_w_out_{l}", sv['merged'], dz1b, shape=(N_DEV, 1, rs, d), grid=(d // tmr, 1, s // tks),
                a_spec=pl.BlockSpec((tks, tmr), lambda m, n, k: (k, m)),
                b_spec=pl.BlockSpec((tks, d), lambda m, n, k: (k, 0)),
                out_spec=pl.BlockSpec((nsh, None, rs, d), lambda m, n, k: (m, 0, 0, 0)), acc_shape=(tmr, d))

        nm = s // tm
        do_all = _mm(f"d_branch_{l}", dproj, Wl['branch'], grid=(4 * nm, 1, d // tkk),
                     a_spec=pl.BlockSpec((None, tm, tkk), lambda m, n, k: (m // nm, m % nm, k)),
                     b_spec=pl.BlockSpec((None, None, bw, tkk), lambda m, n, k: (0, m // nm, 0, k)),
                     dims=NT, acc_shape=(tm, bw), out_shape=[jax.ShapeDtypeStruct((4, s, bw), BF16)],
                     out_specs=[pl.BlockSpec((None, tm, bw), lambda m, n, k: (m // nm, m % nm, 0))],
                     finish=_store(BF16))[0]
        grad_mm('w_branch', f"g_w_branch_{l}", sv['o_all'], dproj, shape=(N_DEV, 1, 4, bw, cs),
                grid=(4, 1, s // tks), finish=split_columns,
                a_spec=pl.BlockSpec((None, tks, bw), lambda m, n, k: (m, k, 0)),
                b_spec=pl.BlockSpec((None, tks, d), lambda m, n, k: (m, k, 0)),
                out_spec=pl.BlockSpec((N_DEV, None, None, bw, cs), lambda m, n, k: (0, 0, m, 0, 0)),
                acc_shape=(bw, d))
        grad_mm('w_gate', f"g_w_gate_{l}", sv['xinb'], dpre, shape=(N_DEV, 1, 4, rs, d), grid=(d // tmr, 4, s // tks),
                a_spec=pl.BlockSpec((tks, tmr), lambda m, n, k: (k, m)),
                b_spec=pl.BlockSpec((None, tks, d), lambda m, n, k: (n, k, 0)),
                out_spec=pl.BlockSpec((nsh, None, None, rs, d), lambda m, n, k: (m, 0, n, 0, 0)),
                acc_shape=(tmr, d))
        nkg = d // tkk
        cores1 = reduce_cores(l, GROUP1)
        dx_gate = _mm(f"d_x_gates_{l}", dpre, Wl['gate'], grid=(s // tm, d // tnd, 4 * nkg),
                      a_spec=pl.BlockSpec((None, tm, tkk), lambda m, n, k: (k // nkg, m, k % nkg)),
                      b_spec=pl.BlockSpec((None, None, tnd, tkk), lambda m, n, k: (0, k // nkg, n, k % nkg)),
                      dims=NT, acc_shape=(tm, tnd), out_shape=[jax.ShapeDtypeStruct((s, d), F32)],
                      out_specs=[tile_md], finish=resid_finish(ALPHA), extras=(dz1,), extra_specs=(tile_md,),
                      side=cores1)[0]

        u = sv['u']
        chips1 = reduce_chips(cores1)
        dfq, dfk, dfv, dcc, dcr = _fox_bwd(f"fox_bwd_{l}", u, sv['cf_col'], sv['cf_row'], sv['o_fox'], do_all[0],
                                          sv['lse'], bw, side=chips1)
        reduction_done(chips1)
        dcf = (dcc[:, :, 0, :].reshape(nh, s) + dcr.reshape(nh, s)).T
        dflb, dbf = _cum_forget_bwd(f"cum_forget_bwd_{l}", jnp.pad(dcf, ((0, 0), (0, LANE - nh))), sv['fl'],
                                    sv['bf_row'])
        sm['b_forget'][l] = dbf[0, :nh]
        drx, dry, dcw, dcb, dwr, dbr, dwi, dbi, dlam = _lru_bwd(f"lru_bwd_{l}", u, sv['hstate'], do_all[1], sv['lp'], bw)
        sm['conv_w'][l], sm['conv_b'][l], sm['w_r'][l], sm['b_r'][l] = dcw[:CONV_WIDTH], dcb[0], dwr, dbr[0]
        sm['w_i'][l], sm['b_i'][l], sm['lru_lambda'][l] = dwi, dbi[0], dlam[0]
        dsq, dsk, dsv = _sb_bwd(f"sb_bwd_{l}", u, do_all[2], bw)
        dcq, dck, dcv, dss = _chunk_bwd(f"chunk_bwd_{l}", u, sv['kpad'], sv['vpad'], sv['bias'], do_all[3], bw)
        sm['rel_bias'][l] = _bias_reduce(f"bias_reduce_{l}", dss, nh)[:, 0, :REL_TABLE]
        du = jnp.concatenate([dfq, dfk, dfv, drx, dry, dsq, dsk, dsv, dcq, dck, dcv, dflb], axis=1)

        tnu = _lane_tile(nue)
        grad_mm('w_main', f"g_w_in_{l}", sv['xinb'], du, shape=(N_DEV, 1, rs, nue),
                grid=(d // tmr, nue // tnu, s // tks),
                a_spec=pl.BlockSpec((tks, tmr), lambda m, n, k: (k, m)),
                b_spec=pl.BlockSpec((tks, tnu), lambda m, n, k: (k, n)),
                out_spec=pl.BlockSpec((nsh, None, rs, tnu), lambda m, n, k: (m, 0, 0, n)), acc_shape=(tmr, tnu))
        cores2 = reduce_cores(l, GROUP2)
        if l == 0:
            _run_side("reduce_cores_last", cores2)
            cores2 = reduce_chips(cores2)
        dcur = _mm(f"d_x_in_{l}", du, Wl['main'], grid=(s // tm, d // tnd, nue // tnu),
                   a_spec=pl.BlockSpec((tm, tnu), lambda m, n, k: (m, k)),
                   b_spec=pl.BlockSpec((None, tnd, tnu), lambda m, n, k: (0, n, k)),
                   dims=NT, acc_shape=(tm, tnd), out_shape=[jax.ShapeDtypeStruct((s, d), F32)],
                   out_specs=[tile_md], finish=resid_finish(1.0), extras=(dx_gate,), extra_specs=(tile_md,),
                   side=cores2)[0]
        if l == 0:
            reduction_done(cores2)
        else:
            pending = reduce_chips(cores2)

    grad_x, _, dg_in, db_in = _ln_bwd("ln_in_bwd", dcur, xs, ln_in_g)

    small_names = ['ln_in_g', 'ln_in_b', 'b_forget', 'conv_w', 'conv_b', 'w_r', 'b_r', 'w_i', 'b_i', 'lru_lambda',
                   'rel_bias', 'b_gate', 'ln1_g', 'ln1_b', 'ln2_g', 'ln2_b']
    local = {'ln_in_g': dg_in[0], 'ln_in_b': db_in[0]}
    for n in small_names[2:]:
        local[n] = jnp.stack(sm[n])
    full_shapes = [local[n].shape for n in small_names]
    packed = _pack([local[n] for n in small_names])
    every = _run_side("gather_small_cores",
                      _cores_side(_run_side("gather_small_chips", _chips_side([packed], True)), True))[0]
    every = jnp.swapaxes(every, 0, 1).reshape((N_DEV,) + packed.shape)
    total = dict(zip(small_names, _unpack(_sum_parts("sum_small", every), full_shapes)))
    for n, width in (('conv_w', bw // N_DEV), ('rel_bias', REL_TABLE // N_DEV), ('b_gate', cs)):
        total[n] = lax.dynamic_slice_in_dim(total[n], dev * width, width, axis=2)

    out = {}

    def update(n, parts):
        cols = parts[0].shape[2]
        w2, m2, v2 = (given[p + n].reshape(-1, cols) for p in ('', 'm_', 'v_'))
        res = None
        for l in range(nl):
            res = _adamw(f"adamw_{n}_{l}", w2, m2, v2, parts[l], layer=l, layers=nl, earlier=res)
        out[n] = [r.reshape(given[n].shape) for r in res]

    def parts_w_in(l):
        pm = reduced[l]['w_main']
        return jnp.concatenate([pm[..., :3 * bw], pm[..., nu:nu + nh], pm[..., 3 * bw:nu]], axis=-1)

    update('w_in', [parts_w_in(l) for l in range(nl)])
    for n in ('w_branch', 'w_gate', 'w_out', 'w_ff1', 'w_ff2'):
        update(n, [reduced[l][n] for l in range(nl)])

    small_shapes2 = [given[n].shape for n in small_names]
    res = _adamw("adamw_small", _pack([given[n] for n in small_names]), _pack([given['m_' + n] for n in small_names]),
                 _pack([given['v_' + n] for n in small_names]), _pack([total[n] for n in small_names])[None])
    res = [_unpack(r, small_shapes2) for r in res]
    for j, n in enumerate(small_names):
        out[n] = [res[k][j] for k in range(4)]

    return (loss, grad_x[None], *[out[n][0] for n in WEIGHTS], *[out[n][1] for n in WEIGHTS],
            *[out[n][2] for n in WEIGHTS], *[out[n][3] for n in WEIGHTS])
```

```python
import functools
import math

import jax
import jax.numpy as jnp
from jax import lax
from jax.experimental import pallas as pl
from jax.experimental.pallas import tpu as pltpu

F32, BF16, I32 = jnp.float32, jnp.bfloat16, jnp.int32
MESH = pl.DeviceIdType.MESH
ANY = pl.BlockSpec(memory_space=pl.ANY)

LANE = 128
VMEM_LIMIT = 56 * 1024 * 1024
N_DEV = 8

HEAD = 128
CHUNK = 64
LOOKBACK = 8
BAND = (LOOKBACK + 1) * CHUNK
QCHUNKS = 4
QBLK = QCHUNKS * CHUNK
WIN = BAND + (QCHUNKS - 1) * CHUNK
PADK = LOOKBACK * CHUNK
REL_CLIP = 256
REL_TABLE = REL_CLIP + CHUNK
REL_PAD = 384
CONV_WIDTH = 4
LRU_C = 8.0
LN_EPS = 1e-5
DEPTH = 2
ALPHA = (2.0 * DEPTH) ** 0.25
NEG = -1e30
SB_DEAD = -104.0
GELU_K = math.sqrt(2.0 / math.pi)
GELU_C = 0.044715

ADAM_LR, ADAM_B1, ADAM_B2, ADAM_EPS, ADAM_WD, ADAM_STEP = 0.001, 0.9, 0.999, 1e-08, 0.01, 10

NN = (((1,), (0,)), ((), ()))
NT = (((1,), (1,)), ((), ()))
TN = (((0,), (0,)), ((), ()))

FQ, FK, FV, RX, RY, SQ, SK, SV, CQ, CK, CV = range(11)

NAMES = ['x', 'ln_in_g', 'ln_in_b', 'w_in', 'b_forget', 'conv_w', 'conv_b', 'w_r', 'b_r', 'w_i', 'b_i', 'lru_lambda',
         'rel_bias', 'w_branch', 'w_gate', 'b_gate', 'w_out', 'ln1_g', 'ln1_b', 'w_ff1', 'w_ff2', 'ln2_g', 'ln2_b']
WEIGHTS = NAMES[1:]


def _cp(sem=None):
    return pltpu.CompilerParams(dimension_semantics=sem, vmem_limit_bytes=VMEM_LIMIT)


def _iota(shape, dim):
    return lax.broadcasted_iota(I32, shape, dim)


def _sigmoid(x):
    return 1.0 / (1.0 + jnp.exp(-x))


def _log_sigmoid(x):
    return jnp.minimum(x, 0.0) - jnp.log(1.0 + jnp.exp(-jnp.abs(x)))


def _lane_tile(n, cap=1536):
    best = max(t for t in range(LANE, cap + 1, LANE) if n % t == 0)
    return n if best == LANE else best


def _pow2_rows(rows, cols, elems=262144):
    t = 8
    while t * 2 <= rows and t * 2 * cols <= elems and rows % (t * 2) == 0:
        t *= 2
    return t


def _position():
    return lax.axis_index("x"), lax.axis_index("y"), lax.axis_index("c")


class _Side:
    def __init__(self, operands, out_shape, sems, start, finish, aliases=(), parts=()):
        self.operands, self.out_shape, self.sems = list(operands), list(out_shape), list(sems)
        self.start, self.finish, self.aliases, self.parts = start, finish, list(aliases), parts
        self.results = None

    def set_results(self, res):
        self.results = list(res)
        off = 0
        for part in self.parts:
            part.set_results(res[off:off + len(part.out_shape)])
            off += len(part.out_shape)


def _merge_sides(a, b):
    ai, ao, asm = len(a.operands), len(a.out_shape), len(a.sems)

    def start(ins, outs, sems):
        a.start(ins[:ai], outs[:ao], sems[:asm])
        b.start(ins[ai:], outs[ao:], sems[asm:])

    def finish(ins, outs, sems):
        a.finish(ins[:ai], outs[:ao], sems[:asm])
        b.finish(ins[ai:], outs[ao:], sems[asm:])

    return _Side(a.operands + b.operands, a.out_shape + b.out_shape, a.sems + b.sems, start, finish,
                 a.aliases + [(i + ai, o + ao) for i, o in b.aliases], parts=(a, b))


def _chips_side(xs, gather):
    n = len(xs)

    def copies(ins, outs, sems, arrivals):
        send_sems, recv_sems, local_sems = sems
        x, y, c = _position()
        q = 2 * x + y
        chips = [(1 - x, y), (x, 1 - y), (1 - x, 1 - y)]

        def src(t, slot):
            return ins[t] if gather else ins[t].at[slot]

        def dst(t, slot):
            return outs[t].at[c, slot] if gather else outs[t].at[slot]

        def remote(t, j, landing):
            px, py = chips[j]
            return pltpu.make_async_remote_copy(
                src_ref=src(t, 2 * px + py), dst_ref=dst(t, landing), send_sem=send_sems.at[t, j],
                recv_sem=recv_sems.at[t, j], device_id=(px, py, c), device_id_type=MESH)

        local = [pltpu.make_async_copy(src(t, q), dst(t, q), local_sems.at[t]) for t in range(n)]
        sends = [remote(t, j, q) for t in range(n) for j in range(3)]
        if not arrivals:
            return local, sends, []
        return local, sends, [remote(t, j, 2 * px + py) for t in range(n) for j, (px, py) in enumerate(chips)]

    def start(ins, outs, sems):
        local, sends, _ = copies(ins, outs, sems, False)
        for cp in local + sends:
            cp.start()

    def finish(ins, outs, sems):
        local, sends, recvs = copies(ins, outs, sems, True)
        for cp in recvs:
            cp.wait_recv()
        for cp in sends:
            cp.wait_send()
        for cp in local:
            cp.wait()

    out_shape = [jax.ShapeDtypeStruct((2, 4) + a.shape if gather else a.shape, a.dtype) for a in xs]
    sems = [pltpu.SemaphoreType.DMA((n, 3)), pltpu.SemaphoreType.DMA((n, 3)), pltpu.SemaphoreType.DMA((n,))]
    return _Side(xs, out_shape, sems, start, finish)


def _cores_side(xs, gather):
    n = len(xs)
    m = 1 if gather else 4

    def copies(ins, outs, sems, arrivals):
        send_sems, recv_sems = sems
        x, y, c = _position()

        def remote(t, j, landing):
            s = outs[t].at[c] if gather else ins[t].at[j, 1 - c]
            d = outs[t].at[landing] if gather else outs[t].at[j]
            return pltpu.make_async_remote_copy(
                src_ref=s, dst_ref=d, send_sem=send_sems.at[t, j], recv_sem=recv_sems.at[t, j],
                device_id=(x, y, 1 - c), device_id_type=MESH)

        sends = [remote(t, j, c) for t in range(n) for j in range(m)]
        return sends, [remote(t, j, 1 - c) for t in range(n) for j in range(m)] if arrivals else []

    def start(ins, outs, sems):
        for cp in copies(ins, outs, sems, False)[0]:
            cp.start()

    def finish(ins, outs, sems):
        sends, recvs = copies(ins, outs, sems, True)
        for cp in recvs:
            cp.wait_recv()
        for cp in sends:
            cp.wait_send()

    if gather:
        out_shape = [jax.ShapeDtypeStruct(a.shape, a.dtype) for a in xs]
    else:
        out_shape = [jax.ShapeDtypeStruct((4,) + a.shape[2:], a.dtype) for a in xs]
    sems = [pltpu.SemaphoreType.DMA((n, m)), pltpu.SemaphoreType.DMA((n, m))]
    return _Side(xs, out_shape, sems, start, finish, aliases=[(t, t) for t in range(n)] if gather else [])


def _run_side(name, side):
    ni, no = len(side.operands), len(side.out_shape)

    def body(*refs):
        ins, outs, sems = refs[:ni], refs[ni:ni + no], refs[ni + no:]
        side.start(ins, outs, sems)
        side.finish(ins, outs, sems)

    side.set_results(pl.pallas_call(
        body, name=name, out_shape=side.out_shape, in_specs=[ANY] * ni, out_specs=[ANY] * no,
        input_output_aliases=dict(side.aliases), scratch_shapes=side.sems)(*side.operands))
    return side.results


def _pcall(body, operands, *, name, grid, in_specs, out_specs, out_shape, scratch_shapes=(), sem=None, aliases=None,
           side=None):
    if side is None:
        return pl.pallas_call(body, name=name, grid=grid, in_specs=list(in_specs), out_specs=list(out_specs),
                              out_shape=list(out_shape), scratch_shapes=list(scratch_shapes),
                              input_output_aliases=aliases or {}, compiler_params=_cp(sem))(*operands)
    ni, no, ns = len(in_specs), len(out_shape), len(scratch_shapes)
    si, so = len(side.operands), len(side.out_shape)

    def carrying(*refs):
        ins, sins = refs[:ni], refs[ni:ni + si]
        outs, souts = refs[ni + si:ni + si + no], refs[ni + si + no:ni + si + no + so]
        scratch, ssems = refs[ni + si + no + so:ni + si + no + so + ns], refs[ni + si + no + so + ns:]
        ids = [pl.program_id(a) for a in range(len(grid))]
        first = functools.reduce(jnp.logical_and, [i == 0 for i in ids])
        last = functools.reduce(jnp.logical_and, [i == g - 1 for i, g in zip(ids, grid)])

        @pl.when(first)
        def _():
            side.start(sins, souts, ssems)

        body(*ins, *outs, *scratch)

        @pl.when(last)
        def _():
            side.finish(sins, souts, ssems)

    joined = dict(aliases or {})
    joined.update({ni + i: no + o for i, o in side.aliases})
    res = pl.pallas_call(
        carrying, name=name, grid=grid, in_specs=[*in_specs, *[ANY] * si], out_specs=[*out_specs, *[ANY] * so],
        out_shape=[*out_shape, *side.out_shape], scratch_shapes=[*scratch_shapes, *side.sems],
        input_output_aliases=joined, compiler_params=_cp(("arbitrary",) * len(grid)))(*operands, *side.operands)
    side.set_results(res[no:])
    return res[:no]


def _add_core_halves(name, mine, other, c, out_dtype):
    _, _, rows, cols = mine.shape
    tr = _pow2_rows(rows, cols)

    def body(c_ref, a_ref, b_ref, o_ref):
        o_ref[...] = (a_ref[...].astype(F32) + b_ref[...].astype(F32)).astype(out_dtype)

    grid_spec = pltpu.PrefetchScalarGridSpec(
        num_scalar_prefetch=1, grid=(4, rows // tr),
        in_specs=[pl.BlockSpec((None, None, tr, cols), lambda j, i, c_ref: (j, c_ref[0], i, 0)),
                  pl.BlockSpec((None, tr, cols), lambda j, i, c_ref: (j, i, 0))],
        out_specs=pl.BlockSpec((None, tr, cols), lambda j, i, c_ref: (j, i, 0)))
    return pl.pallas_call(body, name=name, grid_spec=grid_spec,
                          out_shape=jax.ShapeDtypeStruct((4, rows, cols), out_dtype),
                          compiler_params=_cp(("parallel", "parallel")))(c, mine, other)


def _sum_parts(name, parts):
    p, rows, cols = parts.shape
    tr = _pow2_rows(rows, cols * p)

    def body(a_ref, o_ref):
        acc = a_ref[0]
        for k in range(1, p):
            acc = acc + a_ref[k]
        o_ref[...] = acc

    return pl.pallas_call(body, name=name, grid=(rows // tr,),
                          in_specs=[pl.BlockSpec((p, tr, cols), lambda i: (0, i, 0))],
                          out_specs=pl.BlockSpec((tr, cols), lambda i: (i, 0)),
                          out_shape=jax.ShapeDtypeStruct((rows, cols), F32),
                          compiler_params=_cp(("parallel",)))(parts)


def _mm(name, a, b, *, grid, a_spec, b_spec, dims, acc_shape, out_shape, out_specs, finish,
        extras=(), extra_specs=(), aliases=None, sem=("parallel", "parallel", "arbitrary"), side=None):
    nk, ne, no = grid[2], len(extras), len(out_shape)

    def body(*refs):
        a_ref, b_ref = refs[0], refs[1]
        ex, outs = refs[2:2 + ne], refs[2 + ne:2 + ne + no]
        ids = (pl.program_id(0), pl.program_id(1))
        def prod():
            return lax.dot_general(a_ref[...], b_ref[...], dims, preferred_element_type=F32)

        if nk == 1:
            finish(prod(), ex, outs, ids)
            return
        acc = refs[2 + ne + no]
        k = pl.program_id(2)

        @pl.when(k == 0)
        def _():
            acc[...] = prod()

        @pl.when(jnp.logical_and(k > 0, k < nk - 1))
        def _():
            acc[...] += prod()

        @pl.when(k == nk - 1)
        def _():
            finish(acc[...] + prod(), ex, outs, ids)

    return _pcall(body, (a, b, *extras), name=name, grid=grid, in_specs=[a_spec, b_spec, *extra_specs],
                  out_specs=out_specs, out_shape=out_shape,
                  scratch_shapes=[pltpu.VMEM(acc_shape, F32)] if nk > 1 else [], sem=sem, aliases=aliases, side=side)


def _store(dtype):
    def finish(acc, ex, outs, ids):
        outs[0][...] = acc.reshape(outs[0].shape).astype(dtype)
    return finish


def _layer_norm_rows(z, g, b):
    mu = jnp.mean(z, axis=1, keepdims=True)
    zc = z - mu
    var = jnp.mean(zc * zc, axis=1, keepdims=True)
    return zc * lax.rsqrt(var + LN_EPS) * g + b


def _mm_ln(name, a, w, l, resid, g, b, side=None):
    s, kdim = a.shape
    d = w.shape[2]
    tm, tk = min(512, s), min(1024, kdim)

    def finish(acc, ex, outs, ids):
        z = acc + ALPHA * ex[0][...]
        y = _layer_norm_rows(z, ex[1][...], ex[2][...])
        outs[0][...] = z
        outs[1][...] = y
        outs[2][...] = y.astype(BF16)

    row = pl.BlockSpec((tm, d), lambda m, n, k: (m, 0))
    vec = pl.BlockSpec((1, d), lambda m, n, k: (0, 0))
    return _mm(name, a, w, grid=(s // tm, 1, kdim // tk),
               a_spec=pl.BlockSpec((tm, tk), lambda m, n, k: (m, k)),
               b_spec=pl.BlockSpec((None, tk, d), lambda m, n, k: (l, k, 0)),
               dims=NN, acc_shape=(tm, d),
               out_shape=[jax.ShapeDtypeStruct((s, d), F32), jax.ShapeDtypeStruct((s, d), F32),
                          jax.ShapeDtypeStruct((s, d), BF16)],
               out_specs=[row, row, row], finish=finish,
               extras=(resid, g.reshape(1, d), b.reshape(1, d)), extra_specs=(row, vec, vec), side=side)


def _ln_fwd(name, x, g, b, side=None):
    s, d = x.shape
    tr = min(256, s)

    def body(x_ref, g_ref, b_ref, y_ref, yb_ref):
        y = _layer_norm_rows(x_ref[...], g_ref[...], b_ref[...])
        y_ref[...] = y
        yb_ref[...] = y.astype(BF16)

    row = pl.BlockSpec((tr, d), lambda i: (i, 0))
    vec = pl.BlockSpec((1, d), lambda i: (0, 0))
    return _pcall(body, (x, g.reshape(1, d), b.reshape(1, d)), name=name, grid=(s // tr,), in_specs=[row, vec, vec],
                  out_specs=[row, row],
                  out_shape=[jax.ShapeDtypeStruct((s, d), F32), jax.ShapeDtypeStruct((s, d), BF16)],
                  sem=("parallel",), side=side)


def _ln_bwd_rows(dyv, zz, g, dz_ref, dzb_ref, dg_ref, db_ref):
    mu = jnp.mean(zz, axis=1, keepdims=True)
    zc = zz - mu
    rstd = lax.rsqrt(jnp.mean(zc * zc, axis=1, keepdims=True) + LN_EPS)
    xhat = zc * rstd
    dg_ref[...] += jnp.sum(dyv * xhat, axis=0, keepdims=True)
    db_ref[...] += jnp.sum(dyv, axis=0, keepdims=True)
    dxh = dyv * g
    dz = rstd * (dxh - jnp.mean(dxh, axis=1, keepdims=True) - xhat * jnp.mean(dxh * xhat, axis=1, keepdims=True))
    dz_ref[...] = dz
    dzb_ref[...] = dz.astype(BF16)


def _ln_bwd(name, dy, z, g):
    s, d = z.shape
    tr = min(256, s)

    def body(dy_ref, z_ref, g_ref, dz_ref, dzb_ref, dg_ref, db_ref):
        @pl.when(pl.program_id(0) == 0)
        def _():
            dg_ref[...] = jnp.zeros(dg_ref.shape, F32)
            db_ref[...] = jnp.zeros(db_ref.shape, F32)

        _ln_bwd_rows(dy_ref[...], z_ref[...], g_ref[...], dz_ref, dzb_ref, dg_ref, db_ref)

    row = pl.BlockSpec((tr, d), lambda i: (i, 0))
    vec = pl.BlockSpec((1, d), lambda i: (0, 0))
    return _pcall(
        body, (dy, z, g.reshape(1, d)), name=name, grid=(s // tr,), in_specs=[row, row, vec],
        out_specs=[row, row, vec, vec],
        out_shape=[jax.ShapeDtypeStruct((s, d), F32), jax.ShapeDtypeStruct((s, d), BF16),
                   jax.ShapeDtypeStruct((1, d), F32), jax.ShapeDtypeStruct((1, d), F32)],
        sem=("arbitrary",))


def _loss_ln_bwd(name, y, target, z, g):
    s, d = y.shape
    tr = min(256, s)

    def body(y_ref, t_ref, z_ref, g_ref, loss_ref, dz_ref, dzb_ref, dg_ref, db_ref):
        @pl.when(pl.program_id(0) == 0)
        def _():
            loss_ref[...] = jnp.zeros(loss_ref.shape, F32)
            dg_ref[...] = jnp.zeros(dg_ref.shape, F32)
            db_ref[...] = jnp.zeros(db_ref.shape, F32)

        e = y_ref[...] - t_ref[...]
        loss_ref[...] += jnp.sum(e * e) * (0.5 / d)
        _ln_bwd_rows(e * (1.0 / d), z_ref[...], g_ref[...], dz_ref, dzb_ref, dg_ref, db_ref)

    row = pl.BlockSpec((tr, d), lambda i: (i, 0))
    vec = pl.BlockSpec((1, d), lambda i: (0, 0))
    return pl.pallas_call(
        body, name=name, grid=(s // tr,), in_specs=[row, row, row, vec],
        out_specs=[pl.BlockSpec((8, LANE), lambda i: (0, 0)), row, row, vec, vec],
        out_shape=[jax.ShapeDtypeStruct((8, LANE), F32), jax.ShapeDtypeStruct((s, d), F32),
                   jax.ShapeDtypeStruct((s, d), BF16), jax.ShapeDtypeStruct((1, d), F32),
                   jax.ShapeDtypeStruct((1, d), F32)],
        compiler_params=_cp(("arbitrary",)))(y, target, z, g.reshape(1, d))


def _scan_add(x, reverse):
    ts = x.shape[0]
    rows = _iota((ts, 1), 0)
    dist = 1
    while dist < ts:
        if reverse:
            x = x + jnp.where(rows < ts - dist, pltpu.roll(x, ts - dist, 0), 0.0)
        else:
            x = x + jnp.where(rows >= dist, pltpu.roll(x, dist, 0), 0.0)
        dist *= 2
    return x


def _scan_affine(a, b, reverse):
    ts = a.shape[0]
    rows = _iota((ts, 1), 0)
    dist = 1
    while dist < ts:
        shift = ts - dist if reverse else dist
        valid = rows < ts - dist if reverse else rows >= dist
        b = b + a * jnp.where(valid, pltpu.roll(b, shift, 0), 0.0)
        a = a * jnp.where(valid, pltpu.roll(a, shift, 0), 1.0)
        dist *= 2
    return a, b


def _cum_forget_fwd(name, fl, bias):
    s = fl.shape[0]
    ts = min(1024, s)

    def body(f_ref, b_ref, o_ref, carry):
        @pl.when(pl.program_id(0) == 0)
        def _():
            carry[...] = jnp.zeros(carry.shape, F32)

        o_ref[...] = _scan_add(_log_sigmoid(f_ref[...] + b_ref[...]), False) + carry[...]
        carry[...] = o_ref[pl.ds(ts - 1, 1), :]

    row = pl.BlockSpec((ts, LANE), lambda i: (i, 0))
    return pl.pallas_call(body, name=name, grid=(s // ts,),
                          in_specs=[row, pl.BlockSpec((1, LANE), lambda i: (0, 0))], out_specs=row,
                          out_shape=jax.ShapeDtypeStruct((s, LANE), F32),
                          scratch_shapes=[pltpu.VMEM((1, LANE), F32)],
                          compiler_params=_cp(("arbitrary",)))(fl, bias)


def _cum_forget_bwd(name, dcf, fl, bias):
    s = fl.shape[0]
    ts = min(1024, s)
    nb = s // ts

    def body(d_ref, f_ref, b_ref, o_ref, db_ref, carry):
        @pl.when(pl.program_id(0) == 0)
        def _():
            carry[...] = jnp.zeros(carry.shape, F32)
            db_ref[...] = jnp.zeros(db_ref.shape, F32)

        run = _scan_add(d_ref[...], True) + carry[...]
        carry[...] = jnp.sum(jnp.where(_iota((ts, 1), 0) == 0, run, 0.0), axis=0, keepdims=True)
        dfl = run * _sigmoid(-(f_ref[...] + b_ref[...]))
        o_ref[...] = dfl.astype(BF16)
        db_ref[...] += jnp.sum(dfl, axis=0, keepdims=True)

    row = pl.BlockSpec((ts, LANE), lambda i: (nb - 1 - i, 0))
    vec = pl.BlockSpec((1, LANE), lambda i: (0, 0))
    return pl.pallas_call(body, name=name, grid=(nb,), in_specs=[row, row, vec], out_specs=[row, vec],
                          out_shape=[jax.ShapeDtypeStruct((s, LANE), BF16), jax.ShapeDtypeStruct((1, LANE), F32)],
                          scratch_shapes=[pltpu.VMEM((1, LANE), F32)],
                          compiler_params=_cp(("arbitrary",)))(dcf, fl, bias)


def _fox_specs(s, nh, tq, tk):
    q = pl.BlockSpec((tq, HEAD), lambda h, i: (i, FQ * nh + h))
    k = pl.BlockSpec((s, HEAD), lambda h, i: (0, FK * nh + h))
    v = pl.BlockSpec((s, HEAD), lambda h, i: (0, FV * nh + h))
    col = pl.BlockSpec((None, tq, 1), lambda h, i: (h, i, 0))
    rowv = pl.BlockSpec((None, s // tk, 1, tk), lambda h, i: (h, 0, 0, 0))
    tile = pl.BlockSpec((tq, HEAD), lambda h, i: (i, h))
    full = pl.BlockSpec((s, HEAD), lambda h, i: (0, h))
    return q, k, v, col, rowv, tile, full


def _fox_tile(s):
    return min(512, s), min(1024, s)


def _fox_scores(q, k_ref, cfq, cfr_ref, kb, tk, scale, qpos=None):
    off = pl.multiple_of(kb * tk, tk)
    k = k_ref[pl.ds(off, tk), :]
    sc = lax.dot_general(q, k, NT, preferred_element_type=F32) * scale + cfq - cfr_ref[kb]
    mask = None
    if qpos is not None:
        mask = kb * tk + _iota((1, tk), 1) <= qpos
        sc = jnp.where(mask, sc, NEG)
    return sc, mask, k, off


def _fox_fwd(name, u, cf_col, cf_row, bw, side=None):
    s, nh = u.shape[0], bw // HEAD
    tq, tk = _fox_tile(s)
    scale = HEAD ** -0.5

    def body(q_ref, k_ref, v_ref, cfc_ref, cfr_ref, o_ref, lse_ref):
        i = pl.program_id(1)
        q, cfq = q_ref[...], cfc_ref[...]
        last = ((i + 1) * tq - 1) // tk

        def step(kb, carry, qpos=None):
            m, l, acc = carry
            sc, _, _, off = _fox_scores(q, k_ref, cfq, cfr_ref, kb, tk, scale, qpos)
            m2 = jnp.maximum(m, jnp.max(sc, axis=1, keepdims=True))
            p = jnp.exp(sc - m2)
            al = jnp.exp(m - m2)
            return (m2, al * l + jnp.sum(p, axis=1, keepdims=True),
                    al * acc + jnp.dot(p.astype(BF16), v_ref[pl.ds(off, tk), :], preferred_element_type=F32))

        init = (jnp.full((tq, 1), NEG, F32), jnp.zeros((tq, 1), F32), jnp.zeros((tq, HEAD), F32))
        m, l, acc = step(last, lax.fori_loop(0, last, step, init), i * tq + _iota((tq, 1), 0))
        o_ref[...] = (acc / l).astype(BF16)
        lse_ref[...] = m + jnp.log(l)

    q, k, v, col, rowv, tile, _ = _fox_specs(s, nh, tq, tk)
    return _pcall(body, (u, u, u, cf_col, cf_row), name=name, grid=(nh, s // tq), in_specs=[q, k, v, col, rowv],
                  out_specs=[tile, col],
                  out_shape=[jax.ShapeDtypeStruct((s, bw), BF16), jax.ShapeDtypeStruct((nh, s, 1), F32)],
                  sem=("parallel", "parallel"), side=side)


def _fox_bwd(name, u, cf_col, cf_row, o, do, lse, bw, side=None):
    s, nh = u.shape[0], bw // HEAD
    tq, tk = _fox_tile(s)
    nq = s // tq
    scale = HEAD ** -0.5

    def body(q_ref, k_ref, v_ref, cfc_ref, cfr_ref, o_ref, do_ref, lse_ref,
             dq_ref, dk_ref, dv_ref, dcc_ref, dcr_ref, dk_s, dv_s):
        i = pl.program_id(1)

        @pl.when(i == 0)
        def _():
            dk_s[...] = jnp.zeros(dk_s.shape, F32)
            dv_s[...] = jnp.zeros(dv_s.shape, F32)
            dcr_ref[...] = jnp.zeros(dcr_ref.shape, F32)

        q, dov, cfq, lse_q = q_ref[...], do_ref[...], cfc_ref[...], lse_ref[...]
        delta = jnp.sum(dov.astype(F32) * o_ref[...].astype(F32), axis=1, keepdims=True)
        last = ((i + 1) * tq - 1) // tk

        def step(kb, carry, qpos=None):
            dq, dcq = carry
            sc, mask, k, off = _fox_scores(q, k_ref, cfq, cfr_ref, kb, tk, scale, qpos)
            p = jnp.exp(sc - lse_q)
            if qpos is not None:
                p = jnp.where(mask, p, 0.0)
            dp = lax.dot_general(dov, v_ref[pl.ds(off, tk), :], NT, preferred_element_type=F32)
            ds = p * (dp - delta)
            dsb = ds.astype(BF16)
            dk_s[pl.ds(off, tk), :] += lax.dot_general(dsb, q, TN, preferred_element_type=F32)
            dv_s[pl.ds(off, tk), :] += lax.dot_general(p.astype(BF16), dov, TN, preferred_element_type=F32)
            dcr_ref[kb] += -jnp.sum(ds, axis=0, keepdims=True)
            return (dq + jnp.dot(dsb, k, preferred_element_type=F32), dcq + jnp.sum(ds, axis=1, keepdims=True))

        init = (jnp.zeros((tq, HEAD), F32), jnp.zeros((tq, 1), F32))
        dq, dcq = step(last, lax.fori_loop(0, last, step, init), i * tq + _iota((tq, 1), 0))
        dq_ref[...] = (dq * scale).astype(BF16)
        dcc_ref[...] = jnp.transpose(jnp.broadcast_to(dcq, (tq, LANE)))[:8, :]

        @pl.when(i == nq - 1)
        def _():
            dk_ref[...] = (dk_s[...] * scale).astype(BF16)
            dv_ref[...] = dv_s[...].astype(BF16)

    q, k, v, col, rowv, tile, full = _fox_specs(s, nh, tq, tk)
    by_query = pl.BlockSpec((None, None, 8, tq), lambda h, i: (h, i, 0, 0))
    return _pcall(
        body, (u, u, u, cf_col, cf_row, o, do, lse), name=name, grid=(nh, nq),
        in_specs=[q, k, v, col, rowv, tile, tile, col], out_specs=[tile, full, full, by_query, rowv],
        out_shape=[jax.ShapeDtypeStruct((s, bw), BF16)] * 3
        + [jax.ShapeDtypeStruct((nh, nq, 8, tq), F32), jax.ShapeDtypeStruct((nh, s // tk, 1, tk), F32)],
        scratch_shapes=[pltpu.VMEM((s, HEAD), F32), pltpu.VMEM((s, HEAD), F32)],
        sem=("arbitrary", "arbitrary"), side=side)


def _suffix_mm(x, ones_below):
    hi = x.astype(BF16)
    lo = (x - hi.astype(F32)).astype(BF16)
    return (jnp.dot(hi, ones_below, preferred_element_type=F32) + jnp.dot(lo, ones_below, preferred_element_type=F32))


def _sb_tile(q, k_ref, kb, tk, qpos, scale):
    off = pl.multiple_of(kb * tk, tk)
    k = k_ref[pl.ds(off, tk), :]
    z = lax.dot_general(q, k, NT, preferred_element_type=F32) * scale
    mask = kb * tk + _iota((1, tk), 1) < qpos
    lsn = -jnp.maximum(z, 0.0) - jnp.log(1.0 + jnp.exp(-jnp.abs(z)))
    return z, mask, lsn, jnp.where(mask, lsn, 0.0), k, off


def _sb_specs(s, nh, tq):
    q = pl.BlockSpec((tq, HEAD), lambda h, i: (i, SQ * nh + h))
    k = pl.BlockSpec((s, HEAD), lambda h, i: (0, SK * nh + h))
    v = pl.BlockSpec((s, HEAD), lambda h, i: (0, SV * nh + h))
    tile = pl.BlockSpec((tq, HEAD), lambda h, i: (i, h))
    full = pl.BlockSpec((s, HEAD), lambda h, i: (0, h))
    return q, k, v, tile, full


def _sb_fwd(name, u, bw):
    s, nh = u.shape[0], bw // HEAD
    tq = tk = 256
    scale = HEAD ** -0.5

    def body(q_ref, k_ref, v_ref, o_ref):
        i = pl.program_id(1)
        q = q_ref[...]
        qpos = i * tq + _iota((tq, 1), 0)
        later_keys = (_iota((tk, tk), 0) > _iota((tk, tk), 1)).astype(BF16)
        nk = (i * tq + tq + tk - 2) // tk

        def cond(st):
            return jnp.logical_and(st[0] < nk, st[3] > SB_DEAD)

        def step(st):
            j, c, acc, _ = st
            z, mask, lsn, lm, _, off = _sb_tile(q, k_ref, nk - 1 - j, tk, qpos, scale)
            a = jnp.where(mask, jnp.exp(lsn + z + c + _suffix_mm(lm, later_keys)), 0.0)
            acc = acc + jnp.dot(a.astype(BF16), v_ref[pl.ds(off, tk), :], preferred_element_type=F32)
            c = c + jnp.sum(lm, axis=1, keepdims=True)
            return j + 1, c, acc, jnp.max(c)

        init = (jnp.int32(0), jnp.zeros((tq, 1), F32), jnp.zeros((tq, HEAD), F32), jnp.float32(0.0))
        o_ref[...] = lax.while_loop(cond, step, init)[2].astype(BF16)

    q, k, v, tile, _ = _sb_specs(s, nh, tq)
    return _pcall(body, (u, u, u), name=name, grid=(nh, s // tq), in_specs=[q, k, v], out_specs=[tile],
                  out_shape=[jax.ShapeDtypeStruct((s, bw), BF16)], sem=("parallel", "parallel"))[0]


def _sb_bwd(name, u, do, bw):
    s, nh = u.shape[0], bw // HEAD
    tq = tk = 256
    nq = s // tq
    scale = HEAD ** -0.5

    def body(q_ref, k_ref, v_ref, do_ref, dq_ref, dk_ref, dv_ref, dk_s, dv_s):
        i = pl.program_id(1)

        @pl.when(i == 0)
        def _():
            dk_s[...] = jnp.zeros(dk_s.shape, F32)
            dv_s[...] = jnp.zeros(dv_s.shape, F32)

        q, dov = q_ref[...], do_ref[...]
        qpos = i * tq + _iota((tq, 1), 0)
        later_keys = (_iota((tk, tk), 0) > _iota((tk, tk), 1)).astype(BF16)
        this_and_later = (_iota((tk, tk), 0) >= _iota((tk, tk), 1)).astype(BF16)
        nk = (i * tq + tq + tk - 2) // tk

        def weights(j, c):
            z, mask, lsn, lm, k, off = _sb_tile(q, k_ref, nk - 1 - j, tk, qpos, scale)
            a = jnp.where(mask, jnp.exp(lsn + z + c + _suffix_mm(lm, later_keys)), 0.0)
            w = a * lax.dot_general(dov, v_ref[pl.ds(off, tk), :], NT, preferred_element_type=F32)
            return z, mask, lsn, lm, k, off, a, w

        def cond(st):
            return jnp.logical_and(st[0] < nk, st[3] > SB_DEAD)

        def step1(st):
            j, c, wc, _ = st
            _, _, _, lm, _, _, _, w = weights(j, c)
            c = c + jnp.sum(lm, axis=1, keepdims=True)
            return j + 1, c, wc + jnp.sum(w, axis=1, keepdims=True), jnp.max(c)

        zero = jnp.zeros((tq, 1), F32)
        live, _, total, _ = lax.while_loop(cond, step1, (jnp.int32(0), zero, zero, jnp.float32(0.0)))

        def step2(j, st):
            c, wc, dq = st
            z, mask, lsn, lm, k, off, a, w = weights(j, c)
            earlier = total - (wc + _suffix_mm(w, this_and_later))
            dz = jnp.where(mask, w * jnp.exp(lsn) - jnp.exp(lsn + z) * earlier, 0.0)
            dzb = dz.astype(BF16)
            dk_s[pl.ds(off, tk), :] += lax.dot_general(dzb, q, TN, preferred_element_type=F32)
            dv_s[pl.ds(off, tk), :] += lax.dot_general(a.astype(BF16), dov, TN, preferred_element_type=F32)
            return (c + jnp.sum(lm, axis=1, keepdims=True), wc + jnp.sum(w, axis=1, keepdims=True),
                    dq + jnp.dot(dzb, k, preferred_element_type=F32))

        dq = lax.fori_loop(0, live, step2, (zero, zero, jnp.zeros((tq, HEAD), F32)))[2]
        dq_ref[...] = (dq * scale).astype(BF16)

        @pl.when(i == nq - 1)
        def _():
            dk_ref[...] = (dk_s[...] * scale).astype(BF16)
            dv_ref[...] = dv_s[...].astype(BF16)

    q, k, v, tile, full = _sb_specs(s, nh, tq)
    return pl.pallas_call(
        body, name=name, grid=(nh, nq), in_specs=[q, k, v, tile], out_specs=[tile, full, full],
        out_shape=[jax.ShapeDtypeStruct((s, bw), BF16)] * 3,
        scratch_shapes=[pltpu.VMEM((s, HEAD), F32), pltpu.VMEM((s, HEAD), F32)],
        compiler_params=_cp(("arbitrary", "arbitrary")))(u, u, u, do)


BIAS_W = -(-(WIN + QBLK - 1) // LANE) * LANE


def _strip_onehot():
    col = _iota((1, BIAS_W), 1)
    ridx = jnp.clip(PADK + (QBLK - 1) - col, -(CHUNK - 1), REL_CLIP) + (CHUNK - 1)
    return (_iota((REL_PAD, BIAS_W), 0) == ridx).astype(BF16)


def _split2(x):
    hi = x.astype(BF16)
    return hi, (x - hi.astype(F32)).astype(BF16)


def _bias_expand(name, table, nh):
    def body(t_ref, o_ref, strip):
        table_f32 = t_ref[...]
        hi = table_f32.astype(BF16)
        mid, lo = _split2(table_f32 - hi.astype(F32))
        onehot = _strip_onehot()
        strip[...] = (jnp.dot(hi, onehot, preferred_element_type=F32) + jnp.dot(mid, onehot, preferred_element_type=F32)
                      + jnp.dot(lo, onehot, preferred_element_type=F32))
        row, kl = _iota((QBLK, 1), 0), _iota((1, WIN), 1)
        first = row - jnp.bitwise_and(row, CHUNK - 1)
        valid = jnp.logical_and(kl >= first, kl < first + BAND)
        for h in range(nh):
            rows = jnp.broadcast_to(strip[pl.ds(h, 1), :], (QBLK, BIAS_W))
            rolled = pltpu.roll(rows, BIAS_W - (QBLK - 1), 1, stride=1, stride_axis=0)
            o_ref[h] = jnp.where(valid, rolled[:, :WIN], NEG)

    return pl.pallas_call(body, name=name, out_shape=jax.ShapeDtypeStruct((nh, QBLK, WIN), F32),
                          in_specs=[pl.BlockSpec(memory_space=pltpu.VMEM)],
                          out_specs=pl.BlockSpec(memory_space=pltpu.VMEM),
                          scratch_shapes=[pltpu.VMEM((16, BIAS_W), F32)], compiler_params=_cp())(table)


def _bias_reduce(name, dss, nh):
    def body(x_ref, o_ref):
        onehot = _strip_onehot()
        flip = (_iota((QBLK, QBLK), 0) + _iota((QBLK, QBLK), 1) == QBLK - 1).astype(BF16)
        for h in range(nh):
            x = jnp.concatenate([x_ref[h], jnp.zeros((QBLK, BIAS_W - WIN), F32)], axis=1)
            hi, lo = _split2(x)
            back = jnp.dot(flip, hi, preferred_element_type=F32) + jnp.dot(flip, lo, preferred_element_type=F32)
            lined = pltpu.roll(back, 0, 1, stride=1, stride_axis=0)
            hi, lo = _split2(jnp.broadcast_to(jnp.sum(lined, axis=0, keepdims=True), (8, BIAS_W)))
            o_ref[h] = (lax.dot_general(hi, onehot, NT, preferred_element_type=F32)
                        + lax.dot_general(lo, onehot, NT, preferred_element_type=F32))

    return pl.pallas_call(body, name=name, out_shape=jax.ShapeDtypeStruct((nh, 8, REL_PAD), F32),
                          in_specs=[pl.BlockSpec(memory_space=pltpu.VMEM)],
                          out_specs=pl.BlockSpec(memory_space=pltpu.VMEM), compiler_params=_cp())(dss)


def _chunk_specs(s, nh):
    q = pl.BlockSpec((QBLK, HEAD), lambda h, i: (i, CQ * nh + h))
    kv = pl.BlockSpec((s + PADK, HEAD), lambda h, i: (0, h))
    bias = pl.BlockSpec((None, QBLK, WIN), lambda h, i: (h, 0, 0))
    tile = pl.BlockSpec((QBLK, HEAD), lambda h, i: (i, h))
    full = pl.BlockSpec((s, HEAD), lambda h, i: (0, h))
    return q, kv, bias, tile, full


def _chunk_probs(q, k_ref, b_ref, i, scale):
    off = pl.multiple_of(i * QBLK, QBLK)
    kw = k_ref[pl.ds(off, WIN), :]
    sc = lax.dot_general(q, kw, NT, preferred_element_type=F32) * scale + b_ref[...]
    sc = jnp.where(i * QBLK + _iota((1, WIN), 1) >= PADK, sc, NEG)
    p = jnp.exp(sc - jnp.max(sc, axis=1, keepdims=True))
    return p, jnp.sum(p, axis=1, keepdims=True), kw, off


def _chunk_fwd(name, u, kpad, vpad, bias, bw, side=None):
    s, nh = u.shape[0], bw // HEAD
    scale = HEAD ** -0.5

    def body(q_ref, k_ref, v_ref, b_ref, o_ref):
        p, l, _, off = _chunk_probs(q_ref[...], k_ref, b_ref, pl.program_id(1), scale)
        o = jnp.dot(p.astype(BF16), v_ref[pl.ds(off, WIN), :], preferred_element_type=F32)
        o_ref[...] = (o / l).astype(BF16)

    q, kv, bs, tile, _ = _chunk_specs(s, nh)
    return _pcall(body, (u, kpad, vpad, bias), name=name, grid=(nh, s // QBLK), in_specs=[q, kv, kv, bs],
                  out_specs=[tile], out_shape=[jax.ShapeDtypeStruct((s, bw), BF16)], sem=("parallel", "parallel"),
                  side=side)[0]


def _chunk_bwd(name, u, kpad, vpad, bias, do, bw):
    s, nh = u.shape[0], bw // HEAD
    nq = s // QBLK
    scale = HEAD ** -0.5

    def body(q_ref, k_ref, v_ref, b_ref, do_ref, dq_ref, dk_ref, dv_ref, dss_ref, dk_s, dv_s):
        i = pl.program_id(1)

        @pl.when(i == 0)
        def _():
            dk_s[...] = jnp.zeros(dk_s.shape, F32)
            dv_s[...] = jnp.zeros(dv_s.shape, F32)
            dss_ref[...] = jnp.zeros(dss_ref.shape, F32)

        q, dov = q_ref[...], do_ref[...]
        p, l, kw, off = _chunk_probs(q, k_ref, b_ref, i, scale)
        p = p / l
        dp = lax.dot_general(dov, v_ref[pl.ds(off, WIN), :], NT, preferred_element_type=F32)
        ds = p * (dp - jnp.sum(p * dp, axis=1, keepdims=True))
        dsb = ds.astype(BF16)
        dq_ref[...] = (jnp.dot(dsb, kw, preferred_element_type=F32) * scale).astype(BF16)
        dk_s[pl.ds(off, WIN), :] += lax.dot_general(dsb, q, TN, preferred_element_type=F32)
        dv_s[pl.ds(off, WIN), :] += lax.dot_general(p.astype(BF16), dov, TN, preferred_element_type=F32)
        dss_ref[...] += ds

        @pl.when(i == nq - 1)
        def _():
            dk_ref[...] = (dk_s[pl.ds(PADK, s), :] * scale).astype(BF16)
            dv_ref[...] = dv_s[pl.ds(PADK, s), :].astype(BF16)

    q, kv, bs, tile, full = _chunk_specs(s, nh)
    return pl.pallas_call(
        body, name=name, grid=(nh, nq), in_specs=[q, kv, kv, bs, tile], out_specs=[tile, full, full, bs],
        out_shape=[jax.ShapeDtypeStruct((s, bw), BF16)] * 3 + [jax.ShapeDtypeStruct((nh, QBLK, WIN), F32)],
        scratch_shapes=[pltpu.VMEM((s + PADK, HEAD), F32), pltpu.VMEM((s + PADK, HEAD), F32)],
        compiler_params=_cp(("arbitrary", "arbitrary")))(u, kpad, vpad, bias, do)


def _gelu_parts(y):
    th = jnp.tanh(GELU_K * (y + GELU_C * y * y * y))
    return 0.5 * y * (1.0 + th), th


def _block_diag(xb16, w_ref, nh, dims):
    return jnp.concatenate(
        [lax.dot_general(xb16[:, n * HEAD:(n + 1) * HEAD], w_ref[n], dims, preferred_element_type=F32)
         for n in range(nh)], axis=1)


def _lru_gates(ext, cw_ref, cb_ref, wr_ref, br_ref, wi_ref, bi_ref, lam_ref, ts, nh):
    shifted = [pltpu.roll(ext, CONV_WIDTH - 1 - j, 0)[8:, :] if j < CONV_WIDTH - 1 else ext[8:, :]
               for j in range(CONV_WIDTH)]
    xc = cb_ref[...]
    for j in range(CONV_WIDTH):
        xc = xc + shifted[j] * cw_ref[pl.ds(j, 1), :]
    xcb = xc.astype(BF16)
    r = _sigmoid(_block_diag(xcb, wr_ref, nh, NN) + br_ref[...])
    gi = _sigmoid(_block_diag(xcb, wi_ref, nh, NN) + bi_ref[...])
    lsl = _log_sigmoid(lam_ref[...])
    la = LRU_C * r * lsl
    a = jnp.exp(la)
    e2 = jnp.exp(2.0 * la)
    mult = jnp.sqrt(-jnp.tanh(la) * (e2 + 1.0))
    return shifted, xc, xcb, r, gi, lsl, a, e2, mult


def _lru_param_specs(bw, nh):
    vec = pl.BlockSpec((1, bw), lambda i: (0, 0))
    conv = pl.BlockSpec((8, bw), lambda i: (0, 0))
    blocks = pl.BlockSpec((nh, HEAD, HEAD), lambda i: (0, 0, 0))
    return [conv, vec, blocks, vec, blocks, vec, vec]


def _lru_fwd(name, u, params, bw):
    s, nh = u.shape[0], bw // HEAD
    ts = min(512, s)

    def body(rx_ref, ry_ref, cw_ref, cb_ref, wr_ref, br_ref, wi_ref, bi_ref, lam_ref, o_ref, h_ref, tail, hcar):
        @pl.when(pl.program_id(0) == 0)
        def _():
            tail[...] = jnp.zeros(tail.shape, F32)
            hcar[...] = jnp.zeros(hcar.shape, F32)

        rx = rx_ref[...].astype(F32)
        ext = jnp.concatenate([tail[...], rx], axis=0)
        tail[...] = rx[ts - 8:, :]
        _, xc, _, _, gi, _, a, _, mult = _lru_gates(ext, cw_ref, cb_ref, wr_ref, br_ref, wi_ref, bi_ref, lam_ref, ts, nh)
        acum, bcum = _scan_affine(a, mult * (gi * xc), False)
        h_ref[...] = bcum + acum * hcar[...]
        hcar[...] = h_ref[pl.ds(ts - 1, 1), :]
        o_ref[...] = (h_ref[...] * _gelu_parts(ry_ref[...].astype(F32))[0]).astype(BF16)

    row = pl.BlockSpec((ts, bw), lambda i: (i, 0))
    return pl.pallas_call(
        body, name=name, grid=(s // ts,),
        in_specs=[pl.BlockSpec((ts, bw), lambda i: (i, RX)), pl.BlockSpec((ts, bw), lambda i: (i, RY))]
        + _lru_param_specs(bw, nh),
        out_specs=[row, row],
        out_shape=[jax.ShapeDtypeStruct((s, bw), BF16), jax.ShapeDtypeStruct((s, bw), F32)],
        scratch_shapes=[pltpu.VMEM((8, bw), F32), pltpu.VMEM((1, bw), F32)],
        compiler_params=_cp(("arbitrary",)))(u, u, *params)


def _lru_bwd(name, u, h, do, params, bw):
    s, nh = u.shape[0], bw // HEAD
    ts = min(512, s)
    nb = s // ts
    t8 = ts // 8

    def body(rx_ref, rxp_ref, ry_ref, h_ref, hp_ref, do_ref, cw_ref, cb_ref, wr_ref, br_ref, wi_ref, bi_ref, lam_ref,
             drx_ref, dry_ref, dcw_ref, dcb_ref, dwr_ref, dbr_ref, dwi_ref, dbi_ref, dlam_ref, gcar, head):
        i = pl.program_id(0)
        first = i == nb - 1

        @pl.when(i == 0)
        def _():
            gcar[...] = jnp.zeros(gcar.shape, F32)
            head[...] = jnp.zeros(head.shape, F32)
            for ref in (dcw_ref, dcb_ref, dwr_ref, dbr_ref, dwi_ref, dbi_ref, dlam_ref):
                ref[...] = jnp.zeros(ref.shape, F32)

        rows = _iota((ts, 1), 0)
        rx = rx_ref[...].astype(F32)
        before = jnp.where(first, 0.0, rxp_ref[...].astype(F32))
        ext = jnp.concatenate([before, rx], axis=0)
        shifted, xc, xcb, r, gi, lsl, a, e2, mult = _lru_gates(
            ext, cw_ref, cb_ref, wr_ref, br_ref, wi_ref, bi_ref, lam_ref, ts, nh)

        ry = ry_ref[...].astype(F32)
        gel, th = _gelu_parts(ry)
        dgel = 0.5 * (1.0 + th) + 0.5 * ry * (1.0 - th * th) * GELU_K * (1.0 + 3.0 * GELU_C * ry * ry)
        dov = do_ref[...].astype(F32)
        hv = h_ref[...]
        dry_ref[...] = (dov * hv * dgel).astype(BF16)

        coef = jnp.where(rows < ts - 1, pltpu.roll(a, ts - 1, 0), 0.0)
        dh_in = dov * gel + jnp.where(rows == ts - 1, gcar[...], 0.0)
        dh = _scan_affine(coef, dh_in, True)[1]
        gcar[...] = jnp.sum(jnp.where(rows == 0, a * dh, 0.0), axis=0, keepdims=True)

        hprev = jnp.where(first, 0.0, hp_ref[...])
        hm1 = pltpu.roll(jnp.concatenate([hprev, hv], axis=0), 1, 0)[8:, :]
        dgx = dh * mult
        dla = dh * hm1 * a - dh * gi * xc * (e2 / mult)
        dpre_r = dla * (LRU_C * lsl) * r * (1.0 - r)
        dpre_i = dgx * xc * gi * (1.0 - gi)
        dlam_ref[...] += jnp.sum(dla * r, axis=0, keepdims=True) * (LRU_C * _sigmoid(-lam_ref[...]))
        dbr_ref[...] += jnp.sum(dpre_r, axis=0, keepdims=True)
        dbi_ref[...] += jnp.sum(dpre_i, axis=0, keepdims=True)
        drb, dib = dpre_r.astype(BF16), dpre_i.astype(BF16)
        for n in range(nh):
            cols = slice(n * HEAD, (n + 1) * HEAD)
            dwr_ref[n] += lax.dot_general(xcb[:, cols], drb[:, cols], TN, preferred_element_type=F32)
            dwi_ref[n] += lax.dot_general(xcb[:, cols], dib[:, cols], TN, preferred_element_type=F32)
        dxc = dgx * gi + _block_diag(drb, wr_ref, nh, NT) + _block_diag(dib, wi_ref, nh, NT)

        dcb_ref[...] += jnp.sum(dxc, axis=0, keepdims=True)
        for j in range(CONV_WIDTH):
            dcw_ref[pl.ds(j, 1), :] += jnp.sum(dxc * shifted[j], axis=0, keepdims=True)
        ext2 = jnp.concatenate([dxc, head[...]], axis=0)
        head[...] = dxc[:8, :]
        drx = dxc * cw_ref[pl.ds(CONV_WIDTH - 1, 1), :]
        for j in range(CONV_WIDTH - 1):
            up = CONV_WIDTH - 1 - j
            drx = drx + pltpu.roll(ext2, ts + 8 - up, 0)[:ts, :] * cw_ref[pl.ds(j, 1), :]
        drx_ref[...] = drx.astype(BF16)

    def blk(col):
        return lambda i: (nb - 1 - i, col)

    def prev8(col):
        return lambda i: (jnp.maximum((nb - 1 - i) * t8 - 1, 0), col)

    vec = pl.BlockSpec((1, bw), lambda i: (0, 0))
    conv = pl.BlockSpec((8, bw), lambda i: (0, 0))
    blocks = pl.BlockSpec((nh, HEAD, HEAD), lambda i: (0, 0, 0))
    return pl.pallas_call(
        body, name=name, grid=(nb,),
        in_specs=[pl.BlockSpec((ts, bw), blk(RX)), pl.BlockSpec((8, bw), prev8(RX)), pl.BlockSpec((ts, bw), blk(RY)),
                  pl.BlockSpec((ts, bw), blk(0)), pl.BlockSpec((8, bw), prev8(0)), pl.BlockSpec((ts, bw), blk(0))]
        + _lru_param_specs(bw, nh),
        out_specs=[pl.BlockSpec((ts, bw), blk(0)), pl.BlockSpec((ts, bw), blk(0)), conv, vec, blocks, vec, blocks, vec, vec],
        out_shape=[jax.ShapeDtypeStruct((s, bw), BF16)] * 2
        + [jax.ShapeDtypeStruct((8, bw), F32), jax.ShapeDtypeStruct((1, bw), F32),
           jax.ShapeDtypeStruct((nh, HEAD, HEAD), F32), jax.ShapeDtypeStruct((1, bw), F32),
           jax.ShapeDtypeStruct((nh, HEAD, HEAD), F32), jax.ShapeDtypeStruct((1, bw), F32),
           jax.ShapeDtypeStruct((1, bw), F32)],
        scratch_shapes=[pltpu.VMEM((1, bw), F32), pltpu.VMEM((8, bw), F32)],
        compiler_params=_cp(("arbitrary",)))(u, u, u, h, h, do, *params)


def _gate_merge(name, xb, w_gate, b_gate, o_all, w_branch, l, side=None):
    s, d = xb.shape
    bw = o_all.shape[2]
    tm, tn = min(1024, s), min(256, d)

    def body(x_ref, wg_ref, bg_ref, o_ref, wb_ref, m_ref, g_ref, p_ref):
        x = x_ref[...]
        acc = jnp.zeros((tm, tn), F32)
        for g in range(4):
            gate = _sigmoid(jnp.dot(x, wg_ref[g], preferred_element_type=F32) + bg_ref[g])
            proj = jnp.dot(o_ref[g], wb_ref[g], preferred_element_type=F32)
            term = gate * proj
            g_ref[g] = gate.astype(BF16)
            p_ref[g] = (term * (1.0 - gate)).astype(BF16)
            acc = acc + term
        m_ref[...] = acc.astype(BF16)

    quad = pl.BlockSpec((4, tm, tn), lambda n, m: (0, m, n))
    return _pcall(
        body, (xb, w_gate, b_gate, o_all, w_branch), name=name, grid=(d // tn, s // tm),
        in_specs=[pl.BlockSpec((tm, d), lambda n, m: (m, 0)),
                  pl.BlockSpec((None, 4, d, tn), lambda n, m: (0, 0, 0, n)),
                  pl.BlockSpec((None, 4, 1, tn), lambda n, m: (l, 0, 0, n)),
                  pl.BlockSpec((4, tm, bw), lambda n, m: (0, m, 0)),
                  pl.BlockSpec((None, 4, bw, tn), lambda n, m: (0, 0, 0, n))],
        out_specs=[pl.BlockSpec((tm, tn), lambda n, m: (m, n)), quad, quad],
        out_shape=[jax.ShapeDtypeStruct((s, d), BF16), jax.ShapeDtypeStruct((4, s, d), BF16),
                   jax.ShapeDtypeStruct((4, s, d), BF16)],
        sem=("parallel", "parallel"), side=side)


def _adamw(name, w, m, v, parts, layer=0, layers=1, earlier=None):
    cols = w.shape[1]
    p, rows = parts.shape[0], parts.shape[1]
    tr = _pow2_rows(rows, cols * max(1, p // 2))
    nb = rows // tr
    c1 = 1.0 - ADAM_B1 ** ADAM_STEP
    c2 = 1.0 - ADAM_B2 ** ADAM_STEP

    def body(w_ref, m_ref, v_ref, g_ref, *rest):
        go_ref, do_ref, mo_ref, vo_ref = rest[-4:]
        g = g_ref[0].astype(F32)
        for k in range(1, p):
            g = g + g_ref[k].astype(F32)
        m2 = ADAM_B1 * m_ref[...] + (1.0 - ADAM_B1) * g
        v2 = ADAM_B2 * v_ref[...] + (1.0 - ADAM_B2) * (g * g)
        go_ref[...] = g
        do_ref[...] = -ADAM_LR * ((m2 / c1) / (jnp.sqrt(v2 / c2) + ADAM_EPS) + ADAM_WD * w_ref[...])
        mo_ref[...] = m2
        vo_ref[...] = v2

    row = pl.BlockSpec((tr, cols), lambda i: (layer * nb + i, 0))
    held = list(earlier) if earlier is not None else []
    return pl.pallas_call(
        body, name=name, grid=(nb,),
        in_specs=[row, row, row, pl.BlockSpec((p, tr, cols), lambda i: (0, i, 0))] + [ANY] * len(held),
        out_specs=[row] * 4, out_shape=[jax.ShapeDtypeStruct((layers * rows, cols), F32)] * 4,
        input_output_aliases={4 + k: k for k in range(len(held))},
        compiler_params=_cp(("parallel",)))(w, m, v, parts, *held)


PACK_ROWS = 512


def _pack(arrays):
    rows = []
    for a in arrays:
        flat = a.astype(F32).reshape(-1)
        rows.append(jnp.pad(flat, (0, (-flat.shape[0]) % LANE)).reshape(-1, LANE))
    rows = jnp.concatenate(rows)
    return jnp.pad(rows, ((0, (-rows.shape[0]) % PACK_ROWS), (0, 0)))


def _unpack(packed, shapes):
    out, row = [], 0
    for shp in shapes:
        n = math.prod(shp)
        nrows = -(-n // LANE)
        out.append(packed[row:row + nrows].reshape(-1)[:n].reshape(shp))
        row += nrows
    return out


def _unshard(gathered, axis):
    block = gathered.shape[2:]
    full = jnp.swapaxes(gathered, 0, 1).reshape((N_DEV,) + block)
    full = jnp.moveaxis(full, 0, axis)
    return full.reshape(block[:axis] + (N_DEV * block[axis],) + block[axis + 1:])


def kernel(x, ln_in_g, ln_in_b, w_in, b_forget, conv_w, conv_b, w_r, b_r, w_i, b_i, lru_lambda, rel_bias, w_branch, w_gate, b_gate, w_out, ln1_g, ln1_b, w_ff1, w_ff2, ln2_g, ln2_b, loss_target, m_ln_in_g, m_ln_in_b, m_w_in, m_b_forget, m_conv_w, m_conv_b, m_w_r, m_b_r, m_w_i, m_b_i, m_lru_lambda, m_rel_bias, m_w_branch, m_w_gate, m_b_gate, m_w_out, m_ln1_g, m_ln1_b, m_w_ff1, m_w_ff2, m_ln2_g, m_ln2_b, v_ln_in_g, v_ln_in_b, v_w_in, v_b_forget, v_conv_w, v_conv_b, v_w_r, v_b_r, v_w_i, v_b_i, v_lru_lambda, v_rel_bias, v_w_branch, v_w_gate, v_b_gate, v_w_out, v_ln1_g, v_ln1_b, v_w_ff1, v_w_ff2, v_ln2_g, v_ln2_b):
    given = dict(zip(
        NAMES + ['loss_target'] + ['m_' + n for n in WEIGHTS] + ['v_' + n for n in WEIGHTS],
        (x, ln_in_g, ln_in_b, w_in, b_forget, conv_w, conv_b, w_r, b_r, w_i, b_i, lru_lambda, rel_bias, w_branch, w_gate, b_gate, w_out, ln1_g, ln1_b, w_ff1, w_ff2, ln2_g, ln2_b, loss_target, m_ln_in_g, m_ln_in_b, m_w_in, m_b_forget, m_conv_w, m_conv_b, m_w_r, m_b_r, m_w_i, m_b_i, m_lru_lambda, m_rel_bias, m_w_branch, m_w_gate, m_b_gate, m_w_out, m_ln1_g, m_ln1_b, m_w_ff1, m_w_ff2, m_ln2_g, m_ln2_b, v_ln_in_g, v_ln_in_b, v_w_in, v_b_forget, v_conv_w, v_conv_b, v_w_r, v_b_r, v_w_i, v_b_i, v_lru_lambda, v_rel_bias, v_w_branch, v_w_gate, v_b_gate, v_w_out, v_ln1_g, v_ln1_b, v_w_ff1, v_w_ff2, v_ln2_g, v_ln2_b)))

    s, d = x.shape[1], x.shape[2]
    nl = w_in.shape[0]
    bw = d // 4
    nh = bw // HEAD
    nu = 11 * bw
    rs = d // N_DEV
    dff = w_ff1.shape[2] * N_DEV
    fs = dff // N_DEV
    cs = d // N_DEV
    assert nl == DEPTH and nh * HEAD == bw and s % 1024 == 0 and d % 1024 == 0

    xi, yi, ci = _position()
    dev = 4 * xi + 2 * yi + ci
    c_arr = jnp.reshape(ci, (1,)).astype(I32)

    w_main = jnp.concatenate(
        [w_in[..., :3 * bw], w_in[..., 3 * bw + nh:],
         jnp.pad(w_in[..., 3 * bw:3 * bw + nh], ((0, 0), (0, 0), (0, LANE - nh)))], axis=-1).astype(BF16)
    nue = nu + LANE
    small_shapes = [conv_w.shape, rel_bias.shape, b_gate.shape]
    shard = {'main': w_main, 'branch': w_branch.astype(BF16), 'gate': w_gate.astype(BF16),
             'out': w_out.astype(BF16), 'ff1': w_ff1.astype(BF16), 'ff2': w_ff2.astype(BF16)}
    shard_axis = {'main': 1, 'branch': 3, 'gate': 2, 'out': 1, 'ff1': 2, 'ff2': 1}
    W = [dict() for _ in range(nl)]

    def gather_chips(l, keys, extra=()):
        side = _chips_side([shard[k][l:l + 1] for k in keys] + list(extra), True)
        side.todo = (l, keys)
        return side

    def gather_cores(chips):
        side = _cores_side(chips.results, True)
        side.todo = chips.todo
        return side

    def arrived(cores):
        l, keys = cores.todo
        for k, res in zip(keys, cores.results):
            W[l][k] = _unshard(res, shard_axis[k])
        return cores.results[len(keys):]

    xs = x[0]
    first = gather_chips(0, ['main'], [_pack([conv_w, rel_bias, b_gate])])
    h0, h0b = _ln_fwd("ln_in", xs, ln_in_g, ln_in_b, side=first)
    first = gather_cores(first)
    _run_side("gather_cores_first", first)
    small = arrived(first)[0]
    small = jnp.swapaxes(small, 0, 1).reshape((N_DEV,) + small.shape[2:])
    small = [_unpack(small[j], small_shapes) for j in range(N_DEV)]
    conv_w_full = jnp.concatenate([small[j][0] for j in range(N_DEV)], axis=-1)
    rel_bias_full = jnp.concatenate([small[j][1] for j in range(N_DEV)], axis=-1)
    b_gate_full = jnp.concatenate([small[j][2] for j in range(N_DEV)], axis=-1)
    b_gate4 = b_gate_full.reshape(nl, 4, 1, d)

    def lru_params(l):
        return (jnp.pad(conv_w_full[l], ((0, 8 - CONV_WIDTH), (0, 0))), conv_b[l].reshape(1, bw),
                w_r[l].astype(BF16), b_r[l].reshape(1, bw), w_i[l].astype(BF16), b_i[l].reshape(1, bw),
                lru_lambda[l].reshape(1, bw))

    def bias_rows(l):
        return jnp.pad(rel_bias_full[l], ((0, 16 - nh), (0, REL_PAD - REL_TABLE)))

    tm = min(1024, s)
    tkk = min(2048, d)

    saved = []
    cur, curb = h0, h0b
    chips = {}
    for l in range(nl):
        side = None
        if l == 0:
            side = chips['b0'] = gather_chips(0, ['gate', 'branch'])
        else:
            side = last_cores = gather_cores(chips.pop('d1'))
        u = _mm(f"w_in_{l}", curb, W[l]['main'], grid=(s // tm, nu // bw, d // tkk),
                a_spec=pl.BlockSpec((tm, tkk), lambda m, n, k: (m, k)),
                b_spec=pl.BlockSpec((None, tkk, bw), lambda m, n, k: (0, k, n)),
                dims=NN, acc_shape=(tm, bw), out_shape=[jax.ShapeDtypeStruct((s, nu), BF16)],
                out_specs=[pl.BlockSpec((tm, bw), lambda m, n, k: (m, n))], finish=_store(BF16), side=side)[0]
        if l == 1:
            arrived(last_cores)
        fl = _mm(f"w_forget_{l}", curb, W[l]['main'], grid=(s // tm, 1, d // tkk),
                 a_spec=pl.BlockSpec((tm, tkk), lambda m, n, k: (m, k)),
                 b_spec=pl.BlockSpec((None, tkk, LANE), lambda m, n, k: (0, k, nu // LANE)),
                 dims=NN, acc_shape=(tm, LANE), out_shape=[jax.ShapeDtypeStruct((s, LANE), F32)],
                 out_specs=[pl.BlockSpec((tm, LANE), lambda m, n, k: (m, 0))], finish=_store(F32))[0]
        bf_row = jnp.pad(b_forget[l], (0, LANE - nh)).reshape(1, LANE)
        cf = _cum_forget_fwd(f"cum_forget_{l}", fl, bf_row)
        tkf = _fox_tile(s)[1]
        cf_heads = cf[:, :nh].T
        cf_col = cf_heads.reshape(nh, s, 1)
        cf_row = cf_heads.reshape(nh, s // tkf, 1, tkf)
        side = None
        if l == 0:
            b0 = gather_cores(chips.pop('b0'))
            chips['c0'] = gather_chips(0, ['out', 'ff1', 'ff2'])
            side = _merge_sides(b0, chips['c0'])
        o_fox, lse = _fox_fwd(f"fox_fwd_{l}", u, cf_col, cf_row, bw, side=side)
        if l == 0:
            arrived(b0)
        lp = lru_params(l)
        o_lru, hstate = _lru_fwd(f"lru_fwd_{l}", u, lp, bw)
        o_sb = _sb_fwd(f"sb_fwd_{l}", u, bw)
        bias = _bias_expand(f"bias_expand_{l}", bias_rows(l), nh)
        kpad = jnp.pad(u[:, CK * bw:(CK + 1) * bw], ((PADK, 0), (0, 0)))
        vpad = jnp.pad(u[:, CV * bw:(CV + 1) * bw], ((PADK, 0), (0, 0)))
        side = gather_cores(chips.pop('c0')) if l == 0 else None
        o_ch = _chunk_fwd(f"chunk_fwd_{l}", u, kpad, vpad, bias, bw, side=side)
        if l == 0:
            arrived(side)
        o_all = jnp.stack([o_fox, o_lru, o_sb, o_ch])
        side = None
        if l == 0:
            side = chips['a1'] = gather_chips(1, ['main', 'gate', 'branch'])
        merged, gates, projs = _gate_merge(f"gate_merge_{l}", curb, W[l]['gate'], b_gate4, o_all, W[l]['branch'], l,
                                           side=side)
        side = gather_cores(chips.pop('a1')) if l == 0 else None
        z1, x1, x1b = _mm_ln(f"w_out_ln1_{l}", merged, W[l]['out'], 0, cur, ln1_g[l], ln1_b[l], side=side)
        if l == 0:
            arrived(side)
        tn1 = min(1024, dff)

        def ff1_finish(acc, ex, outs, ids):
            outs[0][...] = acc.astype(BF16)
            r = jnp.maximum(acc, 0.0)
            outs[1][...] = (r * r).astype(BF16)

        side = None
        if l == 0:
            side = chips['c1'] = gather_chips(1, ['out', 'ff1'])
        hp, hid = _mm(f"w_ff1_{l}", x1b, W[l]['ff1'], grid=(s // tm, dff // tn1, d // tkk),
                      a_spec=pl.BlockSpec((tm, tkk), lambda m, n, k: (m, k)),
                      b_spec=pl.BlockSpec((None, tkk, tn1), lambda m, n, k: (0, k, n)),
                      dims=NN, acc_shape=(tm, tn1),
                      out_shape=[jax.ShapeDtypeStruct((s, dff), BF16)] * 2,
                      out_specs=[pl.BlockSpec((tm, tn1), lambda m, n, k: (m, n))] * 2, finish=ff1_finish, side=side)
        side = None
        if l == 0:
            c1 = gather_cores(chips.pop('c1'))
            chips['d1'] = gather_chips(1, ['ff2'])
            side = _merge_sides(c1, chips['d1'])
        z2, x2, x2b = _mm_ln(f"w_ff2_ln2_{l}", hid, W[l]['ff2'], 0, x1, ln2_g[l], ln2_b[l], side=side)
        if l == 0:
            arrived(c1)
        saved.append(dict(xin=cur, xinb=curb, u=u, fl=fl, bf_row=bf_row, cf_col=cf_col, cf_row=cf_row, o_fox=o_fox,
                          lse=lse, lp=lp, hstate=hstate, bias=bias, kpad=kpad, vpad=vpad, o_all=o_all, merged=merged,
                          gates=gates, projs=projs, z1=z1, x1=x1, x1b=x1b, hp=hp, hid=hid, z2=z2))
        cur, curb = x2, x2b

    loss_tile, *last_ln = _loss_ln_bwd("loss_ln2_bwd", cur, loss_target[0], saved[nl - 1]['z2'], ln2_g[nl - 1])
    loss = lax.psum(loss_tile[0, 0], ("x", "y", "c"))
    dcur = None

    big = [dict() for _ in range(nl)]
    reduced = [dict() for _ in range(nl)]
    sm = {n: [None] * nl for n in ['b_forget', 'conv_w', 'conv_b', 'w_r', 'b_r', 'w_i', 'b_i', 'lru_lambda', 'rel_bias',
                                   'b_gate', 'ln1_g', 'ln1_b', 'ln2_g', 'ln2_b']}

    def split_columns(acc, ex, outs, ids):
        for j in range(N_DEV):
            outs[0][j] = acc[:, j * cs:(j + 1) * cs].astype(BF16)

    def grad_mm(key, name, a, b, *, shape, grid, a_spec, b_spec, out_spec, acc_shape, finish=_store(BF16)):
        big[l][key] = _mm(name, a, b, grid=grid, a_spec=a_spec, b_spec=b_spec, dims=TN, acc_shape=acc_shape,
                          out_shape=[jax.ShapeDtypeStruct(shape, BF16)], out_specs=[out_spec], finish=finish)[0]

    def reduce_cores(l, keys):
        side = _cores_side([big[l][k].reshape((4, 2) + big[l][k].shape[1:]) for k in keys], False)
        side.todo = (l, keys)
        return side

    def reduce_chips(cores):
        l, keys = cores.todo
        partial = []
        for k, mine, other in zip(keys, cores.operands, cores.results):
            cols = mine.shape[-1]
            rows = math.prod(mine.shape[2:]) // cols
            partial.append(_add_core_halves(f"add_cores_{k}_{l}", mine.reshape(4, 2, rows, cols),
                                            other.reshape(4, rows, cols), c_arr, BF16))
        side = _chips_side(partial, False)
        side.todo = (l, keys)
        return side

    def reduction_done(chips_side):
        l, keys = chips_side.todo
        reduced[l].update(zip(keys, chips_side.results))

    GROUP1, GROUP2 = ['w_ff2', 'w_ff1', 'w_out', 'w_branch', 'w_gate'], ['w_main']
    pending = None

    tks = min(2048, s)
    tmr = min(1024, d)
    nsh = tmr // rs

    for l in reversed(range(nl)):
        sv = saved[l]
        Wl = W[l]
        dz2, dz2b, dg, db = last_ln if l == nl - 1 else _ln_bwd(f"ln2_bwd_{l}", dcur, sv['z2'], ln2_g[l])
        sm['ln2_g'][l], sm['ln2_b'][l] = dg[0], db[0]
        tn1 = min(1024, dff)

        def dhp_finish(acc, ex, outs, ids):
            outs[0][...] = (acc * (2.0 * jnp.maximum(ex[0][...].astype(F32), 0.0))).astype(BF16)

        dhp = _mm(f"d_hidden_{l}", dz2b, Wl['ff2'], grid=(s // tm, dff // tn1, d // tkk),
                  a_spec=pl.BlockSpec((tm, tkk), lambda m, n, k: (m, k)),
                  b_spec=pl.BlockSpec((None, tn1, tkk), lambda m, n, k: (0, n, k)),
                  dims=NT, acc_shape=(tm, tn1), out_shape=[jax.ShapeDtypeStruct((s, dff), BF16)],
                  out_specs=[pl.BlockSpec((tm, tn1), lambda m, n, k: (m, n))], finish=dhp_finish,
                  extras=(sv['hp'],), extra_specs=(pl.BlockSpec((tm, tn1), lambda m, n, k: (m, n)),),
                  side=pending)[0]
        if pending is not None:
            reduction_done(pending)
            pending = None
        grad_mm('w_ff2', f"g_w_ff2_{l}", sv['hid'], dz2b, shape=(N_DEV, 1, fs, d), grid=(N_DEV, 1, s // tks),
                a_spec=pl.BlockSpec((tks, fs), lambda m, n, k: (k, m)),
                b_spec=pl.BlockSpec((tks, d), lambda m, n, k: (k, 0)),
                out_spec=pl.BlockSpec((None, None, fs, d), lambda m, n, k: (m, 0, 0, 0)), acc_shape=(fs, d))
        grad_mm('w_ff1', f"g_w_ff1_{l}", sv['x1b'], dhp, shape=(N_DEV, 1, d, fs), grid=(d // tmr, N_DEV, s // tks),
                a_spec=pl.BlockSpec((tks, tmr), lambda m, n, k: (k, m)),
                b_spec=pl.BlockSpec((tks, fs), lambda m, n, k: (k, n)),
                out_spec=pl.BlockSpec((None, None, tmr, fs), lambda m, n, k: (n, 0, m, 0)), acc_shape=(tmr, fs))
        tnd = min(1024, d)

        def resid_finish(scale):
            def finish(acc, ex, outs, ids):
                outs[0][...] = acc + scale * ex[0][...]
            return finish

        tile_md = pl.BlockSpec((tm, tnd), lambda m, n, k: (m, n))
        dx1 = _mm(f"d_x1_{l}", dhp, Wl['ff1'], grid=(s // tm, d // tnd, dff // tkk),
                  a_spec=pl.BlockSpec((tm, tkk), lambda m, n, k: (m, k)),
                  b_spec=pl.BlockSpec((None, tnd, tkk), lambda m, n, k: (0, n, k)),
                  dims=NT, acc_shape=(tm, tnd), out_shape=[jax.ShapeDtypeStruct((s, d), F32)],
                  out_specs=[tile_md], finish=resid_finish(ALPHA), extras=(dz2,), extra_specs=(tile_md,))[0]

        dz1, dz1b, dg, db = _ln_bwd(f"ln1_bwd_{l}", dx1, sv['z1'], ln1_g[l])
        sm['ln1_g'][l], sm['ln1_b'][l] = dg[0], db[0]
        tmg, tng = min(512, s), min(512, d)

        def gate_finish(acc, ex, outs, ids):
            @pl.when(ids[1] == 0)
            def _():
                outs[2][...] = jnp.zeros(outs[2].shape, F32)

            ones = jnp.ones((8, tmg), BF16)
            for g in range(4):
                dpre = (acc * ex[1][g].astype(F32)).astype(BF16)
                outs[0][g] = (acc * ex[0][g].astype(F32)).astype(BF16)
                outs[1][g] = dpre
                outs[2][g] += jnp.dot(ones, dpre, preferred_element_type=F32)

        quad = pl.BlockSpec((4, tmg, tng), lambda n, m, k: (0, m, n))
        dproj, dpre, dbg = _mm(
            f"d_merged_{l}", dz1b, Wl['out'], grid=(d // tng, s // tmg, d // tkk),
            a_spec=pl.BlockSpec((tmg, tkk), lambda n, m, k: (m, k)),
            b_spec=pl.BlockSpec((None, tng, tkk), lambda n, m, k: (0, n, k)),
            dims=NT, acc_shape=(tmg, tng),
            out_shape=[jax.ShapeDtypeStruct((4, s, d), BF16), jax.ShapeDtypeStruct((4, s, d), BF16),
                       jax.ShapeDtypeStruct((4, 8, d), F32)],
            out_specs=[quad, quad, pl.BlockSpec((4, 8, tng), lambda n, m, k: (0, 0, n))], finish=gate_finish,
            extras=(sv['gates'], sv['projs']), extra_specs=(quad, quad), sem=("arbitrary", "arbitrary", "arbitrary"))
        sm['b_gate'][l] = dbg[:, 0, :]
        grad_mm('w_out', f"g_w_out_{l}", sv['merged'], dz1b, shape=(N_DEV, 1, rs, d), grid=(d // tmr, 1, s // tks),
                a_spec=pl.BlockSpec((tks, tmr), lambda m, n, k: (k, m)),
                b_spec=pl.BlockSpec((tks, d), lambda m, n, k: (k, 0)),
                out_spec=pl.BlockSpec((nsh, None, rs, d), lambda m, n, k: (m, 0, 0, 0)), acc_shape=(tmr, d))

        nm = s // tm
        do_all = _mm(f"d_branch_{l}", dproj, Wl['branch'], grid=(4 * nm, 1, d // tkk),
                     a_spec=pl.BlockSpec((None, tm, tkk), lambda m, n, k: (m // nm, m % nm, k)),
                     b_spec=pl.BlockSpec((None, None, bw, tkk), lambda m, n, k: (0, m // nm, 0, k)),
                     dims=NT, acc_shape=(tm, bw), out_shape=[jax.ShapeDtypeStruct((4, s, bw), BF16)],
                     out_specs=[pl.BlockSpec((None, tm, bw), lambda m, n, k: (m // nm, m % nm, 0))],
                     finish=_store(BF16))[0]
        grad_mm('w_branch', f"g_w_branch_{l}", sv['o_all'], dproj, shape=(N_DEV, 1, 4, bw, cs),
                grid=(4, 1, s // tks), finish=split_columns,
                a_spec=pl.BlockSpec((None, tks, bw), lambda m, n, k: (m, k, 0)),
                b_spec=pl.BlockSpec((None, tks, d), lambda m, n, k: (m, k, 0)),
                out_spec=pl.BlockSpec((N_DEV, None, None, bw, cs), lambda m, n, k: (0, 0, m, 0, 0)),
                acc_shape=(bw, d))
        grad_mm('w_gate', f"g_w_gate_{l}", sv['xinb'], dpre, shape=(N_DEV, 1, 4, rs, d), grid=(d // tmr, 4, s // tks),
                a_spec=pl.BlockSpec((tks, tmr), lambda m, n, k: (k, m)),
                b_spec=pl.BlockSpec((None, tks, d), lambda m, n, k: (n, k, 0)),
                out_spec=pl.BlockSpec((nsh, None, None, rs, d), lambda m, n, k: (m, 0, n, 0, 0)),
                acc_shape=(tmr, d))
        nkg = d // tkk
        cores1 = reduce_cores(l, GROUP1)
        dx_gate = _mm(f"d_x_gates_{l}", dpre, Wl['gate'], grid=(s // tm, d // tnd, 4 * nkg),
                      a_spec=pl.BlockSpec((None, tm, tkk), lambda m, n, k: (k // nkg, m, k % nkg)),
                      b_spec=pl.BlockSpec((None, None, tnd, tkk), lambda m, n, k: (0, k // nkg, n, k % nkg)),
                      dims=NT, acc_shape=(tm, tnd), out_shape=[jax.ShapeDtypeStruct((s, d), F32)],
                      out_specs=[tile_md], finish=resid_finish(ALPHA), extras=(dz1,), extra_specs=(tile_md,),
                      side=cores1)[0]

        u = sv['u']
        chips1 = reduce_chips(cores1)
        dfq, dfk, dfv, dcc, dcr = _fox_bwd(f"fox_bwd_{l}", u, sv['cf_col'], sv['cf_row'], sv['o_fox'], do_all[0],
                                          sv['lse'], bw, side=chips1)
        reduction_done(chips1)
        dcf = (dcc[:, :, 0, :].reshape(nh, s) + dcr.reshape(nh, s)).T
        dflb, dbf = _cum_forget_bwd(f"cum_forget_bwd_{l}", jnp.pad(dcf, ((0, 0), (0, LANE - nh))), sv['fl'],
                                    sv['bf_row'])
        sm['b_forget'][l] = dbf[0, :nh]
        drx, dry, dcw, dcb, dwr, dbr, dwi, dbi, dlam = _lru_bwd(f"lru_bwd_{l}", u, sv['hstate'], do_all[1], sv['lp'], bw)
        sm['conv_w'][l], sm['conv_b'][l], sm['w_r'][l], sm['b_r'][l] = dcw[:CONV_WIDTH], dcb[0], dwr, dbr[0]
        sm['w_i'][l], sm['b_i'][l], sm['lru_lambda'][l] = dwi, dbi[0], dlam[0]
        dsq, dsk, dsv = _sb_bwd(f"sb_bwd_{l}", u, do_all[2], bw)
        dcq, dck, dcv, dss = _chunk_bwd(f"chunk_bwd_{l}", u, sv['kpad'], sv['vpad'], sv['bias'], do_all[3], bw)
        sm['rel_bias'][l] = _bias_reduce(f"bias_reduce_{l}", dss, nh)[:, 0, :REL_TABLE]
        du = jnp.concatenate([dfq, dfk, dfv, drx, dry, dsq, dsk, dsv, dcq, dck, dcv, dflb], axis=1)

        tnu = _lane_tile(nue)
        grad_mm('w_main', f"g_w_in_{l}", sv['xinb'], du, shape=(N_DEV, 1, rs, nue),
                grid=(d // tmr, nue // tnu, s // tks),
                a_spec=pl.BlockSpec((tks, tmr), lambda m, n, k: (k, m)),
                b_spec=pl.BlockSpec((tks, tnu), lambda m, n, k: (k, n)),
                out_spec=pl.BlockSpec((nsh, None, rs, tnu), lambda m, n, k: (m, 0, 0, n)), acc_shape=(tmr, tnu))
        cores2 = reduce_cores(l, GROUP2)
        if l == 0:
            _run_side("reduce_cores_last", cores2)
            cores2 = reduce_chips(cores2)
        dcur = _mm(f"d_x_in_{l}", du, Wl['main'], grid=(s // tm, d // tnd, nue // tnu),
                   a_spec=pl.BlockSpec((tm, tnu), lambda m, n, k: (m, k)),
                   b_spec=pl.BlockSpec((None, tnd, tnu), lambda m, n, k: (0, n, k)),
                   dims=NT, acc_shape=(tm, tnd), out_shape=[jax.ShapeDtypeStruct((s, d), F32)],
                   out_specs=[tile_md], finish=resid_finish(1.0), extras=(dx_gate,), extra_specs=(tile_md,),
                   side=cores2)[0]
        if l == 0:
            reduction_done(cores2)
        else:
            pending = reduce_chips(cores2)

    grad_x, _, dg_in, db_in = _ln_bwd("ln_in_bwd", dcur, xs, ln_in_g)

    small_names = ['ln_in_g', 'ln_in_b', 'b_forget', 'conv_w', 'conv_b', 'w_r', 'b_r', 'w_i', 'b_i', 'lru_lambda',
                   'rel_bias', 'b_gate', 'ln1_g', 'ln1_b', 'ln2_g', 'ln2_b']
    local = {'ln_in_g': dg_in[0], 'ln_in_b': db_in[0]}
    for n in small_names[2:]:
        local[n] = jnp.stack(sm[n])
    full_shapes = [local[n].shape for n in small_names]
    packed = _pack([local[n] for n in small_names])
    every = _run_side("gather_small_cores",
                      _cores_side(_run_side("gather_small_chips", _chips_side([packed], True)), True))[0]
    every = jnp.swapaxes(every, 0, 1).reshape((N_DEV,) + packed.shape)
    total = dict(zip(small_names, _unpack(_sum_parts("sum_small", every), full_shapes)))
    for n, width in (('conv_w', bw // N_DEV), ('rel_bias', REL_TABLE // N_DEV), ('b_gate', cs)):
        total[n] = lax.dynamic_slice_in_dim(total[n], dev * width, width, axis=2)

    out = {}

    def update(n, parts):
        cols = parts[0].shape[2]
        w2, m2, v2 = (given[p + n].reshape(-1, cols) for p in ('', 'm_', 'v_'))
        res = None
        for l in range(nl):
            res = _adamw(f"adamw_{n}_{l}", w2, m2, v2, parts[l], layer=l, layers=nl, earlier=res)
        out[n] = [r.reshape(given[n].shape) for r in res]

    def parts_w_in(l):
        pm = reduced[l]['w_main']
        return jnp.concatenate([pm[..., :3 * bw], pm[..., nu:nu + nh], pm[..., 3 * bw:nu]], axis=-1)

    update('w_in', [parts_w_in(l) for l in range(nl)])
    for n in ('w_branch', 'w_gate', 'w_out', 'w_ff1', 'w_ff2'):
        update(n, [reduced[l][n] for l in range(nl)])

    small_shapes2 = [given[n].shape for n in small_names]
    res = _adamw("adamw_small", _pack([given[n] for n in small_names]), _pack([given['m_' + n] for n in small_names]),
                 _pack([given['v_' + n] for n in small_names]), _pack([total[n] for n in small_names])[None])
    res = [_unpack(r, small_shapes2) for r in res]
    for j, n in enumerate(small_names):
        out[n] = [res[k][j] for k in range(4)]

    return (loss, grad_x[None], *[out[n][0] for n in WEIGHTS], *[out[n][1] for n in WEIGHTS],
            *[out[n][2] for n in WEIGHTS], *[out[n][3] for n in WEIGHTS])
```

```python
import functools
import math

import jax
import jax.numpy as jnp
from jax import lax
from jax.experimental import pallas as pl
from jax.experimental.pallas import tpu as pltpu

F32, BF16, I32 = jnp.float32, jnp.bfloat16, jnp.int32
MESH = pl.DeviceIdType.MESH
ANY = pl.BlockSpec(memory_space=pl.ANY)

LANE = 128
VMEM_LIMIT = 56 * 1024 * 1024
N_DEV = 8

HEAD = 128
CHUNK = 64
LOOKBACK = 8
BAND = (LOOKBACK + 1) * CHUNK
QCHUNKS = 4
QBLK = QCHUNKS * CHUNK
WIN = BAND + (QCHUNKS - 1) * CHUNK
PADK = LOOKBACK * CHUNK
REL_CLIP = 256
REL_TABLE = REL_CLIP + CHUNK
REL_PAD = 384
CONV_WIDTH = 4
LRU_C = 8.0
LN_EPS = 1e-5
DEPTH = 2
ALPHA = (2.0 * DEPTH) ** 0.25
NEG = -1e30
SB_DEAD = -104.0
GELU_K = math.sqrt(2.0 / math.pi)
GELU_C = 0.044715

ADAM_LR, ADAM_B1, ADAM_B2, ADAM_EPS, ADAM_WD, ADAM_STEP = 0.001, 0.9, 0.999, 1e-08, 0.01, 10

NN = (((1,), (0,)), ((), ()))
NT = (((1,), (1,)), ((), ()))
TN = (((0,), (0,)), ((), ()))

FQ, FK, FV, RX, RY, SQ, SK, SV, CQ, CK, CV = range(11)

NAMES = ['x', 'ln_in_g', 'ln_in_b', 'w_in', 'b_forget', 'conv_w', 'conv_b', 'w_r', 'b_r', 'w_i', 'b_i', 'lru_lambda',
         'rel_bias', 'w_branch', 'w_gate', 'b_gate', 'w_out', 'ln1_g', 'ln1_b', 'w_ff1', 'w_ff2', 'ln2_g', 'ln2_b']
WEIGHTS = NAMES[1:]


def _cp(sem=None):
    return pltpu.CompilerParams(dimension_semantics=sem, vmem_limit_bytes=VMEM_LIMIT)


def _iota(shape, dim):
    return lax.broadcasted_iota(I32, shape, dim)


def _sigmoid(x):
    return 1.0 / (1.0 + jnp.exp(-x))


def _log_sigmoid(x):
    return jnp.minimum(x, 0.0) - jnp.log(1.0 + jnp.exp(-jnp.abs(x)))


def _lane_tile(n, cap=1536):
    best = max(t for t in range(LANE, cap + 1, LANE) if n % t == 0)
    return n if best == LANE else best


def _pow2_rows(rows, cols, elems=524288):
    t = 8
    while t * 2 <= rows and t * 2 * cols <= elems and rows % (t * 2) == 0:
        t *= 2
    return t


def _position():
    return lax.axis_index("x"), lax.axis_index("y"), lax.axis_index("c")


class _Side:
    def __init__(self, operands, out_shape, sems, start, finish, aliases=(), parts=()):
        self.operands, self.out_shape, self.sems = list(operands), list(out_shape), list(sems)
        self.start, self.finish, self.aliases, self.parts = start, finish, list(aliases), parts
        self.results = None

    def set_results(self, res):
        self.results = list(res)
        off = 0
        for part in self.parts:
            part.set_results(res[off:off + len(part.out_shape)])
            off += len(part.out_shape)


def _merge_sides(a, b):
    ai, ao, asm = len(a.operands), len(a.out_shape), len(a.sems)

    def start(ins, outs, sems):
        a.start(ins[:ai], outs[:ao], sems[:asm])
        b.start(ins[ai:], outs[ao:], sems[asm:])

    def finish(ins, outs, sems):
        a.finish(ins[:ai], outs[:ao], sems[:asm])
        b.finish(ins[ai:], outs[ao:], sems[asm:])

    return _Side(a.operands + b.operands, a.out_shape + b.out_shape, a.sems + b.sems, start, finish,
                 a.aliases + [(i + ai, o + ao) for i, o in b.aliases], parts=(a, b))


def _chips_side(xs, gather):
    n = len(xs)

    def copies(ins, outs, sems, arrivals):
        send_sems, recv_sems, local_sems = sems
        x, y, c = _position()
        q = 2 * x + y
        chips = [(1 - x, y), (x, 1 - y), (1 - x, 1 - y)]

        def src(t, slot):
            return ins[t] if gather else ins[t].at[slot]

        def dst(t, slot):
            return outs[t].at[c, slot] if gather else outs[t].at[slot]

        def remote(t, j, landing):
            px, py = chips[j]
            return pltpu.make_async_remote_copy(
                src_ref=src(t, 2 * px + py), dst_ref=dst(t, landing), send_sem=send_sems.at[t, j],
                recv_sem=recv_sems.at[t, j], device_id=(px, py, c), device_id_type=MESH)

        local = [pltpu.make_async_copy(src(t, q), dst(t, q), local_sems.at[t]) for t in range(n)]
        sends = [remote(t, j, q) for t in range(n) for j in range(3)]
        if not arrivals:
            return local, sends, []
        return local, sends, [remote(t, j, 2 * px + py) for t in range(n) for j, (px, py) in enumerate(chips)]

    def start(ins, outs, sems):
        local, sends, _ = copies(ins, outs, sems, False)
        for cp in local + sends:
            cp.start()

    def finish(ins, outs, sems):
        local, sends, recvs = copies(ins, outs, sems, True)
        for cp in recvs:
            cp.wait_recv()
        for cp in sends:
            cp.wait_send()
        for cp in local:
            cp.wait()

    out_shape = [jax.ShapeDtypeStruct((2, 4) + a.shape if gather else a.shape, a.dtype) for a in xs]
    sems = [pltpu.SemaphoreType.DMA((n, 3)), pltpu.SemaphoreType.DMA((n, 3)), pltpu.SemaphoreType.DMA((n,))]
    return _Side(xs, out_shape, sems, start, finish)


def _cores_side(xs, gather):
    n = len(xs)
    m = 1 if gather else 4

    def copies(ins, outs, sems, arrivals):
        send_sems, recv_sems = sems
        x, y, c = _position()

        def remote(t, j, landing):
            s = outs[t].at[c] if gather else ins[t].at[j, 1 - c]
            d = outs[t].at[landing] if gather else outs[t].at[j]
            return pltpu.make_async_remote_copy(
                src_ref=s, dst_ref=d, send_sem=send_sems.at[t, j], recv_sem=recv_sems.at[t, j],
                device_id=(x, y, 1 - c), device_id_type=MESH)

        sends = [remote(t, j, c) for t in range(n) for j in range(m)]
        return sends, [remote(t, j, 1 - c) for t in range(n) for j in range(m)] if arrivals else []

    def start(ins, outs, sems):
        for cp in copies(ins, outs, sems, False)[0]:
            cp.start()

    def finish(ins, outs, sems):
        sends, recvs = copies(ins, outs, sems, True)
        for cp in recvs:
            cp.wait_recv()
        for cp in sends:
            cp.wait_send()

    if gather:
        out_shape = [jax.ShapeDtypeStruct(a.shape, a.dtype) for a in xs]
    else:
        out_shape = [jax.ShapeDtypeStruct((4,) + a.shape[2:], a.dtype) for a in xs]
    sems = [pltpu.SemaphoreType.DMA((n, m)), pltpu.SemaphoreType.DMA((n, m))]
    return _Side(xs, out_shape, sems, start, finish, aliases=[(t, t) for t in range(n)] if gather else [])


def _run_side(name, side):
    ni, no = len(side.operands), len(side.out_shape)

    def body(*refs):
        ins, outs, sems = refs[:ni], refs[ni:ni + no], refs[ni + no:]
        side.start(ins, outs, sems)
        side.finish(ins, outs, sems)

    side.set_results(pl.pallas_call(
        body, name=name, out_shape=side.out_shape, in_specs=[ANY] * ni, out_specs=[ANY] * no,
        input_output_aliases=dict(side.aliases), scratch_shapes=side.sems)(*side.operands))
    return side.results


def _pcall(body, operands, *, name, grid, in_specs, out_specs, out_shape, scratch_shapes=(), sem=None, aliases=None,
           side=None):
    if side is None:
        return pl.pallas_call(body, name=name, grid=grid, in_specs=list(in_specs), out_specs=list(out_specs),
                              out_shape=list(out_shape), scratch_shapes=list(scratch_shapes),
                              input_output_aliases=aliases or {}, compiler_params=_cp(sem))(*operands)
    ni, no, ns = len(in_specs), len(out_shape), len(scratch_shapes)
    si, so = len(side.operands), len(side.out_shape)

    def carrying(*refs):
        ins, sins = refs[:ni], refs[ni:ni + si]
        outs, souts = refs[ni + si:ni + si + no], refs[ni + si + no:ni + si + no + so]
        scratch, ssems = refs[ni + si + no + so:ni + si + no + so + ns], refs[ni + si + no + so + ns:]
        ids = [pl.program_id(a) for a in range(len(grid))]
        first = functools.reduce(jnp.logical_and, [i == 0 for i in ids])
        last = functools.reduce(jnp.logical_and, [i == g - 1 for i, g in zip(ids, grid)])

        @pl.when(first)
        def _():
            side.start(sins, souts, ssems)

        body(*ins, *outs, *scratch)

        @pl.when(last)
        def _():
            side.finish(sins, souts, ssems)

    joined = dict(aliases or {})
    joined.update({ni + i: no + o for i, o in side.aliases})
    res = pl.pallas_call(
        carrying, name=name, grid=grid, in_specs=[*in_specs, *[ANY] * si], out_specs=[*out_specs, *[ANY] * so],
        out_shape=[*out_shape, *side.out_shape], scratch_shapes=[*scratch_shapes, *side.sems],
        input_output_aliases=joined, compiler_params=_cp(("arbitrary",) * len(grid)))(*operands, *side.operands)
    side.set_results(res[no:])
    return res[:no]


def _add_core_halves(name, mine, other, c, out_dtype):
    _, _, rows, cols = mine.shape
    tr = _pow2_rows(rows, cols)

    def body(c_ref, a_ref, b_ref, o_ref):
        o_ref[...] = (a_ref[...].astype(F32) + b_ref[...].astype(F32)).astype(out_dtype)

    grid_spec = pltpu.PrefetchScalarGridSpec(
        num_scalar_prefetch=1, grid=(4, rows // tr),
        in_specs=[pl.BlockSpec((None, None, tr, cols), lambda j, i, c_ref: (j, c_ref[0], i, 0)),
                  pl.BlockSpec((None, tr, cols), lambda j, i, c_ref: (j, i, 0))],
        out_specs=pl.BlockSpec((None, tr, cols), lambda j, i, c_ref: (j, i, 0)))
    return pl.pallas_call(body, name=name, grid_spec=grid_spec,
                          out_shape=jax.ShapeDtypeStruct((4, rows, cols), out_dtype),
                          compiler_params=_cp(("parallel", "parallel")))(c, mine, other)


def _sum_parts(name, parts):
    p, rows, cols = parts.shape
    tr = _pow2_rows(rows, cols * p)

    def body(a_ref, o_ref):
        acc = a_ref[0]
        for k in range(1, p):
            acc = acc + a_ref[k]
        o_ref[...] = acc

    return pl.pallas_call(body, name=name, grid=(rows // tr,),
                          in_specs=[pl.BlockSpec((p, tr, cols), lambda i: (0, i, 0))],
                          out_specs=pl.BlockSpec((tr, cols), lambda i: (i, 0)),
                          out_shape=jax.ShapeDtypeStruct((rows, cols), F32),
                          compiler_params=_cp(("parallel",)))(parts)


def _mm(name, a, b, *, grid, a_spec, b_spec, dims, acc_shape, out_shape, out_specs, finish,
        extras=(), extra_specs=(), aliases=None, sem=("parallel", "parallel", "arbitrary"), side=None):
    nk, ne, no = grid[2], len(extras), len(out_shape)

    def body(*refs):
        a_ref, b_ref = refs[0], refs[1]
        ex, outs = refs[2:2 + ne], refs[2 + ne:2 + ne + no]
        ids = (pl.program_id(0), pl.program_id(1))
        def prod():
            return lax.dot_general(a_ref[...], b_ref[...], dims, preferred_element_type=F32)

        if nk == 1:
            finish(prod(), ex, outs, ids)
            return
        acc = refs[2 + ne + no]
        k = pl.program_id(2)

        @pl.when(k == 0)
        def _():
            acc[...] = prod()

        @pl.when(jnp.logical_and(k > 0, k < nk - 1))
        def _():
            acc[...] += prod()

        @pl.when(k == nk - 1)
        def _():
            finish(acc[...] + prod(), ex, outs, ids)

    return _pcall(body, (a, b, *extras), name=name, grid=grid, in_specs=[a_spec, b_spec, *extra_specs],
                  out_specs=out_specs, out_shape=out_shape,
                  scratch_shapes=[pltpu.VMEM(acc_shape, F32)] if nk > 1 else [], sem=sem, aliases=aliases, side=side)


def _store(dtype):
    def finish(acc, ex, outs, ids):
        outs[0][...] = acc.reshape(outs[0].shape).astype(dtype)
    return finish


def _layer_norm_rows(z, g, b):
    mu = jnp.mean(z, axis=1, keepdims=True)
    zc = z - mu
    var = jnp.mean(zc * zc, axis=1, keepdims=True)
    return zc * lax.rsqrt(var + LN_EPS) * g + b


def _mm_ln(name, a, w, l, resid, g, b, side=None):
    s, kdim = a.shape
    d = w.shape[2]
    tm, tk = min(512, s), min(1024, kdim)

    def finish(acc, ex, outs, ids):
        z = acc + ALPHA * ex[0][...]
        y = _layer_norm_rows(z, ex[1][...], ex[2][...])
        outs[0][...] = z
        outs[1][...] = y
        outs[2][...] = y.astype(BF16)

    row = pl.BlockSpec((tm, d), lambda m, n, k: (m, 0))
    vec = pl.BlockSpec((1, d), lambda m, n, k: (0, 0))
    return _mm(name, a, w, grid=(s // tm, 1, kdim // tk),
               a_spec=pl.BlockSpec((tm, tk), lambda m, n, k: (m, k)),
               b_spec=pl.BlockSpec((None, tk, d), lambda m, n, k: (l, k, 0)),
               dims=NN, acc_shape=(tm, d),
               out_shape=[jax.ShapeDtypeStruct((s, d), F32), jax.ShapeDtypeStruct((s, d), F32),
                          jax.ShapeDtypeStruct((s, d), BF16)],
               out_specs=[row, row, row], finish=finish,
               extras=(resid, g.reshape(1, d), b.reshape(1, d)), extra_specs=(row, vec, vec), side=side)


def _ln_fwd(name, x, g, b, side=None):
    s, d = x.shape
    tr = min(256, s)

    def body(x_ref, g_ref, b_ref, y_ref, yb_ref):
        y = _layer_norm_rows(x_ref[...], g_ref[...], b_ref[...])
        y_ref[...] = y
        yb_ref[...] = y.astype(BF16)

    row = pl.BlockSpec((tr, d), lambda i: (i, 0))
    vec = pl.BlockSpec((1, d), lambda i: (0, 0))
    return _pcall(body, (x, g.reshape(1, d), b.reshape(1, d)), name=name, grid=(s // tr,), in_specs=[row, vec, vec],
                  out_specs=[row, row],
                  out_shape=[jax.ShapeDtypeStruct((s, d), F32), jax.ShapeDtypeStruct((s, d), BF16)],
                  sem=("parallel",), side=side)


def _ln_bwd_rows(dyv, zz, g, dz_ref, dzb_ref, dg_ref, db_ref):
    mu = jnp.mean(zz, axis=1, keepdims=True)
    zc = zz - mu
    rstd = lax.rsqrt(jnp.mean(zc * zc, axis=1, keepdims=True) + LN_EPS)
    xhat = zc * rstd
    dg_ref[...] += jnp.sum(dyv * xhat, axis=0, keepdims=True)
    db_ref[...] += jnp.sum(dyv, axis=0, keepdims=True)
    dxh = dyv * g
    dz = rstd * (dxh - jnp.mean(dxh, axis=1, keepdims=True) - xhat * jnp.mean(dxh * xhat, axis=1, keepdims=True))
    dz_ref[...] = dz
    dzb_ref[...] = dz.astype(BF16)


def _ln_bwd(name, dy, z, g):
    s, d = z.shape
    tr = min(256, s)

    def body(dy_ref, z_ref, g_ref, dz_ref, dzb_ref, dg_ref, db_ref):
        @pl.when(pl.program_id(0) == 0)
        def _():
            dg_ref[...] = jnp.zeros(dg_ref.shape, F32)
            db_ref[...] = jnp.zeros(db_ref.shape, F32)

        _ln_bwd_rows(dy_ref[...], z_ref[...], g_ref[...], dz_ref, dzb_ref, dg_ref, db_ref)

    row = pl.BlockSpec((tr, d), lambda i: (i, 0))
    vec = pl.BlockSpec((1, d), lambda i: (0, 0))
    return _pcall(
        body, (dy, z, g.reshape(1, d)), name=name, grid=(s // tr,), in_specs=[row, row, vec],
        out_specs=[row, row, vec, vec],
        out_shape=[jax.ShapeDtypeStruct((s, d), F32), jax.ShapeDtypeStruct((s, d), BF16),
                   jax.ShapeDtypeStruct((1, d), F32), jax.ShapeDtypeStruct((1, d), F32)],
        sem=("arbitrary",))


def _loss_ln_bwd(name, y, target, z, g):
    s, d = y.shape
    tr = min(256, s)

    def body(y_ref, t_ref, z_ref, g_ref, loss_ref, dz_ref, dzb_ref, dg_ref, db_ref):
        @pl.when(pl.program_id(0) == 0)
        def _():
            loss_ref[...] = jnp.zeros(loss_ref.shape, F32)
            dg_ref[...] = jnp.zeros(dg_ref.shape, F32)
            db_ref[...] = jnp.zeros(db_ref.shape, F32)

        e = y_ref[...] - t_ref[...]
        loss_ref[...] += jnp.sum(e * e) * (0.5 / d)
        _ln_bwd_rows(e * (1.0 / d), z_ref[...], g_ref[...], dz_ref, dzb_ref, dg_ref, db_ref)

    row = pl.BlockSpec((tr, d), lambda i: (i, 0))
    vec = pl.BlockSpec((1, d), lambda i: (0, 0))
    return pl.pallas_call(
        body, name=name, grid=(s // tr,), in_specs=[row, row, row, vec],
        out_specs=[pl.BlockSpec((8, LANE), lambda i: (0, 0)), row, row, vec, vec],
        out_shape=[jax.ShapeDtypeStruct((8, LANE), F32), jax.ShapeDtypeStruct((s, d), F32),
                   jax.ShapeDtypeStruct((s, d), BF16), jax.ShapeDtypeStruct((1, d), F32),
                   jax.ShapeDtypeStruct((1, d), F32)],
        compiler_params=_cp(("arbitrary",)))(y, target, z, g.reshape(1, d))


def _scan_add(x, reverse):
    ts = x.shape[0]
    rows = _iota((ts, 1), 0)
    dist = 1
    while dist < ts:
        if reverse:
            x = x + jnp.where(rows < ts - dist, pltpu.roll(x, ts - dist, 0), 0.0)
        else:
            x = x + jnp.where(rows >= dist, pltpu.roll(x, dist, 0), 0.0)
        dist *= 2
    return x


def _scan_affine(a, b, reverse):
    ts = a.shape[0]
    rows = _iota((ts, 1), 0)
    dist = 1
    while dist < ts:
        shift = ts - dist if reverse else dist
        valid = rows < ts - dist if reverse else rows >= dist
        b = b + a * jnp.where(valid, pltpu.roll(b, shift, 0), 0.0)
        a = a * jnp.where(valid, pltpu.roll(a, shift, 0), 1.0)
        dist *= 2
    return a, b


def _cum_forget_fwd(name, fl, bias):
    s = fl.shape[0]
    ts = min(1024, s)

    def body(f_ref, b_ref, o_ref, carry):
        @pl.when(pl.program_id(0) == 0)
        def _():
            carry[...] = jnp.zeros(carry.shape, F32)

        o_ref[...] = _scan_add(_log_sigmoid(f_ref[...] + b_ref[...]), False) + carry[...]
        carry[...] = o_ref[pl.ds(ts - 1, 1), :]

    row = pl.BlockSpec((ts, LANE), lambda i: (i, 0))
    return pl.pallas_call(body, name=name, grid=(s // ts,),
                          in_specs=[row, pl.BlockSpec((1, LANE), lambda i: (0, 0))], out_specs=row,
                          out_shape=jax.ShapeDtypeStruct((s, LANE), F32),
                          scratch_shapes=[pltpu.VMEM((1, LANE), F32)],
                          compiler_params=_cp(("arbitrary",)))(fl, bias)


def _cum_forget_bwd(name, dcf, fl, bias):
    s = fl.shape[0]
    ts = min(1024, s)
    nb = s // ts

    def body(d_ref, f_ref, b_ref, o_ref, db_ref, carry):
        @pl.when(pl.program_id(0) == 0)
        def _():
            carry[...] = jnp.zeros(carry.shape, F32)
            db_ref[...] = jnp.zeros(db_ref.shape, F32)

        run = _scan_add(d_ref[...], True) + carry[...]
        carry[...] = jnp.sum(jnp.where(_iota((ts, 1), 0) == 0, run, 0.0), axis=0, keepdims=True)
        dfl = run * _sigmoid(-(f_ref[...] + b_ref[...]))
        o_ref[...] = dfl.astype(BF16)
        db_ref[...] += jnp.sum(dfl, axis=0, keepdims=True)

    row = pl.BlockSpec((ts, LANE), lambda i: (nb - 1 - i, 0))
    vec = pl.BlockSpec((1, LANE), lambda i: (0, 0))
    return pl.pallas_call(body, name=name, grid=(nb,), in_specs=[row, row, vec], out_specs=[row, vec],
                          out_shape=[jax.ShapeDtypeStruct((s, LANE), BF16), jax.ShapeDtypeStruct((1, LANE), F32)],
                          scratch_shapes=[pltpu.VMEM((1, LANE), F32)],
                          compiler_params=_cp(("arbitrary",)))(dcf, fl, bias)


def _fox_specs(s, nh, tq, tk):
    q = pl.BlockSpec((tq, HEAD), lambda h, i: (i, FQ * nh + h))
    k = pl.BlockSpec((s, HEAD), lambda h, i: (0, FK * nh + h))
    v = pl.BlockSpec((s, HEAD), lambda h, i: (0, FV * nh + h))
    col = pl.BlockSpec((None, tq, 1), lambda h, i: (h, i, 0))
    rowv = pl.BlockSpec((None, s // tk, 1, tk), lambda h, i: (h, 0, 0, 0))
    tile = pl.BlockSpec((tq, HEAD), lambda h, i: (i, h))
    full = pl.BlockSpec((s, HEAD), lambda h, i: (0, h))
    return q, k, v, col, rowv, tile, full


def _fox_tile(s):
    return min(512, s), min(1024, s)


def _fox_scores(q, k_ref, cfq, cfr_ref, kb, tk, scale, qpos=None):
    off = pl.multiple_of(kb * tk, tk)
    k = k_ref[pl.ds(off, tk), :]
    sc = lax.dot_general(q, k, NT, preferred_element_type=F32) * scale + cfq - cfr_ref[kb]
    mask = None
    if qpos is not None:
        mask = kb * tk + _iota((1, tk), 1) <= qpos
        sc = jnp.where(mask, sc, NEG)
    return sc, mask, k, off


def _fox_fwd(name, u, cf_col, cf_row, bw, side=None):
    s, nh = u.shape[0], bw // HEAD
    tq, tk = _fox_tile(s)
    scale = HEAD ** -0.5

    def body(q_ref, k_ref, v_ref, cfc_ref, cfr_ref, o_ref, lse_ref):
        i = pl.program_id(1)
        q, cfq = q_ref[...], cfc_ref[...]
        last = ((i + 1) * tq - 1) // tk

        def step(kb, carry, qpos=None):
            m, l, acc = carry
            sc, _, _, off = _fox_scores(q, k_ref, cfq, cfr_ref, kb, tk, scale, qpos)
            m2 = jnp.maximum(m, jnp.max(sc, axis=1, keepdims=True))
            p = jnp.exp(sc - m2)
            al = jnp.exp(m - m2)
            return (m2, al * l + jnp.sum(p, axis=1, keepdims=True),
                    al * acc + jnp.dot(p.astype(BF16), v_ref[pl.ds(off, tk), :], preferred_element_type=F32))

        init = (jnp.full((tq, 1), NEG, F32), jnp.zeros((tq, 1), F32), jnp.zeros((tq, HEAD), F32))
        m, l, acc = step(last, lax.fori_loop(0, last, step, init), i * tq + _iota((tq, 1), 0))
        o_ref[...] = (acc / l).astype(BF16)
        lse_ref[...] = m + jnp.log(l)

    q, k, v, col, rowv, tile, _ = _fox_specs(s, nh, tq, tk)
    return _pcall(body, (u, u, u, cf_col, cf_row), name=name, grid=(nh, s // tq), in_specs=[q, k, v, col, rowv],
                  out_specs=[tile, col],
                  out_shape=[jax.ShapeDtypeStruct((s, bw), BF16), jax.ShapeDtypeStruct((nh, s, 1), F32)],
                  sem=("parallel", "parallel"), side=side)


def _fox_bwd(name, u, cf_col, cf_row, o, do, lse, bw, side=None):
    s, nh = u.shape[0], bw // HEAD
    tq, tk = _fox_tile(s)
    nq = s // tq
    scale = HEAD ** -0.5

    def body(q_ref, k_ref, v_ref, cfc_ref, cfr_ref, o_ref, do_ref, lse_ref,
             dq_ref, dk_ref, dv_ref, dcc_ref, dcr_ref, dk_s, dv_s):
        i = pl.program_id(1)

        @pl.when(i == 0)
        def _():
            dk_s[...] = jnp.zeros(dk_s.shape, F32)
            dv_s[...] = jnp.zeros(dv_s.shape, F32)
            dcr_ref[...] = jnp.zeros(dcr_ref.shape, F32)

        q, dov, cfq, lse_q = q_ref[...], do_ref[...], cfc_ref[...], lse_ref[...]
        delta = jnp.sum(dov.astype(F32) * o_ref[...].astype(F32), axis=1, keepdims=True)
        last = ((i + 1) * tq - 1) // tk

        def step(kb, carry, qpos=None):
            dq, dcq = carry
            sc, mask, k, off = _fox_scores(q, k_ref, cfq, cfr_ref, kb, tk, scale, qpos)
            p = jnp.exp(sc - lse_q)
            if qpos is not None:
                p = jnp.where(mask, p, 0.0)
            dp = lax.dot_general(dov, v_ref[pl.ds(off, tk), :], NT, preferred_element_type=F32)
            ds = p * (dp - delta)
            dsb = ds.astype(BF16)
            dk_s[pl.ds(off, tk), :] += lax.dot_general(dsb, q, TN, preferred_element_type=F32)
            dv_s[pl.ds(off, tk), :] += lax.dot_general(p.astype(BF16), dov, TN, preferred_element_type=F32)
            dcr_ref[kb] += -jnp.sum(ds, axis=0, keepdims=True)
            return (dq + jnp.dot(dsb, k, preferred_element_type=F32), dcq + jnp.sum(ds, axis=1, keepdims=True))

        init = (jnp.zeros((tq, HEAD), F32), jnp.zeros((tq, 1), F32))
        dq, dcq = step(last, lax.fori_loop(0, last, step, init), i * tq + _iota((tq, 1), 0))
        dq_ref[...] = (dq * scale).astype(BF16)
        dcc_ref[...] = jnp.transpose(jnp.broadcast_to(dcq, (tq, LANE)))[:8, :]

        @pl.when(i == nq - 1)
        def _():
            dk_ref[...] = (dk_s[...] * scale).astype(BF16)
            dv_ref[...] = dv_s[...].astype(BF16)

    q, k, v, col, rowv, tile, full = _fox_specs(s, nh, tq, tk)
    by_query = pl.BlockSpec((None, None, 8, tq), lambda h, i: (h, i, 0, 0))
    return _pcall(
        body, (u, u, u, cf_col, cf_row, o, do, lse), name=name, grid=(nh, nq),
        in_specs=[q, k, v, col, rowv, tile, tile, col], out_specs=[tile, full, full, by_query, rowv],
        out_shape=[jax.ShapeDtypeStruct((s, bw), BF16)] * 3
        + [jax.ShapeDtypeStruct((nh, nq, 8, tq), F32), jax.ShapeDtypeStruct((nh, s // tk, 1, tk), F32)],
        scratch_shapes=[pltpu.VMEM((s, HEAD), F32), pltpu.VMEM((s, HEAD), F32)],
        sem=("arbitrary", "arbitrary"), side=side)


def _suffix_mm(x, ones_below):
    hi = x.astype(BF16)
    lo = (x - hi.astype(F32)).astype(BF16)
    return (jnp.dot(hi, ones_below, preferred_element_type=F32) + jnp.dot(lo, ones_below, preferred_element_type=F32))


def _sb_tile(q, k_ref, kb, tk, qpos, scale):
    off = pl.multiple_of(kb * tk, tk)
    k = k_ref[pl.ds(off, tk), :]
    z = lax.dot_general(q, k, NT, preferred_element_type=F32) * scale
    mask = kb * tk + _iota((1, tk), 1) < qpos
    lsn = -jnp.maximum(z, 0.0) - jnp.log(1.0 + jnp.exp(-jnp.abs(z)))
    return z, mask, lsn, jnp.where(mask, lsn, 0.0), k, off


def _sb_specs(s, nh, tq):
    q = pl.BlockSpec((tq, HEAD), lambda h, i: (i, SQ * nh + h))
    k = pl.BlockSpec((s, HEAD), lambda h, i: (0, SK * nh + h))
    v = pl.BlockSpec((s, HEAD), lambda h, i: (0, SV * nh + h))
    tile = pl.BlockSpec((tq, HEAD), lambda h, i: (i, h))
    full = pl.BlockSpec((s, HEAD), lambda h, i: (0, h))
    return q, k, v, tile, full


def _sb_fwd(name, u, bw):
    s, nh = u.shape[0], bw // HEAD
    tq = tk = 256
    scale = HEAD ** -0.5

    def body(q_ref, k_ref, v_ref, o_ref):
        i = pl.program_id(1)
        q = q_ref[...]
        qpos = i * tq + _iota((tq, 1), 0)
        later_keys = (_iota((tk, tk), 0) > _iota((tk, tk), 1)).astype(BF16)
        nk = (i * tq + tq + tk - 2) // tk

        def cond(st):
            return jnp.logical_and(st[0] < nk, st[3] > SB_DEAD)

        def step(st):
            j, c, acc, _ = st
            z, mask, lsn, lm, _, off = _sb_tile(q, k_ref, nk - 1 - j, tk, qpos, scale)
            a = jnp.where(mask, jnp.exp(lsn + z + c + _suffix_mm(lm, later_keys)), 0.0)
            acc = acc + jnp.dot(a.astype(BF16), v_ref[pl.ds(off, tk), :], preferred_element_type=F32)
            c = c + jnp.sum(lm, axis=1, keepdims=True)
            return j + 1, c, acc, jnp.max(c)

        init = (jnp.int32(0), jnp.zeros((tq, 1), F32), jnp.zeros((tq, HEAD), F32), jnp.float32(0.0))
        o_ref[...] = lax.while_loop(cond, step, init)[2].astype(BF16)

    q, k, v, tile, _ = _sb_specs(s, nh, tq)
    return _pcall(body, (u, u, u), name=name, grid=(nh, s // tq), in_specs=[q, k, v], out_specs=[tile],
                  out_shape=[jax.ShapeDtypeStruct((s, bw), BF16)], sem=("parallel", "parallel"))[0]


def _sb_bwd(name, u, do, bw):
    s, nh = u.shape[0], bw // HEAD
    tq = tk = 256
    nq = s // tq
    scale = HEAD ** -0.5

    def body(q_ref, k_ref, v_ref, do_ref, dq_ref, dk_ref, dv_ref, dk_s, dv_s):
        i = pl.program_id(1)

        @pl.when(i == 0)
        def _():
            dk_s[...] = jnp.zeros(dk_s.shape, F32)
            dv_s[...] = jnp.zeros(dv_s.shape, F32)

        q, dov = q_ref[...], do_ref[...]
        qpos = i * tq + _iota((tq, 1), 0)
        later_keys = (_iota((tk, tk), 0) > _iota((tk, tk), 1)).astype(BF16)
        this_and_later = (_iota((tk, tk), 0) >= _iota((tk, tk), 1)).astype(BF16)
        nk = (i * tq + tq + tk - 2) // tk

        def weights(j, c):
            z, mask, lsn, lm, k, off = _sb_tile(q, k_ref, nk - 1 - j, tk, qpos, scale)
            a = jnp.where(mask, jnp.exp(lsn + z + c + _suffix_mm(lm, later_keys)), 0.0)
            w = a * lax.dot_general(dov, v_ref[pl.ds(off, tk), :], NT, preferred_element_type=F32)
            return z, mask, lsn, lm, k, off, a, w

        def cond(st):
            return jnp.logical_and(st[0] < nk, st[3] > SB_DEAD)

        def step1(st):
            j, c, wc, _ = st
            _, _, _, lm, _, _, _, w = weights(j, c)
            c = c + jnp.sum(lm, axis=1, keepdims=True)
            return j + 1, c, wc + jnp.sum(w, axis=1, keepdims=True), jnp.max(c)

        zero = jnp.zeros((tq, 1), F32)
        live, _, total, _ = lax.while_loop(cond, step1, (jnp.int32(0), zero, zero, jnp.float32(0.0)))

        def step2(j, st):
            c, wc, dq = st
            z, mask, lsn, lm, k, off, a, w = weights(j, c)
            earlier = total - (wc + _suffix_mm(w, this_and_later))
            dz = jnp.where(mask, w * jnp.exp(lsn) - jnp.exp(lsn + z) * earlier, 0.0)
            dzb = dz.astype(BF16)
            dk_s[pl.ds(off, tk), :] += lax.dot_general(dzb, q, TN, preferred_element_type=F32)
            dv_s[pl.ds(off, tk), :] += lax.dot_general(a.astype(BF16), dov, TN, preferred_element_type=F32)
            return (c + jnp.sum(lm, axis=1, keepdims=True), wc + jnp.sum(w, axis=1, keepdims=True),
                    dq + jnp.dot(dzb, k, preferred_element_type=F32))

        dq = lax.fori_loop(0, live, step2, (zero, zero, jnp.zeros((tq, HEAD), F32)))[2]
        dq_ref[...] = (dq * scale).astype(BF16)

        @pl.when(i == nq - 1)
        def _():
            dk_ref[...] = (dk_s[...] * scale).astype(BF16)
            dv_ref[...] = dv_s[...].astype(BF16)

    q, k, v, tile, full = _sb_specs(s, nh, tq)
    return pl.pallas_call(
        body, name=name, grid=(nh, nq), in_specs=[q, k, v, tile], out_specs=[tile, full, full],
        out_shape=[jax.ShapeDtypeStruct((s, bw), BF16)] * 3,
        scratch_shapes=[pltpu.VMEM((s, HEAD), F32), pltpu.VMEM((s, HEAD), F32)],
        compiler_params=_cp(("arbitrary", "arbitrary")))(u, u, u, do)


BIAS_W = -(-(WIN + QBLK - 1) // LANE) * LANE


def _strip_onehot():
    col = _iota((1, BIAS_W), 1)
    ridx = jnp.clip(PADK + (QBLK - 1) - col, -(CHUNK - 1), REL_CLIP) + (CHUNK - 1)
    return (_iota((REL_PAD, BIAS_W), 0) == ridx).astype(BF16)


def _split2(x):
    hi = x.astype(BF16)
    return hi, (x - hi.astype(F32)).astype(BF16)


def _bias_expand(name, table, nh):
    def body(t_ref, o_ref, strip):
        table_f32 = t_ref[...]
        hi = table_f32.astype(BF16)
        mid, lo = _split2(table_f32 - hi.astype(F32))
        onehot = _strip_onehot()
        strip[...] = (jnp.dot(hi, onehot, preferred_element_type=F32) + jnp.dot(mid, onehot, preferred_element_type=F32)
                      + jnp.dot(lo, onehot, preferred_element_type=F32))
        row, kl = _iota((QBLK, 1), 0), _iota((1, WIN), 1)
        first = row - jnp.bitwise_and(row, CHUNK - 1)
        valid = jnp.logical_and(kl >= first, kl < first + BAND)
        for h in range(nh):
            rows = jnp.broadcast_to(strip[pl.ds(h, 1), :], (QBLK, BIAS_W))
            rolled = pltpu.roll(rows, BIAS_W - (QBLK - 1), 1, stride=1, stride_axis=0)
            o_ref[h] = jnp.where(valid, rolled[:, :WIN], NEG)

    return pl.pallas_call(body, name=name, out_shape=jax.ShapeDtypeStruct((nh, QBLK, WIN), F32),
                          in_specs=[pl.BlockSpec(memory_space=pltpu.VMEM)],
                          out_specs=pl.BlockSpec(memory_space=pltpu.VMEM),
                          scratch_shapes=[pltpu.VMEM((16, BIAS_W), F32)], compiler_params=_cp())(table)


def _bias_reduce(name, dss, nh):
    def body(x_ref, o_ref):
        onehot = _strip_onehot()
        flip = (_iota((QBLK, QBLK), 0) + _iota((QBLK, QBLK), 1) == QBLK - 1).astype(BF16)
        for h in range(nh):
            x = jnp.concatenate([x_ref[h], jnp.zeros((QBLK, BIAS_W - WIN), F32)], axis=1)
            hi, lo = _split2(x)
            back = jnp.dot(flip, hi, preferred_element_type=F32) + jnp.dot(flip, lo, preferred_element_type=F32)
            lined = pltpu.roll(back, 0, 1, stride=1, stride_axis=0)
            hi, lo = _split2(jnp.broadcast_to(jnp.sum(lined, axis=0, keepdims=True), (8, BIAS_W)))
            o_ref[h] = (lax.dot_general(hi, onehot, NT, preferred_element_type=F32)
                        + lax.dot_general(lo, onehot, NT, preferred_element_type=F32))

    return pl.pallas_call(body, name=name, out_shape=jax.ShapeDtypeStruct((nh, 8, REL_PAD), F32),
                          in_specs=[pl.BlockSpec(memory_space=pltpu.VMEM)],
                          out_specs=pl.BlockSpec(memory_space=pltpu.VMEM), compiler_params=_cp())(dss)


def _chunk_specs(s, nh):
    q = pl.BlockSpec((QBLK, HEAD), lambda h, i: (i, CQ * nh + h))
    kv = pl.BlockSpec((s + PADK, HEAD), lambda h, i: (0, h))
    bias = pl.BlockSpec((None, QBLK, WIN), lambda h, i: (h, 0, 0))
    tile = pl.BlockSpec((QBLK, HEAD), lambda h, i: (i, h))
    full = pl.BlockSpec((s, HEAD), lambda h, i: (0, h))
    return q, kv, bias, tile, full


def _chunk_probs(q, k_ref, b_ref, i, scale):
    off = pl.multiple_of(i * QBLK, QBLK)
    kw = k_ref[pl.ds(off, WIN), :]
    sc = lax.dot_general(q, kw, NT, preferred_element_type=F32) * scale + b_ref[...]
    sc = jnp.where(i * QBLK + _iota((1, WIN), 1) >= PADK, sc, NEG)
    p = jnp.exp(sc - jnp.max(sc, axis=1, keepdims=True))
    return p, jnp.sum(p, axis=1, keepdims=True), kw, off


def _chunk_fwd(name, u, kpad, vpad, bias, bw, side=None):
    s, nh = u.shape[0], bw // HEAD
    scale = HEAD ** -0.5

    def body(q_ref, k_ref, v_ref, b_ref, o_ref):
        p, l, _, off = _chunk_probs(q_ref[...], k_ref, b_ref, pl.program_id(1), scale)
        o = jnp.dot(p.astype(BF16), v_ref[pl.ds(off, WIN), :], preferred_element_type=F32)
        o_ref[...] = (o / l).astype(BF16)

    q, kv, bs, tile, _ = _chunk_specs(s, nh)
    return _pcall(body, (u, kpad, vpad, bias), name=name, grid=(nh, s // QBLK), in_specs=[q, kv, kv, bs],
                  out_specs=[tile], out_shape=[jax.ShapeDtypeStruct((s, bw), BF16)], sem=("parallel", "parallel"),
                  side=side)[0]


def _chunk_bwd(name, u, kpad, vpad, bias, do, bw):
    s, nh = u.shape[0], bw // HEAD
    nq = s // QBLK
    scale = HEAD ** -0.5

    def body(q_ref, k_ref, v_ref, b_ref, do_ref, dq_ref, dk_ref, dv_ref, dss_ref, dk_s, dv_s):
        i = pl.program_id(1)

        @pl.when(i == 0)
        def _():
            dk_s[...] = jnp.zeros(dk_s.shape, F32)
            dv_s[...] = jnp.zeros(dv_s.shape, F32)
            dss_ref[...] = jnp.zeros(dss_ref.shape, F32)

        q, dov = q_ref[...], do_ref[...]
        p, l, kw, off = _chunk_probs(q, k_ref, b_ref, i, scale)
        p = p / l
        dp = lax.dot_general(dov, v_ref[pl.ds(off, WIN), :], NT, preferred_element_type=F32)
        ds = p * (dp - jnp.sum(p * dp, axis=1, keepdims=True))
        dsb = ds.astype(BF16)
        dq_ref[...] = (jnp.dot(dsb, kw, preferred_element_type=F32) * scale).astype(BF16)
        dk_s[pl.ds(off, WIN), :] += lax.dot_general(dsb, q, TN, preferred_element_type=F32)
        dv_s[pl.ds(off, WIN), :] += lax.dot_general(p.astype(BF16), dov, TN, preferred_element_type=F32)
        dss_ref[...] += ds

        @pl.when(i == nq - 1)
        def _():
            dk_ref[...] = (dk_s[pl.ds(PADK, s), :] * scale).astype(BF16)
            dv_ref[...] = dv_s[pl.ds(PADK, s), :].astype(BF16)

    q, kv, bs, tile, full = _chunk_specs(s, nh)
    return pl.pallas_call(
        body, name=name, grid=(nh, nq), in_specs=[q, kv, kv, bs, tile], out_specs=[tile, full, full, bs],
        out_shape=[jax.ShapeDtypeStruct((s, bw), BF16)] * 3 + [jax.ShapeDtypeStruct((nh, QBLK, WIN), F32)],
        scratch_shapes=[pltpu.VMEM((s + PADK, HEAD), F32), pltpu.VMEM((s + PADK, HEAD), F32)],
        compiler_params=_cp(("arbitrary", "arbitrary")))(u, kpad, vpad, bias, do)


def _gelu_parts(y):
    th = jnp.tanh(GELU_K * (y + GELU_C * y * y * y))
    return 0.5 * y * (1.0 + th), th


def _block_diag(xb16, w_ref, nh, dims):
    return jnp.concatenate(
        [lax.dot_general(xb16[:, n * HEAD:(n + 1) * HEAD], w_ref[n], dims, preferred_element_type=F32)
         for n in range(nh)], axis=1)


def _lru_gates(ext, cw_ref, cb_ref, wr_ref, br_ref, wi_ref, bi_ref, lam_ref, ts, nh):
    shifted = [pltpu.roll(ext, CONV_WIDTH - 1 - j, 0)[8:, :] if j < CONV_WIDTH - 1 else ext[8:, :]
               for j in range(CONV_WIDTH)]
    xc = cb_ref[...]
    for j in range(CONV_WIDTH):
        xc = xc + shifted[j] * cw_ref[pl.ds(j, 1), :]
    xcb = xc.astype(BF16)
    r = _sigmoid(_block_diag(xcb, wr_ref, nh, NN) + br_ref[...])
    gi = _sigmoid(_block_diag(xcb, wi_ref, nh, NN) + bi_ref[...])
    lsl = _log_sigmoid(lam_ref[...])
    la = LRU_C * r * lsl
    a = jnp.exp(la)
    e2 = jnp.exp(2.0 * la)
    mult = jnp.sqrt(-jnp.tanh(la) * (e2 + 1.0))
    return shifted, xc, xcb, r, gi, lsl, a, e2, mult


def _lru_param_specs(bw, nh):
    vec = pl.BlockSpec((1, bw), lambda i: (0, 0))
    conv = pl.BlockSpec((8, bw), lambda i: (0, 0))
    blocks = pl.BlockSpec((nh, HEAD, HEAD), lambda i: (0, 0, 0))
    return [conv, vec, blocks, vec, blocks, vec, vec]


def _lru_fwd(name, u, params, bw):
    s, nh = u.shape[0], bw // HEAD
    ts = min(512, s)

    def body(rx_ref, ry_ref, cw_ref, cb_ref, wr_ref, br_ref, wi_ref, bi_ref, lam_ref, o_ref, h_ref, tail, hcar):
        @pl.when(pl.program_id(0) == 0)
        def _():
            tail[...] = jnp.zeros(tail.shape, F32)
            hcar[...] = jnp.zeros(hcar.shape, F32)

        rx = rx_ref[...].astype(F32)
        ext = jnp.concatenate([tail[...], rx], axis=0)
        tail[...] = rx[ts - 8:, :]
        _, xc, _, _, gi, _, a, _, mult = _lru_gates(ext, cw_ref, cb_ref, wr_ref, br_ref, wi_ref, bi_ref, lam_ref, ts, nh)
        acum, bcum = _scan_affine(a, mult * (gi * xc), False)
        h_ref[...] = bcum + acum * hcar[...]
        hcar[...] = h_ref[pl.ds(ts - 1, 1), :]
        o_ref[...] = (h_ref[...] * _gelu_parts(ry_ref[...].astype(F32))[0]).astype(BF16)

    row = pl.BlockSpec((ts, bw), lambda i: (i, 0))
    return pl.pallas_call(
        body, name=name, grid=(s // ts,),
        in_specs=[pl.BlockSpec((ts, bw), lambda i: (i, RX)), pl.BlockSpec((ts, bw), lambda i: (i, RY))]
        + _lru_param_specs(bw, nh),
        out_specs=[row, row],
        out_shape=[jax.ShapeDtypeStruct((s, bw), BF16), jax.ShapeDtypeStruct((s, bw), F32)],
        scratch_shapes=[pltpu.VMEM((8, bw), F32), pltpu.VMEM((1, bw), F32)],
        compiler_params=_cp(("arbitrary",)))(u, u, *params)


def _lru_bwd(name, u, h, do, params, bw):
    s, nh = u.shape[0], bw // HEAD
    ts = min(512, s)
    nb = s // ts
    t8 = ts // 8

    def body(rx_ref, rxp_ref, ry_ref, h_ref, hp_ref, do_ref, cw_ref, cb_ref, wr_ref, br_ref, wi_ref, bi_ref, lam_ref,
             drx_ref, dry_ref, dcw_ref, dcb_ref, dwr_ref, dbr_ref, dwi_ref, dbi_ref, dlam_ref, gcar, head):
        i = pl.program_id(0)
        first = i == nb - 1

        @pl.when(i == 0)
        def _():
            gcar[...] = jnp.zeros(gcar.shape, F32)
            head[...] = jnp.zeros(head.shape, F32)
            for ref in (dcw_ref, dcb_ref, dwr_ref, dbr_ref, dwi_ref, dbi_ref, dlam_ref):
                ref[...] = jnp.zeros(ref.shape, F32)

        rows = _iota((ts, 1), 0)
        rx = rx_ref[...].astype(F32)
        before = jnp.where(first, 0.0, rxp_ref[...].astype(F32))
        ext = jnp.concatenate([before, rx], axis=0)
        shifted, xc, xcb, r, gi, lsl, a, e2, mult = _lru_gates(
            ext, cw_ref, cb_ref, wr_ref, br_ref, wi_ref, bi_ref, lam_ref, ts, nh)

        ry = ry_ref[...].astype(F32)
        gel, th = _gelu_parts(ry)
        dgel = 0.5 * (1.0 + th) + 0.5 * ry * (1.0 - th * th) * GELU_K * (1.0 + 3.0 * GELU_C * ry * ry)
        dov = do_ref[...].astype(F32)
        hv = h_ref[...]
        dry_ref[...] = (dov * hv * dgel).astype(BF16)

        coef = jnp.where(rows < ts - 1, pltpu.roll(a, ts - 1, 0), 0.0)
        dh_in = dov * gel + jnp.where(rows == ts - 1, gcar[...], 0.0)
        dh = _scan_affine(coef, dh_in, True)[1]
        gcar[...] = jnp.sum(jnp.where(rows == 0, a * dh, 0.0), axis=0, keepdims=True)

        hprev = jnp.where(first, 0.0, hp_ref[...])
        hm1 = pltpu.roll(jnp.concatenate([hprev, hv], axis=0), 1, 0)[8:, :]
        dgx = dh * mult
        dla = dh * hm1 * a - dh * gi * xc * (e2 / mult)
        dpre_r = dla * (LRU_C * lsl) * r * (1.0 - r)
        dpre_i = dgx * xc * gi * (1.0 - gi)
        dlam_ref[...] += jnp.sum(dla * r, axis=0, keepdims=True) * (LRU_C * _sigmoid(-lam_ref[...]))
        dbr_ref[...] += jnp.sum(dpre_r, axis=0, keepdims=True)
        dbi_ref[...] += jnp.sum(dpre_i, axis=0, keepdims=True)
        drb, dib = dpre_r.astype(BF16), dpre_i.astype(BF16)
        for n in range(nh):
            cols = slice(n * HEAD, (n + 1) * HEAD)
            dwr_ref[n] += lax.dot_general(xcb[:, cols], drb[:, cols], TN, preferred_element_type=F32)
            dwi_ref[n] += lax.dot_general(xcb[:, cols], dib[:, cols], TN, preferred_element_type=F32)
        dxc = dgx * gi + _block_diag(drb, wr_ref, nh, NT) + _block_diag(dib, wi_ref, nh, NT)

        dcb_ref[...] += jnp.sum(dxc, axis=0, keepdims=True)
        for j in range(CONV_WIDTH):
            dcw_ref[pl.ds(j, 1), :] += jnp.sum(dxc * shifted[j], axis=0, keepdims=True)
        ext2 = jnp.concatenate([dxc, head[...]], axis=0)
        head[...] = dxc[:8, :]
        drx = dxc * cw_ref[pl.ds(CONV_WIDTH - 1, 1), :]
        for j in range(CONV_WIDTH - 1):
            up = CONV_WIDTH - 1 - j
            drx = drx + pltpu.roll(ext2, ts + 8 - up, 0)[:ts, :] * cw_ref[pl.ds(j, 1), :]
        drx_ref[...] = drx.astype(BF16)

    def blk(col):
        return lambda i: (nb - 1 - i, col)

    def prev8(col):
        return lambda i: (jnp.maximum((nb - 1 - i) * t8 - 1, 0), col)

    vec = pl.BlockSpec((1, bw), lambda i: (0, 0))
    conv = pl.BlockSpec((8, bw), lambda i: (0, 0))
    blocks = pl.BlockSpec((nh, HEAD, HEAD), lambda i: (0, 0, 0))
    return pl.pallas_call(
        body, name=name, grid=(nb,),
        in_specs=[pl.BlockSpec((ts, bw), blk(RX)), pl.BlockSpec((8, bw), prev8(RX)), pl.BlockSpec((ts, bw), blk(RY)),
                  pl.BlockSpec((ts, bw), blk(0)), pl.BlockSpec((8, bw), prev8(0)), pl.BlockSpec((ts, bw), blk(0))]
        + _lru_param_specs(bw, nh),
        out_specs=[pl.BlockSpec((ts, bw), blk(0)), pl.BlockSpec((ts, bw), blk(0)), conv, vec, blocks, vec, blocks, vec, vec],
        out_shape=[jax.ShapeDtypeStruct((s, bw), BF16)] * 2
        + [jax.ShapeDtypeStruct((8, bw), F32), jax.ShapeDtypeStruct((1, bw), F32),
           jax.ShapeDtypeStruct((nh, HEAD, HEAD), F32), jax.ShapeDtypeStruct((1, bw), F32),
           jax.ShapeDtypeStruct((nh, HEAD, HEAD), F32), jax.ShapeDtypeStruct((1, bw), F32),
           jax.ShapeDtypeStruct((1, bw), F32)],
        scratch_shapes=[pltpu.VMEM((1, bw), F32), pltpu.VMEM((8, bw), F32)],
        compiler_params=_cp(("arbitrary",)))(u, u, u, h, h, do, *params)


def _gate_merge(name, xb, w_gate, b_gate, o_all, w_branch, l, side=None):
    s, d = xb.shape
    bw = o_all.shape[2]
    tm, tn = min(1024, s), min(256, d)

    def body(x_ref, wg_ref, bg_ref, o_ref, wb_ref, m_ref, g_ref, p_ref):
        x = x_ref[...]
        acc = jnp.zeros((tm, tn), F32)
        for g in range(4):
            gate = _sigmoid(jnp.dot(x, wg_ref[g], preferred_element_type=F32) + bg_ref[g])
            proj = jnp.dot(o_ref[g], wb_ref[g], preferred_element_type=F32)
            term = gate * proj
            g_ref[g] = gate.astype(BF16)
            p_ref[g] = (term * (1.0 - gate)).astype(BF16)
            acc = acc + term
        m_ref[...] = acc.astype(BF16)

    quad = pl.BlockSpec((4, tm, tn), lambda n, m: (0, m, n))
    return _pcall(
        body, (xb, w_gate, b_gate, o_all, w_branch), name=name, grid=(d // tn, s // tm),
        in_specs=[pl.BlockSpec((tm, d), lambda n, m: (m, 0)),
                  pl.BlockSpec((None, 4, d, tn), lambda n, m: (0, 0, 0, n)),
                  pl.BlockSpec((None, 4, 1, tn), lambda n, m: (l, 0, 0, n)),
                  pl.BlockSpec((4, tm, bw), lambda n, m: (0, m, 0)),
                  pl.BlockSpec((None, 4, bw, tn), lambda n, m: (0, 0, 0, n))],
        out_specs=[pl.BlockSpec((tm, tn), lambda n, m: (m, n)), quad, quad],
        out_shape=[jax.ShapeDtypeStruct((s, d), BF16), jax.ShapeDtypeStruct((4, s, d), BF16),
                   jax.ShapeDtypeStruct((4, s, d), BF16)],
        sem=("parallel", "parallel"), side=side)


def _adamw(name, w, m, v, parts, layer=0, layers=1, earlier=None):
    cols = w.shape[1]
    p, rows = parts.shape[0], parts.shape[1]
    tr = _pow2_rows(rows, cols * max(1, p // 2))
    nb = rows // tr
    c1 = 1.0 - ADAM_B1 ** ADAM_STEP
    c2 = 1.0 - ADAM_B2 ** ADAM_STEP

    def body(w_ref, m_ref, v_ref, g_ref, *rest):
        go_ref, do_ref, mo_ref, vo_ref = rest[-4:]
        g = g_ref[0].astype(F32)
        for k in range(1, p):
            g = g + g_ref[k].astype(F32)
        m2 = ADAM_B1 * m_ref[...] + (1.0 - ADAM_B1) * g
        v2 = ADAM_B2 * v_ref[...] + (1.0 - ADAM_B2) * (g * g)
        go_ref[...] = g
        do_ref[...] = -ADAM_LR * ((m2 / c1) / (jnp.sqrt(v2 / c2) + ADAM_EPS) + ADAM_WD * w_ref[...])
        mo_ref[...] = m2
        vo_ref[...] = v2

    row = pl.BlockSpec((tr, cols), lambda i: (layer * nb + i, 0))
    held = list(earlier) if earlier is not None else []
    return pl.pallas_call(
        body, name=name, grid=(nb,),
        in_specs=[row, row, row, pl.BlockSpec((p, tr, cols), lambda i: (0, i, 0))] + [ANY] * len(held),
        out_specs=[row] * 4, out_shape=[jax.ShapeDtypeStruct((layers * rows, cols), F32)] * 4,
        input_output_aliases={4 + k: k for k in range(len(held))},
        compiler_params=_cp(("parallel",)))(w, m, v, parts, *held)


PACK_ROWS = 512


def _pack(arrays):
    rows = []
    for a in arrays:
        flat = a.astype(F32).reshape(-1)
        rows.append(jnp.pad(flat, (0, (-flat.shape[0]) % LANE)).reshape(-1, LANE))
    rows = jnp.concatenate(rows)
    return jnp.pad(rows, ((0, (-rows.shape[0]) % PACK_ROWS), (0, 0)))


def _unpack(packed, shapes):
    out, row = [], 0
    for shp in shapes:
        n = math.prod(shp)
        nrows = -(-n // LANE)
        out.append(packed[row:row + nrows].reshape(-1)[:n].reshape(shp))
        row += nrows
    return out


def _unshard(gathered, axis):
    block = gathered.shape[2:]
    full = jnp.swapaxes(gathered, 0, 1).reshape((N_DEV,) + block)
    full = jnp.moveaxis(full, 0, axis)
    return full.reshape(block[:axis] + (N_DEV * block[axis],) + block[axis + 1:])


def kernel(x, ln_in_g, ln_in_b, w_in, b_forget, conv_w, conv_b, w_r, b_r, w_i, b_i, lru_lambda, rel_bias, w_branch, w_gate, b_gate, w_out, ln1_g, ln1_b, w_ff1, w_ff2, ln2_g, ln2_b, loss_target, m_ln_in_g, m_ln_in_b, m_w_in, m_b_forget, m_conv_w, m_conv_b, m_w_r, m_b_r, m_w_i, m_b_i, m_lru_lambda, m_rel_bias, m_w_branch, m_w_gate, m_b_gate, m_w_out, m_ln1_g, m_ln1_b, m_w_ff1, m_w_ff2, m_ln2_g, m_ln2_b, v_ln_in_g, v_ln_in_b, v_w_in, v_b_forget, v_conv_w, v_conv_b, v_w_r, v_b_r, v_w_i, v_b_i, v_lru_lambda, v_rel_bias, v_w_branch, v_w_gate, v_b_gate, v_w_out, v_ln1_g, v_ln1_b, v_w_ff1, v_w_ff2, v_ln2_g, v_ln2_b):
    given = dict(zip(
        NAMES + ['loss_target'] + ['m_' + n for n in WEIGHTS] + ['v_' + n for n in WEIGHTS],
        (x, ln_in_g, ln_in_b, w_in, b_forget, conv_w, conv_b, w_r, b_r, w_i, b_i, lru_lambda, rel_bias, w_branch, w_gate, b_gate, w_out, ln1_g, ln1_b, w_ff1, w_ff2, ln2_g, ln2_b, loss_target, m_ln_in_g, m_ln_in_b, m_w_in, m_b_forget, m_conv_w, m_conv_b, m_w_r, m_b_r, m_w_i, m_b_i, m_lru_lambda, m_rel_bias, m_w_branch, m_w_gate, m_b_gate, m_w_out, m_ln1_g, m_ln1_b, m_w_ff1, m_w_ff2, m_ln2_g, m_ln2_b, v_ln_in_g, v_ln_in_b, v_w_in, v_b_forget, v_conv_w, v_conv_b, v_w_r, v_b_r, v_w_i, v_b_i, v_lru_lambda, v_rel_bias, v_w_branch, v_w_gate, v_b_gate, v_w_out, v_ln1_g, v_ln1_b, v_w_ff1, v_w_ff2, v_ln2_g, v_ln2_b)))

    s, d = x.shape[1], x.shape[2]
    nl = w_in.shape[0]
    bw = d // 4
    nh = bw // HEAD
    nu = 11 * bw
    rs = d // N_DEV
    dff = w_ff1.shape[2] * N_DEV
    fs = dff // N_DEV
    cs = d // N_DEV
    assert nl == DEPTH and nh * HEAD == bw and s % 1024 == 0 and d % 1024 == 0

    xi, yi, ci = _position()
    dev = 4 * xi + 2 * yi + ci
    c_arr = jnp.reshape(ci, (1,)).astype(I32)

    w_main = jnp.concatenate(
        [w_in[..., :3 * bw], w_in[..., 3 * bw + nh:],
         jnp.pad(w_in[..., 3 * bw:3 * bw + nh], ((0, 0), (0, 0), (0, LANE - nh)))], axis=-1).astype(BF16)
    nue = nu + LANE
    small_shapes = [conv_w.shape, rel_bias.shape, b_gate.shape]
    shard = {'main': w_main, 'branch': w_branch.astype(BF16), 'gate': w_gate.astype(BF16),
             'out': w_out.astype(BF16), 'ff1': w_ff1.astype(BF16), 'ff2': w_ff2.astype(BF16)}
    shard_axis = {'main': 1, 'branch': 3, 'gate': 2, 'out': 1, 'ff1': 2, 'ff2': 1}
    W = [dict() for _ in range(nl)]

    def gather_chips(l, keys, extra=()):
        side = _chips_side([shard[k][l:l + 1] for k in keys] + list(extra), True)
        side.todo = (l, keys)
        return side

    def gather_cores(chips):
        side = _cores_side(chips.results, True)
        side.todo = chips.todo
        return side

    def arrived(cores):
        l, keys = cores.todo
        for k, res in zip(keys, cores.results):
            W[l][k] = _unshard(res, shard_axis[k])
        return cores.results[len(keys):]

    xs = x[0]
    first = gather_chips(0, ['main'], [_pack([conv_w, rel_bias, b_gate])])
    h0, h0b = _ln_fwd("ln_in", xs, ln_in_g, ln_in_b, side=first)
    first = gather_cores(first)
    _run_side("gather_cores_first", first)
    small = arrived(first)[0]
    small = jnp.swapaxes(small, 0, 1).reshape((N_DEV,) + small.shape[2:])
    small = [_unpack(small[j], small_shapes) for j in range(N_DEV)]
    conv_w_full = jnp.concatenate([small[j][0] for j in range(N_DEV)], axis=-1)
    rel_bias_full = jnp.concatenate([small[j][1] for j in range(N_DEV)], axis=-1)
    b_gate_full = jnp.concatenate([small[j][2] for j in range(N_DEV)], axis=-1)
    b_gate4 = b_gate_full.reshape(nl, 4, 1, d)

    def lru_params(l):
        return (jnp.pad(conv_w_full[l], ((0, 8 - CONV_WIDTH), (0, 0))), conv_b[l].reshape(1, bw),
                w_r[l].astype(BF16), b_r[l].reshape(1, bw), w_i[l].astype(BF16), b_i[l].reshape(1, bw),
                lru_lambda[l].reshape(1, bw))

    def bias_rows(l):
        return jnp.pad(rel_bias_full[l], ((0, 16 - nh), (0, REL_PAD - REL_TABLE)))

    tm = min(1024, s)
    tkk = min(2048, d)

    saved = []
    cur, curb = h0, h0b
    chips = {}
    for l in range(nl):
        side = None
        if l == 0:
            side = chips['b0'] = gather_chips(0, ['gate', 'branch'])
        else:
            side = last_cores = gather_cores(chips.pop('d1'))
        tnw = _lane_tile(nu)
        u = _mm(f"w_in_{l}", curb, W[l]['main'], grid=(s // tm, nu // tnw, d // tkk),
                a_spec=pl.BlockSpec((tm, tkk), lambda m, n, k: (m, k)),
                b_spec=pl.BlockSpec((None, tkk, tnw), lambda m, n, k: (0, k, n)),
                dims=NN, acc_shape=(tm, tnw), out_shape=[jax.ShapeDtypeStruct((s, nu), BF16)],
                out_specs=[pl.BlockSpec((tm, tnw), lambda m, n, k: (m, n))], finish=_store(BF16), side=side)[0]
        if l == 1:
            arrived(last_cores)
        fl = _mm(f"w_forget_{l}", curb, W[l]['main'], grid=(s // tm, 1, d // tkk),
                 a_spec=pl.BlockSpec((tm, tkk), lambda m, n, k: (m, k)),
                 b_spec=pl.BlockSpec((None, tkk, LANE), lambda m, n, k: (0, k, nu // LANE)),
                 dims=NN, acc_shape=(tm, LANE), out_shape=[jax.ShapeDtypeStruct((s, LANE), F32)],
                 out_specs=[pl.BlockSpec((tm, LANE), lambda m, n, k: (m, 0))], finish=_store(F32))[0]
        bf_row = jnp.pad(b_forget[l], (0, LANE - nh)).reshape(1, LANE)
        cf = _cum_forget_fwd(f"cum_forget_{l}", fl, bf_row)
        tkf = _fox_tile(s)[1]
        cf_heads = cf[:, :nh].T
        cf_col = cf_heads.reshape(nh, s, 1)
        cf_row = cf_heads.reshape(nh, s // tkf, 1, tkf)
        side = None
        if l == 0:
            b0 = gather_cores(chips.pop('b0'))
            chips['c0'] = gather_chips(0, ['out', 'ff1', 'ff2'])
            side = _merge_sides(b0, chips['c0'])
        o_fox, lse = _fox_fwd(f"fox_fwd_{l}", u, cf_col, cf_row, bw, side=side)
        if l == 0:
            arrived(b0)
        lp = lru_params(l)
        o_lru, hstate = _lru_fwd(f"lru_fwd_{l}", u, lp, bw)
        o_sb = _sb_fwd(f"sb_fwd_{l}", u, bw)
        bias = _bias_expand(f"bias_expand_{l}", bias_rows(l), nh)
        kpad = jnp.pad(u[:, CK * bw:(CK + 1) * bw], ((PADK, 0), (0, 0)))
        vpad = jnp.pad(u[:, CV * bw:(CV + 1) * bw], ((PADK, 0), (0, 0)))
        side = gather_cores(chips.pop('c0')) if l == 0 else None
        o_ch = _chunk_fwd(f"chunk_fwd_{l}", u, kpad, vpad, bias, bw, side=side)
        if l == 0:
            arrived(side)
        o_all = jnp.stack([o_fox, o_lru, o_sb, o_ch])
        side = None
        if l == 0:
            side = chips['a1'] = gather_chips(1, ['main', 'gate', 'branch'])
        merged, gates, projs = _gate_merge(f"gate_merge_{l}", curb, W[l]['gate'], b_gate4, o_all, W[l]['branch'], l,
                                           side=side)
        side = gather_cores(chips.pop('a1')) if l == 0 else None
        z1, x1, x1b = _mm_ln(f"w_out_ln1_{l}", merged, W[l]['out'], 0, cur, ln1_g[l], ln1_b[l], side=side)
        if l == 0:
            arrived(side)
        tn1 = min(1024, dff)

        def ff1_finish(acc, ex, outs, ids):
            outs[0][...] = acc.astype(BF16)
            r = jnp.maximum(acc, 0.0)
            outs[1][...] = (r * r).astype(BF16)

        side = None
        if l == 0:
            side = chips['c1'] = gather_chips(1, ['out', 'ff1'])
        hp, hid = _mm(f"w_ff1_{l}", x1b, W[l]['ff1'], grid=(s // tm, dff // tn1, d // tkk),
                      a_spec=pl.BlockSpec((tm, tkk), lambda m, n, k: (m, k)),
                      b_spec=pl.BlockSpec((None, tkk, tn1), lambda m, n, k: (0, k, n)),
                      dims=NN, acc_shape=(tm, tn1),
                      out_shape=[jax.ShapeDtypeStruct((s, dff), BF16)] * 2,
                      out_specs=[pl.BlockSpec((tm, tn1), lambda m, n, k: (m, n))] * 2, finish=ff1_finish, side=side)
        side = None
        if l == 0:
            c1 = gather_cores(chips.pop('c1'))
            chips['d1'] = gather_chips(1, ['ff2'])
            side = _merge_sides(c1, chips['d1'])
        z2, x2, x2b = _mm_ln(f"w_ff2_ln2_{l}", hid, W[l]['ff2'], 0, x1, ln2_g[l], ln2_b[l], side=side)
        if l == 0:
            arrived(c1)
        saved.append(dict(xin=cur, xinb=curb, u=u, fl=fl, bf_row=bf_row, cf_col=cf_col, cf_row=cf_row, o_fox=o_fox,
                          lse=lse, lp=lp, hstate=hstate, bias=bias, kpad=kpad, vpad=vpad, o_all=o_all, merged=merged,
                          gates=gates, projs=projs, z1=z1, x1=x1, x1b=x1b, hp=hp, hid=hid, z2=z2))
        cur, curb = x2, x2b

    loss_tile, *last_ln = _loss_ln_bwd("loss_ln2_bwd", cur, loss_target[0], saved[nl - 1]['z2'], ln2_g[nl - 1])
    loss = lax.psum(loss_tile[0, 0], ("x", "y", "c"))
    dcur = None

    big = [dict() for _ in range(nl)]
    reduced = [dict() for _ in range(nl)]
    sm = {n: [None] * nl for n in ['b_forget', 'conv_w', 'conv_b', 'w_r', 'b_r', 'w_i', 'b_i', 'lru_lambda', 'rel_bias',
                                   'b_gate', 'ln1_g', 'ln1_b', 'ln2_g', 'ln2_b']}

    def split_columns(acc, ex, outs, ids):
        for j in range(N_DEV):
            outs[0][j] = acc[:, j * cs:(j + 1) * cs].astype(BF16)

    def grad_mm(key, name, a, b, *, shape, grid, a_spec, b_spec, out_spec, acc_shape, finish=_store(BF16)):
        big[l][key] = _mm(name, a, b, grid=grid, a_spec=a_spec, b_spec=b_spec, dims=TN, acc_shape=acc_shape,
                          out_shape=[jax.ShapeDtypeStruct(shape, BF16)], out_specs=[out_spec], finish=finish)[0]

    def reduce_cores(l, keys):
        side = _cores_side([big[l][k].reshape((4, 2) + big[l][k].shape[1:]) for k in keys], False)
        side.todo = (l, keys)
        return side

    def reduce_chips(cores):
        l, keys = cores.todo
        partial = []
        for k, mine, other in zip(keys, cores.operands, cores.results):
            cols = mine.shape[-1]
            rows = math.prod(mine.shape[2:]) // cols
            partial.append(_add_core_halves(f"add_cores_{k}_{l}", mine.reshape(4, 2, rows, cols),
                                            other.reshape(4, rows, cols), c_arr, BF16))
        side = _chips_side(partial, False)
        side.todo = (l, keys)
        return side

    def reduction_done(chips_side):
        l, keys = chips_side.todo
        reduced[l].update(zip(keys, chips_side.results))

    GROUP1, GROUP2 = ['w_ff2', 'w_ff1', 'w_out', 'w_branch', 'w_gate'], ['w_main']
    pending = None

    tks = min(2048, s)
    tmr = min(1024, d)
    nsh = tmr // rs

    for l in reversed(range(nl)):
        sv = saved[l]
        Wl = W[l]
        dz2, dz2b, dg, db = last_ln if l == nl - 1 else _ln_bwd(f"ln2_bwd_{l}", dcur, sv['z2'], ln2_g[l])
        sm['ln2_g'][l], sm['ln2_b'][l] = dg[0], db[0]
        tn1 = min(1024, dff)

        def dhp_finish(acc, ex, outs, ids):
            outs[0][...] = (acc * (2.0 * jnp.maximum(ex[0][...].astype(F32), 0.0))).astype(BF16)

        dhp = _mm(f"d_hidden_{l}", dz2b, Wl['ff2'], grid=(s // tm, dff // tn1, d // tkk),
                  a_spec=pl.BlockSpec((tm, tkk), lambda m, n, k: (m, k)),
                  b_spec=pl.BlockSpec((None, tn1, tkk), lambda m, n, k: (0, n, k)),
                  dims=NT, acc_shape=(tm, tn1), out_shape=[jax.ShapeDtypeStruct((s, dff), BF16)],
                  out_specs=[pl.BlockSpec((tm, tn1), lambda m, n, k: (m, n))], finish=dhp_finish,
                  extras=(sv['hp'],), extra_specs=(pl.BlockSpec((tm, tn1), lambda m, n, k: (m, n)),),
                  side=pending)[0]
        if pending is not None:
            reduction_done(pending)
            pending = None
        grad_mm('w_ff2', f"g_w_ff2_{l}", sv['hid'], dz2b, shape=(N_DEV, 1, fs, d), grid=(N_DEV, 1, s // tks),
                a_spec=pl.BlockSpec((tks, fs), lambda m, n, k: (k, m)),
                b_spec=pl.BlockSpec((tks, d), lambda m, n, k: (k, 0)),
                out_spec=pl.BlockSpec((None, None, fs, d), lambda m, n, k: (m, 0, 0, 0)), acc_shape=(fs, d))
        grad_mm('w_ff1', f"g_w_ff1_{l}", sv['x1b'], dhp, shape=(N_DEV, 1, d, fs), grid=(d // tmr, N_DEV, s // tks),
                a_spec=pl.BlockSpec((tks, tmr), lambda m, n, k: (k, m)),
                b_spec=pl.BlockSpec((tks, fs), lambda m, n, k: (k, n)),
                out_spec=pl.BlockSpec((None, None, tmr, fs), lambda m, n, k: (n, 0, m, 0)), acc_shape=(tmr, fs))
        tnd = min(1024, d)

        def resid_finish(scale):
            def finish(acc, ex, outs, ids):
                outs[0][...] = acc + scale * ex[0][...]
            return finish

        tile_md = pl.BlockSpec((tm, tnd), lambda m, n, k: (m, n))
        dx1 = _mm(f"d_x1_{l}", dhp, Wl['ff1'], grid=(s // tm, d // tnd, dff // tkk),
                  a_spec=pl.BlockSpec((tm, tkk), lambda m, n, k: (m, k)),
                  b_spec=pl.BlockSpec((None, tnd, tkk), lambda m, n, k: (0, n, k)),
                  dims=NT, acc_shape=(tm, tnd), out_shape=[jax.ShapeDtypeStruct((s, d), F32)],
                  out_specs=[tile_md], finish=resid_finish(ALPHA), extras=(dz2,), extra_specs=(tile_md,))[0]

        dz1, dz1b, dg, db = _ln_bwd(f"ln1_bwd_{l}", dx1, sv['z1'], ln1_g[l])
        sm['ln1_g'][l], sm['ln1_b'][l] = dg[0], db[0]
        tmg, tng = min(512, s), min(512, d)

        def gate_finish(acc, ex, outs, ids):
            @pl.when(ids[1] == 0)
            def _():
                outs[2][...] = jnp.zeros(outs[2].shape, F32)

            ones = jnp.ones((8, tmg), BF16)
            for g in range(4):
                dpre = (acc * ex[1][g].astype(F32)).astype(BF16)
                outs[0][g] = (acc * ex[0][g].astype(F32)).astype(BF16)
                outs[1][g] = dpre
                outs[2][g] += jnp.dot(ones, dpre, preferred_element_type=F32)

        quad = pl.BlockSpec((4, tmg, tng), lambda n, m, k: (0, m, n))
        dproj, dpre, dbg = _mm(
            f"d_merged_{l}", dz1b, Wl['out'], grid=(d // tng, s // tmg, d // tkk),
            a_spec=pl.BlockSpec((tmg, tkk), lambda n, m, k: (m, k)),
            b_spec=pl.BlockSpec((None, tng, tkk), lambda n, m, k: (0, n, k)),
            dims=NT, acc_shape=(tmg, tng),
            out_shape=[jax.ShapeDtypeStruct((4, s, d), BF16), jax.ShapeDtypeStruct((4, s, d), BF16),
                       jax.ShapeDtypeStruct((4, 8, d), F32)],
            out_specs=[quad, quad, pl.BlockSpec((4, 8, tng), lambda n, m, k: (0, 0, n))], finish=gate_finish,
            extras=(sv['gates'], sv['projs']), extra_specs=(quad, quad), sem=("arbitrary", "arbitrary", "arbitrary"))
        sm['b_gate'][l] = dbg[:, 0, :]
        grad_mm('w_out', f"g_w_out_{l}", sv['merged'], dz1b, shape=(N_DEV, 1, rs, d), grid=(d // tmr, 1, s // tks),
                a_spec=pl.BlockSpec((tks, tmr), lambda m, n, k: (k, m)),
                b_spec=pl.BlockSpec((tks, d), lambda m, n, k: (k, 0)),
                out_spec=pl.BlockSpec((nsh, None, rs, d), lambda m, n, k: (m, 0, 0, 0)), acc_shape=(tmr, d))

        nm = s // tm
        do_all = _mm(f"d_branch_{l}", dproj, Wl['branch'], grid=(4 * nm, 1, d // tkk),
                     a_spec=pl.BlockSpec((None, tm, tkk), lambda m, n, k: (m // nm, m % nm, k)),
                     b_spec=pl.BlockSpec((None, None, bw, tkk), lambda m, n, k: (0, m // nm, 0, k)),
                     dims=NT, acc_shape=(tm, bw), out_shape=[jax.ShapeDtypeStruct((4, s, bw), BF16)],
                     out_specs=[pl.BlockSpec((None, tm, bw), lambda m, n, k: (m // nm, m % nm, 0))],
                     finish=_store(BF16))[0]
        grad_mm('w_branch', f"g_w_branch_{l}", sv['o_all'], dproj, shape=(N_DEV, 1, 4, bw, cs),
                grid=(4, 1, s // tks), finish=split_columns,
                a_spec=pl.BlockSpec((None, tks, bw), lambda m, n, k: (m, k, 0)),
                b_spec=pl.BlockSpec((None, tks, d), lambda m, n, k: (m, k, 0)),
                out_spec=pl.BlockSpec((N_DEV, None, None, bw, cs), lambda m, n, k: (0, 0, m, 0, 0)),
                acc_shape=(bw, d))
        grad_mm('w_gate', f"g_w_gate_{l}", sv['xinb'], dpre, shape=(N_DEV, 1, 4, rs, d), grid=(d // tmr, 4, s // tks),
                a_spec=pl.BlockSpec((tks, tmr), lambda m, n, k: (k, m)),
                b_spec=pl.BlockSpec((None, tks, d), lambda m, n, k: (n, k, 0)),
                out_spec=pl.BlockSpec((nsh, None, None, rs, d), lambda m, n, k: (m, 0, n, 0, 0)),
                acc_shape=(tmr, d))
        nkg = d // tkk
        cores1 = reduce_cores(l, GROUP1)
        dx_gate = _mm(f"d_x_gates_{l}", dpre, Wl['gate'], grid=(s // tm, d // tnd, 4 * nkg),
                      a_spec=pl.BlockSpec((None, tm, tkk), lambda m, n, k: (k // nkg, m, k % nkg)),
                      b_spec=pl.BlockSpec((None, None, tnd, tkk), lambda m, n, k: (0, k // nkg, n, k % nkg)),
                      dims=NT, acc_shape=(tm, tnd), out_shape=[jax.ShapeDtypeStruct((s, d), F32)],
                      out_specs=[tile_md], finish=resid_finish(ALPHA), extras=(dz1,), extra_specs=(tile_md,),
                      side=cores1)[0]

        u = sv['u']
        chips1 = reduce_chips(cores1)
        dfq, dfk, dfv, dcc, dcr = _fox_bwd(f"fox_bwd_{l}", u, sv['cf_col'], sv['cf_row'], sv['o_fox'], do_all[0],
                                          sv['lse'], bw, side=chips1)
        reduction_done(chips1)
        dcf = (dcc[:, :, 0, :].reshape(nh, s) + dcr.reshape(nh, s)).T
        dflb, dbf = _cum_forget_bwd(f"cum_forget_bwd_{l}", jnp.pad(dcf, ((0, 0), (0, LANE - nh))), sv['fl'],
                                    sv['bf_row'])
        sm['b_forget'][l] = dbf[0, :nh]
        drx, dry, dcw, dcb, dwr, dbr, dwi, dbi, dlam = _lru_bwd(f"lru_bwd_{l}", u, sv['hstate'], do_all[1], sv['lp'], bw)
        sm['conv_w'][l], sm['conv_b'][l], sm['w_r'][l], sm['b_r'][l] = dcw[:CONV_WIDTH], dcb[0], dwr, dbr[0]
        sm['w_i'][l], sm['b_i'][l], sm['lru_lambda'][l] = dwi, dbi[0], dlam[0]
        dsq, dsk, dsv = _sb_bwd(f"sb_bwd_{l}", u, do_all[2], bw)
        dcq, dck, dcv, dss = _chunk_bwd(f"chunk_bwd_{l}", u, sv['kpad'], sv['vpad'], sv['bias'], do_all[3], bw)
        sm['rel_bias'][l] = _bias_reduce(f"bias_reduce_{l}", dss, nh)[:, 0, :REL_TABLE]
        du = jnp.concatenate([dfq, dfk, dfv, drx, dry, dsq, dsk, dsv, dcq, dck, dcv, dflb], axis=1)

        tnu = _lane_tile(nue)
        grad_mm('w_main', f"g_w_in_{l}", sv['xinb'], du, shape=(N_DEV, 1, rs, nue),
                grid=(d // tmr, nue // tnu, s // tks),
                a_spec=pl.BlockSpec((tks, tmr), lambda m, n, k: (k, m)),
                b_spec=pl.BlockSpec((tks, tnu), lambda m, n, k: (k, n)),
                out_spec=pl.BlockSpec((nsh, None, rs, tnu), lambda m, n, k: (m, 0, 0, n)), acc_shape=(tmr, tnu))
        cores2 = reduce_cores(l, GROUP2)
        if l == 0:
            _run_side("reduce_cores_last", cores2)
            cores2 = reduce_chips(cores2)
        dcur = _mm(f"d_x_in_{l}", du, Wl['main'], grid=(s // tm, d // tnd, nue // tnu),
                   a_spec=pl.BlockSpec((tm, tnu), lambda m, n, k: (m, k)),
                   b_spec=pl.BlockSpec((None, tnd, tnu), lambda m, n, k: (0, n, k)),
                   dims=NT, acc_shape=(tm, tnd), out_shape=[jax.ShapeDtypeStruct((s, d), F32)],
                   out_specs=[tile_md], finish=resid_finish(1.0), extras=(dx_gate,), extra_specs=(tile_md,),
                   side=cores2)[0]
        if l == 0:
            reduction_done(cores2)
        else:
            pending = reduce_chips(cores2)

    grad_x, _, dg_in, db_in = _ln_bwd("ln_in_bwd", dcur, xs, ln_in_g)

    small_names = ['ln_in_g', 'ln_in_b', 'b_forget', 'conv_w', 'conv_b', 'w_r', 'b_r', 'w_i', 'b_i', 'lru_lambda',
                   'rel_bias', 'b_gate', 'ln1_g', 'ln1_b', 'ln2_g', 'ln2_b']
    local = {'ln_in_g': dg_in[0], 'ln_in_b': db_in[0]}
    for n in small_names[2:]:
        local[n] = jnp.stack(sm[n])
    full_shapes = [local[n].shape for n in small_names]
    packed = _pack([local[n] for n in small_names])
    every = _run_side("gather_small_cores",
                      _cores_side(_run_side("gather_small_chips", _chips_side([packed], True)), True))[0]
    every = jnp.swapaxes(every, 0, 1).reshape((N_DEV,) + packed.shape)
    total = dict(zip(small_names, _unpack(_sum_parts("sum_small", every), full_shapes)))
    for n, width in (('conv_w', bw // N_DEV), ('rel_bias', REL_TABLE // N_DEV), ('b_gate', cs)):
        total[n] = lax.dynamic_slice_in_dim(total[n], dev * width, width, axis=2)

    out = {}

    def update(n, parts):
        cols = parts[0].shape[2]
        w2, m2, v2 = (given[p + n].reshape(-1, cols) for p in ('', 'm_', 'v_'))
        res = None
        for l in range(nl):
            res = _adamw(f"adamw_{n}_{l}", w2, m2, v2, parts[l], layer=l, layers=nl, earlier=res)
        out[n] = [r.reshape(given[n].shape) for r in res]

    def parts_w_in(l):
        pm = reduced[l]['w_main']
        return jnp.concatenate([pm[..., :3 * bw], pm[..., nu:nu + nh], pm[..., 3 * bw:nu]], axis=-1)

    update('w_in', [parts_w_in(l) for l in range(nl)])
    for n in ('w_branch', 'w_gate', 'w_out', 'w_ff1', 'w_ff2'):
        update(n, [reduced[l][n] for l in range(nl)])

    small_shapes2 = [given[n].shape for n in small_names]
    res = _adamw("adamw_small", _pack([given[n] for n in small_names]), _pack([given['m_' + n] for n in small_names]),
                 _pack([given['v_' + n] for n in small_names]), _pack([total[n] for n in small_names])[None])
    res = [_unpack(r, small_shapes2) for r in res]
    for j, n in enumerate(small_names):
        out[n] = [res[k][j] for k in range(4)]

    return (loss, grad_x[None], *[out[n][0] for n in WEIGHTS], *[out[n][1] for n in WEIGHTS],
            *[out[n][2] for n in WEIGHTS], *[out[n][3] for n in WEIGHTS])
```

```python
import functools
import math

import jax
import jax.numpy as jnp
from jax import lax
from jax.experimental import pallas as pl
from jax.experimental.pallas import tpu as pltpu

F32, BF16, I32 = jnp.float32, jnp.bfloat16, jnp.int32
MESH = pl.DeviceIdType.MESH
ANY = pl.BlockSpec(memory_space=pl.ANY)

LANE = 128
VMEM_LIMIT = 56 * 1024 * 1024
N_DEV = 8

HEAD = 128
CHUNK = 64
LOOKBACK = 8
BAND = (LOOKBACK + 1) * CHUNK
QCHUNKS = 4
QBLK = QCHUNKS * CHUNK
WIN = BAND + (QCHUNKS - 1) * CHUNK
PADK = LOOKBACK * CHUNK
REL_CLIP = 256
REL_TABLE = REL_CLIP + CHUNK
REL_PAD = 384
CONV_WIDTH = 4
LRU_C = 8.0
LN_EPS = 1e-5
DEPTH = 2
ALPHA = (2.0 * DEPTH) ** 0.25
NEG = -1e30
SB_DEAD = -104.0
GELU_K = math.sqrt(2.0 / math.pi)
GELU_C = 0.044715

ADAM_LR, ADAM_B1, ADAM_B2, ADAM_EPS, ADAM_WD, ADAM_STEP = 0.001, 0.9, 0.999, 1e-08, 0.01, 10

NN = (((1,), (0,)), ((), ()))
NT = (((1,), (1,)), ((), ()))
TN = (((0,), (0,)), ((), ()))

FQ, FK, FV, RX, RY, SQ, SK, SV, CQ, CK, CV = range(11)

NAMES = ['x', 'ln_in_g', 'ln_in_b', 'w_in', 'b_forget', 'conv_w', 'conv_b', 'w_r', 'b_r', 'w_i', 'b_i', 'lru_lambda',
         'rel_bias', 'w_branch', 'w_gate', 'b_gate', 'w_out', 'ln1_g', 'ln1_b', 'w_ff1', 'w_ff2', 'ln2_g', 'ln2_b']
WEIGHTS = NAMES[1:]


def _cp(sem=None):
    return pltpu.CompilerParams(dimension_semantics=sem, vmem_limit_bytes=VMEM_LIMIT)


def _iota(shape, dim):
    return lax.broadcasted_iota(I32, shape, dim)


def _sigmoid(x):
    return 1.0 / (1.0 + jnp.exp(-x))


def _log_sigmoid(x):
    return jnp.minimum(x, 0.0) - jnp.log(1.0 + jnp.exp(-jnp.abs(x)))


def _lane_tile(n, cap=1536):
    best = max(t for t in range(LANE, cap + 1, LANE) if n % t == 0)
    return n if best == LANE else best


def _pow2_rows(rows, cols, elems=524288):
    t = 8
    while t * 2 <= rows and t * 2 * cols <= elems and rows % (t * 2) == 0:
        t *= 2
    return t


def _position():
    return lax.axis_index("x"), lax.axis_index("y"), lax.axis_index("c")


class _Side:
    def __init__(self, operands, out_shape, sems, start, finish, aliases=(), parts=()):
        self.operands, self.out_shape, self.sems = list(operands), list(out_shape), list(sems)
        self.start, self.finish, self.aliases, self.parts = start, finish, list(aliases), parts
        self.results = None

    def set_results(self, res):
        self.results = list(res)
        off = 0
        for part in self.parts:
            part.set_results(res[off:off + len(part.out_shape)])
            off += len(part.out_shape)


def _merge_sides(a, b):
    ai, ao, asm = len(a.operands), len(a.out_shape), len(a.sems)

    def start(ins, outs, sems):
        a.start(ins[:ai], outs[:ao], sems[:asm])
        b.start(ins[ai:], outs[ao:], sems[asm:])

    def finish(ins, outs, sems):
        a.finish(ins[:ai], outs[:ao], sems[:asm])
        b.finish(ins[ai:], outs[ao:], sems[asm:])

    return _Side(a.operands + b.operands, a.out_shape + b.out_shape, a.sems + b.sems, start, finish,
                 a.aliases + [(i + ai, o + ao) for i, o in b.aliases], parts=(a, b))


def _chips_side(xs, gather):
    n = len(xs)

    def copies(ins, outs, sems, arrivals):
        send_sems, recv_sems, local_sems = sems
        x, y, c = _position()
        q = 2 * x + y
        chips = [(1 - x, y), (x, 1 - y), (1 - x, 1 - y)]

        def src(t, slot):
            return ins[t] if gather else ins[t].at[slot]

        def dst(t, slot):
            return outs[t].at[c, slot] if gather else outs[t].at[slot]

        def remote(t, j, landing):
            px, py = chips[j]
            return pltpu.make_async_remote_copy(
                src_ref=src(t, 2 * px + py), dst_ref=dst(t, landing), send_sem=send_sems.at[t, j],
                recv_sem=recv_sems.at[t, j], device_id=(px, py, c), device_id_type=MESH)

        local = [pltpu.make_async_copy(src(t, q), dst(t, q), local_sems.at[t]) for t in range(n)]
        sends = [remote(t, j, q) for t in range(n) for j in range(3)]
        if not arrivals:
            return local, sends, []
        return local, sends, [remote(t, j, 2 * px + py) for t in range(n) for j, (px, py) in enumerate(chips)]

    def start(ins, outs, sems):
        local, sends, _ = copies(ins, outs, sems, False)
        for cp in local + sends:
            cp.start()

    def finish(ins, outs, sems):
        local, sends, recvs = copies(ins, outs, sems, True)
        for cp in recvs:
            cp.wait_recv()
        for cp in sends:
            cp.wait_send()
        for cp in local:
            cp.wait()

    out_shape = [jax.ShapeDtypeStruct((2, 4) + a.shape if gather else a.shape, a.dtype) for a in xs]
    sems = [pltpu.SemaphoreType.DMA((n, 3)), pltpu.SemaphoreType.DMA((n, 3)), pltpu.SemaphoreType.DMA((n,))]
    return _Side(xs, out_shape, sems, start, finish)


def _cores_side(xs, gather):
    n = len(xs)
    m = 1 if gather else 4

    def copies(ins, outs, sems, arrivals):
        send_sems, recv_sems = sems
        x, y, c = _position()

        def remote(t, j, landing):
            s = outs[t].at[c] if gather else ins[t].at[j, 1 - c]
            d = outs[t].at[landing] if gather else outs[t].at[j]
            return pltpu.make_async_remote_copy(
                src_ref=s, dst_ref=d, send_sem=send_sems.at[t, j], recv_sem=recv_sems.at[t, j],
                device_id=(x, y, 1 - c), device_id_type=MESH)

        sends = [remote(t, j, c) for t in range(n) for j in range(m)]
        return sends, [remote(t, j, 1 - c) for t in range(n) for j in range(m)] if arrivals else []

    def start(ins, outs, sems):
        for cp in copies(ins, outs, sems, False)[0]:
            cp.start()

    def finish(ins, outs, sems):
        sends, recvs = copies(ins, outs, sems, True)
        for cp in recvs:
            cp.wait_recv()
        for cp in sends:
            cp.wait_send()

    if gather:
        out_shape = [jax.ShapeDtypeStruct(a.shape, a.dtype) for a in xs]
    else:
        out_shape = [jax.ShapeDtypeStruct((4,) + a.shape[2:], a.dtype) for a in xs]
    sems = [pltpu.SemaphoreType.DMA((n, m)), pltpu.SemaphoreType.DMA((n, m))]
    return _Side(xs, out_shape, sems, start, finish, aliases=[(t, t) for t in range(n)] if gather else [])


def _run_side(name, side):
    ni, no = len(side.operands), len(side.out_shape)

    def body(*refs):
        ins, outs, sems = refs[:ni], refs[ni:ni + no], refs[ni + no:]
        side.start(ins, outs, sems)
        side.finish(ins, outs, sems)

    side.set_results(pl.pallas_call(
        body, name=name, out_shape=side.out_shape, in_specs=[ANY] * ni, out_specs=[ANY] * no,
        input_output_aliases=dict(side.aliases), scratch_shapes=side.sems)(*side.operands))
    return side.results


def _pcall(body, operands, *, name, grid, in_specs, out_specs, out_shape, scratch_shapes=(), sem=None, aliases=None,
           side=None):
    if side is None:
        return pl.pallas_call(body, name=name, grid=grid, in_specs=list(in_specs), out_specs=list(out_specs),
                              out_shape=list(out_shape), scratch_shapes=list(scratch_shapes),
                              input_output_aliases=aliases or {}, compiler_params=_cp(sem))(*operands)
    ni, no, ns = len(in_specs), len(out_shape), len(scratch_shapes)
    si, so = len(side.operands), len(side.out_shape)

    def carrying(*refs):
        ins, sins = refs[:ni], refs[ni:ni + si]
        outs, souts = refs[ni + si:ni + si + no], refs[ni + si + no:ni + si + no + so]
        scratch, ssems = refs[ni + si + no + so:ni + si + no + so + ns], refs[ni + si + no + so + ns:]
        ids = [pl.program_id(a) for a in range(len(grid))]
        first = functools.reduce(jnp.logical_and, [i == 0 for i in ids])
        last = functools.reduce(jnp.logical_and, [i == g - 1 for i, g in zip(ids, grid)])

        @pl.when(first)
        def _():
            side.start(sins, souts, ssems)

        body(*ins, *outs, *scratch)

        @pl.when(last)
        def _():
            side.finish(sins, souts, ssems)

    joined = dict(aliases or {})
    joined.update({ni + i: no + o for i, o in side.aliases})
    res = pl.pallas_call(
        carrying, name=name, grid=grid, in_specs=[*in_specs, *[ANY] * si], out_specs=[*out_specs, *[ANY] * so],
        out_shape=[*out_shape, *side.out_shape], scratch_shapes=[*scratch_shapes, *side.sems],
        input_output_aliases=joined, compiler_params=_cp(("arbitrary",) * len(grid)))(*operands, *side.operands)
    side.set_results(res[no:])
    return res[:no]


def _add_core_halves(name, mine, other, c, out_dtype):
    _, _, rows, cols = mine.shape
    tr = _pow2_rows(rows, cols)

    def body(c_ref, a_ref, b_ref, o_ref):
        o_ref[...] = (a_ref[...].astype(F32) + b_ref[...].astype(F32)).astype(out_dtype)

    grid_spec = pltpu.PrefetchScalarGridSpec(
        num_scalar_prefetch=1, grid=(4, rows // tr),
        in_specs=[pl.BlockSpec((None, None, tr, cols), lambda j, i, c_ref: (j, c_ref[0], i, 0)),
                  pl.BlockSpec((None, tr, cols), lambda j, i, c_ref: (j, i, 0))],
        out_specs=pl.BlockSpec((None, tr, cols), lambda j, i, c_ref: (j, i, 0)))
    return pl.pallas_call(body, name=name, grid_spec=grid_spec,
                          out_shape=jax.ShapeDtypeStruct((4, rows, cols), out_dtype),
                          compiler_params=_cp(("parallel", "parallel")))(c, mine, other)


def _sum_parts(name, parts):
    p, rows, cols = parts.shape
    tr = _pow2_rows(rows, cols * p)

    def body(a_ref, o_ref):
        acc = a_ref[0]
        for k in range(1, p):
            acc = acc + a_ref[k]
        o_ref[...] = acc

    return pl.pallas_call(body, name=name, grid=(rows // tr,),
                          in_specs=[pl.BlockSpec((p, tr, cols), lambda i: (0, i, 0))],
                          out_specs=pl.BlockSpec((tr, cols), lambda i: (i, 0)),
                          out_shape=jax.ShapeDtypeStruct((rows, cols), F32),
                          compiler_params=_cp(("parallel",)))(parts)


def _mm(name, a, b, *, grid, a_spec, b_spec, dims, acc_shape, out_shape, out_specs, finish,
        extras=(), extra_specs=(), aliases=None, sem=("parallel", "parallel", "arbitrary"), side=None):
    nk, ne, no = grid[2], len(extras), len(out_shape)

    def body(*refs):
        a_ref, b_ref = refs[0], refs[1]
        ex, outs = refs[2:2 + ne], refs[2 + ne:2 + ne + no]
        ids = (pl.program_id(0), pl.program_id(1))
        def prod():
            return lax.dot_general(a_ref[...], b_ref[...], dims, preferred_element_type=F32)

        if nk == 1:
            finish(prod(), ex, outs, ids)
            return
        acc = refs[2 + ne + no]
        k = pl.program_id(2)

        @pl.when(k == 0)
        def _():
            acc[...] = prod()

        @pl.when(jnp.logical_and(k > 0, k < nk - 1))
        def _():
            acc[...] += prod()

        @pl.when(k == nk - 1)
        def _():
            finish(acc[...] + prod(), ex, outs, ids)

    return _pcall(body, (a, b, *extras), name=name, grid=grid, in_specs=[a_spec, b_spec, *extra_specs],
                  out_specs=out_specs, out_shape=out_shape,
                  scratch_shapes=[pltpu.VMEM(acc_shape, F32)] if nk > 1 else [], sem=sem, aliases=aliases, side=side)


def _store(dtype):
    def finish(acc, ex, outs, ids):
        outs[0][...] = acc.reshape(outs[0].shape).astype(dtype)
    return finish


def _layer_norm_rows(z, g, b):
    mu = jnp.mean(z, axis=1, keepdims=True)
    zc = z - mu
    var = jnp.mean(zc * zc, axis=1, keepdims=True)
    return zc * lax.rsqrt(var + LN_EPS) * g + b


def _mm_ln(name, a, w, l, resid, g, b, side=None):
    s, kdim = a.shape
    d = w.shape[2]
    tm, tk = min(512, s), min(1024, kdim)

    def finish(acc, ex, outs, ids):
        z = acc + ALPHA * ex[0][...]
        y = _layer_norm_rows(z, ex[1][...], ex[2][...])
        outs[0][...] = z
        outs[1][...] = y
        outs[2][...] = y.astype(BF16)

    row = pl.BlockSpec((tm, d), lambda m, n, k: (m, 0))
    vec = pl.BlockSpec((1, d), lambda m, n, k: (0, 0))
    return _mm(name, a, w, grid=(s // tm, 1, kdim // tk),
               a_spec=pl.BlockSpec((tm, tk), lambda m, n, k: (m, k)),
               b_spec=pl.BlockSpec((None, tk, d), lambda m, n, k: (l, k, 0)),
               dims=NN, acc_shape=(tm, d),
               out_shape=[jax.ShapeDtypeStruct((s, d), F32), jax.ShapeDtypeStruct((s, d), F32),
                          jax.ShapeDtypeStruct((s, d), BF16)],
               out_specs=[row, row, row], finish=finish,
               extras=(resid, g.reshape(1, d), b.reshape(1, d)), extra_specs=(row, vec, vec), side=side)


def _ln_fwd(name, x, g, b, side=None):
    s, d = x.shape
    tr = min(256, s)

    def body(x_ref, g_ref, b_ref, y_ref, yb_ref):
        y = _layer_norm_rows(x_ref[...], g_ref[...], b_ref[...])
        y_ref[...] = y
        yb_ref[...] = y.astype(BF16)

    row = pl.BlockSpec((tr, d), lambda i: (i, 0))
    vec = pl.BlockSpec((1, d), lambda i: (0, 0))
    return _pcall(body, (x, g.reshape(1, d), b.reshape(1, d)), name=name, grid=(s // tr,), in_specs=[row, vec, vec],
                  out_specs=[row, row],
                  out_shape=[jax.ShapeDtypeStruct((s, d), F32), jax.ShapeDtypeStruct((s, d), BF16)],
                  sem=("parallel",), side=side)


def _ln_bwd_rows(dyv, zz, g, dz_ref, dzb_ref, dg_ref, db_ref):
    mu = jnp.mean(zz, axis=1, keepdims=True)
    zc = zz - mu
    rstd = lax.rsqrt(jnp.mean(zc * zc, axis=1, keepdims=True) + LN_EPS)
    xhat = zc * rstd
    dg_ref[...] += jnp.sum(dyv * xhat, axis=0, keepdims=True)
    db_ref[...] += jnp.sum(dyv, axis=0, keepdims=True)
    dxh = dyv * g
    dz = rstd * (dxh - jnp.mean(dxh, axis=1, keepdims=True) - xhat * jnp.mean(dxh * xhat, axis=1, keepdims=True))
    dz_ref[...] = dz
    dzb_ref[...] = dz.astype(BF16)


def _ln_bwd(name, dy, z, g):
    s, d = z.shape
    tr = min(256, s)

    def body(dy_ref, z_ref, g_ref, dz_ref, dzb_ref, dg_ref, db_ref):
        @pl.when(pl.program_id(0) == 0)
        def _():
            dg_ref[...] = jnp.zeros(dg_ref.shape, F32)
            db_ref[...] = jnp.zeros(db_ref.shape, F32)

        _ln_bwd_rows(dy_ref[...], z_ref[...], g_ref[...], dz_ref, dzb_ref, dg_ref, db_ref)

    row = pl.BlockSpec((tr, d), lambda i: (i, 0))
    vec = pl.BlockSpec((1, d), lambda i: (0, 0))
    return _pcall(
        body, (dy, z, g.reshape(1, d)), name=name, grid=(s // tr,), in_specs=[row, row, vec],
        out_specs=[row, row, vec, vec],
        out_shape=[jax.ShapeDtypeStruct((s, d), F32), jax.ShapeDtypeStruct((s, d), BF16),
                   jax.ShapeDtypeStruct((1, d), F32), jax.ShapeDtypeStruct((1, d), F32)],
        sem=("arbitrary",))


def _loss_ln_bwd(name, y, target, z, g):
    s, d = y.shape
    tr = min(256, s)

    def body(y_ref, t_ref, z_ref, g_ref, loss_ref, dz_ref, dzb_ref, dg_ref, db_ref):
        @pl.when(pl.program_id(0) == 0)
        def _():
            loss_ref[...] = jnp.zeros(loss_ref.shape, F32)
            dg_ref[...] = jnp.zeros(dg_ref.shape, F32)
            db_ref[...] = jnp.zeros(db_ref.shape, F32)

        e = y_ref[...] - t_ref[...]
        loss_ref[...] += jnp.sum(e * e) * (0.5 / d)
        _ln_bwd_rows(e * (1.0 / d), z_ref[...], g_ref[...], dz_ref, dzb_ref, dg_ref, db_ref)

    row = pl.BlockSpec((tr, d), lambda i: (i, 0))
    vec = pl.BlockSpec((1, d), lambda i: (0, 0))
    return pl.pallas_call(
        body, name=name, grid=(s // tr,), in_specs=[row, row, row, vec],
        out_specs=[pl.BlockSpec((8, LANE), lambda i: (0, 0)), row, row, vec, vec],
        out_shape=[jax.ShapeDtypeStruct((8, LANE), F32), jax.ShapeDtypeStruct((s, d), F32),
                   jax.ShapeDtypeStruct((s, d), BF16), jax.ShapeDtypeStruct((1, d), F32),
                   jax.ShapeDtypeStruct((1, d), F32)],
        compiler_params=_cp(("arbitrary",)))(y, target, z, g.reshape(1, d))


def _scan_add(x, reverse):
    ts = x.shape[0]
    rows = _iota((ts, 1), 0)
    dist = 1
    while dist < ts:
        if reverse:
            x = x + jnp.where(rows < ts - dist, pltpu.roll(x, ts - dist, 0), 0.0)
        else:
            x = x + jnp.where(rows >= dist, pltpu.roll(x, dist, 0), 0.0)
        dist *= 2
    return x


def _scan_affine(a, b, reverse):
    ts = a.shape[0]
    rows = _iota((ts, 1), 0)
    dist = 1
    while dist < ts:
        shift = ts - dist if reverse else dist
        valid = rows < ts - dist if reverse else rows >= dist
        b = b + a * jnp.where(valid, pltpu.roll(b, shift, 0), 0.0)
        a = a * jnp.where(valid, pltpu.roll(a, shift, 0), 1.0)
        dist *= 2
    return a, b


def _cum_forget_fwd(name, fl, bias):
    s = fl.shape[0]
    ts = min(1024, s)

    def body(f_ref, b_ref, o_ref, carry):
        @pl.when(pl.program_id(0) == 0)
        def _():
            carry[...] = jnp.zeros(carry.shape, F32)

        o_ref[...] = _scan_add(_log_sigmoid(f_ref[...] + b_ref[...]), False) + carry[...]
        carry[...] = o_ref[pl.ds(ts - 1, 1), :]

    row = pl.BlockSpec((ts, LANE), lambda i: (i, 0))
    return pl.pallas_call(body, name=name, grid=(s // ts,),
                          in_specs=[row, pl.BlockSpec((1, LANE), lambda i: (0, 0))], out_specs=row,
                          out_shape=jax.ShapeDtypeStruct((s, LANE), F32),
                          scratch_shapes=[pltpu.VMEM((1, LANE), F32)],
                          compiler_params=_cp(("arbitrary",)))(fl, bias)


def _cum_forget_bwd(name, dcf, fl, bias):
    s = fl.shape[0]
    ts = min(1024, s)
    nb = s // ts

    def body(d_ref, f_ref, b_ref, o_ref, db_ref, carry):
        @pl.when(pl.program_id(0) == 0)
        def _():
            carry[...] = jnp.zeros(carry.shape, F32)
            db_ref[...] = jnp.zeros(db_ref.shape, F32)

        run = _scan_add(d_ref[...], True) + carry[...]
        carry[...] = jnp.sum(jnp.where(_iota((ts, 1), 0) == 0, run, 0.0), axis=0, keepdims=True)
        dfl = run * _sigmoid(-(f_ref[...] + b_ref[...]))
        o_ref[...] = dfl.astype(BF16)
        db_ref[...] += jnp.sum(dfl, axis=0, keepdims=True)

    row = pl.BlockSpec((ts, LANE), lambda i: (nb - 1 - i, 0))
    vec = pl.BlockSpec((1, LANE), lambda i: (0, 0))
    return pl.pallas_call(body, name=name, grid=(nb,), in_specs=[row, row, vec], out_specs=[row, vec],
                          out_shape=[jax.ShapeDtypeStruct((s, LANE), BF16), jax.ShapeDtypeStruct((1, LANE), F32)],
                          scratch_shapes=[pltpu.VMEM((1, LANE), F32)],
                          compiler_params=_cp(("arbitrary",)))(dcf, fl, bias)


def _fox_specs(s, nh, tq, tk):
    q = pl.BlockSpec((tq, HEAD), lambda h, i: (i, FQ * nh + h))
    k = pl.BlockSpec((s, HEAD), lambda h, i: (0, FK * nh + h))
    v = pl.BlockSpec((s, HEAD), lambda h, i: (0, FV * nh + h))
    col = pl.BlockSpec((None, tq, 1), lambda h, i: (h, i, 0))
    rowv = pl.BlockSpec((None, s // tk, 1, tk), lambda h, i: (h, 0, 0, 0))
    tile = pl.BlockSpec((tq, HEAD), lambda h, i: (i, h))
    full = pl.BlockSpec((s, HEAD), lambda h, i: (0, h))
    return q, k, v, col, rowv, tile, full


def _fox_tile(s):
    return min(512, s), min(1024, s)


def _fox_scores(q, k_ref, cfq, cfr_ref, kb, tk, scale, qpos=None):
    off = pl.multiple_of(kb * tk, tk)
    k = k_ref[pl.ds(off, tk), :]
    sc = lax.dot_general(q, k, NT, preferred_element_type=F32) * scale + cfq - cfr_ref[kb]
    mask = None
    if qpos is not None:
        mask = kb * tk + _iota((1, tk), 1) <= qpos
        sc = jnp.where(mask, sc, NEG)
    return sc, mask, k, off


def _fox_fwd(name, u, cf_col, cf_row, bw, side=None):
    s, nh = u.shape[0], bw // HEAD
    tq, tk = _fox_tile(s)
    scale = HEAD ** -0.5

    def body(q_ref, k_ref, v_ref, cfc_ref, cfr_ref, o_ref, lse_ref):
        i = pl.program_id(1)
        q, cfq = q_ref[...], cfc_ref[...]
        last = ((i + 1) * tq - 1) // tk

        def step(kb, carry, qpos=None):
            m, l, acc = carry
            sc, _, _, off = _fox_scores(q, k_ref, cfq, cfr_ref, kb, tk, scale, qpos)
            m2 = jnp.maximum(m, jnp.max(sc, axis=1, keepdims=True))
            p = jnp.exp(sc - m2)
            al = jnp.exp(m - m2)
            return (m2, al * l + jnp.sum(p, axis=1, keepdims=True),
                    al * acc + jnp.dot(p.astype(BF16), v_ref[pl.ds(off, tk), :], preferred_element_type=F32))

        init = (jnp.full((tq, 1), NEG, F32), jnp.zeros((tq, 1), F32), jnp.zeros((tq, HEAD), F32))
        m, l, acc = step(last, lax.fori_loop(0, last, step, init), i * tq + _iota((tq, 1), 0))
        o_ref[...] = (acc / l).astype(BF16)
        lse_ref[...] = m + jnp.log(l)

    q, k, v, col, rowv, tile, _ = _fox_specs(s, nh, tq, tk)
    return _pcall(body, (u, u, u, cf_col, cf_row), name=name, grid=(nh, s // tq), in_specs=[q, k, v, col, rowv],
                  out_specs=[tile, col],
                  out_shape=[jax.ShapeDtypeStruct((s, bw), BF16), jax.ShapeDtypeStruct((nh, s, 1), F32)],
                  sem=("parallel", "parallel"), side=side)


def _fox_bwd(name, u, cf_col, cf_row, o, do, lse, bw, side=None):
    s, nh = u.shape[0], bw // HEAD
    tq, tk = _fox_tile(s)
    nq = s // tq
    scale = HEAD ** -0.5

    def body(q_ref, k_ref, v_ref, cfc_ref, cfr_ref, o_ref, do_ref, lse_ref,
             dq_ref, dk_ref, dv_ref, dcc_ref, dcr_ref, dk_s, dv_s):
        i = pl.program_id(1)

        @pl.when(i == 0)
        def _():
            dk_s[...] = jnp.zeros(dk_s.shape, F32)
            dv_s[...] = jnp.zeros(dv_s.shape, F32)
            dcr_ref[...] = jnp.zeros(dcr_ref.shape, F32)

        q, dov, cfq, lse_q = q_ref[...], do_ref[...], cfc_ref[...], lse_ref[...]
        delta = jnp.sum(dov.astype(F32) * o_ref[...].astype(F32), axis=1, keepdims=True)
        last = ((i + 1) * tq - 1) // tk

        def step(kb, carry, qpos=None):
            dq, dcq = carry
            sc, mask, k, off = _fox_scores(q, k_ref, cfq, cfr_ref, kb, tk, scale, qpos)
            p = jnp.exp(sc - lse_q)
            if qpos is not None:
                p = jnp.where(mask, p, 0.0)
            dp = lax.dot_general(dov, v_ref[pl.ds(off, tk), :], NT, preferred_element_type=F32)
            ds = p * (dp - delta)
            dsb = ds.astype(BF16)
            dk_s[pl.ds(off, tk), :] += lax.dot_general(dsb, q, TN, preferred_element_type=F32)
            dv_s[pl.ds(off, tk), :] += lax.dot_general(p.astype(BF16), dov, TN, preferred_element_type=F32)
            dcr_ref[kb] += -jnp.sum(ds, axis=0, keepdims=True)
            return (dq + jnp.dot(dsb, k, preferred_element_type=F32), dcq + jnp.sum(ds, axis=1, keepdims=True))

        init = (jnp.zeros((tq, HEAD), F32), jnp.zeros((tq, 1), F32))
        dq, dcq = step(last, lax.fori_loop(0, last, step, init), i * tq + _iota((tq, 1), 0))
        dq_ref[...] = (dq * scale).astype(BF16)
        dcc_ref[...] = jnp.transpose(jnp.broadcast_to(dcq, (tq, LANE)))[:8, :]

        @pl.when(i == nq - 1)
        def _():
            dk_ref[...] = (dk_s[...] * scale).astype(BF16)
            dv_ref[...] = dv_s[...].astype(BF16)

    q, k, v, col, rowv, tile, full = _fox_specs(s, nh, tq, tk)
    by_query = pl.BlockSpec((None, None, 8, tq), lambda h, i: (h, i, 0, 0))
    return _pcall(
        body, (u, u, u, cf_col, cf_row, o, do, lse), name=name, grid=(nh, nq),
        in_specs=[q, k, v, col, rowv, tile, tile, col], out_specs=[tile, full, full, by_query, rowv],
        out_shape=[jax.ShapeDtypeStruct((s, bw), BF16)] * 3
        + [jax.ShapeDtypeStruct((nh, nq, 8, tq), F32), jax.ShapeDtypeStruct((nh, s // tk, 1, tk), F32)],
        scratch_shapes=[pltpu.VMEM((s, HEAD), F32), pltpu.VMEM((s, HEAD), F32)],
        sem=("arbitrary", "arbitrary"), side=side)


def _suffix_mm(x, ones_below):
    hi = x.astype(BF16)
    lo = (x - hi.astype(F32)).astype(BF16)
    return (jnp.dot(hi, ones_below, preferred_element_type=F32) + jnp.dot(lo, ones_below, preferred_element_type=F32))


def _sb_tile(q, k_ref, kb, tk, qpos, scale):
    off = pl.multiple_of(kb * tk, tk)
    k = k_ref[pl.ds(off, tk), :]
    z = lax.dot_general(q, k, NT, preferred_element_type=F32) * scale
    mask = kb * tk + _iota((1, tk), 1) < qpos
    lsn = -jnp.maximum(z, 0.0) - jnp.log(1.0 + jnp.exp(-jnp.abs(z)))
    return z, mask, lsn, jnp.where(mask, lsn, 0.0), k, off


def _sb_specs(s, nh, tq):
    q = pl.BlockSpec((tq, HEAD), lambda h, i: (i, SQ * nh + h))
    k = pl.BlockSpec((s, HEAD), lambda h, i: (0, SK * nh + h))
    v = pl.BlockSpec((s, HEAD), lambda h, i: (0, SV * nh + h))
    tile = pl.BlockSpec((tq, HEAD), lambda h, i: (i, h))
    full = pl.BlockSpec((s, HEAD), lambda h, i: (0, h))
    return q, k, v, tile, full


def _sb_fwd(name, u, bw):
    s, nh = u.shape[0], bw // HEAD
    tq = tk = 256
    scale = HEAD ** -0.5

    def body(q_ref, k_ref, v_ref, o_ref):
        i = pl.program_id(1)
        q = q_ref[...]
        qpos = i * tq + _iota((tq, 1), 0)
        later_keys = (_iota((tk, tk), 0) > _iota((tk, tk), 1)).astype(BF16)
        nk = (i * tq + tq + tk - 2) // tk

        def cond(st):
            return jnp.logical_and(st[0] < nk, st[3] > SB_DEAD)

        def step(st):
            j, c, acc, _ = st
            z, mask, lsn, lm, _, off = _sb_tile(q, k_ref, nk - 1 - j, tk, qpos, scale)
            a = jnp.where(mask, jnp.exp(lsn + z + c + _suffix_mm(lm, later_keys)), 0.0)
            acc = acc + jnp.dot(a.astype(BF16), v_ref[pl.ds(off, tk), :], preferred_element_type=F32)
            c = c + jnp.sum(lm, axis=1, keepdims=True)
            return j + 1, c, acc, jnp.max(c)

        init = (jnp.int32(0), jnp.zeros((tq, 1), F32), jnp.zeros((tq, HEAD), F32), jnp.float32(0.0))
        o_ref[...] = lax.while_loop(cond, step, init)[2].astype(BF16)

    q, k, v, tile, _ = _sb_specs(s, nh, tq)
    return _pcall(body, (u, u, u), name=name, grid=(nh, s // tq), in_specs=[q, k, v], out_specs=[tile],
                  out_shape=[jax.ShapeDtypeStruct((s, bw), BF16)], sem=("parallel", "parallel"))[0]


def _sb_bwd(name, u, do, bw):
    s, nh = u.shape[0], bw // HEAD
    tq = tk = 256
    nq = s // tq
    scale = HEAD ** -0.5

    def body(q_ref, k_ref, v_ref, do_ref, dq_ref, dk_ref, dv_ref, dk_s, dv_s):
        i = pl.program_id(1)

        @pl.when(i == 0)
        def _():
            dk_s[...] = jnp.zeros(dk_s.shape, F32)
            dv_s[...] = jnp.zeros(dv_s.shape, F32)

        q, dov = q_ref[...], do_ref[...]
        qpos = i * tq + _iota((tq, 1), 0)
        later_keys = (_iota((tk, tk), 0) > _iota((tk, tk), 1)).astype(BF16)
        this_and_later = (_iota((tk, tk), 0) >= _iota((tk, tk), 1)).astype(BF16)
        nk = (i * tq + tq + tk - 2) // tk

        def weights(j, c):
            z, mask, lsn, lm, k, off = _sb_tile(q, k_ref, nk - 1 - j, tk, qpos, scale)
            a = jnp.where(mask, jnp.exp(lsn + z + c + _suffix_mm(lm, later_keys)), 0.0)
            w = a * lax.dot_general(dov, v_ref[pl.ds(off, tk), :], NT, preferred_element_type=F32)
            return z, mask, lsn, lm, k, off, a, w

        def cond(st):
            return jnp.logical_and(st[0] < nk, st[3] > SB_DEAD)

        def step1(st):
            j, c, wc, _ = st
            _, _, _, lm, _, _, _, w = weights(j, c)
            c = c + jnp.sum(lm, axis=1, keepdims=True)
            return j + 1, c, wc + jnp.sum(w, axis=1, keepdims=True), jnp.max(c)

        zero = jnp.zeros((tq, 1), F32)
        live, _, total, _ = lax.while_loop(cond, step1, (jnp.int32(0), zero, zero, jnp.float32(0.0)))

        def step2(j, st):
            c, wc, dq = st
            z, mask, lsn, lm, k, off, a, w = weights(j, c)
            earlier = total - (wc + _suffix_mm(w, this_and_later))
            dz = jnp.where(mask, w * jnp.exp(lsn) - jnp.exp(lsn + z) * earlier, 0.0)
            dzb = dz.astype(BF16)
            dk_s[pl.ds(off, tk), :] += lax.dot_general(dzb, q, TN, preferred_element_type=F32)
            dv_s[pl.ds(off, tk), :] += lax.dot_general(a.astype(BF16), dov, TN, preferred_element_type=F32)
            return (c + jnp.sum(lm, axis=1, keepdims=True), wc + jnp.sum(w, axis=1, keepdims=True),
                    dq + jnp.dot(dzb, k, preferred_element_type=F32))

        dq = lax.fori_loop(0, live, step2, (zero, zero, jnp.zeros((tq, HEAD), F32)))[2]
        dq_ref[...] = (dq * scale).astype(BF16)

        @pl.when(i == nq - 1)
        def _():
            dk_ref[...] = (dk_s[...] * scale).astype(BF16)
            dv_ref[...] = dv_s[...].astype(BF16)

    q, k, v, tile, full = _sb_specs(s, nh, tq)
    return pl.pallas_call(
        body, name=name, grid=(nh, nq), in_specs=[q, k, v, tile], out_specs=[tile, full, full],
        out_shape=[jax.ShapeDtypeStruct((s, bw), BF16)] * 3,
        scratch_shapes=[pltpu.VMEM((s, HEAD), F32), pltpu.VMEM((s, HEAD), F32)],
        compiler_params=_cp(("arbitrary", "arbitrary")))(u, u, u, do)


BIAS_W = -(-(WIN + QBLK - 1) // LANE) * LANE


def _strip_onehot():
    col = _iota((1, BIAS_W), 1)
    ridx = jnp.clip(PADK + (QBLK - 1) - col, -(CHUNK - 1), REL_CLIP) + (CHUNK - 1)
    return (_iota((REL_PAD, BIAS_W), 0) == ridx).astype(BF16)


def _split2(x):
    hi = x.astype(BF16)
    return hi, (x - hi.astype(F32)).astype(BF16)


def _bias_expand(name, table, nh):
    def body(t_ref, o_ref, strip):
        table_f32 = t_ref[...]
        hi = table_f32.astype(BF16)
        mid, lo = _split2(table_f32 - hi.astype(F32))
        onehot = _strip_onehot()
        strip[...] = (jnp.dot(hi, onehot, preferred_element_type=F32) + jnp.dot(mid, onehot, preferred_element_type=F32)
                      + jnp.dot(lo, onehot, preferred_element_type=F32))
        row, kl = _iota((QBLK, 1), 0), _iota((1, WIN), 1)
        first = row - jnp.bitwise_and(row, CHUNK - 1)
        valid = jnp.logical_and(kl >= first, kl < first + BAND)
        for h in range(nh):
            rows = jnp.broadcast_to(strip[pl.ds(h, 1), :], (QBLK, BIAS_W))
            rolled = pltpu.roll(rows, BIAS_W - (QBLK - 1), 1, stride=1, stride_axis=0)
            o_ref[h] = jnp.where(valid, rolled[:, :WIN], NEG)

    return pl.pallas_call(body, name=name, out_shape=jax.ShapeDtypeStruct((nh, QBLK, WIN), F32),
                          in_specs=[pl.BlockSpec(memory_space=pltpu.VMEM)],
                          out_specs=pl.BlockSpec(memory_space=pltpu.VMEM),
                          scratch_shapes=[pltpu.VMEM((16, BIAS_W), F32)], compiler_params=_cp())(table)


def _bias_reduce(name, dss, nh):
    def body(x_ref, o_ref):
        onehot = _strip_onehot()
        flip = (_iota((QBLK, QBLK), 0) + _iota((QBLK, QBLK), 1) == QBLK - 1).astype(BF16)
        for h in range(nh):
            x = jnp.concatenate([x_ref[h], jnp.zeros((QBLK, BIAS_W - WIN), F32)], axis=1)
            hi, lo = _split2(x)
            back = jnp.dot(flip, hi, preferred_element_type=F32) + jnp.dot(flip, lo, preferred_element_type=F32)
            lined = pltpu.roll(back, 0, 1, stride=1, stride_axis=0)
            hi, lo = _split2(jnp.broadcast_to(jnp.sum(lined, axis=0, keepdims=True), (8, BIAS_W)))
            o_ref[h] = (lax.dot_general(hi, onehot, NT, preferred_element_type=F32)
                        + lax.dot_general(lo, onehot, NT, preferred_element_type=F32))

    return pl.pallas_call(body, name=name, out_shape=jax.ShapeDtypeStruct((nh, 8, REL_PAD), F32),
                          in_specs=[pl.BlockSpec(memory_space=pltpu.VMEM)],
                          out_specs=pl.BlockSpec(memory_space=pltpu.VMEM), compiler_params=_cp())(dss)


def _chunk_specs(s, nh):
    q = pl.BlockSpec((QBLK, HEAD), lambda h, i: (i, CQ * nh + h))
    kv = pl.BlockSpec((s + PADK, HEAD), lambda h, i: (0, h))
    bias = pl.BlockSpec((None, QBLK, WIN), lambda h, i: (h, 0, 0))
    tile = pl.BlockSpec((QBLK, HEAD), lambda h, i: (i, h))
    full = pl.BlockSpec((s, HEAD), lambda h, i: (0, h))
    return q, kv, bias, tile, full


def _chunk_probs(q, k_ref, b_ref, i, scale):
    off = pl.multiple_of(i * QBLK, QBLK)
    kw = k_ref[pl.ds(off, WIN), :]
    sc = lax.dot_general(q, kw, NT, preferred_element_type=F32) * scale + b_ref[...]
    sc = jnp.where(i * QBLK + _iota((1, WIN), 1) >= PADK, sc, NEG)
    p = jnp.exp(sc - jnp.max(sc, axis=1, keepdims=True))
    return p, jnp.sum(p, axis=1, keepdims=True), kw, off


def _chunk_fwd(name, u, kpad, vpad, bias, bw, side=None):
    s, nh = u.shape[0], bw // HEAD
    scale = HEAD ** -0.5

    def body(q_ref, k_ref, v_ref, b_ref, o_ref):
        p, l, _, off = _chunk_probs(q_ref[...], k_ref, b_ref, pl.program_id(1), scale)
        o = jnp.dot(p.astype(BF16), v_ref[pl.ds(off, WIN), :], preferred_element_type=F32)
        o_ref[...] = (o / l).astype(BF16)

    q, kv, bs, tile, _ = _chunk_specs(s, nh)
    return _pcall(body, (u, kpad, vpad, bias), name=name, grid=(nh, s // QBLK), in_specs=[q, kv, kv, bs],
                  out_specs=[tile], out_shape=[jax.ShapeDtypeStruct((s, bw), BF16)], sem=("parallel", "parallel"),
                  side=side)[0]


def _chunk_bwd(name, u, kpad, vpad, bias, do, bw):
    s, nh = u.shape[0], bw // HEAD
    nq = s // QBLK
    scale = HEAD ** -0.5

    def body(q_ref, k_ref, v_ref, b_ref, do_ref, dq_ref, dk_ref, dv_ref, dss_ref, dk_s, dv_s):
        i = pl.program_id(1)

        @pl.when(i == 0)
        def _():
            dk_s[...] = jnp.zeros(dk_s.shape, F32)
            dv_s[...] = jnp.zeros(dv_s.shape, F32)
            dss_ref[...] = jnp.zeros(dss_ref.shape, F32)

        q, dov = q_ref[...], do_ref[...]
        p, l, kw, off = _chunk_probs(q, k_ref, b_ref, i, scale)
        p = p / l
        dp = lax.dot_general(dov, v_ref[pl.ds(off, WIN), :], NT, preferred_element_type=F32)
        ds = p * (dp - jnp.sum(p * dp, axis=1, keepdims=True))
        dsb = ds.astype(BF16)
        dq_ref[...] = (jnp.dot(dsb, kw, preferred_element_type=F32) * scale).astype(BF16)
        dk_s[pl.ds(off, WIN), :] += lax.dot_general(dsb, q, TN, preferred_element_type=F32)
        dv_s[pl.ds(off, WIN), :] += lax.dot_general(p.astype(BF16), dov, TN, preferred_element_type=F32)
        dss_ref[...] += ds

        @pl.when(i == nq - 1)
        def _():
            dk_ref[...] = (dk_s[pl.ds(PADK, s), :] * scale).astype(BF16)
            dv_ref[...] = dv_s[pl.ds(PADK, s), :].astype(BF16)

    q, kv, bs, tile, full = _chunk_specs(s, nh)
    return pl.pallas_call(
        body, name=name, grid=(nh, nq), in_specs=[q, kv, kv, bs, tile], out_specs=[tile, full, full, bs],
        out_shape=[jax.ShapeDtypeStruct((s, bw), BF16)] * 3 + [jax.ShapeDtypeStruct((nh, QBLK, WIN), F32)],
        scratch_shapes=[pltpu.VMEM((s + PADK, HEAD), F32), pltpu.VMEM((s + PADK, HEAD), F32)],
        compiler_params=_cp(("arbitrary", "arbitrary")))(u, kpad, vpad, bias, do)


def _gelu_parts(y):
    th = jnp.tanh(GELU_K * (y + GELU_C * y * y * y))
    return 0.5 * y * (1.0 + th), th


def _block_diag(xb16, w_ref, nh, dims):
    return jnp.concatenate(
        [lax.dot_general(xb16[:, n * HEAD:(n + 1) * HEAD], w_ref[n], dims, preferred_element_type=F32)
         for n in range(nh)], axis=1)


def _lru_gates(ext, cw_ref, cb_ref, wr_ref, br_ref, wi_ref, bi_ref, lam_ref, ts, nh):
    shifted = [pltpu.roll(ext, CONV_WIDTH - 1 - j, 0)[8:, :] if j < CONV_WIDTH - 1 else ext[8:, :]
               for j in range(CONV_WIDTH)]
    xc = cb_ref[...]
    for j in range(CONV_WIDTH):
        xc = xc + shifted[j] * cw_ref[pl.ds(j, 1), :]
    xcb = xc.astype(BF16)
    r = _sigmoid(_block_diag(xcb, wr_ref, nh, NN) + br_ref[...])
    gi = _sigmoid(_block_diag(xcb, wi_ref, nh, NN) + bi_ref[...])
    lsl = _log_sigmoid(lam_ref[...])
    la = LRU_C * r * lsl
    a = jnp.exp(la)
    e2 = jnp.exp(2.0 * la)
    mult = jnp.sqrt(-jnp.tanh(la) * (e2 + 1.0))
    return shifted, xc, xcb, r, gi, lsl, a, e2, mult


def _lru_param_specs(bw, nh):
    vec = pl.BlockSpec((1, bw), lambda i: (0, 0))
    conv = pl.BlockSpec((8, bw), lambda i: (0, 0))
    blocks = pl.BlockSpec((nh, HEAD, HEAD), lambda i: (0, 0, 0))
    return [conv, vec, blocks, vec, blocks, vec, vec]


def _lru_fwd(name, u, params, bw):
    s, nh = u.shape[0], bw // HEAD
    ts = min(512, s)

    def body(rx_ref, ry_ref, cw_ref, cb_ref, wr_ref, br_ref, wi_ref, bi_ref, lam_ref, o_ref, h_ref, tail, hcar):
        @pl.when(pl.program_id(0) == 0)
        def _():
            tail[...] = jnp.zeros(tail.shape, F32)
            hcar[...] = jnp.zeros(hcar.shape, F32)

        rx = rx_ref[...].astype(F32)
        ext = jnp.concatenate([tail[...], rx], axis=0)
        tail[...] = rx[ts - 8:, :]
        _, xc, _, _, gi, _, a, _, mult = _lru_gates(ext, cw_ref, cb_ref, wr_ref, br_ref, wi_ref, bi_ref, lam_ref, ts, nh)
        acum, bcum = _scan_affine(a, mult * (gi * xc), False)
        h_ref[...] = bcum + acum * hcar[...]
        hcar[...] = h_ref[pl.ds(ts - 1, 1), :]
        o_ref[...] = (h_ref[...] * _gelu_parts(ry_ref[...].astype(F32))[0]).astype(BF16)

    row = pl.BlockSpec((ts, bw), lambda i: (i, 0))
    return pl.pallas_call(
        body, name=name, grid=(s // ts,),
        in_specs=[pl.BlockSpec((ts, bw), lambda i: (i, RX)), pl.BlockSpec((ts, bw), lambda i: (i, RY))]
        + _lru_param_specs(bw, nh),
        out_specs=[row, row],
        out_shape=[jax.ShapeDtypeStruct((s, bw), BF16), jax.ShapeDtypeStruct((s, bw), F32)],
        scratch_shapes=[pltpu.VMEM((8, bw), F32), pltpu.VMEM((1, bw), F32)],
        compiler_params=_cp(("arbitrary",)))(u, u, *params)


def _lru_bwd(name, u, h, do, params, bw):
    s, nh = u.shape[0], bw // HEAD
    ts = min(512, s)
    nb = s // ts
    t8 = ts // 8

    def body(rx_ref, rxp_ref, ry_ref, h_ref, hp_ref, do_ref, cw_ref, cb_ref, wr_ref, br_ref, wi_ref, bi_ref, lam_ref,
             drx_ref, dry_ref, dcw_ref, dcb_ref, dwr_ref, dbr_ref, dwi_ref, dbi_ref, dlam_ref, gcar, head):
        i = pl.program_id(0)
        first = i == nb - 1

        @pl.when(i == 0)
        def _():
            gcar[...] = jnp.zeros(gcar.shape, F32)
            head[...] = jnp.zeros(head.shape, F32)
            for ref in (dcw_ref, dcb_ref, dwr_ref, dbr_ref, dwi_ref, dbi_ref, dlam_ref):
                ref[...] = jnp.zeros(ref.shape, F32)

        rows = _iota((ts, 1), 0)
        rx = rx_ref[...].astype(F32)
        before = jnp.where(first, 0.0, rxp_ref[...].astype(F32))
        ext = jnp.concatenate([before, rx], axis=0)
        shifted, xc, xcb, r, gi, lsl, a, e2, mult = _lru_gates(
            ext, cw_ref, cb_ref, wr_ref, br_ref, wi_ref, bi_ref, lam_ref, ts, nh)

        ry = ry_ref[...].astype(F32)
        gel, th = _gelu_parts(ry)
        dgel = 0.5 * (1.0 + th) + 0.5 * ry * (1.0 - th * th) * GELU_K * (1.0 + 3.0 * GELU_C * ry * ry)
        dov = do_ref[...].astype(F32)
        hv = h_ref[...]
        dry_ref[...] = (dov * hv * dgel).astype(BF16)

        coef = jnp.where(rows < ts - 1, pltpu.roll(a, ts - 1, 0), 0.0)
        dh_in = dov * gel + jnp.where(rows == ts - 1, gcar[...], 0.0)
        dh = _scan_affine(coef, dh_in, True)[1]
        gcar[...] = jnp.sum(jnp.where(rows == 0, a * dh, 0.0), axis=0, keepdims=True)

        hprev = jnp.where(first, 0.0, hp_ref[...])
        hm1 = pltpu.roll(jnp.concatenate([hprev, hv], axis=0), 1, 0)[8:, :]
        dgx = dh * mult
        dla = dh * hm1 * a - dh * gi * xc * (e2 / mult)
        dpre_r = dla * (LRU_C * lsl) * r * (1.0 - r)
        dpre_i = dgx * xc * gi * (1.0 - gi)
        dlam_ref[...] += jnp.sum(dla * r, axis=0, keepdims=True) * (LRU_C * _sigmoid(-lam_ref[...]))
        dbr_ref[...] += jnp.sum(dpre_r, axis=0, keepdims=True)
        dbi_ref[...] += jnp.sum(dpre_i, axis=0, keepdims=True)
        drb, dib = dpre_r.astype(BF16), dpre_i.astype(BF16)
        for n in range(nh):
            cols = slice(n * HEAD, (n + 1) * HEAD)
            dwr_ref[n] += lax.dot_general(xcb[:, cols], drb[:, cols], TN, preferred_element_type=F32)
            dwi_ref[n] += lax.dot_general(xcb[:, cols], dib[:, cols], TN, preferred_element_type=F32)
        dxc = dgx * gi + _block_diag(drb, wr_ref, nh, NT) + _block_diag(dib, wi_ref, nh, NT)

        dcb_ref[...] += jnp.sum(dxc, axis=0, keepdims=True)
        for j in range(CONV_WIDTH):
            dcw_ref[pl.ds(j, 1), :] += jnp.sum(dxc * shifted[j], axis=0, keepdims=True)
        ext2 = jnp.concatenate([dxc, head[...]], axis=0)
        head[...] = dxc[:8, :]
        drx = dxc * cw_ref[pl.ds(CONV_WIDTH - 1, 1), :]
        for j in range(CONV_WIDTH - 1):
            up = CONV_WIDTH - 1 - j
            drx = drx + pltpu.roll(ext2, ts + 8 - up, 0)[:ts, :] * cw_ref[pl.ds(j, 1), :]
        drx_ref[...] = drx.astype(BF16)

    def blk(col):
        return lambda i: (nb - 1 - i, col)

    def prev8(col):
        return lambda i: (jnp.maximum((nb - 1 - i) * t8 - 1, 0), col)

    vec = pl.BlockSpec((1, bw), lambda i: (0, 0))
    conv = pl.BlockSpec((8, bw), lambda i: (0, 0))
    blocks = pl.BlockSpec((nh, HEAD, HEAD), lambda i: (0, 0, 0))
    return pl.pallas_call(
        body, name=name, grid=(nb,),
        in_specs=[pl.BlockSpec((ts, bw), blk(RX)), pl.BlockSpec((8, bw), prev8(RX)), pl.BlockSpec((ts, bw), blk(RY)),
                  pl.BlockSpec((ts, bw), blk(0)), pl.BlockSpec((8, bw), prev8(0)), pl.BlockSpec((ts, bw), blk(0))]
        + _lru_param_specs(bw, nh),
        out_specs=[pl.BlockSpec((ts, bw), blk(0)), pl.BlockSpec((ts, bw), blk(0)), conv, vec, blocks, vec, blocks, vec, vec],
        out_shape=[jax.ShapeDtypeStruct((s, bw), BF16)] * 2
        + [jax.ShapeDtypeStruct((8, bw), F32), jax.ShapeDtypeStruct((1, bw), F32),
           jax.ShapeDtypeStruct((nh, HEAD, HEAD), F32), jax.ShapeDtypeStruct((1, bw), F32),
           jax.ShapeDtypeStruct((nh, HEAD, HEAD), F32), jax.ShapeDtypeStruct((1, bw), F32),
           jax.ShapeDtypeStruct((1, bw), F32)],
        scratch_shapes=[pltpu.VMEM((1, bw), F32), pltpu.VMEM((8, bw), F32)],
        compiler_params=_cp(("arbitrary",)))(u, u, u, h, h, do, *params)


def _gate_merge(name, xb, w_gate, b_gate, o_all, w_branch, l, side=None):
    s, d = xb.shape
    bw = o_all.shape[2]
    tm, tn = min(1024, s), min(256, d)

    def body(x_ref, wg_ref, bg_ref, o_ref, wb_ref, m_ref, g_ref, p_ref):
        x = x_ref[...]
        acc = jnp.zeros((tm, tn), F32)
        for g in range(4):
            gate = _sigmoid(jnp.dot(x, wg_ref[g], preferred_element_type=F32) + bg_ref[g])
            proj = jnp.dot(o_ref[g], wb_ref[g], preferred_element_type=F32)
            term = gate * proj
            g_ref[g] = gate.astype(BF16)
            p_ref[g] = (term * (1.0 - gate)).astype(BF16)
            acc = acc + term
        m_ref[...] = acc.astype(BF16)

    quad = pl.BlockSpec((4, tm, tn), lambda n, m: (0, m, n))
    return _pcall(
        body, (xb, w_gate, b_gate, o_all, w_branch), name=name, grid=(d // tn, s // tm),
        in_specs=[pl.BlockSpec((tm, d), lambda n, m: (m, 0)),
                  pl.BlockSpec((None, 4, d, tn), lambda n, m: (0, 0, 0, n)),
                  pl.BlockSpec((None, 4, 1, tn), lambda n, m: (l, 0, 0, n)),
                  pl.BlockSpec((4, tm, bw), lambda n, m: (0, m, 0)),
                  pl.BlockSpec((None, 4, bw, tn), lambda n, m: (0, 0, 0, n))],
        out_specs=[pl.BlockSpec((tm, tn), lambda n, m: (m, n)), quad, quad],
        out_shape=[jax.ShapeDtypeStruct((s, d), BF16), jax.ShapeDtypeStruct((4, s, d), BF16),
                   jax.ShapeDtypeStruct((4, s, d), BF16)],
        sem=("parallel", "parallel"), side=side)


def _adamw(name, w, m, v, parts, layer=0, layers=1, earlier=None):
    cols = w.shape[1]
    p, rows = parts.shape[0], parts.shape[1]
    tr = _pow2_rows(rows, cols * max(1, p // 2))
    nb = rows // tr
    c1 = 1.0 - ADAM_B1 ** ADAM_STEP
    c2 = 1.0 - ADAM_B2 ** ADAM_STEP

    def body(w_ref, m_ref, v_ref, g_ref, *rest):
        go_ref, do_ref, mo_ref, vo_ref = rest[-4:]
        g = g_ref[0].astype(F32)
        for k in range(1, p):
            g = g + g_ref[k].astype(F32)
        m2 = ADAM_B1 * m_ref[...] + (1.0 - ADAM_B1) * g
        v2 = ADAM_B2 * v_ref[...] + (1.0 - ADAM_B2) * (g * g)
        go_ref[...] = g
        do_ref[...] = -ADAM_LR * ((m2 / c1) / (jnp.sqrt(v2 / c2) + ADAM_EPS) + ADAM_WD * w_ref[...])
        mo_ref[...] = m2
        vo_ref[...] = v2

    row = pl.BlockSpec((tr, cols), lambda i: (layer * nb + i, 0))
    held = list(earlier) if earlier is not None else []
    return pl.pallas_call(
        body, name=name, grid=(nb,),
        in_specs=[row, row, row, pl.BlockSpec((p, tr, cols), lambda i: (0, i, 0))] + [ANY] * len(held),
        out_specs=[row] * 4, out_shape=[jax.ShapeDtypeStruct((layers * rows, cols), F32)] * 4,
        input_output_aliases={4 + k: k for k in range(len(held))},
        compiler_params=_cp(("parallel",)))(w, m, v, parts, *held)


PACK_ROWS = 512


def _pack(arrays):
    rows = []
    for a in arrays:
        flat = a.astype(F32).reshape(-1)
        rows.append(jnp.pad(flat, (0, (-flat.shape[0]) % LANE)).reshape(-1, LANE))
    rows = jnp.concatenate(rows)
    return jnp.pad(rows, ((0, (-rows.shape[0]) % PACK_ROWS), (0, 0)))


def _unpack(packed, shapes):
    out, row = [], 0
    for shp in shapes:
        n = math.prod(shp)
        nrows = -(-n // LANE)
        out.append(packed[row:row + nrows].reshape(-1)[:n].reshape(shp))
        row += nrows
    return out


def _unshard(gathered, axis):
    block = gathered.shape[2:]
    full = jnp.swapaxes(gathered, 0, 1).reshape((N_DEV,) + block)
    full = jnp.moveaxis(full, 0, axis)
    return full.reshape(block[:axis] + (N_DEV * block[axis],) + block[axis + 1:])


def kernel(x, ln_in_g, ln_in_b, w_in, b_forget, conv_w, conv_b, w_r, b_r, w_i, b_i, lru_lambda, rel_bias, w_branch, w_gate, b_gate, w_out, ln1_g, ln1_b, w_ff1, w_ff2, ln2_g, ln2_b, loss_target, m_ln_in_g, m_ln_in_b, m_w_in, m_b_forget, m_conv_w, m_conv_b, m_w_r, m_b_r, m_w_i, m_b_i, m_lru_lambda, m_rel_bias, m_w_branch, m_w_gate, m_b_gate, m_w_out, m_ln1_g, m_ln1_b, m_w_ff1, m_w_ff2, m_ln2_g, m_ln2_b, v_ln_in_g, v_ln_in_b, v_w_in, v_b_forget, v_conv_w, v_conv_b, v_w_r, v_b_r, v_w_i, v_b_i, v_lru_lambda, v_rel_bias, v_w_branch, v_w_gate, v_b_gate, v_w_out, v_ln1_g, v_ln1_b, v_w_ff1, v_w_ff2, v_ln2_g, v_ln2_b):
    given = dict(zip(
        NAMES + ['loss_target'] + ['m_' + n for n in WEIGHTS] + ['v_' + n for n in WEIGHTS],
        (x, ln_in_g, ln_in_b, w_in, b_forget, conv_w, conv_b, w_r, b_r, w_i, b_i, lru_lambda, rel_bias, w_branch, w_gate, b_gate, w_out, ln1_g, ln1_b, w_ff1, w_ff2, ln2_g, ln2_b, loss_target, m_ln_in_g, m_ln_in_b, m_w_in, m_b_forget, m_conv_w, m_conv_b, m_w_r, m_b_r, m_w_i, m_b_i, m_lru_lambda, m_rel_bias, m_w_branch, m_w_gate, m_b_gate, m_w_out, m_ln1_g, m_ln1_b, m_w_ff1, m_w_ff2, m_ln2_g, m_ln2_b, v_ln_in_g, v_ln_in_b, v_w_in, v_b_forget, v_conv_w, v_conv_b, v_w_r, v_b_r, v_w_i, v_b_i, v_lru_lambda, v_rel_bias, v_w_branch, v_w_gate, v_b_gate, v_w_out, v_ln1_g, v_ln1_b, v_w_ff1, v_w_ff2, v_ln2_g, v_ln2_b)))

    s, d = x.shape[1], x.shape[2]
    nl = w_in.shape[0]
    bw = d // 4
    nh = bw // HEAD
    nu = 11 * bw
    rs = d // N_DEV
    dff = w_ff1.shape[2] * N_DEV
    fs = dff // N_DEV
    cs = d // N_DEV
    assert nl == DEPTH and nh * HEAD == bw and s % 1024 == 0 and d % 1024 == 0

    xi, yi, ci = _position()
    dev = 4 * xi + 2 * yi + ci
    c_arr = jnp.reshape(ci, (1,)).astype(I32)

    w_main = jnp.concatenate(
        [w_in[..., :3 * bw], w_in[..., 3 * bw + nh:],
         jnp.pad(w_in[..., 3 * bw:3 * bw + nh], ((0, 0), (0, 0), (0, LANE - nh)))], axis=-1).astype(BF16)
    nue = nu + LANE
    small_shapes = [conv_w.shape, rel_bias.shape, b_gate.shape]
    shard = {'main': w_main, 'branch': w_branch.astype(BF16), 'gate': w_gate.astype(BF16),
             'out': w_out.astype(BF16), 'ff1': w_ff1.astype(BF16), 'ff2': w_ff2.astype(BF16)}
    shard_axis = {'main': 1, 'branch': 3, 'gate': 2, 'out': 1, 'ff1': 2, 'ff2': 1}
    W = [dict() for _ in range(nl)]

    def gather_chips(l, keys, extra=()):
        side = _chips_side([shard[k][l:l + 1] for k in keys] + list(extra), True)
        side.todo = (l, keys)
        return side

    def gather_cores(chips):
        side = _cores_side(chips.results, True)
        side.todo = chips.todo
        return side

    def arrived(cores):
        l, keys = cores.todo
        for k, res in zip(keys, cores.results):
            W[l][k] = _unshard(res, shard_axis[k])
        return cores.results[len(keys):]

    xs = x[0]
    first = gather_chips(0, ['main'], [_pack([conv_w, rel_bias, b_gate])])
    h0, h0b = _ln_fwd("ln_in", xs, ln_in_g, ln_in_b, side=first)
    first = gather_cores(first)
    _run_side("gather_cores_first", first)
    small = arrived(first)[0]
    small = jnp.swapaxes(small, 0, 1).reshape((N_DEV,) + small.shape[2:])
    small = [_unpack(small[j], small_shapes) for j in range(N_DEV)]
    conv_w_full = jnp.concatenate([small[j][0] for j in range(N_DEV)], axis=-1)
    rel_bias_full = jnp.concatenate([small[j][1] for j in range(N_DEV)], axis=-1)
    b_gate_full = jnp.concatenate([small[j][2] for j in range(N_DEV)], axis=-1)
    b_gate4 = b_gate_full.reshape(nl, 4, 1, d)

    def lru_params(l):
        return (jnp.pad(conv_w_full[l], ((0, 8 - CONV_WIDTH), (0, 0))), conv_b[l].reshape(1, bw),
                w_r[l].astype(BF16), b_r[l].reshape(1, bw), w_i[l].astype(BF16), b_i[l].reshape(1, bw),
                lru_lambda[l].reshape(1, bw))

    def bias_rows(l):
        return jnp.pad(rel_bias_full[l], ((0, 16 - nh), (0, REL_PAD - REL_TABLE)))

    tm = min(1024, s)
    tkk = min(2048, d)

    saved = []
    cur, curb = h0, h0b
    chips = {}
    for l in range(nl):
        side = None
        if l == 0:
            side = chips['b0'] = gather_chips(0, ['gate', 'branch'])
        else:
            side = last_cores = gather_cores(chips.pop('d1'))
        tnw = _lane_tile(nu)
        u = _mm(f"w_in_{l}", curb, W[l]['main'], grid=(s // tm, nu // tnw, d // tkk),
                a_spec=pl.BlockSpec((tm, tkk), lambda m, n, k: (m, k)),
                b_spec=pl.BlockSpec((None, tkk, tnw), lambda m, n, k: (0, k, n)),
                dims=NN, acc_shape=(tm, tnw), out_shape=[jax.ShapeDtypeStruct((s, nu), BF16)],
                out_specs=[pl.BlockSpec((tm, tnw), lambda m, n, k: (m, n))], finish=_store(BF16), side=side)[0]
        if l == 1:
            arrived(last_cores)
        fl = _mm(f"w_forget_{l}", curb, W[l]['main'], grid=(s // tm, 1, d // tkk),
                 a_spec=pl.BlockSpec((tm, tkk), lambda m, n, k: (m, k)),
                 b_spec=pl.BlockSpec((None, tkk, LANE), lambda m, n, k: (0, k, nu // LANE)),
                 dims=NN, acc_shape=(tm, LANE), out_shape=[jax.ShapeDtypeStruct((s, LANE), F32)],
                 out_specs=[pl.BlockSpec((tm, LANE), lambda m, n, k: (m, 0))], finish=_store(F32))[0]
        bf_row = jnp.pad(b_forget[l], (0, LANE - nh)).reshape(1, LANE)
        cf = _cum_forget_fwd(f"cum_forget_{l}", fl, bf_row)
        tkf = _fox_tile(s)[1]
        cf_heads = cf[:, :nh].T
        cf_col = cf_heads.reshape(nh, s, 1)
        cf_row = cf_heads.reshape(nh, s // tkf, 1, tkf)
        side = None
        if l == 0:
            b0 = gather_cores(chips.pop('b0'))
            chips['c0'] = gather_chips(0, ['out', 'ff1', 'ff2'])
            side = _merge_sides(b0, chips['c0'])
        o_fox, lse = _fox_fwd(f"fox_fwd_{l}", u, cf_col, cf_row, bw, side=side)
        if l == 0:
            arrived(b0)
        lp = lru_params(l)
        o_lru, hstate = _lru_fwd(f"lru_fwd_{l}", u, lp, bw)
        o_sb = _sb_fwd(f"sb_fwd_{l}", u, bw)
        bias = _bias_expand(f"bias_expand_{l}", bias_rows(l), nh)
        kpad = jnp.pad(u[:, CK * bw:(CK + 1) * bw], ((PADK, 0), (0, 0)))
        vpad = jnp.pad(u[:, CV * bw:(CV + 1) * bw], ((PADK, 0), (0, 0)))
        side = gather_cores(chips.pop('c0')) if l == 0 else None
        o_ch = _chunk_fwd(f"chunk_fwd_{l}", u, kpad, vpad, bias, bw, side=side)
        if l == 0:
            arrived(side)
        o_all = jnp.stack([o_fox, o_lru, o_sb, o_ch])
        side = None
        if l == 0:
            side = chips['a1'] = gather_chips(1, ['main', 'gate', 'branch'])
        merged, gates, projs = _gate_merge(f"gate_merge_{l}", curb, W[l]['gate'], b_gate4, o_all, W[l]['branch'], l,
                                           side=side)
        side = gather_cores(chips.pop('a1')) if l == 0 else None
        z1, x1, x1b = _mm_ln(f"w_out_ln1_{l}", merged, W[l]['out'], 0, cur, ln1_g[l], ln1_b[l], side=side)
        if l == 0:
            arrived(side)
        tn1 = min(1024, dff)

        def ff1_finish(acc, ex, outs, ids):
            outs[0][...] = acc.astype(BF16)
            r = jnp.maximum(acc, 0.0)
            outs[1][...] = (r * r).astype(BF16)

        side = None
        if l == 0:
            side = chips['c1'] = gather_chips(1, ['out', 'ff1'])
        hp, hid = _mm(f"w_ff1_{l}", x1b, W[l]['ff1'], grid=(s // tm, dff // tn1, d // tkk),
                      a_spec=pl.BlockSpec((tm, tkk), lambda m, n, k: (m, k)),
                      b_spec=pl.BlockSpec((None, tkk, tn1), lambda m, n, k: (0, k, n)),
                      dims=NN, acc_shape=(tm, tn1),
                      out_shape=[jax.ShapeDtypeStruct((s, dff), BF16)] * 2,
                      out_specs=[pl.BlockSpec((tm, tn1), lambda m, n, k: (m, n))] * 2, finish=ff1_finish, side=side)
        side = None
        if l == 0:
            c1 = gather_cores(chips.pop('c1'))
            chips['d1'] = gather_chips(1, ['ff2'])
            side = _merge_sides(c1, chips['d1'])
        z2, x2, x2b = _mm_ln(f"w_ff2_ln2_{l}", hid, W[l]['ff2'], 0, x1, ln2_g[l], ln2_b[l], side=side)
        if l == 0:
            arrived(c1)
        saved.append(dict(xin=cur, xinb=curb, u=u, fl=fl, bf_row=bf_row, cf_col=cf_col, cf_row=cf_row, o_fox=o_fox,
                          lse=lse, lp=lp, hstate=hstate, bias=bias, kpad=kpad, vpad=vpad, o_all=o_all, merged=merged,
                          gates=gates, projs=projs, z1=z1, x1=x1, x1b=x1b, hp=hp, hid=hid, z2=z2))
        cur, curb = x2, x2b

    loss_tile, *last_ln = _loss_ln_bwd("loss_ln2_bwd", cur, loss_target[0], saved[nl - 1]['z2'], ln2_g[nl - 1])
    loss = lax.psum(loss_tile[0, 0], ("x", "y", "c"))
    dcur = None

    big = [dict() for _ in range(nl)]
    reduced = [dict() for _ in range(nl)]
    sm = {n: [None] * nl for n in ['b_forget', 'conv_w', 'conv_b', 'w_r', 'b_r', 'w_i', 'b_i', 'lru_lambda', 'rel_bias',
                                   'b_gate', 'ln1_g', 'ln1_b', 'ln2_g', 'ln2_b']}

    def split_columns(acc, ex, outs, ids):
        for j in range(N_DEV):
            outs[0][j] = acc[:, j * cs:(j + 1) * cs].astype(BF16)

    def grad_mm(key, name, a, b, *, shape, grid, a_spec, b_spec, out_spec, acc_shape, finish=_store(BF16)):
        big[l][key] = _mm(name, a, b, grid=grid, a_spec=a_spec, b_spec=b_spec, dims=TN, acc_shape=acc_shape,
                          out_shape=[jax.ShapeDtypeStruct(shape, BF16)], out_specs=[out_spec], finish=finish)[0]

    def reduce_cores(l, keys):
        side = _cores_side([big[l][k].reshape((4, 2) + big[l][k].shape[1:]) for k in keys], False)
        side.todo = (l, keys)
        return side

    def reduce_chips(cores):
        l, keys = cores.todo
        partial = []
        for k, mine, other in zip(keys, cores.operands, cores.results):
            cols = mine.shape[-1]
            rows = math.prod(mine.shape[2:]) // cols
            partial.append(_add_core_halves(f"add_cores_{k}_{l}", mine.reshape(4, 2, rows, cols),
                                            other.reshape(4, rows, cols), c_arr, BF16))
        side = _chips_side(partial, False)
        side.todo = (l, keys)
        return side

    def reduction_done(chips_side):
        l, keys = chips_side.todo
        reduced[l].update(zip(keys, chips_side.results))

    GROUP1, GROUP2 = ['w_ff2', 'w_ff1', 'w_out', 'w_branch', 'w_gate'], ['w_main']
    pending = None

    tks = min(2048, s)
    tmr = min(1024, d)
    nsh = tmr // rs

    for l in reversed(range(nl)):
        sv = saved[l]
        Wl = W[l]
        dz2, dz2b, dg, db = last_ln if l == nl - 1 else _ln_bwd(f"ln2_bwd_{l}", dcur, sv['z2'], ln2_g[l])
        sm['ln2_g'][l], sm['ln2_b'][l] = dg[0], db[0]
        tn1 = min(1024, dff)

        def dhp_finish(acc, ex, outs, ids):
            outs[0][...] = (acc * (2.0 * jnp.maximum(ex[0][...].astype(F32), 0.0))).astype(BF16)

        dhp = _mm(f"d_hidden_{l}", dz2b, Wl['ff2'], grid=(s // tm, dff // tn1, d // tkk),
                  a_spec=pl.BlockSpec((tm, tkk), lambda m, n, k: (m, k)),
                  b_spec=pl.BlockSpec((None, tn1, tkk), lambda m, n, k: (0, n, k)),
                  dims=NT, acc_shape=(tm, tn1), out_shape=[jax.ShapeDtypeStruct((s, dff), BF16)],
                  out_specs=[pl.BlockSpec((tm, tn1), lambda m, n, k: (m, n))], finish=dhp_finish,
                  extras=(sv['hp'],), extra_specs=(pl.BlockSpec((tm, tn1), lambda m, n, k: (m, n)),),
                  side=pending)[0]
        if pending is not None:
            reduction_done(pending)
            pending = None
        grad_mm('w_ff2', f"g_w_ff2_{l}", sv['hid'], dz2b, shape=(N_DEV, 1, fs, d), grid=(N_DEV, 1, s // tks),
                a_spec=pl.BlockSpec((tks, fs), lambda m, n, k: (k, m)),
                b_spec=pl.BlockSpec((tks, d), lambda m, n, k: (k, 0)),
                out_spec=pl.BlockSpec((None, None, fs, d), lambda m, n, k: (m, 0, 0, 0)), acc_shape=(fs, d))
        grad_mm('w_ff1', f"g_w_ff1_{l}", sv['x1b'], dhp, shape=(N_DEV, 1, d, fs), grid=(d // tmr, N_DEV, s // tks),
                a_spec=pl.BlockSpec((tks, tmr), lambda m, n, k: (k, m)),
                b_spec=pl.BlockSpec((tks, fs), lambda m, n, k: (k, n)),
                out_spec=pl.BlockSpec((None, None, tmr, fs), lambda m, n, k: (n, 0, m, 0)), acc_shape=(tmr, fs))
        tnd = min(1024, d)

        def resid_finish(scale):
            def finish(acc, ex, outs, ids):
                outs[0][...] = acc + scale * ex[0][...]
            return finish

        tile_md = pl.BlockSpec((tm, tnd), lambda m, n, k: (m, n))
        dx1 = _mm(f"d_x1_{l}", dhp, Wl['ff1'], grid=(s // tm, d // tnd, dff // tkk),
                  a_spec=pl.BlockSpec((tm, tkk), lambda m, n, k: (m, k)),
                  b_spec=pl.BlockSpec((None, tnd, tkk), lambda m, n, k: (0, n, k)),
                  dims=NT, acc_shape=(tm, tnd), out_shape=[jax.ShapeDtypeStruct((s, d), F32)],
                  out_specs=[tile_md], finish=resid_finish(ALPHA), extras=(dz2,), extra_specs=(tile_md,))[0]

        dz1, dz1b, dg, db = _ln_bwd(f"ln1_bwd_{l}", dx1, sv['z1'], ln1_g[l])
        sm['ln1_g'][l], sm['ln1_b'][l] = dg[0], db[0]
        tmg, tng = min(512, s), min(512, d)

        def gate_finish(acc, ex, outs, ids):
            @pl.when(ids[1] == 0)
            def _():
                outs[2][...] = jnp.zeros(outs[2].shape, F32)

            ones = jnp.ones((8, tmg), BF16)
            accb = acc.astype(BF16)
            for g in range(4):
                dpre = accb * ex[1][g]
                outs[0][g] = accb * ex[0][g]
                outs[1][g] = dpre
                outs[2][g] += jnp.dot(ones, dpre, preferred_element_type=F32)

        quad = pl.BlockSpec((4, tmg, tng), lambda n, m, k: (0, m, n))
        dproj, dpre, dbg = _mm(
            f"d_merged_{l}", dz1b, Wl['out'], grid=(d // tng, s // tmg, d // tkk),
            a_spec=pl.BlockSpec((tmg, tkk), lambda n, m, k: (m, k)),
            b_spec=pl.BlockSpec((None, tng, tkk), lambda n, m, k: (0, n, k)),
            dims=NT, acc_shape=(tmg, tng),
            out_shape=[jax.ShapeDtypeStruct((4, s, d), BF16), jax.ShapeDtypeStruct((4, s, d), BF16),
                       jax.ShapeDtypeStruct((4, 8, d), F32)],
            out_specs=[quad, quad, pl.BlockSpec((4, 8, tng), lambda n, m, k: (0, 0, n))], finish=gate_finish,
            extras=(sv['gates'], sv['projs']), extra_specs=(quad, quad), sem=("arbitrary", "arbitrary", "arbitrary"))
        sm['b_gate'][l] = dbg[:, 0, :]
        grad_mm('w_out', f"g_w_out_{l}", sv['merged'], dz1b, shape=(N_DEV, 1, rs, d), grid=(d // tmr, 1, s // tks),
                a_spec=pl.BlockSpec((tks, tmr), lambda m, n, k: (k, m)),
                b_spec=pl.BlockSpec((tks, d), lambda m, n, k: (k, 0)),
                out_spec=pl.BlockSpec((nsh, None, rs, d), lambda m, n, k: (m, 0, 0, 0)), acc_shape=(tmr, d))

        nm = s // tm
        do_all = _mm(f"d_branch_{l}", dproj, Wl['branch'], grid=(4 * nm, 1, d // tkk),
                     a_spec=pl.BlockSpec((None, tm, tkk), lambda m, n, k: (m // nm, m % nm, k)),
                     b_spec=pl.BlockSpec((None, None, bw, tkk), lambda m, n, k: (0, m // nm, 0, k)),
                     dims=NT, acc_shape=(tm, bw), out_shape=[jax.ShapeDtypeStruct((4, s, bw), BF16)],
                     out_specs=[pl.BlockSpec((None, tm, bw), lambda m, n, k: (m // nm, m % nm, 0))],
                     finish=_store(BF16))[0]
        grad_mm('w_branch', f"g_w_branch_{l}", sv['o_all'], dproj, shape=(N_DEV, 1, 4, bw, cs),
                grid=(4, 1, s // tks), finish=split_columns,
                a_spec=pl.BlockSpec((None, tks, bw), lambda m, n, k: (m, k, 0)),
                b_spec=pl.BlockSpec((None, tks, d), lambda m, n, k: (m, k, 0)),
                out_spec=pl.BlockSpec((N_DEV, None, None, bw, cs), lambda m, n, k: (0, 0, m, 0, 0)),
                acc_shape=(bw, d))
        grad_mm('w_gate', f"g_w_gate_{l}", sv['xinb'], dpre, shape=(N_DEV, 1, 4, rs, d), grid=(d // tmr, 4, s // tks),
                a_spec=pl.BlockSpec((tks, tmr), lambda m, n, k: (k, m)),
                b_spec=pl.BlockSpec((None, tks, d), lambda m, n, k: (n, k, 0)),
                out_spec=pl.BlockSpec((nsh, None, None, rs, d), lambda m, n, k: (m, 0, n, 0, 0)),
                acc_shape=(tmr, d))
        nkg = d // tkk
        cores1 = reduce_cores(l, GROUP1)
        dx_gate = _mm(f"d_x_gates_{l}", dpre, Wl['gate'], grid=(s // tm, d // tnd, 4 * nkg),
                      a_spec=pl.BlockSpec((None, tm, tkk), lambda m, n, k: (k // nkg, m, k % nkg)),
                      b_spec=pl.BlockSpec((None, None, tnd, tkk), lambda m, n, k: (0, k // nkg, n, k % nkg)),
                      dims=NT, acc_shape=(tm, tnd), out_shape=[jax.ShapeDtypeStruct((s, d), F32)],
                      out_specs=[tile_md], finish=resid_finish(ALPHA), extras=(dz1,), extra_specs=(tile_md,),
                      side=cores1)[0]

        u = sv['u']
        chips1 = reduce_chips(cores1)
        dfq, dfk, dfv, dcc, dcr = _fox_bwd(f"fox_bwd_{l}", u, sv['cf_col'], sv['cf_row'], sv['o_fox'], do_all[0],
                                          sv['lse'], bw, side=chips1)
        reduction_done(chips1)
        dcf = (dcc[:, :, 0, :].reshape(nh, s) + dcr.reshape(nh, s)).T
        dflb, dbf = _cum_forget_bwd(f"cum_forget_bwd_{l}", jnp.pad(dcf, ((0, 0), (0, LANE - nh))), sv['fl'],
                                    sv['bf_row'])
        sm['b_forget'][l] = dbf[0, :nh]
        drx, dry, dcw, dcb, dwr, dbr, dwi, dbi, dlam = _lru_bwd(f"lru_bwd_{l}", u, sv['hstate'], do_all[1], sv['lp'], bw)
        sm['conv_w'][l], sm['conv_b'][l], sm['w_r'][l], sm['b_r'][l] = dcw[:CONV_WIDTH], dcb[0], dwr, dbr[0]
        sm['w_i'][l], sm['b_i'][l], sm['lru_lambda'][l] = dwi, dbi[0], dlam[0]
        dsq, dsk, dsv = _sb_bwd(f"sb_bwd_{l}", u, do_all[2], bw)
        dcq, dck, dcv, dss = _chunk_bwd(f"chunk_bwd_{l}", u, sv['kpad'], sv['vpad'], sv['bias'], do_all[3], bw)
        sm['rel_bias'][l] = _bias_reduce(f"bias_reduce_{l}", dss, nh)[:, 0, :REL_TABLE]
        du = jnp.concatenate([dfq, dfk, dfv, drx, dry, dsq, dsk, dsv, dcq, dck, dcv, dflb], axis=1)

        tnu = _lane_tile(nue)
        grad_mm('w_main', f"g_w_in_{l}", sv['xinb'], du, shape=(N_DEV, 1, rs, nue),
                grid=(d // tmr, nue // tnu, s // tks),
                a_spec=pl.BlockSpec((tks, tmr), lambda m, n, k: (k, m)),
                b_spec=pl.BlockSpec((tks, tnu), lambda m, n, k: (k, n)),
                out_spec=pl.BlockSpec((nsh, None, rs, tnu), lambda m, n, k: (m, 0, 0, n)), acc_shape=(tmr, tnu))
        cores2 = reduce_cores(l, GROUP2)
        if l == 0:
            _run_side("reduce_cores_last", cores2)
            cores2 = reduce_chips(cores2)
        dcur = _mm(f"d_x_in_{l}", du, Wl['main'], grid=(s // tm, d // tnd, nue // tnu),
                   a_spec=pl.BlockSpec((tm, tnu), lambda m, n, k: (m, k)),
                   b_spec=pl.BlockSpec((None, tnd, tnu), lambda m, n, k: (0, n, k)),
                   dims=NT, acc_shape=(tm, tnd), out_shape=[jax.ShapeDtypeStruct((s, d), F32)],
                   out_specs=[tile_md], finish=resid_finish(1.0), extras=(dx_gate,), extra_specs=(tile_md,),
                   side=cores2)[0]
        if l == 0:
            reduction_done(cores2)
        else:
            pending = reduce_chips(cores2)

    grad_x, _, dg_in, db_in = _ln_bwd("ln_in_bwd", dcur, xs, ln_in_g)

    small_names = ['ln_in_g', 'ln_in_b', 'b_forget', 'conv_w', 'conv_b', 'w_r', 'b_r', 'w_i', 'b_i', 'lru_lambda',
                   'rel_bias', 'b_gate', 'ln1_g', 'ln1_b', 'ln2_g', 'ln2_b']
    local = {'ln_in_g': dg_in[0], 'ln_in_b': db_in[0]}
    for n in small_names[2:]:
        local[n] = jnp.stack(sm[n])
    full_shapes = [local[n].shape for n in small_names]
    packed = _pack([local[n] for n in small_names])
    every = _run_side("gather_small_cores",
                      _cores_side(_run_side("gather_small_chips", _chips_side([packed], True)), True))[0]
    every = jnp.swapaxes(every, 0, 1).reshape((N_DEV,) + packed.shape)
    total = dict(zip(small_names, _unpack(_sum_parts("sum_small", every), full_shapes)))
    for n, width in (('conv_w', bw // N_DEV), ('rel_bias', REL_TABLE // N_DEV), ('b_gate', cs)):
        total[n] = lax.dynamic_slice_in_dim(total[n], dev * width, width, axis=2)

    out = {}

    def update(n, parts):
        cols = parts[0].shape[2]
        w2, m2, v2 = (given[p + n].reshape(-1, cols) for p in ('', 'm_', 'v_'))
        res = None
        for l in range(nl):
            res = _adamw(f"adamw_{n}_{l}", w2, m2, v2, parts[l], layer=l, layers=nl, earlier=res)
        out[n] = [r.reshape(given[n].shape) for r in res]

    def parts_w_in(l):
        pm = reduced[l]['w_main']
        return jnp.concatenate([pm[..., :3 * bw], pm[..., nu:nu + nh], pm[..., 3 * bw:nu]], axis=-1)

    update('w_in', [parts_w_in(l) for l in range(nl)])
    for n in ('w_branch', 'w_gate', 'w_out', 'w_ff1', 'w_ff2'):
        update(n, [reduced[l][n] for l in range(nl)])

    small_shapes2 = [given[n].shape for n in small_names]
    res = _adamw("adamw_small", _pack([given[n] for n in small_names]), _pack([given['m_' + n] for n in small_names]),
                 _pack([given['v_' + n] for n in small_names]), _pack([total[n] for n in small_names])[None])
    res = [_unpack(r, small_shapes2) for r in res]
    for j, n in enumerate(small_names):
        out[n] = [res[k][j] for k in range(4)]

    return (loss, grad_x[None], *[out[n][0] for n in WEIGHTS], *[out[n][1] for n in WEIGHTS],
            *[out[n][2] for n in WEIGHTS], *[out[n][3] for n in WEIGHTS])
```
